```python
import math
import jax, jax.numpy as jnp
from jax import lax
import numpy as np

D_MODEL = 1024
BATCH = 8
SEQ = 8192
DEPTH = 2

D_SSM = 384
SSM_GROUP = 16
N_SSM_GROUPS = D_SSM // SSM_GROUP
SSM_STATE = 64
POOL_WINDOWS = (2, 4, 8, 16)
N_POOL_GROUPS = len(POOL_WINDOWS)
POOL_GROUP = 64
D_POOL = N_POOL_GROUPS * POOL_GROUP
MAX_WINDOW = max(POOL_WINDOWS)
SGU_HEADS = 6
SGU_HEAD_DIM = 64
D_SGU = SGU_HEADS * SGU_HEAD_DIM
CHUNK = 128
D_MIX = D_SSM + D_POOL + D_SGU
D_IN = D_SSM + D_POOL + 2 * D_SGU
D_FF = ((8 * D_MODEL // 3 + 255) // 256) * 256
EPS = 1e-6

kernel_name = "hybrid_s5_pool_sgu_trunk"


def rms_norm(x, g):
    xf = x.astype(jnp.float32)
    y = xf * lax.rsqrt(jnp.mean(xf * xf, axis=-1, keepdims=True) + EPS)
    return (y * g.astype(jnp.float32)).astype(x.dtype)


def s5_mixer(u, A_re, A_im, log_dt, B_re, B_im, C_re, C_im, D_skip, w_glu, b_glu):
    f32 = jnp.float32
    bsz, seq, _ = u.shape
    uf = u.astype(f32).reshape(bsz, seq, N_SSM_GROUPS, SSM_GROUP)
    A_re = A_re.astype(f32); A_im = A_im.astype(f32)
    dt = jnp.exp(log_dt.astype(f32))[:, None]
    mag = jnp.exp(A_re * dt)
    ar = mag * jnp.cos(A_im * dt)
    ai = mag * jnp.sin(A_im * dt)
    den = A_re * A_re + A_im * A_im
    f_re = ((ar - 1.0) * A_re + ai * A_im) / den
    f_im = (ai * A_re - (ar - 1.0) * A_im) / den
    B_re = B_re.astype(f32); B_im = B_im.astype(f32)
    Bb_re = f_re[..., None] * B_re - f_im[..., None] * B_im
    Bb_im = f_re[..., None] * B_im + f_im[..., None] * B_re
    bu_re = jnp.einsum('bsgc,gnc->bsgn', uf, Bb_re)
    bu_im = jnp.einsum('bsgc,gnc->bsgn', uf, Bb_im)
    a_re = jnp.broadcast_to(ar, bu_re.shape)
    a_im = jnp.broadcast_to(ai, bu_re.shape)

    def combine(left, right):
        a1r, a1i, b1r, b1i = left
        a2r, a2i, b2r, b2i = right
        return (a1r * a2r - a1i * a2i,
                a1r * a2i + a1i * a2r,
                a2r * b1r - a2i * b1i + b2r,
                a2r * b1i + a2i * b1r + b2i)

    _, _, h_re, h_im = lax.associative_scan(combine, (a_re, a_im, bu_re, bu_im), axis=1)
    y = (jnp.einsum('bsgn,gcn->bsgc', h_re, C_re.astype(f32))
         - jnp.einsum('bsgn,gcn->bsgc', h_im, C_im.astype(f32)))
    y = y.reshape(bsz, seq, D_SSM) + D_skip.astype(f32) * uf.reshape(bsz, seq, D_SSM)
    g = jax.nn.gelu(y)
    out = g * jax.nn.sigmoid(g @ w_glu.astype(f32) + b_glu.astype(f32))
    return out.astype(u.dtype)


def pool_mixer(u, w_pool, pool_scale):
    f32 = jnp.float32
    bsz, seq, _ = u.shape
    uf = u.astype(f32).reshape(bsz, seq, N_POOL_GROUPS, POOL_GROUP)
    csum = jnp.cumsum(uf, axis=1)
    cpad = jnp.pad(csum, ((0, 0), (MAX_WINDOW, 0), (0, 0), (0, 0)))
    pos = jnp.arange(1, seq + 1)
    means = []
    for g, w in enumerate(POOL_WINDOWS):
        lagged = cpad[:, MAX_WINDOW - w:MAX_WINDOW - w + seq, g]
        count = jnp.minimum(pos, w).astype(f32)[None, :, None]
        means.append((csum[:, :, g] - lagged) / count)
    pooled = jnp.stack(means, axis=2) - uf
    mixed = jnp.einsum('bsgc,gcd->bsgd', pooled, w_pool.astype(f32))
    out = mixed.reshape(bsz, seq, D_POOL) * pool_scale.astype(f32)
    return out.astype(u.dtype)


def sgu_mixer(zu, zv, ln_g, ln_b, w_spatial, b_spatial):
    f32 = jnp.float32
    bsz, seq, _ = zu.shape
    n_chunks = seq // CHUNK
    u = jax.nn.gelu(zu.astype(f32))
    v = jax.nn.gelu(zv.astype(f32))
    mu = jnp.mean(v, axis=-1, keepdims=True)
    var = jnp.mean(jnp.square(v - mu), axis=-1, keepdims=True)
    v = (v - mu) * lax.rsqrt(var + EPS) * ln_g.astype(f32) + ln_b.astype(f32)
    vh = v.reshape(bsz, n_chunks, CHUNK, SGU_HEADS, SGU_HEAD_DIM)
    mask = jnp.tril(jnp.ones((CHUNK, CHUNK), dtype=bool))
    ws = jnp.where(mask[None], w_spatial.astype(f32), 0.0)
    mixed = jnp.einsum('hts,bnshd->bnthd', ws, vh)
    mixed = mixed + jnp.transpose(b_spatial.astype(f32))[None, None, :, :, None]
    out = u * mixed.reshape(bsz, seq, D_SGU)
    return out.astype(zu.dtype)


def _fwd_setup_inputs(seed: int = 0) -> dict:
    key = jax.random.key(seed)
    ks = jax.random.split(key, 24)
    f32 = jnp.float32
    nrm = lambda k, shape, s: (jax.random.normal(k, shape, f32) * s)
    x = jax.random.normal(ks[0], (BATCH, SEQ, D_MODEL), f32)
    g_mix = 1.0 + nrm(ks[1], (DEPTH, D_MODEL), 0.02)
    w_in = nrm(ks[2], (DEPTH, D_MODEL, D_IN), D_MODEL ** -0.5)
    A_re = -0.5 + nrm(ks[3], (DEPTH, N_SSM_GROUPS, SSM_STATE), 0.01)
    A_im = (jnp.pi * jnp.arange(SSM_STATE, dtype=f32))[None, None, :] + nrm(ks[4], (DEPTH, N_SSM_GROUPS, SSM_STATE), 0.01)
    log_dt = jax.random.uniform(ks[5], (DEPTH, N_SSM_GROUPS), f32, math.log(1e-3), math.log(1e-1))
    B_re = nrm(ks[6], (DEPTH, N_SSM_GROUPS, SSM_STATE, SSM_GROUP), (2 * SSM_GROUP) ** -0.5)
    B_im = nrm(ks[7], (DEPTH, N_SSM_GROUPS, SSM_STATE, SSM_GROUP), (2 * SSM_GROUP) ** -0.5)
    C_re = nrm(ks[8], (DEPTH, N_SSM_GROUPS, SSM_GROUP, SSM_STATE), (2 * SSM_STATE) ** -0.5)
    C_im = nrm(ks[9], (DEPTH, N_SSM_GROUPS, SSM_GROUP, SSM_STATE), (2 * SSM_STATE) ** -0.5)
    D_skip = nrm(ks[10], (DEPTH, D_SSM), 1.0)
    w_glu = nrm(ks[11], (DEPTH, D_SSM, D_SSM), D_SSM ** -0.5)
    b_glu = nrm(ks[12], (DEPTH, D_SSM), 0.01)
    w_pool = nrm(ks[13], (DEPTH, N_POOL_GROUPS, POOL_GROUP, POOL_GROUP), POOL_GROUP ** -0.5)
    pool_scale = 1.0 + nrm(ks[14], (DEPTH, D_POOL), 0.02)
    sgu_ln_g = 1.0 + nrm(ks[15], (DEPTH, D_SGU), 0.02)
    sgu_ln_b = nrm(ks[16], (DEPTH, D_SGU), 0.01)
    w_spatial = nrm(ks[17], (DEPTH, SGU_HEADS, CHUNK, CHUNK), CHUNK ** -0.5)
    b_spatial = 1.0 + nrm(ks[18], (DEPTH, SGU_HEADS, CHUNK), 0.02)
    w_out = nrm(ks[19], (DEPTH, D_MIX, D_MODEL), D_MIX ** -0.5)
    g_ffn = 1.0 + nrm(ks[20], (DEPTH, D_MODEL), 0.02)
    kf = jax.random.split(ks[21], 3)
    w_gate = nrm(kf[0], (DEPTH, D_MODEL, D_FF), D_MODEL ** -0.5)
    w_up = nrm(kf[1], (DEPTH, D_MODEL, D_FF), D_MODEL ** -0.5)
    w_down = nrm(kf[2], (DEPTH, D_FF, D_MODEL), D_FF ** -0.5)
    g_final = 1.0 + nrm(ks[22], (D_MODEL,), 0.02)
    return {"x": x, "g_mix": g_mix, "w_in": w_in, "A_re": A_re, "A_im": A_im,
            "log_dt": log_dt, "B_re": B_re, "B_im": B_im, "C_re": C_re, "C_im": C_im,
            "D_skip": D_skip, "w_glu": w_glu, "b_glu": b_glu, "w_pool": w_pool,
            "pool_scale": pool_scale, "sgu_ln_g": sgu_ln_g, "sgu_ln_b": sgu_ln_b,
            "w_spatial": w_spatial, "b_spatial": b_spatial, "w_out": w_out,
            "g_ffn": g_ffn, "w_gate": w_gate, "w_up": w_up, "w_down": w_down,
            "g_final": g_final}


def _fwd_reference(x, g_mix, w_in, A_re, A_im, log_dt, B_re, B_im, C_re, C_im, D_skip,
              w_glu, b_glu, w_pool, pool_scale, sgu_ln_g, sgu_ln_b, w_spatial, b_spatial,
              w_out, g_ffn, w_gate, w_up, w_down, g_final):
    split_points = (D_SSM, D_SSM + D_POOL, D_SSM + D_POOL + D_SGU)
    for l in range(DEPTH):
        h = rms_norm(x, g_mix[l])
        z = h @ w_in[l]
        z_a, z_b, z_u, z_v = jnp.split(z, split_points, axis=-1)
        y_a = s5_mixer(z_a, A_re[l], A_im[l], log_dt[l], B_re[l], B_im[l], C_re[l], C_im[l],
                       D_skip[l], w_glu[l], b_glu[l])
        y_b = pool_mixer(z_b, w_pool[l], pool_scale[l])
        y_c = sgu_mixer(z_u, z_v, sgu_ln_g[l], sgu_ln_b[l], w_spatial[l], b_spatial[l])
        y = jnp.concatenate([y_a, y_b, y_c], axis=-1) @ w_out[l]
        x = x + y
        h = rms_norm(x, g_ffn[l])
        x = x + (jax.nn.silu(h @ w_gate[l]) * (h @ w_up[l])) @ w_down[l]
    return rms_norm(x, g_final)


import jax as _jax
import jax.numpy as _jnp

TWIN_FORMAT = 'train_step'
FWD_PARAMS = ['x', 'g_mix', 'w_in', 'A_re', 'A_im', 'log_dt', 'B_re', 'B_im', 'C_re', 'C_im', 'D_skip', 'w_glu', 'b_glu', 'w_pool', 'pool_scale', 'sgu_ln_g', 'sgu_ln_b', 'w_spatial', 'b_spatial', 'w_out', 'g_ffn', 'w_gate', 'w_up', 'w_down', 'g_final']
TWIN_WEIGHTS = ['g_mix', 'w_in', 'A_re', 'A_im', 'log_dt', 'B_re', 'B_im', 'C_re', 'C_im', 'D_skip', 'w_glu', 'b_glu', 'w_pool', 'pool_scale', 'sgu_ln_g', 'sgu_ln_b', 'w_spatial', 'b_spatial', 'w_out', 'g_ffn', 'w_gate', 'w_up', 'w_down', 'g_final']
TWIN_DIFF_INPUT = 'x'
TWIN_INPUTS = ['x', 'g_mix', 'w_in', 'A_re', 'A_im', 'log_dt', 'B_re', 'B_im', 'C_re', 'C_im', 'D_skip', 'w_glu', 'b_glu', 'w_pool', 'pool_scale', 'sgu_ln_g', 'sgu_ln_b', 'w_spatial', 'b_spatial', 'w_out', 'g_ffn', 'w_gate', 'w_up', 'w_down', 'g_final', 'loss_target', 'm_g_mix', 'm_w_in', 'm_A_re', 'm_A_im', 'm_log_dt', 'm_B_re', 'm_B_im', 'm_C_re', 'm_C_im', 'm_D_skip', 'm_w_glu', 'm_b_glu', 'm_w_pool', 'm_pool_scale', 'm_sgu_ln_g', 'm_sgu_ln_b', 'm_w_spatial', 'm_b_spatial', 'm_w_out', 'm_g_ffn', 'm_w_gate', 'm_w_up', 'm_w_down', 'm_g_final', 'v_g_mix', 'v_w_in', 'v_A_re', 'v_A_im', 'v_log_dt', 'v_B_re', 'v_B_im', 'v_C_re', 'v_C_im', 'v_D_skip', 'v_w_glu', 'v_b_glu', 'v_w_pool', 'v_pool_scale', 'v_sgu_ln_g', 'v_sgu_ln_b', 'v_w_spatial', 'v_b_spatial', 'v_w_out', 'v_g_ffn', 'v_w_gate', 'v_w_up', 'v_w_down', 'v_g_final']
TWIN_OUTPUTS = ['loss', 'grad_x', 'grad_g_mix', 'grad_w_in', 'grad_A_re', 'grad_A_im', 'grad_log_dt', 'grad_B_re', 'grad_B_im', 'grad_C_re', 'grad_C_im', 'grad_D_skip', 'grad_w_glu', 'grad_b_glu', 'grad_w_pool', 'grad_pool_scale', 'grad_sgu_ln_g', 'grad_sgu_ln_b', 'grad_w_spatial', 'grad_b_spatial', 'grad_w_out', 'grad_g_ffn', 'grad_w_gate', 'grad_w_up', 'grad_w_down', 'grad_g_final', 'delta_g_mix', 'delta_w_in', 'delta_A_re', 'delta_A_im', 'delta_log_dt', 'delta_B_re', 'delta_B_im', 'delta_C_re', 'delta_C_im', 'delta_D_skip', 'delta_w_glu', 'delta_b_glu', 'delta_w_pool', 'delta_pool_scale', 'delta_sgu_ln_g', 'delta_sgu_ln_b', 'delta_w_spatial', 'delta_b_spatial', 'delta_w_out', 'delta_g_ffn', 'delta_w_gate', 'delta_w_up', 'delta_w_down', 'delta_g_final', 'new_m_g_mix', 'new_m_w_in', 'new_m_A_re', 'new_m_A_im', 'new_m_log_dt', 'new_m_B_re', 'new_m_B_im', 'new_m_C_re', 'new_m_C_im', 'new_m_D_skip', 'new_m_w_glu', 'new_m_b_glu', 'new_m_w_pool', 'new_m_pool_scale', 'new_m_sgu_ln_g', 'new_m_sgu_ln_b', 'new_m_w_spatial', 'new_m_b_spatial', 'new_m_w_out', 'new_m_g_ffn', 'new_m_w_gate', 'new_m_w_up', 'new_m_w_down', 'new_m_g_final', 'new_v_g_mix', 'new_v_w_in', 'new_v_A_re', 'new_v_A_im', 'new_v_log_dt', 'new_v_B_re', 'new_v_B_im', 'new_v_C_re', 'new_v_C_im', 'new_v_D_skip', 'new_v_w_glu', 'new_v_b_glu', 'new_v_w_pool', 'new_v_pool_scale', 'new_v_sgu_ln_g', 'new_v_sgu_ln_b', 'new_v_w_spatial', 'new_v_b_spatial', 'new_v_w_out', 'new_v_g_ffn', 'new_v_w_gate', 'new_v_w_up', 'new_v_w_down', 'new_v_g_final']
TWIN_LEAF_KINDS = {'loss': 'loss', 'grad_x': 'grad_x', 'grad_g_mix': 'grad_w', 'grad_w_in': 'grad_w', 'grad_A_re': 'grad_w', 'grad_A_im': 'grad_w', 'grad_log_dt': 'grad_w', 'grad_B_re': 'grad_w', 'grad_B_im': 'grad_w', 'grad_C_re': 'grad_w', 'grad_C_im': 'grad_w', 'grad_D_skip': 'grad_w', 'grad_w_glu': 'grad_w', 'grad_b_glu': 'grad_w', 'grad_w_pool': 'grad_w', 'grad_pool_scale': 'grad_w', 'grad_sgu_ln_g': 'grad_w', 'grad_sgu_ln_b': 'grad_w', 'grad_w_spatial': 'grad_w', 'grad_b_spatial': 'grad_w', 'grad_w_out': 'grad_w', 'grad_g_ffn': 'grad_w', 'grad_w_gate': 'grad_w', 'grad_w_up': 'grad_w', 'grad_w_down': 'grad_w', 'grad_g_final': 'grad_w', 'delta_g_mix': 'delta_w', 'delta_w_in': 'delta_w', 'delta_A_re': 'delta_w', 'delta_A_im': 'delta_w', 'delta_log_dt': 'delta_w', 'delta_B_re': 'delta_w', 'delta_B_im': 'delta_w', 'delta_C_re': 'delta_w', 'delta_C_im': 'delta_w', 'delta_D_skip': 'delta_w', 'delta_w_glu': 'delta_w', 'delta_b_glu': 'delta_w', 'delta_w_pool': 'delta_w', 'delta_pool_scale': 'delta_w', 'delta_sgu_ln_g': 'delta_w', 'delta_sgu_ln_b': 'delta_w', 'delta_w_spatial': 'delta_w', 'delta_b_spatial': 'delta_w', 'delta_w_out': 'delta_w', 'delta_g_ffn': 'delta_w', 'delta_w_gate': 'delta_w', 'delta_w_up': 'delta_w', 'delta_w_down': 'delta_w', 'delta_g_final': 'delta_w', 'new_m_g_mix': 'new_m', 'new_m_w_in': 'new_m', 'new_m_A_re': 'new_m', 'new_m_A_im': 'new_m', 'new_m_log_dt': 'new_m', 'new_m_B_re': 'new_m', 'new_m_B_im': 'new_m', 'new_m_C_re': 'new_m', 'new_m_C_im': 'new_m', 'new_m_D_skip': 'new_m', 'new_m_w_glu': 'new_m', 'new_m_b_glu': 'new_m', 'new_m_w_pool': 'new_m', 'new_m_pool_scale': 'new_m', 'new_m_sgu_ln_g': 'new_m', 'new_m_sgu_ln_b': 'new_m', 'new_m_w_spatial': 'new_m', 'new_m_b_spatial': 'new_m', 'new_m_w_out': 'new_m', 'new_m_g_ffn': 'new_m', 'new_m_w_gate': 'new_m', 'new_m_w_up': 'new_m', 'new_m_w_down': 'new_m', 'new_m_g_final': 'new_m', 'new_v_g_mix': 'new_v', 'new_v_w_in': 'new_v', 'new_v_A_re': 'new_v', 'new_v_A_im': 'new_v', 'new_v_log_dt': 'new_v', 'new_v_B_re': 'new_v', 'new_v_B_im': 'new_v', 'new_v_C_re': 'new_v', 'new_v_C_im': 'new_v', 'new_v_D_skip': 'new_v', 'new_v_w_glu': 'new_v', 'new_v_b_glu': 'new_v', 'new_v_w_pool': 'new_v', 'new_v_pool_scale': 'new_v', 'new_v_sgu_ln_g': 'new_v', 'new_v_sgu_ln_b': 'new_v', 'new_v_w_spatial': 'new_v', 'new_v_b_spatial': 'new_v', 'new_v_w_out': 'new_v', 'new_v_g_ffn': 'new_v', 'new_v_w_gate': 'new_v', 'new_v_w_up': 'new_v', 'new_v_w_down': 'new_v', 'new_v_g_final': 'new_v'}


def _forward(args):
    return _fwd_reference(*[args[k] for k in FWD_PARAMS])


def _output_shape():
    def fwd():
        inp = _fwd_setup_inputs(0)
        return _fwd_reference(*[inp[k] for k in FWD_PARAMS])
    out = _jax.eval_shape(fwd)
    return out.shape, out.dtype

N_MICROBATCH = 1
ADAM_LR = 0.001
ADAM_B1 = 0.9
ADAM_B2 = 0.999
ADAM_EPS = 1e-08
ADAM_WD = 0.01
ADAM_STEP = 10
PER_EXAMPLE_BATCH_AXIS = {'x': 0, 'loss_target': 0}
SHARED_INPUTS = []
_WEIGHT_DTYPES = {'g_mix': _jnp.float32, 'w_in': _jnp.float32, 'A_re': _jnp.float32, 'A_im': _jnp.float32, 'log_dt': _jnp.float32, 'B_re': _jnp.float32, 'B_im': _jnp.float32, 'C_re': _jnp.float32, 'C_im': _jnp.float32, 'D_skip': _jnp.float32, 'w_glu': _jnp.float32, 'b_glu': _jnp.float32, 'w_pool': _jnp.float32, 'pool_scale': _jnp.float32, 'sgu_ln_g': _jnp.float32, 'sgu_ln_b': _jnp.float32, 'w_spatial': _jnp.float32, 'b_spatial': _jnp.float32, 'w_out': _jnp.float32, 'g_ffn': _jnp.float32, 'w_gate': _jnp.float32, 'w_up': _jnp.float32, 'w_down': _jnp.float32, 'g_final': _jnp.float32}
MOMENT_SCALE = {'g_mix': 1.731235e-01, 'w_in': 1.440308e-01, 'A_re': 4.185746e-03, 'A_im': 4.850988e-03, 'log_dt': 1.826184e+00, 'B_re': 2.812419e-03, 'B_im': 2.833056e-03, 'C_re': 5.760843e-03, 'C_im': 5.939634e-03, 'D_skip': 8.212316e-02, 'w_glu': 2.250905e-02, 'b_glu': 3.274006e-02, 'w_pool': 1.899361e-01, 'pool_scale': 1.862077e-01, 'sgu_ln_g': 9.977175e-02, 'sgu_ln_b': 1.081905e-01, 'w_spatial': 6.930831e-02, 'b_spatial': 1.022956e-01, 'w_out': 1.542413e-01, 'g_ffn': 1.544680e-01, 'w_gate': 6.721072e-02, 'w_up': 6.511878e-02, 'w_down': 1.080370e-01, 'g_final': 6.406353e+01}


def _to_microbatches(a, axis):
    t = _jnp.moveaxis(a, axis, 0)
    t = t.reshape((N_MICROBATCH, t.shape[0] // N_MICROBATCH) + t.shape[1:])
    return _jnp.moveaxis(t, 1, axis + 1)


def setup_inputs(seed: int = 0) -> dict:
    inp = _fwd_setup_inputs(seed)
    key = _jax.random.fold_in(_jax.random.key(seed), 7919)
    shape, _ = _output_shape()
    out = dict(inp)
    out["loss_target"] = _jax.random.normal(_jax.random.fold_in(key, 0), shape, _jnp.float32)
    for i, name in enumerate(TWIN_WEIGHTS):
        w = inp[name].astype(_jnp.float32)
        if MOMENT_SCALE is None:
            s = _jnp.sqrt(_jnp.mean(_jnp.square(w)) + 1e-30)
        else:
            s = MOMENT_SCALE[name]
        km, kv = _jax.random.split(_jax.random.fold_in(key, i + 1))
        out[name] = w
        out["m_" + name] = s * _jax.random.normal(km, w.shape, _jnp.float32)
        out["v_" + name] = (s * s) * _jax.random.uniform(kv, w.shape, _jnp.float32, 0.5, 1.5)
    if N_MICROBATCH > 1:
        for name, axis in PER_EXAMPLE_BATCH_AXIS.items():
            out[name] = _to_microbatches(out[name], axis)
    return {'x': out['x'], 'g_mix': out['g_mix'], 'w_in': out['w_in'], 'A_re': out['A_re'], 'A_im': out['A_im'], 'log_dt': out['log_dt'], 'B_re': out['B_re'], 'B_im': out['B_im'], 'C_re': out['C_re'], 'C_im': out['C_im'], 'D_skip': out['D_skip'], 'w_glu': out['w_glu'], 'b_glu': out['b_glu'], 'w_pool': out['w_pool'], 'pool_scale': out['pool_scale'], 'sgu_ln_g': out['sgu_ln_g'], 'sgu_ln_b': out['sgu_ln_b'], 'w_spatial': out['w_spatial'], 'b_spatial': out['b_spatial'], 'w_out': out['w_out'], 'g_ffn': out['g_ffn'], 'w_gate': out['w_gate'], 'w_up': out['w_up'], 'w_down': out['w_down'], 'g_final': out['g_final'], 'loss_target': out['loss_target'], 'm_g_mix': out['m_g_mix'], 'm_w_in': out['m_w_in'], 'm_A_re': out['m_A_re'], 'm_A_im': out['m_A_im'], 'm_log_dt': out['m_log_dt'], 'm_B_re': out['m_B_re'], 'm_B_im': out['m_B_im'], 'm_C_re': out['m_C_re'], 'm_C_im': out['m_C_im'], 'm_D_skip': out['m_D_skip'], 'm_w_glu': out['m_w_glu'], 'm_b_glu': out['m_b_glu'], 'm_w_pool': out['m_w_pool'], 'm_pool_scale': out['m_pool_scale'], 'm_sgu_ln_g': out['m_sgu_ln_g'], 'm_sgu_ln_b': out['m_sgu_ln_b'], 'm_w_spatial': out['m_w_spatial'], 'm_b_spatial': out['m_b_spatial'], 'm_w_out': out['m_w_out'], 'm_g_ffn': out['m_g_ffn'], 'm_w_gate': out['m_w_gate'], 'm_w_up': out['m_w_up'], 'm_w_down': out['m_w_down'], 'm_g_final': out['m_g_final'], 'v_g_mix': out['v_g_mix'], 'v_w_in': out['v_w_in'], 'v_A_re': out['v_A_re'], 'v_A_im': out['v_A_im'], 'v_log_dt': out['v_log_dt'], 'v_B_re': out['v_B_re'], 'v_B_im': out['v_B_im'], 'v_C_re': out['v_C_re'], 'v_C_im': out['v_C_im'], 'v_D_skip': out['v_D_skip'], 'v_w_glu': out['v_w_glu'], 'v_b_glu': out['v_b_glu'], 'v_w_pool': out['v_w_pool'], 'v_pool_scale': out['v_pool_scale'], 'v_sgu_ln_g': out['v_sgu_ln_g'], 'v_sgu_ln_b': out['v_sgu_ln_b'], 'v_w_spatial': out['v_w_spatial'], 'v_b_spatial': out['v_b_spatial'], 'v_w_out': out['v_w_out'], 'v_g_ffn': out['v_g_ffn'], 'v_w_gate': out['v_w_gate'], 'v_w_up': out['v_w_up'], 'v_w_down': out['v_w_down'], 'v_g_final': out['v_g_final']}


def _loss(weights, diff, rest, loss_target):
    with _jax.named_scope("forward"):
        args = {**rest, TWIN_DIFF_INPUT: diff, **{k: w.astype(_WEIGHT_DTYPES[k]) for k, w in weights.items()}}
        y = _forward(args)
    with _jax.named_scope("loss_head"):
        err = _jnp.square(y.astype(_jnp.float32) - loss_target)
        return 0.5 * _jnp.sum(_jnp.mean(err, axis=-1)) if err.ndim else 0.5 * err


def _adamw(w, g, m, v):
    m = ADAM_B1 * m + (1.0 - ADAM_B1) * g
    v = ADAM_B2 * v + (1.0 - ADAM_B2) * _jnp.square(g)
    m_hat = m / (1.0 - ADAM_B1 ** ADAM_STEP)
    v_hat = v / (1.0 - ADAM_B2 ** ADAM_STEP)
    delta = -ADAM_LR * (m_hat / (_jnp.sqrt(v_hat) + ADAM_EPS) + ADAM_WD * w)
    return delta, m, v


def reference(x, g_mix, w_in, A_re, A_im, log_dt, B_re, B_im, C_re, C_im, D_skip, w_glu, b_glu, w_pool, pool_scale, sgu_ln_g, sgu_ln_b, w_spatial, b_spatial, w_out, g_ffn, w_gate, w_up, w_down, g_final, loss_target, m_g_mix, m_w_in, m_A_re, m_A_im, m_log_dt, m_B_re, m_B_im, m_C_re, m_C_im, m_D_skip, m_w_glu, m_b_glu, m_w_pool, m_pool_scale, m_sgu_ln_g, m_sgu_ln_b, m_w_spatial, m_b_spatial, m_w_out, m_g_ffn, m_w_gate, m_w_up, m_w_down, m_g_final, v_g_mix, v_w_in, v_A_re, v_A_im, v_log_dt, v_B_re, v_B_im, v_C_re, v_C_im, v_D_skip, v_w_glu, v_b_glu, v_w_pool, v_pool_scale, v_sgu_ln_g, v_sgu_ln_b, v_w_spatial, v_b_spatial, v_w_out, v_g_ffn, v_w_gate, v_w_up, v_w_down, v_g_final):
    given = dict(x=x, g_mix=g_mix, w_in=w_in, A_re=A_re, A_im=A_im, log_dt=log_dt, B_re=B_re, B_im=B_im, C_re=C_re, C_im=C_im, D_skip=D_skip, w_glu=w_glu, b_glu=b_glu, w_pool=w_pool, pool_scale=pool_scale, sgu_ln_g=sgu_ln_g, sgu_ln_b=sgu_ln_b, w_spatial=w_spatial, b_spatial=b_spatial, w_out=w_out, g_ffn=g_ffn, w_gate=w_gate, w_up=w_up, w_down=w_down, g_final=g_final, loss_target=loss_target, m_g_mix=m_g_mix, m_w_in=m_w_in, m_A_re=m_A_re, m_A_im=m_A_im, m_log_dt=m_log_dt, m_B_re=m_B_re, m_B_im=m_B_im, m_C_re=m_C_re, m_C_im=m_C_im, m_D_skip=m_D_skip, m_w_glu=m_w_glu, m_b_glu=m_b_glu, m_w_pool=m_w_pool, m_pool_scale=m_pool_scale, m_sgu_ln_g=m_sgu_ln_g, m_sgu_ln_b=m_sgu_ln_b, m_w_spatial=m_w_spatial, m_b_spatial=m_b_spatial, m_w_out=m_w_out, m_g_ffn=m_g_ffn, m_w_gate=m_w_gate, m_w_up=m_w_up, m_w_down=m_w_down, m_g_final=m_g_final, v_g_mix=v_g_mix, v_w_in=v_w_in, v_A_re=v_A_re, v_A_im=v_A_im, v_log_dt=v_log_dt, v_B_re=v_B_re, v_B_im=v_B_im, v_C_re=v_C_re, v_C_im=v_C_im, v_D_skip=v_D_skip, v_w_glu=v_w_glu, v_b_glu=v_b_glu, v_w_pool=v_w_pool, v_pool_scale=v_pool_scale, v_sgu_ln_g=v_sgu_ln_g, v_sgu_ln_b=v_sgu_ln_b, v_w_spatial=v_w_spatial, v_b_spatial=v_b_spatial, v_w_out=v_w_out, v_g_ffn=v_g_ffn, v_w_gate=v_w_gate, v_w_up=v_w_up, v_w_down=v_w_down, v_g_final=v_g_final)
    weights = {n: given[n] for n in TWIN_WEIGHTS}
    shared = {n: given[n] for n in SHARED_INPUTS}
    per_example = {n: given[n] for n in ['x']}
    grad_fn = _jax.value_and_grad(_loss, argnums=(0, 1))

    def one_microbatch(ex, loss_target):
        ex = dict(ex)
        diff = ex.pop(TWIN_DIFF_INPUT)
        return grad_fn(weights, diff, {**shared, **ex}, loss_target)

    if N_MICROBATCH == 1:
        loss, (grad_w, grad_x) = one_microbatch(per_example, given["loss_target"])
    else:
        def body(carry, xs):
            loss_sum, grad_sum = carry
            l_k, (gw_k, gx_k) = one_microbatch(xs[0], xs[1])
            with _jax.named_scope("update"):
                return (loss_sum + l_k, _jax.tree.map(_jnp.add, grad_sum, gw_k)), gx_k

        init = (_jnp.zeros((), _jnp.float32), _jax.tree.map(_jnp.zeros_like, weights))
        (loss, grad_w), grad_x = _jax.lax.scan(body, init, (per_example, given["loss_target"]))
    with _jax.named_scope("update"):
        delta_w, new_m, new_v = {}, {}, {}
        for n in TWIN_WEIGHTS:
            delta_w[n], new_m[n], new_v[n] = _adamw(weights[n], grad_w[n], given["m_" + n], given["v_" + n])
    return (loss, grad_x, *[grad_w[n] for n in TWIN_WEIGHTS], *[delta_w[n] for n in TWIN_WEIGHTS],
            *[new_m[n] for n in TWIN_WEIGHTS], *[new_v[n] for n in TWIN_WEIGHTS])
```

```python
import math

import jax
import jax.numpy as jnp
from jax import lax
from jax.experimental import pallas as pl
from jax.experimental.pallas import tpu as pltpu

F32 = jnp.float32
BF16 = jnp.bfloat16

D_MODEL = 1024
DEPTH = 2
D_SSM = 384
SSM_GROUP = 16
N_GROUPS = 24
SSM_STATE = 64
N_STATE = N_GROUPS * SSM_STATE
POOL_WINDOWS = (2, 4, 8, 16)
POOL_GROUP = 64
D_POOL = 256
MAX_WINDOW = 16
SGU_HEADS = 6
SGU_HEAD_DIM = 64
D_SGU = 384
CHUNK = 128
D_IN = D_SSM + D_POOL + 2 * D_SGU
D_FF = 2816
EPS = 1e-6

ADAM_LR = 0.001
ADAM_B1 = 0.9
ADAM_B2 = 0.999
ADAM_EPS = 1e-08
ADAM_WD = 0.01
ADAM_STEP = 10

LANES = 128
SUBLANES = 8
N_SLAB = N_STATE // LANES
VMEM_LIMIT = 56 * 1024 * 1024

TS = 512
TS_FFN = 256

WEIGHTS = ['g_mix', 'w_in', 'A_re', 'A_im', 'log_dt', 'B_re', 'B_im', 'C_re', 'C_im', 'D_skip', 'w_glu', 'b_glu',
           'w_pool', 'pool_scale', 'sgu_ln_g', 'sgu_ln_b', 'w_spatial', 'b_spatial', 'w_out', 'g_ffn', 'w_gate',
           'w_up', 'w_down', 'g_final']
BIG = ['w_in', 'w_glu', 'w_out', 'w_gate', 'w_up', 'w_down']
SMALL = [n for n in WEIGHTS if n not in BIG]
N_CHIPS = 4
N_DEV = 8


def _cp(**kw):
    return pltpu.CompilerParams(vmem_limit_bytes=VMEM_LIMIT, **kw)


def _row(ts, n):
    return pl.BlockSpec((ts, n), lambda i: (i, 0))


def _const(shape):
    nd = len(shape)
    return pl.BlockSpec(shape, lambda i: (0,) * nd, pipeline_mode=pl.Buffered(1))


def _acc(shape):
    nd = len(shape)
    return pl.BlockSpec(shape, lambda i: (0,) * nd)


def _dot(a, b):
    return jnp.dot(a, b, preferred_element_type=F32)


def _dot_tn(a, b):
    return lax.dot_general(a, b, (((0,), (0,)), ((), ())), preferred_element_type=F32)


def _dot_nt(a, b):
    return lax.dot_general(a, b, (((1,), (1,)), ((), ())), preferred_element_type=F32)


_G0 = math.sqrt(2.0 / math.pi)
_G1 = 0.044715


def _gelu(x):
    return 0.5 * x * (1.0 + jnp.tanh(_G0 * (x + _G1 * x * x * x)))


def _gelu_grad(x):
    t = jnp.tanh(_G0 * (x + _G1 * x * x * x))
    return 0.5 * (1.0 + t) + 0.5 * x * (1.0 - t * t) * (_G0 * (1.0 + 3.0 * _G1 * x * x))


def _sigmoid(x):
    return 1.0 / (1.0 + jnp.exp(-x))


def _rms(x):
    r = lax.rsqrt(jnp.mean(x * x, axis=-1, keepdims=True) + EPS)
    return x * r, r


def _rms_bwd(dh, n, r, g):
    dn = dh * g
    return r * (dn - n * jnp.mean(dn * n, axis=-1, keepdims=True)), dh * n


def _colsum8(v):
    rows, n = v.shape
    return jnp.sum(v.reshape(rows // SUBLANES, SUBLANES, n), axis=0)


def _mix_in_fwd(x, g, w, tag):
    s = x.shape[0]

    def body(x_ref, g_ref, w_ref, za_ref, zb_ref, zuv_ref, h_ref):
        n, _ = _rms(x_ref[...])
        h = (n * g_ref[...]).astype(BF16)
        z = _dot(h, w_ref[...])
        za_ref[...] = z[:, :D_SSM]
        zb_ref[...] = z[:, D_SSM:D_SSM + D_POOL]
        zuv_ref[...] = z[:, D_SSM + D_POOL:]
        h_ref[...] = h

    return pl.pallas_call(
        body, grid=(s // TS,),
        in_specs=[_row(TS, D_MODEL), _const((1, D_MODEL)), _const((D_MODEL, D_IN))],
        out_specs=[_row(TS, D_SSM), _row(TS, D_POOL), _row(TS, 2 * D_SGU), _row(TS, D_MODEL)],
        out_shape=[jax.ShapeDtypeStruct((s, D_SSM), F32), jax.ShapeDtypeStruct((s, D_POOL), F32),
                   jax.ShapeDtypeStruct((s, 2 * D_SGU), F32), jax.ShapeDtypeStruct((s, D_MODEL), BF16)],
        name=f"mix_in_fwd_{tag}", compiler_params=_cp(dimension_semantics=("arbitrary",)),
    )(x, g, w)


def _cmul(ar, ai, br, bi):
    return ar * br - ai * bi, ar * bi + ai * br


def _cpow(ar, ai, n):
    assert n & (n - 1) == 0
    while n > 1:
        ar, ai = _cmul(ar, ai, ar, ai)
        n //= 2
    return ar, ai


def _to_slabs(ref, v):
    for j in range(N_SLAB):
        ref[j] = v[:, LANES * j:LANES * (j + 1)]


def _from_slabs(ref):
    return jnp.concatenate([ref[j] for j in range(N_SLAB)], axis=1)


def _s5_fwd(u, p, tag):
    s = u.shape[0]
    seg = TS // SUBLANES

    def body(u_ref, bbr_ref, bbi_ref, ar_ref, ai_ref, ctr_ref, cti_ref, dsk_ref, wglu_ref, bglu_ref,
             oa_ref, y_ref, hr_ref, hi_ref, sr, si, er, ei, ir, ii, cr, ci):
        @pl.when(pl.program_id(0) == 0)
        def _():
            cr[...] = jnp.zeros_like(cr)
            ci[...] = jnp.zeros_like(ci)

        uv = u_ref[...]
        ub = uv.astype(BF16)
        _to_slabs(sr, _dot(ub, bbr_ref[...]))
        _to_slabs(si, _dot(ub, bbi_ref[...]))
        ar = ar_ref[...]
        ai = ai_ref[...]

        def local(k, h):
            rows = pl.ds(k, SUBLANES, stride=seg)
            hr, hi = _cmul(ar, ai, h[0], h[1])
            return hr + sr[:, rows, :], hi + si[:, rows, :]

        zero = jnp.zeros((N_SLAB, SUBLANES, LANES), F32)
        e_r, e_i = lax.fori_loop(0, seg, local, (zero, zero))
        er[...] = e_r
        ei[...] = e_i
        pr, pi = _cpow(ar[:, 0:1, :], ai[:, 0:1, :], seg)
        c_r = cr[...]
        c_i = ci[...]
        for j in range(SUBLANES):
            ir[:, j:j + 1, :] = c_r
            ii[:, j:j + 1, :] = c_i
            n_r, n_i = _cmul(pr, pi, c_r, c_i)
            c_r = n_r + er[:, j:j + 1, :]
            c_i = n_i + ei[:, j:j + 1, :]
        cr[...] = c_r
        ci[...] = c_i

        def full(k, h):
            rows = pl.ds(k, SUBLANES, stride=seg)
            hr, hi = _cmul(ar, ai, h[0], h[1])
            hr = hr + sr[:, rows, :]
            hi = hi + si[:, rows, :]
            sr[:, rows, :] = hr
            si[:, rows, :] = hi
            return hr, hi

        lax.fori_loop(0, seg, full, (ir[...], ii[...]))
        hr = _from_slabs(sr)
        hi = _from_slabs(si)
        hr_ref[...] = hr
        hi_ref[...] = hi
        y = _dot(hr.astype(BF16), ctr_ref[...]) - _dot(hi.astype(BF16), cti_ref[...]) + dsk_ref[...] * uv
        y_ref[...] = y
        g = _gelu(y)
        pre = _dot(g.astype(BF16), wglu_ref[...]) + bglu_ref[...]
        oa_ref[...] = (g * _sigmoid(pre)).astype(BF16)

    slab = (N_SLAB, SUBLANES, LANES)
    return pl.pallas_call(
        body, grid=(s // TS,),
        in_specs=[_row(TS, D_SSM), _const((D_SSM, N_STATE)), _const((D_SSM, N_STATE)), _const(slab), _const(slab),
                  _const((N_STATE, D_SSM)), _const((N_STATE, D_SSM)), _const((1, D_SSM)), _const((D_SSM, D_SSM)),
                  _const((1, D_SSM))],
        out_specs=[_row(TS, D_SSM), _row(TS, D_SSM), _row(TS, N_STATE), _row(TS, N_STATE)],
        out_shape=[jax.ShapeDtypeStruct((s, D_SSM), BF16), jax.ShapeDtypeStruct((s, D_SSM), F32),
                   jax.ShapeDtypeStruct((s, N_STATE), F32), jax.ShapeDtypeStruct((s, N_STATE), F32)],
        scratch_shapes=[pltpu.VMEM((N_SLAB, TS, LANES), F32), pltpu.VMEM((N_SLAB, TS, LANES), F32),
                        pltpu.VMEM(slab, F32), pltpu.VMEM(slab, F32), pltpu.VMEM(slab, F32), pltpu.VMEM(slab, F32),
                        pltpu.VMEM((N_SLAB, 1, LANES), F32), pltpu.VMEM((N_SLAB, 1, LANES), F32)],
        name=f"s5_fwd_{tag}", compiler_params=_cp(dimension_semantics=("arbitrary",)),
    )(u, p["bbt_re"], p["bbt_im"], p["a_re8"], p["a_im8"], p["ct_re"], p["ct_im"], p["d_skip"], p["w_glu"],
      p["b_glu"])


def _pool_consts():
    w = jnp.repeat(jnp.asarray(POOL_WINDOWS, F32), POOL_GROUP)[None, :]
    return w


def _window_sum(buf, first, rows, wl, step):
    acc = buf[pl.ds(first, rows), :]
    for j in range(1, MAX_WINDOW):
        term = buf[pl.ds(first + step * j, rows), :]
        acc = acc + (term if j < min(POOL_WINDOWS) else term * (wl > j).astype(F32))
    return acc


def _pool_count(i, rows, wl, offset=0):
    t = (i * TS + offset + 1).astype(F32) + lax.broadcasted_iota(jnp.int32, (rows, 1), 0).astype(F32)
    return jnp.minimum(t, wl)


def _pool_fwd(zb, p, tag):
    s = zb.shape[0]
    hb = TS // MAX_WINDOW

    def body(u_ref, halo_ref, wl_ref, w_ref, sc_ref, ob_ref, pooled_ref, buf):
        i = pl.program_id(0)
        uv = u_ref[...]
        buf[pl.ds(0, MAX_WINDOW), :] = jnp.where(i > 0, halo_ref[...], 0.0)
        buf[pl.ds(MAX_WINDOW, TS), :] = uv
        wl = wl_ref[...]
        pooled = (_window_sum(buf, MAX_WINDOW, TS, wl, -1) / _pool_count(i, TS, wl) - uv).astype(BF16)
        pooled_ref[...] = pooled
        ob_ref[...] = (_dot(pooled, w_ref[...]) * sc_ref[...]).astype(BF16)

    return pl.pallas_call(
        body, grid=(s // TS,),
        in_specs=[_row(TS, D_POOL),
                  pl.BlockSpec((MAX_WINDOW, D_POOL), lambda i: (jnp.maximum(i * hb - 1, 0), 0)),
                  _const((1, D_POOL)), _const((D_POOL, D_POOL)), _const((1, D_POOL))],
        out_specs=[_row(TS, D_POOL), _row(TS, D_POOL)],
        out_shape=[jax.ShapeDtypeStruct((s, D_POOL), BF16), jax.ShapeDtypeStruct((s, D_POOL), BF16)],
        scratch_shapes=[pltpu.VMEM((TS + MAX_WINDOW, D_POOL), F32)],
        name=f"pool_fwd_{tag}", compiler_params=_cp(dimension_semantics=("arbitrary",)),
    )(zb, zb, _pool_consts(), p["w_pool_bd"], p["pool_scale"])


def _sgu_mix(vl, wpair_ref, lo, hi):
    rows = vl.shape[0]
    chunks = []
    for c in range(rows // CHUNK):
        vc = vl[CHUNK * c:CHUNK * (c + 1), :]
        parts = []
        for q in range(SGU_HEADS // 2):
            vq = vc[:, LANES * q:LANES * (q + 1)]
            rhs = jnp.concatenate([vq * lo, vq * hi], axis=0).astype(BF16)
            parts.append(_dot(wpair_ref[q], rhs))
        chunks.append(jnp.concatenate(parts, axis=1))
    return jnp.concatenate(chunks, axis=0)


def _sgu_front(zuv, lng, lnb):
    zu = zuv[:, :D_SGU]
    zv = zuv[:, D_SGU:]
    u = _gelu(zu)
    v = _gelu(zv)
    mu = jnp.mean(v, axis=-1, keepdims=True)
    vc = v - mu
    rs = lax.rsqrt(jnp.mean(vc * vc, axis=-1, keepdims=True) + EPS)
    vn = vc * rs
    return zu, zv, u, vn, rs, vn * lng + lnb


def _half_masks():
    lane = lax.broadcasted_iota(jnp.int32, (1, LANES), 1)
    lo = (lane < SGU_HEAD_DIM).astype(F32)
    return lo, 1.0 - lo


def _sgu_fwd(zuv, p, tag):
    s = zuv.shape[0]

    def body(z_ref, lng_ref, lnb_ref, wp_ref, bias_ref, oc_ref):
        lo, hi = _half_masks()
        _, _, u, _, _, vl = _sgu_front(z_ref[...], lng_ref[...], lnb_ref[...])
        mixed = _sgu_mix(vl, wp_ref, lo, hi) + jnp.tile(bias_ref[...], (TS // CHUNK, 1))
        oc_ref[...] = (u * mixed).astype(BF16)

    return pl.pallas_call(
        body, grid=(s // TS,),
        in_specs=[_row(TS, 2 * D_SGU), _const((1, D_SGU)), _const((1, D_SGU)),
                  _const((SGU_HEADS // 2, CHUNK, 2 * CHUNK)), _const((CHUNK, D_SGU))],
        out_specs=_row(TS, D_SGU),
        out_shape=jax.ShapeDtypeStruct((s, D_SGU), BF16),
        name=f"sgu_fwd_{tag}", compiler_params=_cp(dimension_semantics=("arbitrary",)),
    )(zuv, p["sgu_ln_g"], p["sgu_ln_b"], p["ws_pair"], p["bias_sp"])


def _blk_fwd(x0, oa, ob, oc, p, tag):
    s = x0.shape[0]
    ts = TS_FFN

    def body(x0_ref, oa_ref, ob_ref, oc_ref, wo_ref, g_ref, wg_ref, wu_ref, wd_ref,
             x1_ref, x2_ref, h2_ref, gt_ref, up_ref):
        ycat = jnp.concatenate([oa_ref[...], ob_ref[...], oc_ref[...]], axis=1)
        x1 = x0_ref[...] + _dot(ycat, wo_ref[...])
        x1_ref[...] = x1
        n, _ = _rms(x1)
        h2 = (n * g_ref[...]).astype(BF16)
        h2_ref[...] = h2
        gt = _dot(h2, wg_ref[...])
        up = _dot(h2, wu_ref[...])
        gt_ref[...] = gt.astype(BF16)
        up_ref[...] = up.astype(BF16)
        act = (gt * _sigmoid(gt) * up).astype(BF16)
        x2_ref[...] = x1 + _dot(act, wd_ref[...])

    return pl.pallas_call(
        body, grid=(s // ts,),
        in_specs=[_row(ts, D_MODEL), _row(ts, D_SSM), _row(ts, D_POOL), _row(ts, D_SGU),
                  _const((D_MODEL, D_MODEL)), _const((1, D_MODEL)), _const((D_MODEL, D_FF)),
                  _const((D_MODEL, D_FF)), _const((D_FF, D_MODEL))],
        out_specs=[_row(ts, D_MODEL), _row(ts, D_MODEL), _row(ts, D_MODEL), _row(ts, D_FF), _row(ts, D_FF)],
        out_shape=[jax.ShapeDtypeStruct((s, D_MODEL), F32), jax.ShapeDtypeStruct((s, D_MODEL), F32),
                   jax.ShapeDtypeStruct((s, D_MODEL), BF16), jax.ShapeDtypeStruct((s, D_FF), BF16),
                   jax.ShapeDtypeStruct((s, D_FF), BF16)],
        name=f"blk_fwd_{tag}", compiler_params=_cp(dimension_semantics=("arbitrary",)),
    )(x0, oa, ob, oc, p["w_out"], p["g_ffn"], p["w_gate"], p["w_up"], p["w_down"])


def _blk_bwd(dx2, x1, gt, up, p, tag):
    s = dx2.shape[0]
    ts = TS_FFN

    def body(dx2_ref, x1_ref, gt_ref, up_ref, wdt_ref, wgt_ref, wut_ref, wot_ref, g_ref,
             dx1_ref, da_ref, db_ref, dc_ref, dgt_ref, dup_ref, act_ref, dg_ref):
        @pl.when(pl.program_id(0) == 0)
        def _():
            dg_ref[...] = jnp.zeros_like(dg_ref)

        dx2v = dx2_ref[...]
        dact = _dot(dx2v.astype(BF16), wdt_ref[...])
        gf = gt_ref[...].astype(F32)
        uf = up_ref[...].astype(F32)
        sg = _sigmoid(gf)
        sl = gf * sg
        act_ref[...] = (sl * uf).astype(BF16)
        dgt = (dact * uf * (sg * (1.0 + gf * (1.0 - sg)))).astype(BF16)
        dup = (dact * sl).astype(BF16)
        dgt_ref[...] = dgt
        dup_ref[...] = dup
        dh2 = _dot(dgt, wgt_ref[...]) + _dot(dup, wut_ref[...])
        n, r = _rms(x1_ref[...])
        dxn, dgp = _rms_bwd(dh2, n, r, g_ref[...])
        dg_ref[...] += _colsum8(dgp)
        dx1 = dx2v + dxn
        dx1_ref[...] = dx1
        dy = _dot(dx1.astype(BF16), wot_ref[...])
        da_ref[...] = dy[:, :D_SSM]
        db_ref[...] = dy[:, D_SSM:D_SSM + D_POOL]
        dc_ref[...] = dy[:, D_SSM + D_POOL:]

    return pl.pallas_call(
        body, grid=(s // ts,),
        in_specs=[_row(ts, D_MODEL), _row(ts, D_MODEL), _row(ts, D_FF), _row(ts, D_FF),
                  _const((D_MODEL, D_FF)), _const((D_FF, D_MODEL)), _const((D_FF, D_MODEL)),
                  _const((D_MODEL, D_MODEL)), _const((1, D_MODEL))],
        out_specs=[_row(ts, D_MODEL), _row(ts, D_SSM), _row(ts, D_POOL), _row(ts, D_SGU), _row(ts, D_FF),
                   _row(ts, D_FF), _row(ts, D_FF), _acc((SUBLANES, D_MODEL))],
        out_shape=[jax.ShapeDtypeStruct((s, D_MODEL), F32), jax.ShapeDtypeStruct((s, D_SSM), F32),
                   jax.ShapeDtypeStruct((s, D_POOL), F32), jax.ShapeDtypeStruct((s, D_SGU), F32),
                   jax.ShapeDtypeStruct((s, D_FF), BF16), jax.ShapeDtypeStruct((s, D_FF), BF16),
                   jax.ShapeDtypeStruct((s, D_FF), BF16), jax.ShapeDtypeStruct((SUBLANES, D_MODEL), F32)],
        name=f"blk_bwd_{tag}", compiler_params=_cp(dimension_semantics=("arbitrary",)),
    )(dx2, x1, gt, up, p["w_down_t"], p["w_gate_t"], p["w_up_t"], p["w_out_t"], p["g_ffn"])


def _s5_bwd(dout, u, y, h_re, h_im, p, tag):
    s = u.shape[0]
    nt = s // TS
    seg = TS // SUBLANES

    def rev(n):
        return pl.BlockSpec((TS, n), lambda i: (nt - 1 - i, 0))

    def body(do_ref, u_ref, y_ref, hr_ref, hi_ref, ar_ref, ai_ref, cbr_ref, cbi_ref, bbr_ref, bbi_ref, dsk_ref,
             wglu_ref, wglut_ref, bglu_ref,
             du_ref, dctr_ref, dcti_ref, dbbr_ref, dbbi_ref, dar_ref, dai_ref, dd_ref, dwglu_ref, dbglu_ref,
             gr, gi, hsr, hsi, er, ei, jr, ji, cr, ci):
        @pl.when(pl.program_id(0) == 0)
        def _():
            for ref in (cr, ci, dctr_ref, dcti_ref, dbbr_ref, dbbi_ref, dar_ref, dai_ref, dd_ref, dwglu_ref,
                        dbglu_ref):
                ref[...] = jnp.zeros_like(ref)

        uv = u_ref[...]
        yv = y_ref[...]
        dov = do_ref[...]
        g = _gelu(yv)
        gb = g.astype(BF16)
        sg = _sigmoid(_dot(gb, wglu_ref[...]) + bglu_ref[...])
        dpre = dov * g * sg * (1.0 - sg)
        dpb = dpre.astype(BF16)
        dwglu_ref[...] += _dot_tn(gb, dpb)
        dbglu_ref[...] += _colsum8(dpre)
        dy = (dov * sg + _dot(dpb, wglut_ref[...])) * _gelu_grad(yv)
        dd_ref[...] += _colsum8(dy * uv)
        dyb = dy.astype(BF16)
        hr = hr_ref[...]
        hi = hi_ref[...]
        dctr_ref[...] += _dot_tn(hr.astype(BF16), dyb)
        dcti_ref[...] -= _dot_tn(hi.astype(BF16), dyb)
        _to_slabs(hsr, hr)
        _to_slabs(hsi, hi)
        _to_slabs(gr, _dot(dyb, cbr_ref[...]))
        _to_slabs(gi, -_dot(dyb, cbi_ref[...]))
        ar = ar_ref[...]
        ai = -ai_ref[...]

        def local(k, h):
            rows = pl.ds(seg - 1 - k, SUBLANES, stride=seg)
            nr, ni = _cmul(ar, ai, h[0], h[1])
            return nr + gr[:, rows, :], ni + gi[:, rows, :]

        zero = jnp.zeros((N_SLAB, SUBLANES, LANES), F32)
        e_r, e_i = lax.fori_loop(0, seg, local, (zero, zero))
        er[...] = e_r
        ei[...] = e_i
        pr, pi = _cpow(ar[:, 0:1, :], ai[:, 0:1, :], seg)
        c_r = cr[...]
        c_i = ci[...]
        for j in range(SUBLANES - 1, -1, -1):
            jr[:, j:j + 1, :] = c_r
            ji[:, j:j + 1, :] = c_i
            n_r, n_i = _cmul(pr, pi, c_r, c_i)
            c_r = n_r + er[:, j:j + 1, :]
            c_i = n_i + ei[:, j:j + 1, :]
        cr[...] = c_r
        ci[...] = c_i

        def full(k, carry):
            g_r, g_i, a_r, a_i = carry
            rows = pl.ds(seg - 1 - k, SUBLANES, stride=seg)
            h_r = hsr[:, rows, :]
            h_i = hsi[:, rows, :]
            a_r = a_r + g_r * h_r + g_i * h_i
            a_i = a_i + g_i * h_r - g_r * h_i
            nr, ni = _cmul(ar, ai, g_r, g_i)
            nr = nr + gr[:, rows, :]
            ni = ni + gi[:, rows, :]
            gr[:, rows, :] = nr
            gi[:, rows, :] = ni
            return nr, ni, a_r, a_i

        _, _, a_r, a_i = lax.fori_loop(0, seg, full, (jr[...], ji[...], zero, zero))
        dar_ref[...] += a_r
        dai_ref[...] += a_i
        gbr = _from_slabs(gr).astype(BF16)
        gbi = _from_slabs(gi).astype(BF16)
        ub = uv.astype(BF16)
        dbbr_ref[...] += _dot_tn(ub, gbr)
        dbbi_ref[...] += _dot_tn(ub, gbi)
        du_ref[...] = dy * dsk_ref[...] + _dot(gbr, bbr_ref[...]) + _dot(gbi, bbi_ref[...])

    slab = (N_SLAB, SUBLANES, LANES)
    big = (N_SLAB, TS, LANES)
    return pl.pallas_call(
        body, grid=(nt,),
        in_specs=[rev(D_SSM), rev(D_SSM), rev(D_SSM), rev(N_STATE), rev(N_STATE), _const(slab), _const(slab),
                  _const((D_SSM, N_STATE)), _const((D_SSM, N_STATE)), _const((N_STATE, D_SSM)),
                  _const((N_STATE, D_SSM)), _const((1, D_SSM)), _const((D_SSM, D_SSM)), _const((D_SSM, D_SSM)),
                  _const((1, D_SSM))],
        out_specs=[rev(D_SSM), _acc((N_STATE, D_SSM)), _acc((N_STATE, D_SSM)), _acc((D_SSM, N_STATE)),
                   _acc((D_SSM, N_STATE)), _acc(slab), _acc(slab), _acc((SUBLANES, D_SSM)), _acc((D_SSM, D_SSM)),
                   _acc((SUBLANES, D_SSM))],
        out_shape=[jax.ShapeDtypeStruct((s, D_SSM), F32), jax.ShapeDtypeStruct((N_STATE, D_SSM), F32),
                   jax.ShapeDtypeStruct((N_STATE, D_SSM), F32), jax.ShapeDtypeStruct((D_SSM, N_STATE), F32),
                   jax.ShapeDtypeStruct((D_SSM, N_STATE), F32), jax.ShapeDtypeStruct(slab, F32),
                   jax.ShapeDtypeStruct(slab, F32), jax.ShapeDtypeStruct((SUBLANES, D_SSM), F32),
                   jax.ShapeDtypeStruct((D_SSM, D_SSM), F32), jax.ShapeDtypeStruct((SUBLANES, D_SSM), F32)],
        scratch_shapes=[pltpu.VMEM(big, F32), pltpu.VMEM(big, F32), pltpu.VMEM(big, F32), pltpu.VMEM(big, F32),
                        pltpu.VMEM(slab, F32), pltpu.VMEM(slab, F32), pltpu.VMEM(slab, F32), pltpu.VMEM(slab, F32),
                        pltpu.VMEM((N_SLAB, 1, LANES), F32), pltpu.VMEM((N_SLAB, 1, LANES), F32)],
        name=f"s5_bwd_{tag}", compiler_params=_cp(dimension_semantics=("arbitrary",)),
    )(dout, u, y, h_re, h_im, p["a_re8"], p["a_im8"], p["cb_re"], p["cb_im"], p["bb_re"], p["bb_im"], p["d_skip"],
      p["w_glu"], p["w_glu_t"], p["b_glu"])


def _pool_bwd(dout, pooled, p, tag):
    s = dout.shape[0]
    nt = s // TS
    hb = TS // MAX_WINDOW

    def halo(n):
        return pl.BlockSpec((MAX_WINDOW, n), lambda i: (jnp.minimum((i + 1) * hb, nt * hb - 1), 0))

    def body(do_ref, po_ref, doh_ref, wl_ref, w_ref, wt_ref, sc_ref, dz_ref, dw_ref, dsc_ref, buf):
        i = pl.program_id(0)

        @pl.when(i == 0)
        def _():
            dw_ref[...] = jnp.zeros_like(dw_ref)
            dsc_ref[...] = jnp.zeros_like(dsc_ref)

        wl = wl_ref[...]
        sc = sc_ref[...]
        dov = do_ref[...]
        pooled_b = po_ref[...]
        dsc_ref[...] += _colsum8(dov * _dot(pooled_b, w_ref[...]))
        dmix = (dov * sc).astype(BF16)
        dw_ref[...] += _dot_tn(pooled_b, dmix)
        dpool = _dot(dmix, wt_ref[...])
        dpool_h = _dot((doh_ref[...] * sc).astype(BF16), wt_ref[...])
        buf[pl.ds(0, TS), :] = dpool / _pool_count(i, TS, wl)
        buf[pl.ds(TS, MAX_WINDOW), :] = jnp.where(i < nt - 1, dpool_h / _pool_count(i, MAX_WINDOW, wl, TS), 0.0)
        dz_ref[...] = _window_sum(buf, 0, TS, wl, 1) - dpool

    return pl.pallas_call(
        body, grid=(nt,),
        in_specs=[_row(TS, D_POOL), _row(TS, D_POOL), halo(D_POOL), _const((1, D_POOL)), _const((D_POOL, D_POOL)),
                  _const((D_POOL, D_POOL)), _const((1, D_POOL))],
        out_specs=[_row(TS, D_POOL), _acc((D_POOL, D_POOL)), _acc((SUBLANES, D_POOL))],
        out_shape=[jax.ShapeDtypeStruct((s, D_POOL), F32), jax.ShapeDtypeStruct((D_POOL, D_POOL), F32),
                   jax.ShapeDtypeStruct((SUBLANES, D_POOL), F32)],
        scratch_shapes=[pltpu.VMEM((TS + MAX_WINDOW, D_POOL), F32)],
        name=f"pool_bwd_{tag}", compiler_params=_cp(dimension_semantics=("arbitrary",)),
    )(dout, pooled, dout, _pool_consts(), p["w_pool_bd"], p["w_pool_bd_t"], p["pool_scale"])


def _sgu_bwd(dout, zuv, p, tag):
    s = zuv.shape[0]

    def body(do_ref, z_ref, lng_ref, lnb_ref, wp_ref, wpt_ref, bias_ref,
             dz_ref, dws_ref, dbias_ref, dlng_ref, dlnb_ref):
        @pl.when(pl.program_id(0) == 0)
        def _():
            for ref in (dws_ref, dbias_ref, dlng_ref, dlnb_ref):
                ref[...] = jnp.zeros_like(ref)

        lo, hi = _half_masks()
        lng = lng_ref[...]
        zu, zv, u, vn, rs, vl = _sgu_front(z_ref[...], lng, lnb_ref[...])
        mixed = _sgu_mix(vl, wp_ref, lo, hi) + jnp.tile(bias_ref[...], (TS // CHUNK, 1))
        dov = do_ref[...]
        dzu = dov * mixed * _gelu_grad(zu)
        dmix = dov * u
        dbias = dbias_ref[...]
        for c in range(TS // CHUNK):
            dmc = dmix[CHUNK * c:CHUNK * (c + 1), :]
            dbias = dbias + dmc
            vlc = vl[CHUNK * c:CHUNK * (c + 1), :].astype(BF16)
            for q in range(SGU_HEADS // 2):
                dq = dmc[:, LANES * q:LANES * (q + 1)]
                vq = vlc[:, LANES * q:LANES * (q + 1)]
                dws_ref[2 * q] += _dot_nt((dq * lo).astype(BF16), vq)
                dws_ref[2 * q + 1] += _dot_nt((dq * hi).astype(BF16), vq)
        dbias_ref[...] = dbias
        dvl = _sgu_mix(dmix, wpt_ref, lo, hi)
        dlng_ref[...] += _colsum8(dvl * vn)
        dlnb_ref[...] += _colsum8(dvl)
        dvn = dvl * lng
        dv = rs * (dvn - jnp.mean(dvn, axis=-1, keepdims=True) - vn * jnp.mean(dvn * vn, axis=-1, keepdims=True))
        dz_ref[...] = jnp.concatenate([dzu, dv * _gelu_grad(zv)], axis=1)

    return pl.pallas_call(
        body, grid=(s // TS,),
        in_specs=[_row(TS, D_SGU), _row(TS, 2 * D_SGU), _const((1, D_SGU)), _const((1, D_SGU)),
                  _const((SGU_HEADS // 2, CHUNK, 2 * CHUNK)), _const((SGU_HEADS // 2, CHUNK, 2 * CHUNK)),
                  _const((CHUNK, D_SGU))],
        out_specs=[_row(TS, 2 * D_SGU), _acc((SGU_HEADS, CHUNK, CHUNK)), _acc((CHUNK, D_SGU)),
                   _acc((SUBLANES, D_SGU)), _acc((SUBLANES, D_SGU))],
        out_shape=[jax.ShapeDtypeStruct((s, 2 * D_SGU), F32), jax.ShapeDtypeStruct((SGU_HEADS, CHUNK, CHUNK), F32),
                   jax.ShapeDtypeStruct((CHUNK, D_SGU), F32), jax.ShapeDtypeStruct((SUBLANES, D_SGU), F32),
                   jax.ShapeDtypeStruct((SUBLANES, D_SGU), F32)],
        name=f"sgu_bwd_{tag}", compiler_params=_cp(dimension_semantics=("arbitrary",)),
    )(dout, zuv, p["sgu_ln_g"], p["sgu_ln_b"], p["ws_pair"], p["ws_pair_t"], p["bias_sp"])


def _mix_in_bwd(dza, dzb, dzuv, x0, dx1, p, tag):
    s = x0.shape[0]

    def body(da_ref, db_ref, dc_ref, x_ref, dx1_ref, wt_ref, g_ref, dx0_ref, dz_ref, dg_ref):
        @pl.when(pl.program_id(0) == 0)
        def _():
            dg_ref[...] = jnp.zeros_like(dg_ref)

        dz = jnp.concatenate([da_ref[...], db_ref[...], dc_ref[...]], axis=1).astype(BF16)
        dz_ref[...] = dz
        n, r = _rms(x_ref[...])
        dxn, dgp = _rms_bwd(_dot(dz, wt_ref[...]), n, r, g_ref[...])
        dg_ref[...] += _colsum8(dgp)
        dx0_ref[...] = dx1_ref[...] + dxn

    return pl.pallas_call(
        body, grid=(s // TS,),
        in_specs=[_row(TS, D_SSM), _row(TS, D_POOL), _row(TS, 2 * D_SGU), _row(TS, D_MODEL), _row(TS, D_MODEL),
                  _const((D_IN, D_MODEL)), _const((1, D_MODEL))],
        out_specs=[_row(TS, D_MODEL), _row(TS, D_IN), _acc((SUBLANES, D_MODEL))],
        out_shape=[jax.ShapeDtypeStruct((s, D_MODEL), F32), jax.ShapeDtypeStruct((s, D_IN), BF16),
                   jax.ShapeDtypeStruct((SUBLANES, D_MODEL), F32)],
        name=f"mix_in_bwd_{tag}", compiler_params=_cp(dimension_semantics=("arbitrary",)),
    )(dza, dzb, dzuv, x0, dx1, p["w_in_t"], p["g_mix"])


def _head(x, target, g):
    s = x.shape[0]

    def body(x_ref, t_ref, g_ref, dx_ref, loss_ref, dg_ref):
        @pl.when(pl.program_id(0) == 0)
        def _():
            loss_ref[...] = jnp.zeros_like(loss_ref)
            dg_ref[...] = jnp.zeros_like(dg_ref)

        gv = g_ref[...]
        n, r = _rms(x_ref[...])
        diff = n * gv - t_ref[...]
        loss_ref[...] += jnp.sum(diff * diff) * (0.5 / D_MODEL)
        dxn, dgp = _rms_bwd(diff * (1.0 / D_MODEL), n, r, gv)
        dg_ref[...] += _colsum8(dgp)
        dx_ref[...] = dxn

    return pl.pallas_call(
        body, grid=(s // TS,),
        in_specs=[_row(TS, D_MODEL), _row(TS, D_MODEL), _const((1, D_MODEL))],
        out_specs=[_row(TS, D_MODEL), _acc((SUBLANES, LANES)), _acc((SUBLANES, D_MODEL))],
        out_shape=[jax.ShapeDtypeStruct((s, D_MODEL), F32), jax.ShapeDtypeStruct((SUBLANES, LANES), F32),
                   jax.ShapeDtypeStruct((SUBLANES, D_MODEL), F32)],
        name="head", compiler_params=_cp(dimension_semantics=("arbitrary",)),
    )(x, target, g)


def _atb(a, b, tn, tag):
    s, ka = a.shape
    kb = b.shape[1]
    ns = s // TS

    def body(a_ref, b_ref, o_ref):
        @pl.when(pl.program_id(1) == 0)
        def _():
            o_ref[...] = jnp.zeros_like(o_ref)

        o_ref[...] += _dot_tn(a_ref[...].astype(BF16), b_ref[...].astype(BF16))

    return pl.pallas_call(
        body, grid=(kb // tn, ns),
        in_specs=[pl.BlockSpec((TS, ka), lambda j, i: (i, 0)), pl.BlockSpec((TS, tn), lambda j, i: (i, j))],
        out_specs=pl.BlockSpec((ka, tn), lambda j, i: (0, j)),
        out_shape=jax.ShapeDtypeStruct((ka, kb), F32),
        name=f"atb_{tag}", compiler_params=_cp(dimension_semantics=("arbitrary", "arbitrary")),
    )(a, b)


def _s5_discretise(a_re, a_im, log_dt, b_re, b_im):
    dt = jnp.exp(log_dt)[:, None]
    mag = jnp.exp(a_re * dt)
    ar = mag * jnp.cos(a_im * dt)
    ai = mag * jnp.sin(a_im * dt)
    den = a_re * a_re + a_im * a_im
    f_re = ((ar - 1.0) * a_re + ai * a_im) / den
    f_im = (ai * a_re - (ar - 1.0) * a_im) / den
    bb_re = f_re[..., None] * b_re - f_im[..., None] * b_im
    bb_im = f_re[..., None] * b_im + f_im[..., None] * b_re
    return ar, ai, bb_re, bb_im


def _block_diag(blocks):
    g, r, c = blocks.shape
    eye = jnp.eye(g, dtype=blocks.dtype)
    return (blocks[:, :, None, :] * eye[:, None, :, None]).reshape(g * r, g * c)


def _block_diag_extract(m, g):
    r = m.shape[0] // g
    c = m.shape[1] // g
    eye = jnp.eye(g, dtype=m.dtype)
    return jnp.sum(m.reshape(g, r, g, c) * eye[:, None, :, None], axis=2)


def _state_slabs(v):
    return jnp.broadcast_to(v.reshape(N_SLAB, 1, LANES), (N_SLAB, SUBLANES, LANES))


def _tril():
    return jnp.tril(jnp.ones((CHUNK, CHUNK), dtype=bool))


def _layer_params(w, big, l):
    row = lambda v: v.reshape(1, -1)
    t = lambda m: jnp.swapaxes(m, -1, -2)
    ar, ai, bb_re, bb_im = _s5_discretise(w["A_re"][l], w["A_im"][l], w["log_dt"][l], w["B_re"][l], w["B_im"][l])
    bbt_re = _block_diag(t(bb_re)).astype(BF16)
    bbt_im = _block_diag(t(bb_im)).astype(BF16)
    ct_re = _block_diag(t(w["C_re"][l])).astype(BF16)
    ct_im = _block_diag(t(w["C_im"][l])).astype(BF16)
    ws = jnp.where(_tril()[None], w["w_spatial"][l], 0.0)
    pair = lambda m: jnp.stack([jnp.concatenate([m[2 * q], m[2 * q + 1]], axis=1)
                                for q in range(SGU_HEADS // 2)]).astype(BF16)
    wp = _block_diag(w["w_pool"][l]).astype(BF16)
    p = dict(
        g_mix=row(w["g_mix"][l]), g_ffn=row(w["g_ffn"][l]), d_skip=row(w["D_skip"][l]), b_glu=row(w["b_glu"][l]),
        pool_scale=row(w["pool_scale"][l]), sgu_ln_g=row(w["sgu_ln_g"][l]), sgu_ln_b=row(w["sgu_ln_b"][l]),
        a_re8=_state_slabs(ar), a_im8=_state_slabs(ai),
        bbt_re=bbt_re, bbt_im=bbt_im, bb_re=t(bbt_re), bb_im=t(bbt_im),
        ct_re=ct_re, ct_im=ct_im, cb_re=t(ct_re), cb_im=t(ct_im),
        w_pool_bd=wp, w_pool_bd_t=t(wp), ws_pair=pair(ws), ws_pair_t=pair(t(ws)),
        bias_sp=jnp.repeat(t(w["b_spatial"][l]), SGU_HEAD_DIM, axis=1),
    )
    for n in BIG:
        p[n] = big[n][l]
        p[n + "_t"] = t(big[n][l])
    return p


def _layer_fwd(x0, p, tag):
    za, zb, zuv, h1 = _mix_in_fwd(x0, p["g_mix"], p["w_in"], tag)
    oa, y, h_re, h_im = _s5_fwd(za, p, tag)
    ob, pooled = _pool_fwd(zb, p, tag)
    oc = _sgu_fwd(zuv, p, tag)
    x1, x2, h2, gt, up = _blk_fwd(x0, oa, ob, oc, p, tag)
    saved = dict(x0=x0, za=za, zuv=zuv, h1=h1, oa=oa, ob=ob, oc=oc, y=y, h_re=h_re, h_im=h_im, pooled=pooled, x1=x1,
                 h2=h2, gt=gt, up=up)
    return x2, saved


def _rows_sum(v):
    return jnp.sum(v, axis=0)


def _layer_bwd(dx2, sv, p, w, l, tag):
    t = lambda m: jnp.swapaxes(m, -1, -2)
    dx1, da, db, dc, dgt, dup, act, dg_ffn = _blk_bwd(dx2, sv["x1"], sv["gt"], sv["up"], p, tag)
    g = {}
    g["w_down"] = _atb(act, dx2, D_MODEL, tag + "_wd")
    g["w_gate"] = _atb(sv["h2"], dgt, D_FF // 2, tag + "_wg")
    g["w_up"] = _atb(sv["h2"], dup, D_FF // 2, tag + "_wu")
    g["w_out"] = jnp.concatenate([_atb(sv["oa"], dx1, D_MODEL, tag + "_woa"), _atb(sv["ob"], dx1, D_MODEL, tag + "_wob"),
                                  _atb(sv["oc"], dx1, D_MODEL, tag + "_woc")], axis=0)
    g["g_ffn"] = _rows_sum(dg_ffn)
    dza, dct_re, dct_im, dbbt_re, dbbt_im, dar8, dai8, dd8, dwglu, dbglu8 = _s5_bwd(
        da, sv["za"], sv["y"], sv["h_re"], sv["h_im"], p, tag)
    dzb, dwp, dsc8 = _pool_bwd(db, sv["pooled"], p, tag)
    dzuv, dws, dbias, dlng8, dlnb8 = _sgu_bwd(dc, sv["zuv"], p, tag)
    dx0, dz, dg_mix = _mix_in_bwd(dza, dzb, dzuv, sv["x0"], dx1, p, tag)
    g["w_in"] = _atb(sv["h1"], dz, D_IN, tag + "_wi")
    g["g_mix"] = _rows_sum(dg_mix)
    g["w_glu"] = dwglu
    g["b_glu"] = _rows_sum(dbglu8)
    g["D_skip"] = _rows_sum(dd8)
    g["C_re"] = t(_block_diag_extract(dct_re, N_GROUPS))
    g["C_im"] = t(_block_diag_extract(dct_im, N_GROUPS))
    dar = jnp.sum(dar8, axis=1).reshape(N_GROUPS, SSM_STATE)
    dai = jnp.sum(dai8, axis=1).reshape(N_GROUPS, SSM_STATE)
    dbb_re = t(_block_diag_extract(dbbt_re, N_GROUPS))
    dbb_im = t(_block_diag_extract(dbbt_im, N_GROUPS))
    _, disc_vjp = jax.vjp(_s5_discretise, w["A_re"][l], w["A_im"][l], w["log_dt"][l], w["B_re"][l], w["B_im"][l])
    g["A_re"], g["A_im"], g["log_dt"], g["B_re"], g["B_im"] = disc_vjp((dar, dai, dbb_re, dbb_im))
    g["w_pool"] = _block_diag_extract(dwp, len(POOL_WINDOWS))
    g["pool_scale"] = _rows_sum(dsc8)
    g["sgu_ln_g"] = _rows_sum(dlng8)
    g["sgu_ln_b"] = _rows_sum(dlnb8)
    g["w_spatial"] = jnp.where(_tril()[None], dws, 0.0)
    g["b_spatial"] = t(jnp.sum(dbias.reshape(CHUNK, SGU_HEADS, SGU_HEAD_DIM), axis=-1))
    return dx0, g


def _local_step(x, target, w, big):
    params = [_layer_params(w, big, l) for l in range(DEPTH)]
    saved = []
    h = x
    for l in range(DEPTH):
        h, sv = _layer_fwd(h, params[l], f"l{l}")
        saved.append(sv)
    dx, loss8, dgf8 = _head(h, target, w["g_final"].reshape(1, -1))
    grads = [None] * DEPTH
    for l in reversed(range(DEPTH)):
        dx, grads[l] = _layer_bwd(dx, saved[l], params[l], w, l, f"l{l}")
    g = {n: jnp.stack([grads[l][n] for l in range(DEPTH)]) for n in WEIGHTS if n != "g_final"}
    g["g_final"] = _rows_sum(dgf8)
    return loss8[0, 0], dx, g


_ANY = pl.BlockSpec(memory_space=pl.ANY)
_MESH = pl.DeviceIdType.MESH


def _place():
    return lax.axis_index("x"), lax.axis_index("y"), lax.axis_index("c")


def _other_chips(x, y):
    return [(1 - x, y), (x, 1 - y), (1 - x, 1 - y)]


def _allgather8(blk, tag):
    halves = blk.ndim == 3
    m, n = blk.shape[-2:]

    def body(x_ref, out_ref, send_sems, recv_sems, local_sem):
        x, y, c = _place()
        me, sibling = (x, y, c), (x, y, 1 - c)
        chips = _other_chips(x, y)
        mine_src = x_ref.at[c] if halves else x_ref

        def slot(px, py, pc):
            return out_ref.at[4 * px + 2 * py + pc]

        def copy(k, block, to, src=None):
            return pltpu.make_async_remote_copy(
                src_ref=slot(*block) if src is None else src, dst_ref=slot(*block),
                send_sem=send_sems.at[k], recv_sem=recv_sems.at[k], device_id=to, device_id_type=_MESH)

        mine = pltpu.make_async_copy(mine_src, slot(*me), local_sem)
        mine.start()
        first = [copy(0, me, sibling, src=mine_src)]
        first += [copy(1 + j, me, (*chip, c), src=mine_src) for j, chip in enumerate(chips)]
        for cp in first:
            cp.start()
        passed = [copy(4 + j, (*chip, c), sibling) for j, chip in enumerate(chips)]
        for j, chip in enumerate(chips):
            copy(1 + j, (*chip, c), me).wait_recv()
            passed[j].start()
        copy(0, sibling, me).wait_recv()
        for j, chip in enumerate(chips):
            copy(4 + j, (*chip, 1 - c), me).wait_recv()
        for cp in first + passed:
            cp.wait_send()
        mine.wait()

    return pl.pallas_call(
        body, out_shape=jax.ShapeDtypeStruct((N_DEV, m, n), blk.dtype), in_specs=[_ANY], out_specs=_ANY,
        scratch_shapes=[pltpu.SemaphoreType.DMA((7,)), pltpu.SemaphoreType.DMA((7,)), pltpu.SemaphoreType.DMA],
        name=f"allgather8_{tag}",
    )(blk)


def _sibling_swap_halves(f2, tag):
    def body(f_ref, recv_ref, send_sem, recv_sem):
        x, y, c = _place()
        cp = pltpu.make_async_remote_copy(src_ref=f_ref.at[1 - c], dst_ref=recv_ref, send_sem=send_sem,
                                          recv_sem=recv_sem, device_id=(x, y, 1 - c), device_id_type=_MESH)
        cp.start()
        cp.wait()

    return pl.pallas_call(
        body, out_shape=jax.ShapeDtypeStruct(f2.shape[1:], f2.dtype), in_specs=[_ANY], out_specs=_ANY,
        scratch_shapes=[pltpu.SemaphoreType.DMA, pltpu.SemaphoreType.DMA], name=f"sibling_swap_{tag}",
    )(f2)


def _chip_exchange(p4, tag):
    m, n = p4.shape[1:]

    def body(p_ref, recv_ref, send_sems, recv_sems):
        x, y, c = _place()
        copies = [pltpu.make_async_remote_copy(
            src_ref=p_ref.at[2 * cx + cy], dst_ref=recv_ref.at[j], send_sem=send_sems.at[j],
            recv_sem=recv_sems.at[j], device_id=(cx, cy, c), device_id_type=_MESH)
            for j, (cx, cy) in enumerate(_other_chips(x, y))]
        for cp in copies:
            cp.start()
        for cp in copies:
            cp.wait()

    return pl.pallas_call(
        body, out_shape=jax.ShapeDtypeStruct((N_CHIPS - 1, m, n), p4.dtype), in_specs=[_ANY], out_specs=_ANY,
        scratch_shapes=[pltpu.SemaphoreType.DMA((N_CHIPS - 1,)), pltpu.SemaphoreType.DMA((N_CHIPS - 1,))],
        name=f"chip_exchange_{tag}",
    )(p4)


def _sibling_gather(q, tag):
    m, n = q.shape

    def body(q_ref, out_ref, send_sem, recv_sem, local_sem):
        x, y, c = _place()
        mine = pltpu.make_async_copy(q_ref, out_ref.at[c], local_sem)
        mine.start()
        cp = pltpu.make_async_remote_copy(src_ref=q_ref, dst_ref=out_ref.at[c], send_sem=send_sem,
                                          recv_sem=recv_sem, device_id=(x, y, 1 - c), device_id_type=_MESH)
        cp.start()
        cp.wait_send()
        pltpu.make_async_remote_copy(src_ref=q_ref, dst_ref=out_ref.at[1 - c], send_sem=send_sem,
                                     recv_sem=recv_sem, device_id=(x, y, 1 - c), device_id_type=_MESH).wait_recv()
        mine.wait()

    return pl.pallas_call(
        body, out_shape=jax.ShapeDtypeStruct((2, m, n), q.dtype), in_specs=[_ANY], out_specs=_ANY,
        scratch_shapes=[pltpu.SemaphoreType.DMA, pltpu.SemaphoreType.DMA, pltpu.SemaphoreType.DMA],
        name=f"sibling_gather_{tag}",
    )(q)


TR = 1024


def _add_halves(f2, recv, sel):
    _, k, r, n = f2.shape
    tr = math.gcd(r, TR)

    def body(sel_ref, a_ref, b_ref, o_ref):
        o_ref[...] = (a_ref[...].astype(F32) + b_ref[...].astype(F32)).astype(o_ref.dtype)

    return pl.pallas_call(
        body, grid_spec=pltpu.PrefetchScalarGridSpec(
            num_scalar_prefetch=1, grid=(k, r // tr),
            in_specs=[pl.BlockSpec((None, None, tr, n), lambda i, j, s: (s[0], i, j, 0)),
                      pl.BlockSpec((None, tr, n), lambda i, j, s: (i, j, 0))],
            out_specs=pl.BlockSpec((None, tr, n), lambda i, j, s: (i, j, 0))),
        out_shape=jax.ShapeDtypeStruct((k, r, n), f2.dtype), name="add_halves",
        compiler_params=_cp(dimension_semantics=("arbitrary", "arbitrary")),
    )(sel, f2, recv)


def _add_chips(p4, recv, sel):
    _, r, n = p4.shape
    tr = math.gcd(r, TR)

    def body(sel_ref, a_ref, b_ref, o_ref):
        acc = a_ref[...].astype(F32)
        for j in range(N_CHIPS - 1):
            acc = acc + b_ref[j].astype(F32)
        o_ref[...] = acc

    return pl.pallas_call(
        body, grid_spec=pltpu.PrefetchScalarGridSpec(
            num_scalar_prefetch=1, grid=(r // tr,),
            in_specs=[pl.BlockSpec((None, tr, n), lambda j, s: (s[1], j, 0)),
                      pl.BlockSpec((N_CHIPS - 1, tr, n), lambda j, s: (0, j, 0))],
            out_specs=pl.BlockSpec((tr, n), lambda j, s: (j, 0))),
        out_shape=jax.ShapeDtypeStruct((r, n), F32), name="add_chips",
        compiler_params=_cp(dimension_semantics=("arbitrary",)),
    )(sel, p4, recv)


def _adamw_math(w, g, m, v):
    m = ADAM_B1 * m + (1.0 - ADAM_B1) * g
    v = ADAM_B2 * v + (1.0 - ADAM_B2) * (g * g)
    m_hat = m / (1.0 - ADAM_B1 ** ADAM_STEP)
    v_hat = v / (1.0 - ADAM_B2 ** ADAM_STEP)
    delta = -ADAM_LR * (m_hat / (jnp.sqrt(v_hat) + ADAM_EPS) + ADAM_WD * w)
    return delta, m, v


ADAM_ROWS = 64


def _adamw(w, g, m, v, tag):
    shape = w.shape
    cols = shape[-1]
    flat = lambda a: a.reshape(-1, cols)
    rows = w.size // cols
    tr = ADAM_ROWS

    def body(w_ref, g_ref, m_ref, v_ref, d_ref, nm_ref, nv_ref):
        d, nm, nv = _adamw_math(w_ref[...], g_ref[...], m_ref[...], v_ref[...])
        d_ref[...] = d
        nm_ref[...] = nm
        nv_ref[...] = nv

    spec = _row(tr, cols)
    out = pl.pallas_call(
        body, grid=(rows // tr,), in_specs=[spec] * 4, out_specs=[spec] * 3,
        out_shape=[jax.ShapeDtypeStruct((rows, cols), F32)] * 3, name=f"adamw_{tag}",
        compiler_params=_cp(dimension_semantics=("arbitrary",)),
    )(flat(w), flat(g), flat(m), flat(v))
    return [o.reshape(shape) for o in out]


SMALL_ROWS = 3584
SMALL_TILE = 512


def _small_reduce_adamw(gathered, w, m, v):
    tr = SMALL_TILE

    def body(ga_ref, w_ref, m_ref, v_ref, g_ref, d_ref, nm_ref, nv_ref):
        g = ga_ref[0]
        for k in range(1, N_DEV):
            g = g + ga_ref[k]
        g_ref[...] = g
        d, nm, nv = _adamw_math(w_ref[...], g, m_ref[...], v_ref[...])
        d_ref[...] = d
        nm_ref[...] = nm
        nv_ref[...] = nv

    spec = _row(tr, LANES)
    return pl.pallas_call(
        body, grid=(SMALL_ROWS // tr,),
        in_specs=[pl.BlockSpec((N_DEV, tr, LANES), lambda i: (0, i, 0)), spec, spec, spec], out_specs=[spec] * 4,
        out_shape=[jax.ShapeDtypeStruct((SMALL_ROWS, LANES), F32)] * 4, name="small_reduce_adamw",
        compiler_params=_cp(dimension_semantics=("arbitrary",)),
    )(gathered, w, m, v)


_SHARD_AXIS = {"w_in": 2, "w_glu": 1, "w_out": 1, "w_gate": 2, "w_up": 2, "w_down": 1}


def _to_chip_major(full, axis):
    d, r, c = full.shape
    if axis == 1:
        return jnp.transpose(full.reshape(d, N_CHIPS, r // N_CHIPS, c), (1, 0, 2, 3))
    return jnp.transpose(full.reshape(d, r, N_CHIPS, c // N_CHIPS), (2, 0, 1, 3))


def _from_chip_major(parts, axis):
    _, d, r, c = parts.shape
    if axis == 1:
        return jnp.transpose(parts, (1, 0, 2, 3)).reshape(d, N_CHIPS * r, c)
    return jnp.transpose(parts, (1, 2, 0, 3)).reshape(d, r, N_CHIPS * c)


def _pack_small(vals, loss):
    flat = jnp.concatenate([vals[n].reshape(-1) for n in SMALL] + [loss.reshape(1)])
    return jnp.pad(flat, (0, SMALL_ROWS * LANES - flat.size)).reshape(SMALL_ROWS, LANES)


def _unpack_small(buf, like):
    flat = buf.reshape(-1)
    out, off = {}, 0
    for n in SMALL:
        out[n] = flat[off:off + like[n].size].reshape(like[n].shape)
        off += like[n].size
    return out, flat[off]


def kernel(x, g_mix, w_in, A_re, A_im, log_dt, B_re, B_im, C_re, C_im, D_skip, w_glu, b_glu, w_pool, pool_scale, sgu_ln_g, sgu_ln_b, w_spatial, b_spatial, w_out, g_ffn, w_gate, w_up, w_down, g_final, loss_target, m_g_mix, m_w_in, m_A_re, m_A_im, m_log_dt, m_B_re, m_B_im, m_C_re, m_C_im, m_D_skip, m_w_glu, m_b_glu, m_w_pool, m_pool_scale, m_sgu_ln_g, m_sgu_ln_b, m_w_spatial, m_b_spatial, m_w_out, m_g_ffn, m_w_gate, m_w_up, m_w_down, m_g_final, v_g_mix, v_w_in, v_A_re, v_A_im, v_log_dt, v_B_re, v_B_im, v_C_re, v_C_im, v_D_skip, v_w_glu, v_b_glu, v_w_pool, v_pool_scale, v_sgu_ln_g, v_sgu_ln_b, v_w_spatial, v_b_spatial, v_w_out, v_g_ffn, v_w_gate, v_w_up, v_w_down, v_g_final):
    loc = locals()
    w = {n: loc[n] for n in WEIGHTS}
    m = {n: loc["m_" + n] for n in WEIGHTS}
    v = {n: loc["v_" + n] for n in WEIGHTS}
    sel = jnp.stack([lax.axis_index("c"), 2 * lax.axis_index("x") + lax.axis_index("y")]).astype(jnp.int32)

    sizes = {n: w[n].size for n in BIG}
    shard_len = sum(sizes.values())
    half_rows = shard_len // (2 * LANES)
    mine = jnp.concatenate([w[n].astype(BF16).reshape(-1) for n in BIG]).reshape(2, half_rows, LANES)
    gathered = _allgather8(mine, "weights").reshape(N_CHIPS, shard_len)
    big, off = {}, 0
    for n in BIG:
        big[n] = _from_chip_major(gathered[:, off:off + sizes[n]].reshape((N_CHIPS,) + w[n].shape), _SHARD_AXIS[n])
        off += sizes[n]

    loss_local, dx, g = _local_step(x[0], loss_target[0], w, big)

    packed = jnp.concatenate([_to_chip_major(g[n], _SHARD_AXIS[n]).reshape(N_CHIPS, -1) for n in BIG], axis=1)
    f2 = jnp.swapaxes(packed.astype(BF16).reshape(N_CHIPS, 2, half_rows, LANES), 0, 1)
    p4 = _add_halves(f2, _sibling_swap_halves(f2, "grads"), sel)
    q = _add_chips(p4, _chip_exchange(p4, "grads"), sel)
    gshard = _sibling_gather(q, "grads").reshape(shard_len)
    grads, deltas, new_m, new_v = {}, {}, {}, {}
    off = 0
    for n in BIG:
        grads[n] = gshard[off:off + sizes[n]].reshape(w[n].shape)
        off += sizes[n]
        deltas[n], new_m[n], new_v[n] = _adamw(w[n], grads[n], m[n], v[n], n)

    small_all = _allgather8(_pack_small(g, loss_local), "small")
    zero = jnp.zeros((), F32)
    gs, ds, ms, vs = _small_reduce_adamw(small_all, _pack_small(w, zero), _pack_small(m, zero), _pack_small(v, zero))
    gsm, loss = _unpack_small(gs, w)
    grads.update(gsm)
    deltas.update(_unpack_small(ds, w)[0])
    new_m.update(_unpack_small(ms, w)[0])
    new_v.update(_unpack_small(vs, w)[0])
    return (loss, dx[None], *[grads[n] for n in WEIGHTS], *[deltas[n] for n in WEIGHTS],
            *[new_m[n] for n in WEIGHTS], *[new_v[n] for n in WEIGHTS])
```

```python
import math

import jax
import jax.numpy as jnp
from jax import lax
from jax.experimental import pallas as pl
from jax.experimental.pallas import tpu as pltpu

F32 = jnp.float32
BF16 = jnp.bfloat16

D_MODEL = 1024
DEPTH = 2
D_SSM = 384
SSM_GROUP = 16
N_GROUPS = 24
SSM_STATE = 64
N_STATE = N_GROUPS * SSM_STATE
POOL_WINDOWS = (2, 4, 8, 16)
POOL_GROUP = 64
D_POOL = 256
MAX_WINDOW = 16
SGU_HEADS = 6
SGU_HEAD_DIM = 64
D_SGU = 384
CHUNK = 128
D_IN = D_SSM + D_POOL + 2 * D_SGU
D_FF = 2816
EPS = 1e-6

ADAM_LR = 0.001
ADAM_B1 = 0.9
ADAM_B2 = 0.999
ADAM_EPS = 1e-08
ADAM_WD = 0.01
ADAM_STEP = 10

LANES = 128
SUBLANES = 8
N_SLAB = N_STATE // LANES
VMEM_LIMIT = 56 * 1024 * 1024

TS = 512
TS_FFN = 256

WEIGHTS = ['g_mix', 'w_in', 'A_re', 'A_im', 'log_dt', 'B_re', 'B_im', 'C_re', 'C_im', 'D_skip', 'w_glu', 'b_glu',
           'w_pool', 'pool_scale', 'sgu_ln_g', 'sgu_ln_b', 'w_spatial', 'b_spatial', 'w_out', 'g_ffn', 'w_gate',
           'w_up', 'w_down', 'g_final']
BIG = ['w_in', 'w_glu', 'w_out', 'w_gate', 'w_up', 'w_down']
SMALL = [n for n in WEIGHTS if n not in BIG]
TRANSPOSED = ("w_in", "w_gate", "w_up")
N_CHIPS = 4
N_DEV = 8


def _cp(**kw):
    return pltpu.CompilerParams(vmem_limit_bytes=VMEM_LIMIT, **kw)


def _row(ts, n):
    return pl.BlockSpec((ts, n), lambda i: (i, 0))


def _const(shape):
    nd = len(shape)
    return pl.BlockSpec(shape, lambda i: (0,) * nd, pipeline_mode=pl.Buffered(1))


def _acc(shape):
    nd = len(shape)
    return pl.BlockSpec(shape, lambda i: (0,) * nd)


def _dot(a, b):
    return jnp.dot(a, b, preferred_element_type=F32)


def _dot_tn(a, b):
    return lax.dot_general(a, b, (((0,), (0,)), ((), ())), preferred_element_type=F32)


def _dot_nt(a, b):
    return lax.dot_general(a, b, (((1,), (1,)), ((), ())), preferred_element_type=F32)


_G0 = math.sqrt(2.0 / math.pi)
_G1 = 0.044715


def _gelu(x):
    return 0.5 * x * (1.0 + jnp.tanh(_G0 * (x + _G1 * x * x * x)))


def _gelu_grad(x):
    t = jnp.tanh(_G0 * (x + _G1 * x * x * x))
    return 0.5 * (1.0 + t) + 0.5 * x * (1.0 - t * t) * (_G0 * (1.0 + 3.0 * _G1 * x * x))


def _sigmoid(x):
    return 1.0 / (1.0 + jnp.exp(-x))


def _rms(x):
    r = lax.rsqrt(jnp.mean(x * x, axis=-1, keepdims=True) + EPS)
    return x * r, r


def _rms_bwd(dh, n, r, g):
    dn = dh * g
    return r * (dn - n * jnp.mean(dn * n, axis=-1, keepdims=True)), dh * n


def _colsum8(v):
    rows, n = v.shape
    return jnp.sum(v.reshape(rows // SUBLANES, SUBLANES, n), axis=0)


def _mix_in_fwd(x, g, w, tag):
    s = x.shape[0]

    def body(x_ref, g_ref, w_ref, za_ref, zb_ref, zuv_ref, h_ref):
        n, _ = _rms(x_ref[...])
        h = (n * g_ref[...]).astype(BF16)
        z = _dot(h, w_ref[...])
        za_ref[...] = z[:, :D_SSM]
        zb_ref[...] = z[:, D_SSM:D_SSM + D_POOL]
        zuv_ref[...] = z[:, D_SSM + D_POOL:]
        h_ref[...] = h

    return pl.pallas_call(
        body, grid=(s // TS,),
        in_specs=[_row(TS, D_MODEL), _const((1, D_MODEL)), _const((D_MODEL, D_IN))],
        out_specs=[_row(TS, D_SSM), _row(TS, D_POOL), _row(TS, 2 * D_SGU), _row(TS, D_MODEL)],
        out_shape=[jax.ShapeDtypeStruct((s, D_SSM), F32), jax.ShapeDtypeStruct((s, D_POOL), F32),
                   jax.ShapeDtypeStruct((s, 2 * D_SGU), F32), jax.ShapeDtypeStruct((s, D_MODEL), BF16)],
        name=f"mix_in_fwd_{tag}", compiler_params=_cp(dimension_semantics=("arbitrary",)),
    )(x, g, w)


def _cmul(ar, ai, br, bi):
    return ar * br - ai * bi, ar * bi + ai * br


def _cpow(ar, ai, n):
    assert n & (n - 1) == 0
    while n > 1:
        ar, ai = _cmul(ar, ai, ar, ai)
        n //= 2
    return ar, ai


def _to_slabs(ref, v):
    for j in range(N_SLAB):
        ref[j] = v[:, LANES * j:LANES * (j + 1)]


def _from_slabs(ref):
    return jnp.concatenate([ref[j] for j in range(N_SLAB)], axis=1)


N_USLAB = D_SSM // LANES
SEG = TS // SUBLANES


def _interleave_rows(v, stage, dst):
    for j in range(N_USLAB):
        stage[j] = v[:, LANES * j:LANES * (j + 1)]

    def step(k, carry):
        dst[:, pl.ds(pl.multiple_of(k * SUBLANES, SUBLANES), SUBLANES), :] = stage[:, pl.ds(k, SUBLANES, stride=SEG), :]
        return carry

    lax.fori_loop(0, SEG, step, 0)
    return jnp.concatenate([dst[j] for j in range(N_USLAB)], axis=1)


def _deinterleave_rows(v, stage, dst):
    for j in range(N_USLAB):
        stage[j] = v[:, LANES * j:LANES * (j + 1)]

    def step(k, carry):
        dst[:, pl.ds(k, SUBLANES, stride=SEG), :] = stage[:, pl.ds(pl.multiple_of(k * SUBLANES, SUBLANES), SUBLANES), :]
        return carry

    lax.fori_loop(0, SEG, step, 0)
    return jnp.concatenate([dst[j] for j in range(N_USLAB)], axis=1)


def _scan_rows(k):
    return pl.ds(pl.multiple_of(k * SUBLANES, SUBLANES), SUBLANES)


def _s5_fwd(u, p, tag):
    s = u.shape[0]
    seg = SEG

    def body(u_ref, bbr_ref, bbi_ref, ar_ref, ai_ref, ctr_ref, cti_ref, dsk_ref, wglu_ref, bglu_ref,
             oa_ref, y_ref, hr_ref, hi_ref, sr, si, er, ei, ir, ii, cr, ci, stage, perm):
        @pl.when(pl.program_id(0) == 0)
        def _():
            cr[...] = jnp.zeros_like(cr)
            ci[...] = jnp.zeros_like(ci)

        uv = _interleave_rows(u_ref[...], stage, perm)
        ub = uv.astype(BF16)
        _to_slabs(sr, _dot(ub, bbr_ref[...]))
        _to_slabs(si, _dot(ub, bbi_ref[...]))
        ar = ar_ref[...]
        ai = ai_ref[...]

        def local(k, h):
            rows = _scan_rows(k)
            hr, hi = _cmul(ar, ai, h[0], h[1])
            return hr + sr[:, rows, :], hi + si[:, rows, :]

        zero = jnp.zeros((N_SLAB, SUBLANES, LANES), F32)
        e_r, e_i = lax.fori_loop(0, seg, local, (zero, zero))
        er[...] = e_r
        ei[...] = e_i
        pr, pi = _cpow(ar[:, 0:1, :], ai[:, 0:1, :], seg)
        c_r = cr[...]
        c_i = ci[...]
        for j in range(SUBLANES):
            ir[:, j:j + 1, :] = c_r
            ii[:, j:j + 1, :] = c_i
            n_r, n_i = _cmul(pr, pi, c_r, c_i)
            c_r = n_r + er[:, j:j + 1, :]
            c_i = n_i + ei[:, j:j + 1, :]
        cr[...] = c_r
        ci[...] = c_i

        def full(k, h):
            rows = _scan_rows(k)
            hr, hi = _cmul(ar, ai, h[0], h[1])
            hr = hr + sr[:, rows, :]
            hi = hi + si[:, rows, :]
            sr[:, rows, :] = hr
            si[:, rows, :] = hi
            return hr, hi

        lax.fori_loop(0, seg, full, (ir[...], ii[...]))
        hr = _from_slabs(sr)
        hi = _from_slabs(si)
        hr_ref[...] = hr
        hi_ref[...] = hi
        y = _dot(hr.astype(BF16), ctr_ref[...]) - _dot(hi.astype(BF16), cti_ref[...]) + dsk_ref[...] * uv
        y_ref[...] = y
        g = _gelu(y)
        pre = _dot(g.astype(BF16), wglu_ref[...]) + bglu_ref[...]
        oa_ref[...] = _deinterleave_rows(g * _sigmoid(pre), stage, perm).astype(BF16)

    slab = (N_SLAB, SUBLANES, LANES)
    uslab = (N_USLAB, TS, LANES)
    return pl.pallas_call(
        body, grid=(s // TS,),
        in_specs=[_row(TS, D_SSM), _const((D_SSM, N_STATE)), _const((D_SSM, N_STATE)), _const(slab), _const(slab),
                  _const((N_STATE, D_SSM)), _const((N_STATE, D_SSM)), _const((1, D_SSM)), _const((D_SSM, D_SSM)),
                  _const((1, D_SSM))],
        out_specs=[_row(TS, D_SSM), _row(TS, D_SSM), _row(TS, N_STATE), _row(TS, N_STATE)],
        out_shape=[jax.ShapeDtypeStruct((s, D_SSM), BF16), jax.ShapeDtypeStruct((s, D_SSM), F32),
                   jax.ShapeDtypeStruct((s, N_STATE), F32), jax.ShapeDtypeStruct((s, N_STATE), F32)],
        scratch_shapes=[pltpu.VMEM((N_SLAB, TS, LANES), F32), pltpu.VMEM((N_SLAB, TS, LANES), F32),
                        pltpu.VMEM(slab, F32), pltpu.VMEM(slab, F32), pltpu.VMEM(slab, F32), pltpu.VMEM(slab, F32),
                        pltpu.VMEM((N_SLAB, 1, LANES), F32), pltpu.VMEM((N_SLAB, 1, LANES), F32),
                        pltpu.VMEM(uslab, F32), pltpu.VMEM(uslab, F32)],
        name=f"s5_fwd_{tag}", compiler_params=_cp(dimension_semantics=("arbitrary",)),
    )(u, p["bbt_re"], p["bbt_im"], p["a_re8"], p["a_im8"], p["ct_re"], p["ct_im"], p["d_skip"], p["w_glu"],
      p["b_glu"])


def _pool_consts():
    w = jnp.repeat(jnp.asarray(POOL_WINDOWS, F32), POOL_GROUP)[None, :]
    return w


def _window_sum(buf, first, rows, wl, step):
    acc = buf[pl.ds(first, rows), :]
    for j in range(1, MAX_WINDOW):
        term = buf[pl.ds(first + step * j, rows), :]
        acc = acc + (term if j < min(POOL_WINDOWS) else term * (wl > j).astype(F32))
    return acc


def _pool_count(i, rows, wl, offset=0):
    t = (i * TS + offset + 1).astype(F32) + lax.broadcasted_iota(jnp.int32, (rows, 1), 0).astype(F32)
    return jnp.minimum(t, wl)


def _pool_fwd(zb, p, tag):
    s = zb.shape[0]
    hb = TS // MAX_WINDOW

    def body(u_ref, halo_ref, wl_ref, w_ref, sc_ref, ob_ref, pooled_ref, buf):
        i = pl.program_id(0)
        uv = u_ref[...]
        buf[pl.ds(0, MAX_WINDOW), :] = jnp.where(i > 0, halo_ref[...], 0.0)
        buf[pl.ds(MAX_WINDOW, TS), :] = uv
        wl = wl_ref[...]
        pooled = (_window_sum(buf, MAX_WINDOW, TS, wl, -1) / _pool_count(i, TS, wl) - uv).astype(BF16)
        pooled_ref[...] = pooled
        ob_ref[...] = (_dot(pooled, w_ref[...]) * sc_ref[...]).astype(BF16)

    return pl.pallas_call(
        body, grid=(s // TS,),
        in_specs=[_row(TS, D_POOL),
                  pl.BlockSpec((MAX_WINDOW, D_POOL), lambda i: (jnp.maximum(i * hb - 1, 0), 0)),
                  _const((1, D_POOL)), _const((D_POOL, D_POOL)), _const((1, D_POOL))],
        out_specs=[_row(TS, D_POOL), _row(TS, D_POOL)],
        out_shape=[jax.ShapeDtypeStruct((s, D_POOL), BF16), jax.ShapeDtypeStruct((s, D_POOL), BF16)],
        scratch_shapes=[pltpu.VMEM((TS + MAX_WINDOW, D_POOL), F32)],
        name=f"pool_fwd_{tag}", compiler_params=_cp(dimension_semantics=("arbitrary",)),
    )(zb, zb, _pool_consts(), p["w_pool_bd"], p["pool_scale"])


def _sgu_mix(vl, wpair_ref, lo, hi):
    rows = vl.shape[0]
    chunks = []
    for c in range(rows // CHUNK):
        vc = vl[CHUNK * c:CHUNK * (c + 1), :]
        parts = []
        for q in range(SGU_HEADS // 2):
            vq = vc[:, LANES * q:LANES * (q + 1)]
            rhs = jnp.concatenate([vq * lo, vq * hi], axis=0).astype(BF16)
            parts.append(_dot(wpair_ref[q], rhs))
        chunks.append(jnp.concatenate(parts, axis=1))
    return jnp.concatenate(chunks, axis=0)


def _sgu_front(zuv, lng, lnb):
    zu = zuv[:, :D_SGU]
    zv = zuv[:, D_SGU:]
    u = _gelu(zu)
    v = _gelu(zv)
    mu = jnp.mean(v, axis=-1, keepdims=True)
    vc = v - mu
    rs = lax.rsqrt(jnp.mean(vc * vc, axis=-1, keepdims=True) + EPS)
    vn = vc * rs
    return zu, zv, u, vn, rs, vn * lng + lnb


def _half_masks():
    lane = lax.broadcasted_iota(jnp.int32, (1, LANES), 1)
    lo = (lane < SGU_HEAD_DIM).astype(F32)
    return lo, 1.0 - lo


def _sgu_fwd(zuv, p, tag):
    s = zuv.shape[0]

    def body(z_ref, lng_ref, lnb_ref, wp_ref, bias_ref, oc_ref):
        lo, hi = _half_masks()
        _, _, u, _, _, vl = _sgu_front(z_ref[...], lng_ref[...], lnb_ref[...])
        mixed = _sgu_mix(vl, wp_ref, lo, hi) + jnp.tile(bias_ref[...], (TS // CHUNK, 1))
        oc_ref[...] = (u * mixed).astype(BF16)

    return pl.pallas_call(
        body, grid=(s // TS,),
        in_specs=[_row(TS, 2 * D_SGU), _const((1, D_SGU)), _const((1, D_SGU)),
                  _const((SGU_HEADS // 2, CHUNK, 2 * CHUNK)), _const((CHUNK, D_SGU))],
        out_specs=_row(TS, D_SGU),
        out_shape=jax.ShapeDtypeStruct((s, D_SGU), BF16),
        name=f"sgu_fwd_{tag}", compiler_params=_cp(dimension_semantics=("arbitrary",)),
    )(zuv, p["sgu_ln_g"], p["sgu_ln_b"], p["ws_pair"], p["bias_sp"])


def _blk_fwd(x0, oa, ob, oc, p, tag):
    s = x0.shape[0]
    ts = TS_FFN

    def body(x0_ref, oa_ref, ob_ref, oc_ref, wo_ref, g_ref, wg_ref, wu_ref, wd_ref,
             x1_ref, x2_ref, h2_ref, gt_ref, up_ref, ycat_ref):
        ycat = jnp.concatenate([oa_ref[...], ob_ref[...], oc_ref[...]], axis=1)
        ycat_ref[...] = ycat
        x1 = x0_ref[...] + _dot(ycat, wo_ref[...])
        x1_ref[...] = x1
        n, _ = _rms(x1)
        h2 = (n * g_ref[...]).astype(BF16)
        h2_ref[...] = h2
        gt = _dot(h2, wg_ref[...])
        up = _dot(h2, wu_ref[...])
        gt_ref[...] = gt.astype(BF16)
        up_ref[...] = up.astype(BF16)
        act = (gt * _sigmoid(gt) * up).astype(BF16)
        x2_ref[...] = x1 + _dot(act, wd_ref[...])

    return pl.pallas_call(
        body, grid=(s // ts,),
        in_specs=[_row(ts, D_MODEL), _row(ts, D_SSM), _row(ts, D_POOL), _row(ts, D_SGU),
                  _const((D_MODEL, D_MODEL)), _const((1, D_MODEL)), _const((D_MODEL, D_FF)),
                  _const((D_MODEL, D_FF)), _const((D_FF, D_MODEL))],
        out_specs=[_row(ts, D_MODEL), _row(ts, D_MODEL), _row(ts, D_MODEL), _row(ts, D_FF), _row(ts, D_FF),
                   _row(ts, D_MODEL)],
        out_shape=[jax.ShapeDtypeStruct((s, D_MODEL), F32), jax.ShapeDtypeStruct((s, D_MODEL), F32),
                   jax.ShapeDtypeStruct((s, D_MODEL), BF16), jax.ShapeDtypeStruct((s, D_FF), BF16),
                   jax.ShapeDtypeStruct((s, D_FF), BF16), jax.ShapeDtypeStruct((s, D_MODEL), BF16)],
        name=f"blk_fwd_{tag}", compiler_params=_cp(dimension_semantics=("arbitrary",)),
    )(x0, oa, ob, oc, p["w_out"], p["g_ffn"], p["w_gate"], p["w_up"], p["w_down"])


def _blk_bwd(dx2, x1, gt, up, p, tag):
    s = dx2.shape[0]
    ts = TS_FFN

    def body(dx2_ref, x1_ref, gt_ref, up_ref, wdt_ref, wgt_ref, wut_ref, wot_ref, g_ref,
             dx1_ref, da_ref, db_ref, dc_ref, dgt_ref, dup_ref, act_ref, dg_ref):
        @pl.when(pl.program_id(0) == 0)
        def _():
            dg_ref[...] = jnp.zeros_like(dg_ref)

        dx2v = dx2_ref[...]
        dact = _dot(dx2v.astype(BF16), wdt_ref[...])
        gf = gt_ref[...].astype(F32)
        uf = up_ref[...].astype(F32)
        sg = _sigmoid(gf)
        sl = gf * sg
        act_ref[...] = (sl * uf).astype(BF16)
        dgt = (dact * uf * (sg * (1.0 + gf * (1.0 - sg)))).astype(BF16)
        dup = (dact * sl).astype(BF16)
        dgt_ref[...] = dgt
        dup_ref[...] = dup
        dh2 = _dot(dgt, wgt_ref[...]) + _dot(dup, wut_ref[...])
        n, r = _rms(x1_ref[...])
        dxn, dgp = _rms_bwd(dh2, n, r, g_ref[...])
        dg_ref[...] += _colsum8(dgp)
        dx1 = dx2v + dxn
        dx1_ref[...] = dx1
        dy = _dot(dx1.astype(BF16), wot_ref[...])
        da_ref[...] = dy[:, :D_SSM]
        db_ref[...] = dy[:, D_SSM:D_SSM + D_POOL]
        dc_ref[...] = dy[:, D_SSM + D_POOL:]

    return pl.pallas_call(
        body, grid=(s // ts,),
        in_specs=[_row(ts, D_MODEL), _row(ts, D_MODEL), _row(ts, D_FF), _row(ts, D_FF),
                  _const((D_MODEL, D_FF)), _const((D_FF, D_MODEL)), _const((D_FF, D_MODEL)),
                  _const((D_MODEL, D_MODEL)), _const((1, D_MODEL))],
        out_specs=[_row(ts, D_MODEL), _row(ts, D_SSM), _row(ts, D_POOL), _row(ts, D_SGU), _row(ts, D_FF),
                   _row(ts, D_FF), _row(ts, D_FF), _acc((SUBLANES, D_MODEL))],
        out_shape=[jax.ShapeDtypeStruct((s, D_MODEL), F32), jax.ShapeDtypeStruct((s, D_SSM), F32),
                   jax.ShapeDtypeStruct((s, D_POOL), F32), jax.ShapeDtypeStruct((s, D_SGU), F32),
                   jax.ShapeDtypeStruct((s, D_FF), BF16), jax.ShapeDtypeStruct((s, D_FF), BF16),
                   jax.ShapeDtypeStruct((s, D_FF), BF16), jax.ShapeDtypeStruct((SUBLANES, D_MODEL), F32)],
        name=f"blk_bwd_{tag}", compiler_params=_cp(dimension_semantics=("arbitrary",)),
    )(dx2, x1, gt, up, p["w_down_t"], p["w_gate_t"], p["w_up_t"], p["w_out_t"], p["g_ffn"])


def _s5_bwd(dout, u, y, h_re, h_im, p, tag):
    s = u.shape[0]
    nt = s // TS
    seg = SEG

    def rev(n):
        return pl.BlockSpec((TS, n), lambda i: (nt - 1 - i, 0))

    def body(do_ref, u_ref, y_ref, hr_ref, hi_ref, ar_ref, ai_ref, cbr_ref, cbi_ref, bbr_ref, bbi_ref, dsk_ref,
             wglu_ref, wglut_ref, bglu_ref,
             du_ref, dctr_ref, dcti_ref, dbbr_ref, dbbi_ref, dar_ref, dai_ref, dd_ref, dwglu_ref, dbglu_ref,
             gr, gi, hsr, hsi, er, ei, jr, ji, cr, ci, stage, perm):
        @pl.when(pl.program_id(0) == 0)
        def _():
            for ref in (cr, ci, dctr_ref, dcti_ref, dbbr_ref, dbbi_ref, dar_ref, dai_ref, dd_ref, dwglu_ref,
                        dbglu_ref):
                ref[...] = jnp.zeros_like(ref)

        uv = _interleave_rows(u_ref[...], stage, perm)
        yv = y_ref[...]
        dov = _interleave_rows(do_ref[...], stage, perm)
        g = _gelu(yv)
        gb = g.astype(BF16)
        sg = _sigmoid(_dot(gb, wglu_ref[...]) + bglu_ref[...])
        dpre = dov * g * sg * (1.0 - sg)
        dpb = dpre.astype(BF16)
        dwglu_ref[...] += _dot_tn(gb, dpb)
        dbglu_ref[...] += _colsum8(dpre)
        dy = (dov * sg + _dot(dpb, wglut_ref[...])) * _gelu_grad(yv)
        dd_ref[...] += _colsum8(dy * uv)
        dyb = dy.astype(BF16)
        hr = hr_ref[...]
        hi = hi_ref[...]
        dctr_ref[...] += _dot_tn(hr.astype(BF16), dyb)
        dcti_ref[...] -= _dot_tn(hi.astype(BF16), dyb)
        _to_slabs(hsr, hr)
        _to_slabs(hsi, hi)
        _to_slabs(gr, _dot(dyb, cbr_ref[...]))
        _to_slabs(gi, -_dot(dyb, cbi_ref[...]))
        ar = ar_ref[...]
        ai = -ai_ref[...]

        def local(k, h):
            rows = _scan_rows(seg - 1 - k)
            nr, ni = _cmul(ar, ai, h[0], h[1])
            return nr + gr[:, rows, :], ni + gi[:, rows, :]

        zero = jnp.zeros((N_SLAB, SUBLANES, LANES), F32)
        e_r, e_i = lax.fori_loop(0, seg, local, (zero, zero))
        er[...] = e_r
        ei[...] = e_i
        pr, pi = _cpow(ar[:, 0:1, :], ai[:, 0:1, :], seg)
        c_r = cr[...]
        c_i = ci[...]
        for j in range(SUBLANES - 1, -1, -1):
            jr[:, j:j + 1, :] = c_r
            ji[:, j:j + 1, :] = c_i
            n_r, n_i = _cmul(pr, pi, c_r, c_i)
            c_r = n_r + er[:, j:j + 1, :]
            c_i = n_i + ei[:, j:j + 1, :]
        cr[...] = c_r
        ci[...] = c_i

        def full(k, carry):
            g_r, g_i, a_r, a_i = carry
            rows = _scan_rows(seg - 1 - k)
            h_r = hsr[:, rows, :]
            h_i = hsi[:, rows, :]
            a_r = a_r + g_r * h_r + g_i * h_i
            a_i = a_i + g_i * h_r - g_r * h_i
            nr, ni = _cmul(ar, ai, g_r, g_i)
            nr = nr + gr[:, rows, :]
            ni = ni + gi[:, rows, :]
            gr[:, rows, :] = nr
            gi[:, rows, :] = ni
            return nr, ni, a_r, a_i

        _, _, a_r, a_i = lax.fori_loop(0, seg, full, (jr[...], ji[...], zero, zero))
        dar_ref[...] += a_r
        dai_ref[...] += a_i
        gbr = _from_slabs(gr).astype(BF16)
        gbi = _from_slabs(gi).astype(BF16)
        ub = uv.astype(BF16)
        dbbr_ref[...] += _dot_tn(ub, gbr)
        dbbi_ref[...] += _dot_tn(ub, gbi)
        du = dy * dsk_ref[...] + _dot(gbr, bbr_ref[...]) + _dot(gbi, bbi_ref[...])
        du_ref[...] = _deinterleave_rows(du, stage, perm)

    slab = (N_SLAB, SUBLANES, LANES)
    big = (N_SLAB, TS, LANES)
    uslab = (N_USLAB, TS, LANES)
    return pl.pallas_call(
        body, grid=(nt,),
        in_specs=[rev(D_SSM), rev(D_SSM), rev(D_SSM), rev(N_STATE), rev(N_STATE), _const(slab), _const(slab),
                  _const((D_SSM, N_STATE)), _const((D_SSM, N_STATE)), _const((N_STATE, D_SSM)),
                  _const((N_STATE, D_SSM)), _const((1, D_SSM)), _const((D_SSM, D_SSM)), _const((D_SSM, D_SSM)),
                  _const((1, D_SSM))],
        out_specs=[rev(D_SSM), _acc((N_STATE, D_SSM)), _acc((N_STATE, D_SSM)), _acc((D_SSM, N_STATE)),
                   _acc((D_SSM, N_STATE)), _acc(slab), _acc(slab), _acc((SUBLANES, D_SSM)), _acc((D_SSM, D_SSM)),
                   _acc((SUBLANES, D_SSM))],
        out_shape=[jax.ShapeDtypeStruct((s, D_SSM), F32), jax.ShapeDtypeStruct((N_STATE, D_SSM), F32),
                   jax.ShapeDtypeStruct((N_STATE, D_SSM), F32), jax.ShapeDtypeStruct((D_SSM, N_STATE), F32),
                   jax.ShapeDtypeStruct((D_SSM, N_STATE), F32), jax.ShapeDtypeStruct(slab, F32),
                   jax.ShapeDtypeStruct(slab, F32), jax.ShapeDtypeStruct((SUBLANES, D_SSM), F32),
                   jax.ShapeDtypeStruct((D_SSM, D_SSM), F32), jax.ShapeDtypeStruct((SUBLANES, D_SSM), F32)],
        scratch_shapes=[pltpu.VMEM(big, F32), pltpu.VMEM(big, F32), pltpu.VMEM(big, F32), pltpu.VMEM(big, F32),
                        pltpu.VMEM(slab, F32), pltpu.VMEM(slab, F32), pltpu.VMEM(slab, F32), pltpu.VMEM(slab, F32),
                        pltpu.VMEM((N_SLAB, 1, LANES), F32), pltpu.VMEM((N_SLAB, 1, LANES), F32),
                        pltpu.VMEM(uslab, F32), pltpu.VMEM(uslab, F32)],
        name=f"s5_bwd_{tag}", compiler_params=_cp(dimension_semantics=("arbitrary",)),
    )(dout, u, y, h_re, h_im, p["a_re8"], p["a_im8"], p["cb_re"], p["cb_im"], p["bb_re"], p["bb_im"], p["d_skip"],
      p["w_glu"], p["w_glu_t"], p["b_glu"])


def _pool_bwd(dout, pooled, p, tag):
    s = dout.shape[0]
    nt = s // TS
    hb = TS // MAX_WINDOW

    def halo(n):
        return pl.BlockSpec((MAX_WINDOW, n), lambda i: (jnp.minimum((i + 1) * hb, nt * hb - 1), 0))

    def body(do_ref, po_ref, doh_ref, wl_ref, w_ref, wt_ref, sc_ref, dz_ref, dw_ref, dsc_ref, buf):
        i = pl.program_id(0)

        @pl.when(i == 0)
        def _():
            dw_ref[...] = jnp.zeros_like(dw_ref)
            dsc_ref[...] = jnp.zeros_like(dsc_ref)

        wl = wl_ref[...]
        sc = sc_ref[...]
        dov = do_ref[...]
        pooled_b = po_ref[...]
        dsc_ref[...] += _colsum8(dov * _dot(pooled_b, w_ref[...]))
        dmix = (dov * sc).astype(BF16)
        dw_ref[...] += _dot_tn(pooled_b, dmix)
        dpool = _dot(dmix, wt_ref[...])
        dpool_h = _dot((doh_ref[...] * sc).astype(BF16), wt_ref[...])
        buf[pl.ds(0, TS), :] = dpool / _pool_count(i, TS, wl)
        buf[pl.ds(TS, MAX_WINDOW), :] = jnp.where(i < nt - 1, dpool_h / _pool_count(i, MAX_WINDOW, wl, TS), 0.0)
        dz_ref[...] = _window_sum(buf, 0, TS, wl, 1) - dpool

    return pl.pallas_call(
        body, grid=(nt,),
        in_specs=[_row(TS, D_POOL), _row(TS, D_POOL), halo(D_POOL), _const((1, D_POOL)), _const((D_POOL, D_POOL)),
                  _const((D_POOL, D_POOL)), _const((1, D_POOL))],
        out_specs=[_row(TS, D_POOL), _acc((D_POOL, D_POOL)), _acc((SUBLANES, D_POOL))],
        out_shape=[jax.ShapeDtypeStruct((s, D_POOL), F32), jax.ShapeDtypeStruct((D_POOL, D_POOL), F32),
                   jax.ShapeDtypeStruct((SUBLANES, D_POOL), F32)],
        scratch_shapes=[pltpu.VMEM((TS + MAX_WINDOW, D_POOL), F32)],
        name=f"pool_bwd_{tag}", compiler_params=_cp(dimension_semantics=("arbitrary",)),
    )(dout, pooled, dout, _pool_consts(), p["w_pool_bd"], p["w_pool_bd_t"], p["pool_scale"])


def _sgu_bwd(dout, zuv, p, tag):
    s = zuv.shape[0]

    def body(do_ref, z_ref, lng_ref, lnb_ref, wp_ref, wpt_ref, bias_ref,
             dz_ref, dws_ref, dbias_ref, dlng_ref, dlnb_ref):
        @pl.when(pl.program_id(0) == 0)
        def _():
            for ref in (dws_ref, dbias_ref, dlng_ref, dlnb_ref):
                ref[...] = jnp.zeros_like(ref)

        lo, hi = _half_masks()
        lng = lng_ref[...]
        zu, zv, u, vn, rs, vl = _sgu_front(z_ref[...], lng, lnb_ref[...])
        mixed = _sgu_mix(vl, wp_ref, lo, hi) + jnp.tile(bias_ref[...], (TS // CHUNK, 1))
        dov = do_ref[...]
        dzu = dov * mixed * _gelu_grad(zu)
        dmix = dov * u
        dbias = dbias_ref[...]
        for c in range(TS // CHUNK):
            dmc = dmix[CHUNK * c:CHUNK * (c + 1), :]
            dbias = dbias + dmc
            vlc = vl[CHUNK * c:CHUNK * (c + 1), :].astype(BF16)
            for q in range(SGU_HEADS // 2):
                dq = dmc[:, LANES * q:LANES * (q + 1)]
                vq = vlc[:, LANES * q:LANES * (q + 1)]
                dws_ref[2 * q] += _dot_nt((dq * lo).astype(BF16), vq)
                dws_ref[2 * q + 1] += _dot_nt((dq * hi).astype(BF16), vq)
        dbias_ref[...] = dbias
        dvl = _sgu_mix(dmix, wpt_ref, lo, hi)
        dlng_ref[...] += _colsum8(dvl * vn)
        dlnb_ref[...] += _colsum8(dvl)
        dvn = dvl * lng
        dv = rs * (dvn - jnp.mean(dvn, axis=-1, keepdims=True) - vn * jnp.mean(dvn * vn, axis=-1, keepdims=True))
        dz_ref[...] = jnp.concatenate([dzu, dv * _gelu_grad(zv)], axis=1)

    return pl.pallas_call(
        body, grid=(s // TS,),
        in_specs=[_row(TS, D_SGU), _row(TS, 2 * D_SGU), _const((1, D_SGU)), _const((1, D_SGU)),
                  _const((SGU_HEADS // 2, CHUNK, 2 * CHUNK)), _const((SGU_HEADS // 2, CHUNK, 2 * CHUNK)),
                  _const((CHUNK, D_SGU))],
        out_specs=[_row(TS, 2 * D_SGU), _acc((SGU_HEADS, CHUNK, CHUNK)), _acc((CHUNK, D_SGU)),
                   _acc((SUBLANES, D_SGU)), _acc((SUBLANES, D_SGU))],
        out_shape=[jax.ShapeDtypeStruct((s, 2 * D_SGU), F32), jax.ShapeDtypeStruct((SGU_HEADS, CHUNK, CHUNK), F32),
                   jax.ShapeDtypeStruct((CHUNK, D_SGU), F32), jax.ShapeDtypeStruct((SUBLANES, D_SGU), F32),
                   jax.ShapeDtypeStruct((SUBLANES, D_SGU), F32)],
        name=f"sgu_bwd_{tag}", compiler_params=_cp(dimension_semantics=("arbitrary",)),
    )(dout, zuv, p["sgu_ln_g"], p["sgu_ln_b"], p["ws_pair"], p["ws_pair_t"], p["bias_sp"])


def _mix_in_bwd(dza, dzb, dzuv, x0, dx1, p, tag):
    s = x0.shape[0]

    def body(da_ref, db_ref, dc_ref, x_ref, dx1_ref, wt_ref, g_ref, dx0_ref, dz_ref, dg_ref):
        @pl.when(pl.program_id(0) == 0)
        def _():
            dg_ref[...] = jnp.zeros_like(dg_ref)

        dz = jnp.concatenate([da_ref[...], db_ref[...], dc_ref[...]], axis=1).astype(BF16)
        dz_ref[...] = dz
        n, r = _rms(x_ref[...])
        dxn, dgp = _rms_bwd(_dot(dz, wt_ref[...]), n, r, g_ref[...])
        dg_ref[...] += _colsum8(dgp)
        dx0_ref[...] = dx1_ref[...] + dxn

    return pl.pallas_call(
        body, grid=(s // TS,),
        in_specs=[_row(TS, D_SSM), _row(TS, D_POOL), _row(TS, 2 * D_SGU), _row(TS, D_MODEL), _row(TS, D_MODEL),
                  _const((D_IN, D_MODEL)), _const((1, D_MODEL))],
        out_specs=[_row(TS, D_MODEL), _row(TS, D_IN), _acc((SUBLANES, D_MODEL))],
        out_shape=[jax.ShapeDtypeStruct((s, D_MODEL), F32), jax.ShapeDtypeStruct((s, D_IN), BF16),
                   jax.ShapeDtypeStruct((SUBLANES, D_MODEL), F32)],
        name=f"mix_in_bwd_{tag}", compiler_params=_cp(dimension_semantics=("arbitrary",)),
    )(dza, dzb, dzuv, x0, dx1, p["w_in_t"], p["g_mix"])


def _head(x, target, g):
    s = x.shape[0]

    def body(x_ref, t_ref, g_ref, dx_ref, loss_ref, dg_ref):
        @pl.when(pl.program_id(0) == 0)
        def _():
            loss_ref[...] = jnp.zeros_like(loss_ref)
            dg_ref[...] = jnp.zeros_like(dg_ref)

        gv = g_ref[...]
        n, r = _rms(x_ref[...])
        diff = n * gv - t_ref[...]
        loss_ref[...] += jnp.sum(diff * diff) * (0.5 / D_MODEL)
        dxn, dgp = _rms_bwd(diff * (1.0 / D_MODEL), n, r, gv)
        dg_ref[...] += _colsum8(dgp)
        dx_ref[...] = dxn

    return pl.pallas_call(
        body, grid=(s // TS,),
        in_specs=[_row(TS, D_MODEL), _row(TS, D_MODEL), _const((1, D_MODEL))],
        out_specs=[_row(TS, D_MODEL), _acc((SUBLANES, LANES)), _acc((SUBLANES, D_MODEL))],
        out_shape=[jax.ShapeDtypeStruct((s, D_MODEL), F32), jax.ShapeDtypeStruct((SUBLANES, LANES), F32),
                   jax.ShapeDtypeStruct((SUBLANES, D_MODEL), F32)],
        name="head", compiler_params=_cp(dimension_semantics=("arbitrary",)),
    )(x, target, g)


def _atb(a, b, tn, tag, layer, into):
    s, ka = a.shape
    kb = b.shape[1]
    ns = s // TS

    def body(a_ref, b_ref, *rest):
        o_ref = rest[-1]

        @pl.when(pl.program_id(1) == 0)
        def _():
            o_ref[...] = jnp.zeros_like(o_ref)

        o_ref[...] += _dot_tn(a_ref[...].astype(BF16), b_ref[...].astype(BF16))

    in_specs = [pl.BlockSpec((TS, ka), lambda j, i: (i, 0)), pl.BlockSpec((TS, tn), lambda j, i: (i, j))]
    args = (a, b)
    aliases = {}
    if into is not None:
        in_specs.append(pl.BlockSpec(memory_space=pl.ANY))
        args += (into,)
        aliases = {2: 0}
    return pl.pallas_call(
        body, grid=(kb // tn, ns), in_specs=in_specs,
        out_specs=pl.BlockSpec((None, ka, tn), lambda j, i: (layer, 0, j)),
        out_shape=jax.ShapeDtypeStruct((DEPTH, ka, kb), F32), input_output_aliases=aliases,
        name=f"atb_{tag}", compiler_params=_cp(dimension_semantics=("arbitrary", "arbitrary")),
    )(*args)


def _s5_discretise(a_re, a_im, log_dt, b_re, b_im):
    dt = jnp.exp(log_dt)[:, None]
    mag = jnp.exp(a_re * dt)
    ar = mag * jnp.cos(a_im * dt)
    ai = mag * jnp.sin(a_im * dt)
    den = a_re * a_re + a_im * a_im
    f_re = ((ar - 1.0) * a_re + ai * a_im) / den
    f_im = (ai * a_re - (ar - 1.0) * a_im) / den
    bb_re = f_re[..., None] * b_re - f_im[..., None] * b_im
    bb_im = f_re[..., None] * b_im + f_im[..., None] * b_re
    return ar, ai, bb_re, bb_im


def _block_diag(blocks):
    g, r, c = blocks.shape
    eye = jnp.eye(g, dtype=blocks.dtype)
    return (blocks[:, :, None, :] * eye[:, None, :, None]).reshape(g * r, g * c)


def _block_diag_extract(m, g):
    r = m.shape[0] // g
    c = m.shape[1] // g
    eye = jnp.eye(g, dtype=m.dtype)
    return jnp.sum(m.reshape(g, r, g, c) * eye[:, None, :, None], axis=2)


def _state_slabs(v):
    return jnp.broadcast_to(v.reshape(N_SLAB, 1, LANES), (N_SLAB, SUBLANES, LANES))


def _tril():
    return jnp.tril(jnp.ones((CHUNK, CHUNK), dtype=bool))


def _layer_params(w, big, l):
    row = lambda v: v.reshape(1, -1)
    t = lambda m: jnp.swapaxes(m, -1, -2)
    ar, ai, bb_re, bb_im = _s5_discretise(w["A_re"][l], w["A_im"][l], w["log_dt"][l], w["B_re"][l], w["B_im"][l])
    bbt_re = _block_diag(t(bb_re)).astype(BF16)
    bbt_im = _block_diag(t(bb_im)).astype(BF16)
    ct_re = _block_diag(t(w["C_re"][l])).astype(BF16)
    ct_im = _block_diag(t(w["C_im"][l])).astype(BF16)
    ws = jnp.where(_tril()[None], w["w_spatial"][l], 0.0)
    pair = lambda m: jnp.stack([jnp.concatenate([m[2 * q], m[2 * q + 1]], axis=1)
                                for q in range(SGU_HEADS // 2)]).astype(BF16)
    wp = _block_diag(w["w_pool"][l]).astype(BF16)
    p = dict(
        g_mix=row(w["g_mix"][l]), g_ffn=row(w["g_ffn"][l]), d_skip=row(w["D_skip"][l]), b_glu=row(w["b_glu"][l]),
        pool_scale=row(w["pool_scale"][l]), sgu_ln_g=row(w["sgu_ln_g"][l]), sgu_ln_b=row(w["sgu_ln_b"][l]),
        a_re8=_state_slabs(ar), a_im8=_state_slabs(ai),
        bbt_re=bbt_re, bbt_im=bbt_im, bb_re=t(bbt_re), bb_im=t(bbt_im),
        ct_re=ct_re, ct_im=ct_im, cb_re=t(ct_re), cb_im=t(ct_im),
        w_pool_bd=wp, w_pool_bd_t=t(wp), ws_pair=pair(ws), ws_pair_t=pair(t(ws)),
        bias_sp=jnp.repeat(t(w["b_spatial"][l]), SGU_HEAD_DIM, axis=1),
    )
    for n in BIG:
        m = big[n][l]
        p[n], p[n + "_t"] = (t(m), m) if n in TRANSPOSED else (m, t(m))
    return p


def _layer_fwd(x0, p, tag):
    za, zb, zuv, h1 = _mix_in_fwd(x0, p["g_mix"], p["w_in"], tag)
    oa, y, h_re, h_im = _s5_fwd(za, p, tag)
    ob, pooled = _pool_fwd(zb, p, tag)
    oc = _sgu_fwd(zuv, p, tag)
    x1, x2, h2, gt, up, ycat = _blk_fwd(x0, oa, ob, oc, p, tag)
    saved = dict(x0=x0, za=za, zuv=zuv, h1=h1, ycat=ycat, y=y, h_re=h_re, h_im=h_im, pooled=pooled, x1=x1, h2=h2, gt=gt,
                 up=up)
    return x2, saved


def _rows_sum(v):
    return jnp.sum(v, axis=0)


ATB_COLS = 512


def _layer_bwd(dx2, sv, p, w, l, tag, into):
    t = lambda m: jnp.swapaxes(m, -1, -2)
    dx1, da, db, dc, dgt, dup, act, dg_ffn = _blk_bwd(dx2, sv["x1"], sv["gt"], sv["up"], p, tag)
    big = {}
    big["w_down"] = _atb(act, dx2, ATB_COLS, tag + "_wd", l, into.get("w_down"))
    big["w_gate"] = _atb(dgt, sv["h2"], ATB_COLS, tag + "_wg", l, into.get("w_gate"))
    big["w_up"] = _atb(dup, sv["h2"], ATB_COLS, tag + "_wu", l, into.get("w_up"))
    big["w_out"] = _atb(sv["ycat"], dx1, ATB_COLS, tag + "_wo", l, into.get("w_out"))
    g = {}
    g["g_ffn"] = _rows_sum(dg_ffn)
    dza, dct_re, dct_im, dbbt_re, dbbt_im, dar8, dai8, dd8, dwglu, dbglu8 = _s5_bwd(
        da, sv["za"], sv["y"], sv["h_re"], sv["h_im"], p, tag)
    dzb, dwp, dsc8 = _pool_bwd(db, sv["pooled"], p, tag)
    dzuv, dws, dbias, dlng8, dlnb8 = _sgu_bwd(dc, sv["zuv"], p, tag)
    dx0, dz, dg_mix = _mix_in_bwd(dza, dzb, dzuv, sv["x0"], dx1, p, tag)
    big["w_in"] = _atb(dz, sv["h1"], D_MODEL, tag + "_wi", l, into.get("w_in"))
    g["g_mix"] = _rows_sum(dg_mix)
    g["w_glu"] = dwglu
    g["b_glu"] = _rows_sum(dbglu8)
    g["D_skip"] = _rows_sum(dd8)
    g["C_re"] = t(_block_diag_extract(dct_re, N_GROUPS))
    g["C_im"] = t(_block_diag_extract(dct_im, N_GROUPS))
    dar = jnp.sum(dar8, axis=1).reshape(N_GROUPS, SSM_STATE)
    dai = jnp.sum(dai8, axis=1).reshape(N_GROUPS, SSM_STATE)
    dbb_re = t(_block_diag_extract(dbbt_re, N_GROUPS))
    dbb_im = t(_block_diag_extract(dbbt_im, N_GROUPS))
    _, disc_vjp = jax.vjp(_s5_discretise, w["A_re"][l], w["A_im"][l], w["log_dt"][l], w["B_re"][l], w["B_im"][l])
    g["A_re"], g["A_im"], g["log_dt"], g["B_re"], g["B_im"] = disc_vjp((dar, dai, dbb_re, dbb_im))
    g["w_pool"] = _block_diag_extract(dwp, len(POOL_WINDOWS))
    g["pool_scale"] = _rows_sum(dsc8)
    g["sgu_ln_g"] = _rows_sum(dlng8)
    g["sgu_ln_b"] = _rows_sum(dlnb8)
    g["w_spatial"] = jnp.where(_tril()[None], dws, 0.0)
    g["b_spatial"] = t(jnp.sum(dbias.reshape(CHUNK, SGU_HEADS, SGU_HEAD_DIM), axis=-1))
    return dx0, g, big


def _local_step(x, target, w, big):
    params = [_layer_params(w, big, l) for l in range(DEPTH)]
    saved = []
    h = x
    for l in range(DEPTH):
        h, sv = _layer_fwd(h, params[l], f"l{l}")
        saved.append(sv)
    dx, loss8, dgf8 = _head(h, target, w["g_final"].reshape(1, -1))
    grads = [None] * DEPTH
    gbig = {}
    for l in reversed(range(DEPTH)):
        dx, grads[l], gbig = _layer_bwd(dx, saved[l], params[l], w, l, f"l{l}", gbig)
    g = {n: jnp.stack([grads[l][n] for l in range(DEPTH)]) for n in SMALL + ["w_glu"] if n != "g_final"}
    g["g_final"] = _rows_sum(dgf8)
    gbig["w_glu"] = g.pop("w_glu")
    return loss8[0, 0], dx, g, gbig


_ANY = pl.BlockSpec(memory_space=pl.ANY)
_MESH = pl.DeviceIdType.MESH


def _place():
    return lax.axis_index("x"), lax.axis_index("y"), lax.axis_index("c")


def _other_chips(x, y):
    return [(1 - x, y), (x, 1 - y), (1 - x, 1 - y)]


def _allgather8(blk, tag):
    m, n = blk.shape

    def body(x_ref, out_ref, send_sems, recv_sems, local_sem):
        x, y, c = _place()
        me, sibling = (x, y, c), (x, y, 1 - c)
        chips = _other_chips(x, y)
        mine_src = x_ref

        def slot(px, py, pc):
            return out_ref.at[4 * px + 2 * py + pc]

        def copy(k, block, to, src=None):
            return pltpu.make_async_remote_copy(
                src_ref=slot(*block) if src is None else src, dst_ref=slot(*block),
                send_sem=send_sems.at[k], recv_sem=recv_sems.at[k], device_id=to, device_id_type=_MESH)

        mine = pltpu.make_async_copy(mine_src, slot(*me), local_sem)
        mine.start()
        first = [copy(0, me, sibling, src=mine_src)]
        first += [copy(1 + j, me, (*chip, c), src=mine_src) for j, chip in enumerate(chips)]
        for cp in first:
            cp.start()
        passed = [copy(4 + j, (*chip, c), sibling) for j, chip in enumerate(chips)]
        for j, chip in enumerate(chips):
            copy(1 + j, (*chip, c), me).wait_recv()
            passed[j].start()
        copy(0, sibling, me).wait_recv()
        for j, chip in enumerate(chips):
            copy(4 + j, (*chip, 1 - c), me).wait_recv()
        for cp in first + passed:
            cp.wait_send()
        mine.wait()

    return pl.pallas_call(
        body, out_shape=jax.ShapeDtypeStruct((N_DEV, m, n), blk.dtype), in_specs=[_ANY], out_specs=_ANY,
        scratch_shapes=[pltpu.SemaphoreType.DMA((7,)), pltpu.SemaphoreType.DMA((7,)), pltpu.SemaphoreType.DMA],
        name=f"allgather8_{tag}",
    )(blk)


def _dma_sems(n):
    return pltpu.SemaphoreType.DMA((n,))


def _remote(src, dst, send_sems, recv_sems, k, to):
    return pltpu.make_async_remote_copy(src_ref=src, dst_ref=dst, send_sem=send_sems.at[k], recv_sem=recv_sems.at[k],
                                        device_id=to, device_id_type=_MESH)


def _gather_weights(shards, tag):
    nw = len(shards)

    def body(*refs):
        ins, outs = refs[:nw], refs[nw:2 * nw]
        send_sems, recv_sems, local_sems = refs[2 * nw:]
        x, y, c = _place()
        me, sibling = (x, y, c), (x, y, 1 - c)
        chips = _other_chips(x, y)

        def slot(i, px, py, pc):
            r = ins[i].shape[1]
            return outs[i].at[pc, pl.ds((2 * px + py) * r, r)]

        def copy(i, k, block, to, src=None):
            return _remote(slot(i, *block) if src is None else src, slot(i, *block), send_sems, recv_sems, 7 * i + k, to)

        mine = [pltpu.make_async_copy(ins[i].at[c], slot(i, *me), local_sems.at[i]) for i in range(nw)]
        for cp in mine:
            cp.start()
        first = []
        for i in range(nw):
            first += [copy(i, 1 + j, me, (*chip, c), src=ins[i].at[c]) for j, chip in enumerate(chips)]
            first.append(copy(i, 0, me, sibling, src=ins[i].at[c]))
        for cp in first:
            cp.start()
        passed = []
        for i in range(nw):
            for j, chip in enumerate(chips):
                copy(i, 1 + j, (*chip, c), me).wait_recv()
                passed.append(copy(i, 4 + j, (*chip, c), sibling))
                passed[-1].start()
        for i in range(nw):
            copy(i, 0, sibling, me).wait_recv()
            for j, chip in enumerate(chips):
                copy(i, 4 + j, (*chip, 1 - c), me).wait_recv()
        for cp in first + passed:
            cp.wait_send()
        for cp in mine:
            cp.wait()

    return pl.pallas_call(
        body, out_shape=[jax.ShapeDtypeStruct((DEPTH, N_CHIPS * s.shape[1], s.shape[2]), s.dtype) for s in shards],
        in_specs=[_ANY] * nw, out_specs=[_ANY] * nw,
        scratch_shapes=[_dma_sems(7 * nw), _dma_sems(7 * nw), _dma_sems(nw)], name=f"gather_{tag}",
    )(*shards)


def _swap_layers(gs, tag):
    nw = len(gs)

    def body(*refs):
        ins, outs = refs[:nw], refs[nw:2 * nw]
        send_sems, recv_sems = refs[2 * nw:]
        x, y, c = _place()
        copies = [_remote(ins[i].at[1 - c], outs[i], send_sems, recv_sems, i, (x, y, 1 - c)) for i in range(nw)]
        for cp in copies:
            cp.start()
        for cp in copies:
            cp.wait()

    return pl.pallas_call(
        body, out_shape=[jax.ShapeDtypeStruct(g.shape[1:], g.dtype) for g in gs], in_specs=[_ANY] * nw,
        out_specs=[_ANY] * nw, scratch_shapes=[_dma_sems(nw), _dma_sems(nw)], name=f"swap_layers_{tag}",
    )(*gs)


def _exchange_chips(ps, tag):
    nw = len(ps)
    nrel = N_CHIPS - 1

    def body(*refs):
        ins, outs = refs[:nw], refs[nw:2 * nw]
        send_sems, recv_sems = refs[2 * nw:]
        x, y, c = _place()
        copies = []
        for i in range(nw):
            r = ins[i].shape[0] // N_CHIPS
            for j, (cx, cy) in enumerate(_other_chips(x, y)):
                copies.append(_remote(ins[i].at[pl.ds((2 * cx + cy) * r, r)], outs[i].at[j], send_sems, recv_sems,
                                      nrel * i + j, (cx, cy, c)))
        for cp in copies:
            cp.start()
        for cp in copies:
            cp.wait()

    return pl.pallas_call(
        body, out_shape=[jax.ShapeDtypeStruct((nrel, p.shape[0] // N_CHIPS, p.shape[1]), p.dtype) for p in ps],
        in_specs=[_ANY] * nw, out_specs=[_ANY] * nw, scratch_shapes=[_dma_sems(nrel * nw), _dma_sems(nrel * nw)],
        name=f"exchange_chips_{tag}",
    )(*ps)


def _gather_layers(qs, tag):
    nw = len(qs)

    def body(*refs):
        ins, outs = refs[:nw], refs[nw:2 * nw]
        send_sems, recv_sems, local_sems = refs[2 * nw:]
        x, y, c = _place()
        mine = [pltpu.make_async_copy(ins[i], outs[i].at[c], local_sems.at[i]) for i in range(nw)]
        sends = [_remote(ins[i], outs[i].at[c], send_sems, recv_sems, i, (x, y, 1 - c)) for i in range(nw)]
        for cp in mine + sends:
            cp.start()
        for i in range(nw):
            sends[i].wait_send()
            _remote(ins[i], outs[i].at[1 - c], send_sems, recv_sems, i, (x, y, 1 - c)).wait_recv()
        for cp in mine:
            cp.wait()

    return pl.pallas_call(
        body, out_shape=[jax.ShapeDtypeStruct((DEPTH,) + q.shape, q.dtype) for q in qs], in_specs=[_ANY] * nw,
        out_specs=[_ANY] * nw, scratch_shapes=[_dma_sems(nw), _dma_sems(nw), _dma_sems(nw)],
        name=f"gather_layers_{tag}",
    )(*qs)


def _add_layers(g2, recv, sel, tag):
    _, rows, n = g2.shape
    r = rows // N_CHIPS

    def body(sel_ref, a_ref, b_ref, o_ref):
        o_ref[...] = (a_ref[...] + b_ref[...]).astype(BF16)

    return pl.pallas_call(
        body, grid_spec=pltpu.PrefetchScalarGridSpec(
            num_scalar_prefetch=1, grid=(N_CHIPS,),
            in_specs=[pl.BlockSpec((None, r, n), lambda i, s: (s[0], i, 0)), pl.BlockSpec((r, n), lambda i, s: (i, 0))],
            out_specs=pl.BlockSpec((r, n), lambda i, s: (i, 0))),
        out_shape=jax.ShapeDtypeStruct((rows, n), BF16), name=f"add_layers_{tag}",
        compiler_params=_cp(dimension_semantics=("arbitrary",)),
    )(sel, g2, recv)


def _add_chips(p, recv, sel, tag):
    nrel, r, n = recv.shape

    def body(sel_ref, a_ref, b_ref, o_ref):
        acc = a_ref[...].astype(F32)
        for j in range(nrel):
            acc = acc + b_ref[j].astype(F32)
        o_ref[...] = acc

    return pl.pallas_call(
        body, grid_spec=pltpu.PrefetchScalarGridSpec(
            num_scalar_prefetch=1, grid=(1,),
            in_specs=[pl.BlockSpec((r, n), lambda i, s: (s[1], 0)), pl.BlockSpec((nrel, r, n), lambda i, s: (0, 0, 0))],
            out_specs=pl.BlockSpec((r, n), lambda i, s: (0, 0))),
        out_shape=jax.ShapeDtypeStruct((r, n), F32), name=f"add_chips_{tag}",
        compiler_params=_cp(dimension_semantics=("arbitrary",)),
    )(sel, p, recv)


def _adamw_math(w, g, m, v):
    m = ADAM_B1 * m + (1.0 - ADAM_B1) * g
    v = ADAM_B2 * v + (1.0 - ADAM_B2) * (g * g)
    m_hat = m / (1.0 - ADAM_B1 ** ADAM_STEP)
    v_hat = v / (1.0 - ADAM_B2 ** ADAM_STEP)
    delta = -ADAM_LR * (m_hat / (jnp.sqrt(v_hat) + ADAM_EPS) + ADAM_WD * w)
    return delta, m, v


ADAM_ROWS = 64


def _adamw(w, g, m, v, tag):
    shape = w.shape
    cols = shape[-1]
    flat = lambda a: a.reshape(-1, cols)
    rows = w.size // cols
    tr = ADAM_ROWS

    def body(w_ref, g_ref, m_ref, v_ref, d_ref, nm_ref, nv_ref):
        d, nm, nv = _adamw_math(w_ref[...], g_ref[...], m_ref[...], v_ref[...])
        d_ref[...] = d
        nm_ref[...] = nm
        nv_ref[...] = nv

    spec = _row(tr, cols)
    out = pl.pallas_call(
        body, grid=(rows // tr,), in_specs=[spec] * 4, out_specs=[spec] * 3,
        out_shape=[jax.ShapeDtypeStruct((rows, cols), F32)] * 3, name=f"adamw_{tag}",
        compiler_params=_cp(dimension_semantics=("arbitrary",)),
    )(flat(w), flat(g), flat(m), flat(v))
    return [o.reshape(shape) for o in out]


SMALL_ROWS = 3584
SMALL_TILE = 512


def _small_reduce_adamw(gathered, w, m, v):
    tr = SMALL_TILE

    def body(ga_ref, w_ref, m_ref, v_ref, g_ref, d_ref, nm_ref, nv_ref):
        g = ga_ref[0]
        for k in range(1, N_DEV):
            g = g + ga_ref[k]
        g_ref[...] = g
        d, nm, nv = _adamw_math(w_ref[...], g, m_ref[...], v_ref[...])
        d_ref[...] = d
        nm_ref[...] = nm
        nv_ref[...] = nv

    spec = _row(tr, LANES)
    return pl.pallas_call(
        body, grid=(SMALL_ROWS // tr,),
        in_specs=[pl.BlockSpec((N_DEV, tr, LANES), lambda i: (0, i, 0)), spec, spec, spec], out_specs=[spec] * 4,
        out_shape=[jax.ShapeDtypeStruct((SMALL_ROWS, LANES), F32)] * 4, name="small_reduce_adamw",
        compiler_params=_cp(dimension_semantics=("arbitrary",)),
    )(gathered, w, m, v)


def _exchange_form(n, a):
    return jnp.swapaxes(a, 1, 2) if n in TRANSPOSED else a


def _pack_small(vals, loss):
    flat = jnp.concatenate([vals[n].reshape(-1) for n in SMALL] + [loss.reshape(1)])
    return jnp.pad(flat, (0, SMALL_ROWS * LANES - flat.size)).reshape(SMALL_ROWS, LANES)


def _unpack_small(buf, like):
    flat = buf.reshape(-1)
    out, off = {}, 0
    for n in SMALL:
        out[n] = flat[off:off + like[n].size].reshape(like[n].shape)
        off += like[n].size
    return out, flat[off]


def kernel(x, g_mix, w_in, A_re, A_im, log_dt, B_re, B_im, C_re, C_im, D_skip, w_glu, b_glu, w_pool, pool_scale, sgu_ln_g, sgu_ln_b, w_spatial, b_spatial, w_out, g_ffn, w_gate, w_up, w_down, g_final, loss_target, m_g_mix, m_w_in, m_A_re, m_A_im, m_log_dt, m_B_re, m_B_im, m_C_re, m_C_im, m_D_skip, m_w_glu, m_b_glu, m_w_pool, m_pool_scale, m_sgu_ln_g, m_sgu_ln_b, m_w_spatial, m_b_spatial, m_w_out, m_g_ffn, m_w_gate, m_w_up, m_w_down, m_g_final, v_g_mix, v_w_in, v_A_re, v_A_im, v_log_dt, v_B_re, v_B_im, v_C_re, v_C_im, v_D_skip, v_w_glu, v_b_glu, v_w_pool, v_pool_scale, v_sgu_ln_g, v_sgu_ln_b, v_w_spatial, v_b_spatial, v_w_out, v_g_ffn, v_w_gate, v_w_up, v_w_down, v_g_final):
    loc = locals()
    w = {n: loc[n] for n in WEIGHTS}
    m = {n: loc["m_" + n] for n in WEIGHTS}
    v = {n: loc["v_" + n] for n in WEIGHTS}
    sel = jnp.stack([lax.axis_index("c"), 2 * lax.axis_index("x") + lax.axis_index("y")]).astype(jnp.int32)

    gathered = _gather_weights([_exchange_form(n, w[n]).astype(BF16) for n in BIG], "weights")
    big = dict(zip(BIG, gathered))

    loss_local, dx, g, gbig = _local_step(x[0], loss_target[0], w, big)

    recv = _swap_layers([gbig[n] for n in BIG], "grads")
    part = [_add_layers(gbig[n], recv[i], sel, n) for i, n in enumerate(BIG)]
    slabs = _exchange_chips(part, "grads")
    mine = [_add_chips(part[i], slabs[i], sel, n) for i, n in enumerate(BIG)]
    gshards = _gather_layers(mine, "grads")
    grads, deltas, new_m, new_v = {}, {}, {}, {}
    for i, n in enumerate(BIG):
        grads[n] = _exchange_form(n, gshards[i])
        deltas[n], new_m[n], new_v[n] = _adamw(w[n], grads[n], m[n], v[n], n)

    small_all = _allgather8(_pack_small(g, loss_local), "small")
    zero = jnp.zeros((), F32)
    gs, ds, ms, vs = _small_reduce_adamw(small_all, _pack_small(w, zero), _pack_small(m, zero), _pack_small(v, zero))
    gsm, loss = _unpack_small(gs, w)
    grads.update(gsm)
    deltas.update(_unpack_small(ds, w)[0])
    new_m.update(_unpack_small(ms, w)[0])
    new_v.update(_unpack_small(vs, w)[0])
    return (loss, dx[None], *[grads[n] for n in WEIGHTS], *[deltas[n] for n in WEIGHTS],
            *[new_m[n] for n in WEIGHTS], *[new_v[n] for n in WEIGHTS])
```

```python
import math

import jax
import jax.numpy as jnp
from jax import lax
from jax.experimental import pallas as pl
from jax.experimental.pallas import tpu as pltpu

F32 = jnp.float32
BF16 = jnp.bfloat16

D_MODEL = 1024
DEPTH = 2
D_SSM = 384
SSM_GROUP = 16
N_GROUPS = 24
SSM_STATE = 64
N_STATE = N_GROUPS * SSM_STATE
POOL_WINDOWS = (2, 4, 8, 16)
POOL_GROUP = 64
D_POOL = 256
MAX_WINDOW = 16
SGU_HEADS = 6
SGU_HEAD_DIM = 64
D_SGU = 384
CHUNK = 128
D_IN = D_SSM + D_POOL + 2 * D_SGU
D_FF = 2816
EPS = 1e-6

ADAM_LR = 0.001
ADAM_B1 = 0.9
ADAM_B2 = 0.999
ADAM_EPS = 1e-08
ADAM_WD = 0.01
ADAM_STEP = 10

LANES = 128
SUBLANES = 8
N_SLAB = N_STATE // LANES
VMEM_LIMIT = 56 * 1024 * 1024

TS = 512
TS_FFN = 256

WEIGHTS = ['g_mix', 'w_in', 'A_re', 'A_im', 'log_dt', 'B_re', 'B_im', 'C_re', 'C_im', 'D_skip', 'w_glu', 'b_glu',
           'w_pool', 'pool_scale', 'sgu_ln_g', 'sgu_ln_b', 'w_spatial', 'b_spatial', 'w_out', 'g_ffn', 'w_gate',
           'w_up', 'w_down', 'g_final']
BIG = ['w_in', 'w_glu', 'w_out', 'w_gate', 'w_up', 'w_down']
SMALL = [n for n in WEIGHTS if n not in BIG]
TRANSPOSED = ("w_in", "w_gate", "w_up")
N_CHIPS = 4
N_DEV = 8


def _cp(**kw):
    return pltpu.CompilerParams(vmem_limit_bytes=VMEM_LIMIT, **kw)


def _row(ts, n):
    return pl.BlockSpec((ts, n), lambda i: (i, 0))


def _const(shape):
    nd = len(shape)
    return pl.BlockSpec(shape, lambda i: (0,) * nd, pipeline_mode=pl.Buffered(1))


def _acc(shape):
    nd = len(shape)
    return pl.BlockSpec(shape, lambda i: (0,) * nd)


def _dot(a, b):
    return jnp.dot(a, b, preferred_element_type=F32)


def _dot_tn(a, b):
    return lax.dot_general(a, b, (((0,), (0,)), ((), ())), preferred_element_type=F32)


def _dot_nt(a, b):
    return lax.dot_general(a, b, (((1,), (1,)), ((), ())), preferred_element_type=F32)


_G0 = math.sqrt(2.0 / math.pi)
_G1 = 0.044715


def _gelu(x):
    return 0.5 * x * (1.0 + jnp.tanh(_G0 * (x + _G1 * x * x * x)))


def _gelu_grad(x):
    t = jnp.tanh(_G0 * (x + _G1 * x * x * x))
    return 0.5 * (1.0 + t) + 0.5 * x * (1.0 - t * t) * (_G0 * (1.0 + 3.0 * _G1 * x * x))


def _sigmoid(x):
    return 1.0 / (1.0 + jnp.exp(-x))


def _rms(x):
    r = lax.rsqrt(jnp.mean(x * x, axis=-1, keepdims=True) + EPS)
    return x * r, r


def _rms_bwd(dh, n, r, g):
    dn = dh * g
    return r * (dn - n * jnp.mean(dn * n, axis=-1, keepdims=True)), dh * n


def _colsum8(v):
    rows, n = v.shape
    return jnp.sum(v.reshape(rows // SUBLANES, SUBLANES, n), axis=0)


def _mix_in_fwd(x, g, w, tag):
    s = x.shape[0]

    def body(x_ref, g_ref, w_ref, za_ref, zb_ref, zuv_ref, h_ref):
        n, _ = _rms(x_ref[...])
        h = (n * g_ref[...]).astype(BF16)
        z = _dot(h, w_ref[...])
        za_ref[...] = z[:, :D_SSM]
        zb_ref[...] = z[:, D_SSM:D_SSM + D_POOL]
        zuv_ref[...] = z[:, D_SSM + D_POOL:]
        h_ref[...] = h

    return pl.pallas_call(
        body, grid=(s // TS,),
        in_specs=[_row(TS, D_MODEL), _const((1, D_MODEL)), _const((D_MODEL, D_IN))],
        out_specs=[_row(TS, D_SSM), _row(TS, D_POOL), _row(TS, 2 * D_SGU), _row(TS, D_MODEL)],
        out_shape=[jax.ShapeDtypeStruct((s, D_SSM), F32), jax.ShapeDtypeStruct((s, D_POOL), F32),
                   jax.ShapeDtypeStruct((s, 2 * D_SGU), F32), jax.ShapeDtypeStruct((s, D_MODEL), BF16)],
        name=f"mix_in_fwd_{tag}", compiler_params=_cp(dimension_semantics=("arbitrary",)),
    )(x, g, w)


def _cmul(ar, ai, br, bi):
    return ar * br - ai * bi, ar * bi + ai * br


def _cpow(ar, ai, n):
    assert n & (n - 1) == 0
    while n > 1:
        ar, ai = _cmul(ar, ai, ar, ai)
        n //= 2
    return ar, ai


def _to_slabs(ref, v):
    for j in range(N_SLAB):
        ref[j] = v[:, LANES * j:LANES * (j + 1)]


def _from_slabs(ref):
    return jnp.concatenate([ref[j] for j in range(N_SLAB)], axis=1)


N_USLAB = D_SSM // LANES
SEG = TS // SUBLANES


def _interleave_rows(v, stage, dst):
    for j in range(N_USLAB):
        stage[j] = v[:, LANES * j:LANES * (j + 1)]

    def step(k, carry):
        dst[:, pl.ds(pl.multiple_of(k * SUBLANES, SUBLANES), SUBLANES), :] = stage[:, pl.ds(k, SUBLANES, stride=SEG), :]
        return carry

    lax.fori_loop(0, SEG, step, 0)
    return jnp.concatenate([dst[j] for j in range(N_USLAB)], axis=1)


def _deinterleave_rows(v, stage, dst):
    for j in range(N_USLAB):
        stage[j] = v[:, LANES * j:LANES * (j + 1)]

    def step(k, carry):
        dst[:, pl.ds(k, SUBLANES, stride=SEG), :] = stage[:, pl.ds(pl.multiple_of(k * SUBLANES, SUBLANES), SUBLANES), :]
        return carry

    lax.fori_loop(0, SEG, step, 0)
    return jnp.concatenate([dst[j] for j in range(N_USLAB)], axis=1)


def _scan_rows(k):
    return pl.ds(pl.multiple_of(k * SUBLANES, SUBLANES), SUBLANES)


def _s5_fwd(u, p, tag):
    s = u.shape[0]
    seg = SEG

    def body(u_ref, bbr_ref, bbi_ref, ar_ref, ai_ref, ctr_ref, cti_ref, dsk_ref, wglu_ref, bglu_ref,
             oa_ref, y_ref, hr_ref, hi_ref, sr, si, er, ei, ir, ii, cr, ci, stage, perm):
        @pl.when(pl.program_id(0) == 0)
        def _():
            cr[...] = jnp.zeros_like(cr)
            ci[...] = jnp.zeros_like(ci)

        uv = _interleave_rows(u_ref[...], stage, perm)
        ub = uv.astype(BF16)
        _to_slabs(sr, _dot(ub, bbr_ref[...]))
        _to_slabs(si, _dot(ub, bbi_ref[...]))
        ar = ar_ref[...]
        ai = ai_ref[...]

        def local(k, h):
            rows = _scan_rows(k)
            hr, hi = _cmul(ar, ai, h[0], h[1])
            return hr + sr[:, rows, :], hi + si[:, rows, :]

        zero = jnp.zeros((N_SLAB, SUBLANES, LANES), F32)
        e_r, e_i = lax.fori_loop(0, seg, local, (zero, zero))
        er[...] = e_r
        ei[...] = e_i
        pr, pi = _cpow(ar[:, 0:1, :], ai[:, 0:1, :], seg)
        c_r = cr[...]
        c_i = ci[...]
        for j in range(SUBLANES):
            ir[:, j:j + 1, :] = c_r
            ii[:, j:j + 1, :] = c_i
            n_r, n_i = _cmul(pr, pi, c_r, c_i)
            c_r = n_r + er[:, j:j + 1, :]
            c_i = n_i + ei[:, j:j + 1, :]
        cr[...] = c_r
        ci[...] = c_i

        def full(k, h):
            rows = _scan_rows(k)
            hr, hi = _cmul(ar, ai, h[0], h[1])
            hr = hr + sr[:, rows, :]
            hi = hi + si[:, rows, :]
            sr[:, rows, :] = hr
            si[:, rows, :] = hi
            return hr, hi

        lax.fori_loop(0, seg, full, (ir[...], ii[...]))
        hr = _from_slabs(sr)
        hi = _from_slabs(si)
        hr_ref[...] = hr
        hi_ref[...] = hi
        y = _dot(hr.astype(BF16), ctr_ref[...]) - _dot(hi.astype(BF16), cti_ref[...]) + dsk_ref[...] * uv
        y_ref[...] = y
        g = _gelu(y)
        pre = _dot(g.astype(BF16), wglu_ref[...]) + bglu_ref[...]
        oa_ref[...] = _deinterleave_rows(g * _sigmoid(pre), stage, perm).astype(BF16)

    slab = (N_SLAB, SUBLANES, LANES)
    uslab = (N_USLAB, TS, LANES)
    return pl.pallas_call(
        body, grid=(s // TS,),
        in_specs=[_row(TS, D_SSM), _const((D_SSM, N_STATE)), _const((D_SSM, N_STATE)), _const(slab), _const(slab),
                  _const((N_STATE, D_SSM)), _const((N_STATE, D_SSM)), _const((1, D_SSM)), _const((D_SSM, D_SSM)),
                  _const((1, D_SSM))],
        out_specs=[_row(TS, D_SSM), _row(TS, D_SSM), _row(TS, N_STATE), _row(TS, N_STATE)],
        out_shape=[jax.ShapeDtypeStruct((s, D_SSM), BF16), jax.ShapeDtypeStruct((s, D_SSM), F32),
                   jax.ShapeDtypeStruct((s, N_STATE), F32), jax.ShapeDtypeStruct((s, N_STATE), F32)],
        scratch_shapes=[pltpu.VMEM((N_SLAB, TS, LANES), F32), pltpu.VMEM((N_SLAB, TS, LANES), F32),
                        pltpu.VMEM(slab, F32), pltpu.VMEM(slab, F32), pltpu.VMEM(slab, F32), pltpu.VMEM(slab, F32),
                        pltpu.VMEM((N_SLAB, 1, LANES), F32), pltpu.VMEM((N_SLAB, 1, LANES), F32),
                        pltpu.VMEM(uslab, F32), pltpu.VMEM(uslab, F32)],
        name=f"s5_fwd_{tag}", compiler_params=_cp(dimension_semantics=("arbitrary",)),
    )(u, p["bbt_re"], p["bbt_im"], p["a_re8"], p["a_im8"], p["ct_re"], p["ct_im"], p["d_skip"], p["w_glu"],
      p["b_glu"])


def _pool_consts():
    w = jnp.repeat(jnp.asarray(POOL_WINDOWS, F32), POOL_GROUP)[None, :]
    return w


def _window_sum(buf, first, rows, wl, step):
    acc = buf[pl.ds(first, rows), :]
    for j in range(1, MAX_WINDOW):
        term = buf[pl.ds(first + step * j, rows), :]
        acc = acc + (term if j < min(POOL_WINDOWS) else term * (wl > j).astype(F32))
    return acc


def _pool_count(i, rows, wl, offset=0):
    t = (i * TS + offset + 1).astype(F32) + lax.broadcasted_iota(jnp.int32, (rows, 1), 0).astype(F32)
    return jnp.minimum(t, wl)


def _pool_fwd(zb, p, tag):
    s = zb.shape[0]
    hb = TS // MAX_WINDOW

    def body(u_ref, halo_ref, wl_ref, w_ref, sc_ref, ob_ref, pooled_ref, buf):
        i = pl.program_id(0)
        uv = u_ref[...]
        buf[pl.ds(0, MAX_WINDOW), :] = jnp.where(i > 0, halo_ref[...], 0.0)
        buf[pl.ds(MAX_WINDOW, TS), :] = uv
        wl = wl_ref[...]
        pooled = (_window_sum(buf, MAX_WINDOW, TS, wl, -1) / _pool_count(i, TS, wl) - uv).astype(BF16)
        pooled_ref[...] = pooled
        ob_ref[...] = (_dot(pooled, w_ref[...]) * sc_ref[...]).astype(BF16)

    return pl.pallas_call(
        body, grid=(s // TS,),
        in_specs=[_row(TS, D_POOL),
                  pl.BlockSpec((MAX_WINDOW, D_POOL), lambda i: (jnp.maximum(i * hb - 1, 0), 0)),
                  _const((1, D_POOL)), _const((D_POOL, D_POOL)), _const((1, D_POOL))],
        out_specs=[_row(TS, D_POOL), _row(TS, D_POOL)],
        out_shape=[jax.ShapeDtypeStruct((s, D_POOL), BF16), jax.ShapeDtypeStruct((s, D_POOL), BF16)],
        scratch_shapes=[pltpu.VMEM((TS + MAX_WINDOW, D_POOL), F32)],
        name=f"pool_fwd_{tag}", compiler_params=_cp(dimension_semantics=("arbitrary",)),
    )(zb, zb, _pool_consts(), p["w_pool_bd"], p["pool_scale"])


def _sgu_mix(vl, wpair_ref, lo, hi):
    rows = vl.shape[0]
    chunks = []
    for c in range(rows // CHUNK):
        vc = vl[CHUNK * c:CHUNK * (c + 1), :]
        parts = []
        for q in range(SGU_HEADS // 2):
            vq = vc[:, LANES * q:LANES * (q + 1)]
            rhs = jnp.concatenate([vq * lo, vq * hi], axis=0).astype(BF16)
            parts.append(_dot(wpair_ref[q], rhs))
        chunks.append(jnp.concatenate(parts, axis=1))
    return jnp.concatenate(chunks, axis=0)


def _sgu_front(zuv, lng, lnb):
    zu = zuv[:, :D_SGU]
    zv = zuv[:, D_SGU:]
    u = _gelu(zu)
    v = _gelu(zv)
    mu = jnp.mean(v, axis=-1, keepdims=True)
    vc = v - mu
    rs = lax.rsqrt(jnp.mean(vc * vc, axis=-1, keepdims=True) + EPS)
    vn = vc * rs
    return zu, zv, u, vn, rs, vn * lng + lnb


def _half_masks():
    lane = lax.broadcasted_iota(jnp.int32, (1, LANES), 1)
    lo = (lane < SGU_HEAD_DIM).astype(F32)
    return lo, 1.0 - lo


def _sgu_fwd(zuv, p, tag):
    s = zuv.shape[0]

    def body(z_ref, lng_ref, lnb_ref, wp_ref, bias_ref, oc_ref):
        lo, hi = _half_masks()
        _, _, u, _, _, vl = _sgu_front(z_ref[...], lng_ref[...], lnb_ref[...])
        mixed = _sgu_mix(vl, wp_ref, lo, hi) + jnp.tile(bias_ref[...], (TS // CHUNK, 1))
        oc_ref[...] = (u * mixed).astype(BF16)

    return pl.pallas_call(
        body, grid=(s // TS,),
        in_specs=[_row(TS, 2 * D_SGU), _const((1, D_SGU)), _const((1, D_SGU)),
                  _const((SGU_HEADS // 2, CHUNK, 2 * CHUNK)), _const((CHUNK, D_SGU))],
        out_specs=_row(TS, D_SGU),
        out_shape=jax.ShapeDtypeStruct((s, D_SGU), BF16),
        name=f"sgu_fwd_{tag}", compiler_params=_cp(dimension_semantics=("arbitrary",)),
    )(zuv, p["sgu_ln_g"], p["sgu_ln_b"], p["ws_pair"], p["bias_sp"])


def _blk_fwd(x0, oa, ob, oc, p, tag):
    s = x0.shape[0]
    ts = TS_FFN

    def body(x0_ref, oa_ref, ob_ref, oc_ref, wo_ref, g_ref, wg_ref, wu_ref, wd_ref,
             x1_ref, x2_ref, h2_ref, gt_ref, up_ref, ycat_ref):
        ycat = jnp.concatenate([oa_ref[...], ob_ref[...], oc_ref[...]], axis=1)
        ycat_ref[...] = ycat
        x1 = x0_ref[...] + _dot(ycat, wo_ref[...])
        x1_ref[...] = x1
        n, _ = _rms(x1)
        h2 = (n * g_ref[...]).astype(BF16)
        h2_ref[...] = h2
        gt = _dot(h2, wg_ref[...])
        up = _dot(h2, wu_ref[...])
        gt_ref[...] = gt.astype(BF16)
        up_ref[...] = up.astype(BF16)
        act = (gt * _sigmoid(gt) * up).astype(BF16)
        x2_ref[...] = x1 + _dot(act, wd_ref[...])

    return pl.pallas_call(
        body, grid=(s // ts,),
        in_specs=[_row(ts, D_MODEL), _row(ts, D_SSM), _row(ts, D_POOL), _row(ts, D_SGU),
                  _const((D_MODEL, D_MODEL)), _const((1, D_MODEL)), _const((D_MODEL, D_FF)),
                  _const((D_MODEL, D_FF)), _const((D_FF, D_MODEL))],
        out_specs=[_row(ts, D_MODEL), _row(ts, D_MODEL), _row(ts, D_MODEL), _row(ts, D_FF), _row(ts, D_FF),
                   _row(ts, D_MODEL)],
        out_shape=[jax.ShapeDtypeStruct((s, D_MODEL), F32), jax.ShapeDtypeStruct((s, D_MODEL), F32),
                   jax.ShapeDtypeStruct((s, D_MODEL), BF16), jax.ShapeDtypeStruct((s, D_FF), BF16),
                   jax.ShapeDtypeStruct((s, D_FF), BF16), jax.ShapeDtypeStruct((s, D_MODEL), BF16)],
        name=f"blk_fwd_{tag}", compiler_params=_cp(dimension_semantics=("arbitrary",)),
    )(x0, oa, ob, oc, p["w_out"], p["g_ffn"], p["w_gate"], p["w_up"], p["w_down"])


def _blk_bwd(dx2, x1, gt, up, p, tag):
    s = dx2.shape[0]
    ts = TS_FFN

    def body(dx2_ref, x1_ref, gt_ref, up_ref, wdt_ref, wgt_ref, wut_ref, wot_ref, g_ref,
             dx1_ref, da_ref, db_ref, dc_ref, dgt_ref, dup_ref, act_ref, dg_ref):
        @pl.when(pl.program_id(0) == 0)
        def _():
            dg_ref[...] = jnp.zeros_like(dg_ref)

        dx2v = dx2_ref[...]
        dact = _dot(dx2v.astype(BF16), wdt_ref[...])
        gf = gt_ref[...].astype(F32)
        uf = up_ref[...].astype(F32)
        sg = _sigmoid(gf)
        sl = gf * sg
        act_ref[...] = (sl * uf).astype(BF16)
        dgt = (dact * uf * (sg * (1.0 + gf * (1.0 - sg)))).astype(BF16)
        dup = (dact * sl).astype(BF16)
        dgt_ref[...] = dgt
        dup_ref[...] = dup
        dh2 = _dot(dgt, wgt_ref[...]) + _dot(dup, wut_ref[...])
        n, r = _rms(x1_ref[...])
        dxn, dgp = _rms_bwd(dh2, n, r, g_ref[...])
        dg_ref[...] += _colsum8(dgp)
        dx1 = dx2v + dxn
        dx1_ref[...] = dx1
        dy = _dot(dx1.astype(BF16), wot_ref[...])
        da_ref[...] = dy[:, :D_SSM]
        db_ref[...] = dy[:, D_SSM:D_SSM + D_POOL]
        dc_ref[...] = dy[:, D_SSM + D_POOL:]

    return pl.pallas_call(
        body, grid=(s // ts,),
        in_specs=[_row(ts, D_MODEL), _row(ts, D_MODEL), _row(ts, D_FF), _row(ts, D_FF),
                  _const((D_MODEL, D_FF)), _const((D_FF, D_MODEL)), _const((D_FF, D_MODEL)),
                  _const((D_MODEL, D_MODEL)), _const((1, D_MODEL))],
        out_specs=[_row(ts, D_MODEL), _row(ts, D_SSM), _row(ts, D_POOL), _row(ts, D_SGU), _row(ts, D_FF),
                   _row(ts, D_FF), _row(ts, D_FF), _acc((SUBLANES, D_MODEL))],
        out_shape=[jax.ShapeDtypeStruct((s, D_MODEL), F32), jax.ShapeDtypeStruct((s, D_SSM), F32),
                   jax.ShapeDtypeStruct((s, D_POOL), F32), jax.ShapeDtypeStruct((s, D_SGU), F32),
                   jax.ShapeDtypeStruct((s, D_FF), BF16), jax.ShapeDtypeStruct((s, D_FF), BF16),
                   jax.ShapeDtypeStruct((s, D_FF), BF16), jax.ShapeDtypeStruct((SUBLANES, D_MODEL), F32)],
        name=f"blk_bwd_{tag}", compiler_params=_cp(dimension_semantics=("arbitrary",)),
    )(dx2, x1, gt, up, p["w_down_t"], p["w_gate_t"], p["w_up_t"], p["w_out_t"], p["g_ffn"])


def _s5_bwd(dout, u, y, h_re, h_im, p, tag):
    s = u.shape[0]
    nt = s // TS
    seg = SEG

    def rev(n):
        return pl.BlockSpec((TS, n), lambda i: (nt - 1 - i, 0))

    def body(do_ref, u_ref, y_ref, hr_ref, hi_ref, ar_ref, ai_ref, cbr_ref, cbi_ref, bbr_ref, bbi_ref, dsk_ref,
             wglu_ref, wglut_ref, bglu_ref,
             du_ref, dctr_ref, dcti_ref, dbbr_ref, dbbi_ref, dar_ref, dai_ref, dd_ref, dwglu_ref, dbglu_ref,
             gr, gi, hsr, hsi, er, ei, jr, ji, cr, ci, stage, perm):
        @pl.when(pl.program_id(0) == 0)
        def _():
            for ref in (cr, ci, dctr_ref, dcti_ref, dbbr_ref, dbbi_ref, dar_ref, dai_ref, dd_ref, dwglu_ref,
                        dbglu_ref):
                ref[...] = jnp.zeros_like(ref)

        uv = _interleave_rows(u_ref[...], stage, perm)
        yv = y_ref[...]
        dov = _interleave_rows(do_ref[...], stage, perm)
        g = _gelu(yv)
        gb = g.astype(BF16)
        sg = _sigmoid(_dot(gb, wglu_ref[...]) + bglu_ref[...])
        dpre = dov * g * sg * (1.0 - sg)
        dpb = dpre.astype(BF16)
        dwglu_ref[...] += _dot_tn(gb, dpb)
        dbglu_ref[...] += _colsum8(dpre)
        dy = (dov * sg + _dot(dpb, wglut_ref[...])) * _gelu_grad(yv)
        dd_ref[...] += _colsum8(dy * uv)
        dyb = dy.astype(BF16)
        hr = hr_ref[...]
        hi = hi_ref[...]
        dctr_ref[...] += _dot_tn(hr.astype(BF16), dyb)
        dcti_ref[...] -= _dot_tn(hi.astype(BF16), dyb)
        _to_slabs(hsr, hr)
        _to_slabs(hsi, hi)
        _to_slabs(gr, _dot(dyb, cbr_ref[...]))
        _to_slabs(gi, -_dot(dyb, cbi_ref[...]))
        ar = ar_ref[...]
        ai = -ai_ref[...]

        def local(k, h):
            rows = _scan_rows(seg - 1 - k)
            nr, ni = _cmul(ar, ai, h[0], h[1])
            return nr + gr[:, rows, :], ni + gi[:, rows, :]

        zero = jnp.zeros((N_SLAB, SUBLANES, LANES), F32)
        e_r, e_i = lax.fori_loop(0, seg, local, (zero, zero))
        er[...] = e_r
        ei[...] = e_i
        pr, pi = _cpow(ar[:, 0:1, :], ai[:, 0:1, :], seg)
        c_r = cr[...]
        c_i = ci[...]
        for j in range(SUBLANES - 1, -1, -1):
            jr[:, j:j + 1, :] = c_r
            ji[:, j:j + 1, :] = c_i
            n_r, n_i = _cmul(pr, pi, c_r, c_i)
            c_r = n_r + er[:, j:j + 1, :]
            c_i = n_i + ei[:, j:j + 1, :]
        cr[...] = c_r
        ci[...] = c_i

        def full(k, carry):
            g_r, g_i, a_r, a_i = carry
            rows = _scan_rows(seg - 1 - k)
            h_r = hsr[:, rows, :]
            h_i = hsi[:, rows, :]
            a_r = a_r + g_r * h_r + g_i * h_i
            a_i = a_i + g_i * h_r - g_r * h_i
            nr, ni = _cmul(ar, ai, g_r, g_i)
            nr = nr + gr[:, rows, :]
            ni = ni + gi[:, rows, :]
            gr[:, rows, :] = nr
            gi[:, rows, :] = ni
            return nr, ni, a_r, a_i

        _, _, a_r, a_i = lax.fori_loop(0, seg, full, (jr[...], ji[...], zero, zero))
        dar_ref[...] += a_r
        dai_ref[...] += a_i
        gbr = _from_slabs(gr).astype(BF16)
        gbi = _from_slabs(gi).astype(BF16)
        ub = uv.astype(BF16)
        dbbr_ref[...] += _dot_tn(ub, gbr)
        dbbi_ref[...] += _dot_tn(ub, gbi)
        du = dy * dsk_ref[...] + _dot(gbr, bbr_ref[...]) + _dot(gbi, bbi_ref[...])
        du_ref[...] = _deinterleave_rows(du, stage, perm)

    slab = (N_SLAB, SUBLANES, LANES)
    big = (N_SLAB, TS, LANES)
    uslab = (N_USLAB, TS, LANES)
    return pl.pallas_call(
        body, grid=(nt,),
        in_specs=[rev(D_SSM), rev(D_SSM), rev(D_SSM), rev(N_STATE), rev(N_STATE), _const(slab), _const(slab),
                  _const((D_SSM, N_STATE)), _const((D_SSM, N_STATE)), _const((N_STATE, D_SSM)),
                  _const((N_STATE, D_SSM)), _const((1, D_SSM)), _const((D_SSM, D_SSM)), _const((D_SSM, D_SSM)),
                  _const((1, D_SSM))],
        out_specs=[rev(D_SSM), _acc((N_STATE, D_SSM)), _acc((N_STATE, D_SSM)), _acc((D_SSM, N_STATE)),
                   _acc((D_SSM, N_STATE)), _acc(slab), _acc(slab), _acc((SUBLANES, D_SSM)), _acc((D_SSM, D_SSM)),
                   _acc((SUBLANES, D_SSM))],
        out_shape=[jax.ShapeDtypeStruct((s, D_SSM), F32), jax.ShapeDtypeStruct((N_STATE, D_SSM), F32),
                   jax.ShapeDtypeStruct((N_STATE, D_SSM), F32), jax.ShapeDtypeStruct((D_SSM, N_STATE), F32),
                   jax.ShapeDtypeStruct((D_SSM, N_STATE), F32), jax.ShapeDtypeStruct(slab, F32),
                   jax.ShapeDtypeStruct(slab, F32), jax.ShapeDtypeStruct((SUBLANES, D_SSM), F32),
                   jax.ShapeDtypeStruct((D_SSM, D_SSM), F32), jax.ShapeDtypeStruct((SUBLANES, D_SSM), F32)],
        scratch_shapes=[pltpu.VMEM(big, F32), pltpu.VMEM(big, F32), pltpu.VMEM(big, F32), pltpu.VMEM(big, F32),
                        pltpu.VMEM(slab, F32), pltpu.VMEM(slab, F32), pltpu.VMEM(slab, F32), pltpu.VMEM(slab, F32),
                        pltpu.VMEM((N_SLAB, 1, LANES), F32), pltpu.VMEM((N_SLAB, 1, LANES), F32),
                        pltpu.VMEM(uslab, F32), pltpu.VMEM(uslab, F32)],
        name=f"s5_bwd_{tag}", compiler_params=_cp(dimension_semantics=("arbitrary",)),
    )(dout, u, y, h_re, h_im, p["a_re8"], p["a_im8"], p["cb_re"], p["cb_im"], p["bb_re"], p["bb_im"], p["d_skip"],
      p["w_glu"], p["w_glu_t"], p["b_glu"])


def _pool_bwd(dout, pooled, p, tag):
    s = dout.shape[0]
    nt = s // TS
    hb = TS // MAX_WINDOW

    def halo(n):
        return pl.BlockSpec((MAX_WINDOW, n), lambda i: (jnp.minimum((i + 1) * hb, nt * hb - 1), 0))

    def body(do_ref, po_ref, doh_ref, wl_ref, w_ref, wt_ref, sc_ref, dz_ref, dw_ref, dsc_ref, buf):
        i = pl.program_id(0)

        @pl.when(i == 0)
        def _():
            dw_ref[...] = jnp.zeros_like(dw_ref)
            dsc_ref[...] = jnp.zeros_like(dsc_ref)

        wl = wl_ref[...]
        sc = sc_ref[...]
        dov = do_ref[...]
        pooled_b = po_ref[...]
        dsc_ref[...] += _colsum8(dov * _dot(pooled_b, w_ref[...]))
        dmix = (dov * sc).astype(BF16)
        dw_ref[...] += _dot_tn(pooled_b, dmix)
        dpool = _dot(dmix, wt_ref[...])
        dpool_h = _dot((doh_ref[...] * sc).astype(BF16), wt_ref[...])
        buf[pl.ds(0, TS), :] = dpool / _pool_count(i, TS, wl)
        buf[pl.ds(TS, MAX_WINDOW), :] = jnp.where(i < nt - 1, dpool_h / _pool_count(i, MAX_WINDOW, wl, TS), 0.0)
        dz_ref[...] = _window_sum(buf, 0, TS, wl, 1) - dpool

    return pl.pallas_call(
        body, grid=(nt,),
        in_specs=[_row(TS, D_POOL), _row(TS, D_POOL), halo(D_POOL), _const((1, D_POOL)), _const((D_POOL, D_POOL)),
                  _const((D_POOL, D_POOL)), _const((1, D_POOL))],
        out_specs=[_row(TS, D_POOL), _acc((D_POOL, D_POOL)), _acc((SUBLANES, D_POOL))],
        out_shape=[jax.ShapeDtypeStruct((s, D_POOL), F32), jax.ShapeDtypeStruct((D_POOL, D_POOL), F32),
                   jax.ShapeDtypeStruct((SUBLANES, D_POOL), F32)],
        scratch_shapes=[pltpu.VMEM((TS + MAX_WINDOW, D_POOL), F32)],
        name=f"pool_bwd_{tag}", compiler_params=_cp(dimension_semantics=("arbitrary",)),
    )(dout, pooled, dout, _pool_consts(), p["w_pool_bd"], p["w_pool_bd_t"], p["pool_scale"])


def _sgu_bwd(dout, zuv, p, tag):
    s = zuv.shape[0]

    def body(do_ref, z_ref, lng_ref, lnb_ref, wp_ref, wpt_ref, bias_ref,
             dz_ref, dws_ref, dbias_ref, dlng_ref, dlnb_ref):
        @pl.when(pl.program_id(0) == 0)
        def _():
            for ref in (dws_ref, dbias_ref, dlng_ref, dlnb_ref):
                ref[...] = jnp.zeros_like(ref)

        lo, hi = _half_masks()
        lng = lng_ref[...]
        zu, zv, u, vn, rs, vl = _sgu_front(z_ref[...], lng, lnb_ref[...])
        mixed = _sgu_mix(vl, wp_ref, lo, hi) + jnp.tile(bias_ref[...], (TS // CHUNK, 1))
        dov = do_ref[...]
        dzu = dov * mixed * _gelu_grad(zu)
        dmix = dov * u
        dbias = dbias_ref[...]
        for c in range(TS // CHUNK):
            dmc = dmix[CHUNK * c:CHUNK * (c + 1), :]
            dbias = dbias + dmc
            vlc = vl[CHUNK * c:CHUNK * (c + 1), :].astype(BF16)
            for q in range(SGU_HEADS // 2):
                dq = dmc[:, LANES * q:LANES * (q + 1)]
                vq = vlc[:, LANES * q:LANES * (q + 1)]
                dws_ref[2 * q] += _dot_nt((dq * lo).astype(BF16), vq)
                dws_ref[2 * q + 1] += _dot_nt((dq * hi).astype(BF16), vq)
        dbias_ref[...] = dbias
        dvl = _sgu_mix(dmix, wpt_ref, lo, hi)
        dlng_ref[...] += _colsum8(dvl * vn)
        dlnb_ref[...] += _colsum8(dvl)
        dvn = dvl * lng
        dv = rs * (dvn - jnp.mean(dvn, axis=-1, keepdims=True) - vn * jnp.mean(dvn * vn, axis=-1, keepdims=True))
        dz_ref[...] = jnp.concatenate([dzu, dv * _gelu_grad(zv)], axis=1)

    return pl.pallas_call(
        body, grid=(s // TS,),
        in_specs=[_row(TS, D_SGU), _row(TS, 2 * D_SGU), _const((1, D_SGU)), _const((1, D_SGU)),
                  _const((SGU_HEADS // 2, CHUNK, 2 * CHUNK)), _const((SGU_HEADS // 2, CHUNK, 2 * CHUNK)),
                  _const((CHUNK, D_SGU))],
        out_specs=[_row(TS, 2 * D_SGU), _acc((SGU_HEADS, CHUNK, CHUNK)), _acc((CHUNK, D_SGU)),
                   _acc((SUBLANES, D_SGU)), _acc((SUBLANES, D_SGU))],
        out_shape=[jax.ShapeDtypeStruct((s, 2 * D_SGU), F32), jax.ShapeDtypeStruct((SGU_HEADS, CHUNK, CHUNK), F32),
                   jax.ShapeDtypeStruct((CHUNK, D_SGU), F32), jax.ShapeDtypeStruct((SUBLANES, D_SGU), F32),
                   jax.ShapeDtypeStruct((SUBLANES, D_SGU), F32)],
        name=f"sgu_bwd_{tag}", compiler_params=_cp(dimension_semantics=("arbitrary",)),
    )(dout, zuv, p["sgu_ln_g"], p["sgu_ln_b"], p["ws_pair"], p["ws_pair_t"], p["bias_sp"])


def _mix_in_bwd(dza, dzb, dzuv, x0, dx1, p, tag):
    s = x0.shape[0]

    def body(da_ref, db_ref, dc_ref, x_ref, dx1_ref, wt_ref, g_ref, dx0_ref, dz_ref, dg_ref):
        @pl.when(pl.program_id(0) == 0)
        def _():
            dg_ref[...] = jnp.zeros_like(dg_ref)

        dz = jnp.concatenate([da_ref[...], db_ref[...], dc_ref[...]], axis=1).astype(BF16)
        dz_ref[...] = dz
        n, r = _rms(x_ref[...])
        dxn, dgp = _rms_bwd(_dot(dz, wt_ref[...]), n, r, g_ref[...])
        dg_ref[...] += _colsum8(dgp)
        dx0_ref[...] = dx1_ref[...] + dxn

    return pl.pallas_call(
        body, grid=(s // TS,),
        in_specs=[_row(TS, D_SSM), _row(TS, D_POOL), _row(TS, 2 * D_SGU), _row(TS, D_MODEL), _row(TS, D_MODEL),
                  _const((D_IN, D_MODEL)), _const((1, D_MODEL))],
        out_specs=[_row(TS, D_MODEL), _row(TS, D_IN), _acc((SUBLANES, D_MODEL))],
        out_shape=[jax.ShapeDtypeStruct((s, D_MODEL), F32), jax.ShapeDtypeStruct((s, D_IN), BF16),
                   jax.ShapeDtypeStruct((SUBLANES, D_MODEL), F32)],
        name=f"mix_in_bwd_{tag}", compiler_params=_cp(dimension_semantics=("arbitrary",)),
    )(dza, dzb, dzuv, x0, dx1, p["w_in_t"], p["g_mix"])


def _head(x, target, g):
    s = x.shape[0]

    def body(x_ref, t_ref, g_ref, dx_ref, loss_ref, dg_ref):
        @pl.when(pl.program_id(0) == 0)
        def _():
            loss_ref[...] = jnp.zeros_like(loss_ref)
            dg_ref[...] = jnp.zeros_like(dg_ref)

        gv = g_ref[...]
        n, r = _rms(x_ref[...])
        diff = n * gv - t_ref[...]
        loss_ref[...] += jnp.sum(diff * diff) * (0.5 / D_MODEL)
        dxn, dgp = _rms_bwd(diff * (1.0 / D_MODEL), n, r, gv)
        dg_ref[...] += _colsum8(dgp)
        dx_ref[...] = dxn

    return pl.pallas_call(
        body, grid=(s // TS,),
        in_specs=[_row(TS, D_MODEL), _row(TS, D_MODEL), _const((1, D_MODEL))],
        out_specs=[_row(TS, D_MODEL), _acc((SUBLANES, LANES)), _acc((SUBLANES, D_MODEL))],
        out_shape=[jax.ShapeDtypeStruct((s, D_MODEL), F32), jax.ShapeDtypeStruct((SUBLANES, LANES), F32),
                   jax.ShapeDtypeStruct((SUBLANES, D_MODEL), F32)],
        name="head", compiler_params=_cp(dimension_semantics=("arbitrary",)),
    )(x, target, g)


def _atb(a, b, tn, tag):
    s, ka = a.shape
    kb = b.shape[1]
    ns = s // TS

    def body(a_ref, b_ref, o_ref):
        @pl.when(pl.program_id(1) == 0)
        def _():
            o_ref[...] = jnp.zeros_like(o_ref)

        o_ref[...] += _dot_tn(a_ref[...].astype(BF16), b_ref[...].astype(BF16))

    return pl.pallas_call(
        body, grid=(kb // tn, ns),
        in_specs=[pl.BlockSpec((TS, ka), lambda j, i: (i, 0)), pl.BlockSpec((TS, tn), lambda j, i: (i, j))],
        out_specs=pl.BlockSpec((ka, tn), lambda j, i: (0, j)),
        out_shape=jax.ShapeDtypeStruct((ka, kb), F32),
        name=f"atb_{tag}", compiler_params=_cp(dimension_semantics=("arbitrary", "arbitrary")),
    )(a, b)


def _s5_discretise(a_re, a_im, log_dt, b_re, b_im):
    dt = jnp.exp(log_dt)[:, None]
    mag = jnp.exp(a_re * dt)
    ar = mag * jnp.cos(a_im * dt)
    ai = mag * jnp.sin(a_im * dt)
    den = a_re * a_re + a_im * a_im
    f_re = ((ar - 1.0) * a_re + ai * a_im) / den
    f_im = (ai * a_re - (ar - 1.0) * a_im) / den
    bb_re = f_re[..., None] * b_re - f_im[..., None] * b_im
    bb_im = f_re[..., None] * b_im + f_im[..., None] * b_re
    return ar, ai, bb_re, bb_im


def _block_diag(blocks):
    g, r, c = blocks.shape
    eye = jnp.eye(g, dtype=blocks.dtype)
    return (blocks[:, :, None, :] * eye[:, None, :, None]).reshape(g * r, g * c)


def _block_diag_extract(m, g):
    r = m.shape[0] // g
    c = m.shape[1] // g
    eye = jnp.eye(g, dtype=m.dtype)
    return jnp.sum(m.reshape(g, r, g, c) * eye[:, None, :, None], axis=2)


def _state_slabs(v):
    return jnp.broadcast_to(v.reshape(N_SLAB, 1, LANES), (N_SLAB, SUBLANES, LANES))


def _tril():
    return jnp.tril(jnp.ones((CHUNK, CHUNK), dtype=bool))


def _layer_params(w, l):
    row = lambda v: v.reshape(1, -1)
    t = lambda m: jnp.swapaxes(m, -1, -2)
    ar, ai, bb_re, bb_im = _s5_discretise(w["A_re"][l], w["A_im"][l], w["log_dt"][l], w["B_re"][l], w["B_im"][l])
    bbt_re = _block_diag(t(bb_re)).astype(BF16)
    bbt_im = _block_diag(t(bb_im)).astype(BF16)
    ct_re = _block_diag(t(w["C_re"][l])).astype(BF16)
    ct_im = _block_diag(t(w["C_im"][l])).astype(BF16)
    ws = jnp.where(_tril()[None], w["w_spatial"][l], 0.0)
    pair = lambda m: jnp.stack([jnp.concatenate([m[2 * q], m[2 * q + 1]], axis=1)
                                for q in range(SGU_HEADS // 2)]).astype(BF16)
    wp = _block_diag(w["w_pool"][l]).astype(BF16)
    p = dict(
        g_mix=row(w["g_mix"][l]), g_ffn=row(w["g_ffn"][l]), d_skip=row(w["D_skip"][l]), b_glu=row(w["b_glu"][l]),
        pool_scale=row(w["pool_scale"][l]), sgu_ln_g=row(w["sgu_ln_g"][l]), sgu_ln_b=row(w["sgu_ln_b"][l]),
        a_re8=_state_slabs(ar), a_im8=_state_slabs(ai),
        bbt_re=bbt_re, bbt_im=bbt_im, bb_re=t(bbt_re), bb_im=t(bbt_im),
        ct_re=ct_re, ct_im=ct_im, cb_re=t(ct_re), cb_im=t(ct_im),
        w_pool_bd=wp, w_pool_bd_t=t(wp), ws_pair=pair(ws), ws_pair_t=pair(t(ws)),
        bias_sp=jnp.repeat(t(w["b_spatial"][l]), SGU_HEAD_DIM, axis=1),
    )
    return p


MIX_WEIGHTS = ("w_in", "w_glu")
FFN_WEIGHTS = ("w_out", "w_gate", "w_up", "w_down")


def _with_big(p, mats):
    t = lambda m: jnp.swapaxes(m, -1, -2)
    for n, m in mats.items():
        p[n], p[n + "_t"] = (t(m), m) if n in TRANSPOSED else (m, t(m))


def _rows_sum(v):
    return jnp.sum(v, axis=0)


ATB_COLS = 512


def _after(v, token):
    return v if token is None else v + token


def _layer_bwd(dx2, sv, p, w, l, tag, on_grads, token):
    t = lambda m: jnp.swapaxes(m, -1, -2)
    dx1, da, db, dc, dgt, dup, act, dg_ffn = _blk_bwd(dx2, sv["x1"], sv["gt"], sv["up"],
                                                      dict(p, g_ffn=_after(p["g_ffn"], token)), tag)
    token = on_grads(l, "ffn", {
        "w_down": _atb(act, dx2, ATB_COLS, tag + "_wd"), "w_gate": _atb(dgt, sv["h2"], ATB_COLS, tag + "_wg"),
        "w_up": _atb(dup, sv["h2"], ATB_COLS, tag + "_wu"), "w_out": _atb(sv["ycat"], dx1, ATB_COLS, tag + "_wo")})
    g = {}
    g["g_ffn"] = _rows_sum(dg_ffn)
    dza, dct_re, dct_im, dbbt_re, dbbt_im, dar8, dai8, dd8, dwglu, dbglu8 = _s5_bwd(
        da, sv["za"], sv["y"], sv["h_re"], sv["h_im"], dict(p, d_skip=_after(p["d_skip"], token)), tag)
    dzb, dwp, dsc8 = _pool_bwd(db, sv["pooled"], p, tag)
    dzuv, dws, dbias, dlng8, dlnb8 = _sgu_bwd(dc, sv["zuv"], p, tag)
    dx0, dz, dg_mix = _mix_in_bwd(dza, dzb, dzuv, sv["x0"], dx1, p, tag)
    token = on_grads(l, "mix", {"w_in": _atb(dz, sv["h1"], D_MODEL, tag + "_wi"), "w_glu": dwglu})
    g["g_mix"] = _rows_sum(dg_mix)
    g["b_glu"] = _rows_sum(dbglu8)
    g["D_skip"] = _rows_sum(dd8)
    g["C_re"] = t(_block_diag_extract(dct_re, N_GROUPS))
    g["C_im"] = t(_block_diag_extract(dct_im, N_GROUPS))
    dar = jnp.sum(dar8, axis=1).reshape(N_GROUPS, SSM_STATE)
    dai = jnp.sum(dai8, axis=1).reshape(N_GROUPS, SSM_STATE)
    dbb_re = t(_block_diag_extract(dbbt_re, N_GROUPS))
    dbb_im = t(_block_diag_extract(dbbt_im, N_GROUPS))
    _, disc_vjp = jax.vjp(_s5_discretise, w["A_re"][l], w["A_im"][l], w["log_dt"][l], w["B_re"][l], w["B_im"][l])
    g["A_re"], g["A_im"], g["log_dt"], g["B_re"], g["B_im"] = disc_vjp((dar, dai, dbb_re, dbb_im))
    g["w_pool"] = _block_diag_extract(dwp, len(POOL_WINDOWS))
    g["pool_scale"] = _rows_sum(dsc8)
    g["sgu_ln_g"] = _rows_sum(dlng8)
    g["sgu_ln_b"] = _rows_sum(dlnb8)
    g["w_spatial"] = jnp.where(_tril()[None], dws, 0.0)
    g["b_spatial"] = t(jnp.sum(dbias.reshape(CHUNK, SGU_HEADS, SGU_HEAD_DIM), axis=-1))
    return dx0, g, token


def _local_step(x, target, w, get_big, on_grads):
    params = [_layer_params(w, l) for l in range(DEPTH)]
    saved = []
    h = x
    for l in range(DEPTH):
        p, tag = params[l], f"l{l}"
        _with_big(p, get_big(l, "mix", [h]))
        za, zb, zuv, h1 = _mix_in_fwd(h, p["g_mix"], p["w_in"], tag)
        oa, y, h_re, h_im = _s5_fwd(za, p, tag)
        ob, pooled = _pool_fwd(zb, p, tag)
        oc = _sgu_fwd(zuv, p, tag)
        _with_big(p, get_big(l, "ffn", [oa, ob, oc]))
        x1, x2, h2, gt, up, ycat = _blk_fwd(h, oa, ob, oc, p, tag)
        saved.append(dict(x0=h, za=za, zuv=zuv, h1=h1, ycat=ycat, y=y, h_re=h_re, h_im=h_im, pooled=pooled, x1=x1,
                          h2=h2, gt=gt, up=up))
        h = x2
    dx, loss8, dgf8 = _head(h, target, w["g_final"].reshape(1, -1))
    grads = [None] * DEPTH
    token = None
    for l in reversed(range(DEPTH)):
        dx, grads[l], token = _layer_bwd(dx, saved[l], params[l], w, l, f"l{l}", on_grads, token)
    g = {n: jnp.stack([grads[l][n] for l in range(DEPTH)]) for n in SMALL if n != "g_final"}
    g["g_final"] = _rows_sum(dgf8)
    return loss8[0, 0], dx, g


_ANY = pl.BlockSpec(memory_space=pl.ANY)
_MESH = pl.DeviceIdType.MESH


def _place():
    return lax.axis_index("x"), lax.axis_index("y"), lax.axis_index("c")


def _other_chips(x, y):
    return [(1 - x, y), (x, 1 - y), (1 - x, 1 - y)]


def _allgather8(blk, tag):
    m, n = blk.shape

    def body(x_ref, out_ref, send_sems, recv_sems, local_sem):
        x, y, c = _place()
        me, sibling = (x, y, c), (x, y, 1 - c)
        chips = _other_chips(x, y)
        mine_src = x_ref

        def slot(px, py, pc):
            return out_ref.at[4 * px + 2 * py + pc]

        def copy(k, block, to, src=None):
            return pltpu.make_async_remote_copy(
                src_ref=slot(*block) if src is None else src, dst_ref=slot(*block),
                send_sem=send_sems.at[k], recv_sem=recv_sems.at[k], device_id=to, device_id_type=_MESH)

        mine = pltpu.make_async_copy(mine_src, slot(*me), local_sem)
        mine.start()
        first = [copy(0, me, sibling, src=mine_src)]
        first += [copy(1 + j, me, (*chip, c), src=mine_src) for j, chip in enumerate(chips)]
        for cp in first:
            cp.start()
        passed = [copy(4 + j, (*chip, c), sibling) for j, chip in enumerate(chips)]
        for j, chip in enumerate(chips):
            copy(1 + j, (*chip, c), me).wait_recv()
            passed[j].start()
        copy(0, sibling, me).wait_recv()
        for j, chip in enumerate(chips):
            copy(4 + j, (*chip, 1 - c), me).wait_recv()
        for cp in first + passed:
            cp.wait_send()
        mine.wait()

    return pl.pallas_call(
        body, out_shape=jax.ShapeDtypeStruct((N_DEV, m, n), blk.dtype), in_specs=[_ANY], out_specs=_ANY,
        scratch_shapes=[pltpu.SemaphoreType.DMA((7,)), pltpu.SemaphoreType.DMA((7,)), pltpu.SemaphoreType.DMA],
        name=f"allgather8_{tag}",
    )(blk)


def _dma_sems(n):
    return pltpu.SemaphoreType.DMA((n,))


def _remote(src, dst, send_sems, recv_sems, k, to):
    return pltpu.make_async_remote_copy(src_ref=src, dst_ref=dst, send_sem=send_sems.at[k], recv_sem=recv_sems.at[k],
                                        device_id=to, device_id_type=_MESH)


_HBM = pl.BlockSpec(memory_space=pltpu.HBM)
_SEM = pl.BlockSpec(memory_space=pltpu.SEMAPHORE)
_EFFECT = pltpu.SideEffectType.DATAFLOW_SIDE_EFFECTING
N_REL = N_CHIPS - 1


def _gather_plan(x, y, c, srcs, lands):
    me = 2 * x + y
    return [(s, l.at[pl.ds(me * s.shape[0], s.shape[0])], (cx, cy, c))
            for s, l in zip(srcs, lands) for cx, cy in _other_chips(x, y)]


def _slab_plan(x, y, c, srcs, lands):
    return [(s.at[2 * cx + cy], l.at[j], (cx, cy, c))
            for s, l in zip(srcs, lands) for j, (cx, cy) in enumerate(_other_chips(x, y))]


def _plan_copies(plan, srcs, lands, send_sems, recv_sems):
    x, y, c = _place()
    return [_remote(s, d, send_sems, recv_sems, k, to) for k, (s, d, to) in enumerate(plan(x, y, c, srcs, lands))]


def _hbm(a):
    return pltpu.with_memory_space_constraint(a, pltpu.HBM)


def _copies_start(name, plan, srcs, lands):
    n = len(srcs)
    ncopies = N_REL * n

    def body(*refs):
        for cp in _plan_copies(plan, refs[:n], refs[n:2 * n], refs[2 * n], refs[2 * n + 1]):
            cp.start()
        refs[-1][...] = jnp.zeros_like(refs[-1])

    ref_out = [pltpu.HBM(a.shape, a.dtype) for a in (*srcs, *lands)]
    out = pl.pallas_call(
        body, name=name, in_specs=[_HBM] * (2 * n),
        out_shape=(_dma_sems(ncopies), _dma_sems(ncopies), *ref_out, jax.ShapeDtypeStruct((SUBLANES, LANES), F32)),
        out_specs=(_SEM, _SEM, *[_HBM] * (2 * n), pl.BlockSpec(memory_space=pltpu.VMEM)),
        input_output_aliases={i: 2 + i for i in range(2 * n)},
        compiler_params=pltpu.CompilerParams(has_side_effects=_EFFECT),
    )(*[_hbm(a) for a in (*srcs, *lands)])
    return dict(name=name, plan=plan, sems=out[:2], srcs=out[2:2 + n], lands=out[2 + n:2 + 2 * n], token=out[-1][0, 0])


def _copies_wait(started, after):
    n = len(started["srcs"])
    plan = started["plan"]

    def body(*refs):
        for cp in _plan_copies(plan, refs[:n], refs[n:2 * n], refs[2 * n], refs[2 * n + 1]):
            cp.wait_send()
            cp.wait_recv()

    args = (*started["srcs"], *started["lands"])
    out = pl.pallas_call(
        body, name=started["name"] + "_wait", out_shape=[pltpu.HBM(a.shape, a.dtype) for a in args],
        in_specs=[_HBM] * (2 * n) + [_SEM, _SEM] + [_ANY] * len(after), out_specs=[_HBM] * (2 * n),
        input_output_aliases={i: i for i in range(2 * n)},
        compiler_params=pltpu.CompilerParams(has_side_effects=_EFFECT),
    )(*args, *started["sems"], *after)
    return out[n:]


def _swap_halves(g4s, tag):
    nw = len(g4s)

    def body(*refs):
        ins, outs = refs[:nw], refs[nw:2 * nw]
        send_sems, recv_sems = refs[2 * nw:]
        x, y, c = _place()
        copies = [_remote(ins[i].at[:, 1 - c], outs[i], send_sems, recv_sems, i, (x, y, 1 - c)) for i in range(nw)]
        for cp in copies:
            cp.start()
        for cp in copies:
            cp.wait()

    return pl.pallas_call(
        body, out_shape=[jax.ShapeDtypeStruct((g.shape[0],) + g.shape[2:], g.dtype) for g in g4s],
        in_specs=[_ANY] * nw, out_specs=[_ANY] * nw, scratch_shapes=[_dma_sems(nw), _dma_sems(nw)],
        name=f"swap_halves_{tag}",
    )(*g4s)


def _share_halves(fs, layer, tag):
    nw = len(fs)

    def body(*refs):
        ins = refs[:nw]
        send_sems, recv_sems = refs[2 * nw:]
        x, y, c = _place()

        def half(i, who):
            h = ins[i].shape[1] // 2
            return ins[i].at[layer, pl.ds(who * h, h)]

        sends = [_remote(half(i, c), half(i, c), send_sems, recv_sems, i, (x, y, 1 - c)) for i in range(nw)]
        for cp in sends:
            cp.start()
        for i in range(nw):
            sends[i].wait_send()
            _remote(half(i, c), half(i, 1 - c), send_sems, recv_sems, i, (x, y, 1 - c)).wait_recv()

    return pl.pallas_call(
        body, out_shape=[jax.ShapeDtypeStruct(f.shape, f.dtype) for f in fs], in_specs=[_ANY] * nw,
        out_specs=[_ANY] * nw, input_output_aliases={i: i for i in range(nw)},
        scratch_shapes=[_dma_sems(nw), _dma_sems(nw)], name=f"share_halves_{tag}",
    )(*fs)


def _add_halves(g4s, recvs, sel, tag):
    nw = len(g4s)

    def body(sel_ref, *refs):
        for i in range(nw):
            refs[2 * nw + i][...] = (refs[i][...] + refs[nw + i][...]).astype(BF16)

    mine = [pl.BlockSpec((None, None) + g.shape[2:], lambda k, s: (k, s[0], 0, 0)) for g in g4s]
    slab = [pl.BlockSpec((None,) + g.shape[2:], lambda k, s: (k, 0, 0)) for g in g4s]
    return pl.pallas_call(
        body, grid_spec=pltpu.PrefetchScalarGridSpec(num_scalar_prefetch=1, grid=(N_CHIPS,), in_specs=mine + slab,
                                                     out_specs=slab),
        out_shape=[jax.ShapeDtypeStruct(r.shape, BF16) for r in recvs], name=f"add_halves_{tag}",
        compiler_params=_cp(dimension_semantics=("arbitrary",)),
    )(sel, *g4s, *recvs)


def _add_chips(ps, slabs, fs, layer, sel, tag):
    nw = len(ps)
    old = [f for f in fs if f is not None]

    def body(sel_ref, *refs):
        outs = refs[2 * nw + len(old):]
        for i in range(nw):
            acc = refs[i][...].astype(F32)
            for j in range(N_REL):
                acc = acc + refs[nw + i][j].astype(F32)
            outs[i][...] = acc

    shapes = [(DEPTH, 2 * p.shape[1], p.shape[2]) for p in ps]
    in_specs = [pl.BlockSpec((None,) + p.shape[1:], lambda i, s: (s[1], 0, 0)) for p in ps]
    in_specs += [pl.BlockSpec(sl.shape, lambda i, s: (0, 0, 0)) for sl in slabs]
    in_specs += [_ANY] * len(old)
    first_old = 1 + 2 * nw
    aliases, k = {}, 0
    for i, f in enumerate(fs):
        if f is not None:
            aliases[first_old + k] = i
            k += 1
    return pl.pallas_call(
        body, grid_spec=pltpu.PrefetchScalarGridSpec(
            num_scalar_prefetch=1, grid=(1,), in_specs=in_specs,
            out_specs=[pl.BlockSpec((None,) + p.shape[1:], lambda i, s: (layer, s[0], 0)) for p in ps]),
        out_shape=[jax.ShapeDtypeStruct(sh, F32) for sh in shapes], input_output_aliases=aliases,
        name=f"add_chips_{tag}", compiler_params=_cp(dimension_semantics=("arbitrary",)),
    )(sel, *ps, *slabs, *old)


def _adamw_math(w, g, m, v):
    m = ADAM_B1 * m + (1.0 - ADAM_B1) * g
    v = ADAM_B2 * v + (1.0 - ADAM_B2) * (g * g)
    m_hat = m / (1.0 - ADAM_B1 ** ADAM_STEP)
    v_hat = v / (1.0 - ADAM_B2 ** ADAM_STEP)
    delta = -ADAM_LR * (m_hat / (jnp.sqrt(v_hat) + ADAM_EPS) + ADAM_WD * w)
    return delta, m, v


ADAM_ROWS = 64


def _adamw(w, g, m, v, tag):
    shape = w.shape
    cols = shape[-1]
    flat = lambda a: a.reshape(-1, cols)
    rows = w.size // cols
    tr = ADAM_ROWS

    def body(w_ref, g_ref, m_ref, v_ref, d_ref, nm_ref, nv_ref):
        d, nm, nv = _adamw_math(w_ref[...], g_ref[...], m_ref[...], v_ref[...])
        d_ref[...] = d
        nm_ref[...] = nm
        nv_ref[...] = nv

    spec = _row(tr, cols)
    out = pl.pallas_call(
        body, grid=(rows // tr,), in_specs=[spec] * 4, out_specs=[spec] * 3,
        out_shape=[jax.ShapeDtypeStruct((rows, cols), F32)] * 3, name=f"adamw_{tag}",
        compiler_params=_cp(dimension_semantics=("arbitrary",)),
    )(flat(w), flat(g), flat(m), flat(v))
    return [o.reshape(shape) for o in out]


SMALL_ROWS = 3584
SMALL_TILE = 512


def _small_reduce_adamw(gathered, w, m, v):
    tr = SMALL_TILE

    def body(ga_ref, w_ref, m_ref, v_ref, g_ref, d_ref, nm_ref, nv_ref):
        g = ga_ref[0]
        for k in range(1, N_DEV):
            g = g + ga_ref[k]
        g_ref[...] = g
        d, nm, nv = _adamw_math(w_ref[...], g, m_ref[...], v_ref[...])
        d_ref[...] = d
        nm_ref[...] = nm
        nv_ref[...] = nv

    spec = _row(tr, LANES)
    return pl.pallas_call(
        body, grid=(SMALL_ROWS // tr,),
        in_specs=[pl.BlockSpec((N_DEV, tr, LANES), lambda i: (0, i, 0)), spec, spec, spec], out_specs=[spec] * 4,
        out_shape=[jax.ShapeDtypeStruct((SMALL_ROWS, LANES), F32)] * 4, name="small_reduce_adamw",
        compiler_params=_cp(dimension_semantics=("arbitrary",)),
    )(gathered, w, m, v)


def _exchange_form(n, a):
    return jnp.swapaxes(a, 1, 2) if n in TRANSPOSED else a


def _pack_small(vals, loss):
    flat = jnp.concatenate([vals[n].reshape(-1) for n in SMALL] + [loss.reshape(1)])
    return jnp.pad(flat, (0, SMALL_ROWS * LANES - flat.size)).reshape(SMALL_ROWS, LANES)


def _unpack_small(buf, like):
    flat = buf.reshape(-1)
    out, off = {}, 0
    for n in SMALL:
        out[n] = flat[off:off + like[n].size].reshape(like[n].shape)
        off += like[n].size
    return out, flat[off]


def kernel(x, g_mix, w_in, A_re, A_im, log_dt, B_re, B_im, C_re, C_im, D_skip, w_glu, b_glu, w_pool, pool_scale, sgu_ln_g, sgu_ln_b, w_spatial, b_spatial, w_out, g_ffn, w_gate, w_up, w_down, g_final, loss_target, m_g_mix, m_w_in, m_A_re, m_A_im, m_log_dt, m_B_re, m_B_im, m_C_re, m_C_im, m_D_skip, m_w_glu, m_b_glu, m_w_pool, m_pool_scale, m_sgu_ln_g, m_sgu_ln_b, m_w_spatial, m_b_spatial, m_w_out, m_g_ffn, m_w_gate, m_w_up, m_w_down, m_g_final, v_g_mix, v_w_in, v_A_re, v_A_im, v_log_dt, v_B_re, v_B_im, v_C_re, v_C_im, v_D_skip, v_w_glu, v_b_glu, v_w_pool, v_pool_scale, v_sgu_ln_g, v_sgu_ln_b, v_w_spatial, v_b_spatial, v_w_out, v_g_ffn, v_w_gate, v_w_up, v_w_down, v_g_final):
    loc = locals()
    w = {n: loc[n] for n in WEIGHTS}
    m = {n: loc["m_" + n] for n in WEIGHTS}
    v = {n: loc["v_" + n] for n in WEIGHTS}
    sel = jnp.stack([lax.axis_index("c"), 2 * lax.axis_index("x") + lax.axis_index("y")]).astype(jnp.int32)

    chip = sel[1]

    halves = [(l, half) for l in range(DEPTH) for half in ("mix", "ffn")]
    names = {"mix": MIX_WEIGHTS, "ffn": FFN_WEIGHTS}
    started = {}
    for l, half in halves:
        shards = [_exchange_form(n, w[n])[l].astype(BF16) for n in names[half]]
        lands = [lax.dynamic_update_slice(lax.empty((N_CHIPS * s.shape[0], s.shape[1]), BF16), s,
                                          (chip * s.shape[0], 0)) for s in shards]
        started[l, half] = _copies_start(f"weights_l{l}_{half}", _gather_plan, shards, lands)
    w = dict(w, g_mix=_after(w["g_mix"], sum(st["token"] for st in started.values())))

    def get_big(l, half, after):
        return dict(zip(names[half], _copies_wait(started[l, half], after)))

    result = {n: None for n in BIG}
    in_flight = []

    def finish(after):
        for part, ex, ns, l, tag in in_flight:
            bufs = _add_chips(part, _copies_wait(ex, after), [result[n] for n in ns], l, sel, tag)
            for n, f in zip(ns, _share_halves(bufs, l, tag)):
                result[n] = f
        in_flight.clear()

    def on_grads(l, half, grads):
        ns = list(grads)
        tag = f"l{l}_{half}"
        finish([grads[ns[0]]])
        g4s = [grads[n].reshape(N_CHIPS, 2, grads[n].shape[0] // (2 * N_CHIPS), grads[n].shape[1]) for n in ns]
        part = _add_halves(g4s, _swap_halves(g4s, tag), sel, tag)
        slabs = [lax.empty((N_REL,) + p.shape[1:], BF16) for p in part]
        ex = _copies_start(f"grads_{tag}", _slab_plan, part, slabs)
        in_flight.append((part, ex, ns, l, tag))
        return ex["token"]

    loss_local, dx, g = _local_step(x[0], loss_target[0], w, get_big, on_grads)
    finish([])
    grads, deltas, new_m, new_v = {}, {}, {}, {}
    for n in BIG:
        grads[n] = _exchange_form(n, result[n])
        deltas[n], new_m[n], new_v[n] = _adamw(w[n], grads[n], m[n], v[n], n)

    small_all = _allgather8(_pack_small(g, loss_local), "small")
    zero = jnp.zeros((), F32)
    gs, ds, ms, vs = _small_reduce_adamw(small_all, _pack_small(w, zero), _pack_small(m, zero), _pack_small(v, zero))
    gsm, loss = _unpack_small(gs, w)
    grads.update(gsm)
    deltas.update(_unpack_small(ds, w)[0])
    new_m.update(_unpack_small(ms, w)[0])
    new_v.update(_unpack_small(vs, w)[0])
    return (loss, dx[None], *[grads[n] for n in WEIGHTS], *[deltas[n] for n in WEIGHTS],
            *[new_m[n] for n in WEIGHTS], *[new_v[n] for n in WEIGHTS])
```

```python
import math

import jax
import jax.numpy as jnp
from jax import lax
from jax.experimental import pallas as pl
from jax.experimental.pallas import tpu as pltpu

F32 = jnp.float32
BF16 = jnp.bfloat16

D_MODEL = 1024
DEPTH = 2
D_SSM = 384
SSM_GROUP = 16
N_GROUPS = 24
SSM_STATE = 64
N_STATE = N_GROUPS * SSM_STATE
POOL_WINDOWS = (2, 4, 8, 16)
POOL_GROUP = 64
D_POOL = 256
MAX_WINDOW = 16
SGU_HEADS = 6
SGU_HEAD_DIM = 64
D_SGU = 384
CHUNK = 128
D_IN = D_SSM + D_POOL + 2 * D_SGU
D_FF = 2816
EPS = 1e-6

ADAM_LR = 0.001
ADAM_B1 = 0.9
ADAM_B2 = 0.999
ADAM_EPS = 1e-08
ADAM_WD = 0.01
ADAM_STEP = 10

LANES = 128
SUBLANES = 8
N_SLAB = N_STATE // LANES
VMEM_LIMIT = 56 * 1024 * 1024

TS = 512
TS_FFN = 256

WEIGHTS = ['g_mix', 'w_in', 'A_re', 'A_im', 'log_dt', 'B_re', 'B_im', 'C_re', 'C_im', 'D_skip', 'w_glu', 'b_glu',
           'w_pool', 'pool_scale', 'sgu_ln_g', 'sgu_ln_b', 'w_spatial', 'b_spatial', 'w_out', 'g_ffn', 'w_gate',
           'w_up', 'w_down', 'g_final']
BIG = ['w_in', 'w_glu', 'w_out', 'w_gate', 'w_up', 'w_down']
SMALL = [n for n in WEIGHTS if n not in BIG]
TRANSPOSED = ("w_in", "w_gate", "w_up")
N_CHIPS = 4
N_DEV = 8


def _cp(**kw):
    return pltpu.CompilerParams(vmem_limit_bytes=VMEM_LIMIT, **kw)


def _row(ts, n):
    return pl.BlockSpec((ts, n), lambda i: (i, 0))


def _const(shape):
    nd = len(shape)
    return pl.BlockSpec(shape, lambda i: (0,) * nd, pipeline_mode=pl.Buffered(1))


def _acc(shape):
    nd = len(shape)
    return pl.BlockSpec(shape, lambda i: (0,) * nd)


def _dot(a, b):
    return jnp.dot(a, b, preferred_element_type=F32)


def _dot_tn(a, b):
    return lax.dot_general(a, b, (((0,), (0,)), ((), ())), preferred_element_type=F32)


def _dot_nt(a, b):
    return lax.dot_general(a, b, (((1,), (1,)), ((), ())), preferred_element_type=F32)


_G0 = math.sqrt(2.0 / math.pi)
_G1 = 0.044715


def _gelu(x):
    return 0.5 * x * (1.0 + jnp.tanh(_G0 * (x + _G1 * x * x * x)))


def _gelu_grad(x):
    t = jnp.tanh(_G0 * (x + _G1 * x * x * x))
    return 0.5 * (1.0 + t) + 0.5 * x * (1.0 - t * t) * (_G0 * (1.0 + 3.0 * _G1 * x * x))


def _sigmoid(x):
    return 1.0 / (1.0 + jnp.exp(-x))


def _rms(x):
    r = lax.rsqrt(jnp.mean(x * x, axis=-1, keepdims=True) + EPS)
    return x * r, r


def _rms_bwd(dh, n, r, g):
    dn = dh * g
    return r * (dn - n * jnp.mean(dn * n, axis=-1, keepdims=True)), dh * n


def _colsum8(v):
    rows, n = v.shape
    return jnp.sum(v.reshape(rows // SUBLANES, SUBLANES, n), axis=0)


def _mix_in_fwd(x, g, w, tag):
    s = x.shape[0]

    def body(x_ref, g_ref, w_ref, za_ref, zb_ref, zuv_ref, h_ref):
        n, _ = _rms(x_ref[...])
        h = (n * g_ref[...]).astype(BF16)
        z = _dot(h, w_ref[...])
        za_ref[...] = z[:, :D_SSM]
        zb_ref[...] = z[:, D_SSM:D_SSM + D_POOL]
        zuv_ref[...] = z[:, D_SSM + D_POOL:]
        h_ref[...] = h

    return pl.pallas_call(
        body, grid=(s // TS,),
        in_specs=[_row(TS, D_MODEL), _const((1, D_MODEL)), _const((D_MODEL, D_IN))],
        out_specs=[_row(TS, D_SSM), _row(TS, D_POOL), _row(TS, 2 * D_SGU), _row(TS, D_MODEL)],
        out_shape=[jax.ShapeDtypeStruct((s, D_SSM), F32), jax.ShapeDtypeStruct((s, D_POOL), F32),
                   jax.ShapeDtypeStruct((s, 2 * D_SGU), F32), jax.ShapeDtypeStruct((s, D_MODEL), BF16)],
        name=f"mix_in_fwd_{tag}", compiler_params=_cp(dimension_semantics=("arbitrary",)),
    )(x, g, w)


def _cmul(ar, ai, br, bi):
    return ar * br - ai * bi, ar * bi + ai * br


def _cpow(ar, ai, n):
    assert n & (n - 1) == 0
    while n > 1:
        ar, ai = _cmul(ar, ai, ar, ai)
        n //= 2
    return ar, ai


def _to_slabs(ref, v):
    for j in range(N_SLAB):
        ref[j] = v[:, LANES * j:LANES * (j + 1)]


def _from_slabs(ref):
    return jnp.concatenate([ref[j] for j in range(N_SLAB)], axis=1)


N_USLAB = D_SSM // LANES
SEG = TS // SUBLANES


def _interleave_rows(v, stage, dst):
    for j in range(N_USLAB):
        stage[j] = v[:, LANES * j:LANES * (j + 1)]

    def step(k, carry):
        dst[:, pl.ds(pl.multiple_of(k * SUBLANES, SUBLANES), SUBLANES), :] = stage[:, pl.ds(k, SUBLANES, stride=SEG), :]
        return carry

    lax.fori_loop(0, SEG, step, 0)
    return jnp.concatenate([dst[j] for j in range(N_USLAB)], axis=1)


def _deinterleave_rows(v, stage, dst):
    for j in range(N_USLAB):
        stage[j] = v[:, LANES * j:LANES * (j + 1)]

    def step(k, carry):
        dst[:, pl.ds(k, SUBLANES, stride=SEG), :] = stage[:, pl.ds(pl.multiple_of(k * SUBLANES, SUBLANES), SUBLANES), :]
        return carry

    lax.fori_loop(0, SEG, step, 0)
    return jnp.concatenate([dst[j] for j in range(N_USLAB)], axis=1)


def _scan_rows(k):
    return pl.ds(pl.multiple_of(k * SUBLANES, SUBLANES), SUBLANES)


SLABS_PER_USLAB = N_SLAB // N_USLAB
S5_IN = (N_USLAB, LANES, 2 * N_STATE // N_USLAB)
S5_OUT = (N_USLAB, 2 * N_STATE // N_USLAB, LANES)


def _lanes(v, j):
    return v[:, LANES * j:LANES * (j + 1)]


def _state_split(re_ref, im_ref, j, v):
    for q in range(SLABS_PER_USLAB):
        re_ref[SLABS_PER_USLAB * j + q] = _lanes(v, q)
        im_ref[SLABS_PER_USLAB * j + q] = _lanes(v, SLABS_PER_USLAB + q)


def _state_cat(re_ref, im_ref, j):
    idx = range(SLABS_PER_USLAB * j, SLABS_PER_USLAB * (j + 1))
    return jnp.concatenate([re_ref[q] for q in idx] + [im_ref[q] for q in idx], axis=1).astype(BF16)


def _s5_fwd(u, p, tag):
    s = u.shape[0]
    seg = SEG

    def body(u_ref, bbt_ref, ar_ref, ai_ref, ct_ref, dsk_ref, wglu_ref, bglu_ref,
             oa_ref, y_ref, hr_ref, hi_ref, sr, si, er, ei, ir, ii, cr, ci, stage, perm):
        @pl.when(pl.program_id(0) == 0)
        def _():
            cr[...] = jnp.zeros_like(cr)
            ci[...] = jnp.zeros_like(ci)

        uv = _interleave_rows(u_ref[...], stage, perm)
        ub = uv.astype(BF16)
        for j in range(N_USLAB):
            _state_split(sr, si, j, _dot(_lanes(ub, j), bbt_ref[j]))
        ar = ar_ref[...]
        ai = ai_ref[...]

        def local(k, h):
            rows = _scan_rows(k)
            hr, hi = _cmul(ar, ai, h[0], h[1])
            return hr + sr[:, rows, :], hi + si[:, rows, :]

        zero = jnp.zeros((N_SLAB, SUBLANES, LANES), F32)
        e_r, e_i = lax.fori_loop(0, seg, local, (zero, zero))
        er[...] = e_r
        ei[...] = e_i
        pr, pi = _cpow(ar[:, 0:1, :], ai[:, 0:1, :], seg)
        c_r = cr[...]
        c_i = ci[...]
        for j in range(SUBLANES):
            ir[:, j:j + 1, :] = c_r
            ii[:, j:j + 1, :] = c_i
            n_r, n_i = _cmul(pr, pi, c_r, c_i)
            c_r = n_r + er[:, j:j + 1, :]
            c_i = n_i + ei[:, j:j + 1, :]
        cr[...] = c_r
        ci[...] = c_i

        def full(k, h):
            rows = _scan_rows(k)
            hr, hi = _cmul(ar, ai, h[0], h[1])
            hr = hr + sr[:, rows, :]
            hi = hi + si[:, rows, :]
            sr[:, rows, :] = hr
            si[:, rows, :] = hi
            return hr, hi

        lax.fori_loop(0, seg, full, (ir[...], ii[...]))
        hr_ref[...] = _from_slabs(sr)
        hi_ref[...] = _from_slabs(si)
        y = jnp.concatenate([_dot(_state_cat(sr, si, j), ct_ref[j]) for j in range(N_USLAB)], axis=1)
        y = y + dsk_ref[...] * uv
        y_ref[...] = y
        g = _gelu(y)
        pre = _dot(g.astype(BF16), wglu_ref[...]) + bglu_ref[...]
        oa_ref[...] = _deinterleave_rows(g * _sigmoid(pre), stage, perm).astype(BF16)

    slab = (N_SLAB, SUBLANES, LANES)
    uslab = (N_USLAB, TS, LANES)
    return pl.pallas_call(
        body, grid=(s // TS,),
        in_specs=[_row(TS, D_SSM), _const(S5_IN), _const(slab), _const(slab), _const(S5_OUT), _const((1, D_SSM)),
                  _const((D_SSM, D_SSM)), _const((1, D_SSM))],
        out_specs=[_row(TS, D_SSM), _row(TS, D_SSM), _row(TS, N_STATE), _row(TS, N_STATE)],
        out_shape=[jax.ShapeDtypeStruct((s, D_SSM), BF16), jax.ShapeDtypeStruct((s, D_SSM), F32),
                   jax.ShapeDtypeStruct((s, N_STATE), F32), jax.ShapeDtypeStruct((s, N_STATE), F32)],
        scratch_shapes=[pltpu.VMEM((N_SLAB, TS, LANES), F32), pltpu.VMEM((N_SLAB, TS, LANES), F32),
                        pltpu.VMEM(slab, F32), pltpu.VMEM(slab, F32), pltpu.VMEM(slab, F32), pltpu.VMEM(slab, F32),
                        pltpu.VMEM((N_SLAB, 1, LANES), F32), pltpu.VMEM((N_SLAB, 1, LANES), F32),
                        pltpu.VMEM(uslab, F32), pltpu.VMEM(uslab, F32)],
        name=f"s5_fwd_{tag}", compiler_params=_cp(dimension_semantics=("arbitrary",)),
    )(u, p["bbt3"], p["a_re8"], p["a_im8"], p["ct3"], p["d_skip"], p["w_glu"], p["b_glu"])


def _pool_consts():
    w = jnp.repeat(jnp.asarray(POOL_WINDOWS, F32), POOL_GROUP)[None, :]
    return w


def _window_sum(buf, first, rows, wl, step):
    acc = buf[pl.ds(first, rows), :]
    for j in range(1, MAX_WINDOW):
        term = buf[pl.ds(first + step * j, rows), :]
        acc = acc + (term if j < min(POOL_WINDOWS) else term * (wl > j).astype(F32))
    return acc


def _pool_count(i, rows, wl, offset=0):
    t = (i * TS + offset + 1).astype(F32) + lax.broadcasted_iota(jnp.int32, (rows, 1), 0).astype(F32)
    return jnp.minimum(t, wl)


def _pool_fwd(zb, p, tag):
    s = zb.shape[0]
    hb = TS // MAX_WINDOW

    def body(u_ref, halo_ref, wl_ref, w_ref, sc_ref, ob_ref, pooled_ref, buf):
        i = pl.program_id(0)
        uv = u_ref[...]
        buf[pl.ds(0, MAX_WINDOW), :] = jnp.where(i > 0, halo_ref[...], 0.0)
        buf[pl.ds(MAX_WINDOW, TS), :] = uv
        wl = wl_ref[...]
        pooled = (_window_sum(buf, MAX_WINDOW, TS, wl, -1) / _pool_count(i, TS, wl) - uv).astype(BF16)
        pooled_ref[...] = pooled
        ob_ref[...] = (_dot(pooled, w_ref[...]) * sc_ref[...]).astype(BF16)

    return pl.pallas_call(
        body, grid=(s // TS,),
        in_specs=[_row(TS, D_POOL),
                  pl.BlockSpec((MAX_WINDOW, D_POOL), lambda i: (jnp.maximum(i * hb - 1, 0), 0)),
                  _const((1, D_POOL)), _const((D_POOL, D_POOL)), _const((1, D_POOL))],
        out_specs=[_row(TS, D_POOL), _row(TS, D_POOL)],
        out_shape=[jax.ShapeDtypeStruct((s, D_POOL), BF16), jax.ShapeDtypeStruct((s, D_POOL), BF16)],
        scratch_shapes=[pltpu.VMEM((TS + MAX_WINDOW, D_POOL), F32)],
        name=f"pool_fwd_{tag}", compiler_params=_cp(dimension_semantics=("arbitrary",)),
    )(zb, zb, _pool_consts(), p["w_pool_bd"], p["pool_scale"])


def _sgu_mix(vl, wpair_ref, lo, hi):
    rows = vl.shape[0]
    chunks = []
    for c in range(rows // CHUNK):
        vc = vl[CHUNK * c:CHUNK * (c + 1), :]
        parts = []
        for q in range(SGU_HEADS // 2):
            vq = vc[:, LANES * q:LANES * (q + 1)]
            rhs = jnp.concatenate([vq * lo, vq * hi], axis=0).astype(BF16)
            parts.append(_dot(wpair_ref[q], rhs))
        chunks.append(jnp.concatenate(parts, axis=1))
    return jnp.concatenate(chunks, axis=0)


def _sgu_front(zuv, lng, lnb):
    zu = zuv[:, :D_SGU]
    zv = zuv[:, D_SGU:]
    u = _gelu(zu)
    v = _gelu(zv)
    mu = jnp.mean(v, axis=-1, keepdims=True)
    vc = v - mu
    rs = lax.rsqrt(jnp.mean(vc * vc, axis=-1, keepdims=True) + EPS)
    vn = vc * rs
    return zu, zv, u, vn, rs, vn * lng + lnb


def _half_masks():
    lane = lax.broadcasted_iota(jnp.int32, (1, LANES), 1)
    lo = (lane < SGU_HEAD_DIM).astype(F32)
    return lo, 1.0 - lo


def _sgu_fwd(zuv, p, tag):
    s = zuv.shape[0]

    def body(z_ref, lng_ref, lnb_ref, wp_ref, bias_ref, oc_ref):
        lo, hi = _half_masks()
        _, _, u, _, _, vl = _sgu_front(z_ref[...], lng_ref[...], lnb_ref[...])
        mixed = _sgu_mix(vl, wp_ref, lo, hi) + jnp.tile(bias_ref[...], (TS // CHUNK, 1))
        oc_ref[...] = (u * mixed).astype(BF16)

    return pl.pallas_call(
        body, grid=(s // TS,),
        in_specs=[_row(TS, 2 * D_SGU), _const((1, D_SGU)), _const((1, D_SGU)),
                  _const((SGU_HEADS // 2, CHUNK, 2 * CHUNK)), _const((CHUNK, D_SGU))],
        out_specs=_row(TS, D_SGU),
        out_shape=jax.ShapeDtypeStruct((s, D_SGU), BF16),
        name=f"sgu_fwd_{tag}", compiler_params=_cp(dimension_semantics=("arbitrary",)),
    )(zuv, p["sgu_ln_g"], p["sgu_ln_b"], p["ws_pair"], p["bias_sp"])


def _blk_fwd(x0, oa, ob, oc, p, tag):
    s = x0.shape[0]
    ts = TS_FFN

    def body(x0_ref, oa_ref, ob_ref, oc_ref, wo_ref, g_ref, wg_ref, wu_ref, wd_ref,
             x1_ref, x2_ref, h2_ref, gt_ref, up_ref, ycat_ref):
        ycat = jnp.concatenate([oa_ref[...], ob_ref[...], oc_ref[...]], axis=1)
        ycat_ref[...] = ycat
        x1 = x0_ref[...] + _dot(ycat, wo_ref[...])
        x1_ref[...] = x1
        n, _ = _rms(x1)
        h2 = (n * g_ref[...]).astype(BF16)
        h2_ref[...] = h2
        gt = _dot(h2, wg_ref[...])
        up = _dot(h2, wu_ref[...])
        gt_ref[...] = gt.astype(BF16)
        up_ref[...] = up.astype(BF16)
        act = (gt * _sigmoid(gt) * up).astype(BF16)
        x2_ref[...] = x1 + _dot(act, wd_ref[...])

    return pl.pallas_call(
        body, grid=(s // ts,),
        in_specs=[_row(ts, D_MODEL), _row(ts, D_SSM), _row(ts, D_POOL), _row(ts, D_SGU),
                  _const((D_MODEL, D_MODEL)), _const((1, D_MODEL)), _const((D_MODEL, D_FF)),
                  _const((D_MODEL, D_FF)), _const((D_FF, D_MODEL))],
        out_specs=[_row(ts, D_MODEL), _row(ts, D_MODEL), _row(ts, D_MODEL), _row(ts, D_FF), _row(ts, D_FF),
                   _row(ts, D_MODEL)],
        out_shape=[jax.ShapeDtypeStruct((s, D_MODEL), F32), jax.ShapeDtypeStruct((s, D_MODEL), F32),
                   jax.ShapeDtypeStruct((s, D_MODEL), BF16), jax.ShapeDtypeStruct((s, D_FF), BF16),
                   jax.ShapeDtypeStruct((s, D_FF), BF16), jax.ShapeDtypeStruct((s, D_MODEL), BF16)],
        name=f"blk_fwd_{tag}", compiler_params=_cp(dimension_semantics=("arbitrary",)),
    )(x0, oa, ob, oc, p["w_out"], p["g_ffn"], p["w_gate"], p["w_up"], p["w_down"])


def _blk_bwd(dx2, x1, gt, up, p, tag):
    s = dx2.shape[0]
    ts = TS_FFN

    def body(dx2_ref, x1_ref, gt_ref, up_ref, wdt_ref, wgt_ref, wut_ref, wot_ref, g_ref,
             dx1_ref, da_ref, db_ref, dc_ref, dgt_ref, dup_ref, act_ref, dg_ref):
        @pl.when(pl.program_id(0) == 0)
        def _():
            dg_ref[...] = jnp.zeros_like(dg_ref)

        dx2v = dx2_ref[...]
        dact = _dot(dx2v.astype(BF16), wdt_ref[...])
        gf = gt_ref[...].astype(F32)
        uf = up_ref[...].astype(F32)
        sg = _sigmoid(gf)
        sl = gf * sg
        act_ref[...] = (sl * uf).astype(BF16)
        dgt = (dact * uf * (sg * (1.0 + gf * (1.0 - sg)))).astype(BF16)
        dup = (dact * sl).astype(BF16)
        dgt_ref[...] = dgt
        dup_ref[...] = dup
        dh2 = _dot(dgt, wgt_ref[...]) + _dot(dup, wut_ref[...])
        n, r = _rms(x1_ref[...])
        dxn, dgp = _rms_bwd(dh2, n, r, g_ref[...])
        dg_ref[...] += _colsum8(dgp)
        dx1 = dx2v + dxn
        dx1_ref[...] = dx1
        dy = _dot(dx1.astype(BF16), wot_ref[...])
        da_ref[...] = dy[:, :D_SSM]
        db_ref[...] = dy[:, D_SSM:D_SSM + D_POOL]
        dc_ref[...] = dy[:, D_SSM + D_POOL:]

    return pl.pallas_call(
        body, grid=(s // ts,),
        in_specs=[_row(ts, D_MODEL), _row(ts, D_MODEL), _row(ts, D_FF), _row(ts, D_FF),
                  _const((D_MODEL, D_FF)), _const((D_FF, D_MODEL)), _const((D_FF, D_MODEL)),
                  _const((D_MODEL, D_MODEL)), _const((1, D_MODEL))],
        out_specs=[_row(ts, D_MODEL), _row(ts, D_SSM), _row(ts, D_POOL), _row(ts, D_SGU), _row(ts, D_FF),
                   _row(ts, D_FF), _row(ts, D_FF), _acc((SUBLANES, D_MODEL))],
        out_shape=[jax.ShapeDtypeStruct((s, D_MODEL), F32), jax.ShapeDtypeStruct((s, D_SSM), F32),
                   jax.ShapeDtypeStruct((s, D_POOL), F32), jax.ShapeDtypeStruct((s, D_SGU), F32),
                   jax.ShapeDtypeStruct((s, D_FF), BF16), jax.ShapeDtypeStruct((s, D_FF), BF16),
                   jax.ShapeDtypeStruct((s, D_FF), BF16), jax.ShapeDtypeStruct((SUBLANES, D_MODEL), F32)],
        name=f"blk_bwd_{tag}", compiler_params=_cp(dimension_semantics=("arbitrary",)),
    )(dx2, x1, gt, up, p["w_down_t"], p["w_gate_t"], p["w_up_t"], p["w_out_t"], p["g_ffn"])


def _s5_bwd(dout, u, y, h_re, h_im, p, tag):
    s = u.shape[0]
    nt = s // TS
    seg = SEG

    def rev(n):
        return pl.BlockSpec((TS, n), lambda i: (nt - 1 - i, 0))

    def body(do_ref, u_ref, y_ref, hr_ref, hi_ref, ar_ref, ai_ref, cb_ref, bb_ref, dsk_ref,
             wglu_ref, wglut_ref, bglu_ref,
             du_ref, dct_ref, dbb_ref, dar_ref, dai_ref, dd_ref, dwglu_ref, dbglu_ref,
             gr, gi, hsr, hsi, er, ei, jr, ji, cr, ci, stage, perm):
        @pl.when(pl.program_id(0) == 0)
        def _():
            for ref in (cr, ci, dct_ref, dbb_ref, dar_ref, dai_ref, dd_ref, dwglu_ref, dbglu_ref):
                ref[...] = jnp.zeros_like(ref)

        uv = _interleave_rows(u_ref[...], stage, perm)
        yv = y_ref[...]
        dov = _interleave_rows(do_ref[...], stage, perm)
        g = _gelu(yv)
        gb = g.astype(BF16)
        sg = _sigmoid(_dot(gb, wglu_ref[...]) + bglu_ref[...])
        dpre = dov * g * sg * (1.0 - sg)
        dpb = dpre.astype(BF16)
        dwglu_ref[...] += _dot_tn(gb, dpb)
        dbglu_ref[...] += _colsum8(dpre)
        dy = (dov * sg + _dot(dpb, wglut_ref[...])) * _gelu_grad(yv)
        dd_ref[...] += _colsum8(dy * uv)
        dyb = dy.astype(BF16)
        _to_slabs(hsr, hr_ref[...])
        _to_slabs(hsi, hi_ref[...])
        for j in range(N_USLAB):
            dct_ref[j] += _dot_tn(_state_cat(hsr, hsi, j), _lanes(dyb, j))
            _state_split(gr, gi, j, _dot(_lanes(dyb, j), cb_ref[j]))
        ar = ar_ref[...]
        ai = -ai_ref[...]

        def local(k, h):
            rows = _scan_rows(seg - 1 - k)
            nr, ni = _cmul(ar, ai, h[0], h[1])
            return nr + gr[:, rows, :], ni + gi[:, rows, :]

        zero = jnp.zeros((N_SLAB, SUBLANES, LANES), F32)
        e_r, e_i = lax.fori_loop(0, seg, local, (zero, zero))
        er[...] = e_r
        ei[...] = e_i
        pr, pi = _cpow(ar[:, 0:1, :], ai[:, 0:1, :], seg)
        c_r = cr[...]
        c_i = ci[...]
        for j in range(SUBLANES - 1, -1, -1):
            jr[:, j:j + 1, :] = c_r
            ji[:, j:j + 1, :] = c_i
            n_r, n_i = _cmul(pr, pi, c_r, c_i)
            c_r = n_r + er[:, j:j + 1, :]
            c_i = n_i + ei[:, j:j + 1, :]
        cr[...] = c_r
        ci[...] = c_i

        def full(k, carry):
            g_r, g_i, a_r, a_i = carry
            rows = _scan_rows(seg - 1 - k)
            h_r = hsr[:, rows, :]
            h_i = hsi[:, rows, :]
            a_r = a_r + g_r * h_r + g_i * h_i
            a_i = a_i + g_i * h_r - g_r * h_i
            nr, ni = _cmul(ar, ai, g_r, g_i)
            nr = nr + gr[:, rows, :]
            ni = ni + gi[:, rows, :]
            gr[:, rows, :] = nr
            gi[:, rows, :] = ni
            return nr, ni, a_r, a_i

        _, _, a_r, a_i = lax.fori_loop(0, seg, full, (jr[...], ji[...], zero, zero))
        dar_ref[...] += a_r
        dai_ref[...] += a_i
        ub = uv.astype(BF16)
        dus = []
        for j in range(N_USLAB):
            gb_j = _state_cat(gr, gi, j)
            dbb_ref[j] += _dot_tn(_lanes(ub, j), gb_j)
            dus.append(_dot(gb_j, bb_ref[j]))
        du = dy * dsk_ref[...] + jnp.concatenate(dus, axis=1)
        du_ref[...] = _deinterleave_rows(du, stage, perm)

    slab = (N_SLAB, SUBLANES, LANES)
    big = (N_SLAB, TS, LANES)
    uslab = (N_USLAB, TS, LANES)
    return pl.pallas_call(
        body, grid=(nt,),
        in_specs=[rev(D_SSM), rev(D_SSM), rev(D_SSM), rev(N_STATE), rev(N_STATE), _const(slab), _const(slab),
                  _const(S5_IN), _const(S5_OUT), _const((1, D_SSM)), _const((D_SSM, D_SSM)), _const((D_SSM, D_SSM)),
                  _const((1, D_SSM))],
        out_specs=[rev(D_SSM), _acc(S5_OUT), _acc(S5_IN), _acc(slab), _acc(slab), _acc((SUBLANES, D_SSM)),
                   _acc((D_SSM, D_SSM)), _acc((SUBLANES, D_SSM))],
        out_shape=[jax.ShapeDtypeStruct((s, D_SSM), F32), jax.ShapeDtypeStruct(S5_OUT, F32),
                   jax.ShapeDtypeStruct(S5_IN, F32), jax.ShapeDtypeStruct(slab, F32),
                   jax.ShapeDtypeStruct(slab, F32), jax.ShapeDtypeStruct((SUBLANES, D_SSM), F32),
                   jax.ShapeDtypeStruct((D_SSM, D_SSM), F32), jax.ShapeDtypeStruct((SUBLANES, D_SSM), F32)],
        scratch_shapes=[pltpu.VMEM(big, F32), pltpu.VMEM(big, F32), pltpu.VMEM(big, F32), pltpu.VMEM(big, F32),
                        pltpu.VMEM(slab, F32), pltpu.VMEM(slab, F32), pltpu.VMEM(slab, F32), pltpu.VMEM(slab, F32),
                        pltpu.VMEM((N_SLAB, 1, LANES), F32), pltpu.VMEM((N_SLAB, 1, LANES), F32),
                        pltpu.VMEM(uslab, F32), pltpu.VMEM(uslab, F32)],
        name=f"s5_bwd_{tag}", compiler_params=_cp(dimension_semantics=("arbitrary",)),
    )(dout, u, y, h_re, h_im, p["a_re8"], p["a_im8"], p["cb3"], p["bb3"], p["d_skip"], p["w_glu"], p["w_glu_t"],
      p["b_glu"])


def _pool_bwd(dout, pooled, p, tag):
    s = dout.shape[0]
    nt = s // TS
    hb = TS // MAX_WINDOW

    def halo(n):
        return pl.BlockSpec((MAX_WINDOW, n), lambda i: (jnp.minimum((i + 1) * hb, nt * hb - 1), 0))

    def body(do_ref, po_ref, doh_ref, wl_ref, w_ref, wt_ref, sc_ref, dz_ref, dw_ref, dsc_ref, buf):
        i = pl.program_id(0)

        @pl.when(i == 0)
        def _():
            dw_ref[...] = jnp.zeros_like(dw_ref)
            dsc_ref[...] = jnp.zeros_like(dsc_ref)

        wl = wl_ref[...]
        sc = sc_ref[...]
        dov = do_ref[...]
        pooled_b = po_ref[...]
        dsc_ref[...] += _colsum8(dov * _dot(pooled_b, w_ref[...]))
        dmix = (dov * sc).astype(BF16)
        dw_ref[...] += _dot_tn(pooled_b, dmix)
        dpool = _dot(dmix, wt_ref[...])
        dpool_h = _dot((doh_ref[...] * sc).astype(BF16), wt_ref[...])
        buf[pl.ds(0, TS), :] = dpool / _pool_count(i, TS, wl)
        buf[pl.ds(TS, MAX_WINDOW), :] = jnp.where(i < nt - 1, dpool_h / _pool_count(i, MAX_WINDOW, wl, TS), 0.0)
        dz_ref[...] = _window_sum(buf, 0, TS, wl, 1) - dpool

    return pl.pallas_call(
        body, grid=(nt,),
        in_specs=[_row(TS, D_POOL), _row(TS, D_POOL), halo(D_POOL), _const((1, D_POOL)), _const((D_POOL, D_POOL)),
                  _const((D_POOL, D_POOL)), _const((1, D_POOL))],
        out_specs=[_row(TS, D_POOL), _acc((D_POOL, D_POOL)), _acc((SUBLANES, D_POOL))],
        out_shape=[jax.ShapeDtypeStruct((s, D_POOL), F32), jax.ShapeDtypeStruct((D_POOL, D_POOL), F32),
                   jax.ShapeDtypeStruct((SUBLANES, D_POOL), F32)],
        scratch_shapes=[pltpu.VMEM((TS + MAX_WINDOW, D_POOL), F32)],
        name=f"pool_bwd_{tag}", compiler_params=_cp(dimension_semantics=("arbitrary",)),
    )(dout, pooled, dout, _pool_consts(), p["w_pool_bd"], p["w_pool_bd_t"], p["pool_scale"])


def _sgu_bwd(dout, zuv, p, tag):
    s = zuv.shape[0]

    def body(do_ref, z_ref, lng_ref, lnb_ref, wp_ref, wpt_ref, bias_ref,
             dz_ref, dws_ref, dbias_ref, dlng_ref, dlnb_ref):
        @pl.when(pl.program_id(0) == 0)
        def _():
            for ref in (dws_ref, dbias_ref, dlng_ref, dlnb_ref):
                ref[...] = jnp.zeros_like(ref)

        lo, hi = _half_masks()
        lng = lng_ref[...]
        zu, zv, u, vn, rs, vl = _sgu_front(z_ref[...], lng, lnb_ref[...])
        mixed = _sgu_mix(vl, wp_ref, lo, hi) + jnp.tile(bias_ref[...], (TS // CHUNK, 1))
        dov = do_ref[...]
        dzu = dov * mixed * _gelu_grad(zu)
        dmix = dov * u
        dbias = dbias_ref[...]
        for c in range(TS // CHUNK):
            dmc = dmix[CHUNK * c:CHUNK * (c + 1), :]
            dbias = dbias + dmc
            vlc = vl[CHUNK * c:CHUNK * (c + 1), :].astype(BF16)
            for q in range(SGU_HEADS // 2):
                dq = dmc[:, LANES * q:LANES * (q + 1)]
                vq = vlc[:, LANES * q:LANES * (q + 1)]
                dws_ref[2 * q] += _dot_nt((dq * lo).astype(BF16), vq)
                dws_ref[2 * q + 1] += _dot_nt((dq * hi).astype(BF16), vq)
        dbias_ref[...] = dbias
        dvl = _sgu_mix(dmix, wpt_ref, lo, hi)
        dlng_ref[...] += _colsum8(dvl * vn)
        dlnb_ref[...] += _colsum8(dvl)
        dvn = dvl * lng
        dv = rs * (dvn - jnp.mean(dvn, axis=-1, keepdims=True) - vn * jnp.mean(dvn * vn, axis=-1, keepdims=True))
        dz_ref[...] = jnp.concatenate([dzu, dv * _gelu_grad(zv)], axis=1)

    return pl.pallas_call(
        body, grid=(s // TS,),
        in_specs=[_row(TS, D_SGU), _row(TS, 2 * D_SGU), _const((1, D_SGU)), _const((1, D_SGU)),
                  _const((SGU_HEADS // 2, CHUNK, 2 * CHUNK)), _const((SGU_HEADS // 2, CHUNK, 2 * CHUNK)),
                  _const((CHUNK, D_SGU))],
        out_specs=[_row(TS, 2 * D_SGU), _acc((SGU_HEADS, CHUNK, CHUNK)), _acc((CHUNK, D_SGU)),
                   _acc((SUBLANES, D_SGU)), _acc((SUBLANES, D_SGU))],
        out_shape=[jax.ShapeDtypeStruct((s, 2 * D_SGU), F32), jax.ShapeDtypeStruct((SGU_HEADS, CHUNK, CHUNK), F32),
                   jax.ShapeDtypeStruct((CHUNK, D_SGU), F32), jax.ShapeDtypeStruct((SUBLANES, D_SGU), F32),
                   jax.ShapeDtypeStruct((SUBLANES, D_SGU), F32)],
        name=f"sgu_bwd_{tag}", compiler_params=_cp(dimension_semantics=("arbitrary",)),
    )(dout, zuv, p["sgu_ln_g"], p["sgu_ln_b"], p["ws_pair"], p["ws_pair_t"], p["bias_sp"])


def _mix_in_bwd(dza, dzb, dzuv, x0, dx1, p, tag):
    s = x0.shape[0]

    def body(da_ref, db_ref, dc_ref, x_ref, dx1_ref, wt_ref, g_ref, dx0_ref, dz_ref, dg_ref):
        @pl.when(pl.program_id(0) == 0)
        def _():
            dg_ref[...] = jnp.zeros_like(dg_ref)

        dz = jnp.concatenate([da_ref[...], db_ref[...], dc_ref[...]], axis=1).astype(BF16)
        dz_ref[...] = dz
        n, r = _rms(x_ref[...])
        dxn, dgp = _rms_bwd(_dot(dz, wt_ref[...]), n, r, g_ref[...])
        dg_ref[...] += _colsum8(dgp)
        dx0_ref[...] = dx1_ref[...] + dxn

    return pl.pallas_call(
        body, grid=(s // TS,),
        in_specs=[_row(TS, D_SSM), _row(TS, D_POOL), _row(TS, 2 * D_SGU), _row(TS, D_MODEL), _row(TS, D_MODEL),
                  _const((D_IN, D_MODEL)), _const((1, D_MODEL))],
        out_specs=[_row(TS, D_MODEL), _row(TS, D_IN), _acc((SUBLANES, D_MODEL))],
        out_shape=[jax.ShapeDtypeStruct((s, D_MODEL), F32), jax.ShapeDtypeStruct((s, D_IN), BF16),
                   jax.ShapeDtypeStruct((SUBLANES, D_MODEL), F32)],
        name=f"mix_in_bwd_{tag}", compiler_params=_cp(dimension_semantics=("arbitrary",)),
    )(dza, dzb, dzuv, x0, dx1, p["w_in_t"], p["g_mix"])


def _head(x, target, g):
    s = x.shape[0]

    def body(x_ref, t_ref, g_ref, dx_ref, loss_ref, dg_ref):
        @pl.when(pl.program_id(0) == 0)
        def _():
            loss_ref[...] = jnp.zeros_like(loss_ref)
            dg_ref[...] = jnp.zeros_like(dg_ref)

        gv = g_ref[...]
        n, r = _rms(x_ref[...])
        diff = n * gv - t_ref[...]
        loss_ref[...] += jnp.sum(diff * diff) * (0.5 / D_MODEL)
        dxn, dgp = _rms_bwd(diff * (1.0 / D_MODEL), n, r, gv)
        dg_ref[...] += _colsum8(dgp)
        dx_ref[...] = dxn

    return pl.pallas_call(
        body, grid=(s // TS,),
        in_specs=[_row(TS, D_MODEL), _row(TS, D_MODEL), _const((1, D_MODEL))],
        out_specs=[_row(TS, D_MODEL), _acc((SUBLANES, LANES)), _acc((SUBLANES, D_MODEL))],
        out_shape=[jax.ShapeDtypeStruct((s, D_MODEL), F32), jax.ShapeDtypeStruct((SUBLANES, LANES), F32),
                   jax.ShapeDtypeStruct((SUBLANES, D_MODEL), F32)],
        name="head", compiler_params=_cp(dimension_semantics=("arbitrary",)),
    )(x, target, g)


def _atb(a, b, tn, tag):
    s, ka = a.shape
    kb = b.shape[1]
    ns = s // TS

    def body(a_ref, b_ref, o_ref):
        @pl.when(pl.program_id(1) == 0)
        def _():
            o_ref[...] = jnp.zeros_like(o_ref)

        o_ref[...] += _dot_tn(a_ref[...].astype(BF16), b_ref[...].astype(BF16))

    return pl.pallas_call(
        body, grid=(kb // tn, ns),
        in_specs=[pl.BlockSpec((TS, ka), lambda j, i: (i, 0)), pl.BlockSpec((TS, tn), lambda j, i: (i, j))],
        out_specs=pl.BlockSpec((ka, tn), lambda j, i: (0, j)),
        out_shape=jax.ShapeDtypeStruct((ka, kb), F32),
        name=f"atb_{tag}", compiler_params=_cp(dimension_semantics=("arbitrary", "arbitrary")),
    )(a, b)


def _s5_discretise(a_re, a_im, log_dt, b_re, b_im):
    dt = jnp.exp(log_dt)[:, None]
    mag = jnp.exp(a_re * dt)
    ar = mag * jnp.cos(a_im * dt)
    ai = mag * jnp.sin(a_im * dt)
    den = a_re * a_re + a_im * a_im
    f_re = ((ar - 1.0) * a_re + ai * a_im) / den
    f_im = (ai * a_re - (ar - 1.0) * a_im) / den
    bb_re = f_re[..., None] * b_re - f_im[..., None] * b_im
    bb_im = f_re[..., None] * b_im + f_im[..., None] * b_re
    return ar, ai, bb_re, bb_im


def _block_diag(blocks):
    g, r, c = blocks.shape
    eye = jnp.eye(g, dtype=blocks.dtype)
    return (blocks[:, :, None, :] * eye[:, None, :, None]).reshape(g * r, g * c)


def _block_diag_extract(m, g):
    r = m.shape[0] // g
    c = m.shape[1] // g
    eye = jnp.eye(g, dtype=m.dtype)
    return jnp.sum(m.reshape(g, r, g, c) * eye[:, None, :, None], axis=2)


GROUPS_PER_SLAB = N_GROUPS // N_USLAB


def _slab_diag(blocks):
    return jnp.stack([_block_diag(blocks[GROUPS_PER_SLAB * j:GROUPS_PER_SLAB * (j + 1)]) for j in range(N_USLAB)])


def _slab_diag_extract(m):
    return jnp.concatenate([_block_diag_extract(m[j], GROUPS_PER_SLAB) for j in range(N_USLAB)])


def _state_slabs(v):
    return jnp.broadcast_to(v.reshape(N_SLAB, 1, LANES), (N_SLAB, SUBLANES, LANES))


def _tril():
    return jnp.tril(jnp.ones((CHUNK, CHUNK), dtype=bool))


def _layer_params(w, l):
    row = lambda v: v.reshape(1, -1)
    t = lambda m: jnp.swapaxes(m, -1, -2)
    ar, ai, bb_re, bb_im = _s5_discretise(w["A_re"][l], w["A_im"][l], w["log_dt"][l], w["B_re"][l], w["B_im"][l])
    bbt3 = jnp.concatenate([_slab_diag(t(bb_re)), _slab_diag(t(bb_im))], axis=2).astype(BF16)
    ct3 = jnp.concatenate([_slab_diag(t(w["C_re"][l])), -_slab_diag(t(w["C_im"][l]))], axis=1).astype(BF16)
    ws = jnp.where(_tril()[None], w["w_spatial"][l], 0.0)
    pair = lambda m: jnp.stack([jnp.concatenate([m[2 * q], m[2 * q + 1]], axis=1)
                                for q in range(SGU_HEADS // 2)]).astype(BF16)
    wp = _block_diag(w["w_pool"][l]).astype(BF16)
    p = dict(
        g_mix=row(w["g_mix"][l]), g_ffn=row(w["g_ffn"][l]), d_skip=row(w["D_skip"][l]), b_glu=row(w["b_glu"][l]),
        pool_scale=row(w["pool_scale"][l]), sgu_ln_g=row(w["sgu_ln_g"][l]), sgu_ln_b=row(w["sgu_ln_b"][l]),
        a_re8=_state_slabs(ar), a_im8=_state_slabs(ai),
        bbt3=bbt3, bb3=t(bbt3), ct3=ct3, cb3=t(ct3),
        w_pool_bd=wp, w_pool_bd_t=t(wp), ws_pair=pair(ws), ws_pair_t=pair(t(ws)),
        bias_sp=jnp.repeat(t(w["b_spatial"][l]), SGU_HEAD_DIM, axis=1),
    )
    return p


MIX_WEIGHTS = ("w_in", "w_glu")
FFN_WEIGHTS = ("w_out", "w_gate", "w_up", "w_down")


def _with_big(p, mats):
    t = lambda m: jnp.swapaxes(m, -1, -2)
    for n, m in mats.items():
        p[n], p[n + "_t"] = (t(m), m) if n in TRANSPOSED else (m, t(m))


def _rows_sum(v):
    return jnp.sum(v, axis=0)


ATB_COLS = 512


def _after(v, token):
    return v if token is None else v + token


def _layer_bwd(dx2, sv, p, w, l, tag, on_grads, token):
    t = lambda m: jnp.swapaxes(m, -1, -2)
    dx1, da, db, dc, dgt, dup, act, dg_ffn = _blk_bwd(dx2, sv["x1"], sv["gt"], sv["up"],
                                                      dict(p, g_ffn=_after(p["g_ffn"], token)), tag)
    token = on_grads(l, "ffn", {
        "w_down": _atb(act, dx2, ATB_COLS, tag + "_wd"), "w_gate": _atb(dgt, sv["h2"], ATB_COLS, tag + "_wg"),
        "w_up": _atb(dup, sv["h2"], ATB_COLS, tag + "_wu"), "w_out": _atb(sv["ycat"], dx1, ATB_COLS, tag + "_wo")})
    g = {}
    g["g_ffn"] = _rows_sum(dg_ffn)
    dza, dct3, dbbt3, dar8, dai8, dd8, dwglu, dbglu8 = _s5_bwd(
        da, sv["za"], sv["y"], sv["h_re"], sv["h_im"], dict(p, d_skip=_after(p["d_skip"], token)), tag)
    dzb, dwp, dsc8 = _pool_bwd(db, sv["pooled"], p, tag)
    dzuv, dws, dbias, dlng8, dlnb8 = _sgu_bwd(dc, sv["zuv"], p, tag)
    dx0, dz, dg_mix = _mix_in_bwd(dza, dzb, dzuv, sv["x0"], dx1, p, tag)
    token = on_grads(l, "mix", {"w_in": _atb(dz, sv["h1"], D_MODEL, tag + "_wi"), "w_glu": dwglu})
    g["g_mix"] = _rows_sum(dg_mix)
    g["b_glu"] = _rows_sum(dbglu8)
    g["D_skip"] = _rows_sum(dd8)
    half = N_STATE // N_USLAB
    g["C_re"] = t(_slab_diag_extract(dct3[:, :half, :]))
    g["C_im"] = -t(_slab_diag_extract(dct3[:, half:, :]))
    dar = jnp.sum(dar8, axis=1).reshape(N_GROUPS, SSM_STATE)
    dai = jnp.sum(dai8, axis=1).reshape(N_GROUPS, SSM_STATE)
    dbb_re = t(_slab_diag_extract(dbbt3[:, :, :half]))
    dbb_im = t(_slab_diag_extract(dbbt3[:, :, half:]))
    _, disc_vjp = jax.vjp(_s5_discretise, w["A_re"][l], w["A_im"][l], w["log_dt"][l], w["B_re"][l], w["B_im"][l])
    g["A_re"], g["A_im"], g["log_dt"], g["B_re"], g["B_im"] = disc_vjp((dar, dai, dbb_re, dbb_im))
    g["w_pool"] = _block_diag_extract(dwp, len(POOL_WINDOWS))
    g["pool_scale"] = _rows_sum(dsc8)
    g["sgu_ln_g"] = _rows_sum(dlng8)
    g["sgu_ln_b"] = _rows_sum(dlnb8)
    g["w_spatial"] = jnp.where(_tril()[None], dws, 0.0)
    g["b_spatial"] = t(jnp.sum(dbias.reshape(CHUNK, SGU_HEADS, SGU_HEAD_DIM), axis=-1))
    return dx0, g, token


def _local_step(x, target, w, get_big, on_grads):
    params = [_layer_params(w, l) for l in range(DEPTH)]
    saved = []
    h = x
    for l in range(DEPTH):
        p, tag = params[l], f"l{l}"
        _with_big(p, get_big(l, "mix", [h]))
        za, zb, zuv, h1 = _mix_in_fwd(h, p["g_mix"], p["w_in"], tag)
        oa, y, h_re, h_im = _s5_fwd(za, p, tag)
        ob, pooled = _pool_fwd(zb, p, tag)
        oc = _sgu_fwd(zuv, p, tag)
        _with_big(p, get_big(l, "ffn", [oa, ob, oc]))
        x1, x2, h2, gt, up, ycat = _blk_fwd(h, oa, ob, oc, p, tag)
        saved.append(dict(x0=h, za=za, zuv=zuv, h1=h1, ycat=ycat, y=y, h_re=h_re, h_im=h_im, pooled=pooled, x1=x1,
                          h2=h2, gt=gt, up=up))
        h = x2
    dx, loss8, dgf8 = _head(h, target, w["g_final"].reshape(1, -1))
    grads = [None] * DEPTH
    token = None
    for l in reversed(range(DEPTH)):
        dx, grads[l], token = _layer_bwd(dx, saved[l], params[l], w, l, f"l{l}", on_grads, token)
    g = {n: jnp.stack([grads[l][n] for l in range(DEPTH)]) for n in SMALL if n != "g_final"}
    g["g_final"] = _rows_sum(dgf8)
    return loss8[0, 0], dx, g


_ANY = pl.BlockSpec(memory_space=pl.ANY)
_MESH = pl.DeviceIdType.MESH


def _place():
    return lax.axis_index("x"), lax.axis_index("y"), lax.axis_index("c")


def _other_chips(x, y):
    return [(1 - x, y), (x, 1 - y), (1 - x, 1 - y)]


def _dma_sems(n):
    return pltpu.SemaphoreType.DMA((n,))


def _remote(src, dst, send_sems, recv_sems, k, to):
    return pltpu.make_async_remote_copy(src_ref=src, dst_ref=dst, send_sem=send_sems.at[k], recv_sem=recv_sems.at[k],
                                        device_id=to, device_id_type=_MESH)


_HBM = pl.BlockSpec(memory_space=pltpu.HBM)
_SEM = pl.BlockSpec(memory_space=pltpu.SEMAPHORE)
_EFFECT = pltpu.SideEffectType.DATAFLOW_SIDE_EFFECTING
N_REL = N_CHIPS - 1


def _gather_plan(x, y, c, srcs, lands):
    me = 2 * x + y
    return [(s, l.at[pl.ds(me * s.shape[0], s.shape[0])], (cx, cy, c))
            for s, l in zip(srcs, lands) for cx, cy in _other_chips(x, y)]


def _slab_plan(x, y, c, srcs, lands):
    return [(s.at[2 * cx + cy], l.at[j], (cx, cy, c))
            for s, l in zip(srcs, lands) for j, (cx, cy) in enumerate(_other_chips(x, y))]


def _plan_copies(plan, srcs, lands, send_sems, recv_sems):
    x, y, c = _place()
    return [_remote(s, d, send_sems, recv_sems, k, to) for k, (s, d, to) in enumerate(plan(x, y, c, srcs, lands))]


def _hbm(a):
    return pltpu.with_memory_space_constraint(a, pltpu.HBM)


def _everyone_plan(x, y, c, srcs, lands):
    me = 4 * x + 2 * y + c
    peers = [(x, y, 1 - c)] + [(cx, cy, cc) for cx, cy in _other_chips(x, y) for cc in (c, 1 - c)]
    return [(s, l.at[me], peer) for s, l in zip(srcs, lands) for peer in peers]


def _copies_start(name, plan, srcs, lands, fanout=N_REL):
    n = len(srcs)
    ncopies = fanout * n

    def body(*refs):
        for cp in _plan_copies(plan, refs[:n], refs[n:2 * n], refs[2 * n], refs[2 * n + 1]):
            cp.start()
        refs[-1][...] = jnp.zeros_like(refs[-1])

    ref_out = [pltpu.HBM(a.shape, a.dtype) for a in (*srcs, *lands)]
    out = pl.pallas_call(
        body, name=name, in_specs=[_HBM] * (2 * n),
        out_shape=(_dma_sems(ncopies), _dma_sems(ncopies), *ref_out, jax.ShapeDtypeStruct((SUBLANES, LANES), F32)),
        out_specs=(_SEM, _SEM, *[_HBM] * (2 * n), pl.BlockSpec(memory_space=pltpu.VMEM)),
        input_output_aliases={i: 2 + i for i in range(2 * n)},
        compiler_params=pltpu.CompilerParams(has_side_effects=_EFFECT),
    )(*[_hbm(a) for a in (*srcs, *lands)])
    return dict(name=name, plan=plan, sems=out[:2], srcs=out[2:2 + n], lands=out[2 + n:2 + 2 * n], token=out[-1][0, 0],
                token_tile=out[-1])


def _copies_wait(started, after):
    n = len(started["srcs"])
    plan = started["plan"]

    def body(*refs):
        for cp in _plan_copies(plan, refs[:n], refs[n:2 * n], refs[2 * n], refs[2 * n + 1]):
            cp.wait_send()
            cp.wait_recv()

    args = (*started["srcs"], *started["lands"])
    out = pl.pallas_call(
        body, name=started["name"] + "_wait", out_shape=[pltpu.HBM(a.shape, a.dtype) for a in args],
        in_specs=[_HBM] * (2 * n) + [_SEM, _SEM] + [_ANY] * len(after), out_specs=[_HBM] * (2 * n),
        input_output_aliases={i: i for i in range(2 * n)},
        compiler_params=pltpu.CompilerParams(has_side_effects=_EFFECT),
    )(*args, *started["sems"], *after)
    return out[n:]


def _swap_halves(g4s, tag):
    nw = len(g4s)

    def body(*refs):
        ins, outs = refs[:nw], refs[nw:2 * nw]
        send_sems, recv_sems = refs[2 * nw:]
        x, y, c = _place()
        copies = [_remote(ins[i].at[:, 1 - c], outs[i], send_sems, recv_sems, i, (x, y, 1 - c)) for i in range(nw)]
        for cp in copies:
            cp.start()
        for cp in copies:
            cp.wait()

    return pl.pallas_call(
        body, out_shape=[jax.ShapeDtypeStruct((g.shape[0],) + g.shape[2:], g.dtype) for g in g4s],
        in_specs=[_ANY] * nw, out_specs=[_ANY] * nw, scratch_shapes=[_dma_sems(nw), _dma_sems(nw)],
        name=f"swap_halves_{tag}",
    )(*g4s)


def _share_halves(fs, layer, tag):
    nw = len(fs)

    def body(*refs):
        ins = refs[:nw]
        send_sems, recv_sems = refs[2 * nw:]
        x, y, c = _place()

        def half(i, who):
            h = ins[i].shape[1] // 2
            return ins[i].at[layer, pl.ds(who * h, h)]

        sends = [_remote(half(i, c), half(i, c), send_sems, recv_sems, i, (x, y, 1 - c)) for i in range(nw)]
        for cp in sends:
            cp.start()
        for i in range(nw):
            sends[i].wait_send()
            _remote(half(i, c), half(i, 1 - c), send_sems, recv_sems, i, (x, y, 1 - c)).wait_recv()

    return pl.pallas_call(
        body, out_shape=[jax.ShapeDtypeStruct(f.shape, f.dtype) for f in fs], in_specs=[_ANY] * nw,
        out_specs=[_ANY] * nw, input_output_aliases={i: i for i in range(nw)},
        scratch_shapes=[_dma_sems(nw), _dma_sems(nw)], name=f"share_halves_{tag}",
    )(*fs)


def _add_halves(g4s, recvs, sel, tag):
    nw = len(g4s)

    def body(sel_ref, *refs):
        for i in range(nw):
            refs[2 * nw + i][...] = (refs[i][...] + refs[nw + i][...]).astype(BF16)

    mine = [pl.BlockSpec((None, None) + g.shape[2:], lambda k, s: (k, s[0], 0, 0)) for g in g4s]
    slab = [pl.BlockSpec((None,) + g.shape[2:], lambda k, s: (k, 0, 0)) for g in g4s]
    return pl.pallas_call(
        body, grid_spec=pltpu.PrefetchScalarGridSpec(num_scalar_prefetch=1, grid=(N_CHIPS,), in_specs=mine + slab,
                                                     out_specs=slab),
        out_shape=[jax.ShapeDtypeStruct(r.shape, BF16) for r in recvs], name=f"add_halves_{tag}",
        compiler_params=_cp(dimension_semantics=("arbitrary",)),
    )(sel, *g4s, *recvs)


def _add_chips(ps, slabs, fs, layer, sel, tag):
    nw = len(ps)
    old = [f for f in fs if f is not None]

    def body(sel_ref, *refs):
        outs = refs[2 * nw + len(old):]
        for i in range(nw):
            acc = refs[i][...].astype(F32)
            for j in range(N_REL):
                acc = acc + refs[nw + i][j].astype(F32)
            outs[i][...] = acc

    shapes = [(DEPTH, 2 * p.shape[1], p.shape[2]) for p in ps]
    in_specs = [pl.BlockSpec((None,) + p.shape[1:], lambda i, s: (s[1], 0, 0)) for p in ps]
    in_specs += [pl.BlockSpec(sl.shape, lambda i, s: (0, 0, 0)) for sl in slabs]
    in_specs += [_ANY] * len(old)
    first_old = 1 + 2 * nw
    aliases, k = {}, 0
    for i, f in enumerate(fs):
        if f is not None:
            aliases[first_old + k] = i
            k += 1
    return pl.pallas_call(
        body, grid_spec=pltpu.PrefetchScalarGridSpec(
            num_scalar_prefetch=1, grid=(1,), in_specs=in_specs,
            out_specs=[pl.BlockSpec((None,) + p.shape[1:], lambda i, s: (layer, s[0], 0)) for p in ps]),
        out_shape=[jax.ShapeDtypeStruct(sh, F32) for sh in shapes], input_output_aliases=aliases,
        name=f"add_chips_{tag}", compiler_params=_cp(dimension_semantics=("arbitrary",)),
    )(sel, *ps, *slabs, *old)


def _adamw_math(w, g, m, v):
    m = ADAM_B1 * m + (1.0 - ADAM_B1) * g
    v = ADAM_B2 * v + (1.0 - ADAM_B2) * (g * g)
    m_hat = m / (1.0 - ADAM_B1 ** ADAM_STEP)
    v_hat = v / (1.0 - ADAM_B2 ** ADAM_STEP)
    delta = -ADAM_LR * (m_hat / (jnp.sqrt(v_hat) + ADAM_EPS) + ADAM_WD * w)
    return delta, m, v


ADAM_ROWS = 512


def _row_tile(rows, most):
    return max(t for t in range(SUBLANES, most + 1, SUBLANES) if rows % t == 0)


def _adamw(w, g, m, v, tag):
    shape = w.shape
    cols = shape[-1]
    flat = lambda a: a.reshape(-1, cols)
    rows = w.size // cols
    tr = _row_tile(rows, ADAM_ROWS)

    def body(w_ref, g_ref, m_ref, v_ref, d_ref, nm_ref, nv_ref):
        d, nm, nv = _adamw_math(w_ref[...], g_ref[...], m_ref[...], v_ref[...])
        d_ref[...] = d
        nm_ref[...] = nm
        nv_ref[...] = nv

    spec = _row(tr, cols)
    out = pl.pallas_call(
        body, grid=(rows // tr,), in_specs=[spec] * 4, out_specs=[spec] * 3,
        out_shape=[jax.ShapeDtypeStruct((rows, cols), F32)] * 3, name=f"adamw_{tag}",
        compiler_params=_cp(dimension_semantics=("arbitrary",)),
    )(flat(w), flat(g), flat(m), flat(v))
    return [o.reshape(shape) for o in out]


SMALL_TILE = 384
PRECISE = ("g_final",)
COARSE = [n for n in SMALL if n not in PRECISE]


def _small_reduce_adamw(gathered, w, m, v, tag):
    rows = w.shape[0]
    tr = math.gcd(rows, SMALL_TILE)

    def body(ga_ref, w_ref, m_ref, v_ref, g_ref, d_ref, nm_ref, nv_ref):
        g = ga_ref[0].astype(F32)
        for k in range(1, N_DEV):
            g = g + ga_ref[k].astype(F32)
        g_ref[...] = g
        d, nm, nv = _adamw_math(w_ref[...], g, m_ref[...], v_ref[...])
        d_ref[...] = d
        nm_ref[...] = nm
        nv_ref[...] = nv

    spec = _row(tr, LANES)
    return pl.pallas_call(
        body, grid=(rows // tr,),
        in_specs=[pl.BlockSpec((N_DEV, tr, LANES), lambda i: (0, i, 0)), spec, spec, spec], out_specs=[spec] * 4,
        out_shape=[jax.ShapeDtypeStruct((rows, LANES), F32)] * 4, name=f"small_reduce_adamw_{tag}",
        compiler_params=_cp(dimension_semantics=("arbitrary",)),
    )(gathered, w, m, v)


def _exchange_form(n, a):
    return jnp.swapaxes(a, 1, 2) if n in TRANSPOSED else a


PACK_ROWS = 16


def _pack(vals, names, extra=None):
    parts = [vals[n].reshape(-1) for n in names] + ([] if extra is None else [extra.reshape(1)])
    flat = jnp.concatenate(parts)
    rows = -(-flat.size // (LANES * PACK_ROWS)) * PACK_ROWS
    return jnp.pad(flat, (0, rows * LANES - flat.size)).reshape(rows, LANES)


def _unpack(buf, like, names):
    flat = buf.reshape(-1)
    out, off = {}, 0
    for n in names:
        out[n] = flat[off:off + like[n].size].reshape(like[n].shape)
        off += like[n].size
    return out, flat[off:]


def kernel(x, g_mix, w_in, A_re, A_im, log_dt, B_re, B_im, C_re, C_im, D_skip, w_glu, b_glu, w_pool, pool_scale, sgu_ln_g, sgu_ln_b, w_spatial, b_spatial, w_out, g_ffn, w_gate, w_up, w_down, g_final, loss_target, m_g_mix, m_w_in, m_A_re, m_A_im, m_log_dt, m_B_re, m_B_im, m_C_re, m_C_im, m_D_skip, m_w_glu, m_b_glu, m_w_pool, m_pool_scale, m_sgu_ln_g, m_sgu_ln_b, m_w_spatial, m_b_spatial, m_w_out, m_g_ffn, m_w_gate, m_w_up, m_w_down, m_g_final, v_g_mix, v_w_in, v_A_re, v_A_im, v_log_dt, v_B_re, v_B_im, v_C_re, v_C_im, v_D_skip, v_w_glu, v_b_glu, v_w_pool, v_pool_scale, v_sgu_ln_g, v_sgu_ln_b, v_w_spatial, v_b_spatial, v_w_out, v_g_ffn, v_w_gate, v_w_up, v_w_down, v_g_final):
    loc = locals()
    w = {n: loc[n] for n in WEIGHTS}
    m = {n: loc["m_" + n] for n in WEIGHTS}
    v = {n: loc["v_" + n] for n in WEIGHTS}
    sel = jnp.stack([lax.axis_index("c"), 2 * lax.axis_index("x") + lax.axis_index("y")]).astype(jnp.int32)

    chip = sel[1]

    halves = [(l, half) for l in range(DEPTH) for half in ("mix", "ffn")]
    names = {"mix": MIX_WEIGHTS, "ffn": FFN_WEIGHTS}
    started = {}
    token = None
    for l, half in halves:
        shards = [_after(_exchange_form(n, w[n])[l], token).astype(BF16) for n in names[half]]
        lands = [lax.dynamic_update_slice(lax.empty((N_CHIPS * s.shape[0], s.shape[1]), BF16), s,
                                          (chip * s.shape[0], 0)) for s in shards]
        started[l, half] = _copies_start(f"weights_l{l}_{half}", _gather_plan, shards, lands)
        token = started[l, half]["token"]
    w = dict(w, g_mix=_after(w["g_mix"], token))

    def get_big(l, half, after):
        return dict(zip(names[half], _copies_wait(started[l, half], after)))

    result = {n: None for n in BIG}
    in_flight = []

    def finish(after):
        for part, ex, ns, l, tag in in_flight:
            bufs = _add_chips(part, _copies_wait(ex, after), [result[n] for n in ns], l, sel, tag)
            for n, f in zip(ns, _share_halves(bufs, l, tag)):
                result[n] = f
        in_flight.clear()

    def on_grads(l, half, grads):
        ns = list(grads)
        tag = f"l{l}_{half}"
        finish([grads[ns[0]]])
        g4s = [grads[n].reshape(N_CHIPS, 2, grads[n].shape[0] // (2 * N_CHIPS), grads[n].shape[1]) for n in ns]
        part = _add_halves(g4s, _swap_halves(g4s, tag), sel, tag)
        slabs = [lax.empty((N_REL,) + p.shape[1:], BF16) for p in part]
        ex = _copies_start(f"grads_{tag}", _slab_plan, part, slabs)
        in_flight.append((part, ex, ns, l, tag))
        return ex["token"]

    loss_local, dx, g = _local_step(x[0], loss_target[0], w, get_big, on_grads)

    me = 2 * chip + sel[0]
    blocks = [_pack(g, COARSE).astype(BF16), _pack(g, PRECISE, loss_local)]
    lands = [lax.dynamic_update_slice(lax.empty((N_DEV,) + b.shape, b.dtype), b[None], (me, 0, 0)) for b in blocks]
    small = _copies_start("small_grads", _everyone_plan, blocks, lands, fanout=N_DEV - 1)

    finish([small["token_tile"]])
    grads, deltas, new_m, new_v = {}, {}, {}, {}
    for n in BIG:
        grads[n] = _exchange_form(n, result[n])
        deltas[n], new_m[n], new_v[n] = _adamw(w[n], grads[n], m[n], v[n], n)

    gathered = _copies_wait(small, [new_v[n] for n in BIG])
    zero = jnp.zeros((), F32)
    loss = None
    for names_k, extra, block, tag in ((COARSE, None, gathered[0], "coarse"), (PRECISE, zero, gathered[1], "precise")):
        outs = _small_reduce_adamw(block, _pack(w, names_k, extra), _pack(m, names_k, extra), _pack(v, names_k, extra),
                                   tag)
        for store, buf in zip((grads, deltas, new_m, new_v), outs):
            vals, rest = _unpack(buf, w, names_k)
            store.update(vals)
            if store is grads and extra is not None:
                loss = rest[0]
    return (loss, dx[None], *[grads[n] for n in WEIGHTS], *[deltas[n] for n in WEIGHTS],
            *[new_m[n] for n in WEIGHTS], *[new_v[n] for n in WEIGHTS])
```

```python
import math

import jax
import jax.numpy as jnp
from jax import lax
from jax.experimental import pallas as pl
from jax.experimental.pallas import tpu as pltpu

F32 = jnp.float32
BF16 = jnp.bfloat16

D_MODEL = 1024
DEPTH = 2
D_SSM = 384
SSM_GROUP = 16
N_GROUPS = 24
SSM_STATE = 64
N_STATE = N_GROUPS * SSM_STATE
POOL_WINDOWS = (2, 4, 8, 16)
POOL_GROUP = 64
D_POOL = 256
MAX_WINDOW = 16
SGU_HEADS = 6
SGU_HEAD_DIM = 64
D_SGU = 384
CHUNK = 128
D_IN = D_SSM + D_POOL + 2 * D_SGU
D_FF = 2816
EPS = 1e-6

ADAM_LR = 0.001
ADAM_B1 = 0.9
ADAM_B2 = 0.999
ADAM_EPS = 1e-08
ADAM_WD = 0.01
ADAM_STEP = 10

LANES = 128
SUBLANES = 8
N_SLAB = N_STATE // LANES
VMEM_LIMIT = 56 * 1024 * 1024

TS = 512
TS_FFN = 256

WEIGHTS = ['g_mix', 'w_in', 'A_re', 'A_im', 'log_dt', 'B_re', 'B_im', 'C_re', 'C_im', 'D_skip', 'w_glu', 'b_glu',
           'w_pool', 'pool_scale', 'sgu_ln_g', 'sgu_ln_b', 'w_spatial', 'b_spatial', 'w_out', 'g_ffn', 'w_gate',
           'w_up', 'w_down', 'g_final']
BIG = ['w_in', 'w_glu', 'w_out', 'w_gate', 'w_up', 'w_down']
SMALL = [n for n in WEIGHTS if n not in BIG]
TRANSPOSED = ("w_in", "w_gate", "w_up")
N_CHIPS = 4
N_DEV = 8


def _cp(**kw):
    return pltpu.CompilerParams(vmem_limit_bytes=VMEM_LIMIT, **kw)


def _row(ts, n):
    return pl.BlockSpec((ts, n), lambda i: (i, 0))


def _const(shape):
    nd = len(shape)
    return pl.BlockSpec(shape, lambda i: (0,) * nd, pipeline_mode=pl.Buffered(1))


def _acc(shape):
    nd = len(shape)
    return pl.BlockSpec(shape, lambda i: (0,) * nd)


def _dot(a, b):
    return jnp.dot(a, b, preferred_element_type=F32)


def _dot_tn(a, b):
    return lax.dot_general(a, b, (((0,), (0,)), ((), ())), preferred_element_type=F32)


def _dot_nt(a, b):
    return lax.dot_general(a, b, (((1,), (1,)), ((), ())), preferred_element_type=F32)


_G0 = math.sqrt(2.0 / math.pi)
_G1 = 0.044715


def _gelu(x):
    return 0.5 * x * (1.0 + jnp.tanh(_G0 * (x + _G1 * x * x * x)))


def _gelu_grad(x):
    t = jnp.tanh(_G0 * (x + _G1 * x * x * x))
    return 0.5 * (1.0 + t) + 0.5 * x * (1.0 - t * t) * (_G0 * (1.0 + 3.0 * _G1 * x * x))


def _sigmoid(x):
    return 1.0 / (1.0 + jnp.exp(-x))


def _rms(x):
    r = lax.rsqrt(jnp.mean(x * x, axis=-1, keepdims=True) + EPS)
    return x * r, r


def _rms_bwd(dh, n, r, g):
    dn = dh * g
    return r * (dn - n * jnp.mean(dn * n, axis=-1, keepdims=True)), dh * n


def _colsum8(v):
    rows, n = v.shape
    return jnp.sum(v.reshape(rows // SUBLANES, SUBLANES, n), axis=0)


def _mix_in_fwd(x, g, w, tag):
    s = x.shape[0]

    def body(x_ref, g_ref, w_ref, za_ref, zb_ref, zuv_ref, h_ref):
        n, _ = _rms(x_ref[...])
        h = (n * g_ref[...]).astype(BF16)
        z = _dot_nt(h, w_ref[...])
        za_ref[...] = z[:, :D_SSM]
        zb_ref[...] = z[:, D_SSM:D_SSM + D_POOL]
        zuv_ref[...] = z[:, D_SSM + D_POOL:]
        h_ref[...] = h

    return pl.pallas_call(
        body, grid=(s // TS,),
        in_specs=[_row(TS, D_MODEL), _const((1, D_MODEL)), _const((D_IN, D_MODEL))],
        out_specs=[_row(TS, D_SSM), _row(TS, D_POOL), _row(TS, 2 * D_SGU), _row(TS, D_MODEL)],
        out_shape=[jax.ShapeDtypeStruct((s, D_SSM), F32), jax.ShapeDtypeStruct((s, D_POOL), F32),
                   jax.ShapeDtypeStruct((s, 2 * D_SGU), F32), jax.ShapeDtypeStruct((s, D_MODEL), BF16)],
        name=f"mix_in_fwd_{tag}", compiler_params=_cp(dimension_semantics=("arbitrary",)),
    )(x, g, w)


def _cmul(ar, ai, br, bi):
    return ar * br - ai * bi, ar * bi + ai * br


def _cpow(ar, ai, n):
    assert n & (n - 1) == 0
    while n > 1:
        ar, ai = _cmul(ar, ai, ar, ai)
        n //= 2
    return ar, ai


def _to_slabs(ref, v):
    for j in range(N_SLAB):
        ref[j] = v[:, LANES * j:LANES * (j + 1)]


def _from_slabs(ref):
    return jnp.concatenate([ref[j] for j in range(N_SLAB)], axis=1)


N_USLAB = D_SSM // LANES
SEG = TS // SUBLANES


def _interleave_rows(v, stage, dst):
    for j in range(N_USLAB):
        stage[j] = v[:, LANES * j:LANES * (j + 1)]

    def step(k, carry):
        dst[:, pl.ds(pl.multiple_of(k * SUBLANES, SUBLANES), SUBLANES), :] = stage[:, pl.ds(k, SUBLANES, stride=SEG), :]
        return carry

    lax.fori_loop(0, SEG, step, 0)
    return jnp.concatenate([dst[j] for j in range(N_USLAB)], axis=1)


def _deinterleave_rows(v, stage, dst):
    for j in range(N_USLAB):
        stage[j] = v[:, LANES * j:LANES * (j + 1)]

    def step(k, carry):
        dst[:, pl.ds(k, SUBLANES, stride=SEG), :] = stage[:, pl.ds(pl.multiple_of(k * SUBLANES, SUBLANES), SUBLANES), :]
        return carry

    lax.fori_loop(0, SEG, step, 0)
    return jnp.concatenate([dst[j] for j in range(N_USLAB)], axis=1)


def _scan_rows(k):
    return pl.ds(pl.multiple_of(k * SUBLANES, SUBLANES), SUBLANES)


SLABS_PER_USLAB = N_SLAB // N_USLAB
S5_IN = (N_USLAB, LANES, 2 * N_STATE // N_USLAB)
S5_OUT = (N_USLAB, 2 * N_STATE // N_USLAB, LANES)


def _lanes(v, j):
    return v[:, LANES * j:LANES * (j + 1)]


def _state_split(re_ref, im_ref, j, v):
    for q in range(SLABS_PER_USLAB):
        re_ref[SLABS_PER_USLAB * j + q] = _lanes(v, q)
        im_ref[SLABS_PER_USLAB * j + q] = _lanes(v, SLABS_PER_USLAB + q)


def _state_cat(re_ref, im_ref, j):
    idx = range(SLABS_PER_USLAB * j, SLABS_PER_USLAB * (j + 1))
    return jnp.concatenate([re_ref[q] for q in idx] + [im_ref[q] for q in idx], axis=1).astype(BF16)


def _s5_fwd(u, p, tag):
    s = u.shape[0]
    seg = SEG

    def body(u_ref, bbt_ref, ar_ref, ai_ref, ct_ref, dsk_ref, wglu_ref, bglu_ref,
             oa_ref, y_ref, hr_ref, hi_ref, sr, si, er, ei, ir, ii, cr, ci, stage, perm):
        @pl.when(pl.program_id(0) == 0)
        def _():
            cr[...] = jnp.zeros_like(cr)
            ci[...] = jnp.zeros_like(ci)

        uv = _interleave_rows(u_ref[...], stage, perm)
        ub = uv.astype(BF16)
        for j in range(N_USLAB):
            _state_split(sr, si, j, _dot(_lanes(ub, j), bbt_ref[j]))
        ar = ar_ref[...]
        ai = ai_ref[...]

        def local(k, h):
            rows = _scan_rows(k)
            hr, hi = _cmul(ar, ai, h[0], h[1])
            return hr + sr[:, rows, :], hi + si[:, rows, :]

        zero = jnp.zeros((N_SLAB, SUBLANES, LANES), F32)
        e_r, e_i = lax.fori_loop(0, seg, local, (zero, zero))
        er[...] = e_r
        ei[...] = e_i
        pr, pi = _cpow(ar[:, 0:1, :], ai[:, 0:1, :], seg)
        c_r = cr[...]
        c_i = ci[...]
        for j in range(SUBLANES):
            ir[:, j:j + 1, :] = c_r
            ii[:, j:j + 1, :] = c_i
            n_r, n_i = _cmul(pr, pi, c_r, c_i)
            c_r = n_r + er[:, j:j + 1, :]
            c_i = n_i + ei[:, j:j + 1, :]
        cr[...] = c_r
        ci[...] = c_i

        def full(k, h):
            rows = _scan_rows(k)
            hr, hi = _cmul(ar, ai, h[0], h[1])
            hr = hr + sr[:, rows, :]
            hi = hi + si[:, rows, :]
            sr[:, rows, :] = hr
            si[:, rows, :] = hi
            return hr, hi

        lax.fori_loop(0, seg, full, (ir[...], ii[...]))
        hr_ref[...] = _from_slabs(sr)
        hi_ref[...] = _from_slabs(si)
        y = jnp.concatenate([_dot(_state_cat(sr, si, j), ct_ref[j]) for j in range(N_USLAB)], axis=1)
        y = y + dsk_ref[...] * uv
        y_ref[...] = y
        g = _gelu(y)
        pre = _dot(g.astype(BF16), wglu_ref[...]) + bglu_ref[...]
        oa_ref[...] = _deinterleave_rows(g * _sigmoid(pre), stage, perm).astype(BF16)

    slab = (N_SLAB, SUBLANES, LANES)
    uslab = (N_USLAB, TS, LANES)
    return pl.pallas_call(
        body, grid=(s // TS,),
        in_specs=[_row(TS, D_SSM), _const(S5_IN), _const(slab), _const(slab), _const(S5_OUT), _const((1, D_SSM)),
                  _const((D_SSM, D_SSM)), _const((1, D_SSM))],
        out_specs=[_row(TS, D_SSM), _row(TS, D_SSM), _row(TS, N_STATE), _row(TS, N_STATE)],
        out_shape=[jax.ShapeDtypeStruct((s, D_SSM), BF16), jax.ShapeDtypeStruct((s, D_SSM), F32),
                   jax.ShapeDtypeStruct((s, N_STATE), F32), jax.ShapeDtypeStruct((s, N_STATE), F32)],
        scratch_shapes=[pltpu.VMEM((N_SLAB, TS, LANES), F32), pltpu.VMEM((N_SLAB, TS, LANES), F32),
                        pltpu.VMEM(slab, F32), pltpu.VMEM(slab, F32), pltpu.VMEM(slab, F32), pltpu.VMEM(slab, F32),
                        pltpu.VMEM((N_SLAB, 1, LANES), F32), pltpu.VMEM((N_SLAB, 1, LANES), F32),
                        pltpu.VMEM(uslab, F32), pltpu.VMEM(uslab, F32)],
        name=f"s5_fwd_{tag}", compiler_params=_cp(dimension_semantics=("arbitrary",)),
    )(u, p["bbt3"], p["a_re8"], p["a_im8"], p["ct3"], p["d_skip"], p["w_glu"], p["b_glu"])


def _pool_consts():
    w = jnp.repeat(jnp.asarray(POOL_WINDOWS, F32), POOL_GROUP)[None, :]
    return w


def _window_sum(buf, first, rows, wl, step):
    acc = buf[pl.ds(first, rows), :]
    for j in range(1, MAX_WINDOW):
        term = buf[pl.ds(first + step * j, rows), :]
        acc = acc + (term if j < min(POOL_WINDOWS) else term * (wl > j).astype(F32))
    return acc


def _pool_count(i, rows, wl, offset=0):
    t = (i * TS + offset + 1).astype(F32) + lax.broadcasted_iota(jnp.int32, (rows, 1), 0).astype(F32)
    return jnp.minimum(t, wl)


def _pool_fwd(zb, p, tag):
    s = zb.shape[0]
    hb = TS // MAX_WINDOW

    def body(u_ref, halo_ref, wl_ref, w_ref, sc_ref, ob_ref, pooled_ref, buf):
        i = pl.program_id(0)
        uv = u_ref[...]
        buf[pl.ds(0, MAX_WINDOW), :] = jnp.where(i > 0, halo_ref[...], 0.0)
        buf[pl.ds(MAX_WINDOW, TS), :] = uv
        wl = wl_ref[...]
        pooled = (_window_sum(buf, MAX_WINDOW, TS, wl, -1) / _pool_count(i, TS, wl) - uv).astype(BF16)
        pooled_ref[...] = pooled
        ob_ref[...] = (_dot(pooled, w_ref[...]) * sc_ref[...]).astype(BF16)

    return pl.pallas_call(
        body, grid=(s // TS,),
        in_specs=[_row(TS, D_POOL),
                  pl.BlockSpec((MAX_WINDOW, D_POOL), lambda i: (jnp.maximum(i * hb - 1, 0), 0)),
                  _const((1, D_POOL)), _const((D_POOL, D_POOL)), _const((1, D_POOL))],
        out_specs=[_row(TS, D_POOL), _row(TS, D_POOL)],
        out_shape=[jax.ShapeDtypeStruct((s, D_POOL), BF16), jax.ShapeDtypeStruct((s, D_POOL), BF16)],
        scratch_shapes=[pltpu.VMEM((TS + MAX_WINDOW, D_POOL), F32)],
        name=f"pool_fwd_{tag}", compiler_params=_cp(dimension_semantics=("arbitrary",)),
    )(zb, zb, _pool_consts(), p["w_pool_bd"], p["pool_scale"])


def _sgu_mix(vl, wpair_ref, lo, hi):
    rows = vl.shape[0]
    chunks = []
    for c in range(rows // CHUNK):
        vc = vl[CHUNK * c:CHUNK * (c + 1), :]
        parts = []
        for q in range(SGU_HEADS // 2):
            vq = vc[:, LANES * q:LANES * (q + 1)]
            rhs = jnp.concatenate([vq * lo, vq * hi], axis=0).astype(BF16)
            parts.append(_dot(wpair_ref[q], rhs))
        chunks.append(jnp.concatenate(parts, axis=1))
    return jnp.concatenate(chunks, axis=0)


def _sgu_front(zuv, lng, lnb):
    zu = zuv[:, :D_SGU]
    zv = zuv[:, D_SGU:]
    u = _gelu(zu)
    v = _gelu(zv)
    mu = jnp.mean(v, axis=-1, keepdims=True)
    vc = v - mu
    rs = lax.rsqrt(jnp.mean(vc * vc, axis=-1, keepdims=True) + EPS)
    vn = vc * rs
    return zu, zv, u, vn, rs, vn * lng + lnb


def _half_masks():
    lane = lax.broadcasted_iota(jnp.int32, (1, LANES), 1)
    lo = (lane < SGU_HEAD_DIM).astype(F32)
    return lo, 1.0 - lo


def _sgu_fwd(zuv, p, tag):
    s = zuv.shape[0]

    def body(z_ref, lng_ref, lnb_ref, wp_ref, bias_ref, oc_ref):
        lo, hi = _half_masks()
        _, _, u, _, _, vl = _sgu_front(z_ref[...], lng_ref[...], lnb_ref[...])
        mixed = _sgu_mix(vl, wp_ref, lo, hi) + jnp.tile(bias_ref[...], (TS // CHUNK, 1))
        oc_ref[...] = (u * mixed).astype(BF16)

    return pl.pallas_call(
        body, grid=(s // TS,),
        in_specs=[_row(TS, 2 * D_SGU), _const((1, D_SGU)), _const((1, D_SGU)),
                  _const((SGU_HEADS // 2, CHUNK, 2 * CHUNK)), _const((CHUNK, D_SGU))],
        out_specs=_row(TS, D_SGU),
        out_shape=jax.ShapeDtypeStruct((s, D_SGU), BF16),
        name=f"sgu_fwd_{tag}", compiler_params=_cp(dimension_semantics=("arbitrary",)),
    )(zuv, p["sgu_ln_g"], p["sgu_ln_b"], p["ws_pair"], p["bias_sp"])


def _blk_fwd(x0, oa, ob, oc, p, tag):
    s = x0.shape[0]
    ts = TS_FFN

    def body(x0_ref, oa_ref, ob_ref, oc_ref, wo_ref, g_ref, wg_ref, wu_ref, wd_ref,
             x1_ref, x2_ref, h2_ref, gt_ref, up_ref, ycat_ref):
        ycat = jnp.concatenate([oa_ref[...], ob_ref[...], oc_ref[...]], axis=1)
        ycat_ref[...] = ycat
        x1 = x0_ref[...] + _dot(ycat, wo_ref[...])
        x1_ref[...] = x1
        n, _ = _rms(x1)
        h2 = (n * g_ref[...]).astype(BF16)
        h2_ref[...] = h2
        gt = _dot_nt(h2, wg_ref[...])
        up = _dot_nt(h2, wu_ref[...])
        gt_ref[...] = gt.astype(BF16)
        up_ref[...] = up.astype(BF16)
        act = (gt * _sigmoid(gt) * up).astype(BF16)
        x2_ref[...] = x1 + _dot(act, wd_ref[...])

    return pl.pallas_call(
        body, grid=(s // ts,),
        in_specs=[_row(ts, D_MODEL), _row(ts, D_SSM), _row(ts, D_POOL), _row(ts, D_SGU),
                  _const((D_MODEL, D_MODEL)), _const((1, D_MODEL)), _const((D_FF, D_MODEL)),
                  _const((D_FF, D_MODEL)), _const((D_FF, D_MODEL))],
        out_specs=[_row(ts, D_MODEL), _row(ts, D_MODEL), _row(ts, D_MODEL), _row(ts, D_FF), _row(ts, D_FF),
                   _row(ts, D_MODEL)],
        out_shape=[jax.ShapeDtypeStruct((s, D_MODEL), F32), jax.ShapeDtypeStruct((s, D_MODEL), F32),
                   jax.ShapeDtypeStruct((s, D_MODEL), BF16), jax.ShapeDtypeStruct((s, D_FF), BF16),
                   jax.ShapeDtypeStruct((s, D_FF), BF16), jax.ShapeDtypeStruct((s, D_MODEL), BF16)],
        name=f"blk_fwd_{tag}", compiler_params=_cp(dimension_semantics=("arbitrary",)),
    )(x0, oa, ob, oc, p["w_out"], p["g_ffn"], p["w_gate"], p["w_up"], p["w_down"])


def _blk_bwd(dx2, x1, gt, up, p, tag):
    s = dx2.shape[0]
    ts = TS_FFN

    def body(dx2_ref, x1_ref, gt_ref, up_ref, wd_ref, wgt_ref, wut_ref, wo_ref, g_ref,
             dx1_ref, da_ref, db_ref, dc_ref, dgt_ref, dup_ref, act_ref, dg_ref):
        @pl.when(pl.program_id(0) == 0)
        def _():
            dg_ref[...] = jnp.zeros_like(dg_ref)

        dx2v = dx2_ref[...]
        dact = _dot_nt(dx2v.astype(BF16), wd_ref[...])
        gf = gt_ref[...].astype(F32)
        uf = up_ref[...].astype(F32)
        sg = _sigmoid(gf)
        sl = gf * sg
        act_ref[...] = (sl * uf).astype(BF16)
        dgt = (dact * uf * (sg * (1.0 + gf * (1.0 - sg)))).astype(BF16)
        dup = (dact * sl).astype(BF16)
        dgt_ref[...] = dgt
        dup_ref[...] = dup
        dh2 = _dot(dgt, wgt_ref[...]) + _dot(dup, wut_ref[...])
        n, r = _rms(x1_ref[...])
        dxn, dgp = _rms_bwd(dh2, n, r, g_ref[...])
        dg_ref[...] += _colsum8(dgp)
        dx1 = dx2v + dxn
        dx1_ref[...] = dx1
        dy = _dot_nt(dx1.astype(BF16), wo_ref[...])
        da_ref[...] = dy[:, :D_SSM]
        db_ref[...] = dy[:, D_SSM:D_SSM + D_POOL]
        dc_ref[...] = dy[:, D_SSM + D_POOL:]

    return pl.pallas_call(
        body, grid=(s // ts,),
        in_specs=[_row(ts, D_MODEL), _row(ts, D_MODEL), _row(ts, D_FF), _row(ts, D_FF),
                  _const((D_FF, D_MODEL)), _const((D_FF, D_MODEL)), _const((D_FF, D_MODEL)),
                  _const((D_MODEL, D_MODEL)), _const((1, D_MODEL))],
        out_specs=[_row(ts, D_MODEL), _row(ts, D_SSM), _row(ts, D_POOL), _row(ts, D_SGU), _row(ts, D_FF),
                   _row(ts, D_FF), _row(ts, D_FF), _acc((SUBLANES, D_MODEL))],
        out_shape=[jax.ShapeDtypeStruct((s, D_MODEL), F32), jax.ShapeDtypeStruct((s, D_SSM), F32),
                   jax.ShapeDtypeStruct((s, D_POOL), F32), jax.ShapeDtypeStruct((s, D_SGU), F32),
                   jax.ShapeDtypeStruct((s, D_FF), BF16), jax.ShapeDtypeStruct((s, D_FF), BF16),
                   jax.ShapeDtypeStruct((s, D_FF), BF16), jax.ShapeDtypeStruct((SUBLANES, D_MODEL), F32)],
        name=f"blk_bwd_{tag}", compiler_params=_cp(dimension_semantics=("arbitrary",)),
    )(dx2, x1, gt, up, p["w_down"], p["w_gate"], p["w_up"], p["w_out"], p["g_ffn"])


def _s5_bwd(dout, u, y, h_re, h_im, p, tag):
    s = u.shape[0]
    nt = s // TS
    seg = SEG

    def rev(n):
        return pl.BlockSpec((TS, n), lambda i: (nt - 1 - i, 0))

    def body(do_ref, u_ref, y_ref, hr_ref, hi_ref, ar_ref, ai_ref, cb_ref, bb_ref, dsk_ref,
             wglu_ref, bglu_ref,
             du_ref, dct_ref, dbb_ref, dar_ref, dai_ref, dd_ref, dwglu_ref, dbglu_ref,
             gr, gi, hsr, hsi, er, ei, jr, ji, cr, ci, stage, perm):
        @pl.when(pl.program_id(0) == 0)
        def _():
            for ref in (cr, ci, dct_ref, dbb_ref, dar_ref, dai_ref, dd_ref, dwglu_ref, dbglu_ref):
                ref[...] = jnp.zeros_like(ref)

        uv = _interleave_rows(u_ref[...], stage, perm)
        yv = y_ref[...]
        dov = _interleave_rows(do_ref[...], stage, perm)
        g = _gelu(yv)
        gb = g.astype(BF16)
        sg = _sigmoid(_dot(gb, wglu_ref[...]) + bglu_ref[...])
        dpre = dov * g * sg * (1.0 - sg)
        dpb = dpre.astype(BF16)
        dwglu_ref[...] += _dot_tn(gb, dpb)
        dbglu_ref[...] += _colsum8(dpre)
        dy = (dov * sg + _dot_nt(dpb, wglu_ref[...])) * _gelu_grad(yv)
        dd_ref[...] += _colsum8(dy * uv)
        dyb = dy.astype(BF16)
        _to_slabs(hsr, hr_ref[...])
        _to_slabs(hsi, hi_ref[...])
        for j in range(N_USLAB):
            dct_ref[j] += _dot_tn(_state_cat(hsr, hsi, j), _lanes(dyb, j))
            _state_split(gr, gi, j, _dot(_lanes(dyb, j), cb_ref[j]))
        ar = ar_ref[...]
        ai = -ai_ref[...]

        def local(k, h):
            rows = _scan_rows(seg - 1 - k)
            nr, ni = _cmul(ar, ai, h[0], h[1])
            return nr + gr[:, rows, :], ni + gi[:, rows, :]

        zero = jnp.zeros((N_SLAB, SUBLANES, LANES), F32)
        e_r, e_i = lax.fori_loop(0, seg, local, (zero, zero))
        er[...] = e_r
        ei[...] = e_i
        pr, pi = _cpow(ar[:, 0:1, :], ai[:, 0:1, :], seg)
        c_r = cr[...]
        c_i = ci[...]
        for j in range(SUBLANES - 1, -1, -1):
            jr[:, j:j + 1, :] = c_r
            ji[:, j:j + 1, :] = c_i
            n_r, n_i = _cmul(pr, pi, c_r, c_i)
            c_r = n_r + er[:, j:j + 1, :]
            c_i = n_i + ei[:, j:j + 1, :]
        cr[...] = c_r
        ci[...] = c_i

        def full(k, carry):
            g_r, g_i, a_r, a_i = carry
            rows = _scan_rows(seg - 1 - k)
            h_r = hsr[:, rows, :]
            h_i = hsi[:, rows, :]
            a_r = a_r + g_r * h_r + g_i * h_i
            a_i = a_i + g_i * h_r - g_r * h_i
            nr, ni = _cmul(ar, ai, g_r, g_i)
            nr = nr + gr[:, rows, :]
            ni = ni + gi[:, rows, :]
            gr[:, rows, :] = nr
            gi[:, rows, :] = ni
            return nr, ni, a_r, a_i

        _, _, a_r, a_i = lax.fori_loop(0, seg, full, (jr[...], ji[...], zero, zero))
        dar_ref[...] += a_r
        dai_ref[...] += a_i
        ub = uv.astype(BF16)
        dus = []
        for j in range(N_USLAB):
            gb_j = _state_cat(gr, gi, j)
            dbb_ref[j] += _dot_tn(_lanes(ub, j), gb_j)
            dus.append(_dot(gb_j, bb_ref[j]))
        du = dy * dsk_ref[...] + jnp.concatenate(dus, axis=1)
        du_ref[...] = _deinterleave_rows(du, stage, perm)

    slab = (N_SLAB, SUBLANES, LANES)
    big = (N_SLAB, TS, LANES)
    uslab = (N_USLAB, TS, LANES)
    return pl.pallas_call(
        body, grid=(nt,),
        in_specs=[rev(D_SSM), rev(D_SSM), rev(D_SSM), rev(N_STATE), rev(N_STATE), _const(slab), _const(slab),
                  _const(S5_IN), _const(S5_OUT), _const((1, D_SSM)), _const((D_SSM, D_SSM)), _const((1, D_SSM))],
        out_specs=[rev(D_SSM), _acc(S5_OUT), _acc(S5_IN), _acc(slab), _acc(slab), _acc((SUBLANES, D_SSM)),
                   _acc((D_SSM, D_SSM)), _acc((SUBLANES, D_SSM))],
        out_shape=[jax.ShapeDtypeStruct((s, D_SSM), F32), jax.ShapeDtypeStruct(S5_OUT, F32),
                   jax.ShapeDtypeStruct(S5_IN, F32), jax.ShapeDtypeStruct(slab, F32),
                   jax.ShapeDtypeStruct(slab, F32), jax.ShapeDtypeStruct((SUBLANES, D_SSM), F32),
                   jax.ShapeDtypeStruct((D_SSM, D_SSM), F32), jax.ShapeDtypeStruct((SUBLANES, D_SSM), F32)],
        scratch_shapes=[pltpu.VMEM(big, F32), pltpu.VMEM(big, F32), pltpu.VMEM(big, F32), pltpu.VMEM(big, F32),
                        pltpu.VMEM(slab, F32), pltpu.VMEM(slab, F32), pltpu.VMEM(slab, F32), pltpu.VMEM(slab, F32),
                        pltpu.VMEM((N_SLAB, 1, LANES), F32), pltpu.VMEM((N_SLAB, 1, LANES), F32),
                        pltpu.VMEM(uslab, F32), pltpu.VMEM(uslab, F32)],
        name=f"s5_bwd_{tag}", compiler_params=_cp(dimension_semantics=("arbitrary",)),
    )(dout, u, y, h_re, h_im, p["a_re8"], p["a_im8"], p["cb3"], p["bb3"], p["d_skip"], p["w_glu"], p["b_glu"])


def _pool_bwd(dout, pooled, p, tag):
    s = dout.shape[0]
    nt = s // TS
    hb = TS // MAX_WINDOW

    def halo(n):
        return pl.BlockSpec((MAX_WINDOW, n), lambda i: (jnp.minimum((i + 1) * hb, nt * hb - 1), 0))

    def body(do_ref, po_ref, doh_ref, wl_ref, w_ref, wt_ref, sc_ref, dz_ref, dw_ref, dsc_ref, buf):
        i = pl.program_id(0)

        @pl.when(i == 0)
        def _():
            dw_ref[...] = jnp.zeros_like(dw_ref)
            dsc_ref[...] = jnp.zeros_like(dsc_ref)

        wl = wl_ref[...]
        sc = sc_ref[...]
        dov = do_ref[...]
        pooled_b = po_ref[...]
        dsc_ref[...] += _colsum8(dov * _dot(pooled_b, w_ref[...]))
        dmix = (dov * sc).astype(BF16)
        dw_ref[...] += _dot_tn(pooled_b, dmix)
        dpool = _dot(dmix, wt_ref[...])
        dpool_h = _dot((doh_ref[...] * sc).astype(BF16), wt_ref[...])
        buf[pl.ds(0, TS), :] = dpool / _pool_count(i, TS, wl)
        buf[pl.ds(TS, MAX_WINDOW), :] = jnp.where(i < nt - 1, dpool_h / _pool_count(i, MAX_WINDOW, wl, TS), 0.0)
        dz_ref[...] = _window_sum(buf, 0, TS, wl, 1) - dpool

    return pl.pallas_call(
        body, grid=(nt,),
        in_specs=[_row(TS, D_POOL), _row(TS, D_POOL), halo(D_POOL), _const((1, D_POOL)), _const((D_POOL, D_POOL)),
                  _const((D_POOL, D_POOL)), _const((1, D_POOL))],
        out_specs=[_row(TS, D_POOL), _acc((D_POOL, D_POOL)), _acc((SUBLANES, D_POOL))],
        out_shape=[jax.ShapeDtypeStruct((s, D_POOL), F32), jax.ShapeDtypeStruct((D_POOL, D_POOL), F32),
                   jax.ShapeDtypeStruct((SUBLANES, D_POOL), F32)],
        scratch_shapes=[pltpu.VMEM((TS + MAX_WINDOW, D_POOL), F32)],
        name=f"pool_bwd_{tag}", compiler_params=_cp(dimension_semantics=("arbitrary",)),
    )(dout, pooled, dout, _pool_consts(), p["w_pool_bd"], p["w_pool_bd_t"], p["pool_scale"])


def _sgu_bwd(dout, zuv, p, tag):
    s = zuv.shape[0]

    def body(do_ref, z_ref, lng_ref, lnb_ref, wp_ref, wpt_ref, bias_ref,
             dz_ref, dws_ref, dbias_ref, dlng_ref, dlnb_ref):
        @pl.when(pl.program_id(0) == 0)
        def _():
            for ref in (dws_ref, dbias_ref, dlng_ref, dlnb_ref):
                ref[...] = jnp.zeros_like(ref)

        lo, hi = _half_masks()
        lng = lng_ref[...]
        zu, zv, u, vn, rs, vl = _sgu_front(z_ref[...], lng, lnb_ref[...])
        mixed = _sgu_mix(vl, wp_ref, lo, hi) + jnp.tile(bias_ref[...], (TS // CHUNK, 1))
        dov = do_ref[...]
        dzu = dov * mixed * _gelu_grad(zu)
        dmix = dov * u
        dbias = dbias_ref[...]
        for c in range(TS // CHUNK):
            dmc = dmix[CHUNK * c:CHUNK * (c + 1), :]
            dbias = dbias + dmc
            vlc = vl[CHUNK * c:CHUNK * (c + 1), :].astype(BF16)
            for q in range(SGU_HEADS // 2):
                dq = dmc[:, LANES * q:LANES * (q + 1)]
                vq = vlc[:, LANES * q:LANES * (q + 1)]
                dws_ref[2 * q] += _dot_nt((dq * lo).astype(BF16), vq)
                dws_ref[2 * q + 1] += _dot_nt((dq * hi).astype(BF16), vq)
        dbias_ref[...] = dbias
        dvl = _sgu_mix(dmix, wpt_ref, lo, hi)
        dlng_ref[...] += _colsum8(dvl * vn)
        dlnb_ref[...] += _colsum8(dvl)
        dvn = dvl * lng
        dv = rs * (dvn - jnp.mean(dvn, axis=-1, keepdims=True) - vn * jnp.mean(dvn * vn, axis=-1, keepdims=True))
        dz_ref[...] = jnp.concatenate([dzu, dv * _gelu_grad(zv)], axis=1)

    return pl.pallas_call(
        body, grid=(s // TS,),
        in_specs=[_row(TS, D_SGU), _row(TS, 2 * D_SGU), _const((1, D_SGU)), _const((1, D_SGU)),
                  _const((SGU_HEADS // 2, CHUNK, 2 * CHUNK)), _const((SGU_HEADS // 2, CHUNK, 2 * CHUNK)),
                  _const((CHUNK, D_SGU))],
        out_specs=[_row(TS, 2 * D_SGU), _acc((SGU_HEADS, CHUNK, CHUNK)), _acc((CHUNK, D_SGU)),
                   _acc((SUBLANES, D_SGU)), _acc((SUBLANES, D_SGU))],
        out_shape=[jax.ShapeDtypeStruct((s, 2 * D_SGU), F32), jax.ShapeDtypeStruct((SGU_HEADS, CHUNK, CHUNK), F32),
                   jax.ShapeDtypeStruct((CHUNK, D_SGU), F32), jax.ShapeDtypeStruct((SUBLANES, D_SGU), F32),
                   jax.ShapeDtypeStruct((SUBLANES, D_SGU), F32)],
        name=f"sgu_bwd_{tag}", compiler_params=_cp(dimension_semantics=("arbitrary",)),
    )(dout, zuv, p["sgu_ln_g"], p["sgu_ln_b"], p["ws_pair"], p["ws_pair_t"], p["bias_sp"])


def _mix_in_bwd(dza, dzb, dzuv, x0, dx1, p, tag):
    s = x0.shape[0]

    def body(da_ref, db_ref, dc_ref, x_ref, dx1_ref, wt_ref, g_ref, dx0_ref, dz_ref, dg_ref):
        @pl.when(pl.program_id(0) == 0)
        def _():
            dg_ref[...] = jnp.zeros_like(dg_ref)

        dz = jnp.concatenate([da_ref[...], db_ref[...], dc_ref[...]], axis=1).astype(BF16)
        dz_ref[...] = dz
        n, r = _rms(x_ref[...])
        dxn, dgp = _rms_bwd(_dot(dz, wt_ref[...]), n, r, g_ref[...])
        dg_ref[...] += _colsum8(dgp)
        dx0_ref[...] = dx1_ref[...] + dxn

    return pl.pallas_call(
        body, grid=(s // TS,),
        in_specs=[_row(TS, D_SSM), _row(TS, D_POOL), _row(TS, 2 * D_SGU), _row(TS, D_MODEL), _row(TS, D_MODEL),
                  _const((D_IN, D_MODEL)), _const((1, D_MODEL))],
        out_specs=[_row(TS, D_MODEL), _row(TS, D_IN), _acc((SUBLANES, D_MODEL))],
        out_shape=[jax.ShapeDtypeStruct((s, D_MODEL), F32), jax.ShapeDtypeStruct((s, D_IN), BF16),
                   jax.ShapeDtypeStruct((SUBLANES, D_MODEL), F32)],
        name=f"mix_in_bwd_{tag}", compiler_params=_cp(dimension_semantics=("arbitrary",)),
    )(dza, dzb, dzuv, x0, dx1, p["w_in"], p["g_mix"])


def _head(x, target, g):
    s = x.shape[0]

    def body(x_ref, t_ref, g_ref, dx_ref, loss_ref, dg_ref):
        @pl.when(pl.program_id(0) == 0)
        def _():
            loss_ref[...] = jnp.zeros_like(loss_ref)
            dg_ref[...] = jnp.zeros_like(dg_ref)

        gv = g_ref[...]
        n, r = _rms(x_ref[...])
        diff = n * gv - t_ref[...]
        loss_ref[...] += jnp.sum(diff * diff) * (0.5 / D_MODEL)
        dxn, dgp = _rms_bwd(diff * (1.0 / D_MODEL), n, r, gv)
        dg_ref[...] += _colsum8(dgp)
        dx_ref[...] = dxn

    return pl.pallas_call(
        body, grid=(s // TS,),
        in_specs=[_row(TS, D_MODEL), _row(TS, D_MODEL), _const((1, D_MODEL))],
        out_specs=[_row(TS, D_MODEL), _acc((SUBLANES, LANES)), _acc((SUBLANES, D_MODEL))],
        out_shape=[jax.ShapeDtypeStruct((s, D_MODEL), F32), jax.ShapeDtypeStruct((SUBLANES, LANES), F32),
                   jax.ShapeDtypeStruct((SUBLANES, D_MODEL), F32)],
        name="head", compiler_params=_cp(dimension_semantics=("arbitrary",)),
    )(x, target, g)


def _atb(a, b, tn, tag):
    s, ka = a.shape
    kb = b.shape[1]
    ns = s // TS

    def body(a_ref, b_ref, o_ref):
        @pl.when(pl.program_id(1) == 0)
        def _():
            o_ref[...] = jnp.zeros_like(o_ref)

        o_ref[...] += _dot_tn(a_ref[...].astype(BF16), b_ref[...].astype(BF16))

    return pl.pallas_call(
        body, grid=(kb // tn, ns),
        in_specs=[pl.BlockSpec((TS, ka), lambda j, i: (i, 0)), pl.BlockSpec((TS, tn), lambda j, i: (i, j))],
        out_specs=pl.BlockSpec((ka, tn), lambda j, i: (0, j)),
        out_shape=jax.ShapeDtypeStruct((ka, kb), F32),
        name=f"atb_{tag}", compiler_params=_cp(dimension_semantics=("arbitrary", "arbitrary")),
    )(a, b)


def _s5_discretise(a_re, a_im, log_dt, b_re, b_im):
    dt = jnp.exp(log_dt)[:, None]
    mag = jnp.exp(a_re * dt)
    ar = mag * jnp.cos(a_im * dt)
    ai = mag * jnp.sin(a_im * dt)
    den = a_re * a_re + a_im * a_im
    f_re = ((ar - 1.0) * a_re + ai * a_im) / den
    f_im = (ai * a_re - (ar - 1.0) * a_im) / den
    bb_re = f_re[..., None] * b_re - f_im[..., None] * b_im
    bb_im = f_re[..., None] * b_im + f_im[..., None] * b_re
    return ar, ai, bb_re, bb_im


def _block_diag(blocks):
    g, r, c = blocks.shape
    eye = jnp.eye(g, dtype=blocks.dtype)
    return (blocks[:, :, None, :] * eye[:, None, :, None]).reshape(g * r, g * c)


def _block_diag_extract(m, g):
    r = m.shape[0] // g
    c = m.shape[1] // g
    eye = jnp.eye(g, dtype=m.dtype)
    return jnp.sum(m.reshape(g, r, g, c) * eye[:, None, :, None], axis=2)


GROUPS_PER_SLAB = N_GROUPS // N_USLAB


def _slab_diag(blocks):
    k = GROUPS_PER_SLAB
    _, r, c = blocks.shape
    eye = jnp.eye(k, dtype=blocks.dtype)
    spread = blocks.reshape(N_USLAB, k, r, 1, c) * eye[None, :, None, :, None]
    return spread.reshape(N_USLAB, k * r, k * c)


def _slab_diag_extract(m):
    k = GROUPS_PER_SLAB
    r, c = m.shape[1] // k, m.shape[2] // k
    eye = jnp.eye(k, dtype=m.dtype)
    return jnp.sum(m.reshape(N_USLAB, k, r, k, c) * eye[None, :, None, :, None], axis=3).reshape(N_GROUPS, r, c)


def _state_slabs(v):
    return jnp.broadcast_to(v.reshape(N_SLAB, 1, LANES), (N_SLAB, SUBLANES, LANES))


def _tril():
    return jnp.tril(jnp.ones((CHUNK, CHUNK), dtype=bool))


def _layer_params(w, l):
    row = lambda v: v.reshape(1, -1)
    t = lambda m: jnp.swapaxes(m, -1, -2)
    ar, ai, bb_re, bb_im = _s5_discretise(w["A_re"][l], w["A_im"][l], w["log_dt"][l], w["B_re"][l], w["B_im"][l])
    bbt3 = jnp.concatenate([_slab_diag(t(bb_re)), _slab_diag(t(bb_im))], axis=2).astype(BF16)
    ct3 = jnp.concatenate([_slab_diag(t(w["C_re"][l])), -_slab_diag(t(w["C_im"][l]))], axis=1).astype(BF16)
    ws = jnp.where(_tril()[None], w["w_spatial"][l], 0.0)
    pair = lambda m: jnp.stack([jnp.concatenate([m[2 * q], m[2 * q + 1]], axis=1)
                                for q in range(SGU_HEADS // 2)]).astype(BF16)
    wp = _block_diag(w["w_pool"][l]).astype(BF16)
    p = dict(
        g_mix=row(w["g_mix"][l]), g_ffn=row(w["g_ffn"][l]), d_skip=row(w["D_skip"][l]), b_glu=row(w["b_glu"][l]),
        pool_scale=row(w["pool_scale"][l]), sgu_ln_g=row(w["sgu_ln_g"][l]), sgu_ln_b=row(w["sgu_ln_b"][l]),
        a_re8=_state_slabs(ar), a_im8=_state_slabs(ai),
        bbt3=bbt3, bb3=t(bbt3), ct3=ct3, cb3=t(ct3),
        w_pool_bd=wp, w_pool_bd_t=t(wp), ws_pair=pair(ws), ws_pair_t=pair(t(ws)),
        bias_sp=jnp.repeat(t(w["b_spatial"][l]), SGU_HEAD_DIM, axis=1),
    )
    return p


MIX_WEIGHTS = ("w_in", "w_glu")
FFN_WEIGHTS = ("w_out", "w_gate", "w_up", "w_down")


def _with_big(p, mats):
    p.update(mats)


def _rows_sum(v):
    return jnp.sum(v, axis=0)


ATB_COLS = 512


def _after(v, token):
    return v if token is None else v + token


def _layer_bwd(dx2, sv, p, w, l, tag, on_grads, token):
    t = lambda m: jnp.swapaxes(m, -1, -2)
    dx1, da, db, dc, dgt, dup, act, dg_ffn = _blk_bwd(dx2, sv["x1"], sv["gt"], sv["up"],
                                                      dict(p, g_ffn=_after(p["g_ffn"], token)), tag)
    token = on_grads(l, "ffn", {
        "w_down": _atb(act, dx2, ATB_COLS, tag + "_wd"), "w_gate": _atb(dgt, sv["h2"], ATB_COLS, tag + "_wg"),
        "w_up": _atb(dup, sv["h2"], ATB_COLS, tag + "_wu"), "w_out": _atb(sv["ycat"], dx1, ATB_COLS, tag + "_wo")})
    g = {}
    g["g_ffn"] = _rows_sum(dg_ffn)
    dza, dct3, dbbt3, dar8, dai8, dd8, dwglu, dbglu8 = _s5_bwd(
        da, sv["za"], sv["y"], sv["h_re"], sv["h_im"], dict(p, d_skip=_after(p["d_skip"], token)), tag)
    dzb, dwp, dsc8 = _pool_bwd(db, sv["pooled"], p, tag)
    dzuv, dws, dbias, dlng8, dlnb8 = _sgu_bwd(dc, sv["zuv"], p, tag)
    dx0, dz, dg_mix = _mix_in_bwd(dza, dzb, dzuv, sv["x0"], dx1, p, tag)
    token = on_grads(l, "mix", {"w_in": _atb(dz, sv["h1"], D_MODEL, tag + "_wi"), "w_glu": dwglu})
    g["g_mix"] = _rows_sum(dg_mix)
    g["b_glu"] = _rows_sum(dbglu8)
    g["D_skip"] = _rows_sum(dd8)
    half = N_STATE // N_USLAB
    g["C_re"] = t(_slab_diag_extract(dct3[:, :half, :]))
    g["C_im"] = -t(_slab_diag_extract(dct3[:, half:, :]))
    dar = jnp.sum(dar8, axis=1).reshape(N_GROUPS, SSM_STATE)
    dai = jnp.sum(dai8, axis=1).reshape(N_GROUPS, SSM_STATE)
    dbb_re = t(_slab_diag_extract(dbbt3[:, :, :half]))
    dbb_im = t(_slab_diag_extract(dbbt3[:, :, half:]))
    _, disc_vjp = jax.vjp(_s5_discretise, w["A_re"][l], w["A_im"][l], w["log_dt"][l], w["B_re"][l], w["B_im"][l])
    g["A_re"], g["A_im"], g["log_dt"], g["B_re"], g["B_im"] = disc_vjp((dar, dai, dbb_re, dbb_im))
    g["w_pool"] = _block_diag_extract(dwp, len(POOL_WINDOWS))
    g["pool_scale"] = _rows_sum(dsc8)
    g["sgu_ln_g"] = _rows_sum(dlng8)
    g["sgu_ln_b"] = _rows_sum(dlnb8)
    g["w_spatial"] = jnp.where(_tril()[None], dws, 0.0)
    g["b_spatial"] = t(jnp.sum(dbias.reshape(CHUNK, SGU_HEADS, SGU_HEAD_DIM), axis=-1))
    return dx0, g, token


def _local_step(x, target, w, get_big, on_grads):
    params = [_layer_params(w, l) for l in range(DEPTH)]
    saved = []
    h = x
    for l in range(DEPTH):
        p, tag = params[l], f"l{l}"
        _with_big(p, get_big(l, "mix", [h]))
        za, zb, zuv, h1 = _mix_in_fwd(h, p["g_mix"], p["w_in"], tag)
        oa, y, h_re, h_im = _s5_fwd(za, p, tag)
        ob, pooled = _pool_fwd(zb, p, tag)
        oc = _sgu_fwd(zuv, p, tag)
        _with_big(p, get_big(l, "ffn", [oa, ob, oc]))
        x1, x2, h2, gt, up, ycat = _blk_fwd(h, oa, ob, oc, p, tag)
        saved.append(dict(x0=h, za=za, zuv=zuv, h1=h1, ycat=ycat, y=y, h_re=h_re, h_im=h_im, pooled=pooled, x1=x1,
                          h2=h2, gt=gt, up=up))
        h = x2
    dx, loss8, dgf8 = _head(h, target, w["g_final"].reshape(1, -1))
    grads = [None] * DEPTH
    token = None
    for l in reversed(range(DEPTH)):
        dx, grads[l], token = _layer_bwd(dx, saved[l], params[l], w, l, f"l{l}", on_grads, token)
    g = {n: jnp.stack([grads[l][n] for l in range(DEPTH)]) for n in SMALL if n != "g_final"}
    g["g_final"] = _rows_sum(dgf8)
    return loss8[0, 0], dx, g


_ANY = pl.BlockSpec(memory_space=pl.ANY)
_MESH = pl.DeviceIdType.MESH


def _place():
    return lax.axis_index("x"), lax.axis_index("y"), lax.axis_index("c")


def _other_chips(x, y):
    return [(1 - x, y), (x, 1 - y), (1 - x, 1 - y)]


def _dma_sems(n):
    return pltpu.SemaphoreType.DMA((n,))


def _remote(src, dst, send_sems, recv_sems, k, to):
    return pltpu.make_async_remote_copy(src_ref=src, dst_ref=dst, send_sem=send_sems.at[k], recv_sem=recv_sems.at[k],
                                        device_id=to, device_id_type=_MESH)


_HBM = pl.BlockSpec(memory_space=pltpu.HBM)
_SEM = pl.BlockSpec(memory_space=pltpu.SEMAPHORE)
_EFFECT = pltpu.SideEffectType.DATAFLOW_SIDE_EFFECTING
N_REL = N_CHIPS - 1


def _gather_plan(x, y, c, srcs, lands):
    me = 2 * x + y
    return [(s, l.at[pl.ds(me * s.shape[0], s.shape[0])], (cx, cy, c))
            for s, l in zip(srcs, lands) for cx, cy in _other_chips(x, y)]


def _slab_plan(x, y, c, srcs, lands):
    return [(s.at[2 * cx + cy], l.at[j], (cx, cy, c))
            for s, l in zip(srcs, lands) for j, (cx, cy) in enumerate(_other_chips(x, y))]


def _plan_copies(plan, srcs, lands, send_sems, recv_sems):
    x, y, c = _place()
    return [_remote(s, d, send_sems, recv_sems, k, to) for k, (s, d, to) in enumerate(plan(x, y, c, srcs, lands))]


def _hbm(a):
    return pltpu.with_memory_space_constraint(a, pltpu.HBM)


def _everyone_plan(x, y, c, srcs, lands):
    me = 4 * x + 2 * y + c
    peers = [(x, y, 1 - c)] + [(cx, cy, cc) for cx, cy in _other_chips(x, y) for cc in (c, 1 - c)]
    return [(s, l.at[me], peer) for s, l in zip(srcs, lands) for peer in peers]


def _copies_start(name, plan, srcs, lands, fanout=N_REL):
    n = len(srcs)
    ncopies = fanout * n

    def body(*refs):
        for cp in _plan_copies(plan, refs[:n], refs[n:2 * n], refs[2 * n], refs[2 * n + 1]):
            cp.start()
        refs[-1][...] = jnp.zeros_like(refs[-1])

    ref_out = [pltpu.HBM(a.shape, a.dtype) for a in (*srcs, *lands)]
    out = pl.pallas_call(
        body, name=name, in_specs=[_HBM] * (2 * n),
        out_shape=(_dma_sems(ncopies), _dma_sems(ncopies), *ref_out, jax.ShapeDtypeStruct((SUBLANES, LANES), F32)),
        out_specs=(_SEM, _SEM, *[_HBM] * (2 * n), pl.BlockSpec(memory_space=pltpu.VMEM)),
        input_output_aliases={i: 2 + i for i in range(2 * n)},
        compiler_params=pltpu.CompilerParams(has_side_effects=_EFFECT),
    )(*[_hbm(a) for a in (*srcs, *lands)])
    return dict(name=name, plan=plan, sems=out[:2], srcs=out[2:2 + n], lands=out[2 + n:2 + 2 * n], token=out[-1][0, 0],
                token_tile=out[-1])


def _copies_wait(started, after):
    n = len(started["srcs"])
    plan = started["plan"]

    def body(*refs):
        for cp in _plan_copies(plan, refs[:n], refs[n:2 * n], refs[2 * n], refs[2 * n + 1]):
            cp.wait_send()
            cp.wait_recv()

    args = (*started["srcs"], *started["lands"])
    out = pl.pallas_call(
        body, name=started["name"] + "_wait", out_shape=[pltpu.HBM(a.shape, a.dtype) for a in args],
        in_specs=[_HBM] * (2 * n) + [_SEM, _SEM] + [_ANY] * len(after), out_specs=[_HBM] * (2 * n),
        input_output_aliases={i: i for i in range(2 * n)},
        compiler_params=pltpu.CompilerParams(has_side_effects=_EFFECT),
    )(*args, *started["sems"], *after)
    return out[n:]


def _swap_halves(g4s, tag):
    nw = len(g4s)

    def body(*refs):
        ins, outs = refs[:nw], refs[nw:2 * nw]
        send_sems, recv_sems = refs[2 * nw:]
        x, y, c = _place()
        copies = [_remote(ins[i].at[:, 1 - c], outs[i], send_sems, recv_sems, i, (x, y, 1 - c)) for i in range(nw)]
        for cp in copies:
            cp.start()
        for cp in copies:
            cp.wait()

    return pl.pallas_call(
        body, out_shape=[jax.ShapeDtypeStruct((g.shape[0],) + g.shape[2:], g.dtype) for g in g4s],
        in_specs=[_ANY] * nw, out_specs=[_ANY] * nw, scratch_shapes=[_dma_sems(nw), _dma_sems(nw)],
        name=f"swap_halves_{tag}",
    )(*g4s)


def _share_halves(fs, layer, tag):
    nw = len(fs)

    def body(*refs):
        ins = refs[:nw]
        send_sems, recv_sems = refs[2 * nw:]
        x, y, c = _place()

        def half(i, who):
            h = ins[i].shape[1] // 2
            return ins[i].at[layer, pl.ds(who * h, h)]

        sends = [_remote(half(i, c), half(i, c), send_sems, recv_sems, i, (x, y, 1 - c)) for i in range(nw)]
        for cp in sends:
            cp.start()
        for i in range(nw):
            sends[i].wait_send()
            _remote(half(i, c), half(i, 1 - c), send_sems, recv_sems, i, (x, y, 1 - c)).wait_recv()

    return pl.pallas_call(
        body, out_shape=[jax.ShapeDtypeStruct(f.shape, f.dtype) for f in fs], in_specs=[_ANY] * nw,
        out_specs=[_ANY] * nw, input_output_aliases={i: i for i in range(nw)},
        scratch_shapes=[_dma_sems(nw), _dma_sems(nw)], name=f"share_halves_{tag}",
    )(*fs)


def _add_halves(g4s, recvs, sel, tag):
    nw = len(g4s)

    def body(sel_ref, *refs):
        for i in range(nw):
            refs[2 * nw + i][...] = (refs[i][...] + refs[nw + i][...]).astype(BF16)

    mine = [pl.BlockSpec((None, None) + g.shape[2:], lambda k, s: (k, s[0], 0, 0)) for g in g4s]
    slab = [pl.BlockSpec((None,) + g.shape[2:], lambda k, s: (k, 0, 0)) for g in g4s]
    return pl.pallas_call(
        body, grid_spec=pltpu.PrefetchScalarGridSpec(num_scalar_prefetch=1, grid=(N_CHIPS,), in_specs=mine + slab,
                                                     out_specs=slab),
        out_shape=[jax.ShapeDtypeStruct(r.shape, BF16) for r in recvs], name=f"add_halves_{tag}",
        compiler_params=_cp(dimension_semantics=("arbitrary",)),
    )(sel, *g4s, *recvs)


def _add_chips(ps, slabs, fs, layer, sel, tag):
    nw = len(ps)
    old = [f for f in fs if f is not None]

    def body(sel_ref, *refs):
        outs = refs[2 * nw + len(old):]
        for i in range(nw):
            acc = refs[i][...].astype(F32)
            for j in range(N_REL):
                acc = acc + refs[nw + i][j].astype(F32)
            outs[i][...] = acc

    shapes = [(DEPTH, 2 * p.shape[1], p.shape[2]) for p in ps]
    in_specs = [pl.BlockSpec((None,) + p.shape[1:], lambda i, s: (s[1], 0, 0)) for p in ps]
    in_specs += [pl.BlockSpec(sl.shape, lambda i, s: (0, 0, 0)) for sl in slabs]
    in_specs += [_ANY] * len(old)
    first_old = 1 + 2 * nw
    aliases, k = {}, 0
    for i, f in enumerate(fs):
        if f is not None:
            aliases[first_old + k] = i
            k += 1
    return pl.pallas_call(
        body, grid_spec=pltpu.PrefetchScalarGridSpec(
            num_scalar_prefetch=1, grid=(1,), in_specs=in_specs,
            out_specs=[pl.BlockSpec((None,) + p.shape[1:], lambda i, s: (layer, s[0], 0)) for p in ps]),
        out_shape=[jax.ShapeDtypeStruct(sh, F32) for sh in shapes], input_output_aliases=aliases,
        name=f"add_chips_{tag}", compiler_params=_cp(dimension_semantics=("arbitrary",)),
    )(sel, *ps, *slabs, *old)


def _adamw_math(w, g, m, v):
    m = ADAM_B1 * m + (1.0 - ADAM_B1) * g
    v = ADAM_B2 * v + (1.0 - ADAM_B2) * (g * g)
    m_hat = m / (1.0 - ADAM_B1 ** ADAM_STEP)
    v_hat = v / (1.0 - ADAM_B2 ** ADAM_STEP)
    delta = -ADAM_LR * (m_hat / (jnp.sqrt(v_hat) + ADAM_EPS) + ADAM_WD * w)
    return delta, m, v


ADAM_ROWS = 512


def _row_tile(rows, most):
    return max(t for t in range(SUBLANES, most + 1, SUBLANES) if rows % t == 0)


def _adamw(w, g, m, v, tag):
    depth, rows, cols = w.shape
    tr = _row_tile(rows, ADAM_ROWS)

    def body(w_ref, g_ref, m_ref, v_ref, d_ref, nm_ref, nv_ref):
        d, nm, nv = _adamw_math(w_ref[...], g_ref[...], m_ref[...], v_ref[...])
        d_ref[...] = d
        nm_ref[...] = nm
        nv_ref[...] = nv

    spec = pl.BlockSpec((None, tr, cols), lambda l, i: (l, i, 0))
    return pl.pallas_call(
        body, grid=(depth, rows // tr), in_specs=[spec] * 4, out_specs=[spec] * 3,
        out_shape=[jax.ShapeDtypeStruct(w.shape, F32)] * 3, name=f"adamw_{tag}",
        compiler_params=_cp(dimension_semantics=("arbitrary", "arbitrary")),
    )(w, g, m, v)


SMALL_TILE = 384
PRECISE = ("g_final",)
COARSE = [n for n in SMALL if n not in PRECISE]


def _small_reduce_adamw(gathered, w, m, v, tag):
    rows = w.shape[0]
    tr = math.gcd(rows, SMALL_TILE)

    def body(ga_ref, w_ref, m_ref, v_ref, g_ref, d_ref, nm_ref, nv_ref):
        g = ga_ref[0].astype(F32)
        for k in range(1, N_DEV):
            g = g + ga_ref[k].astype(F32)
        g_ref[...] = g
        d, nm, nv = _adamw_math(w_ref[...], g, m_ref[...], v_ref[...])
        d_ref[...] = d
        nm_ref[...] = nm
        nv_ref[...] = nv

    spec = _row(tr, LANES)
    return pl.pallas_call(
        body, grid=(rows // tr,),
        in_specs=[pl.BlockSpec((N_DEV, tr, LANES), lambda i: (0, i, 0)), spec, spec, spec], out_specs=[spec] * 4,
        out_shape=[jax.ShapeDtypeStruct((rows, LANES), F32)] * 4, name=f"small_reduce_adamw_{tag}",
        compiler_params=_cp(dimension_semantics=("arbitrary",)),
    )(gathered, w, m, v)


def _exchange_form(n, a):
    return jnp.swapaxes(a, 1, 2) if n in TRANSPOSED else a


PACK_ROWS = 16


def _pack(vals, names, extra=None):
    parts = [vals[n].reshape(-1) for n in names] + ([] if extra is None else [extra.reshape(1)])
    flat = jnp.concatenate(parts)
    rows = -(-flat.size // (LANES * PACK_ROWS)) * PACK_ROWS
    return jnp.pad(flat, (0, rows * LANES - flat.size)).reshape(rows, LANES)


def _unpack(buf, like, names):
    flat = buf.reshape(-1)
    out, off = {}, 0
    for n in names:
        out[n] = flat[off:off + like[n].size].reshape(like[n].shape)
        off += like[n].size
    return out, flat[off:]


def kernel(x, g_mix, w_in, A_re, A_im, log_dt, B_re, B_im, C_re, C_im, D_skip, w_glu, b_glu, w_pool, pool_scale, sgu_ln_g, sgu_ln_b, w_spatial, b_spatial, w_out, g_ffn, w_gate, w_up, w_down, g_final, loss_target, m_g_mix, m_w_in, m_A_re, m_A_im, m_log_dt, m_B_re, m_B_im, m_C_re, m_C_im, m_D_skip, m_w_glu, m_b_glu, m_w_pool, m_pool_scale, m_sgu_ln_g, m_sgu_ln_b, m_w_spatial, m_b_spatial, m_w_out, m_g_ffn, m_w_gate, m_w_up, m_w_down, m_g_final, v_g_mix, v_w_in, v_A_re, v_A_im, v_log_dt, v_B_re, v_B_im, v_C_re, v_C_im, v_D_skip, v_w_glu, v_b_glu, v_w_pool, v_pool_scale, v_sgu_ln_g, v_sgu_ln_b, v_w_spatial, v_b_spatial, v_w_out, v_g_ffn, v_w_gate, v_w_up, v_w_down, v_g_final):
    loc = locals()
    w = {n: loc[n] for n in WEIGHTS}
    m = {n: loc["m_" + n] for n in WEIGHTS}
    v = {n: loc["v_" + n] for n in WEIGHTS}
    sel = jnp.stack([lax.axis_index("c"), 2 * lax.axis_index("x") + lax.axis_index("y")]).astype(jnp.int32)

    chip = sel[1]

    halves = [(l, half) for l in range(DEPTH) for half in ("mix", "ffn")]
    names = {"mix": MIX_WEIGHTS, "ffn": FFN_WEIGHTS}
    started = {}
    token = None
    for l, half in halves:
        shards = [_after(_exchange_form(n, w[n])[l], token).astype(BF16) for n in names[half]]
        lands = [lax.dynamic_update_slice(lax.empty((N_CHIPS * s.shape[0], s.shape[1]), BF16), s,
                                          (chip * s.shape[0], 0)) for s in shards]
        started[l, half] = _copies_start(f"weights_l{l}_{half}", _gather_plan, shards, lands)
        token = started[l, half]["token"]
    w = dict(w, g_mix=_after(w["g_mix"], token))

    def get_big(l, half, after):
        return dict(zip(names[half], _copies_wait(started[l, half], after)))

    result = {n: None for n in BIG}
    in_flight = []

    def finish(after):
        for part, ex, ns, l, tag in in_flight:
            bufs = _add_chips(part, _copies_wait(ex, after), [result[n] for n in ns], l, sel, tag)
            for n, f in zip(ns, _share_halves(bufs, l, tag)):
                result[n] = f
        in_flight.clear()

    def on_grads(l, half, grads):
        ns = list(grads)
        tag = f"l{l}_{half}"
        finish([grads[ns[0]]])
        g4s = [grads[n].reshape(N_CHIPS, 2, grads[n].shape[0] // (2 * N_CHIPS), grads[n].shape[1]) for n in ns]
        part = _add_halves(g4s, _swap_halves(g4s, tag), sel, tag)
        slabs = [lax.empty((N_REL,) + p.shape[1:], BF16) for p in part]
        ex = _copies_start(f"grads_{tag}", _slab_plan, part, slabs)
        in_flight.append((part, ex, ns, l, tag))
        return ex["token"]

    loss_local, dx, g = _local_step(x[0], loss_target[0], w, get_big, on_grads)

    me = 2 * chip + sel[0]
    blocks = [_pack(g, COARSE).astype(BF16), _pack(g, PRECISE, loss_local)]
    lands = [lax.dynamic_update_slice(lax.empty((N_DEV,) + b.shape, b.dtype), b[None], (me, 0, 0)) for b in blocks]
    small = _copies_start("small_grads", _everyone_plan, blocks, lands, fanout=N_DEV - 1)

    finish([small["token_tile"]])
    grads, deltas, new_m, new_v = {}, {}, {}, {}
    for n in BIG:
        grads[n] = _exchange_form(n, result[n])
        deltas[n], new_m[n], new_v[n] = _adamw(w[n], grads[n], m[n], v[n], n)

    gathered = _copies_wait(small, [new_v[n] for n in BIG])
    zero = jnp.zeros((), F32)
    loss = None
    for names_k, extra, block, tag in ((COARSE, None, gathered[0], "coarse"), (PRECISE, zero, gathered[1], "precise")):
        outs = _small_reduce_adamw(block, _pack(w, names_k, extra), _pack(m, names_k, extra), _pack(v, names_k, extra),
                                   tag)
        for store, buf in zip((grads, deltas, new_m, new_v), outs):
            vals, rest = _unpack(buf, w, names_k)
            store.update(vals)
            if store is grads and extra is not None:
                loss = rest[0]
    return (loss, dx[None], *[grads[n] for n in WEIGHTS], *[deltas[n] for n in WEIGHTS],
            *[new_m[n] for n in WEIGHTS], *[new_v[n] for n in WEIGHTS])
```

```python
import math

import jax
import jax.numpy as jnp
from jax import lax
from jax.experimental import pallas as pl
from jax.experimental.pallas import tpu as pltpu

F32 = jnp.float32
BF16 = jnp.bfloat16

D_MODEL = 1024
DEPTH = 2
D_SSM = 384
SSM_GROUP = 16
N_GROUPS = 24
SSM_STATE = 64
N_STATE = N_GROUPS * SSM_STATE
POOL_WINDOWS = (2, 4, 8, 16)
POOL_GROUP = 64
D_POOL = 256
MAX_WINDOW = 16
SGU_HEADS = 6
SGU_HEAD_DIM = 64
D_SGU = 384
CHUNK = 128
D_IN = D_SSM + D_POOL + 2 * D_SGU
D_FF = 2816
EPS = 1e-6

ADAM_LR = 0.001
ADAM_B1 = 0.9
ADAM_B2 = 0.999
ADAM_EPS = 1e-08
ADAM_WD = 0.01
ADAM_STEP = 10

LANES = 128
SUBLANES = 8
N_SLAB = N_STATE // LANES
VMEM_LIMIT = 56 * 1024 * 1024

TS = 512
TS_FFN = 256

WEIGHTS = ['g_mix', 'w_in', 'A_re', 'A_im', 'log_dt', 'B_re', 'B_im', 'C_re', 'C_im', 'D_skip', 'w_glu', 'b_glu',
           'w_pool', 'pool_scale', 'sgu_ln_g', 'sgu_ln_b', 'w_spatial', 'b_spatial', 'w_out', 'g_ffn', 'w_gate',
           'w_up', 'w_down', 'g_final']
BIG = ['w_in', 'w_glu', 'w_out', 'w_gate', 'w_up', 'w_down']
SMALL = [n for n in WEIGHTS if n not in BIG]
TRANSPOSED = ("w_in", "w_gate", "w_up")
N_CHIPS = 4
N_DEV = 8


def _cp(**kw):
    return pltpu.CompilerParams(vmem_limit_bytes=VMEM_LIMIT, **kw)


def _row(ts, n):
    return pl.BlockSpec((ts, n), lambda i: (i, 0))


def _const(shape):
    nd = len(shape)
    return pl.BlockSpec(shape, lambda i: (0,) * nd, pipeline_mode=pl.Buffered(1))


def _acc(shape):
    nd = len(shape)
    return pl.BlockSpec(shape, lambda i: (0,) * nd)


def _dot(a, b):
    return jnp.dot(a, b, preferred_element_type=F32)


def _dot_tn(a, b):
    return lax.dot_general(a, b, (((0,), (0,)), ((), ())), preferred_element_type=F32)


def _dot_nt(a, b):
    return lax.dot_general(a, b, (((1,), (1,)), ((), ())), preferred_element_type=F32)


_G0 = math.sqrt(2.0 / math.pi)
_G1 = 0.044715


def _gelu(x):
    return 0.5 * x * (1.0 + jnp.tanh(_G0 * (x + _G1 * x * x * x)))


def _gelu_grad(x):
    t = jnp.tanh(_G0 * (x + _G1 * x * x * x))
    return 0.5 * (1.0 + t) + 0.5 * x * (1.0 - t * t) * (_G0 * (1.0 + 3.0 * _G1 * x * x))


def _sigmoid(x):
    return 1.0 / (1.0 + jnp.exp(-x))


def _rms(x):
    r = lax.rsqrt(jnp.mean(x * x, axis=-1, keepdims=True) + EPS)
    return x * r, r


def _rms_bwd(dh, n, r, g):
    dn = dh * g
    return r * (dn - n * jnp.mean(dn * n, axis=-1, keepdims=True)), dh * n


def _colsum8(v):
    rows, n = v.shape
    return jnp.sum(v.reshape(rows // SUBLANES, SUBLANES, n), axis=0)


def _mix_in_fwd(x, g, w, tag):
    s = x.shape[0]

    def body(x_ref, g_ref, w_ref, za_ref, zb_ref, zuv_ref, h_ref):
        n, _ = _rms(x_ref[...])
        h = (n * g_ref[...]).astype(BF16)
        z = _dot_nt(h, w_ref[...])
        za_ref[...] = z[:, :D_SSM]
        zb_ref[...] = z[:, D_SSM:D_SSM + D_POOL]
        zuv_ref[...] = z[:, D_SSM + D_POOL:]
        h_ref[...] = h

    return pl.pallas_call(
        body, grid=(s // TS,),
        in_specs=[_row(TS, D_MODEL), _const((1, D_MODEL)), _const((D_IN, D_MODEL))],
        out_specs=[_row(TS, D_SSM), _row(TS, D_POOL), _row(TS, 2 * D_SGU), _row(TS, D_MODEL)],
        out_shape=[jax.ShapeDtypeStruct((s, D_SSM), F32), jax.ShapeDtypeStruct((s, D_POOL), F32),
                   jax.ShapeDtypeStruct((s, 2 * D_SGU), F32), jax.ShapeDtypeStruct((s, D_MODEL), BF16)],
        name=f"mix_in_fwd_{tag}", compiler_params=_cp(dimension_semantics=("arbitrary",)),
    )(x, g, w)


def _cmul(ar, ai, br, bi):
    return ar * br - ai * bi, ar * bi + ai * br


def _cpow(ar, ai, n):
    assert n & (n - 1) == 0
    while n > 1:
        ar, ai = _cmul(ar, ai, ar, ai)
        n //= 2
    return ar, ai


def _to_slabs(ref, v):
    for j in range(N_SLAB):
        ref[j] = v[:, LANES * j:LANES * (j + 1)]


def _from_slabs(ref):
    return jnp.concatenate([ref[j] for j in range(N_SLAB)], axis=1)


N_USLAB = D_SSM // LANES
SEG = TS // SUBLANES


def _interleave_rows(v, stage, dst):
    for j in range(N_USLAB):
        stage[j] = v[:, LANES * j:LANES * (j + 1)]

    def step(k, carry):
        dst[:, pl.ds(pl.multiple_of(k * SUBLANES, SUBLANES), SUBLANES), :] = stage[:, pl.ds(k, SUBLANES, stride=SEG), :]
        return carry

    lax.fori_loop(0, SEG, step, 0)
    return jnp.concatenate([dst[j] for j in range(N_USLAB)], axis=1)


def _deinterleave_rows(v, stage, dst):
    for j in range(N_USLAB):
        stage[j] = v[:, LANES * j:LANES * (j + 1)]

    def step(k, carry):
        dst[:, pl.ds(k, SUBLANES, stride=SEG), :] = stage[:, pl.ds(pl.multiple_of(k * SUBLANES, SUBLANES), SUBLANES), :]
        return carry

    lax.fori_loop(0, SEG, step, 0)
    return jnp.concatenate([dst[j] for j in range(N_USLAB)], axis=1)


def _scan_rows(k):
    return pl.ds(pl.multiple_of(k * SUBLANES, SUBLANES), SUBLANES)


SLABS_PER_USLAB = N_SLAB // N_USLAB
S5_IN = (N_USLAB, LANES, 2 * N_STATE // N_USLAB)
S5_OUT = (N_USLAB, 2 * N_STATE // N_USLAB, LANES)


def _lanes(v, j):
    return v[:, LANES * j:LANES * (j + 1)]


def _state_split(re_ref, im_ref, j, v):
    for q in range(SLABS_PER_USLAB):
        re_ref[SLABS_PER_USLAB * j + q] = _lanes(v, q)
        im_ref[SLABS_PER_USLAB * j + q] = _lanes(v, SLABS_PER_USLAB + q)


def _state_cat(re_ref, im_ref, j):
    idx = range(SLABS_PER_USLAB * j, SLABS_PER_USLAB * (j + 1))
    return jnp.concatenate([re_ref[q] for q in idx] + [im_ref[q] for q in idx], axis=1).astype(BF16)


def _s5_fwd(u, p, tag):
    s = u.shape[0]
    seg = SEG

    def body(u_ref, bbt_ref, ar_ref, ai_ref, ct_ref, dsk_ref, wglu_ref, bglu_ref,
             oa_ref, y_ref, hr_ref, hi_ref, sr, si, er, ei, ir, ii, cr, ci, stage, perm):
        @pl.when(pl.program_id(0) == 0)
        def _():
            cr[...] = jnp.zeros_like(cr)
            ci[...] = jnp.zeros_like(ci)

        uv = _interleave_rows(u_ref[...], stage, perm)
        ub = uv.astype(BF16)
        for j in range(N_USLAB):
            _state_split(sr, si, j, _dot(_lanes(ub, j), bbt_ref[j]))
        ar = ar_ref[...]
        ai = ai_ref[...]

        def local(k, h):
            rows = _scan_rows(k)
            hr, hi = _cmul(ar, ai, h[0], h[1])
            return hr + sr[:, rows, :], hi + si[:, rows, :]

        zero = jnp.zeros((N_SLAB, SUBLANES, LANES), F32)
        e_r, e_i = lax.fori_loop(0, seg, local, (zero, zero))
        er[...] = e_r
        ei[...] = e_i
        pr, pi = _cpow(ar[:, 0:1, :], ai[:, 0:1, :], seg)
        c_r = cr[...]
        c_i = ci[...]
        for j in range(SUBLANES):
            ir[:, j:j + 1, :] = c_r
            ii[:, j:j + 1, :] = c_i
            n_r, n_i = _cmul(pr, pi, c_r, c_i)
            c_r = n_r + er[:, j:j + 1, :]
            c_i = n_i + ei[:, j:j + 1, :]
        cr[...] = c_r
        ci[...] = c_i

        def full(k, h):
            rows = _scan_rows(k)
            hr, hi = _cmul(ar, ai, h[0], h[1])
            hr = hr + sr[:, rows, :]
            hi = hi + si[:, rows, :]
            sr[:, rows, :] = hr
            si[:, rows, :] = hi
            return hr, hi

        lax.fori_loop(0, seg, full, (ir[...], ii[...]))
        hr_ref[...] = _from_slabs(sr).astype(BF16)
        hi_ref[...] = _from_slabs(si).astype(BF16)
        y = jnp.concatenate([_dot(_state_cat(sr, si, j), ct_ref[j]) for j in range(N_USLAB)], axis=1)
        y = y + dsk_ref[...] * uv
        y_ref[...] = y
        g = _gelu(y)
        pre = _dot(g.astype(BF16), wglu_ref[...]) + bglu_ref[...]
        oa_ref[...] = _deinterleave_rows(g * _sigmoid(pre), stage, perm).astype(BF16)

    slab = (N_SLAB, SUBLANES, LANES)
    uslab = (N_USLAB, TS, LANES)
    return pl.pallas_call(
        body, grid=(s // TS,),
        in_specs=[_row(TS, D_SSM), _const(S5_IN), _const(slab), _const(slab), _const(S5_OUT), _const((1, D_SSM)),
                  _const((D_SSM, D_SSM)), _const((1, D_SSM))],
        out_specs=[_row(TS, D_SSM), _row(TS, D_SSM), _row(TS, N_STATE), _row(TS, N_STATE)],
        out_shape=[jax.ShapeDtypeStruct((s, D_SSM), BF16), jax.ShapeDtypeStruct((s, D_SSM), F32),
                   jax.ShapeDtypeStruct((s, N_STATE), BF16), jax.ShapeDtypeStruct((s, N_STATE), BF16)],
        scratch_shapes=[pltpu.VMEM((N_SLAB, TS, LANES), F32), pltpu.VMEM((N_SLAB, TS, LANES), F32),
                        pltpu.VMEM(slab, F32), pltpu.VMEM(slab, F32), pltpu.VMEM(slab, F32), pltpu.VMEM(slab, F32),
                        pltpu.VMEM((N_SLAB, 1, LANES), F32), pltpu.VMEM((N_SLAB, 1, LANES), F32),
                        pltpu.VMEM(uslab, F32), pltpu.VMEM(uslab, F32)],
        name=f"s5_fwd_{tag}", compiler_params=_cp(dimension_semantics=("arbitrary",)),
    )(u, p["bbt3"], p["a_re8"], p["a_im8"], p["ct3"], p["d_skip"], p["w_glu"], p["b_glu"])


def _pool_consts():
    w = jnp.repeat(jnp.asarray(POOL_WINDOWS, F32), POOL_GROUP)[None, :]
    return w


def _window_sum(buf, first, rows, wl, step):
    acc = buf[pl.ds(first, rows), :]
    for j in range(1, MAX_WINDOW):
        term = buf[pl.ds(first + step * j, rows), :]
        acc = acc + (term if j < min(POOL_WINDOWS) else term * (wl > j).astype(F32))
    return acc


def _pool_count(i, rows, wl, offset=0):
    t = (i * TS + offset + 1).astype(F32) + lax.broadcasted_iota(jnp.int32, (rows, 1), 0).astype(F32)
    return jnp.minimum(t, wl)


def _pool_fwd(zb, p, tag):
    s = zb.shape[0]
    hb = TS // MAX_WINDOW

    def body(u_ref, halo_ref, wl_ref, w_ref, sc_ref, ob_ref, pooled_ref, buf):
        i = pl.program_id(0)
        uv = u_ref[...]
        buf[pl.ds(0, MAX_WINDOW), :] = jnp.where(i > 0, halo_ref[...], 0.0)
        buf[pl.ds(MAX_WINDOW, TS), :] = uv
        wl = wl_ref[...]
        pooled = (_window_sum(buf, MAX_WINDOW, TS, wl, -1) / _pool_count(i, TS, wl) - uv).astype(BF16)
        pooled_ref[...] = pooled
        ob_ref[...] = (_dot(pooled, w_ref[...]) * sc_ref[...]).astype(BF16)

    return pl.pallas_call(
        body, grid=(s // TS,),
        in_specs=[_row(TS, D_POOL),
                  pl.BlockSpec((MAX_WINDOW, D_POOL), lambda i: (jnp.maximum(i * hb - 1, 0), 0)),
                  _const((1, D_POOL)), _const((D_POOL, D_POOL)), _const((1, D_POOL))],
        out_specs=[_row(TS, D_POOL), _row(TS, D_POOL)],
        out_shape=[jax.ShapeDtypeStruct((s, D_POOL), BF16), jax.ShapeDtypeStruct((s, D_POOL), BF16)],
        scratch_shapes=[pltpu.VMEM((TS + MAX_WINDOW, D_POOL), F32)],
        name=f"pool_fwd_{tag}", compiler_params=_cp(dimension_semantics=("arbitrary",)),
    )(zb, zb, _pool_consts(), p["w_pool_bd"], p["pool_scale"])


def _sgu_mix(vl, wpair_ref, lo, hi):
    rows = vl.shape[0]
    chunks = []
    for c in range(rows // CHUNK):
        vc = vl[CHUNK * c:CHUNK * (c + 1), :]
        parts = []
        for q in range(SGU_HEADS // 2):
            vq = vc[:, LANES * q:LANES * (q + 1)]
            rhs = jnp.concatenate([vq * lo, vq * hi], axis=0).astype(BF16)
            parts.append(_dot(wpair_ref[q], rhs))
        chunks.append(jnp.concatenate(parts, axis=1))
    return jnp.concatenate(chunks, axis=0)


def _sgu_front(zuv, lng, lnb):
    zu = zuv[:, :D_SGU]
    zv = zuv[:, D_SGU:]
    u = _gelu(zu)
    v = _gelu(zv)
    mu = jnp.mean(v, axis=-1, keepdims=True)
    vc = v - mu
    rs = lax.rsqrt(jnp.mean(vc * vc, axis=-1, keepdims=True) + EPS)
    vn = vc * rs
    return zu, zv, u, vn, rs, vn * lng + lnb


def _half_masks():
    lane = lax.broadcasted_iota(jnp.int32, (1, LANES), 1)
    lo = (lane < SGU_HEAD_DIM).astype(F32)
    return lo, 1.0 - lo


def _sgu_fwd(zuv, p, tag):
    s = zuv.shape[0]

    def body(z_ref, lng_ref, lnb_ref, wp_ref, bias_ref, oc_ref):
        lo, hi = _half_masks()
        _, _, u, _, _, vl = _sgu_front(z_ref[...], lng_ref[...], lnb_ref[...])
        mixed = _sgu_mix(vl, wp_ref, lo, hi) + jnp.tile(bias_ref[...], (TS // CHUNK, 1))
        oc_ref[...] = (u * mixed).astype(BF16)

    return pl.pallas_call(
        body, grid=(s // TS,),
        in_specs=[_row(TS, 2 * D_SGU), _const((1, D_SGU)), _const((1, D_SGU)),
                  _const((SGU_HEADS // 2, CHUNK, 2 * CHUNK)), _const((CHUNK, D_SGU))],
        out_specs=_row(TS, D_SGU),
        out_shape=jax.ShapeDtypeStruct((s, D_SGU), BF16),
        name=f"sgu_fwd_{tag}", compiler_params=_cp(dimension_semantics=("arbitrary",)),
    )(zuv, p["sgu_ln_g"], p["sgu_ln_b"], p["ws_pair"], p["bias_sp"])


def _blk_fwd(x0, oa, ob, oc, p, tag):
    s = x0.shape[0]
    ts = TS_FFN

    def body(x0_ref, oa_ref, ob_ref, oc_ref, wo_ref, g_ref, wg_ref, wu_ref, wd_ref,
             x1_ref, x2_ref, h2_ref, gt_ref, up_ref, ycat_ref):
        ycat = jnp.concatenate([oa_ref[...], ob_ref[...], oc_ref[...]], axis=1)
        ycat_ref[...] = ycat
        x1 = x0_ref[...] + _dot(ycat, wo_ref[...])
        x1_ref[...] = x1
        n, _ = _rms(x1)
        h2 = (n * g_ref[...]).astype(BF16)
        h2_ref[...] = h2
        gt = _dot_nt(h2, wg_ref[...])
        up = _dot_nt(h2, wu_ref[...])
        gt_ref[...] = gt.astype(BF16)
        up_ref[...] = up.astype(BF16)
        act = (gt * _sigmoid(gt) * up).astype(BF16)
        x2_ref[...] = x1 + _dot(act, wd_ref[...])

    return pl.pallas_call(
        body, grid=(s // ts,),
        in_specs=[_row(ts, D_MODEL), _row(ts, D_SSM), _row(ts, D_POOL), _row(ts, D_SGU),
                  _const((D_MODEL, D_MODEL)), _const((1, D_MODEL)), _const((D_FF, D_MODEL)),
                  _const((D_FF, D_MODEL)), _const((D_FF, D_MODEL))],
        out_specs=[_row(ts, D_MODEL), _row(ts, D_MODEL), _row(ts, D_MODEL), _row(ts, D_FF), _row(ts, D_FF),
                   _row(ts, D_MODEL)],
        out_shape=[jax.ShapeDtypeStruct((s, D_MODEL), F32), jax.ShapeDtypeStruct((s, D_MODEL), F32),
                   jax.ShapeDtypeStruct((s, D_MODEL), BF16), jax.ShapeDtypeStruct((s, D_FF), BF16),
                   jax.ShapeDtypeStruct((s, D_FF), BF16), jax.ShapeDtypeStruct((s, D_MODEL), BF16)],
        name=f"blk_fwd_{tag}", compiler_params=_cp(dimension_semantics=("arbitrary",)),
    )(x0, oa, ob, oc, p["w_out"], p["g_ffn"], p["w_gate"], p["w_up"], p["w_down"])


def _blk_bwd(dx2, x1, gt, up, p, tag):
    s = dx2.shape[0]
    ts = TS_FFN

    def body(dx2_ref, x1_ref, gt_ref, up_ref, wd_ref, wgt_ref, wut_ref, wo_ref, g_ref,
             dx1_ref, da_ref, db_ref, dc_ref, dgt_ref, dup_ref, act_ref, dg_ref):
        @pl.when(pl.program_id(0) == 0)
        def _():
            dg_ref[...] = jnp.zeros_like(dg_ref)

        dx2v = dx2_ref[...]
        dact = _dot_nt(dx2v.astype(BF16), wd_ref[...])
        gf = gt_ref[...].astype(F32)
        uf = up_ref[...].astype(F32)
        sg = _sigmoid(gf)
        sl = gf * sg
        act_ref[...] = (sl * uf).astype(BF16)
        dgt = (dact * uf * (sg * (1.0 + gf * (1.0 - sg)))).astype(BF16)
        dup = (dact * sl).astype(BF16)
        dgt_ref[...] = dgt
        dup_ref[...] = dup
        dh2 = _dot(dgt, wgt_ref[...]) + _dot(dup, wut_ref[...])
        n, r = _rms(x1_ref[...])
        dxn, dgp = _rms_bwd(dh2, n, r, g_ref[...])
        dg_ref[...] += _colsum8(dgp)
        dx1 = dx2v + dxn
        dx1_ref[...] = dx1
        dy = _dot_nt(dx1.astype(BF16), wo_ref[...])
        da_ref[...] = dy[:, :D_SSM]
        db_ref[...] = dy[:, D_SSM:D_SSM + D_POOL]
        dc_ref[...] = dy[:, D_SSM + D_POOL:]

    return pl.pallas_call(
        body, grid=(s // ts,),
        in_specs=[_row(ts, D_MODEL), _row(ts, D_MODEL), _row(ts, D_FF), _row(ts, D_FF),
                  _const((D_FF, D_MODEL)), _const((D_FF, D_MODEL)), _const((D_FF, D_MODEL)),
                  _const((D_MODEL, D_MODEL)), _const((1, D_MODEL))],
        out_specs=[_row(ts, D_MODEL), _row(ts, D_SSM), _row(ts, D_POOL), _row(ts, D_SGU), _row(ts, D_FF),
                   _row(ts, D_FF), _row(ts, D_FF), _acc((SUBLANES, D_MODEL))],
        out_shape=[jax.ShapeDtypeStruct((s, D_MODEL), F32), jax.ShapeDtypeStruct((s, D_SSM), F32),
                   jax.ShapeDtypeStruct((s, D_POOL), F32), jax.ShapeDtypeStruct((s, D_SGU), F32),
                   jax.ShapeDtypeStruct((s, D_FF), BF16), jax.ShapeDtypeStruct((s, D_FF), BF16),
                   jax.ShapeDtypeStruct((s, D_FF), BF16), jax.ShapeDtypeStruct((SUBLANES, D_MODEL), F32)],
        name=f"blk_bwd_{tag}", compiler_params=_cp(dimension_semantics=("arbitrary",)),
    )(dx2, x1, gt, up, p["w_down"], p["w_gate"], p["w_up"], p["w_out"], p["g_ffn"])


def _s5_bwd(dout, u, y, h_re, h_im, p, tag):
    s = u.shape[0]
    nt = s // TS
    seg = SEG

    def rev(n):
        return pl.BlockSpec((TS, n), lambda i: (nt - 1 - i, 0))

    def body(do_ref, u_ref, y_ref, hr_ref, hi_ref, ar_ref, ai_ref, cb_ref, bb_ref, dsk_ref,
             wglu_ref, bglu_ref,
             du_ref, dct_ref, dbb_ref, dar_ref, dai_ref, dd_ref, dwglu_ref, dbglu_ref,
             gr, gi, hsr, hsi, er, ei, jr, ji, cr, ci, stage, perm):
        @pl.when(pl.program_id(0) == 0)
        def _():
            for ref in (cr, ci, dct_ref, dbb_ref, dar_ref, dai_ref, dd_ref, dwglu_ref, dbglu_ref):
                ref[...] = jnp.zeros_like(ref)

        uv = _interleave_rows(u_ref[...], stage, perm)
        yv = y_ref[...]
        dov = _interleave_rows(do_ref[...], stage, perm)
        g = _gelu(yv)
        gb = g.astype(BF16)
        sg = _sigmoid(_dot(gb, wglu_ref[...]) + bglu_ref[...])
        dpre = dov * g * sg * (1.0 - sg)
        dpb = dpre.astype(BF16)
        dwglu_ref[...] += _dot_tn(gb, dpb)
        dbglu_ref[...] += _colsum8(dpre)
        dy = (dov * sg + _dot_nt(dpb, wglu_ref[...])) * _gelu_grad(yv)
        dd_ref[...] += _colsum8(dy * uv)
        dyb = dy.astype(BF16)
        _to_slabs(hsr, hr_ref[...].astype(F32))
        _to_slabs(hsi, hi_ref[...].astype(F32))
        for j in range(N_USLAB):
            dct_ref[j] += _dot_tn(_state_cat(hsr, hsi, j), _lanes(dyb, j))
            _state_split(gr, gi, j, _dot(_lanes(dyb, j), cb_ref[j]))
        ar = ar_ref[...]
        ai = -ai_ref[...]

        def local(k, h):
            rows = _scan_rows(seg - 1 - k)
            nr, ni = _cmul(ar, ai, h[0], h[1])
            return nr + gr[:, rows, :], ni + gi[:, rows, :]

        zero = jnp.zeros((N_SLAB, SUBLANES, LANES), F32)
        e_r, e_i = lax.fori_loop(0, seg, local, (zero, zero))
        er[...] = e_r
        ei[...] = e_i
        pr, pi = _cpow(ar[:, 0:1, :], ai[:, 0:1, :], seg)
        c_r = cr[...]
        c_i = ci[...]
        for j in range(SUBLANES - 1, -1, -1):
            jr[:, j:j + 1, :] = c_r
            ji[:, j:j + 1, :] = c_i
            n_r, n_i = _cmul(pr, pi, c_r, c_i)
            c_r = n_r + er[:, j:j + 1, :]
            c_i = n_i + ei[:, j:j + 1, :]
        cr[...] = c_r
        ci[...] = c_i

        def full(k, carry):
            g_r, g_i, a_r, a_i = carry
            rows = _scan_rows(seg - 1 - k)
            h_r = hsr[:, rows, :]
            h_i = hsi[:, rows, :]
            a_r = a_r + g_r * h_r + g_i * h_i
            a_i = a_i + g_i * h_r - g_r * h_i
            nr, ni = _cmul(ar, ai, g_r, g_i)
            nr = nr + gr[:, rows, :]
            ni = ni + gi[:, rows, :]
            gr[:, rows, :] = nr
            gi[:, rows, :] = ni
            return nr, ni, a_r, a_i

        _, _, a_r, a_i = lax.fori_loop(0, seg, full, (jr[...], ji[...], zero, zero))
        dar_ref[...] += a_r
        dai_ref[...] += a_i
        ub = uv.astype(BF16)
        dus = []
        for j in range(N_USLAB):
            gb_j = _state_cat(gr, gi, j)
            dbb_ref[j] += _dot_tn(_lanes(ub, j), gb_j)
            dus.append(_dot(gb_j, bb_ref[j]))
        du = dy * dsk_ref[...] + jnp.concatenate(dus, axis=1)
        du_ref[...] = _deinterleave_rows(du, stage, perm)

    slab = (N_SLAB, SUBLANES, LANES)
    big = (N_SLAB, TS, LANES)
    uslab = (N_USLAB, TS, LANES)
    return pl.pallas_call(
        body, grid=(nt,),
        in_specs=[rev(D_SSM), rev(D_SSM), rev(D_SSM), rev(N_STATE), rev(N_STATE), _const(slab), _const(slab),
                  _const(S5_IN), _const(S5_OUT), _const((1, D_SSM)), _const((D_SSM, D_SSM)), _const((1, D_SSM))],
        out_specs=[rev(D_SSM), _acc(S5_OUT), _acc(S5_IN), _acc(slab), _acc(slab), _acc((SUBLANES, D_SSM)),
                   _acc((D_SSM, D_SSM)), _acc((SUBLANES, D_SSM))],
        out_shape=[jax.ShapeDtypeStruct((s, D_SSM), F32), jax.ShapeDtypeStruct(S5_OUT, F32),
                   jax.ShapeDtypeStruct(S5_IN, F32), jax.ShapeDtypeStruct(slab, F32),
                   jax.ShapeDtypeStruct(slab, F32), jax.ShapeDtypeStruct((SUBLANES, D_SSM), F32),
                   jax.ShapeDtypeStruct((D_SSM, D_SSM), F32), jax.ShapeDtypeStruct((SUBLANES, D_SSM), F32)],
        scratch_shapes=[pltpu.VMEM(big, F32), pltpu.VMEM(big, F32), pltpu.VMEM(big, F32), pltpu.VMEM(big, F32),
                        pltpu.VMEM(slab, F32), pltpu.VMEM(slab, F32), pltpu.VMEM(slab, F32), pltpu.VMEM(slab, F32),
                        pltpu.VMEM((N_SLAB, 1, LANES), F32), pltpu.VMEM((N_SLAB, 1, LANES), F32),
                        pltpu.VMEM(uslab, F32), pltpu.VMEM(uslab, F32)],
        name=f"s5_bwd_{tag}", compiler_params=_cp(dimension_semantics=("arbitrary",)),
    )(dout, u, y, h_re, h_im, p["a_re8"], p["a_im8"], p["cb3"], p["bb3"], p["d_skip"], p["w_glu"], p["b_glu"])


def _pool_bwd(dout, pooled, p, tag):
    s = dout.shape[0]
    nt = s // TS
    hb = TS // MAX_WINDOW

    def halo(n):
        return pl.BlockSpec((MAX_WINDOW, n), lambda i: (jnp.minimum((i + 1) * hb, nt * hb - 1), 0))

    def body(do_ref, po_ref, doh_ref, wl_ref, w_ref, wt_ref, sc_ref, dz_ref, dw_ref, dsc_ref, buf):
        i = pl.program_id(0)

        @pl.when(i == 0)
        def _():
            dw_ref[...] = jnp.zeros_like(dw_ref)
            dsc_ref[...] = jnp.zeros_like(dsc_ref)

        wl = wl_ref[...]
        sc = sc_ref[...]
        dov = do_ref[...]
        pooled_b = po_ref[...]
        dsc_ref[...] += _colsum8(dov * _dot(pooled_b, w_ref[...]))
        dmix = (dov * sc).astype(BF16)
        dw_ref[...] += _dot_tn(pooled_b, dmix)
        dpool = _dot(dmix, wt_ref[...])
        dpool_h = _dot((doh_ref[...] * sc).astype(BF16), wt_ref[...])
        buf[pl.ds(0, TS), :] = dpool / _pool_count(i, TS, wl)
        buf[pl.ds(TS, MAX_WINDOW), :] = jnp.where(i < nt - 1, dpool_h / _pool_count(i, MAX_WINDOW, wl, TS), 0.0)
        dz_ref[...] = _window_sum(buf, 0, TS, wl, 1) - dpool

    return pl.pallas_call(
        body, grid=(nt,),
        in_specs=[_row(TS, D_POOL), _row(TS, D_POOL), halo(D_POOL), _const((1, D_POOL)), _const((D_POOL, D_POOL)),
                  _const((D_POOL, D_POOL)), _const((1, D_POOL))],
        out_specs=[_row(TS, D_POOL), _acc((D_POOL, D_POOL)), _acc((SUBLANES, D_POOL))],
        out_shape=[jax.ShapeDtypeStruct((s, D_POOL), F32), jax.ShapeDtypeStruct((D_POOL, D_POOL), F32),
                   jax.ShapeDtypeStruct((SUBLANES, D_POOL), F32)],
        scratch_shapes=[pltpu.VMEM((TS + MAX_WINDOW, D_POOL), F32)],
        name=f"pool_bwd_{tag}", compiler_params=_cp(dimension_semantics=("arbitrary",)),
    )(dout, pooled, dout, _pool_consts(), p["w_pool_bd"], p["w_pool_bd_t"], p["pool_scale"])


def _sgu_bwd(dout, zuv, p, tag):
    s = zuv.shape[0]

    def body(do_ref, z_ref, lng_ref, lnb_ref, wp_ref, wpt_ref, bias_ref,
             dz_ref, dws_ref, dbias_ref, dlng_ref, dlnb_ref):
        @pl.when(pl.program_id(0) == 0)
        def _():
            for ref in (dws_ref, dbias_ref, dlng_ref, dlnb_ref):
                ref[...] = jnp.zeros_like(ref)

        lo, hi = _half_masks()
        lng = lng_ref[...]
        zu, zv, u, vn, rs, vl = _sgu_front(z_ref[...], lng, lnb_ref[...])
        mixed = _sgu_mix(vl, wp_ref, lo, hi) + jnp.tile(bias_ref[...], (TS // CHUNK, 1))
        dov = do_ref[...]
        dzu = dov * mixed * _gelu_grad(zu)
        dmix = dov * u
        dbias = dbias_ref[...]
        for c in range(TS // CHUNK):
            dmc = dmix[CHUNK * c:CHUNK * (c + 1), :]
            dbias = dbias + dmc
            vlc = vl[CHUNK * c:CHUNK * (c + 1), :].astype(BF16)
            for q in range(SGU_HEADS // 2):
                dq = dmc[:, LANES * q:LANES * (q + 1)]
                vq = vlc[:, LANES * q:LANES * (q + 1)]
                dws_ref[2 * q] += _dot_nt((dq * lo).astype(BF16), vq)
                dws_ref[2 * q + 1] += _dot_nt((dq * hi).astype(BF16), vq)
        dbias_ref[...] = dbias
        dvl = _sgu_mix(dmix, wpt_ref, lo, hi)
        dlng_ref[...] += _colsum8(dvl * vn)
        dlnb_ref[...] += _colsum8(dvl)
        dvn = dvl * lng
        dv = rs * (dvn - jnp.mean(dvn, axis=-1, keepdims=True) - vn * jnp.mean(dvn * vn, axis=-1, keepdims=True))
        dz_ref[...] = jnp.concatenate([dzu, dv * _gelu_grad(zv)], axis=1)

    return pl.pallas_call(
        body, grid=(s // TS,),
        in_specs=[_row(TS, D_SGU), _row(TS, 2 * D_SGU), _const((1, D_SGU)), _const((1, D_SGU)),
                  _const((SGU_HEADS // 2, CHUNK, 2 * CHUNK)), _const((SGU_HEADS // 2, CHUNK, 2 * CHUNK)),
                  _const((CHUNK, D_SGU))],
        out_specs=[_row(TS, 2 * D_SGU), _acc((SGU_HEADS, CHUNK, CHUNK)), _acc((CHUNK, D_SGU)),
                   _acc((SUBLANES, D_SGU)), _acc((SUBLANES, D_SGU))],
        out_shape=[jax.ShapeDtypeStruct((s, 2 * D_SGU), F32), jax.ShapeDtypeStruct((SGU_HEADS, CHUNK, CHUNK), F32),
                   jax.ShapeDtypeStruct((CHUNK, D_SGU), F32), jax.ShapeDtypeStruct((SUBLANES, D_SGU), F32),
                   jax.ShapeDtypeStruct((SUBLANES, D_SGU), F32)],
        name=f"sgu_bwd_{tag}", compiler_params=_cp(dimension_semantics=("arbitrary",)),
    )(dout, zuv, p["sgu_ln_g"], p["sgu_ln_b"], p["ws_pair"], p["ws_pair_t"], p["bias_sp"])


def _mix_in_bwd(dza, dzb, dzuv, x0, dx1, p, tag):
    s = x0.shape[0]

    def body(da_ref, db_ref, dc_ref, x_ref, dx1_ref, wt_ref, g_ref, dx0_ref, dz_ref, dg_ref):
        @pl.when(pl.program_id(0) == 0)
        def _():
            dg_ref[...] = jnp.zeros_like(dg_ref)

        dz = jnp.concatenate([da_ref[...], db_ref[...], dc_ref[...]], axis=1).astype(BF16)
        dz_ref[...] = dz
        n, r = _rms(x_ref[...])
        dxn, dgp = _rms_bwd(_dot(dz, wt_ref[...]), n, r, g_ref[...])
        dg_ref[...] += _colsum8(dgp)
        dx0_ref[...] = dx1_ref[...] + dxn

    return pl.pallas_call(
        body, grid=(s // TS,),
        in_specs=[_row(TS, D_SSM), _row(TS, D_POOL), _row(TS, 2 * D_SGU), _row(TS, D_MODEL), _row(TS, D_MODEL),
                  _const((D_IN, D_MODEL)), _const((1, D_MODEL))],
        out_specs=[_row(TS, D_MODEL), _row(TS, D_IN), _acc((SUBLANES, D_MODEL))],
        out_shape=[jax.ShapeDtypeStruct((s, D_MODEL), F32), jax.ShapeDtypeStruct((s, D_IN), BF16),
                   jax.ShapeDtypeStruct((SUBLANES, D_MODEL), F32)],
        name=f"mix_in_bwd_{tag}", compiler_params=_cp(dimension_semantics=("arbitrary",)),
    )(dza, dzb, dzuv, x0, dx1, p["w_in"], p["g_mix"])


def _head(x, target, g):
    s = x.shape[0]

    def body(x_ref, t_ref, g_ref, dx_ref, loss_ref, dg_ref):
        @pl.when(pl.program_id(0) == 0)
        def _():
            loss_ref[...] = jnp.zeros_like(loss_ref)
            dg_ref[...] = jnp.zeros_like(dg_ref)

        gv = g_ref[...]
        n, r = _rms(x_ref[...])
        diff = n * gv - t_ref[...]
        loss_ref[...] += jnp.sum(diff * diff) * (0.5 / D_MODEL)
        dxn, dgp = _rms_bwd(diff * (1.0 / D_MODEL), n, r, gv)
        dg_ref[...] += _colsum8(dgp)
        dx_ref[...] = dxn

    return pl.pallas_call(
        body, grid=(s // TS,),
        in_specs=[_row(TS, D_MODEL), _row(TS, D_MODEL), _const((1, D_MODEL))],
        out_specs=[_row(TS, D_MODEL), _acc((SUBLANES, LANES)), _acc((SUBLANES, D_MODEL))],
        out_shape=[jax.ShapeDtypeStruct((s, D_MODEL), F32), jax.ShapeDtypeStruct((SUBLANES, LANES), F32),
                   jax.ShapeDtypeStruct((SUBLANES, D_MODEL), F32)],
        name="head", compiler_params=_cp(dimension_semantics=("arbitrary",)),
    )(x, target, g)


def _atb(a, b, tn, tag):
    s, ka = a.shape
    kb = b.shape[1]
    ns = s // TS

    def body(a_ref, b_ref, o_ref):
        @pl.when(pl.program_id(1) == 0)
        def _():
            o_ref[...] = jnp.zeros_like(o_ref)

        o_ref[...] += _dot_tn(a_ref[...].astype(BF16), b_ref[...].astype(BF16))

    return pl.pallas_call(
        body, grid=(kb // tn, ns),
        in_specs=[pl.BlockSpec((TS, ka), lambda j, i: (i, 0)), pl.BlockSpec((TS, tn), lambda j, i: (i, j))],
        out_specs=pl.BlockSpec((ka, tn), lambda j, i: (0, j)),
        out_shape=jax.ShapeDtypeStruct((ka, kb), F32),
        name=f"atb_{tag}", compiler_params=_cp(dimension_semantics=("arbitrary", "arbitrary")),
    )(a, b)


def _s5_discretise(a_re, a_im, log_dt, b_re, b_im):
    dt = jnp.exp(log_dt)[:, None]
    mag = jnp.exp(a_re * dt)
    ar = mag * jnp.cos(a_im * dt)
    ai = mag * jnp.sin(a_im * dt)
    den = a_re * a_re + a_im * a_im
    f_re = ((ar - 1.0) * a_re + ai * a_im) / den
    f_im = (ai * a_re - (ar - 1.0) * a_im) / den
    bb_re = f_re[..., None] * b_re - f_im[..., None] * b_im
    bb_im = f_re[..., None] * b_im + f_im[..., None] * b_re
    return ar, ai, bb_re, bb_im


def _block_diag(blocks):
    g, r, c = blocks.shape
    eye = jnp.eye(g, dtype=blocks.dtype)
    return (blocks[:, :, None, :] * eye[:, None, :, None]).reshape(g * r, g * c)


def _block_diag_extract(m, g):
    r = m.shape[0] // g
    c = m.shape[1] // g
    eye = jnp.eye(g, dtype=m.dtype)
    return jnp.sum(m.reshape(g, r, g, c) * eye[:, None, :, None], axis=2)


GROUPS_PER_SLAB = N_GROUPS // N_USLAB


def _slab_diag(blocks):
    k = GROUPS_PER_SLAB
    _, r, c = blocks.shape
    eye = jnp.eye(k, dtype=blocks.dtype)
    spread = blocks.reshape(N_USLAB, k, r, 1, c) * eye[None, :, None, :, None]
    return spread.reshape(N_USLAB, k * r, k * c)


def _slab_diag_extract(m):
    k = GROUPS_PER_SLAB
    r, c = m.shape[1] // k, m.shape[2] // k
    eye = jnp.eye(k, dtype=m.dtype)
    return jnp.sum(m.reshape(N_USLAB, k, r, k, c) * eye[None, :, None, :, None], axis=3).reshape(N_GROUPS, r, c)


def _state_slabs(v):
    return jnp.broadcast_to(v.reshape(N_SLAB, 1, LANES), (N_SLAB, SUBLANES, LANES))


def _tril():
    return jnp.tril(jnp.ones((CHUNK, CHUNK), dtype=bool))


def _layer_params(w, l):
    row = lambda v: v.reshape(1, -1)
    t = lambda m: jnp.swapaxes(m, -1, -2)
    ar, ai, bb_re, bb_im = _s5_discretise(w["A_re"][l], w["A_im"][l], w["log_dt"][l], w["B_re"][l], w["B_im"][l])
    bbt3 = jnp.concatenate([_slab_diag(t(bb_re)), _slab_diag(t(bb_im))], axis=2).astype(BF16)
    ct3 = jnp.concatenate([_slab_diag(t(w["C_re"][l])), -_slab_diag(t(w["C_im"][l]))], axis=1).astype(BF16)
    ws = jnp.where(_tril()[None], w["w_spatial"][l], 0.0)
    pair = lambda m: jnp.stack([jnp.concatenate([m[2 * q], m[2 * q + 1]], axis=1)
                                for q in range(SGU_HEADS // 2)]).astype(BF16)
    wp = _block_diag(w["w_pool"][l]).astype(BF16)
    p = dict(
        g_mix=row(w["g_mix"][l]), g_ffn=row(w["g_ffn"][l]), d_skip=row(w["D_skip"][l]), b_glu=row(w["b_glu"][l]),
        pool_scale=row(w["pool_scale"][l]), sgu_ln_g=row(w["sgu_ln_g"][l]), sgu_ln_b=row(w["sgu_ln_b"][l]),
        a_re8=_state_slabs(ar), a_im8=_state_slabs(ai),
        bbt3=bbt3, bb3=t(bbt3), ct3=ct3, cb3=t(ct3),
        w_pool_bd=wp, w_pool_bd_t=t(wp), ws_pair=pair(ws), ws_pair_t=pair(t(ws)),
        bias_sp=jnp.repeat(t(w["b_spatial"][l]), SGU_HEAD_DIM, axis=1),
    )
    return p


MIX_WEIGHTS = ("w_in", "w_glu")
FFN_WEIGHTS = ("w_out", "w_gate", "w_up", "w_down")


def _with_big(p, mats):
    p.update(mats)


def _rows_sum(v):
    return jnp.sum(v, axis=0)


ATB_COLS = 512


def _after(v, token):
    return v if token is None else v + token


def _layer_bwd(dx2, sv, p, w, l, tag, on_grads, token):
    t = lambda m: jnp.swapaxes(m, -1, -2)
    dx1, da, db, dc, dgt, dup, act, dg_ffn = _blk_bwd(dx2, sv["x1"], sv["gt"], sv["up"],
                                                      dict(p, g_ffn=_after(p["g_ffn"], token)), tag)
    token = on_grads(l, "ffn", {
        "w_down": _atb(act, dx2, ATB_COLS, tag + "_wd"), "w_gate": _atb(dgt, sv["h2"], ATB_COLS, tag + "_wg"),
        "w_up": _atb(dup, sv["h2"], ATB_COLS, tag + "_wu"), "w_out": _atb(sv["ycat"], dx1, ATB_COLS, tag + "_wo")})
    g = {}
    g["g_ffn"] = _rows_sum(dg_ffn)
    dza, dct3, dbbt3, dar8, dai8, dd8, dwglu, dbglu8 = _s5_bwd(
        da, sv["za"], sv["y"], sv["h_re"], sv["h_im"], dict(p, d_skip=_after(p["d_skip"], token)), tag)
    dzb, dwp, dsc8 = _pool_bwd(db, sv["pooled"], p, tag)
    dzuv, dws, dbias, dlng8, dlnb8 = _sgu_bwd(dc, sv["zuv"], p, tag)
    dx0, dz, dg_mix = _mix_in_bwd(dza, dzb, dzuv, sv["x0"], dx1, p, tag)
    token = on_grads(l, "mix", {"w_in": _atb(dz, sv["h1"], D_MODEL, tag + "_wi"), "w_glu": dwglu})
    g["g_mix"] = _rows_sum(dg_mix)
    g["b_glu"] = _rows_sum(dbglu8)
    g["D_skip"] = _rows_sum(dd8)
    half = N_STATE // N_USLAB
    g["C_re"] = t(_slab_diag_extract(dct3[:, :half, :]))
    g["C_im"] = -t(_slab_diag_extract(dct3[:, half:, :]))
    dar = jnp.sum(dar8, axis=1).reshape(N_GROUPS, SSM_STATE)
    dai = jnp.sum(dai8, axis=1).reshape(N_GROUPS, SSM_STATE)
    dbb_re = t(_slab_diag_extract(dbbt3[:, :, :half]))
    dbb_im = t(_slab_diag_extract(dbbt3[:, :, half:]))
    _, disc_vjp = jax.vjp(_s5_discretise, w["A_re"][l], w["A_im"][l], w["log_dt"][l], w["B_re"][l], w["B_im"][l])
    g["A_re"], g["A_im"], g["log_dt"], g["B_re"], g["B_im"] = disc_vjp((dar, dai, dbb_re, dbb_im))
    g["w_pool"] = _block_diag_extract(dwp, len(POOL_WINDOWS))
    g["pool_scale"] = _rows_sum(dsc8)
    g["sgu_ln_g"] = _rows_sum(dlng8)
    g["sgu_ln_b"] = _rows_sum(dlnb8)
    g["w_spatial"] = jnp.where(_tril()[None], dws, 0.0)
    g["b_spatial"] = t(jnp.sum(dbias.reshape(CHUNK, SGU_HEADS, SGU_HEAD_DIM), axis=-1))
    return dx0, g, token


def _local_step(x, target, w, get_big, on_grads):
    params = [_layer_params(w, l) for l in range(DEPTH)]
    saved = []
    h = x
    for l in range(DEPTH):
        p, tag = params[l], f"l{l}"
        _with_big(p, get_big(l, "mix", [h]))
        za, zb, zuv, h1 = _mix_in_fwd(h, p["g_mix"], p["w_in"], tag)
        oa, y, h_re, h_im = _s5_fwd(za, p, tag)
        ob, pooled = _pool_fwd(zb, p, tag)
        oc = _sgu_fwd(zuv, p, tag)
        _with_big(p, get_big(l, "ffn", [oa, ob, oc]))
        x1, x2, h2, gt, up, ycat = _blk_fwd(h, oa, ob, oc, p, tag)
        saved.append(dict(x0=h, za=za, zuv=zuv, h1=h1, ycat=ycat, y=y, h_re=h_re, h_im=h_im, pooled=pooled, x1=x1,
                          h2=h2, gt=gt, up=up))
        h = x2
    dx, loss8, dgf8 = _head(h, target, w["g_final"].reshape(1, -1))
    grads = [None] * DEPTH
    token = None
    for l in reversed(range(DEPTH)):
        dx, grads[l], token = _layer_bwd(dx, saved[l], params[l], w, l, f"l{l}", on_grads, token)
    g = {n: jnp.stack([grads[l][n] for l in range(DEPTH)]) for n in SMALL if n != "g_final"}
    g["g_final"] = _rows_sum(dgf8)
    return loss8[0, 0], dx, g


_ANY = pl.BlockSpec(memory_space=pl.ANY)
_MESH = pl.DeviceIdType.MESH


def _place():
    return lax.axis_index("x"), lax.axis_index("y"), lax.axis_index("c")


def _other_chips(x, y):
    return [(1 - x, y), (x, 1 - y), (1 - x, 1 - y)]


def _dma_sems(n):
    return pltpu.SemaphoreType.DMA((n,))


def _remote(src, dst, send_sems, recv_sems, k, to):
    return pltpu.make_async_remote_copy(src_ref=src, dst_ref=dst, send_sem=send_sems.at[k], recv_sem=recv_sems.at[k],
                                        device_id=to, device_id_type=_MESH)


_HBM = pl.BlockSpec(memory_space=pltpu.HBM)
_SEM = pl.BlockSpec(memory_space=pltpu.SEMAPHORE)
_EFFECT = pltpu.SideEffectType.DATAFLOW_SIDE_EFFECTING
N_REL = N_CHIPS - 1


def _gather_plan(x, y, c, srcs, lands):
    plan = []
    for l in lands:
        r = l.shape[0] // N_CHIPS
        rows = l.at[pl.ds((2 * x + y) * r, r)]
        plan += [(rows, rows, (cx, cy, c)) for cx, cy in _other_chips(x, y)]
    return plan


def _slab_plan(x, y, c, srcs, lands):
    return [(s.at[2 * cx + cy], l.at[j], (cx, cy, c))
            for s, l in zip(srcs, lands) for j, (cx, cy) in enumerate(_other_chips(x, y))]


def _plan_copies(plan, srcs, lands, send_sems, recv_sems):
    x, y, c = _place()
    return [_remote(s, d, send_sems, recv_sems, k, to) for k, (s, d, to) in enumerate(plan(x, y, c, srcs, lands))]


def _hbm(a):
    return pltpu.with_memory_space_constraint(a, pltpu.HBM)


def _everyone_plan(x, y, c, srcs, lands):
    me = 4 * x + 2 * y + c
    peers = [(x, y, 1 - c)] + [(cx, cy, cc) for cx, cy in _other_chips(x, y) for cc in (c, 1 - c)]
    return [(s, l.at[me], peer) for s, l in zip(srcs, lands) for peer in peers]


def _copies_start(name, plan, srcs, lands, ncopies):
    ns, n = len(srcs), len(srcs) + len(lands)

    def body(*refs):
        for cp in _plan_copies(plan, refs[:ns], refs[ns:n], refs[n], refs[n + 1]):
            cp.start()
        refs[-1][...] = jnp.zeros_like(refs[-1])

    ref_out = [pltpu.HBM(a.shape, a.dtype) for a in (*srcs, *lands)]
    out = pl.pallas_call(
        body, name=name, in_specs=[_HBM] * n,
        out_shape=(_dma_sems(ncopies), _dma_sems(ncopies), *ref_out, jax.ShapeDtypeStruct((SUBLANES, LANES), F32)),
        out_specs=(_SEM, _SEM, *[_HBM] * n, pl.BlockSpec(memory_space=pltpu.VMEM)),
        input_output_aliases={i: 2 + i for i in range(n)},
        compiler_params=pltpu.CompilerParams(has_side_effects=_EFFECT),
    )(*[_hbm(a) for a in (*srcs, *lands)])
    return dict(name=name, plan=plan, sems=out[:2], srcs=out[2:2 + ns], lands=out[2 + ns:2 + n], token=out[-1][0, 0],
                token_tile=out[-1])


def _copies_wait(started, after):
    ns = len(started["srcs"])
    n = ns + len(started["lands"])
    plan = started["plan"]

    def body(*refs):
        for cp in _plan_copies(plan, refs[:ns], refs[ns:n], refs[n], refs[n + 1]):
            cp.wait_send()
            cp.wait_recv()

    args = (*started["srcs"], *started["lands"])
    out = pl.pallas_call(
        body, name=started["name"] + "_wait", out_shape=[pltpu.HBM(a.shape, a.dtype) for a in args],
        in_specs=[_HBM] * n + [_SEM, _SEM] + [_ANY] * len(after), out_specs=[_HBM] * n,
        input_output_aliases={i: i for i in range(n)},
        compiler_params=pltpu.CompilerParams(has_side_effects=_EFFECT),
    )(*args, *started["sems"], *after)
    return out[ns:]


def _place_shards(ws, layer, sel, after, tag):
    nw = len(ws)

    def body(sel_ref, *refs):
        for i in range(nw):
            refs[nw + len(after) + i][...] = refs[i][...].astype(BF16)

    return pl.pallas_call(
        body, grid_spec=pltpu.PrefetchScalarGridSpec(
            num_scalar_prefetch=1, grid=(1,),
            in_specs=[pl.BlockSpec((None,) + a.shape[1:], lambda i, s: (layer, 0, 0)) for a in ws] + [_ANY] * len(after),
            out_specs=[pl.BlockSpec(a.shape[1:], lambda i, s: (s[1], 0)) for a in ws]),
        out_shape=[jax.ShapeDtypeStruct((N_CHIPS * a.shape[1], a.shape[2]), BF16) for a in ws],
        name=f"place_shards_{tag}", compiler_params=_cp(dimension_semantics=("arbitrary",)),
    )(sel, *ws, *after)


def _swap_halves(g4s, tag):
    nw = len(g4s)

    def body(*refs):
        ins, outs = refs[:nw], refs[nw:2 * nw]
        send_sems, recv_sems = refs[2 * nw:]
        x, y, c = _place()
        copies = [_remote(ins[i].at[:, 1 - c], outs[i], send_sems, recv_sems, i, (x, y, 1 - c)) for i in range(nw)]
        for cp in copies:
            cp.start()
        for cp in copies:
            cp.wait()

    return pl.pallas_call(
        body, out_shape=[jax.ShapeDtypeStruct((g.shape[0],) + g.shape[2:], g.dtype) for g in g4s],
        in_specs=[_ANY] * nw, out_specs=[_ANY] * nw, scratch_shapes=[_dma_sems(nw), _dma_sems(nw)],
        name=f"swap_halves_{tag}",
    )(*g4s)


def _share_halves(fs, layer, tag):
    nw = len(fs)

    def body(*refs):
        ins = refs[:nw]
        send_sems, recv_sems = refs[2 * nw:]
        x, y, c = _place()

        def half(i, who):
            h = ins[i].shape[1] // 2
            return ins[i].at[layer, pl.ds(who * h, h)]

        sends = [_remote(half(i, c), half(i, c), send_sems, recv_sems, i, (x, y, 1 - c)) for i in range(nw)]
        for cp in sends:
            cp.start()
        for i in range(nw):
            sends[i].wait_send()
            _remote(half(i, c), half(i, 1 - c), send_sems, recv_sems, i, (x, y, 1 - c)).wait_recv()

    return pl.pallas_call(
        body, out_shape=[jax.ShapeDtypeStruct(f.shape, f.dtype) for f in fs], in_specs=[_ANY] * nw,
        out_specs=[_ANY] * nw, input_output_aliases={i: i for i in range(nw)},
        scratch_shapes=[_dma_sems(nw), _dma_sems(nw)], name=f"share_halves_{tag}",
    )(*fs)


def _add_halves(g4s, recvs, sel, tag):
    nw = len(g4s)

    def body(sel_ref, *refs):
        for i in range(nw):
            refs[2 * nw + i][...] = (refs[i][...] + refs[nw + i][...]).astype(BF16)

    mine = [pl.BlockSpec((None, None) + g.shape[2:], lambda k, s: (k, s[0], 0, 0)) for g in g4s]
    slab = [pl.BlockSpec((None,) + g.shape[2:], lambda k, s: (k, 0, 0)) for g in g4s]
    return pl.pallas_call(
        body, grid_spec=pltpu.PrefetchScalarGridSpec(num_scalar_prefetch=1, grid=(N_CHIPS,), in_specs=mine + slab,
                                                     out_specs=slab),
        out_shape=[jax.ShapeDtypeStruct(r.shape, BF16) for r in recvs], name=f"add_halves_{tag}",
        compiler_params=_cp(dimension_semantics=("arbitrary",)),
    )(sel, *g4s, *recvs)


def _add_chips(ps, slabs, fs, layer, sel, tag):
    nw = len(ps)
    old = [f for f in fs if f is not None]

    def body(sel_ref, *refs):
        outs = refs[2 * nw + len(old):]
        for i in range(nw):
            acc = refs[i][...].astype(F32)
            for j in range(N_REL):
                acc = acc + refs[nw + i][j].astype(F32)
            outs[i][...] = acc

    shapes = [(DEPTH, 2 * p.shape[1], p.shape[2]) for p in ps]
    in_specs = [pl.BlockSpec((None,) + p.shape[1:], lambda i, s: (s[1], 0, 0)) for p in ps]
    in_specs += [pl.BlockSpec(sl.shape, lambda i, s: (0, 0, 0)) for sl in slabs]
    in_specs += [_ANY] * len(old)
    first_old = 1 + 2 * nw
    aliases, k = {}, 0
    for i, f in enumerate(fs):
        if f is not None:
            aliases[first_old + k] = i
            k += 1
    return pl.pallas_call(
        body, grid_spec=pltpu.PrefetchScalarGridSpec(
            num_scalar_prefetch=1, grid=(1,), in_specs=in_specs,
            out_specs=[pl.BlockSpec((None,) + p.shape[1:], lambda i, s: (layer, s[0], 0)) for p in ps]),
        out_shape=[jax.ShapeDtypeStruct(sh, F32) for sh in shapes], input_output_aliases=aliases,
        name=f"add_chips_{tag}", compiler_params=_cp(dimension_semantics=("arbitrary",)),
    )(sel, *ps, *slabs, *old)


def _adamw_math(w, g, m, v):
    m = ADAM_B1 * m + (1.0 - ADAM_B1) * g
    v = ADAM_B2 * v + (1.0 - ADAM_B2) * (g * g)
    m_hat = m / (1.0 - ADAM_B1 ** ADAM_STEP)
    v_hat = v / (1.0 - ADAM_B2 ** ADAM_STEP)
    delta = -ADAM_LR * (m_hat / (jnp.sqrt(v_hat) + ADAM_EPS) + ADAM_WD * w)
    return delta, m, v


ADAM_ROWS = 512


def _row_tile(rows, most):
    return max(t for t in range(SUBLANES, most + 1, SUBLANES) if rows % t == 0)


def _adamw(w, g, m, v, tag):
    depth, rows, cols = w.shape
    tr = _row_tile(rows, ADAM_ROWS)

    def body(w_ref, g_ref, m_ref, v_ref, d_ref, nm_ref, nv_ref):
        d, nm, nv = _adamw_math(w_ref[...], g_ref[...], m_ref[...], v_ref[...])
        d_ref[...] = d
        nm_ref[...] = nm
        nv_ref[...] = nv

    spec = pl.BlockSpec((None, tr, cols), lambda l, i: (l, i, 0))
    return pl.pallas_call(
        body, grid=(depth, rows // tr), in_specs=[spec] * 4, out_specs=[spec] * 3,
        out_shape=[jax.ShapeDtypeStruct(w.shape, F32)] * 3, name=f"adamw_{tag}",
        compiler_params=_cp(dimension_semantics=("arbitrary", "arbitrary")),
    )(w, g, m, v)


SMALL_TILE = 384
PRECISE = ("g_final",)
COARSE = [n for n in SMALL if n not in PRECISE]


def _small_reduce_adamw(gathered, w, m, v, tag):
    rows = w.shape[0]
    tr = math.gcd(rows, SMALL_TILE)

    def body(ga_ref, w_ref, m_ref, v_ref, g_ref, d_ref, nm_ref, nv_ref):
        g = ga_ref[0].astype(F32)
        for k in range(1, N_DEV):
            g = g + ga_ref[k].astype(F32)
        g_ref[...] = g
        d, nm, nv = _adamw_math(w_ref[...], g, m_ref[...], v_ref[...])
        d_ref[...] = d
        nm_ref[...] = nm
        nv_ref[...] = nv

    spec = _row(tr, LANES)
    return pl.pallas_call(
        body, grid=(rows // tr,),
        in_specs=[pl.BlockSpec((N_DEV, tr, LANES), lambda i: (0, i, 0)), spec, spec, spec], out_specs=[spec] * 4,
        out_shape=[jax.ShapeDtypeStruct((rows, LANES), F32)] * 4, name=f"small_reduce_adamw_{tag}",
        compiler_params=_cp(dimension_semantics=("arbitrary",)),
    )(gathered, w, m, v)


def _exchange_form(n, a):
    return jnp.swapaxes(a, 1, 2) if n in TRANSPOSED else a


PACK_ROWS = 16


def _pack(vals, names, extra=None):
    parts = [vals[n].reshape(-1) for n in names] + ([] if extra is None else [extra.reshape(1)])
    flat = jnp.concatenate(parts)
    rows = -(-flat.size // (LANES * PACK_ROWS)) * PACK_ROWS
    return jnp.pad(flat, (0, rows * LANES - flat.size)).reshape(rows, LANES)


def _unpack(buf, like, names):
    flat = buf.reshape(-1)
    out, off = {}, 0
    for n in names:
        out[n] = flat[off:off + like[n].size].reshape(like[n].shape)
        off += like[n].size
    return out, flat[off:]


def kernel(x, g_mix, w_in, A_re, A_im, log_dt, B_re, B_im, C_re, C_im, D_skip, w_glu, b_glu, w_pool, pool_scale, sgu_ln_g, sgu_ln_b, w_spatial, b_spatial, w_out, g_ffn, w_gate, w_up, w_down, g_final, loss_target, m_g_mix, m_w_in, m_A_re, m_A_im, m_log_dt, m_B_re, m_B_im, m_C_re, m_C_im, m_D_skip, m_w_glu, m_b_glu, m_w_pool, m_pool_scale, m_sgu_ln_g, m_sgu_ln_b, m_w_spatial, m_b_spatial, m_w_out, m_g_ffn, m_w_gate, m_w_up, m_w_down, m_g_final, v_g_mix, v_w_in, v_A_re, v_A_im, v_log_dt, v_B_re, v_B_im, v_C_re, v_C_im, v_D_skip, v_w_glu, v_b_glu, v_w_pool, v_pool_scale, v_sgu_ln_g, v_sgu_ln_b, v_w_spatial, v_b_spatial, v_w_out, v_g_ffn, v_w_gate, v_w_up, v_w_down, v_g_final):
    loc = locals()
    w = {n: loc[n] for n in WEIGHTS}
    m = {n: loc["m_" + n] for n in WEIGHTS}
    v = {n: loc["v_" + n] for n in WEIGHTS}
    sel = jnp.stack([lax.axis_index("c"), 2 * lax.axis_index("x") + lax.axis_index("y")]).astype(jnp.int32)

    chip = sel[1]

    halves = [(l, half) for l in range(DEPTH) for half in ("mix", "ffn")]
    names = {"mix": MIX_WEIGHTS, "ffn": FFN_WEIGHTS}
    started = {}
    wx = {n: _exchange_form(n, w[n]) for n in BIG}
    chain = []
    for l, half in halves:
        lands = _place_shards([wx[n] for n in names[half]], l, sel, chain, f"l{l}_{half}")
        started[l, half] = _copies_start(f"weights_l{l}_{half}", _gather_plan, [], lands, N_REL * len(lands))
        chain = [started[l, half]["token_tile"]]
    w = dict(w, g_mix=_after(w["g_mix"], started[halves[-1]]["token"]))

    def get_big(l, half, after):
        return dict(zip(names[half], _copies_wait(started[l, half], after)))

    result = {n: None for n in BIG}
    in_flight = []

    def finish(after):
        for part, ex, ns, l, tag in in_flight:
            bufs = _add_chips(part, _copies_wait(ex, after), [result[n] for n in ns], l, sel, tag)
            for n, f in zip(ns, _share_halves(bufs, l, tag)):
                result[n] = f
        in_flight.clear()

    def on_grads(l, half, grads):
        ns = list(grads)
        tag = f"l{l}_{half}"
        finish([grads[ns[0]]])
        g4s = [grads[n].reshape(N_CHIPS, 2, grads[n].shape[0] // (2 * N_CHIPS), grads[n].shape[1]) for n in ns]
        part = _add_halves(g4s, _swap_halves(g4s, tag), sel, tag)
        slabs = [lax.empty((N_REL,) + p.shape[1:], BF16) for p in part]
        ex = _copies_start(f"grads_{tag}", _slab_plan, part, slabs, N_REL * len(part))
        in_flight.append((part, ex, ns, l, tag))
        return ex["token"]

    loss_local, dx, g = _local_step(x[0], loss_target[0], w, get_big, on_grads)

    me = 2 * chip + sel[0]
    blocks = [_pack(g, COARSE).astype(BF16), _pack(g, PRECISE, loss_local)]
    lands = [lax.dynamic_update_slice(lax.empty((N_DEV,) + b.shape, b.dtype), b[None], (me, 0, 0)) for b in blocks]
    small = _copies_start("small_grads", _everyone_plan, blocks, lands, (N_DEV - 1) * len(blocks))

    finish([small["token_tile"]])
    grads, deltas, new_m, new_v = {}, {}, {}, {}
    for n in BIG:
        outs = _adamw(wx[n], result[n], _exchange_form(n, m[n]), _exchange_form(n, v[n]), n)
        grads[n], deltas[n], new_m[n], new_v[n] = [_exchange_form(n, a) for a in (result[n], *outs)]

    gathered = _copies_wait(small, [new_v[n] for n in BIG])
    zero = jnp.zeros((), F32)
    loss = None
    for names_k, extra, block, tag in ((COARSE, None, gathered[0], "coarse"), (PRECISE, zero, gathered[1], "precise")):
        outs = _small_reduce_adamw(block, _pack(w, names_k, extra), _pack(m, names_k, extra), _pack(v, names_k, extra),
                                   tag)
        for store, buf in zip((grads, deltas, new_m, new_v), outs):
            vals, rest = _unpack(buf, w, names_k)
            store.update(vals)
            if store is grads and extra is not None:
                loss = rest[0]
    return (loss, dx[None], *[grads[n] for n in WEIGHTS], *[deltas[n] for n in WEIGHTS],
            *[new_m[n] for n in WEIGHTS], *[new_v[n] for n in WEIGHTS])
```

```python
import math

import jax
import jax.numpy as jnp
from jax import lax
from jax.experimental import pallas as pl
from jax.experimental.pallas import tpu as pltpu

F32 = jnp.float32
BF16 = jnp.bfloat16

D_MODEL = 1024
DEPTH = 2
D_SSM = 384
SSM_GROUP = 16
N_GROUPS = 24
SSM_STATE = 64
N_STATE = N_GROUPS * SSM_STATE
POOL_WINDOWS = (2, 4, 8, 16)
POOL_GROUP = 64
D_POOL = 256
MAX_WINDOW = 16
SGU_HEADS = 6
SGU_HEAD_DIM = 64
D_SGU = 384
CHUNK = 128
D_IN = D_SSM + D_POOL + 2 * D_SGU
D_FF = 2816
EPS = 1e-6

ADAM_LR = 0.001
ADAM_B1 = 0.9
ADAM_B2 = 0.999
ADAM_EPS = 1e-08
ADAM_WD = 0.01
ADAM_STEP = 10

LANES = 128
SUBLANES = 8
N_SLAB = N_STATE // LANES
VMEM_LIMIT = 56 * 1024 * 1024

TS = 512
TS_FFN = 256

WEIGHTS = ['g_mix', 'w_in', 'A_re', 'A_im', 'log_dt', 'B_re', 'B_im', 'C_re', 'C_im', 'D_skip', 'w_glu', 'b_glu',
           'w_pool', 'pool_scale', 'sgu_ln_g', 'sgu_ln_b', 'w_spatial', 'b_spatial', 'w_out', 'g_ffn', 'w_gate',
           'w_up', 'w_down', 'g_final']
BIG = ['w_in', 'w_glu', 'w_out', 'w_gate', 'w_up', 'w_down']
SMALL = [n for n in WEIGHTS if n not in BIG]
TRANSPOSED = ("w_in", "w_gate", "w_up")
N_CHIPS = 4
N_DEV = 8


def _cp(**kw):
    return pltpu.CompilerParams(vmem_limit_bytes=VMEM_LIMIT, **kw)


def _row(ts, n):
    return pl.BlockSpec((ts, n), lambda i: (i, 0))


def _const(shape):
    nd = len(shape)
    return pl.BlockSpec(shape, lambda i: (0,) * nd, pipeline_mode=pl.Buffered(1))


def _acc(shape):
    nd = len(shape)
    return pl.BlockSpec(shape, lambda i: (0,) * nd)


def _dot(a, b):
    return jnp.dot(a, b, preferred_element_type=F32)


def _dot_tn(a, b):
    return lax.dot_general(a, b, (((0,), (0,)), ((), ())), preferred_element_type=F32)


def _dot_nt(a, b):
    return lax.dot_general(a, b, (((1,), (1,)), ((), ())), preferred_element_type=F32)


_G0 = math.sqrt(2.0 / math.pi)
_G1 = 0.044715


def _gelu(x):
    return 0.5 * x * (1.0 + jnp.tanh(_G0 * (x + _G1 * x * x * x)))


def _gelu_grad(x):
    t = jnp.tanh(_G0 * (x + _G1 * x * x * x))
    return 0.5 * (1.0 + t) + 0.5 * x * (1.0 - t * t) * (_G0 * (1.0 + 3.0 * _G1 * x * x))


def _sigmoid(x):
    return 1.0 / (1.0 + jnp.exp(-x))


def _rms(x):
    r = lax.rsqrt(jnp.mean(x * x, axis=-1, keepdims=True) + EPS)
    return x * r, r


def _rms_bwd(dh, n, r, g):
    dn = dh * g
    return r * (dn - n * jnp.mean(dn * n, axis=-1, keepdims=True)), dh * n


def _colsum8(v):
    rows, n = v.shape
    return jnp.sum(v.reshape(rows // SUBLANES, SUBLANES, n), axis=0)


def _mix_in_fwd(x, g, w, tag):
    s = x.shape[0]

    def body(x_ref, g_ref, w_ref, za_ref, zb_ref, zuv_ref, h_ref):
        n, _ = _rms(x_ref[...])
        h = (n * g_ref[...]).astype(BF16)
        z = _dot_nt(h, w_ref[...])
        za_ref[...] = z[:, :D_SSM]
        zb_ref[...] = z[:, D_SSM:D_SSM + D_POOL]
        zuv_ref[...] = z[:, D_SSM + D_POOL:]
        h_ref[...] = h

    return pl.pallas_call(
        body, grid=(s // TS,),
        in_specs=[_row(TS, D_MODEL), _const((1, D_MODEL)), _const((D_IN, D_MODEL))],
        out_specs=[_row(TS, D_SSM), _row(TS, D_POOL), _row(TS, 2 * D_SGU), _row(TS, D_MODEL)],
        out_shape=[jax.ShapeDtypeStruct((s, D_SSM), F32), jax.ShapeDtypeStruct((s, D_POOL), F32),
                   jax.ShapeDtypeStruct((s, 2 * D_SGU), F32), jax.ShapeDtypeStruct((s, D_MODEL), BF16)],
        name=f"mix_in_fwd_{tag}", compiler_params=_cp(dimension_semantics=("arbitrary",)),
    )(x, g, w)


def _cmul(ar, ai, br, bi):
    return ar * br - ai * bi, ar * bi + ai * br


def _cpow(ar, ai, n):
    assert n & (n - 1) == 0
    while n > 1:
        ar, ai = _cmul(ar, ai, ar, ai)
        n //= 2
    return ar, ai


def _to_slabs(ref, v):
    for j in range(N_SLAB):
        ref[j] = v[:, LANES * j:LANES * (j + 1)]


def _from_slabs(ref):
    return jnp.concatenate([ref[j] for j in range(N_SLAB)], axis=1)


N_USLAB = D_SSM // LANES
SEG = TS // SUBLANES


def _interleave_rows(v, stage, dst):
    for j in range(N_USLAB):
        stage[j] = v[:, LANES * j:LANES * (j + 1)]

    def step(k, carry):
        dst[:, pl.ds(pl.multiple_of(k * SUBLANES, SUBLANES), SUBLANES), :] = stage[:, pl.ds(k, SUBLANES, stride=SEG), :]
        return carry

    lax.fori_loop(0, SEG, step, 0)
    return jnp.concatenate([dst[j] for j in range(N_USLAB)], axis=1)


def _deinterleave_rows(v, stage, dst):
    for j in range(N_USLAB):
        stage[j] = v[:, LANES * j:LANES * (j + 1)]

    def step(k, carry):
        dst[:, pl.ds(k, SUBLANES, stride=SEG), :] = stage[:, pl.ds(pl.multiple_of(k * SUBLANES, SUBLANES), SUBLANES), :]
        return carry

    lax.fori_loop(0, SEG, step, 0)
    return jnp.concatenate([dst[j] for j in range(N_USLAB)], axis=1)


def _scan_rows(k):
    return pl.ds(pl.multiple_of(k * SUBLANES, SUBLANES), SUBLANES)


SLABS_PER_USLAB = N_SLAB // N_USLAB
S5_IN = (N_USLAB, LANES, 2 * N_STATE // N_USLAB)
S5_OUT = (N_USLAB, 2 * N_STATE // N_USLAB, LANES)


def _lanes(v, j):
    return v[:, LANES * j:LANES * (j + 1)]


def _state_split(re_ref, im_ref, j, v):
    for q in range(SLABS_PER_USLAB):
        re_ref[SLABS_PER_USLAB * j + q] = _lanes(v, q)
        im_ref[SLABS_PER_USLAB * j + q] = _lanes(v, SLABS_PER_USLAB + q)


def _state_cat(re_ref, im_ref, j):
    idx = range(SLABS_PER_USLAB * j, SLABS_PER_USLAB * (j + 1))
    return jnp.concatenate([re_ref[q] for q in idx] + [im_ref[q] for q in idx], axis=1).astype(BF16)


def _s5_fwd(u, p, tag):
    s = u.shape[0]
    seg = SEG

    def body(u_ref, bbt_ref, ar_ref, ai_ref, ct_ref, dsk_ref, wglu_ref, bglu_ref,
             oa_ref, y_ref, hr_ref, hi_ref, sr, si, er, ei, ir, ii, cr, ci, stage, perm):
        @pl.when(pl.program_id(0) == 0)
        def _():
            cr[...] = jnp.zeros_like(cr)
            ci[...] = jnp.zeros_like(ci)

        uv = _interleave_rows(u_ref[...], stage, perm)
        ub = uv.astype(BF16)
        for j in range(N_USLAB):
            _state_split(sr, si, j, _dot(_lanes(ub, j), bbt_ref[j]))
        ar = ar_ref[...]
        ai = ai_ref[...]

        def local(k, h):
            rows = _scan_rows(k)
            hr, hi = _cmul(ar, ai, h[0], h[1])
            return hr + sr[:, rows, :], hi + si[:, rows, :]

        zero = jnp.zeros((N_SLAB, SUBLANES, LANES), F32)
        e_r, e_i = lax.fori_loop(0, seg, local, (zero, zero))
        er[...] = e_r
        ei[...] = e_i
        pr, pi = _cpow(ar[:, 0:1, :], ai[:, 0:1, :], seg)
        c_r = cr[...]
        c_i = ci[...]
        for j in range(SUBLANES):
            ir[:, j:j + 1, :] = c_r
            ii[:, j:j + 1, :] = c_i
            n_r, n_i = _cmul(pr, pi, c_r, c_i)
            c_r = n_r + er[:, j:j + 1, :]
            c_i = n_i + ei[:, j:j + 1, :]
        cr[...] = c_r
        ci[...] = c_i

        def full(k, h):
            rows = _scan_rows(k)
            hr, hi = _cmul(ar, ai, h[0], h[1])
            hr = hr + sr[:, rows, :]
            hi = hi + si[:, rows, :]
            sr[:, rows, :] = hr
            si[:, rows, :] = hi
            return hr, hi

        lax.fori_loop(0, seg, full, (ir[...], ii[...]))
        hr_ref[...] = _from_slabs(sr).astype(BF16)
        hi_ref[...] = _from_slabs(si).astype(BF16)
        y = jnp.concatenate([_dot(_state_cat(sr, si, j), ct_ref[j]) for j in range(N_USLAB)], axis=1)
        y = y + dsk_ref[...] * uv
        y_ref[...] = y
        g = _gelu(y)
        pre = _dot(g.astype(BF16), wglu_ref[...]) + bglu_ref[...]
        oa_ref[...] = _deinterleave_rows(g * _sigmoid(pre), stage, perm).astype(BF16)

    slab = (N_SLAB, SUBLANES, LANES)
    uslab = (N_USLAB, TS, LANES)
    return pl.pallas_call(
        body, grid=(s // TS,),
        in_specs=[_row(TS, D_SSM), _const(S5_IN), _const(slab), _const(slab), _const(S5_OUT), _const((1, D_SSM)),
                  _const((D_SSM, D_SSM)), _const((1, D_SSM))],
        out_specs=[_row(TS, D_SSM), _row(TS, D_SSM), _row(TS, N_STATE), _row(TS, N_STATE)],
        out_shape=[jax.ShapeDtypeStruct((s, D_SSM), BF16), jax.ShapeDtypeStruct((s, D_SSM), F32),
                   jax.ShapeDtypeStruct((s, N_STATE), BF16), jax.ShapeDtypeStruct((s, N_STATE), BF16)],
        scratch_shapes=[pltpu.VMEM((N_SLAB, TS, LANES), F32), pltpu.VMEM((N_SLAB, TS, LANES), F32),
                        pltpu.VMEM(slab, F32), pltpu.VMEM(slab, F32), pltpu.VMEM(slab, F32), pltpu.VMEM(slab, F32),
                        pltpu.VMEM((N_SLAB, 1, LANES), F32), pltpu.VMEM((N_SLAB, 1, LANES), F32),
                        pltpu.VMEM(uslab, F32), pltpu.VMEM(uslab, F32)],
        name=f"s5_fwd_{tag}", compiler_params=_cp(dimension_semantics=("arbitrary",)),
    )(u, p["bbt3"], p["a_re8"], p["a_im8"], p["ct3"], p["d_skip"], p["w_glu"], p["b_glu"])


def _pool_consts():
    w = jnp.repeat(jnp.asarray(POOL_WINDOWS, F32), POOL_GROUP)[None, :]
    return w


def _window_sum(buf, first, rows, wl, step):
    acc = buf[pl.ds(first, rows), :]
    for j in range(1, MAX_WINDOW):
        term = buf[pl.ds(first + step * j, rows), :]
        acc = acc + (term if j < min(POOL_WINDOWS) else term * (wl > j).astype(F32))
    return acc


def _pool_count(i, rows, wl, offset=0):
    t = (i * TS + offset + 1).astype(F32) + lax.broadcasted_iota(jnp.int32, (rows, 1), 0).astype(F32)
    return jnp.minimum(t, wl)


def _pool_fwd(zb, p, tag):
    s = zb.shape[0]
    hb = TS // MAX_WINDOW

    def body(u_ref, halo_ref, wl_ref, w_ref, sc_ref, ob_ref, pooled_ref, buf):
        i = pl.program_id(0)
        uv = u_ref[...]
        buf[pl.ds(0, MAX_WINDOW), :] = jnp.where(i > 0, halo_ref[...], 0.0)
        buf[pl.ds(MAX_WINDOW, TS), :] = uv
        wl = wl_ref[...]
        pooled = (_window_sum(buf, MAX_WINDOW, TS, wl, -1) / _pool_count(i, TS, wl) - uv).astype(BF16)
        pooled_ref[...] = pooled
        ob_ref[...] = (_dot(pooled, w_ref[...]) * sc_ref[...]).astype(BF16)

    return pl.pallas_call(
        body, grid=(s // TS,),
        in_specs=[_row(TS, D_POOL),
                  pl.BlockSpec((MAX_WINDOW, D_POOL), lambda i: (jnp.maximum(i * hb - 1, 0), 0)),
                  _const((1, D_POOL)), _const((D_POOL, D_POOL)), _const((1, D_POOL))],
        out_specs=[_row(TS, D_POOL), _row(TS, D_POOL)],
        out_shape=[jax.ShapeDtypeStruct((s, D_POOL), BF16), jax.ShapeDtypeStruct((s, D_POOL), BF16)],
        scratch_shapes=[pltpu.VMEM((TS + MAX_WINDOW, D_POOL), F32)],
        name=f"pool_fwd_{tag}", compiler_params=_cp(dimension_semantics=("arbitrary",)),
    )(zb, zb, _pool_consts(), p["w_pool_bd"], p["pool_scale"])


def _sgu_mix(vl, wpair_ref, lo, hi):
    rows = vl.shape[0]
    chunks = []
    for c in range(rows // CHUNK):
        vc = vl[CHUNK * c:CHUNK * (c + 1), :]
        parts = []
        for q in range(SGU_HEADS // 2):
            vq = vc[:, LANES * q:LANES * (q + 1)]
            rhs = jnp.concatenate([vq * lo, vq * hi], axis=0).astype(BF16)
            parts.append(_dot(wpair_ref[q], rhs))
        chunks.append(jnp.concatenate(parts, axis=1))
    return jnp.concatenate(chunks, axis=0)


def _sgu_front(zuv, lng, lnb):
    zu = zuv[:, :D_SGU]
    zv = zuv[:, D_SGU:]
    u = _gelu(zu)
    v = _gelu(zv)
    mu = jnp.mean(v, axis=-1, keepdims=True)
    vc = v - mu
    rs = lax.rsqrt(jnp.mean(vc * vc, axis=-1, keepdims=True) + EPS)
    vn = vc * rs
    return zu, zv, u, vn, rs, vn * lng + lnb


def _half_masks():
    lane = lax.broadcasted_iota(jnp.int32, (1, LANES), 1)
    lo = (lane < SGU_HEAD_DIM).astype(F32)
    return lo, 1.0 - lo


def _sgu_fwd(zuv, p, tag):
    s = zuv.shape[0]

    def body(z_ref, lng_ref, lnb_ref, wp_ref, bias_ref, oc_ref):
        lo, hi = _half_masks()
        _, _, u, _, _, vl = _sgu_front(z_ref[...], lng_ref[...], lnb_ref[...])
        mixed = _sgu_mix(vl, wp_ref, lo, hi) + jnp.tile(bias_ref[...], (TS // CHUNK, 1))
        oc_ref[...] = (u * mixed).astype(BF16)

    return pl.pallas_call(
        body, grid=(s // TS,),
        in_specs=[_row(TS, 2 * D_SGU), _const((1, D_SGU)), _const((1, D_SGU)),
                  _const((SGU_HEADS // 2, CHUNK, 2 * CHUNK)), _const((CHUNK, D_SGU))],
        out_specs=_row(TS, D_SGU),
        out_shape=jax.ShapeDtypeStruct((s, D_SGU), BF16),
        name=f"sgu_fwd_{tag}", compiler_params=_cp(dimension_semantics=("arbitrary",)),
    )(zuv, p["sgu_ln_g"], p["sgu_ln_b"], p["ws_pair"], p["bias_sp"])


def _blk_fwd(x0, oa, ob, oc, p, tag):
    s = x0.shape[0]
    ts = TS_FFN

    def body(x0_ref, oa_ref, ob_ref, oc_ref, wo_ref, g_ref, wg_ref, wu_ref, wd_ref,
             x1_ref, x2_ref, h2_ref, gt_ref, up_ref, ycat_ref):
        ycat = jnp.concatenate([oa_ref[...], ob_ref[...], oc_ref[...]], axis=1)
        ycat_ref[...] = ycat
        x1 = x0_ref[...] + _dot(ycat, wo_ref[...])
        x1_ref[...] = x1
        n, _ = _rms(x1)
        h2 = (n * g_ref[...]).astype(BF16)
        h2_ref[...] = h2
        gt = _dot_nt(h2, wg_ref[...])
        up = _dot_nt(h2, wu_ref[...])
        gt_ref[...] = gt.astype(BF16)
        up_ref[...] = up.astype(BF16)
        act = (gt * _sigmoid(gt) * up).astype(BF16)
        x2_ref[...] = x1 + _dot(act, wd_ref[...])

    return pl.pallas_call(
        body, grid=(s // ts,),
        in_specs=[_row(ts, D_MODEL), _row(ts, D_SSM), _row(ts, D_POOL), _row(ts, D_SGU),
                  _const((D_MODEL, D_MODEL)), _const((1, D_MODEL)), _const((D_FF, D_MODEL)),
                  _const((D_FF, D_MODEL)), _const((D_FF, D_MODEL))],
        out_specs=[_row(ts, D_MODEL), _row(ts, D_MODEL), _row(ts, D_MODEL), _row(ts, D_FF), _row(ts, D_FF),
                   _row(ts, D_MODEL)],
        out_shape=[jax.ShapeDtypeStruct((s, D_MODEL), F32), jax.ShapeDtypeStruct((s, D_MODEL), F32),
                   jax.ShapeDtypeStruct((s, D_MODEL), BF16), jax.ShapeDtypeStruct((s, D_FF), BF16),
                   jax.ShapeDtypeStruct((s, D_FF), BF16), jax.ShapeDtypeStruct((s, D_MODEL), BF16)],
        name=f"blk_fwd_{tag}", compiler_params=_cp(dimension_semantics=("arbitrary",)),
    )(x0, oa, ob, oc, p["w_out"], p["g_ffn"], p["w_gate"], p["w_up"], p["w_down"])


def _blk_bwd(dx2, x1, gt, up, p, tag):
    s = dx2.shape[0]
    ts = TS_FFN

    def body(dx2_ref, x1_ref, gt_ref, up_ref, wd_ref, wgt_ref, wut_ref, wo_ref, g_ref,
             dx1_ref, da_ref, db_ref, dc_ref, dgt_ref, dup_ref, act_ref, dg_ref):
        @pl.when(pl.program_id(0) == 0)
        def _():
            dg_ref[...] = jnp.zeros_like(dg_ref)

        dx2v = dx2_ref[...]
        dact = _dot_nt(dx2v.astype(BF16), wd_ref[...])
        gf = gt_ref[...].astype(F32)
        uf = up_ref[...].astype(F32)
        sg = _sigmoid(gf)
        sl = gf * sg
        act_ref[...] = (sl * uf).astype(BF16)
        dgt = (dact * uf * (sg * (1.0 + gf * (1.0 - sg)))).astype(BF16)
        dup = (dact * sl).astype(BF16)
        dgt_ref[...] = dgt
        dup_ref[...] = dup
        dh2 = _dot(dgt, wgt_ref[...]) + _dot(dup, wut_ref[...])
        n, r = _rms(x1_ref[...])
        dxn, dgp = _rms_bwd(dh2, n, r, g_ref[...])
        dg_ref[...] += _colsum8(dgp)
        dx1 = dx2v + dxn
        dx1_ref[...] = dx1
        dy = _dot_nt(dx1.astype(BF16), wo_ref[...])
        da_ref[...] = dy[:, :D_SSM]
        db_ref[...] = dy[:, D_SSM:D_SSM + D_POOL]
        dc_ref[...] = dy[:, D_SSM + D_POOL:]

    return pl.pallas_call(
        body, grid=(s // ts,),
        in_specs=[_row(ts, D_MODEL), _row(ts, D_MODEL), _row(ts, D_FF), _row(ts, D_FF),
                  _const((D_FF, D_MODEL)), _const((D_FF, D_MODEL)), _const((D_FF, D_MODEL)),
                  _const((D_MODEL, D_MODEL)), _const((1, D_MODEL))],
        out_specs=[_row(ts, D_MODEL), _row(ts, D_SSM), _row(ts, D_POOL), _row(ts, D_SGU), _row(ts, D_FF),
                   _row(ts, D_FF), _row(ts, D_FF), _acc((SUBLANES, D_MODEL))],
        out_shape=[jax.ShapeDtypeStruct((s, D_MODEL), F32), jax.ShapeDtypeStruct((s, D_SSM), F32),
                   jax.ShapeDtypeStruct((s, D_POOL), F32), jax.ShapeDtypeStruct((s, D_SGU), F32),
                   jax.ShapeDtypeStruct((s, D_FF), BF16), jax.ShapeDtypeStruct((s, D_FF), BF16),
                   jax.ShapeDtypeStruct((s, D_FF), BF16), jax.ShapeDtypeStruct((SUBLANES, D_MODEL), F32)],
        name=f"blk_bwd_{tag}", compiler_params=_cp(dimension_semantics=("arbitrary",)),
    )(dx2, x1, gt, up, p["w_down"], p["w_gate"], p["w_up"], p["w_out"], p["g_ffn"])


def _s5_bwd(dout, u, y, h_re, h_im, p, tag):
    s = u.shape[0]
    nt = s // TS
    seg = SEG

    def rev(n):
        return pl.BlockSpec((TS, n), lambda i: (nt - 1 - i, 0))

    def body(do_ref, u_ref, y_ref, hr_ref, hi_ref, ar_ref, ai_ref, cb_ref, bb_ref, dsk_ref,
             wglu_ref, bglu_ref,
             du_ref, dct_ref, dbb_ref, dar_ref, dai_ref, dd_ref, dwglu_ref, dbglu_ref,
             gr, gi, hsr, hsi, er, ei, jr, ji, cr, ci, stage, perm):
        @pl.when(pl.program_id(0) == 0)
        def _():
            for ref in (cr, ci, dct_ref, dbb_ref, dar_ref, dai_ref, dd_ref, dwglu_ref, dbglu_ref):
                ref[...] = jnp.zeros_like(ref)

        uv = _interleave_rows(u_ref[...], stage, perm)
        yv = y_ref[...]
        dov = _interleave_rows(do_ref[...], stage, perm)
        g = _gelu(yv)
        gb = g.astype(BF16)
        sg = _sigmoid(_dot(gb, wglu_ref[...]) + bglu_ref[...])
        dpre = dov * g * sg * (1.0 - sg)
        dpb = dpre.astype(BF16)
        dwglu_ref[...] += _dot_tn(gb, dpb)
        dbglu_ref[...] += _colsum8(dpre)
        dy = (dov * sg + _dot_nt(dpb, wglu_ref[...])) * _gelu_grad(yv)
        dd_ref[...] += _colsum8(dy * uv)
        dyb = dy.astype(BF16)
        _to_slabs(hsr, hr_ref[...].astype(F32))
        _to_slabs(hsi, hi_ref[...].astype(F32))
        for j in range(N_USLAB):
            dct_ref[j] += _dot_tn(_state_cat(hsr, hsi, j), _lanes(dyb, j))
            _state_split(gr, gi, j, _dot(_lanes(dyb, j), cb_ref[j]))
        ar = ar_ref[...]
        ai = -ai_ref[...]

        def local(k, h):
            rows = _scan_rows(seg - 1 - k)
            nr, ni = _cmul(ar, ai, h[0], h[1])
            return nr + gr[:, rows, :], ni + gi[:, rows, :]

        zero = jnp.zeros((N_SLAB, SUBLANES, LANES), F32)
        e_r, e_i = lax.fori_loop(0, seg, local, (zero, zero))
        er[...] = e_r
        ei[...] = e_i
        pr, pi = _cpow(ar[:, 0:1, :], ai[:, 0:1, :], seg)
        c_r = cr[...]
        c_i = ci[...]
        for j in range(SUBLANES - 1, -1, -1):
            jr[:, j:j + 1, :] = c_r
            ji[:, j:j + 1, :] = c_i
            n_r, n_i = _cmul(pr, pi, c_r, c_i)
            c_r = n_r + er[:, j:j + 1, :]
            c_i = n_i + ei[:, j:j + 1, :]
        cr[...] = c_r
        ci[...] = c_i

        def full(k, carry):
            g_r, g_i, a_r, a_i = carry
            rows = _scan_rows(seg - 1 - k)
            h_r = hsr[:, rows, :]
            h_i = hsi[:, rows, :]
            a_r = a_r + g_r * h_r + g_i * h_i
            a_i = a_i + g_i * h_r - g_r * h_i
            nr, ni = _cmul(ar, ai, g_r, g_i)
            nr = nr + gr[:, rows, :]
            ni = ni + gi[:, rows, :]
            gr[:, rows, :] = nr
            gi[:, rows, :] = ni
            return nr, ni, a_r, a_i

        _, _, a_r, a_i = lax.fori_loop(0, seg, full, (jr[...], ji[...], zero, zero))
        dar_ref[...] += a_r
        dai_ref[...] += a_i
        ub = uv.astype(BF16)
        dus = []
        for j in range(N_USLAB):
            gb_j = _state_cat(gr, gi, j)
            dbb_ref[j] += _dot_tn(_lanes(ub, j), gb_j)
            dus.append(_dot(gb_j, bb_ref[j]))
        du = dy * dsk_ref[...] + jnp.concatenate(dus, axis=1)
        du_ref[...] = _deinterleave_rows(du, stage, perm).astype(BF16)

    slab = (N_SLAB, SUBLANES, LANES)
    big = (N_SLAB, TS, LANES)
    uslab = (N_USLAB, TS, LANES)
    return pl.pallas_call(
        body, grid=(nt,),
        in_specs=[rev(D_SSM), rev(D_SSM), rev(D_SSM), rev(N_STATE), rev(N_STATE), _const(slab), _const(slab),
                  _const(S5_IN), _const(S5_OUT), _const((1, D_SSM)), _const((D_SSM, D_SSM)), _const((1, D_SSM))],
        out_specs=[rev(D_SSM), _acc(S5_OUT), _acc(S5_IN), _acc(slab), _acc(slab), _acc((SUBLANES, D_SSM)),
                   _acc((D_SSM, D_SSM)), _acc((SUBLANES, D_SSM))],
        out_shape=[jax.ShapeDtypeStruct((s, D_SSM), BF16), jax.ShapeDtypeStruct(S5_OUT, F32),
                   jax.ShapeDtypeStruct(S5_IN, F32), jax.ShapeDtypeStruct(slab, F32),
                   jax.ShapeDtypeStruct(slab, F32), jax.ShapeDtypeStruct((SUBLANES, D_SSM), F32),
                   jax.ShapeDtypeStruct((D_SSM, D_SSM), F32), jax.ShapeDtypeStruct((SUBLANES, D_SSM), F32)],
        scratch_shapes=[pltpu.VMEM(big, F32), pltpu.VMEM(big, F32), pltpu.VMEM(big, F32), pltpu.VMEM(big, F32),
                        pltpu.VMEM(slab, F32), pltpu.VMEM(slab, F32), pltpu.VMEM(slab, F32), pltpu.VMEM(slab, F32),
                        pltpu.VMEM((N_SLAB, 1, LANES), F32), pltpu.VMEM((N_SLAB, 1, LANES), F32),
                        pltpu.VMEM(uslab, F32), pltpu.VMEM(uslab, F32)],
        name=f"s5_bwd_{tag}", compiler_params=_cp(dimension_semantics=("arbitrary",)),
    )(dout, u, y, h_re, h_im, p["a_re8"], p["a_im8"], p["cb3"], p["bb3"], p["d_skip"], p["w_glu"], p["b_glu"])


def _pool_bwd(dout, pooled, p, tag):
    s = dout.shape[0]
    nt = s // TS
    hb = TS // MAX_WINDOW

    def halo(n):
        return pl.BlockSpec((MAX_WINDOW, n), lambda i: (jnp.minimum((i + 1) * hb, nt * hb - 1), 0))

    def body(do_ref, po_ref, doh_ref, wl_ref, w_ref, wt_ref, sc_ref, dz_ref, dw_ref, dsc_ref, buf):
        i = pl.program_id(0)

        @pl.when(i == 0)
        def _():
            dw_ref[...] = jnp.zeros_like(dw_ref)
            dsc_ref[...] = jnp.zeros_like(dsc_ref)

        wl = wl_ref[...]
        sc = sc_ref[...]
        dov = do_ref[...]
        pooled_b = po_ref[...]
        dsc_ref[...] += _colsum8(dov * _dot(pooled_b, w_ref[...]))
        dmix = (dov * sc).astype(BF16)
        dw_ref[...] += _dot_tn(pooled_b, dmix)
        dpool = _dot(dmix, wt_ref[...])
        dpool_h = _dot((doh_ref[...] * sc).astype(BF16), wt_ref[...])
        buf[pl.ds(0, TS), :] = dpool / _pool_count(i, TS, wl)
        buf[pl.ds(TS, MAX_WINDOW), :] = jnp.where(i < nt - 1, dpool_h / _pool_count(i, MAX_WINDOW, wl, TS), 0.0)
        dz_ref[...] = (_window_sum(buf, 0, TS, wl, 1) - dpool).astype(BF16)

    return pl.pallas_call(
        body, grid=(nt,),
        in_specs=[_row(TS, D_POOL), _row(TS, D_POOL), halo(D_POOL), _const((1, D_POOL)), _const((D_POOL, D_POOL)),
                  _const((D_POOL, D_POOL)), _const((1, D_POOL))],
        out_specs=[_row(TS, D_POOL), _acc((D_POOL, D_POOL)), _acc((SUBLANES, D_POOL))],
        out_shape=[jax.ShapeDtypeStruct((s, D_POOL), BF16), jax.ShapeDtypeStruct((D_POOL, D_POOL), F32),
                   jax.ShapeDtypeStruct((SUBLANES, D_POOL), F32)],
        scratch_shapes=[pltpu.VMEM((TS + MAX_WINDOW, D_POOL), F32)],
        name=f"pool_bwd_{tag}", compiler_params=_cp(dimension_semantics=("arbitrary",)),
    )(dout, pooled, dout, _pool_consts(), p["w_pool_bd"], p["w_pool_bd_t"], p["pool_scale"])


def _sgu_bwd(dout, zuv, p, tag):
    s = zuv.shape[0]

    def body(do_ref, z_ref, lng_ref, lnb_ref, wp_ref, wpt_ref, bias_ref,
             dz_ref, dws_ref, dbias_ref, dlng_ref, dlnb_ref):
        @pl.when(pl.program_id(0) == 0)
        def _():
            for ref in (dws_ref, dbias_ref, dlng_ref, dlnb_ref):
                ref[...] = jnp.zeros_like(ref)

        lo, hi = _half_masks()
        lng = lng_ref[...]
        zu, zv, u, vn, rs, vl = _sgu_front(z_ref[...], lng, lnb_ref[...])
        mixed = _sgu_mix(vl, wp_ref, lo, hi) + jnp.tile(bias_ref[...], (TS // CHUNK, 1))
        dov = do_ref[...]
        dzu = dov * mixed * _gelu_grad(zu)
        dmix = dov * u
        dbias = dbias_ref[...]
        for c in range(TS // CHUNK):
            dmc = dmix[CHUNK * c:CHUNK * (c + 1), :]
            dbias = dbias + dmc
            vlc = vl[CHUNK * c:CHUNK * (c + 1), :].astype(BF16)
            for q in range(SGU_HEADS // 2):
                dq = dmc[:, LANES * q:LANES * (q + 1)]
                vq = vlc[:, LANES * q:LANES * (q + 1)]
                dws_ref[2 * q] += _dot_nt((dq * lo).astype(BF16), vq)
                dws_ref[2 * q + 1] += _dot_nt((dq * hi).astype(BF16), vq)
        dbias_ref[...] = dbias
        dvl = _sgu_mix(dmix, wpt_ref, lo, hi)
        dlng_ref[...] += _colsum8(dvl * vn)
        dlnb_ref[...] += _colsum8(dvl)
        dvn = dvl * lng
        dv = rs * (dvn - jnp.mean(dvn, axis=-1, keepdims=True) - vn * jnp.mean(dvn * vn, axis=-1, keepdims=True))
        dz_ref[...] = jnp.concatenate([dzu, dv * _gelu_grad(zv)], axis=1).astype(BF16)

    return pl.pallas_call(
        body, grid=(s // TS,),
        in_specs=[_row(TS, D_SGU), _row(TS, 2 * D_SGU), _const((1, D_SGU)), _const((1, D_SGU)),
                  _const((SGU_HEADS // 2, CHUNK, 2 * CHUNK)), _const((SGU_HEADS // 2, CHUNK, 2 * CHUNK)),
                  _const((CHUNK, D_SGU))],
        out_specs=[_row(TS, 2 * D_SGU), _acc((SGU_HEADS, CHUNK, CHUNK)), _acc((CHUNK, D_SGU)),
                   _acc((SUBLANES, D_SGU)), _acc((SUBLANES, D_SGU))],
        out_shape=[jax.ShapeDtypeStruct((s, 2 * D_SGU), BF16), jax.ShapeDtypeStruct((SGU_HEADS, CHUNK, CHUNK), F32),
                   jax.ShapeDtypeStruct((CHUNK, D_SGU), F32), jax.ShapeDtypeStruct((SUBLANES, D_SGU), F32),
                   jax.ShapeDtypeStruct((SUBLANES, D_SGU), F32)],
        name=f"sgu_bwd_{tag}", compiler_params=_cp(dimension_semantics=("arbitrary",)),
    )(dout, zuv, p["sgu_ln_g"], p["sgu_ln_b"], p["ws_pair"], p["ws_pair_t"], p["bias_sp"])


def _mix_in_bwd(dza, dzb, dzuv, x0, dx1, p, tag):
    s = x0.shape[0]

    def body(da_ref, db_ref, dc_ref, x_ref, dx1_ref, wt_ref, g_ref, dx0_ref, dz_ref, dg_ref):
        @pl.when(pl.program_id(0) == 0)
        def _():
            dg_ref[...] = jnp.zeros_like(dg_ref)

        dz = jnp.concatenate([da_ref[...], db_ref[...], dc_ref[...]], axis=1).astype(BF16)
        dz_ref[...] = dz
        n, r = _rms(x_ref[...])
        dxn, dgp = _rms_bwd(_dot(dz, wt_ref[...]), n, r, g_ref[...])
        dg_ref[...] += _colsum8(dgp)
        dx0_ref[...] = dx1_ref[...] + dxn

    return pl.pallas_call(
        body, grid=(s // TS,),
        in_specs=[_row(TS, D_SSM), _row(TS, D_POOL), _row(TS, 2 * D_SGU), _row(TS, D_MODEL), _row(TS, D_MODEL),
                  _const((D_IN, D_MODEL)), _const((1, D_MODEL))],
        out_specs=[_row(TS, D_MODEL), _row(TS, D_IN), _acc((SUBLANES, D_MODEL))],
        out_shape=[jax.ShapeDtypeStruct((s, D_MODEL), F32), jax.ShapeDtypeStruct((s, D_IN), BF16),
                   jax.ShapeDtypeStruct((SUBLANES, D_MODEL), F32)],
        name=f"mix_in_bwd_{tag}", compiler_params=_cp(dimension_semantics=("arbitrary",)),
    )(dza, dzb, dzuv, x0, dx1, p["w_in"], p["g_mix"])


def _head(x, target, g):
    s = x.shape[0]

    def body(x_ref, t_ref, g_ref, dx_ref, loss_ref, dg_ref):
        @pl.when(pl.program_id(0) == 0)
        def _():
            loss_ref[...] = jnp.zeros_like(loss_ref)
            dg_ref[...] = jnp.zeros_like(dg_ref)

        gv = g_ref[...]
        n, r = _rms(x_ref[...])
        diff = n * gv - t_ref[...]
        loss_ref[...] += jnp.sum(diff * diff) * (0.5 / D_MODEL)
        dxn, dgp = _rms_bwd(diff * (1.0 / D_MODEL), n, r, gv)
        dg_ref[...] += _colsum8(dgp)
        dx_ref[...] = dxn

    return pl.pallas_call(
        body, grid=(s // TS,),
        in_specs=[_row(TS, D_MODEL), _row(TS, D_MODEL), _const((1, D_MODEL))],
        out_specs=[_row(TS, D_MODEL), _acc((SUBLANES, LANES)), _acc((SUBLANES, D_MODEL))],
        out_shape=[jax.ShapeDtypeStruct((s, D_MODEL), F32), jax.ShapeDtypeStruct((SUBLANES, LANES), F32),
                   jax.ShapeDtypeStruct((SUBLANES, D_MODEL), F32)],
        name="head", compiler_params=_cp(dimension_semantics=("arbitrary",)),
    )(x, target, g)


def _atb(a, b, tn, tag):
    s, ka = a.shape
    kb = b.shape[1]
    ts = ATB_ROWS
    ns = s // ts

    def body(a_ref, b_ref, o_ref):
        @pl.when(pl.program_id(1) == 0)
        def _():
            o_ref[...] = jnp.zeros_like(o_ref)

        o_ref[...] += _dot_tn(a_ref[...].astype(BF16), b_ref[...].astype(BF16))

    return pl.pallas_call(
        body, grid=(kb // tn, ns),
        in_specs=[pl.BlockSpec((ts, ka), lambda j, i: (i, 0)), pl.BlockSpec((ts, tn), lambda j, i: (i, j))],
        out_specs=pl.BlockSpec((ka, tn), lambda j, i: (0, j)),
        out_shape=jax.ShapeDtypeStruct((ka, kb), F32),
        name=f"atb_{tag}", compiler_params=_cp(dimension_semantics=("arbitrary", "arbitrary")),
    )(a, b)


def _s5_discretise(a_re, a_im, log_dt, b_re, b_im):
    dt = jnp.exp(log_dt)[:, None]
    mag = jnp.exp(a_re * dt)
    ar = mag * jnp.cos(a_im * dt)
    ai = mag * jnp.sin(a_im * dt)
    den = a_re * a_re + a_im * a_im
    f_re = ((ar - 1.0) * a_re + ai * a_im) / den
    f_im = (ai * a_re - (ar - 1.0) * a_im) / den
    bb_re = f_re[..., None] * b_re - f_im[..., None] * b_im
    bb_im = f_re[..., None] * b_im + f_im[..., None] * b_re
    return ar, ai, bb_re, bb_im


def _block_diag(blocks):
    g, r, c = blocks.shape
    eye = jnp.eye(g, dtype=blocks.dtype)
    return (blocks[:, :, None, :] * eye[:, None, :, None]).reshape(g * r, g * c)


def _block_diag_extract(m, g):
    r = m.shape[0] // g
    c = m.shape[1] // g
    eye = jnp.eye(g, dtype=m.dtype)
    return jnp.sum(m.reshape(g, r, g, c) * eye[:, None, :, None], axis=2)


GROUPS_PER_SLAB = N_GROUPS // N_USLAB


def _slab_diag(blocks):
    k = GROUPS_PER_SLAB
    _, r, c = blocks.shape
    eye = jnp.eye(k, dtype=blocks.dtype)
    spread = blocks.reshape(N_USLAB, k, r, 1, c) * eye[None, :, None, :, None]
    return spread.reshape(N_USLAB, k * r, k * c)


def _slab_diag_extract(m):
    k = GROUPS_PER_SLAB
    r, c = m.shape[1] // k, m.shape[2] // k
    eye = jnp.eye(k, dtype=m.dtype)
    return jnp.sum(m.reshape(N_USLAB, k, r, k, c) * eye[None, :, None, :, None], axis=3).reshape(N_GROUPS, r, c)


def _state_slabs(v):
    return jnp.broadcast_to(v.reshape(N_SLAB, 1, LANES), (N_SLAB, SUBLANES, LANES))


def _tril():
    return jnp.tril(jnp.ones((CHUNK, CHUNK), dtype=bool))


def _layer_params(w, l):
    row = lambda v: v.reshape(1, -1)
    t = lambda m: jnp.swapaxes(m, -1, -2)
    ar, ai, bb_re, bb_im = _s5_discretise(w["A_re"][l], w["A_im"][l], w["log_dt"][l], w["B_re"][l], w["B_im"][l])
    bbt3 = jnp.concatenate([_slab_diag(t(bb_re)), _slab_diag(t(bb_im))], axis=2).astype(BF16)
    ct3 = jnp.concatenate([_slab_diag(t(w["C_re"][l])), -_slab_diag(t(w["C_im"][l]))], axis=1).astype(BF16)
    ws = jnp.where(_tril()[None], w["w_spatial"][l], 0.0)
    pair = lambda m: jnp.stack([jnp.concatenate([m[2 * q], m[2 * q + 1]], axis=1)
                                for q in range(SGU_HEADS // 2)]).astype(BF16)
    wp = _block_diag(w["w_pool"][l]).astype(BF16)
    p = dict(
        g_mix=row(w["g_mix"][l]), g_ffn=row(w["g_ffn"][l]), d_skip=row(w["D_skip"][l]), b_glu=row(w["b_glu"][l]),
        pool_scale=row(w["pool_scale"][l]), sgu_ln_g=row(w["sgu_ln_g"][l]), sgu_ln_b=row(w["sgu_ln_b"][l]),
        a_re8=_state_slabs(ar), a_im8=_state_slabs(ai),
        bbt3=bbt3, bb3=t(bbt3), ct3=ct3, cb3=t(ct3),
        w_pool_bd=wp, w_pool_bd_t=t(wp), ws_pair=pair(ws), ws_pair_t=pair(t(ws)),
        bias_sp=jnp.repeat(t(w["b_spatial"][l]), SGU_HEAD_DIM, axis=1),
    )
    return p


MIX_WEIGHTS = ("w_in", "w_glu")
FFN_WEIGHTS = ("w_out", "w_gate", "w_up", "w_down")


def _with_big(p, mats):
    p.update(mats)


def _rows_sum(v):
    return jnp.sum(v, axis=0)


ATB_COLS = 1024
ATB_ROWS = 1024


def _after(v, token):
    return v if token is None else v + token


def _join(a, b):
    return b if a is None else a if b is None else a + b


def _layer_bwd(dx2, sv, p, w, l, tag, on_grads, tick, token):
    t = lambda m: jnp.swapaxes(m, -1, -2)
    dx1, da, db, dc, dgt, dup, act, dg_ffn = _blk_bwd(dx2, sv["x1"], sv["gt"], sv["up"],
                                                      dict(p, g_ffn=_after(p["g_ffn"], token)), tag)
    token = tick([dx1])
    token = _join(token, on_grads(l, "ffn", {
        "w_down": _atb(act, dx2, ATB_COLS, tag + "_wd"), "w_gate": _atb(dgt, sv["h2"], ATB_COLS, tag + "_wg"),
        "w_up": _atb(dup, sv["h2"], ATB_COLS, tag + "_wu"), "w_out": _atb(sv["ycat"], dx1, ATB_COLS, tag + "_wo")}))
    g = {}
    g["g_ffn"] = _rows_sum(dg_ffn)
    dza, dct3, dbbt3, dar8, dai8, dd8, dwglu, dbglu8 = _s5_bwd(
        da, sv["za"], sv["y"], sv["h_re"], sv["h_im"], dict(p, d_skip=_after(p["d_skip"], token)), tag)
    token = tick([dza])
    dzb, dwp, dsc8 = _pool_bwd(db, sv["pooled"], dict(p, pool_scale=_after(p["pool_scale"], token)), tag)
    dzuv, dws, dbias, dlng8, dlnb8 = _sgu_bwd(dc, sv["zuv"], p, tag)
    dx0, dz, dg_mix = _mix_in_bwd(dza, dzb, dzuv, sv["x0"], dx1, p, tag)
    token = on_grads(l, "mix", {"w_in": _atb(dz, sv["h1"], D_MODEL, tag + "_wi"), "w_glu": dwglu})
    g["g_mix"] = _rows_sum(dg_mix)
    g["b_glu"] = _rows_sum(dbglu8)
    g["D_skip"] = _rows_sum(dd8)
    half = N_STATE // N_USLAB
    g["C_re"] = t(_slab_diag_extract(dct3[:, :half, :]))
    g["C_im"] = -t(_slab_diag_extract(dct3[:, half:, :]))
    dar = jnp.sum(dar8, axis=1).reshape(N_GROUPS, SSM_STATE)
    dai = jnp.sum(dai8, axis=1).reshape(N_GROUPS, SSM_STATE)
    dbb_re = t(_slab_diag_extract(dbbt3[:, :, :half]))
    dbb_im = t(_slab_diag_extract(dbbt3[:, :, half:]))
    _, disc_vjp = jax.vjp(_s5_discretise, w["A_re"][l], w["A_im"][l], w["log_dt"][l], w["B_re"][l], w["B_im"][l])
    g["A_re"], g["A_im"], g["log_dt"], g["B_re"], g["B_im"] = disc_vjp((dar, dai, dbb_re, dbb_im))
    g["w_pool"] = _block_diag_extract(dwp, len(POOL_WINDOWS))
    g["pool_scale"] = _rows_sum(dsc8)
    g["sgu_ln_g"] = _rows_sum(dlng8)
    g["sgu_ln_b"] = _rows_sum(dlnb8)
    g["w_spatial"] = jnp.where(_tril()[None], dws, 0.0)
    g["b_spatial"] = t(jnp.sum(dbias.reshape(CHUNK, SGU_HEADS, SGU_HEAD_DIM), axis=-1))
    return dx0, g, token


def _local_step(x, target, w, get_big, on_grads, tick):
    params = [_layer_params(w, l) for l in range(DEPTH)]
    saved = []
    h = x
    for l in range(DEPTH):
        p, tag = params[l], f"l{l}"
        _with_big(p, get_big(l, "mix", [h]))
        za, zb, zuv, h1 = _mix_in_fwd(h, p["g_mix"], p["w_in"], tag)
        oa, y, h_re, h_im = _s5_fwd(za, p, tag)
        ob, pooled = _pool_fwd(zb, p, tag)
        oc = _sgu_fwd(zuv, p, tag)
        _with_big(p, get_big(l, "ffn", [oa, ob, oc]))
        x1, x2, h2, gt, up, ycat = _blk_fwd(h, oa, ob, oc, p, tag)
        saved.append(dict(x0=h, za=za, zuv=zuv, h1=h1, ycat=ycat, y=y, h_re=h_re, h_im=h_im, pooled=pooled, x1=x1,
                          h2=h2, gt=gt, up=up))
        h = x2
    dx, loss8, dgf8 = _head(h, target, w["g_final"].reshape(1, -1))
    grads = [None] * DEPTH
    token = None
    for l in reversed(range(DEPTH)):
        dx, grads[l], token = _layer_bwd(dx, saved[l], params[l], w, l, f"l{l}", on_grads, tick, token)
    g = {n: jnp.stack([grads[l][n] for l in range(DEPTH)]) for n in SMALL if n != "g_final"}
    g["g_final"] = _rows_sum(dgf8)
    return loss8[0, 0], dx, g


_ANY = pl.BlockSpec(memory_space=pl.ANY)
_MESH = pl.DeviceIdType.MESH


def _place():
    return lax.axis_index("x"), lax.axis_index("y"), lax.axis_index("c")


def _other_chips(x, y):
    return [(1 - x, y), (x, 1 - y), (1 - x, 1 - y)]


def _dma_sems(n):
    return pltpu.SemaphoreType.DMA((n,))


def _remote(src, dst, send_sems, recv_sems, k, to):
    return pltpu.make_async_remote_copy(src_ref=src, dst_ref=dst, send_sem=send_sems.at[k], recv_sem=recv_sems.at[k],
                                        device_id=to, device_id_type=_MESH)


_HBM = pl.BlockSpec(memory_space=pltpu.HBM)
_SEM = pl.BlockSpec(memory_space=pltpu.SEMAPHORE)
_EFFECT = pltpu.SideEffectType.DATAFLOW_SIDE_EFFECTING
N_REL = N_CHIPS - 1


def _gather_plan(x, y, c, srcs, lands):
    plan = []
    for l in lands:
        r = l.shape[0] // N_CHIPS
        rows = l.at[pl.ds((2 * x + y) * r, r)]
        plan += [(rows, rows, (cx, cy, c)) for cx, cy in _other_chips(x, y)]
    return plan


def _sibling_plan(x, y, c, srcs, lands):
    return [(s.at[:, 1 - c], l, (x, y, 1 - c)) for s, l in zip(srcs, lands)]


def _slab_plan(x, y, c, srcs, lands):
    return [(s.at[2 * cx + cy], l.at[j], (cx, cy, c))
            for s, l in zip(srcs, lands) for j, (cx, cy) in enumerate(_other_chips(x, y))]


def _plan_copies(plan, srcs, lands, send_sems, recv_sems):
    x, y, c = _place()
    return [_remote(s, d, send_sems, recv_sems, k, to) for k, (s, d, to) in enumerate(plan(x, y, c, srcs, lands))]


def _hbm(a):
    return pltpu.with_memory_space_constraint(a, pltpu.HBM)


def _everyone_plan(x, y, c, srcs, lands):
    me = 4 * x + 2 * y + c
    peers = [(x, y, 1 - c)] + [(cx, cy, cc) for cx, cy in _other_chips(x, y) for cc in (c, 1 - c)]
    return [(s, l.at[me], peer) for s, l in zip(srcs, lands) for peer in peers]


def _copies_start(name, plan, srcs, lands, ncopies):
    ns, n = len(srcs), len(srcs) + len(lands)

    def body(*refs):
        for cp in _plan_copies(plan, refs[:ns], refs[ns:n], refs[n], refs[n + 1]):
            cp.start()
        refs[-1][...] = jnp.zeros_like(refs[-1])

    ref_out = [pltpu.HBM(a.shape, a.dtype) for a in (*srcs, *lands)]
    out = pl.pallas_call(
        body, name=name, in_specs=[_HBM] * n,
        out_shape=(_dma_sems(ncopies), _dma_sems(ncopies), *ref_out, jax.ShapeDtypeStruct((SUBLANES, LANES), F32)),
        out_specs=(_SEM, _SEM, *[_HBM] * n, pl.BlockSpec(memory_space=pltpu.VMEM)),
        input_output_aliases={i: 2 + i for i in range(n)},
        compiler_params=pltpu.CompilerParams(has_side_effects=_EFFECT),
    )(*[_hbm(a) for a in (*srcs, *lands)])
    return dict(name=name, plan=plan, sems=out[:2], srcs=out[2:2 + ns], lands=out[2 + ns:2 + n], token=out[-1][0, 0],
                token_tile=out[-1])


def _copies_wait(started, after):
    ns = len(started["srcs"])
    n = ns + len(started["lands"])
    plan = started["plan"]

    def body(*refs):
        for cp in _plan_copies(plan, refs[:ns], refs[ns:n], refs[n], refs[n + 1]):
            cp.wait_send()
            cp.wait_recv()

    args = (*started["srcs"], *started["lands"])
    out = pl.pallas_call(
        body, name=started["name"] + "_wait", out_shape=[pltpu.HBM(a.shape, a.dtype) for a in args],
        in_specs=[_HBM] * n + [_SEM, _SEM] + [_ANY] * len(after), out_specs=[_HBM] * n,
        input_output_aliases={i: i for i in range(n)},
        compiler_params=pltpu.CompilerParams(has_side_effects=_EFFECT),
    )(*args, *started["sems"], *after)
    return out[:ns], out[ns:]


def _place_shards(ws, layer, sel, after, tag):
    nw = len(ws)

    def body(sel_ref, *refs):
        for i in range(nw):
            refs[nw + len(after) + i][...] = refs[i][...].astype(BF16)

    return pl.pallas_call(
        body, grid_spec=pltpu.PrefetchScalarGridSpec(
            num_scalar_prefetch=1, grid=(1,),
            in_specs=[pl.BlockSpec((None,) + a.shape[1:], lambda i, s: (layer, 0, 0)) for a in ws] + [_ANY] * len(after),
            out_specs=[pl.BlockSpec(a.shape[1:], lambda i, s: (s[1], 0)) for a in ws]),
        out_shape=[jax.ShapeDtypeStruct((N_CHIPS * a.shape[1], a.shape[2]), BF16) for a in ws],
        name=f"place_shards_{tag}", compiler_params=_cp(dimension_semantics=("arbitrary",)),
    )(sel, *ws, *after)


def _share_halves(fs, layer, tag):
    nw = len(fs)

    def body(*refs):
        ins = refs[:nw]
        send_sems, recv_sems = refs[2 * nw:]
        x, y, c = _place()

        def half(i, who):
            h = ins[i].shape[1] // 2
            return ins[i].at[layer, pl.ds(who * h, h)]

        sends = [_remote(half(i, c), half(i, c), send_sems, recv_sems, i, (x, y, 1 - c)) for i in range(nw)]
        for cp in sends:
            cp.start()
        for i in range(nw):
            sends[i].wait_send()
            _remote(half(i, c), half(i, 1 - c), send_sems, recv_sems, i, (x, y, 1 - c)).wait_recv()

    return pl.pallas_call(
        body, out_shape=[jax.ShapeDtypeStruct(f.shape, f.dtype) for f in fs], in_specs=[_ANY] * nw,
        out_specs=[_ANY] * nw, input_output_aliases={i: i for i in range(nw)},
        scratch_shapes=[_dma_sems(nw), _dma_sems(nw)], name=f"share_halves_{tag}",
    )(*fs)


def _add_halves(g4s, recvs, sel, tag):
    nw = len(g4s)

    def body(sel_ref, *refs):
        for i in range(nw):
            refs[2 * nw + i][...] = (refs[i][...] + refs[nw + i][...]).astype(BF16)

    mine = [pl.BlockSpec((None, None) + g.shape[2:], lambda k, s: (k, s[0], 0, 0)) for g in g4s]
    slab = [pl.BlockSpec((None,) + g.shape[2:], lambda k, s: (k, 0, 0)) for g in g4s]
    return pl.pallas_call(
        body, grid_spec=pltpu.PrefetchScalarGridSpec(num_scalar_prefetch=1, grid=(N_CHIPS,), in_specs=mine + slab,
                                                     out_specs=slab),
        out_shape=[jax.ShapeDtypeStruct(r.shape, BF16) for r in recvs], name=f"add_halves_{tag}",
        compiler_params=_cp(dimension_semantics=("arbitrary",)),
    )(sel, *g4s, *recvs)


def _add_chips(ps, slabs, fs, layer, sel, tag):
    nw = len(ps)
    old = [f for f in fs if f is not None]

    def body(sel_ref, *refs):
        outs = refs[2 * nw + len(old):]
        for i in range(nw):
            acc = refs[i][...].astype(F32)
            for j in range(N_REL):
                acc = acc + refs[nw + i][j].astype(F32)
            outs[i][...] = acc

    shapes = [(DEPTH, 2 * p.shape[1], p.shape[2]) for p in ps]
    in_specs = [pl.BlockSpec((None,) + p.shape[1:], lambda i, s: (s[1], 0, 0)) for p in ps]
    in_specs += [pl.BlockSpec(sl.shape, lambda i, s: (0, 0, 0)) for sl in slabs]
    in_specs += [_ANY] * len(old)
    first_old = 1 + 2 * nw
    aliases, k = {}, 0
    for i, f in enumerate(fs):
        if f is not None:
            aliases[first_old + k] = i
            k += 1
    return pl.pallas_call(
        body, grid_spec=pltpu.PrefetchScalarGridSpec(
            num_scalar_prefetch=1, grid=(1,), in_specs=in_specs,
            out_specs=[pl.BlockSpec((None,) + p.shape[1:], lambda i, s: (layer, s[0], 0)) for p in ps]),
        out_shape=[jax.ShapeDtypeStruct(sh, F32) for sh in shapes], input_output_aliases=aliases,
        name=f"add_chips_{tag}", compiler_params=_cp(dimension_semantics=("arbitrary",)),
    )(sel, *ps, *slabs, *old)


def _adamw_math(w, g, m, v):
    m = ADAM_B1 * m + (1.0 - ADAM_B1) * g
    v = ADAM_B2 * v + (1.0 - ADAM_B2) * (g * g)
    m_hat = m / (1.0 - ADAM_B1 ** ADAM_STEP)
    v_hat = v / (1.0 - ADAM_B2 ** ADAM_STEP)
    delta = -ADAM_LR * (m_hat / (jnp.sqrt(v_hat) + ADAM_EPS) + ADAM_WD * w)
    return delta, m, v


ADAM_ROWS = 512


def _row_tile(rows, most):
    return max(t for t in range(SUBLANES, most + 1, SUBLANES) if rows % t == 0)


def _adamw(w, g, m, v, tag):
    depth, rows, cols = w.shape
    tr = _row_tile(rows, ADAM_ROWS)

    def body(w_ref, g_ref, m_ref, v_ref, d_ref, nm_ref, nv_ref):
        d, nm, nv = _adamw_math(w_ref[...], g_ref[...], m_ref[...], v_ref[...])
        d_ref[...] = d
        nm_ref[...] = nm
        nv_ref[...] = nv

    spec = pl.BlockSpec((None, tr, cols), lambda l, i: (l, i, 0))
    return pl.pallas_call(
        body, grid=(depth, rows // tr), in_specs=[spec] * 4, out_specs=[spec] * 3,
        out_shape=[jax.ShapeDtypeStruct(w.shape, F32)] * 3, name=f"adamw_{tag}",
        compiler_params=_cp(dimension_semantics=("arbitrary", "arbitrary")),
    )(w, g, m, v)


SMALL_TILE = 384
PRECISE = ("g_final",)
COARSE = [n for n in SMALL if n not in PRECISE]


def _small_reduce_adamw(gathered, w, m, v, tag):
    rows = w.shape[0]
    tr = math.gcd(rows, SMALL_TILE)

    def body(ga_ref, w_ref, m_ref, v_ref, g_ref, d_ref, nm_ref, nv_ref):
        g = ga_ref[0].astype(F32)
        for k in range(1, N_DEV):
            g = g + ga_ref[k].astype(F32)
        g_ref[...] = g
        d, nm, nv = _adamw_math(w_ref[...], g, m_ref[...], v_ref[...])
        d_ref[...] = d
        nm_ref[...] = nm
        nv_ref[...] = nv

    spec = _row(tr, LANES)
    return pl.pallas_call(
        body, grid=(rows // tr,),
        in_specs=[pl.BlockSpec((N_DEV, tr, LANES), lambda i: (0, i, 0)), spec, spec, spec], out_specs=[spec] * 4,
        out_shape=[jax.ShapeDtypeStruct((rows, LANES), F32)] * 4, name=f"small_reduce_adamw_{tag}",
        compiler_params=_cp(dimension_semantics=("arbitrary",)),
    )(gathered, w, m, v)


def _exchange_form(n, a):
    return jnp.swapaxes(a, 1, 2) if n in TRANSPOSED else a


PACK_ROWS = 16


def _pack(vals, names, extra=None):
    parts = [vals[n].reshape(-1) for n in names] + ([] if extra is None else [extra.reshape(1)])
    flat = jnp.concatenate(parts)
    rows = -(-flat.size // (LANES * PACK_ROWS)) * PACK_ROWS
    return jnp.pad(flat, (0, rows * LANES - flat.size)).reshape(rows, LANES)


def _unpack(buf, like, names):
    flat = buf.reshape(-1)
    out, off = {}, 0
    for n in names:
        out[n] = flat[off:off + like[n].size].reshape(like[n].shape)
        off += like[n].size
    return out, flat[off:]


def kernel(x, g_mix, w_in, A_re, A_im, log_dt, B_re, B_im, C_re, C_im, D_skip, w_glu, b_glu, w_pool, pool_scale, sgu_ln_g, sgu_ln_b, w_spatial, b_spatial, w_out, g_ffn, w_gate, w_up, w_down, g_final, loss_target, m_g_mix, m_w_in, m_A_re, m_A_im, m_log_dt, m_B_re, m_B_im, m_C_re, m_C_im, m_D_skip, m_w_glu, m_b_glu, m_w_pool, m_pool_scale, m_sgu_ln_g, m_sgu_ln_b, m_w_spatial, m_b_spatial, m_w_out, m_g_ffn, m_w_gate, m_w_up, m_w_down, m_g_final, v_g_mix, v_w_in, v_A_re, v_A_im, v_log_dt, v_B_re, v_B_im, v_C_re, v_C_im, v_D_skip, v_w_glu, v_b_glu, v_w_pool, v_pool_scale, v_sgu_ln_g, v_sgu_ln_b, v_w_spatial, v_b_spatial, v_w_out, v_g_ffn, v_w_gate, v_w_up, v_w_down, v_g_final):
    loc = locals()
    w = {n: loc[n] for n in WEIGHTS}
    m = {n: loc["m_" + n] for n in WEIGHTS}
    v = {n: loc["v_" + n] for n in WEIGHTS}
    sel = jnp.stack([lax.axis_index("c"), 2 * lax.axis_index("x") + lax.axis_index("y")]).astype(jnp.int32)

    chip = sel[1]

    halves = [(l, half) for l in range(DEPTH) for half in ("mix", "ffn")]
    names = {"mix": MIX_WEIGHTS, "ffn": FFN_WEIGHTS}
    started = {}
    wx = {n: _exchange_form(n, w[n]) for n in BIG}
    chain = []
    for l, half in halves:
        lands = _place_shards([wx[n] for n in names[half]], l, sel, chain, f"l{l}_{half}")
        started[l, half] = _copies_start(f"weights_l{l}_{half}", _gather_plan, [], lands, N_REL * len(lands))
        chain = [started[l, half]["token_tile"]]
    w = dict(w, g_mix=_after(w["g_mix"], started[halves[-1]]["token"]))

    def get_big(l, half, after):
        return dict(zip(names[half], _copies_wait(started[l, half], after)[1]))

    result = {n: None for n in BIG}
    stage = {"swap": None, "slabs": None}

    def advance(after):
        if stage["slabs"] is not None:
            ex, ns, l, tag = stage["slabs"]
            part, slabs = _copies_wait(ex, after)
            bufs = _add_chips(part, slabs, [result[n] for n in ns], l, sel, tag)
            for n, f in zip(ns, _share_halves(bufs, l, tag)):
                result[n] = f
            stage["slabs"] = None
        if stage["swap"] is None:
            return None
        sw, ns, l, tag = stage["swap"]
        part = _add_halves(*_copies_wait(sw, after), sel, tag)
        slabs = [lax.empty((N_REL,) + p.shape[1:], BF16) for p in part]
        ex = _copies_start(f"grads_{tag}", _slab_plan, part, slabs, N_REL * len(part))
        stage["slabs"], stage["swap"] = (ex, ns, l, tag), None
        return ex["token"]

    def on_grads(l, half, grads):
        ns = list(grads)
        tag = f"l{l}_{half}"
        token = advance([grads[ns[0]]])
        g4s = [grads[n].reshape(N_CHIPS, 2, grads[n].shape[0] // (2 * N_CHIPS), grads[n].shape[1]) for n in ns]
        recvs = [lax.empty((N_CHIPS,) + g4.shape[2:], F32) for g4 in g4s]
        sw = _copies_start(f"swap_{tag}", _sibling_plan, g4s, recvs, len(g4s))
        stage["swap"] = (sw, ns, l, tag)
        return _join(token, sw["token"])

    loss_local, dx, g = _local_step(x[0], loss_target[0], w, get_big, on_grads, advance)

    me = 2 * chip + sel[0]
    blocks = [_pack(g, COARSE).astype(BF16), _pack(g, PRECISE, loss_local)]
    lands = [lax.dynamic_update_slice(lax.empty((N_DEV,) + b.shape, b.dtype), b[None], (me, 0, 0)) for b in blocks]
    small = _copies_start("small_grads", _everyone_plan, blocks, lands, (N_DEV - 1) * len(blocks))

    advance([small["token_tile"]])
    advance([])
    grads, deltas, new_m, new_v = {}, {}, {}, {}
    for n in BIG:
        outs = _adamw(wx[n], result[n], _exchange_form(n, m[n]), _exchange_form(n, v[n]), n)
        grads[n], deltas[n], new_m[n], new_v[n] = [_exchange_form(n, a) for a in (result[n], *outs)]

    _, gathered = _copies_wait(small, [new_v[n] for n in BIG])
    zero = jnp.zeros((), F32)
    loss = None
    for names_k, extra, block, tag in ((COARSE, None, gathered[0], "coarse"), (PRECISE, zero, gathered[1], "precise")):
        outs = _small_reduce_adamw(block, _pack(w, names_k, extra), _pack(m, names_k, extra), _pack(v, names_k, extra),
                                   tag)
        for store, buf in zip((grads, deltas, new_m, new_v), outs):
            vals, rest = _unpack(buf, w, names_k)
            store.update(vals)
            if store is grads and extra is not None:
                loss = rest[0]
    return (loss, dx[None], *[grads[n] for n in WEIGHTS], *[deltas[n] for n in WEIGHTS],
            *[new_m[n] for n in WEIGHTS], *[new_v[n] for n in WEIGHTS])
```

```python
import math

import jax
import jax.numpy as jnp
from jax import lax
from jax.experimental import pallas as pl
from jax.experimental.pallas import tpu as pltpu

F32 = jnp.float32
BF16 = jnp.bfloat16

D_MODEL = 1024
DEPTH = 2
D_SSM = 384
SSM_GROUP = 16
N_GROUPS = 24
SSM_STATE = 64
N_STATE = N_GROUPS * SSM_STATE
POOL_WINDOWS = (2, 4, 8, 16)
POOL_GROUP = 64
D_POOL = 256
MAX_WINDOW = 16
SGU_HEADS = 6
SGU_HEAD_DIM = 64
D_SGU = 384
CHUNK = 128
D_IN = D_SSM + D_POOL + 2 * D_SGU
D_FF = 2816
EPS = 1e-6

ADAM_LR = 0.001
ADAM_B1 = 0.9
ADAM_B2 = 0.999
ADAM_EPS = 1e-08
ADAM_WD = 0.01
ADAM_STEP = 10

LANES = 128
SUBLANES = 8
N_SLAB = N_STATE // LANES
VMEM_LIMIT = 56 * 1024 * 1024

TS = 512
TS_FFN = 256

WEIGHTS = ['g_mix', 'w_in', 'A_re', 'A_im', 'log_dt', 'B_re', 'B_im', 'C_re', 'C_im', 'D_skip', 'w_glu', 'b_glu',
           'w_pool', 'pool_scale', 'sgu_ln_g', 'sgu_ln_b', 'w_spatial', 'b_spatial', 'w_out', 'g_ffn', 'w_gate',
           'w_up', 'w_down', 'g_final']
BIG = ['w_in', 'w_glu', 'w_out', 'w_gate', 'w_up', 'w_down']
SMALL = [n for n in WEIGHTS if n not in BIG]
TRANSPOSED = ("w_in", "w_gate", "w_up")
N_CHIPS = 4
N_DEV = 8


def _cp(**kw):
    return pltpu.CompilerParams(vmem_limit_bytes=VMEM_LIMIT, **kw)


def _row(ts, n):
    return pl.BlockSpec((ts, n), lambda i: (i, 0))


def _const(shape):
    nd = len(shape)
    return pl.BlockSpec(shape, lambda i: (0,) * nd, pipeline_mode=pl.Buffered(1))


def _acc(shape):
    nd = len(shape)
    return pl.BlockSpec(shape, lambda i: (0,) * nd)


def _dot(a, b):
    return jnp.dot(a, b, preferred_element_type=F32)


def _dot_tn(a, b):
    return lax.dot_general(a, b, (((0,), (0,)), ((), ())), preferred_element_type=F32)


def _dot_nt(a, b):
    return lax.dot_general(a, b, (((1,), (1,)), ((), ())), preferred_element_type=F32)


_G0 = math.sqrt(2.0 / math.pi)
_G1 = 0.044715


def _gelu(x):
    return 0.5 * x * (1.0 + jnp.tanh(_G0 * (x + _G1 * x * x * x)))


def _gelu_grad(x):
    t = jnp.tanh(_G0 * (x + _G1 * x * x * x))
    return 0.5 * (1.0 + t) + 0.5 * x * (1.0 - t * t) * (_G0 * (1.0 + 3.0 * _G1 * x * x))


def _sigmoid(x):
    return 1.0 / (1.0 + jnp.exp(-x))


def _rms(x):
    r = lax.rsqrt(jnp.mean(x * x, axis=-1, keepdims=True) + EPS)
    return x * r, r


def _rms_bwd(dh, n, r, g):
    dn = dh * g
    return r * (dn - n * jnp.mean(dn * n, axis=-1, keepdims=True)), dh * n


def _colsum8(v):
    rows, n = v.shape
    return jnp.sum(v.reshape(rows // SUBLANES, SUBLANES, n), axis=0)


def _mix_in_fwd(x, g, w, tag):
    s = x.shape[0]

    def body(x_ref, g_ref, w_ref, za_ref, zb_ref, zuv_ref, h_ref):
        n, _ = _rms(x_ref[...])
        h = (n * g_ref[...]).astype(BF16)
        z = _dot_nt(h, w_ref[...])
        za_ref[...] = z[:, :D_SSM]
        zb_ref[...] = z[:, D_SSM:D_SSM + D_POOL]
        zuv_ref[...] = z[:, D_SSM + D_POOL:]
        h_ref[...] = h

    return pl.pallas_call(
        body, grid=(s // TS,),
        in_specs=[_row(TS, D_MODEL), _const((1, D_MODEL)), _const((D_IN, D_MODEL))],
        out_specs=[_row(TS, D_SSM), _row(TS, D_POOL), _row(TS, 2 * D_SGU), _row(TS, D_MODEL)],
        out_shape=[jax.ShapeDtypeStruct((s, D_SSM), F32), jax.ShapeDtypeStruct((s, D_POOL), F32),
                   jax.ShapeDtypeStruct((s, 2 * D_SGU), F32), jax.ShapeDtypeStruct((s, D_MODEL), BF16)],
        name=f"mix_in_fwd_{tag}", compiler_params=_cp(dimension_semantics=("arbitrary",)),
    )(x, g, w)


def _cmul(ar, ai, br, bi):
    return ar * br - ai * bi, ar * bi + ai * br


def _cpow(ar, ai, n):
    assert n & (n - 1) == 0
    while n > 1:
        ar, ai = _cmul(ar, ai, ar, ai)
        n //= 2
    return ar, ai


def _to_slabs(ref, v):
    for j in range(N_SLAB):
        ref[j] = v[:, LANES * j:LANES * (j + 1)]


def _from_slabs(ref):
    return jnp.concatenate([ref[j] for j in range(N_SLAB)], axis=1)


N_USLAB = D_SSM // LANES
SEG = TS // SUBLANES


def _interleave_rows(v, stage, dst):
    for j in range(N_USLAB):
        stage[j] = v[:, LANES * j:LANES * (j + 1)]

    def step(k, carry):
        dst[:, pl.ds(pl.multiple_of(k * SUBLANES, SUBLANES), SUBLANES), :] = stage[:, pl.ds(k, SUBLANES, stride=SEG), :]
        return carry

    lax.fori_loop(0, SEG, step, 0)
    return jnp.concatenate([dst[j] for j in range(N_USLAB)], axis=1)


def _deinterleave_rows(v, stage, dst):
    for j in range(N_USLAB):
        stage[j] = v[:, LANES * j:LANES * (j + 1)]

    def step(k, carry):
        dst[:, pl.ds(k, SUBLANES, stride=SEG), :] = stage[:, pl.ds(pl.multiple_of(k * SUBLANES, SUBLANES), SUBLANES), :]
        return carry

    lax.fori_loop(0, SEG, step, 0)
    return jnp.concatenate([dst[j] for j in range(N_USLAB)], axis=1)


def _scan_rows(k):
    return pl.ds(pl.multiple_of(k * SUBLANES, SUBLANES), SUBLANES)


SLABS_PER_USLAB = N_SLAB // N_USLAB
S5_IN = (N_USLAB, LANES, 2 * N_STATE // N_USLAB)
S5_OUT = (N_USLAB, 2 * N_STATE // N_USLAB, LANES)


def _lanes(v, j):
    return v[:, LANES * j:LANES * (j + 1)]


def _state_split(re_ref, im_ref, j, v):
    for q in range(SLABS_PER_USLAB):
        re_ref[SLABS_PER_USLAB * j + q] = _lanes(v, q)
        im_ref[SLABS_PER_USLAB * j + q] = _lanes(v, SLABS_PER_USLAB + q)


def _state_cat(re_ref, im_ref, j):
    idx = range(SLABS_PER_USLAB * j, SLABS_PER_USLAB * (j + 1))
    return jnp.concatenate([re_ref[q] for q in idx] + [im_ref[q] for q in idx], axis=1).astype(BF16)


def _s5_fwd(u, p, tag):
    s = u.shape[0]
    seg = SEG

    def body(u_ref, bbt_ref, ar_ref, ai_ref, ct_ref, dsk_ref, wglu_ref, bglu_ref,
             oa_ref, y_ref, hr_ref, hi_ref, sr, si, er, ei, ir, ii, cr, ci, stage, perm):
        @pl.when(pl.program_id(0) == 0)
        def _():
            cr[...] = jnp.zeros_like(cr)
            ci[...] = jnp.zeros_like(ci)

        uv = _interleave_rows(u_ref[...], stage, perm)
        ub = uv.astype(BF16)
        for j in range(N_USLAB):
            _state_split(sr, si, j, _dot(_lanes(ub, j), bbt_ref[j]))
        ar = ar_ref[...]
        ai = ai_ref[...]

        def local(k, h):
            rows = _scan_rows(k)
            hr, hi = _cmul(ar, ai, h[0], h[1])
            return hr + sr[:, rows, :], hi + si[:, rows, :]

        zero = jnp.zeros((N_SLAB, SUBLANES, LANES), F32)
        e_r, e_i = lax.fori_loop(0, seg, local, (zero, zero))
        er[...] = e_r
        ei[...] = e_i
        pr, pi = _cpow(ar[:, 0:1, :], ai[:, 0:1, :], seg)
        c_r = cr[...]
        c_i = ci[...]
        for j in range(SUBLANES):
            ir[:, j:j + 1, :] = c_r
            ii[:, j:j + 1, :] = c_i
            n_r, n_i = _cmul(pr, pi, c_r, c_i)
            c_r = n_r + er[:, j:j + 1, :]
            c_i = n_i + ei[:, j:j + 1, :]
        cr[...] = c_r
        ci[...] = c_i

        def full(k, h):
            rows = _scan_rows(k)
            hr, hi = _cmul(ar, ai, h[0], h[1])
            hr = hr + sr[:, rows, :]
            hi = hi + si[:, rows, :]
            sr[:, rows, :] = hr
            si[:, rows, :] = hi
            return hr, hi

        lax.fori_loop(0, seg, full, (ir[...], ii[...]))
        hr_ref[...] = _from_slabs(sr).astype(BF16)
        hi_ref[...] = _from_slabs(si).astype(BF16)
        y = jnp.concatenate([_dot(_state_cat(sr, si, j), ct_ref[j]) for j in range(N_USLAB)], axis=1)
        y = y + dsk_ref[...] * uv
        y_ref[...] = y
        g = _gelu(y)
        pre = _dot(g.astype(BF16), wglu_ref[...]) + bglu_ref[...]
        oa_ref[...] = _deinterleave_rows(g * _sigmoid(pre), stage, perm).astype(BF16)

    slab = (N_SLAB, SUBLANES, LANES)
    uslab = (N_USLAB, TS, LANES)
    return pl.pallas_call(
        body, grid=(s // TS,),
        in_specs=[_row(TS, D_SSM), _const(S5_IN), _const(slab), _const(slab), _const(S5_OUT), _const((1, D_SSM)),
                  _const((D_SSM, D_SSM)), _const((1, D_SSM))],
        out_specs=[_row(TS, D_SSM), _row(TS, D_SSM), _row(TS, N_STATE), _row(TS, N_STATE)],
        out_shape=[jax.ShapeDtypeStruct((s, D_SSM), BF16), jax.ShapeDtypeStruct((s, D_SSM), F32),
                   jax.ShapeDtypeStruct((s, N_STATE), BF16), jax.ShapeDtypeStruct((s, N_STATE), BF16)],
        scratch_shapes=[pltpu.VMEM((N_SLAB, TS, LANES), F32), pltpu.VMEM((N_SLAB, TS, LANES), F32),
                        pltpu.VMEM(slab, F32), pltpu.VMEM(slab, F32), pltpu.VMEM(slab, F32), pltpu.VMEM(slab, F32),
                        pltpu.VMEM((N_SLAB, 1, LANES), F32), pltpu.VMEM((N_SLAB, 1, LANES), F32),
                        pltpu.VMEM(uslab, F32), pltpu.VMEM(uslab, F32)],
        name=f"s5_fwd_{tag}", compiler_params=_cp(dimension_semantics=("arbitrary",)),
    )(u, p["bbt3"], p["a_re8"], p["a_im8"], p["ct3"], p["d_skip"], p["w_glu"], p["b_glu"])


def _pool_consts():
    w = jnp.repeat(jnp.asarray(POOL_WINDOWS, F32), POOL_GROUP)[None, :]
    return w


def _window_sum(buf, first, rows, wl, step):
    acc = buf[pl.ds(first, rows), :]
    for j in range(1, MAX_WINDOW):
        term = buf[pl.ds(first + step * j, rows), :]
        acc = acc + (term if j < min(POOL_WINDOWS) else term * (wl > j).astype(F32))
    return acc


def _pool_count(i, rows, wl, offset=0):
    t = (i * TS + offset + 1).astype(F32) + lax.broadcasted_iota(jnp.int32, (rows, 1), 0).astype(F32)
    return jnp.minimum(t, wl)


def _pool_fwd(zb, p, tag):
    s = zb.shape[0]
    hb = TS // MAX_WINDOW

    def body(u_ref, halo_ref, wl_ref, w_ref, sc_ref, ob_ref, pooled_ref, buf):
        i = pl.program_id(0)
        uv = u_ref[...]
        buf[pl.ds(0, MAX_WINDOW), :] = jnp.where(i > 0, halo_ref[...], 0.0)
        buf[pl.ds(MAX_WINDOW, TS), :] = uv
        wl = wl_ref[...]
        pooled = (_window_sum(buf, MAX_WINDOW, TS, wl, -1) / _pool_count(i, TS, wl) - uv).astype(BF16)
        pooled_ref[...] = pooled
        ob_ref[...] = (_dot(pooled, w_ref[...]) * sc_ref[...]).astype(BF16)

    return pl.pallas_call(
        body, grid=(s // TS,),
        in_specs=[_row(TS, D_POOL),
                  pl.BlockSpec((MAX_WINDOW, D_POOL), lambda i: (jnp.maximum(i * hb - 1, 0), 0)),
                  _const((1, D_POOL)), _const((D_POOL, D_POOL)), _const((1, D_POOL))],
        out_specs=[_row(TS, D_POOL), _row(TS, D_POOL)],
        out_shape=[jax.ShapeDtypeStruct((s, D_POOL), BF16), jax.ShapeDtypeStruct((s, D_POOL), BF16)],
        scratch_shapes=[pltpu.VMEM((TS + MAX_WINDOW, D_POOL), F32)],
        name=f"pool_fwd_{tag}", compiler_params=_cp(dimension_semantics=("arbitrary",)),
    )(zb, zb, _pool_consts(), p["w_pool_bd"], p["pool_scale"])


def _sgu_mix(vl, wpair_ref, lo, hi):
    rows = vl.shape[0]
    chunks = []
    for c in range(rows // CHUNK):
        vc = vl[CHUNK * c:CHUNK * (c + 1), :]
        parts = []
        for q in range(SGU_HEADS // 2):
            vq = vc[:, LANES * q:LANES * (q + 1)]
            rhs = jnp.concatenate([vq * lo, vq * hi], axis=0).astype(BF16)
            parts.append(_dot(wpair_ref[q], rhs))
        chunks.append(jnp.concatenate(parts, axis=1))
    return jnp.concatenate(chunks, axis=0)


def _sgu_front(zuv, lng, lnb):
    zu = zuv[:, :D_SGU]
    zv = zuv[:, D_SGU:]
    u = _gelu(zu)
    v = _gelu(zv)
    mu = jnp.mean(v, axis=-1, keepdims=True)
    vc = v - mu
    rs = lax.rsqrt(jnp.mean(vc * vc, axis=-1, keepdims=True) + EPS)
    vn = vc * rs
    return zu, zv, u, vn, rs, vn * lng + lnb


def _half_masks():
    lane = lax.broadcasted_iota(jnp.int32, (1, LANES), 1)
    lo = (lane < SGU_HEAD_DIM).astype(F32)
    return lo, 1.0 - lo


def _sgu_fwd(zuv, p, tag):
    s = zuv.shape[0]

    def body(z_ref, lng_ref, lnb_ref, wp_ref, bias_ref, oc_ref):
        lo, hi = _half_masks()
        _, _, u, _, _, vl = _sgu_front(z_ref[...], lng_ref[...], lnb_ref[...])
        mixed = _sgu_mix(vl, wp_ref, lo, hi) + jnp.tile(bias_ref[...], (TS // CHUNK, 1))
        oc_ref[...] = (u * mixed).astype(BF16)

    return pl.pallas_call(
        body, grid=(s // TS,),
        in_specs=[_row(TS, 2 * D_SGU), _const((1, D_SGU)), _const((1, D_SGU)),
                  _const((SGU_HEADS // 2, CHUNK, 2 * CHUNK)), _const((CHUNK, D_SGU))],
        out_specs=_row(TS, D_SGU),
        out_shape=jax.ShapeDtypeStruct((s, D_SGU), BF16),
        name=f"sgu_fwd_{tag}", compiler_params=_cp(dimension_semantics=("arbitrary",)),
    )(zuv, p["sgu_ln_g"], p["sgu_ln_b"], p["ws_pair"], p["bias_sp"])


def _blk_fwd(x0, oa, ob, oc, p, tag):
    s = x0.shape[0]
    ts = TS_FFN

    def body(x0_ref, oa_ref, ob_ref, oc_ref, wo_ref, g_ref, wg_ref, wu_ref, wd_ref,
             x1_ref, x2_ref, h2_ref, gt_ref, up_ref, ycat_ref):
        ycat = jnp.concatenate([oa_ref[...], ob_ref[...], oc_ref[...]], axis=1)
        ycat_ref[...] = ycat
        x1 = x0_ref[...] + _dot(ycat, wo_ref[...])
        x1_ref[...] = x1
        n, _ = _rms(x1)
        h2 = (n * g_ref[...]).astype(BF16)
        h2_ref[...] = h2
        gt = _dot_nt(h2, wg_ref[...])
        up = _dot_nt(h2, wu_ref[...])
        gt_ref[...] = gt.astype(BF16)
        up_ref[...] = up.astype(BF16)
        act = (gt * _sigmoid(gt) * up).astype(BF16)
        x2_ref[...] = x1 + _dot(act, wd_ref[...])

    return pl.pallas_call(
        body, grid=(s // ts,),
        in_specs=[_row(ts, D_MODEL), _row(ts, D_SSM), _row(ts, D_POOL), _row(ts, D_SGU),
                  _const((D_MODEL, D_MODEL)), _const((1, D_MODEL)), _const((D_FF, D_MODEL)),
                  _const((D_FF, D_MODEL)), _const((D_FF, D_MODEL))],
        out_specs=[_row(ts, D_MODEL), _row(ts, D_MODEL), _row(ts, D_MODEL), _row(ts, D_FF), _row(ts, D_FF),
                   _row(ts, D_MODEL)],
        out_shape=[jax.ShapeDtypeStruct((s, D_MODEL), F32), jax.ShapeDtypeStruct((s, D_MODEL), F32),
                   jax.ShapeDtypeStruct((s, D_MODEL), BF16), jax.ShapeDtypeStruct((s, D_FF), BF16),
                   jax.ShapeDtypeStruct((s, D_FF), BF16), jax.ShapeDtypeStruct((s, D_MODEL), BF16)],
        name=f"blk_fwd_{tag}", compiler_params=_cp(dimension_semantics=("arbitrary",)),
    )(x0, oa, ob, oc, p["w_out"], p["g_ffn"], p["w_gate"], p["w_up"], p["w_down"])


def _blk_bwd(dx2, x1, gt, up, p, tag):
    s = dx2.shape[0]
    ts = TS_FFN

    def body(dx2_ref, x1_ref, gt_ref, up_ref, wd_ref, wgt_ref, wut_ref, wo_ref, g_ref,
             dx1_ref, da_ref, db_ref, dc_ref, dgt_ref, dup_ref, act_ref, dg_ref):
        @pl.when(pl.program_id(0) == 0)
        def _():
            dg_ref[...] = jnp.zeros_like(dg_ref)

        dx2v = dx2_ref[...]
        dact = _dot_nt(dx2v.astype(BF16), wd_ref[...])
        gf = gt_ref[...].astype(F32)
        uf = up_ref[...].astype(F32)
        sg = _sigmoid(gf)
        sl = gf * sg
        act_ref[...] = (sl * uf).astype(BF16)
        dgt = (dact * uf * (sg * (1.0 + gf * (1.0 - sg)))).astype(BF16)
        dup = (dact * sl).astype(BF16)
        dgt_ref[...] = dgt
        dup_ref[...] = dup
        dh2 = _dot(dgt, wgt_ref[...]) + _dot(dup, wut_ref[...])
        n, r = _rms(x1_ref[...])
        dxn, dgp = _rms_bwd(dh2, n, r, g_ref[...])
        dg_ref[...] += _colsum8(dgp)
        dx1 = dx2v + dxn
        dx1_ref[...] = dx1
        dy = _dot_nt(dx1.astype(BF16), wo_ref[...])
        da_ref[...] = dy[:, :D_SSM]
        db_ref[...] = dy[:, D_SSM:D_SSM + D_POOL]
        dc_ref[...] = dy[:, D_SSM + D_POOL:]

    return pl.pallas_call(
        body, grid=(s // ts,),
        in_specs=[_row(ts, D_MODEL), _row(ts, D_MODEL), _row(ts, D_FF), _row(ts, D_FF),
                  _const((D_FF, D_MODEL)), _const((D_FF, D_MODEL)), _const((D_FF, D_MODEL)),
                  _const((D_MODEL, D_MODEL)), _const((1, D_MODEL))],
        out_specs=[_row(ts, D_MODEL), _row(ts, D_SSM), _row(ts, D_POOL), _row(ts, D_SGU), _row(ts, D_FF),
                   _row(ts, D_FF), _row(ts, D_FF), _acc((SUBLANES, D_MODEL))],
        out_shape=[jax.ShapeDtypeStruct((s, D_MODEL), F32), jax.ShapeDtypeStruct((s, D_SSM), F32),
                   jax.ShapeDtypeStruct((s, D_POOL), F32), jax.ShapeDtypeStruct((s, D_SGU), F32),
                   jax.ShapeDtypeStruct((s, D_FF), BF16), jax.ShapeDtypeStruct((s, D_FF), BF16),
                   jax.ShapeDtypeStruct((s, D_FF), BF16), jax.ShapeDtypeStruct((SUBLANES, D_MODEL), F32)],
        name=f"blk_bwd_{tag}", compiler_params=_cp(dimension_semantics=("arbitrary",)),
    )(dx2, x1, gt, up, p["w_down"], p["w_gate"], p["w_up"], p["w_out"], p["g_ffn"])


def _s5_bwd(dout, u, y, h_re, h_im, p, tag):
    s = u.shape[0]
    nt = s // TS
    seg = SEG

    def rev(n):
        return pl.BlockSpec((TS, n), lambda i: (nt - 1 - i, 0))

    def body(do_ref, u_ref, y_ref, hr_ref, hi_ref, ar_ref, ai_ref, cb_ref, bb_ref, dsk_ref,
             wglu_ref, bglu_ref,
             du_ref, dct_ref, dbb_ref, dar_ref, dai_ref, dd_ref, dwglu_ref, dbglu_ref,
             gr, gi, hsr, hsi, er, ei, jr, ji, cr, ci, stage, perm):
        @pl.when(pl.program_id(0) == 0)
        def _():
            for ref in (cr, ci, dct_ref, dbb_ref, dar_ref, dai_ref, dd_ref, dwglu_ref, dbglu_ref):
                ref[...] = jnp.zeros_like(ref)

        uv = _interleave_rows(u_ref[...], stage, perm)
        yv = y_ref[...]
        dov = _interleave_rows(do_ref[...], stage, perm)
        g = _gelu(yv)
        gb = g.astype(BF16)
        sg = _sigmoid(_dot(gb, wglu_ref[...]) + bglu_ref[...])
        dpre = dov * g * sg * (1.0 - sg)
        dpb = dpre.astype(BF16)
        dwglu_ref[...] += _dot_tn(gb, dpb)
        dbglu_ref[...] += _colsum8(dpre)
        dy = (dov * sg + _dot_nt(dpb, wglu_ref[...])) * _gelu_grad(yv)
        dd_ref[...] += _colsum8(dy * uv)
        dyb = dy.astype(BF16)
        _to_slabs(hsr, hr_ref[...].astype(F32))
        _to_slabs(hsi, hi_ref[...].astype(F32))
        for j in range(N_USLAB):
            dct_ref[j] += _dot_tn(_state_cat(hsr, hsi, j), _lanes(dyb, j))
            _state_split(gr, gi, j, _dot(_lanes(dyb, j), cb_ref[j]))
        ar = ar_ref[...]
        ai = -ai_ref[...]

        def local(k, h):
            rows = _scan_rows(seg - 1 - k)
            nr, ni = _cmul(ar, ai, h[0], h[1])
            return nr + gr[:, rows, :], ni + gi[:, rows, :]

        zero = jnp.zeros((N_SLAB, SUBLANES, LANES), F32)
        e_r, e_i = lax.fori_loop(0, seg, local, (zero, zero))
        er[...] = e_r
        ei[...] = e_i
        pr, pi = _cpow(ar[:, 0:1, :], ai[:, 0:1, :], seg)
        c_r = cr[...]
        c_i = ci[...]
        for j in range(SUBLANES - 1, -1, -1):
            jr[:, j:j + 1, :] = c_r
            ji[:, j:j + 1, :] = c_i
            n_r, n_i = _cmul(pr, pi, c_r, c_i)
            c_r = n_r + er[:, j:j + 1, :]
            c_i = n_i + ei[:, j:j + 1, :]
        cr[...] = c_r
        ci[...] = c_i

        def full(k, carry):
            g_r, g_i, a_r, a_i = carry
            rows = _scan_rows(seg - 1 - k)
            h_r = hsr[:, rows, :]
            h_i = hsi[:, rows, :]
            a_r = a_r + g_r * h_r + g_i * h_i
            a_i = a_i + g_i * h_r - g_r * h_i
            nr, ni = _cmul(ar, ai, g_r, g_i)
            nr = nr + gr[:, rows, :]
            ni = ni + gi[:, rows, :]
            gr[:, rows, :] = nr
            gi[:, rows, :] = ni
            return nr, ni, a_r, a_i

        _, _, a_r, a_i = lax.fori_loop(0, seg, full, (jr[...], ji[...], zero, zero))
        dar_ref[...] += a_r
        dai_ref[...] += a_i
        ub = uv.astype(BF16)
        dus = []
        for j in range(N_USLAB):
            gb_j = _state_cat(gr, gi, j)
            dbb_ref[j] += _dot_tn(_lanes(ub, j), gb_j)
            dus.append(_dot(gb_j, bb_ref[j]))
        du = dy * dsk_ref[...] + jnp.concatenate(dus, axis=1)
        du_ref[...] = _deinterleave_rows(du, stage, perm).astype(BF16)

    slab = (N_SLAB, SUBLANES, LANES)
    big = (N_SLAB, TS, LANES)
    uslab = (N_USLAB, TS, LANES)
    return pl.pallas_call(
        body, grid=(nt,),
        in_specs=[rev(D_SSM), rev(D_SSM), rev(D_SSM), rev(N_STATE), rev(N_STATE), _const(slab), _const(slab),
                  _const(S5_IN), _const(S5_OUT), _const((1, D_SSM)), _const((D_SSM, D_SSM)), _const((1, D_SSM))],
        out_specs=[rev(D_SSM), _acc(S5_OUT), _acc(S5_IN), _acc(slab), _acc(slab), _acc((SUBLANES, D_SSM)),
                   _acc((D_SSM, D_SSM)), _acc((SUBLANES, D_SSM))],
        out_shape=[jax.ShapeDtypeStruct((s, D_SSM), BF16), jax.ShapeDtypeStruct(S5_OUT, F32),
                   jax.ShapeDtypeStruct(S5_IN, F32), jax.ShapeDtypeStruct(slab, F32),
                   jax.ShapeDtypeStruct(slab, F32), jax.ShapeDtypeStruct((SUBLANES, D_SSM), F32),
                   jax.ShapeDtypeStruct((D_SSM, D_SSM), F32), jax.ShapeDtypeStruct((SUBLANES, D_SSM), F32)],
        scratch_shapes=[pltpu.VMEM(big, F32), pltpu.VMEM(big, F32), pltpu.VMEM(big, F32), pltpu.VMEM(big, F32),
                        pltpu.VMEM(slab, F32), pltpu.VMEM(slab, F32), pltpu.VMEM(slab, F32), pltpu.VMEM(slab, F32),
                        pltpu.VMEM((N_SLAB, 1, LANES), F32), pltpu.VMEM((N_SLAB, 1, LANES), F32),
                        pltpu.VMEM(uslab, F32), pltpu.VMEM(uslab, F32)],
        name=f"s5_bwd_{tag}", compiler_params=_cp(dimension_semantics=("arbitrary",)),
    )(dout, u, y, h_re, h_im, p["a_re8"], p["a_im8"], p["cb3"], p["bb3"], p["d_skip"], p["w_glu"], p["b_glu"])


def _pool_bwd(dout, pooled, p, tag):
    s = dout.shape[0]
    nt = s // TS
    hb = TS // MAX_WINDOW

    def halo(n):
        return pl.BlockSpec((MAX_WINDOW, n), lambda i: (jnp.minimum((i + 1) * hb, nt * hb - 1), 0))

    def body(do_ref, po_ref, doh_ref, wl_ref, w_ref, wt_ref, sc_ref, dz_ref, dw_ref, dsc_ref, buf):
        i = pl.program_id(0)

        @pl.when(i == 0)
        def _():
            dw_ref[...] = jnp.zeros_like(dw_ref)
            dsc_ref[...] = jnp.zeros_like(dsc_ref)

        wl = wl_ref[...]
        sc = sc_ref[...]
        dov = do_ref[...]
        pooled_b = po_ref[...]
        dsc_ref[...] += _colsum8(dov * _dot(pooled_b, w_ref[...]))
        dmix = (dov * sc).astype(BF16)
        dw_ref[...] += _dot_tn(pooled_b, dmix)
        dpool = _dot(dmix, wt_ref[...])
        dpool_h = _dot((doh_ref[...] * sc).astype(BF16), wt_ref[...])
        buf[pl.ds(0, TS), :] = dpool / _pool_count(i, TS, wl)
        buf[pl.ds(TS, MAX_WINDOW), :] = jnp.where(i < nt - 1, dpool_h / _pool_count(i, MAX_WINDOW, wl, TS), 0.0)
        dz_ref[...] = (_window_sum(buf, 0, TS, wl, 1) - dpool).astype(BF16)

    return pl.pallas_call(
        body, grid=(nt,),
        in_specs=[_row(TS, D_POOL), _row(TS, D_POOL), halo(D_POOL), _const((1, D_POOL)), _const((D_POOL, D_POOL)),
                  _const((D_POOL, D_POOL)), _const((1, D_POOL))],
        out_specs=[_row(TS, D_POOL), _acc((D_POOL, D_POOL)), _acc((SUBLANES, D_POOL))],
        out_shape=[jax.ShapeDtypeStruct((s, D_POOL), BF16), jax.ShapeDtypeStruct((D_POOL, D_POOL), F32),
                   jax.ShapeDtypeStruct((SUBLANES, D_POOL), F32)],
        scratch_shapes=[pltpu.VMEM((TS + MAX_WINDOW, D_POOL), F32)],
        name=f"pool_bwd_{tag}", compiler_params=_cp(dimension_semantics=("arbitrary",)),
    )(dout, pooled, dout, _pool_consts(), p["w_pool_bd"], p["w_pool_bd_t"], p["pool_scale"])


def _sgu_bwd(dout, zuv, p, tag):
    s = zuv.shape[0]

    def body(do_ref, z_ref, lng_ref, lnb_ref, wp_ref, wpt_ref, bias_ref,
             dz_ref, dws_ref, dbias_ref, dlng_ref, dlnb_ref):
        @pl.when(pl.program_id(0) == 0)
        def _():
            for ref in (dws_ref, dbias_ref, dlng_ref, dlnb_ref):
                ref[...] = jnp.zeros_like(ref)

        lo, hi = _half_masks()
        lng = lng_ref[...]
        zu, zv, u, vn, rs, vl = _sgu_front(z_ref[...], lng, lnb_ref[...])
        mixed = _sgu_mix(vl, wp_ref, lo, hi) + jnp.tile(bias_ref[...], (TS // CHUNK, 1))
        dov = do_ref[...]
        dzu = dov * mixed * _gelu_grad(zu)
        dmix = dov * u
        dbias = dbias_ref[...]
        for c in range(TS // CHUNK):
            dmc = dmix[CHUNK * c:CHUNK * (c + 1), :]
            dbias = dbias + dmc
            vlc = vl[CHUNK * c:CHUNK * (c + 1), :].astype(BF16)
            for q in range(SGU_HEADS // 2):
                dq = dmc[:, LANES * q:LANES * (q + 1)]
                vq = vlc[:, LANES * q:LANES * (q + 1)]
                dws_ref[2 * q] += _dot_nt((dq * lo).astype(BF16), vq)
                dws_ref[2 * q + 1] += _dot_nt((dq * hi).astype(BF16), vq)
        dbias_ref[...] = dbias
        dvl = _sgu_mix(dmix, wpt_ref, lo, hi)
        dlng_ref[...] += _colsum8(dvl * vn)
        dlnb_ref[...] += _colsum8(dvl)
        dvn = dvl * lng
        dv = rs * (dvn - jnp.mean(dvn, axis=-1, keepdims=True) - vn * jnp.mean(dvn * vn, axis=-1, keepdims=True))
        dz_ref[...] = jnp.concatenate([dzu, dv * _gelu_grad(zv)], axis=1).astype(BF16)

    return pl.pallas_call(
        body, grid=(s // TS,),
        in_specs=[_row(TS, D_SGU), _row(TS, 2 * D_SGU), _const((1, D_SGU)), _const((1, D_SGU)),
                  _const((SGU_HEADS // 2, CHUNK, 2 * CHUNK)), _const((SGU_HEADS // 2, CHUNK, 2 * CHUNK)),
                  _const((CHUNK, D_SGU))],
        out_specs=[_row(TS, 2 * D_SGU), _acc((SGU_HEADS, CHUNK, CHUNK)), _acc((CHUNK, D_SGU)),
                   _acc((SUBLANES, D_SGU)), _acc((SUBLANES, D_SGU))],
        out_shape=[jax.ShapeDtypeStruct((s, 2 * D_SGU), BF16), jax.ShapeDtypeStruct((SGU_HEADS, CHUNK, CHUNK), F32),
                   jax.ShapeDtypeStruct((CHUNK, D_SGU), F32), jax.ShapeDtypeStruct((SUBLANES, D_SGU), F32),
                   jax.ShapeDtypeStruct((SUBLANES, D_SGU), F32)],
        name=f"sgu_bwd_{tag}", compiler_params=_cp(dimension_semantics=("arbitrary",)),
    )(dout, zuv, p["sgu_ln_g"], p["sgu_ln_b"], p["ws_pair"], p["ws_pair_t"], p["bias_sp"])


def _mix_in_bwd(dza, dzb, dzuv, x0, dx1, p, tag):
    s = x0.shape[0]

    def body(da_ref, db_ref, dc_ref, x_ref, dx1_ref, wt_ref, g_ref, dx0_ref, dz_ref, dg_ref):
        @pl.when(pl.program_id(0) == 0)
        def _():
            dg_ref[...] = jnp.zeros_like(dg_ref)

        dz = jnp.concatenate([da_ref[...], db_ref[...], dc_ref[...]], axis=1).astype(BF16)
        dz_ref[...] = dz
        n, r = _rms(x_ref[...])
        dxn, dgp = _rms_bwd(_dot(dz, wt_ref[...]), n, r, g_ref[...])
        dg_ref[...] += _colsum8(dgp)
        dx0_ref[...] = dx1_ref[...] + dxn

    return pl.pallas_call(
        body, grid=(s // TS,),
        in_specs=[_row(TS, D_SSM), _row(TS, D_POOL), _row(TS, 2 * D_SGU), _row(TS, D_MODEL), _row(TS, D_MODEL),
                  _const((D_IN, D_MODEL)), _const((1, D_MODEL))],
        out_specs=[_row(TS, D_MODEL), _row(TS, D_IN), _acc((SUBLANES, D_MODEL))],
        out_shape=[jax.ShapeDtypeStruct((s, D_MODEL), F32), jax.ShapeDtypeStruct((s, D_IN), BF16),
                   jax.ShapeDtypeStruct((SUBLANES, D_MODEL), F32)],
        name=f"mix_in_bwd_{tag}", compiler_params=_cp(dimension_semantics=("arbitrary",)),
    )(dza, dzb, dzuv, x0, dx1, p["w_in"], p["g_mix"])


def _head(x, target, g):
    s = x.shape[0]

    def body(x_ref, t_ref, g_ref, dx_ref, loss_ref, dg_ref):
        @pl.when(pl.program_id(0) == 0)
        def _():
            loss_ref[...] = jnp.zeros_like(loss_ref)
            dg_ref[...] = jnp.zeros_like(dg_ref)

        gv = g_ref[...]
        n, r = _rms(x_ref[...])
        diff = n * gv - t_ref[...]
        loss_ref[...] += jnp.sum(diff * diff) * (0.5 / D_MODEL)
        dxn, dgp = _rms_bwd(diff * (1.0 / D_MODEL), n, r, gv)
        dg_ref[...] += _colsum8(dgp)
        dx_ref[...] = dxn

    return pl.pallas_call(
        body, grid=(s // TS,),
        in_specs=[_row(TS, D_MODEL), _row(TS, D_MODEL), _const((1, D_MODEL))],
        out_specs=[_row(TS, D_MODEL), _acc((SUBLANES, LANES)), _acc((SUBLANES, D_MODEL))],
        out_shape=[jax.ShapeDtypeStruct((s, D_MODEL), F32), jax.ShapeDtypeStruct((SUBLANES, LANES), F32),
                   jax.ShapeDtypeStruct((SUBLANES, D_MODEL), F32)],
        name="head", compiler_params=_cp(dimension_semantics=("arbitrary",)),
    )(x, target, g)


def _atb(a, b, tag, token=None):
    s, ka = a.shape
    kb = b.shape[1]
    ts = ATB_ROWS
    tn = min(kb, ATB_COLS)
    ns = s // ts
    after = [] if token is None else [token]

    def body(a_ref, b_ref, *rest):
        o_ref = rest[-1]

        @pl.when(pl.program_id(1) == 0)
        def _():
            o_ref[...] = jnp.zeros_like(o_ref)

        o_ref[...] += _dot_tn(a_ref[...].astype(BF16), b_ref[...].astype(BF16))

    return pl.pallas_call(
        body, grid=(kb // tn, ns),
        in_specs=[pl.BlockSpec((ts, ka), lambda j, i: (i, 0)), pl.BlockSpec((ts, tn), lambda j, i: (i, j))]
        + [pl.BlockSpec(memory_space=pl.ANY)] * len(after),
        out_specs=pl.BlockSpec((ka, tn), lambda j, i: (0, j)),
        out_shape=jax.ShapeDtypeStruct((ka, kb), F32),
        name=f"atb_{tag}", compiler_params=_cp(dimension_semantics=("arbitrary", "arbitrary")),
    )(a, b, *after)


def _s5_discretise(a_re, a_im, log_dt, b_re, b_im):
    dt = jnp.exp(log_dt)[:, None]
    mag = jnp.exp(a_re * dt)
    ar = mag * jnp.cos(a_im * dt)
    ai = mag * jnp.sin(a_im * dt)
    den = a_re * a_re + a_im * a_im
    f_re = ((ar - 1.0) * a_re + ai * a_im) / den
    f_im = (ai * a_re - (ar - 1.0) * a_im) / den
    bb_re = f_re[..., None] * b_re - f_im[..., None] * b_im
    bb_im = f_re[..., None] * b_im + f_im[..., None] * b_re
    return ar, ai, bb_re, bb_im


def _block_diag(blocks):
    g, r, c = blocks.shape
    eye = jnp.eye(g, dtype=blocks.dtype)
    return (blocks[:, :, None, :] * eye[:, None, :, None]).reshape(g * r, g * c)


def _block_diag_extract(m, g):
    r = m.shape[0] // g
    c = m.shape[1] // g
    eye = jnp.eye(g, dtype=m.dtype)
    return jnp.sum(m.reshape(g, r, g, c) * eye[:, None, :, None], axis=2)


GROUPS_PER_SLAB = N_GROUPS // N_USLAB


def _slab_diag(blocks):
    k = GROUPS_PER_SLAB
    _, r, c = blocks.shape
    eye = jnp.eye(k, dtype=blocks.dtype)
    spread = blocks.reshape(N_USLAB, k, r, 1, c) * eye[None, :, None, :, None]
    return spread.reshape(N_USLAB, k * r, k * c)


def _slab_diag_extract(m):
    k = GROUPS_PER_SLAB
    r, c = m.shape[1] // k, m.shape[2] // k
    eye = jnp.eye(k, dtype=m.dtype)
    return jnp.sum(m.reshape(N_USLAB, k, r, k, c) * eye[None, :, None, :, None], axis=3).reshape(N_GROUPS, r, c)


def _state_slabs(v):
    return jnp.broadcast_to(v.reshape(N_SLAB, 1, LANES), (N_SLAB, SUBLANES, LANES))


def _tril():
    return jnp.tril(jnp.ones((CHUNK, CHUNK), dtype=bool))


def _layer_params(w, l):
    row = lambda v: v.reshape(1, -1)
    t = lambda m: jnp.swapaxes(m, -1, -2)
    ar, ai, bb_re, bb_im = _s5_discretise(w["A_re"][l], w["A_im"][l], w["log_dt"][l], w["B_re"][l], w["B_im"][l])
    bbt3 = jnp.concatenate([_slab_diag(t(bb_re)), _slab_diag(t(bb_im))], axis=2).astype(BF16)
    ct3 = jnp.concatenate([_slab_diag(t(w["C_re"][l])), -_slab_diag(t(w["C_im"][l]))], axis=1).astype(BF16)
    ws = jnp.where(_tril()[None], w["w_spatial"][l], 0.0)
    pair = lambda m: jnp.stack([jnp.concatenate([m[2 * q], m[2 * q + 1]], axis=1)
                                for q in range(SGU_HEADS // 2)]).astype(BF16)
    wp = _block_diag(w["w_pool"][l]).astype(BF16)
    p = dict(
        g_mix=row(w["g_mix"][l]), g_ffn=row(w["g_ffn"][l]), d_skip=row(w["D_skip"][l]), b_glu=row(w["b_glu"][l]),
        pool_scale=row(w["pool_scale"][l]), sgu_ln_g=row(w["sgu_ln_g"][l]), sgu_ln_b=row(w["sgu_ln_b"][l]),
        a_re8=_state_slabs(ar), a_im8=_state_slabs(ai),
        bbt3=bbt3, bb3=t(bbt3), ct3=ct3, cb3=t(ct3),
        w_pool_bd=wp, w_pool_bd_t=t(wp), ws_pair=pair(ws), ws_pair_t=pair(t(ws)),
        bias_sp=jnp.repeat(t(w["b_spatial"][l]), SGU_HEAD_DIM, axis=1),
    )
    return p


MIX_WEIGHTS = ("w_in", "w_glu")
FFN_WEIGHTS = ("w_out", "w_gate", "w_up", "w_down")


def _with_big(p, mats):
    p.update(mats)


def _rows_sum(v):
    return jnp.sum(v, axis=0)


ATB_COLS = 1024
ATB_ROWS = 1024


def _after(v, token):
    return v if token is None else v + token[0, 0]


def _join(a, b):
    return b if a is None else a if b is None else a + b


def _layer_bwd(dx2, sv, p, w, l, tag, hooks, token):
    t = lambda m: jnp.swapaxes(m, -1, -2)
    dx1, da, db, dc, dgt, dup, act, dg_ffn = _blk_bwd(dx2, sv["x1"], sv["gt"], sv["up"],
                                                      dict(p, g_ffn=_after(p["g_ffn"], token)), tag)
    token = hooks["tick"]([dx1])
    token = _join(token, hooks["on_grads"](l, "ffn", {
        "w_down": _atb(act, dx2, tag + "_wd", token), "w_gate": _atb(dgt, sv["h2"], tag + "_wg", token),
        "w_up": _atb(dup, sv["h2"], tag + "_wu", token), "w_out": _atb(sv["ycat"], dx1, tag + "_wo", token)}))
    g = {}
    g["g_ffn"] = _rows_sum(dg_ffn)
    dza, dct3, dbbt3, dar8, dai8, dd8, dwglu, dbglu8 = _s5_bwd(
        da, sv["za"], sv["y"], sv["h_re"], sv["h_im"], dict(p, d_skip=_after(p["d_skip"], token)), tag)
    token = hooks["tick"]([dza])
    dzb, dwp, dsc8 = _pool_bwd(db, sv["pooled"], dict(p, pool_scale=_after(p["pool_scale"], token)), tag)
    dzuv, dws, dbias, dlng8, dlnb8 = _sgu_bwd(dc, sv["zuv"], p, tag)
    dx0, dz, dg_mix = _mix_in_bwd(dza, dzb, dzuv, sv["x0"], dx1, p, tag)
    g["g_mix"] = _rows_sum(dg_mix)
    g["b_glu"] = _rows_sum(dbglu8)
    g["D_skip"] = _rows_sum(dd8)
    half = N_STATE // N_USLAB
    g["C_re"] = t(_slab_diag_extract(dct3[:, :half, :]))
    g["C_im"] = -t(_slab_diag_extract(dct3[:, half:, :]))
    dar = jnp.sum(dar8, axis=1).reshape(N_GROUPS, SSM_STATE)
    dai = jnp.sum(dai8, axis=1).reshape(N_GROUPS, SSM_STATE)
    dbb_re = t(_slab_diag_extract(dbbt3[:, :, :half]))
    dbb_im = t(_slab_diag_extract(dbbt3[:, :, half:]))
    _, disc_vjp = jax.vjp(_s5_discretise, w["A_re"][l], w["A_im"][l], w["log_dt"][l], w["B_re"][l], w["B_im"][l])
    g["A_re"], g["A_im"], g["log_dt"], g["B_re"], g["B_im"] = disc_vjp((dar, dai, dbb_re, dbb_im))
    g["w_pool"] = _block_diag_extract(dwp, len(POOL_WINDOWS))
    g["pool_scale"] = _rows_sum(dsc8)
    g["sgu_ln_g"] = _rows_sum(dlng8)
    g["sgu_ln_b"] = _rows_sum(dlnb8)
    g["w_spatial"] = jnp.where(_tril()[None], dws, 0.0)
    g["b_spatial"] = t(jnp.sum(dbias.reshape(CHUNK, SGU_HEADS, SGU_HEAD_DIM), axis=-1))
    token = hooks["on_small"](l, g)
    token = hooks["on_grads"](l, "mix", {"w_in": _atb(dz, sv["h1"], tag + "_wi", token), "w_glu": dwglu})
    return dx0, token


def _local_step(x, target, w, hooks):
    params = [_layer_params(w, l) for l in range(DEPTH)]
    saved = []
    h = x
    for l in range(DEPTH):
        p, tag = params[l], f"l{l}"
        _with_big(p, hooks["get_big"](l, "mix", [h]))
        za, zb, zuv, h1 = _mix_in_fwd(h, p["g_mix"], p["w_in"], tag)
        oa, y, h_re, h_im = _s5_fwd(za, p, tag)
        ob, pooled = _pool_fwd(zb, p, tag)
        oc = _sgu_fwd(zuv, p, tag)
        _with_big(p, hooks["get_big"](l, "ffn", [oa, ob, oc]))
        x1, x2, h2, gt, up, ycat = _blk_fwd(h, oa, ob, oc, p, tag)
        saved.append(dict(x0=h, za=za, zuv=zuv, h1=h1, ycat=ycat, y=y, h_re=h_re, h_im=h_im, pooled=pooled, x1=x1,
                          h2=h2, gt=gt, up=up))
        h = x2
    dx, loss8, dgf8 = _head(h, target, w["g_final"].reshape(1, -1))
    grads = [None] * DEPTH

    def on_small(l, g_l):
        grads[l] = g_l
        if l > 0:
            return None
        g = {n: jnp.stack([grads[k][n] for k in range(DEPTH)]) for n in SMALL if n != "g_final"}
        g["g_final"] = _rows_sum(dgf8)
        return hooks["on_small"](g, loss8[0, 0])

    token = None
    for l in reversed(range(DEPTH)):
        dx, token = _layer_bwd(dx, saved[l], params[l], w, l, f"l{l}", dict(hooks, on_small=on_small), token)
    return dx


_ANY = pl.BlockSpec(memory_space=pl.ANY)
_MESH = pl.DeviceIdType.MESH


def _place():
    return lax.axis_index("x"), lax.axis_index("y"), lax.axis_index("c")


def _other_chips(x, y):
    return [(1 - x, y), (x, 1 - y), (1 - x, 1 - y)]


def _dma_sems(n):
    return pltpu.SemaphoreType.DMA((n,))


def _remote(src, dst, send_sems, recv_sems, k, to):
    return pltpu.make_async_remote_copy(src_ref=src, dst_ref=dst, send_sem=send_sems.at[k], recv_sem=recv_sems.at[k],
                                        device_id=to, device_id_type=_MESH)


_HBM = pl.BlockSpec(memory_space=pltpu.HBM)
_SEM = pl.BlockSpec(memory_space=pltpu.SEMAPHORE)
_EFFECT = pltpu.SideEffectType.DATAFLOW_SIDE_EFFECTING
N_REL = N_CHIPS - 1


def _gather_plan(x, y, c, srcs, lands):
    plan = []
    for l in lands:
        r = l.shape[0] // N_CHIPS
        rows = l.at[pl.ds((2 * x + y) * r, r)]
        plan += [(rows, rows, (cx, cy, c)) for cx, cy in _other_chips(x, y)]
    return plan


def _sibling_plan(x, y, c, srcs, lands):
    return [(s.at[:, 1 - c], l, (x, y, 1 - c)) for s, l in zip(srcs, lands)]


def _slab_plan(x, y, c, srcs, lands):
    return [(s.at[2 * cx + cy], l.at[j], (cx, cy, c))
            for s, l in zip(srcs, lands) for j, (cx, cy) in enumerate(_other_chips(x, y))]


def _plan_copies(plan, srcs, lands, send_sems, recv_sems):
    x, y, c = _place()
    return [_remote(s, d, send_sems, recv_sems, k, to) for k, (s, d, to) in enumerate(plan(x, y, c, srcs, lands))]


def _hbm(a):
    return pltpu.with_memory_space_constraint(a, pltpu.HBM)


def _everyone_plan(x, y, c, srcs, lands):
    me = 4 * x + 2 * y + c
    peers = [(x, y, 1 - c)] + [(cx, cy, cc) for cx, cy in _other_chips(x, y) for cc in (c, 1 - c)]
    return [(s, l.at[me], peer) for s, l in zip(srcs, lands) for peer in peers]


def _copies_start(name, plan, srcs, lands, ncopies):
    ns, n = len(srcs), len(srcs) + len(lands)

    def body(*refs):
        for cp in _plan_copies(plan, refs[:ns], refs[ns:n], refs[n], refs[n + 1]):
            cp.start()
        refs[-1][...] = jnp.zeros_like(refs[-1])

    ref_out = [pltpu.HBM(a.shape, a.dtype) for a in (*srcs, *lands)]
    out = pl.pallas_call(
        body, name=name, in_specs=[_HBM] * n,
        out_shape=(_dma_sems(ncopies), _dma_sems(ncopies), *ref_out, jax.ShapeDtypeStruct((SUBLANES, LANES), F32)),
        out_specs=(_SEM, _SEM, *[_HBM] * n, pl.BlockSpec(memory_space=pltpu.VMEM)),
        input_output_aliases={i: 2 + i for i in range(n)},
        compiler_params=pltpu.CompilerParams(has_side_effects=_EFFECT),
    )(*[_hbm(a) for a in (*srcs, *lands)])
    return dict(name=name, plan=plan, sems=out[:2], srcs=out[2:2 + ns], lands=out[2 + ns:2 + n], token=out[-1])


def _copies_wait(started, after):
    ns = len(started["srcs"])
    n = ns + len(started["lands"])
    plan = started["plan"]

    def body(*refs):
        for cp in _plan_copies(plan, refs[:ns], refs[ns:n], refs[n], refs[n + 1]):
            cp.wait_send()
            cp.wait_recv()

    args = (*started["srcs"], *started["lands"])
    out = pl.pallas_call(
        body, name=started["name"] + "_wait", out_shape=[pltpu.HBM(a.shape, a.dtype) for a in args],
        in_specs=[_HBM] * n + [_SEM, _SEM] + [_ANY] * len(after), out_specs=[_HBM] * n,
        input_output_aliases={i: i for i in range(n)},
        compiler_params=pltpu.CompilerParams(has_side_effects=_EFFECT),
    )(*args, *started["sems"], *after)
    return out[:ns], out[ns:]


def _place_shards(ws, layer, sel, after, tag):
    nw = len(ws)

    def body(sel_ref, *refs):
        for i in range(nw):
            refs[nw + len(after) + i][...] = refs[i][...].astype(BF16)

    return pl.pallas_call(
        body, grid_spec=pltpu.PrefetchScalarGridSpec(
            num_scalar_prefetch=1, grid=(1,),
            in_specs=[pl.BlockSpec((None,) + a.shape[1:], lambda i, s: (layer, 0, 0)) for a in ws] + [_ANY] * len(after),
            out_specs=[pl.BlockSpec(a.shape[1:], lambda i, s: (s[1], 0)) for a in ws]),
        out_shape=[jax.ShapeDtypeStruct((N_CHIPS * a.shape[1], a.shape[2]), BF16) for a in ws],
        name=f"place_shards_{tag}", compiler_params=_cp(dimension_semantics=("arbitrary",)),
    )(sel, *ws, *after)


def _share_halves(fs, layer, tag):
    nw = len(fs)

    def body(*refs):
        ins = refs[:nw]
        send_sems, recv_sems = refs[2 * nw:]
        x, y, c = _place()

        def half(i, who):
            h = ins[i].shape[1] // 2
            return ins[i].at[layer, pl.ds(who * h, h)]

        sends = [_remote(half(i, c), half(i, c), send_sems, recv_sems, i, (x, y, 1 - c)) for i in range(nw)]
        for cp in sends:
            cp.start()
        for i in range(nw):
            sends[i].wait_send()
            _remote(half(i, c), half(i, 1 - c), send_sems, recv_sems, i, (x, y, 1 - c)).wait_recv()

    return pl.pallas_call(
        body, out_shape=[jax.ShapeDtypeStruct(f.shape, f.dtype) for f in fs], in_specs=[_ANY] * nw,
        out_specs=[_ANY] * nw, input_output_aliases={i: i for i in range(nw)},
        scratch_shapes=[_dma_sems(nw), _dma_sems(nw)], name=f"share_halves_{tag}",
    )(*fs)


def _add_halves(g4s, recvs, sel, tag):
    nw = len(g4s)

    def body(sel_ref, *refs):
        for i in range(nw):
            refs[2 * nw + i][...] = (refs[i][...] + refs[nw + i][...]).astype(BF16)

    mine = [pl.BlockSpec((None, None) + g.shape[2:], lambda k, s: (k, s[0], 0, 0)) for g in g4s]
    slab = [pl.BlockSpec((None,) + g.shape[2:], lambda k, s: (k, 0, 0)) for g in g4s]
    return pl.pallas_call(
        body, grid_spec=pltpu.PrefetchScalarGridSpec(num_scalar_prefetch=1, grid=(N_CHIPS,), in_specs=mine + slab,
                                                     out_specs=slab),
        out_shape=[jax.ShapeDtypeStruct(r.shape, BF16) for r in recvs], name=f"add_halves_{tag}",
        compiler_params=_cp(dimension_semantics=("arbitrary",)),
    )(sel, *g4s, *recvs)


def _add_chips(ps, slabs, fs, layer, sel, tag):
    nw = len(ps)
    old = [f for f in fs if f is not None]

    def body(sel_ref, *refs):
        outs = refs[2 * nw + len(old):]
        for i in range(nw):
            acc = refs[i][...].astype(F32)
            for j in range(N_REL):
                acc = acc + refs[nw + i][j].astype(F32)
            outs[i][...] = acc

    shapes = [(DEPTH, 2 * p.shape[1], p.shape[2]) for p in ps]
    in_specs = [pl.BlockSpec((None,) + p.shape[1:], lambda i, s: (s[1], 0, 0)) for p in ps]
    in_specs += [pl.BlockSpec(sl.shape, lambda i, s: (0, 0, 0)) for sl in slabs]
    in_specs += [_ANY] * len(old)
    first_old = 1 + 2 * nw
    aliases, k = {}, 0
    for i, f in enumerate(fs):
        if f is not None:
            aliases[first_old + k] = i
            k += 1
    return pl.pallas_call(
        body, grid_spec=pltpu.PrefetchScalarGridSpec(
            num_scalar_prefetch=1, grid=(1,), in_specs=in_specs,
            out_specs=[pl.BlockSpec((None,) + p.shape[1:], lambda i, s: (layer, s[0], 0)) for p in ps]),
        out_shape=[jax.ShapeDtypeStruct(sh, F32) for sh in shapes], input_output_aliases=aliases,
        name=f"add_chips_{tag}", compiler_params=_cp(dimension_semantics=("arbitrary",)),
    )(sel, *ps, *slabs, *old)


def _adamw_math(w, g, m, v):
    m = ADAM_B1 * m + (1.0 - ADAM_B1) * g
    v = ADAM_B2 * v + (1.0 - ADAM_B2) * (g * g)
    m_hat = m / (1.0 - ADAM_B1 ** ADAM_STEP)
    v_hat = v / (1.0 - ADAM_B2 ** ADAM_STEP)
    delta = -ADAM_LR * (m_hat / (jnp.sqrt(v_hat) + ADAM_EPS) + ADAM_WD * w)
    return delta, m, v


ADAM_ROWS = 512


def _row_tile(rows, most):
    return max(t for t in range(SUBLANES, most + 1, SUBLANES) if rows % t == 0)


def _adamw(w, g, m, v, tag):
    depth, rows, cols = w.shape
    tr = _row_tile(rows, ADAM_ROWS)

    def body(w_ref, g_ref, m_ref, v_ref, d_ref, nm_ref, nv_ref):
        d, nm, nv = _adamw_math(w_ref[...], g_ref[...], m_ref[...], v_ref[...])
        d_ref[...] = d
        nm_ref[...] = nm
        nv_ref[...] = nv

    spec = pl.BlockSpec((None, tr, cols), lambda l, i: (l, i, 0))
    return pl.pallas_call(
        body, grid=(depth, rows // tr), in_specs=[spec] * 4, out_specs=[spec] * 3,
        out_shape=[jax.ShapeDtypeStruct(w.shape, F32)] * 3, name=f"adamw_{tag}",
        compiler_params=_cp(dimension_semantics=("arbitrary", "arbitrary")),
    )(w, g, m, v)


SMALL_TILE = 384
PRECISE = ("g_final",)
COARSE = [n for n in SMALL if n not in PRECISE]


def _small_reduce_adamw(gathered, w, m, v, tag):
    rows = w.shape[0]
    tr = math.gcd(rows, SMALL_TILE)

    def body(ga_ref, w_ref, m_ref, v_ref, g_ref, d_ref, nm_ref, nv_ref):
        g = ga_ref[0].astype(F32)
        for k in range(1, N_DEV):
            g = g + ga_ref[k].astype(F32)
        g_ref[...] = g
        d, nm, nv = _adamw_math(w_ref[...], g, m_ref[...], v_ref[...])
        d_ref[...] = d
        nm_ref[...] = nm
        nv_ref[...] = nv

    spec = _row(tr, LANES)
    return pl.pallas_call(
        body, grid=(rows // tr,),
        in_specs=[pl.BlockSpec((N_DEV, tr, LANES), lambda i: (0, i, 0)), spec, spec, spec], out_specs=[spec] * 4,
        out_shape=[jax.ShapeDtypeStruct((rows, LANES), F32)] * 4, name=f"small_reduce_adamw_{tag}",
        compiler_params=_cp(dimension_semantics=("arbitrary",)),
    )(gathered, w, m, v)


def _exchange_form(n, a):
    return jnp.swapaxes(a, 1, 2) if n in TRANSPOSED else a


PACK_ROWS = 16


def _pack(vals, names, extra=None):
    parts = [vals[n].reshape(-1) for n in names] + ([] if extra is None else [extra.reshape(1)])
    flat = jnp.concatenate(parts)
    rows = -(-flat.size // (LANES * PACK_ROWS)) * PACK_ROWS
    return jnp.pad(flat, (0, rows * LANES - flat.size)).reshape(rows, LANES)


def _unpack(buf, like, names):
    flat = buf.reshape(-1)
    out, off = {}, 0
    for n in names:
        out[n] = flat[off:off + like[n].size].reshape(like[n].shape)
        off += like[n].size
    return out, flat[off:]


def kernel(x, g_mix, w_in, A_re, A_im, log_dt, B_re, B_im, C_re, C_im, D_skip, w_glu, b_glu, w_pool, pool_scale, sgu_ln_g, sgu_ln_b, w_spatial, b_spatial, w_out, g_ffn, w_gate, w_up, w_down, g_final, loss_target, m_g_mix, m_w_in, m_A_re, m_A_im, m_log_dt, m_B_re, m_B_im, m_C_re, m_C_im, m_D_skip, m_w_glu, m_b_glu, m_w_pool, m_pool_scale, m_sgu_ln_g, m_sgu_ln_b, m_w_spatial, m_b_spatial, m_w_out, m_g_ffn, m_w_gate, m_w_up, m_w_down, m_g_final, v_g_mix, v_w_in, v_A_re, v_A_im, v_log_dt, v_B_re, v_B_im, v_C_re, v_C_im, v_D_skip, v_w_glu, v_b_glu, v_w_pool, v_pool_scale, v_sgu_ln_g, v_sgu_ln_b, v_w_spatial, v_b_spatial, v_w_out, v_g_ffn, v_w_gate, v_w_up, v_w_down, v_g_final):
    loc = locals()
    w = {n: loc[n] for n in WEIGHTS}
    m = {n: loc["m_" + n] for n in WEIGHTS}
    v = {n: loc["v_" + n] for n in WEIGHTS}
    sel = jnp.stack([lax.axis_index("c"), 2 * lax.axis_index("x") + lax.axis_index("y")]).astype(jnp.int32)

    chip = sel[1]

    halves = [(l, half) for l in range(DEPTH) for half in ("mix", "ffn")]
    names = {"mix": MIX_WEIGHTS, "ffn": FFN_WEIGHTS}
    started = {}
    wx = {n: _exchange_form(n, w[n]) for n in BIG}
    chain = []
    for l, half in halves:
        lands = _place_shards([wx[n] for n in names[half]], l, sel, chain, f"l{l}_{half}")
        started[l, half] = _copies_start(f"weights_l{l}_{half}", _gather_plan, [], lands, N_REL * len(lands))
        chain = [started[l, half]["token"]]
    w = dict(w, g_mix=_after(w["g_mix"], started[halves[-1]]["token"]))

    def get_big(l, half, after):
        return dict(zip(names[half], _copies_wait(started[l, half], after)[1]))

    result = {n: None for n in BIG}
    stage = {"swap": None, "slabs": None}

    def advance(after):
        if stage["slabs"] is not None:
            ex, ns, l, tag = stage["slabs"]
            part, slabs = _copies_wait(ex, after)
            bufs = _add_chips(part, slabs, [result[n] for n in ns], l, sel, tag)
            for n, f in zip(ns, _share_halves(bufs, l, tag)):
                result[n] = f
            stage["slabs"] = None
        if stage["swap"] is None:
            return None
        sw, ns, l, tag = stage["swap"]
        part = _add_halves(*_copies_wait(sw, after), sel, tag)
        slabs = [lax.empty((N_REL,) + p.shape[1:], BF16) for p in part]
        ex = _copies_start(f"grads_{tag}", _slab_plan, part, slabs, N_REL * len(part))
        stage["slabs"], stage["swap"] = (ex, ns, l, tag), None
        return ex["token"]

    def on_grads(l, half, grads):
        ns = list(grads)
        tag = f"l{l}_{half}"
        token = advance([grads[ns[0]]])
        g4s = [grads[n].reshape(N_CHIPS, 2, grads[n].shape[0] // (2 * N_CHIPS), grads[n].shape[1]) for n in ns]
        recvs = [lax.empty((N_CHIPS,) + g4.shape[2:], F32) for g4 in g4s]
        sw = _copies_start(f"swap_{tag}", _sibling_plan, g4s, recvs, len(g4s))
        stage["swap"] = (sw, ns, l, tag)
        return _join(token, sw["token"])

    small = {}

    def on_small(g, loss_local):
        me = 2 * chip + sel[0]
        blocks = [_pack(g, COARSE).astype(BF16), _pack(g, PRECISE, loss_local)]
        lands = [lax.dynamic_update_slice(lax.empty((N_DEV,) + b.shape, b.dtype), b[None], (me, 0, 0)) for b in blocks]
        small.update(_copies_start("small_grads", _everyone_plan, blocks, lands, (N_DEV - 1) * len(blocks)))
        return small["token"]

    dx = _local_step(x[0], loss_target[0], w, dict(get_big=get_big, on_grads=on_grads, tick=advance, on_small=on_small))
    advance([])
    advance([])
    grads, deltas, new_m, new_v = {}, {}, {}, {}
    for n in BIG:
        outs = _adamw(wx[n], result[n], _exchange_form(n, m[n]), _exchange_form(n, v[n]), n)
        grads[n], deltas[n], new_m[n], new_v[n] = [_exchange_form(n, a) for a in (result[n], *outs)]

    _, gathered = _copies_wait(small, [new_v[n] for n in BIG])
    zero = jnp.zeros((), F32)
    loss = None
    for names_k, extra, block, tag in ((COARSE, None, gathered[0], "coarse"), (PRECISE, zero, gathered[1], "precise")):
        outs = _small_reduce_adamw(block, _pack(w, names_k, extra), _pack(m, names_k, extra), _pack(v, names_k, extra),
                                   tag)
        for store, buf in zip((grads, deltas, new_m, new_v), outs):
            vals, rest = _unpack(buf, w, names_k)
            store.update(vals)
            if store is grads and extra is not None:
                loss = rest[0]
    return (loss, dx[None], *[grads[n] for n in WEIGHTS], *[deltas[n] for n in WEIGHTS],
            *[new_m[n] for n in WEIGHTS], *[new_v[n] for n in WEIGHTS])
```

```python
import math

import jax
import jax.numpy as jnp
from jax import lax
from jax.experimental import pallas as pl
from jax.experimental.pallas import tpu as pltpu

F32 = jnp.float32
BF16 = jnp.bfloat16

D_MODEL = 1024
DEPTH = 2
D_SSM = 384
SSM_GROUP = 16
N_GROUPS = 24
SSM_STATE = 64
N_STATE = N_GROUPS * SSM_STATE
POOL_WINDOWS = (2, 4, 8, 16)
POOL_GROUP = 64
D_POOL = 256
MAX_WINDOW = 16
SGU_HEADS = 6
SGU_HEAD_DIM = 64
D_SGU = 384
CHUNK = 128
D_IN = D_SSM + D_POOL + 2 * D_SGU
D_FF = 2816
EPS = 1e-6

ADAM_LR = 0.001
ADAM_B1 = 0.9
ADAM_B2 = 0.999
ADAM_EPS = 1e-08
ADAM_WD = 0.01
ADAM_STEP = 10

LANES = 128
SUBLANES = 8
N_SLAB = N_STATE // LANES
VMEM_LIMIT = 56 * 1024 * 1024

TS = 512
TS_FFN = 256

WEIGHTS = ['g_mix', 'w_in', 'A_re', 'A_im', 'log_dt', 'B_re', 'B_im', 'C_re', 'C_im', 'D_skip', 'w_glu', 'b_glu',
           'w_pool', 'pool_scale', 'sgu_ln_g', 'sgu_ln_b', 'w_spatial', 'b_spatial', 'w_out', 'g_ffn', 'w_gate',
           'w_up', 'w_down', 'g_final']
BIG = ['w_in', 'w_glu', 'w_out', 'w_gate', 'w_up', 'w_down']
SMALL = [n for n in WEIGHTS if n not in BIG]
TRANSPOSED = ("w_in", "w_gate", "w_up")
N_CHIPS = 4
N_DEV = 8


def _cp(**kw):
    return pltpu.CompilerParams(vmem_limit_bytes=VMEM_LIMIT, **kw)


def _row(ts, n):
    return pl.BlockSpec((ts, n), lambda i: (i, 0))


def _const(shape):
    nd = len(shape)
    return pl.BlockSpec(shape, lambda i: (0,) * nd, pipeline_mode=pl.Buffered(1))


def _acc(shape):
    nd = len(shape)
    return pl.BlockSpec(shape, lambda i: (0,) * nd)


def _dot(a, b):
    return jnp.dot(a, b, preferred_element_type=F32)


def _dot_tn(a, b):
    return lax.dot_general(a, b, (((0,), (0,)), ((), ())), preferred_element_type=F32)


def _dot_nt(a, b):
    return lax.dot_general(a, b, (((1,), (1,)), ((), ())), preferred_element_type=F32)


_G0 = math.sqrt(2.0 / math.pi)
_G1 = 0.044715


def _gelu(x):
    return 0.5 * x * (1.0 + jnp.tanh(_G0 * (x + _G1 * x * x * x)))


def _gelu_grad(x):
    t = jnp.tanh(_G0 * (x + _G1 * x * x * x))
    return 0.5 * (1.0 + t) + 0.5 * x * (1.0 - t * t) * (_G0 * (1.0 + 3.0 * _G1 * x * x))


def _sigmoid(x):
    return 1.0 / (1.0 + jnp.exp(-x))


def _rms(x):
    r = lax.rsqrt(jnp.mean(x * x, axis=-1, keepdims=True) + EPS)
    return x * r, r


def _rms_bwd(dh, n, r, g):
    dn = dh * g
    return r * (dn - n * jnp.mean(dn * n, axis=-1, keepdims=True)), dh * n


def _colsum8(v):
    rows, n = v.shape
    return jnp.sum(v.reshape(rows // SUBLANES, SUBLANES, n), axis=0)


def _mix_in_fwd(x, g, w, tag):
    s = x.shape[0]

    def body(x_ref, g_ref, w_ref, za_ref, zb_ref, zuv_ref, h_ref):
        n, _ = _rms(x_ref[...])
        h = (n * g_ref[...]).astype(BF16)
        z = _dot_nt(h, w_ref[...])
        za_ref[...] = z[:, :D_SSM]
        zb_ref[...] = z[:, D_SSM:D_SSM + D_POOL]
        zuv_ref[...] = z[:, D_SSM + D_POOL:]
        h_ref[...] = h

    return pl.pallas_call(
        body, grid=(s // TS,),
        in_specs=[_row(TS, D_MODEL), _const((1, D_MODEL)), _const((D_IN, D_MODEL))],
        out_specs=[_row(TS, D_SSM), _row(TS, D_POOL), _row(TS, 2 * D_SGU), _row(TS, D_MODEL)],
        out_shape=[jax.ShapeDtypeStruct((s, D_SSM), F32), jax.ShapeDtypeStruct((s, D_POOL), F32),
                   jax.ShapeDtypeStruct((s, 2 * D_SGU), F32), jax.ShapeDtypeStruct((s, D_MODEL), BF16)],
        name=f"mix_in_fwd_{tag}", compiler_params=_cp(dimension_semantics=("arbitrary",)),
    )(x, g, w)


def _cmul(ar, ai, br, bi):
    return ar * br - ai * bi, ar * bi + ai * br


def _cpow(ar, ai, n):
    assert n & (n - 1) == 0
    while n > 1:
        ar, ai = _cmul(ar, ai, ar, ai)
        n //= 2
    return ar, ai


N_USLAB = D_SSM // LANES
SEG = TS // SUBLANES
SLAB_STATES = N_STATE // N_USLAB
S5_IN = (N_USLAB, LANES, 2 * SLAB_STATES)
S5_OUT = (N_USLAB, 2 * SLAB_STATES, LANES)
STATE_TILE = (SUBLANES, N_STATE)


def _scan_order():
    p = jnp.arange(TS)
    src = (p % SUBLANES) * SEG + p // SUBLANES
    return (src[:, None] == jnp.arange(TS)[None, :]).astype(BF16)


def _to_scan_order(perm, v):
    hi = v.astype(BF16)
    lo = (v - hi.astype(F32)).astype(BF16)
    return _dot(perm, hi) + _dot(perm, lo)


def _scan_rows(k):
    return pl.ds(pl.multiple_of(k * SUBLANES, SUBLANES), SUBLANES)


def _lanes(v, j):
    return v[:, LANES * j:LANES * (j + 1)]


def _states(j):
    return pl.ds(SLAB_STATES * j, SLAB_STATES)


def _state_split(re_ref, im_ref, j, v):
    re_ref[:, _states(j)] = v[:, :SLAB_STATES]
    im_ref[:, _states(j)] = v[:, SLAB_STATES:]


def _state_cat(re_ref, im_ref, j):
    return jnp.concatenate([re_ref[:, _states(j)], im_ref[:, _states(j)]], axis=1).astype(BF16)


def _s5_fwd(u, p, tag):
    s = u.shape[0]
    seg = SEG

    def body(u_ref, perm_ref, bbt_ref, ar_ref, ai_ref, ct_ref, dsk_ref, wglu_ref, bglu_ref,
             oa_ref, y_ref, hr_ref, hi_ref, sr, si, er, ei, ir, ii, cr, ci):
        @pl.when(pl.program_id(0) == 0)
        def _():
            cr[...] = jnp.zeros_like(cr)
            ci[...] = jnp.zeros_like(ci)

        perm = perm_ref[...]
        uv = _to_scan_order(perm, u_ref[...])
        ub = uv.astype(BF16)
        for j in range(N_USLAB):
            _state_split(sr, si, j, _dot(_lanes(ub, j), bbt_ref[j]))
        ar = ar_ref[...]
        ai = ai_ref[...]

        def local(k, h):
            rows = _scan_rows(k)
            hr, hi = _cmul(ar, ai, h[0], h[1])
            return hr + sr[rows, :], hi + si[rows, :]

        zero = jnp.zeros(STATE_TILE, F32)
        e_r, e_i = lax.fori_loop(0, seg, local, (zero, zero))
        er[...] = e_r
        ei[...] = e_i
        pr, pi = _cpow(ar[0:1, :], ai[0:1, :], seg)
        c_r = cr[...]
        c_i = ci[...]
        for j in range(SUBLANES):
            ir[j:j + 1, :] = c_r
            ii[j:j + 1, :] = c_i
            n_r, n_i = _cmul(pr, pi, c_r, c_i)
            c_r = n_r + er[j:j + 1, :]
            c_i = n_i + ei[j:j + 1, :]
        cr[...] = c_r
        ci[...] = c_i

        def full(k, h):
            rows = _scan_rows(k)
            hr, hi = _cmul(ar, ai, h[0], h[1])
            hr = hr + sr[rows, :]
            hi = hi + si[rows, :]
            sr[rows, :] = hr
            si[rows, :] = hi
            return hr, hi

        lax.fori_loop(0, seg, full, (ir[...], ii[...]))
        hr_ref[...] = sr[...].astype(BF16)
        hi_ref[...] = si[...].astype(BF16)
        y = jnp.concatenate([_dot(_state_cat(sr, si, j), ct_ref[j]) for j in range(N_USLAB)], axis=1)
        y = y + dsk_ref[...] * uv
        y_ref[...] = y
        g = _gelu(y)
        pre = _dot(g.astype(BF16), wglu_ref[...]) + bglu_ref[...]
        oa_ref[...] = _dot_tn(perm, (g * _sigmoid(pre)).astype(BF16)).astype(BF16)

    return pl.pallas_call(
        body, grid=(s // TS,),
        in_specs=[_row(TS, D_SSM), _const((TS, TS)), _const(S5_IN), _const(STATE_TILE), _const(STATE_TILE),
                  _const(S5_OUT), _const((1, D_SSM)), _const((D_SSM, D_SSM)), _const((1, D_SSM))],
        out_specs=[_row(TS, D_SSM), _row(TS, D_SSM), _row(TS, N_STATE), _row(TS, N_STATE)],
        out_shape=[jax.ShapeDtypeStruct((s, D_SSM), BF16), jax.ShapeDtypeStruct((s, D_SSM), F32),
                   jax.ShapeDtypeStruct((s, N_STATE), BF16), jax.ShapeDtypeStruct((s, N_STATE), BF16)],
        scratch_shapes=[pltpu.VMEM((TS, N_STATE), F32), pltpu.VMEM((TS, N_STATE), F32),
                        pltpu.VMEM(STATE_TILE, F32), pltpu.VMEM(STATE_TILE, F32), pltpu.VMEM(STATE_TILE, F32),
                        pltpu.VMEM(STATE_TILE, F32), pltpu.VMEM((1, N_STATE), F32), pltpu.VMEM((1, N_STATE), F32)],
        name=f"s5_fwd_{tag}", compiler_params=_cp(dimension_semantics=("arbitrary",)),
    )(u, _scan_order(), p["bbt3"], p["a_re8"], p["a_im8"], p["ct3"], p["d_skip"], p["w_glu"], p["b_glu"])


def _pool_consts():
    w = jnp.repeat(jnp.asarray(POOL_WINDOWS, F32), POOL_GROUP)[None, :]
    return w


def _window_sum(buf, first, rows, wl, step):
    acc = buf[pl.ds(first, rows), :]
    for j in range(1, MAX_WINDOW):
        term = buf[pl.ds(first + step * j, rows), :]
        acc = acc + (term if j < min(POOL_WINDOWS) else term * (wl > j).astype(F32))
    return acc


def _pool_count(i, rows, wl, offset=0):
    t = (i * TS + offset + 1).astype(F32) + lax.broadcasted_iota(jnp.int32, (rows, 1), 0).astype(F32)
    return jnp.minimum(t, wl)


def _pool_fwd(zb, p, tag):
    s = zb.shape[0]
    hb = TS // MAX_WINDOW

    def body(u_ref, halo_ref, wl_ref, w_ref, sc_ref, ob_ref, pooled_ref, buf):
        i = pl.program_id(0)
        uv = u_ref[...]
        buf[pl.ds(0, MAX_WINDOW), :] = jnp.where(i > 0, halo_ref[...], 0.0)
        buf[pl.ds(MAX_WINDOW, TS), :] = uv
        wl = wl_ref[...]
        pooled = (_window_sum(buf, MAX_WINDOW, TS, wl, -1) / _pool_count(i, TS, wl) - uv).astype(BF16)
        pooled_ref[...] = pooled
        ob_ref[...] = (_dot(pooled, w_ref[...]) * sc_ref[...]).astype(BF16)

    return pl.pallas_call(
        body, grid=(s // TS,),
        in_specs=[_row(TS, D_POOL),
                  pl.BlockSpec((MAX_WINDOW, D_POOL), lambda i: (jnp.maximum(i * hb - 1, 0), 0)),
                  _const((1, D_POOL)), _const((D_POOL, D_POOL)), _const((1, D_POOL))],
        out_specs=[_row(TS, D_POOL), _row(TS, D_POOL)],
        out_shape=[jax.ShapeDtypeStruct((s, D_POOL), BF16), jax.ShapeDtypeStruct((s, D_POOL), BF16)],
        scratch_shapes=[pltpu.VMEM((TS + MAX_WINDOW, D_POOL), F32)],
        name=f"pool_fwd_{tag}", compiler_params=_cp(dimension_semantics=("arbitrary",)),
    )(zb, zb, _pool_consts(), p["w_pool_bd"], p["pool_scale"])


def _sgu_mix(vl, wpair_ref, lo, hi):
    rows = vl.shape[0]
    chunks = []
    for c in range(rows // CHUNK):
        vc = vl[CHUNK * c:CHUNK * (c + 1), :]
        parts = []
        for q in range(SGU_HEADS // 2):
            vq = vc[:, LANES * q:LANES * (q + 1)]
            rhs = jnp.concatenate([vq * lo, vq * hi], axis=0).astype(BF16)
            parts.append(_dot(wpair_ref[q], rhs))
        chunks.append(jnp.concatenate(parts, axis=1))
    return jnp.concatenate(chunks, axis=0)


def _sgu_front(zuv, lng, lnb):
    zu = zuv[:, :D_SGU]
    zv = zuv[:, D_SGU:]
    u = _gelu(zu)
    v = _gelu(zv)
    mu = jnp.mean(v, axis=-1, keepdims=True)
    vc = v - mu
    rs = lax.rsqrt(jnp.mean(vc * vc, axis=-1, keepdims=True) + EPS)
    vn = vc * rs
    return zu, zv, u, vn, rs, vn * lng + lnb


def _half_masks():
    lane = lax.broadcasted_iota(jnp.int32, (1, LANES), 1)
    lo = (lane < SGU_HEAD_DIM).astype(F32)
    return lo, 1.0 - lo


def _sgu_fwd(zuv, p, tag):
    s = zuv.shape[0]

    def body(z_ref, lng_ref, lnb_ref, wp_ref, bias_ref, oc_ref):
        lo, hi = _half_masks()
        _, _, u, _, _, vl = _sgu_front(z_ref[...], lng_ref[...], lnb_ref[...])
        mixed = _sgu_mix(vl, wp_ref, lo, hi) + jnp.tile(bias_ref[...], (TS // CHUNK, 1))
        oc_ref[...] = (u * mixed).astype(BF16)

    return pl.pallas_call(
        body, grid=(s // TS,),
        in_specs=[_row(TS, 2 * D_SGU), _const((1, D_SGU)), _const((1, D_SGU)),
                  _const((SGU_HEADS // 2, CHUNK, 2 * CHUNK)), _const((CHUNK, D_SGU))],
        out_specs=_row(TS, D_SGU),
        out_shape=jax.ShapeDtypeStruct((s, D_SGU), BF16),
        name=f"sgu_fwd_{tag}", compiler_params=_cp(dimension_semantics=("arbitrary",)),
    )(zuv, p["sgu_ln_g"], p["sgu_ln_b"], p["ws_pair"], p["bias_sp"])


def _blk_fwd(x0, oa, ob, oc, p, tag):
    s = x0.shape[0]
    ts = TS_FFN

    def body(x0_ref, oa_ref, ob_ref, oc_ref, wo_ref, g_ref, wg_ref, wu_ref, wd_ref,
             x1_ref, x2_ref, h2_ref, gt_ref, up_ref, ycat_ref):
        ycat = jnp.concatenate([oa_ref[...], ob_ref[...], oc_ref[...]], axis=1)
        ycat_ref[...] = ycat
        x1 = x0_ref[...] + _dot(ycat, wo_ref[...])
        x1_ref[...] = x1
        n, _ = _rms(x1)
        h2 = (n * g_ref[...]).astype(BF16)
        h2_ref[...] = h2
        gt = _dot_nt(h2, wg_ref[...])
        up = _dot_nt(h2, wu_ref[...])
        gt_ref[...] = gt.astype(BF16)
        up_ref[...] = up.astype(BF16)
        act = (gt * _sigmoid(gt) * up).astype(BF16)
        x2_ref[...] = x1 + _dot(act, wd_ref[...])

    return pl.pallas_call(
        body, grid=(s // ts,),
        in_specs=[_row(ts, D_MODEL), _row(ts, D_SSM), _row(ts, D_POOL), _row(ts, D_SGU),
                  _const((D_MODEL, D_MODEL)), _const((1, D_MODEL)), _const((D_FF, D_MODEL)),
                  _const((D_FF, D_MODEL)), _const((D_FF, D_MODEL))],
        out_specs=[_row(ts, D_MODEL), _row(ts, D_MODEL), _row(ts, D_MODEL), _row(ts, D_FF), _row(ts, D_FF),
                   _row(ts, D_MODEL)],
        out_shape=[jax.ShapeDtypeStruct((s, D_MODEL), F32), jax.ShapeDtypeStruct((s, D_MODEL), F32),
                   jax.ShapeDtypeStruct((s, D_MODEL), BF16), jax.ShapeDtypeStruct((s, D_FF), BF16),
                   jax.ShapeDtypeStruct((s, D_FF), BF16), jax.ShapeDtypeStruct((s, D_MODEL), BF16)],
        name=f"blk_fwd_{tag}", compiler_params=_cp(dimension_semantics=("arbitrary",)),
    )(x0, oa, ob, oc, p["w_out"], p["g_ffn"], p["w_gate"], p["w_up"], p["w_down"])


def _blk_bwd(dx2, x1, gt, up, p, tag):
    s = dx2.shape[0]
    ts = TS_FFN

    def body(dx2_ref, x1_ref, gt_ref, up_ref, wd_ref, wgt_ref, wut_ref, wo_ref, g_ref,
             dx1_ref, da_ref, db_ref, dc_ref, dgt_ref, dup_ref, act_ref, dg_ref):
        @pl.when(pl.program_id(0) == 0)
        def _():
            dg_ref[...] = jnp.zeros_like(dg_ref)

        dx2v = dx2_ref[...]
        dact = _dot_nt(dx2v.astype(BF16), wd_ref[...])
        gf = gt_ref[...].astype(F32)
        uf = up_ref[...].astype(F32)
        sg = _sigmoid(gf)
        sl = gf * sg
        act_ref[...] = (sl * uf).astype(BF16)
        dgt = (dact * uf * (sg * (1.0 + gf * (1.0 - sg)))).astype(BF16)
        dup = (dact * sl).astype(BF16)
        dgt_ref[...] = dgt
        dup_ref[...] = dup
        dh2 = _dot(dgt, wgt_ref[...]) + _dot(dup, wut_ref[...])
        n, r = _rms(x1_ref[...])
        dxn, dgp = _rms_bwd(dh2, n, r, g_ref[...])
        dg_ref[...] += _colsum8(dgp)
        dx1 = dx2v + dxn
        dx1_ref[...] = dx1
        dy = _dot_nt(dx1.astype(BF16), wo_ref[...])
        da_ref[...] = dy[:, :D_SSM]
        db_ref[...] = dy[:, D_SSM:D_SSM + D_POOL]
        dc_ref[...] = dy[:, D_SSM + D_POOL:]

    return pl.pallas_call(
        body, grid=(s // ts,),
        in_specs=[_row(ts, D_MODEL), _row(ts, D_MODEL), _row(ts, D_FF), _row(ts, D_FF),
                  _const((D_FF, D_MODEL)), _const((D_FF, D_MODEL)), _const((D_FF, D_MODEL)),
                  _const((D_MODEL, D_MODEL)), _const((1, D_MODEL))],
        out_specs=[_row(ts, D_MODEL), _row(ts, D_SSM), _row(ts, D_POOL), _row(ts, D_SGU), _row(ts, D_FF),
                   _row(ts, D_FF), _row(ts, D_FF), _acc((SUBLANES, D_MODEL))],
        out_shape=[jax.ShapeDtypeStruct((s, D_MODEL), F32), jax.ShapeDtypeStruct((s, D_SSM), F32),
                   jax.ShapeDtypeStruct((s, D_POOL), F32), jax.ShapeDtypeStruct((s, D_SGU), F32),
                   jax.ShapeDtypeStruct((s, D_FF), BF16), jax.ShapeDtypeStruct((s, D_FF), BF16),
                   jax.ShapeDtypeStruct((s, D_FF), BF16), jax.ShapeDtypeStruct((SUBLANES, D_MODEL), F32)],
        name=f"blk_bwd_{tag}", compiler_params=_cp(dimension_semantics=("arbitrary",)),
    )(dx2, x1, gt, up, p["w_down"], p["w_gate"], p["w_up"], p["w_out"], p["g_ffn"])


def _s5_bwd(dout, u, y, h_re, h_im, p, tag):
    s = u.shape[0]
    nt = s // TS
    seg = SEG

    def rev(n):
        return pl.BlockSpec((TS, n), lambda i: (nt - 1 - i, 0))

    def body(do_ref, u_ref, y_ref, hr_ref, hi_ref, perm_ref, ar_ref, ai_ref, cb_ref, bb_ref, dsk_ref,
             wglu_ref, bglu_ref,
             du_ref, dct_ref, dbb_ref, dar_ref, dai_ref, dd_ref, dwglu_ref, dbglu_ref,
             gr, gi, hsr, hsi, er, ei, jr, ji, cr, ci):
        @pl.when(pl.program_id(0) == 0)
        def _():
            for ref in (cr, ci, dct_ref, dbb_ref, dar_ref, dai_ref, dd_ref, dwglu_ref, dbglu_ref):
                ref[...] = jnp.zeros_like(ref)

        perm = perm_ref[...]
        uv = _to_scan_order(perm, u_ref[...])
        yv = y_ref[...]
        dov = _to_scan_order(perm, do_ref[...])
        g = _gelu(yv)
        gb = g.astype(BF16)
        sg = _sigmoid(_dot(gb, wglu_ref[...]) + bglu_ref[...])
        dpre = dov * g * sg * (1.0 - sg)
        dpb = dpre.astype(BF16)
        dwglu_ref[...] += _dot_tn(gb, dpb)
        dbglu_ref[...] += _colsum8(dpre)
        dy = (dov * sg + _dot_nt(dpb, wglu_ref[...])) * _gelu_grad(yv)
        dd_ref[...] += _colsum8(dy * uv)
        dyb = dy.astype(BF16)
        hsr[...] = hr_ref[...].astype(F32)
        hsi[...] = hi_ref[...].astype(F32)
        for j in range(N_USLAB):
            dct_ref[j] += _dot_tn(_state_cat(hsr, hsi, j), _lanes(dyb, j))
            _state_split(gr, gi, j, _dot(_lanes(dyb, j), cb_ref[j]))
        ar = ar_ref[...]
        ai = -ai_ref[...]

        def local(k, h):
            rows = _scan_rows(seg - 1 - k)
            nr, ni = _cmul(ar, ai, h[0], h[1])
            return nr + gr[rows, :], ni + gi[rows, :]

        zero = jnp.zeros(STATE_TILE, F32)
        e_r, e_i = lax.fori_loop(0, seg, local, (zero, zero))
        er[...] = e_r
        ei[...] = e_i
        pr, pi = _cpow(ar[0:1, :], ai[0:1, :], seg)
        c_r = cr[...]
        c_i = ci[...]
        for j in range(SUBLANES - 1, -1, -1):
            jr[j:j + 1, :] = c_r
            ji[j:j + 1, :] = c_i
            n_r, n_i = _cmul(pr, pi, c_r, c_i)
            c_r = n_r + er[j:j + 1, :]
            c_i = n_i + ei[j:j + 1, :]
        cr[...] = c_r
        ci[...] = c_i

        def full(k, carry):
            g_r, g_i, a_r, a_i = carry
            rows = _scan_rows(seg - 1 - k)
            h_r = hsr[rows, :]
            h_i = hsi[rows, :]
            a_r = a_r + g_r * h_r + g_i * h_i
            a_i = a_i + g_i * h_r - g_r * h_i
            nr, ni = _cmul(ar, ai, g_r, g_i)
            nr = nr + gr[rows, :]
            ni = ni + gi[rows, :]
            gr[rows, :] = nr
            gi[rows, :] = ni
            return nr, ni, a_r, a_i

        _, _, a_r, a_i = lax.fori_loop(0, seg, full, (jr[...], ji[...], zero, zero))
        dar_ref[...] += a_r
        dai_ref[...] += a_i
        ub = uv.astype(BF16)
        dus = []
        for j in range(N_USLAB):
            gb_j = _state_cat(gr, gi, j)
            dbb_ref[j] += _dot_tn(_lanes(ub, j), gb_j)
            dus.append(_dot(gb_j, bb_ref[j]))
        du = dy * dsk_ref[...] + jnp.concatenate(dus, axis=1)
        du_ref[...] = _dot_tn(perm, du.astype(BF16)).astype(BF16)

    big = (TS, N_STATE)
    return pl.pallas_call(
        body, grid=(nt,),
        in_specs=[rev(D_SSM), rev(D_SSM), rev(D_SSM), rev(N_STATE), rev(N_STATE), _const((TS, TS)),
                  _const(STATE_TILE), _const(STATE_TILE), _const(S5_IN), _const(S5_OUT), _const((1, D_SSM)),
                  _const((D_SSM, D_SSM)), _const((1, D_SSM))],
        out_specs=[rev(D_SSM), _acc(S5_OUT), _acc(S5_IN), _acc(STATE_TILE), _acc(STATE_TILE),
                   _acc((SUBLANES, D_SSM)), _acc((D_SSM, D_SSM)), _acc((SUBLANES, D_SSM))],
        out_shape=[jax.ShapeDtypeStruct((s, D_SSM), BF16), jax.ShapeDtypeStruct(S5_OUT, F32),
                   jax.ShapeDtypeStruct(S5_IN, F32), jax.ShapeDtypeStruct(STATE_TILE, F32),
                   jax.ShapeDtypeStruct(STATE_TILE, F32), jax.ShapeDtypeStruct((SUBLANES, D_SSM), F32),
                   jax.ShapeDtypeStruct((D_SSM, D_SSM), F32), jax.ShapeDtypeStruct((SUBLANES, D_SSM), F32)],
        scratch_shapes=[pltpu.VMEM(big, F32), pltpu.VMEM(big, F32), pltpu.VMEM(big, F32), pltpu.VMEM(big, F32),
                        pltpu.VMEM(STATE_TILE, F32), pltpu.VMEM(STATE_TILE, F32), pltpu.VMEM(STATE_TILE, F32),
                        pltpu.VMEM(STATE_TILE, F32), pltpu.VMEM((1, N_STATE), F32), pltpu.VMEM((1, N_STATE), F32)],
        name=f"s5_bwd_{tag}", compiler_params=_cp(dimension_semantics=("arbitrary",)),
    )(dout, u, y, h_re, h_im, _scan_order(), p["a_re8"], p["a_im8"], p["cb3"], p["bb3"], p["d_skip"], p["w_glu"],
      p["b_glu"])


def _pool_bwd(dout, pooled, p, tag):
    s = dout.shape[0]
    nt = s // TS
    hb = TS // MAX_WINDOW

    def halo(n):
        return pl.BlockSpec((MAX_WINDOW, n), lambda i: (jnp.minimum((i + 1) * hb, nt * hb - 1), 0))

    def body(do_ref, po_ref, doh_ref, wl_ref, w_ref, wt_ref, sc_ref, dz_ref, dw_ref, dsc_ref, buf):
        i = pl.program_id(0)

        @pl.when(i == 0)
        def _():
            dw_ref[...] = jnp.zeros_like(dw_ref)
            dsc_ref[...] = jnp.zeros_like(dsc_ref)

        wl = wl_ref[...]
        sc = sc_ref[...]
        dov = do_ref[...]
        pooled_b = po_ref[...]
        dsc_ref[...] += _colsum8(dov * _dot(pooled_b, w_ref[...]))
        dmix = (dov * sc).astype(BF16)
        dw_ref[...] += _dot_tn(pooled_b, dmix)
        dpool = _dot(dmix, wt_ref[...])
        dpool_h = _dot((doh_ref[...] * sc).astype(BF16), wt_ref[...])
        buf[pl.ds(0, TS), :] = dpool / _pool_count(i, TS, wl)
        buf[pl.ds(TS, MAX_WINDOW), :] = jnp.where(i < nt - 1, dpool_h / _pool_count(i, MAX_WINDOW, wl, TS), 0.0)
        dz_ref[...] = (_window_sum(buf, 0, TS, wl, 1) - dpool).astype(BF16)

    return pl.pallas_call(
        body, grid=(nt,),
        in_specs=[_row(TS, D_POOL), _row(TS, D_POOL), halo(D_POOL), _const((1, D_POOL)), _const((D_POOL, D_POOL)),
                  _const((D_POOL, D_POOL)), _const((1, D_POOL))],
        out_specs=[_row(TS, D_POOL), _acc((D_POOL, D_POOL)), _acc((SUBLANES, D_POOL))],
        out_shape=[jax.ShapeDtypeStruct((s, D_POOL), BF16), jax.ShapeDtypeStruct((D_POOL, D_POOL), F32),
                   jax.ShapeDtypeStruct((SUBLANES, D_POOL), F32)],
        scratch_shapes=[pltpu.VMEM((TS + MAX_WINDOW, D_POOL), F32)],
        name=f"pool_bwd_{tag}", compiler_params=_cp(dimension_semantics=("arbitrary",)),
    )(dout, pooled, dout, _pool_consts(), p["w_pool_bd"], p["w_pool_bd_t"], p["pool_scale"])


def _sgu_bwd(dout, zuv, p, tag):
    s = zuv.shape[0]

    def body(do_ref, z_ref, lng_ref, lnb_ref, wp_ref, wpt_ref, bias_ref,
             dz_ref, dws_ref, dbias_ref, dlng_ref, dlnb_ref):
        @pl.when(pl.program_id(0) == 0)
        def _():
            for ref in (dws_ref, dbias_ref, dlng_ref, dlnb_ref):
                ref[...] = jnp.zeros_like(ref)

        lo, hi = _half_masks()
        lng = lng_ref[...]
        zu, zv, u, vn, rs, vl = _sgu_front(z_ref[...], lng, lnb_ref[...])
        mixed = _sgu_mix(vl, wp_ref, lo, hi) + jnp.tile(bias_ref[...], (TS // CHUNK, 1))
        dov = do_ref[...]
        dzu = dov * mixed * _gelu_grad(zu)
        dmix = dov * u
        dbias = dbias_ref[...]
        for c in range(TS // CHUNK):
            dmc = dmix[CHUNK * c:CHUNK * (c + 1), :]
            dbias = dbias + dmc
            vlc = vl[CHUNK * c:CHUNK * (c + 1), :].astype(BF16)
            for q in range(SGU_HEADS // 2):
                dq = dmc[:, LANES * q:LANES * (q + 1)]
                vq = vlc[:, LANES * q:LANES * (q + 1)]
                dws_ref[2 * q] += _dot_nt((dq * lo).astype(BF16), vq)
                dws_ref[2 * q + 1] += _dot_nt((dq * hi).astype(BF16), vq)
        dbias_ref[...] = dbias
        dvl = _sgu_mix(dmix, wpt_ref, lo, hi)
        dlng_ref[...] += _colsum8(dvl * vn)
        dlnb_ref[...] += _colsum8(dvl)
        dvn = dvl * lng
        dv = rs * (dvn - jnp.mean(dvn, axis=-1, keepdims=True) - vn * jnp.mean(dvn * vn, axis=-1, keepdims=True))
        dz_ref[...] = jnp.concatenate([dzu, dv * _gelu_grad(zv)], axis=1).astype(BF16)

    return pl.pallas_call(
        body, grid=(s // TS,),
        in_specs=[_row(TS, D_SGU), _row(TS, 2 * D_SGU), _const((1, D_SGU)), _const((1, D_SGU)),
                  _const((SGU_HEADS // 2, CHUNK, 2 * CHUNK)), _const((SGU_HEADS // 2, CHUNK, 2 * CHUNK)),
                  _const((CHUNK, D_SGU))],
        out_specs=[_row(TS, 2 * D_SGU), _acc((SGU_HEADS, CHUNK, CHUNK)), _acc((CHUNK, D_SGU)),
                   _acc((SUBLANES, D_SGU)), _acc((SUBLANES, D_SGU))],
        out_shape=[jax.ShapeDtypeStruct((s, 2 * D_SGU), BF16), jax.ShapeDtypeStruct((SGU_HEADS, CHUNK, CHUNK), F32),
                   jax.ShapeDtypeStruct((CHUNK, D_SGU), F32), jax.ShapeDtypeStruct((SUBLANES, D_SGU), F32),
                   jax.ShapeDtypeStruct((SUBLANES, D_SGU), F32)],
        name=f"sgu_bwd_{tag}", compiler_params=_cp(dimension_semantics=("arbitrary",)),
    )(dout, zuv, p["sgu_ln_g"], p["sgu_ln_b"], p["ws_pair"], p["ws_pair_t"], p["bias_sp"])


def _mix_in_bwd(dza, dzb, dzuv, x0, dx1, p, tag):
    s = x0.shape[0]

    def body(da_ref, db_ref, dc_ref, x_ref, dx1_ref, wt_ref, g_ref, dx0_ref, dz_ref, dg_ref):
        @pl.when(pl.program_id(0) == 0)
        def _():
            dg_ref[...] = jnp.zeros_like(dg_ref)

        dz = jnp.concatenate([da_ref[...], db_ref[...], dc_ref[...]], axis=1).astype(BF16)
        dz_ref[...] = dz
        n, r = _rms(x_ref[...])
        dxn, dgp = _rms_bwd(_dot(dz, wt_ref[...]), n, r, g_ref[...])
        dg_ref[...] += _colsum8(dgp)
        dx0_ref[...] = dx1_ref[...] + dxn

    return pl.pallas_call(
        body, grid=(s // TS,),
        in_specs=[_row(TS, D_SSM), _row(TS, D_POOL), _row(TS, 2 * D_SGU), _row(TS, D_MODEL), _row(TS, D_MODEL),
                  _const((D_IN, D_MODEL)), _const((1, D_MODEL))],
        out_specs=[_row(TS, D_MODEL), _row(TS, D_IN), _acc((SUBLANES, D_MODEL))],
        out_shape=[jax.ShapeDtypeStruct((s, D_MODEL), F32), jax.ShapeDtypeStruct((s, D_IN), BF16),
                   jax.ShapeDtypeStruct((SUBLANES, D_MODEL), F32)],
        name=f"mix_in_bwd_{tag}", compiler_params=_cp(dimension_semantics=("arbitrary",)),
    )(dza, dzb, dzuv, x0, dx1, p["w_in"], p["g_mix"])


def _head(x, target, g):
    s = x.shape[0]

    def body(x_ref, t_ref, g_ref, dx_ref, loss_ref, dg_ref):
        @pl.when(pl.program_id(0) == 0)
        def _():
            loss_ref[...] = jnp.zeros_like(loss_ref)
            dg_ref[...] = jnp.zeros_like(dg_ref)

        gv = g_ref[...]
        n, r = _rms(x_ref[...])
        diff = n * gv - t_ref[...]
        loss_ref[...] += jnp.sum(diff * diff) * (0.5 / D_MODEL)
        dxn, dgp = _rms_bwd(diff * (1.0 / D_MODEL), n, r, gv)
        dg_ref[...] += _colsum8(dgp)
        dx_ref[...] = dxn

    return pl.pallas_call(
        body, grid=(s // TS,),
        in_specs=[_row(TS, D_MODEL), _row(TS, D_MODEL), _const((1, D_MODEL))],
        out_specs=[_row(TS, D_MODEL), _acc((SUBLANES, LANES)), _acc((SUBLANES, D_MODEL))],
        out_shape=[jax.ShapeDtypeStruct((s, D_MODEL), F32), jax.ShapeDtypeStruct((SUBLANES, LANES), F32),
                   jax.ShapeDtypeStruct((SUBLANES, D_MODEL), F32)],
        name="head", compiler_params=_cp(dimension_semantics=("arbitrary",)),
    )(x, target, g)


def _atb(a, b, tag, token=None):
    s, ka = a.shape
    kb = b.shape[1]
    ts = ATB_ROWS
    tn = min(kb, ATB_COLS)
    ns = s // ts
    after = [] if token is None else [token]

    def body(a_ref, b_ref, *rest):
        o_ref = rest[-1]

        @pl.when(pl.program_id(1) == 0)
        def _():
            o_ref[...] = jnp.zeros_like(o_ref)

        o_ref[...] += _dot_tn(a_ref[...].astype(BF16), b_ref[...].astype(BF16))

    return pl.pallas_call(
        body, grid=(kb // tn, ns),
        in_specs=[pl.BlockSpec((ts, ka), lambda j, i: (i, 0)), pl.BlockSpec((ts, tn), lambda j, i: (i, j))]
        + [pl.BlockSpec(memory_space=pl.ANY)] * len(after),
        out_specs=pl.BlockSpec((ka, tn), lambda j, i: (0, j)),
        out_shape=jax.ShapeDtypeStruct((ka, kb), F32),
        name=f"atb_{tag}", compiler_params=_cp(dimension_semantics=("arbitrary", "arbitrary")),
    )(a, b, *after)


def _s5_discretise(a_re, a_im, log_dt, b_re, b_im):
    dt = jnp.exp(log_dt)[:, None]
    mag = jnp.exp(a_re * dt)
    ar = mag * jnp.cos(a_im * dt)
    ai = mag * jnp.sin(a_im * dt)
    den = a_re * a_re + a_im * a_im
    f_re = ((ar - 1.0) * a_re + ai * a_im) / den
    f_im = (ai * a_re - (ar - 1.0) * a_im) / den
    bb_re = f_re[..., None] * b_re - f_im[..., None] * b_im
    bb_im = f_re[..., None] * b_im + f_im[..., None] * b_re
    return ar, ai, bb_re, bb_im


def _block_diag(blocks):
    g, r, c = blocks.shape
    eye = jnp.eye(g, dtype=blocks.dtype)
    return (blocks[:, :, None, :] * eye[:, None, :, None]).reshape(g * r, g * c)


def _block_diag_extract(m, g):
    r = m.shape[0] // g
    c = m.shape[1] // g
    eye = jnp.eye(g, dtype=m.dtype)
    return jnp.sum(m.reshape(g, r, g, c) * eye[:, None, :, None], axis=2)


GROUPS_PER_SLAB = N_GROUPS // N_USLAB


def _slab_diag(blocks):
    k = GROUPS_PER_SLAB
    _, r, c = blocks.shape
    eye = jnp.eye(k, dtype=blocks.dtype)
    spread = blocks.reshape(N_USLAB, k, r, 1, c) * eye[None, :, None, :, None]
    return spread.reshape(N_USLAB, k * r, k * c)


def _slab_diag_extract(m):
    k = GROUPS_PER_SLAB
    r, c = m.shape[1] // k, m.shape[2] // k
    eye = jnp.eye(k, dtype=m.dtype)
    return jnp.sum(m.reshape(N_USLAB, k, r, k, c) * eye[None, :, None, :, None], axis=3).reshape(N_GROUPS, r, c)


def _state_slabs(v):
    return jnp.broadcast_to(v.reshape(1, N_STATE), STATE_TILE)


def _tril():
    return jnp.tril(jnp.ones((CHUNK, CHUNK), dtype=bool))


def _layer_params(w, l):
    row = lambda v: v.reshape(1, -1)
    t = lambda m: jnp.swapaxes(m, -1, -2)
    ar, ai, bb_re, bb_im = _s5_discretise(w["A_re"][l], w["A_im"][l], w["log_dt"][l], w["B_re"][l], w["B_im"][l])
    bbt3 = jnp.concatenate([_slab_diag(t(bb_re)), _slab_diag(t(bb_im))], axis=2).astype(BF16)
    ct3 = jnp.concatenate([_slab_diag(t(w["C_re"][l])), -_slab_diag(t(w["C_im"][l]))], axis=1).astype(BF16)
    ws = jnp.where(_tril()[None], w["w_spatial"][l], 0.0)
    pair = lambda m: jnp.stack([jnp.concatenate([m[2 * q], m[2 * q + 1]], axis=1)
                                for q in range(SGU_HEADS // 2)]).astype(BF16)
    wp = _block_diag(w["w_pool"][l]).astype(BF16)
    p = dict(
        g_mix=row(w["g_mix"][l]), g_ffn=row(w["g_ffn"][l]), d_skip=row(w["D_skip"][l]), b_glu=row(w["b_glu"][l]),
        pool_scale=row(w["pool_scale"][l]), sgu_ln_g=row(w["sgu_ln_g"][l]), sgu_ln_b=row(w["sgu_ln_b"][l]),
        a_re8=_state_slabs(ar), a_im8=_state_slabs(ai),
        bbt3=bbt3, bb3=t(bbt3), ct3=ct3, cb3=t(ct3),
        w_pool_bd=wp, w_pool_bd_t=t(wp), ws_pair=pair(ws), ws_pair_t=pair(t(ws)),
        bias_sp=jnp.repeat(t(w["b_spatial"][l]), SGU_HEAD_DIM, axis=1),
    )
    return p


MIX_WEIGHTS = ("w_in", "w_glu")
FFN_WEIGHTS = ("w_out", "w_gate", "w_up", "w_down")


def _with_big(p, mats):
    p.update(mats)


def _rows_sum(v):
    return jnp.sum(v, axis=0)


ATB_COLS = 1024
ATB_ROWS = 1024


def _after(v, token):
    return v if token is None else v + token[0, 0]


def _join(a, b):
    return b if a is None else a if b is None else a + b


def _layer_bwd(dx2, sv, p, w, l, tag, hooks, token):
    t = lambda m: jnp.swapaxes(m, -1, -2)
    dx1, da, db, dc, dgt, dup, act, dg_ffn = _blk_bwd(dx2, sv["x1"], sv["gt"], sv["up"],
                                                      dict(p, g_ffn=_after(p["g_ffn"], token)), tag)
    token = hooks["tick"]([dx1])
    token = _join(token, hooks["on_grads"](l, "ffn", {
        "w_down": _atb(act, dx2, tag + "_wd", token), "w_gate": _atb(dgt, sv["h2"], tag + "_wg", token),
        "w_up": _atb(dup, sv["h2"], tag + "_wu", token), "w_out": _atb(sv["ycat"], dx1, tag + "_wo", token)}))
    g = {}
    g["g_ffn"] = _rows_sum(dg_ffn)
    dza, dct3, dbbt3, dar8, dai8, dd8, dwglu, dbglu8 = _s5_bwd(
        da, sv["za"], sv["y"], sv["h_re"], sv["h_im"], dict(p, d_skip=_after(p["d_skip"], token)), tag)
    token = hooks["tick"]([dza])
    dzb, dwp, dsc8 = _pool_bwd(db, sv["pooled"], dict(p, pool_scale=_after(p["pool_scale"], token)), tag)
    dzuv, dws, dbias, dlng8, dlnb8 = _sgu_bwd(dc, sv["zuv"], p, tag)
    dx0, dz, dg_mix = _mix_in_bwd(dza, dzb, dzuv, sv["x0"], dx1, p, tag)
    g["g_mix"] = _rows_sum(dg_mix)
    g["b_glu"] = _rows_sum(dbglu8)
    g["D_skip"] = _rows_sum(dd8)
    half = N_STATE // N_USLAB
    g["C_re"] = t(_slab_diag_extract(dct3[:, :half, :]))
    g["C_im"] = -t(_slab_diag_extract(dct3[:, half:, :]))
    dar = jnp.sum(dar8, axis=0).reshape(N_GROUPS, SSM_STATE)
    dai = jnp.sum(dai8, axis=0).reshape(N_GROUPS, SSM_STATE)
    dbb_re = t(_slab_diag_extract(dbbt3[:, :, :half]))
    dbb_im = t(_slab_diag_extract(dbbt3[:, :, half:]))
    _, disc_vjp = jax.vjp(_s5_discretise, w["A_re"][l], w["A_im"][l], w["log_dt"][l], w["B_re"][l], w["B_im"][l])
    g["A_re"], g["A_im"], g["log_dt"], g["B_re"], g["B_im"] = disc_vjp((dar, dai, dbb_re, dbb_im))
    g["w_pool"] = _block_diag_extract(dwp, len(POOL_WINDOWS))
    g["pool_scale"] = _rows_sum(dsc8)
    g["sgu_ln_g"] = _rows_sum(dlng8)
    g["sgu_ln_b"] = _rows_sum(dlnb8)
    g["w_spatial"] = jnp.where(_tril()[None], dws, 0.0)
    g["b_spatial"] = t(jnp.sum(dbias.reshape(CHUNK, SGU_HEADS, SGU_HEAD_DIM), axis=-1))
    token = hooks["on_small"](l, g)
    token = hooks["on_grads"](l, "mix", {"w_in": _atb(dz, sv["h1"], tag + "_wi", token), "w_glu": dwglu})
    return dx0, token


def _local_step(x, target, w, hooks):
    params = [_layer_params(w, l) for l in range(DEPTH)]
    saved = []
    h = x
    for l in range(DEPTH):
        p, tag = params[l], f"l{l}"
        _with_big(p, hooks["get_big"](l, "mix", [h]))
        za, zb, zuv, h1 = _mix_in_fwd(h, p["g_mix"], p["w_in"], tag)
        oa, y, h_re, h_im = _s5_fwd(za, p, tag)
        ob, pooled = _pool_fwd(zb, p, tag)
        oc = _sgu_fwd(zuv, p, tag)
        _with_big(p, hooks["get_big"](l, "ffn", [oa, ob, oc]))
        x1, x2, h2, gt, up, ycat = _blk_fwd(h, oa, ob, oc, p, tag)
        saved.append(dict(x0=h, za=za, zuv=zuv, h1=h1, ycat=ycat, y=y, h_re=h_re, h_im=h_im, pooled=pooled, x1=x1,
                          h2=h2, gt=gt, up=up))
        h = x2
    dx, loss8, dgf8 = _head(h, target, w["g_final"].reshape(1, -1))
    grads = [None] * DEPTH

    def on_small(l, g_l):
        grads[l] = g_l
        if l > 0:
            return None
        g = {n: jnp.stack([grads[k][n] for k in range(DEPTH)]) for n in SMALL if n != "g_final"}
        g["g_final"] = _rows_sum(dgf8)
        return hooks["on_small"](g, loss8[0, 0])

    token = None
    for l in reversed(range(DEPTH)):
        dx, token = _layer_bwd(dx, saved[l], params[l], w, l, f"l{l}", dict(hooks, on_small=on_small), token)
    return dx


_ANY = pl.BlockSpec(memory_space=pl.ANY)
_MESH = pl.DeviceIdType.MESH


def _place():
    return lax.axis_index("x"), lax.axis_index("y"), lax.axis_index("c")


def _other_chips(x, y):
    return [(1 - x, y), (x, 1 - y), (1 - x, 1 - y)]


def _dma_sems(n):
    return pltpu.SemaphoreType.DMA((n,))


def _remote(src, dst, send_sems, recv_sems, k, to):
    return pltpu.make_async_remote_copy(src_ref=src, dst_ref=dst, send_sem=send_sems.at[k], recv_sem=recv_sems.at[k],
                                        device_id=to, device_id_type=_MESH)


_HBM = pl.BlockSpec(memory_space=pltpu.HBM)
_SEM = pl.BlockSpec(memory_space=pltpu.SEMAPHORE)
_EFFECT = pltpu.SideEffectType.DATAFLOW_SIDE_EFFECTING
N_REL = N_CHIPS - 1


def _gather_plan(x, y, c, srcs, lands):
    plan = []
    for l in lands:
        r = l.shape[0] // N_CHIPS
        rows = l.at[pl.ds((2 * x + y) * r, r)]
        plan += [(rows, rows, (cx, cy, c)) for cx, cy in _other_chips(x, y)]
    return plan


def _sibling_plan(x, y, c, srcs, lands):
    return [(s.at[:, 1 - c], l, (x, y, 1 - c)) for s, l in zip(srcs, lands)]


def _slab_plan(x, y, c, srcs, lands):
    return [(s.at[2 * cx + cy], l.at[j], (cx, cy, c))
            for s, l in zip(srcs, lands) for j, (cx, cy) in enumerate(_other_chips(x, y))]


def _plan_copies(plan, srcs, lands, send_sems, recv_sems):
    x, y, c = _place()
    return [_remote(s, d, send_sems, recv_sems, k, to) for k, (s, d, to) in enumerate(plan(x, y, c, srcs, lands))]


def _hbm(a):
    return pltpu.with_memory_space_constraint(a, pltpu.HBM)


def _everyone_plan(x, y, c, srcs, lands):
    me = 4 * x + 2 * y + c
    peers = [(x, y, 1 - c)] + [(cx, cy, cc) for cx, cy in _other_chips(x, y) for cc in (c, 1 - c)]
    return [(s, l.at[me], peer) for s, l in zip(srcs, lands) for peer in peers]


def _copies_start(name, plan, srcs, lands, ncopies):
    ns, n = len(srcs), len(srcs) + len(lands)

    def body(*refs):
        for cp in _plan_copies(plan, refs[:ns], refs[ns:n], refs[n], refs[n + 1]):
            cp.start()
        refs[-1][...] = jnp.zeros_like(refs[-1])

    ref_out = [pltpu.HBM(a.shape, a.dtype) for a in (*srcs, *lands)]
    out = pl.pallas_call(
        body, name=name, in_specs=[_HBM] * n,
        out_shape=(_dma_sems(ncopies), _dma_sems(ncopies), *ref_out, jax.ShapeDtypeStruct((SUBLANES, LANES), F32)),
        out_specs=(_SEM, _SEM, *[_HBM] * n, pl.BlockSpec(memory_space=pltpu.VMEM)),
        input_output_aliases={i: 2 + i for i in range(n)},
        compiler_params=pltpu.CompilerParams(has_side_effects=_EFFECT),
    )(*[_hbm(a) for a in (*srcs, *lands)])
    return dict(name=name, plan=plan, sems=out[:2], srcs=out[2:2 + ns], lands=out[2 + ns:2 + n], token=out[-1])


def _copies_wait(started, after):
    ns = len(started["srcs"])
    n = ns + len(started["lands"])
    plan = started["plan"]

    def body(*refs):
        for cp in _plan_copies(plan, refs[:ns], refs[ns:n], refs[n], refs[n + 1]):
            cp.wait_send()
            cp.wait_recv()

    args = (*started["srcs"], *started["lands"])
    out = pl.pallas_call(
        body, name=started["name"] + "_wait", out_shape=[pltpu.HBM(a.shape, a.dtype) for a in args],
        in_specs=[_HBM] * n + [_SEM, _SEM] + [_ANY] * len(after), out_specs=[_HBM] * n,
        input_output_aliases={i: i for i in range(n)},
        compiler_params=pltpu.CompilerParams(has_side_effects=_EFFECT),
    )(*args, *started["sems"], *after)
    return out[:ns], out[ns:]


def _place_shards(ws, layer, sel, after, tag):
    nw = len(ws)

    def body(sel_ref, *refs):
        for i in range(nw):
            refs[nw + len(after) + i][...] = refs[i][...].astype(BF16)

    return pl.pallas_call(
        body, grid_spec=pltpu.PrefetchScalarGridSpec(
            num_scalar_prefetch=1, grid=(1,),
            in_specs=[pl.BlockSpec((None,) + a.shape[1:], lambda i, s: (layer, 0, 0)) for a in ws] + [_ANY] * len(after),
            out_specs=[pl.BlockSpec(a.shape[1:], lambda i, s: (s[1], 0)) for a in ws]),
        out_shape=[jax.ShapeDtypeStruct((N_CHIPS * a.shape[1], a.shape[2]), BF16) for a in ws],
        name=f"place_shards_{tag}", compiler_params=_cp(dimension_semantics=("arbitrary",)),
    )(sel, *ws, *after)


def _share_halves(fs, layer, tag):
    nw = len(fs)

    def body(*refs):
        ins = refs[:nw]
        send_sems, recv_sems = refs[2 * nw:]
        x, y, c = _place()

        def half(i, who):
            h = ins[i].shape[1] // 2
            return ins[i].at[layer, pl.ds(who * h, h)]

        sends = [_remote(half(i, c), half(i, c), send_sems, recv_sems, i, (x, y, 1 - c)) for i in range(nw)]
        for cp in sends:
            cp.start()
        for i in range(nw):
            sends[i].wait_send()
            _remote(half(i, c), half(i, 1 - c), send_sems, recv_sems, i, (x, y, 1 - c)).wait_recv()

    return pl.pallas_call(
        body, out_shape=[jax.ShapeDtypeStruct(f.shape, f.dtype) for f in fs], in_specs=[_ANY] * nw,
        out_specs=[_ANY] * nw, input_output_aliases={i: i for i in range(nw)},
        scratch_shapes=[_dma_sems(nw), _dma_sems(nw)], name=f"share_halves_{tag}",
    )(*fs)


def _add_halves(g4s, recvs, sel, tag):
    nw = len(g4s)

    def body(sel_ref, *refs):
        for i in range(nw):
            refs[2 * nw + i][...] = (refs[i][...] + refs[nw + i][...]).astype(BF16)

    mine = [pl.BlockSpec((None, None) + g.shape[2:], lambda k, s: (k, s[0], 0, 0)) for g in g4s]
    slab = [pl.BlockSpec((None,) + g.shape[2:], lambda k, s: (k, 0, 0)) for g in g4s]
    return pl.pallas_call(
        body, grid_spec=pltpu.PrefetchScalarGridSpec(num_scalar_prefetch=1, grid=(N_CHIPS,), in_specs=mine + slab,
                                                     out_specs=slab),
        out_shape=[jax.ShapeDtypeStruct(r.shape, BF16) for r in recvs], name=f"add_halves_{tag}",
        compiler_params=_cp(dimension_semantics=("arbitrary",)),
    )(sel, *g4s, *recvs)


def _add_chips(ps, slabs, fs, layer, sel, tag):
    nw = len(ps)
    old = [f for f in fs if f is not None]

    def body(sel_ref, *refs):
        outs = refs[2 * nw + len(old):]
        for i in range(nw):
            acc = refs[i][...].astype(F32)
            for j in range(N_REL):
                acc = acc + refs[nw + i][j].astype(F32)
            outs[i][...] = acc

    shapes = [(DEPTH, 2 * p.shape[1], p.shape[2]) for p in ps]
    in_specs = [pl.BlockSpec((None,) + p.shape[1:], lambda i, s: (s[1], 0, 0)) for p in ps]
    in_specs += [pl.BlockSpec(sl.shape, lambda i, s: (0, 0, 0)) for sl in slabs]
    in_specs += [_ANY] * len(old)
    first_old = 1 + 2 * nw
    aliases, k = {}, 0
    for i, f in enumerate(fs):
        if f is not None:
            aliases[first_old + k] = i
            k += 1
    return pl.pallas_call(
        body, grid_spec=pltpu.PrefetchScalarGridSpec(
            num_scalar_prefetch=1, grid=(1,), in_specs=in_specs,
            out_specs=[pl.BlockSpec((None,) + p.shape[1:], lambda i, s: (layer, s[0], 0)) for p in ps]),
        out_shape=[jax.ShapeDtypeStruct(sh, F32) for sh in shapes], input_output_aliases=aliases,
        name=f"add_chips_{tag}", compiler_params=_cp(dimension_semantics=("arbitrary",)),
    )(sel, *ps, *slabs, *old)


def _adamw_math(w, g, m, v):
    m = ADAM_B1 * m + (1.0 - ADAM_B1) * g
    v = ADAM_B2 * v + (1.0 - ADAM_B2) * (g * g)
    m_hat = m / (1.0 - ADAM_B1 ** ADAM_STEP)
    v_hat = v / (1.0 - ADAM_B2 ** ADAM_STEP)
    delta = -ADAM_LR * (m_hat / (jnp.sqrt(v_hat) + ADAM_EPS) + ADAM_WD * w)
    return delta, m, v


ADAM_ROWS = 512


def _row_tile(rows, most):
    return max(t for t in range(SUBLANES, most + 1, SUBLANES) if rows % t == 0)


def _adamw(w, g, m, v, tag):
    depth, rows, cols = w.shape
    tr = _row_tile(rows, ADAM_ROWS)

    def body(w_ref, g_ref, m_ref, v_ref, d_ref, nm_ref, nv_ref):
        d, nm, nv = _adamw_math(w_ref[...], g_ref[...], m_ref[...], v_ref[...])
        d_ref[...] = d
        nm_ref[...] = nm
        nv_ref[...] = nv

    spec = pl.BlockSpec((None, tr, cols), lambda l, i: (l, i, 0))
    return pl.pallas_call(
        body, grid=(depth, rows // tr), in_specs=[spec] * 4, out_specs=[spec] * 3,
        out_shape=[jax.ShapeDtypeStruct(w.shape, F32)] * 3, name=f"adamw_{tag}",
        compiler_params=_cp(dimension_semantics=("arbitrary", "arbitrary")),
    )(w, g, m, v)


SMALL_TILE = 384
PRECISE = ("g_final",)
COARSE = [n for n in SMALL if n not in PRECISE]


def _small_reduce_adamw(gathered, w, m, v, tag):
    rows = w.shape[0]
    tr = math.gcd(rows, SMALL_TILE)

    def body(ga_ref, w_ref, m_ref, v_ref, g_ref, d_ref, nm_ref, nv_ref):
        g = ga_ref[0].astype(F32)
        for k in range(1, N_DEV):
            g = g + ga_ref[k].astype(F32)
        g_ref[...] = g
        d, nm, nv = _adamw_math(w_ref[...], g, m_ref[...], v_ref[...])
        d_ref[...] = d
        nm_ref[...] = nm
        nv_ref[...] = nv

    spec = _row(tr, LANES)
    return pl.pallas_call(
        body, grid=(rows // tr,),
        in_specs=[pl.BlockSpec((N_DEV, tr, LANES), lambda i: (0, i, 0)), spec, spec, spec], out_specs=[spec] * 4,
        out_shape=[jax.ShapeDtypeStruct((rows, LANES), F32)] * 4, name=f"small_reduce_adamw_{tag}",
        compiler_params=_cp(dimension_semantics=("arbitrary",)),
    )(gathered, w, m, v)


def _exchange_form(n, a):
    return jnp.swapaxes(a, 1, 2) if n in TRANSPOSED else a


PACK_ROWS = 16


def _pack(vals, names, extra=None):
    parts = [vals[n].reshape(-1) for n in names] + ([] if extra is None else [extra.reshape(1)])
    flat = jnp.concatenate(parts)
    rows = -(-flat.size // (LANES * PACK_ROWS)) * PACK_ROWS
    return jnp.pad(flat, (0, rows * LANES - flat.size)).reshape(rows, LANES)


def _unpack(buf, like, names):
    flat = buf.reshape(-1)
    out, off = {}, 0
    for n in names:
        out[n] = flat[off:off + like[n].size].reshape(like[n].shape)
        off += like[n].size
    return out, flat[off:]


def kernel(x, g_mix, w_in, A_re, A_im, log_dt, B_re, B_im, C_re, C_im, D_skip, w_glu, b_glu, w_pool, pool_scale, sgu_ln_g, sgu_ln_b, w_spatial, b_spatial, w_out, g_ffn, w_gate, w_up, w_down, g_final, loss_target, m_g_mix, m_w_in, m_A_re, m_A_im, m_log_dt, m_B_re, m_B_im, m_C_re, m_C_im, m_D_skip, m_w_glu, m_b_glu, m_w_pool, m_pool_scale, m_sgu_ln_g, m_sgu_ln_b, m_w_spatial, m_b_spatial, m_w_out, m_g_ffn, m_w_gate, m_w_up, m_w_down, m_g_final, v_g_mix, v_w_in, v_A_re, v_A_im, v_log_dt, v_B_re, v_B_im, v_C_re, v_C_im, v_D_skip, v_w_glu, v_b_glu, v_w_pool, v_pool_scale, v_sgu_ln_g, v_sgu_ln_b, v_w_spatial, v_b_spatial, v_w_out, v_g_ffn, v_w_gate, v_w_up, v_w_down, v_g_final):
    loc = locals()
    w = {n: loc[n] for n in WEIGHTS}
    m = {n: loc["m_" + n] for n in WEIGHTS}
    v = {n: loc["v_" + n] for n in WEIGHTS}
    sel = jnp.stack([lax.axis_index("c"), 2 * lax.axis_index("x") + lax.axis_index("y")]).astype(jnp.int32)

    chip = sel[1]

    halves = [(l, half) for l in range(DEPTH) for half in ("mix", "ffn")]
    names = {"mix": MIX_WEIGHTS, "ffn": FFN_WEIGHTS}
    started = {}
    wx = {n: _exchange_form(n, w[n]) for n in BIG}
    chain = []
    for l, half in halves:
        lands = _place_shards([wx[n] for n in names[half]], l, sel, chain, f"l{l}_{half}")
        started[l, half] = _copies_start(f"weights_l{l}_{half}", _gather_plan, [], lands, N_REL * len(lands))
        chain = [started[l, half]["token"]]
    w = dict(w, g_mix=_after(w["g_mix"], started[halves[-1]]["token"]))

    def get_big(l, half, after):
        return dict(zip(names[half], _copies_wait(started[l, half], after)[1]))

    result = {n: None for n in BIG}
    stage = {"swap": None, "slabs": None}

    def advance(after):
        if stage["slabs"] is not None:
            ex, ns, l, tag = stage["slabs"]
            part, slabs = _copies_wait(ex, after)
            bufs = _add_chips(part, slabs, [result[n] for n in ns], l, sel, tag)
            for n, f in zip(ns, _share_halves(bufs, l, tag)):
                result[n] = f
            stage["slabs"] = None
        if stage["swap"] is None:
            return None
        sw, ns, l, tag = stage["swap"]
        part = _add_halves(*_copies_wait(sw, after), sel, tag)
        slabs = [lax.empty((N_REL,) + p.shape[1:], BF16) for p in part]
        ex = _copies_start(f"grads_{tag}", _slab_plan, part, slabs, N_REL * len(part))
        stage["slabs"], stage["swap"] = (ex, ns, l, tag), None
        return ex["token"]

    def on_grads(l, half, grads):
        ns = list(grads)
        tag = f"l{l}_{half}"
        token = advance([grads[ns[0]]])
        g4s = [grads[n].reshape(N_CHIPS, 2, grads[n].shape[0] // (2 * N_CHIPS), grads[n].shape[1]) for n in ns]
        recvs = [lax.empty((N_CHIPS,) + g4.shape[2:], F32) for g4 in g4s]
        sw = _copies_start(f"swap_{tag}", _sibling_plan, g4s, recvs, len(g4s))
        stage["swap"] = (sw, ns, l, tag)
        return _join(token, sw["token"])

    small = {}

    def on_small(g, loss_local):
        me = 2 * chip + sel[0]
        blocks = [_pack(g, COARSE).astype(BF16), _pack(g, PRECISE, loss_local)]
        lands = [lax.dynamic_update_slice(lax.empty((N_DEV,) + b.shape, b.dtype), b[None], (me, 0, 0)) for b in blocks]
        small.update(_copies_start("small_grads", _everyone_plan, blocks, lands, (N_DEV - 1) * len(blocks)))
        return small["token"]

    dx = _local_step(x[0], loss_target[0], w, dict(get_big=get_big, on_grads=on_grads, tick=advance, on_small=on_small))
    advance([])
    advance([])
    grads, deltas, new_m, new_v = {}, {}, {}, {}
    for n in BIG:
        outs = _adamw(wx[n], result[n], _exchange_form(n, m[n]), _exchange_form(n, v[n]), n)
        grads[n], deltas[n], new_m[n], new_v[n] = [_exchange_form(n, a) for a in (result[n], *outs)]

    _, gathered = _copies_wait(small, [new_v[n] for n in BIG])
    zero = jnp.zeros((), F32)
    loss = None
    for names_k, extra, block, tag in ((COARSE, None, gathered[0], "coarse"), (PRECISE, zero, gathered[1], "precise")):
        outs = _small_reduce_adamw(block, _pack(w, names_k, extra), _pack(m, names_k, extra), _pack(v, names_k, extra),
                                   tag)
        for store, buf in zip((grads, deltas, new_m, new_v), outs):
            vals, rest = _unpack(buf, w, names_k)
            store.update(vals)
            if store is grads and extra is not None:
                loss = rest[0]
    return (loss, dx[None], *[grads[n] for n in WEIGHTS], *[deltas[n] for n in WEIGHTS],
            *[new_m[n] for n in WEIGHTS], *[new_v[n] for n in WEIGHTS])
```

```python
import math

import jax
import jax.numpy as jnp
from jax import lax
from jax.experimental import pallas as pl
from jax.experimental.pallas import tpu as pltpu

F32 = jnp.float32
BF16 = jnp.bfloat16

D_MODEL = 1024
DEPTH = 2
D_SSM = 384
SSM_GROUP = 16
N_GROUPS = 24
SSM_STATE = 64
N_STATE = N_GROUPS * SSM_STATE
POOL_WINDOWS = (2, 4, 8, 16)
POOL_GROUP = 64
D_POOL = 256
MAX_WINDOW = 16
SGU_HEADS = 6
SGU_HEAD_DIM = 64
D_SGU = 384
CHUNK = 128
D_IN = D_SSM + D_POOL + 2 * D_SGU
D_FF = 2816
EPS = 1e-6

ADAM_LR = 0.001
ADAM_B1 = 0.9
ADAM_B2 = 0.999
ADAM_EPS = 1e-08
ADAM_WD = 0.01
ADAM_STEP = 10

LANES = 128
SUBLANES = 8
N_SLAB = N_STATE // LANES
VMEM_LIMIT = 56 * 1024 * 1024

TS = 512
TS_FFN = 256

WEIGHTS = ['g_mix', 'w_in', 'A_re', 'A_im', 'log_dt', 'B_re', 'B_im', 'C_re', 'C_im', 'D_skip', 'w_glu', 'b_glu',
           'w_pool', 'pool_scale', 'sgu_ln_g', 'sgu_ln_b', 'w_spatial', 'b_spatial', 'w_out', 'g_ffn', 'w_gate',
           'w_up', 'w_down', 'g_final']
BIG = ['w_in', 'w_glu', 'w_out', 'w_gate', 'w_up', 'w_down']
SMALL = [n for n in WEIGHTS if n not in BIG]
TRANSPOSED = ("w_in", "w_gate", "w_up")
N_CHIPS = 4
N_DEV = 8


def _cp(**kw):
    return pltpu.CompilerParams(vmem_limit_bytes=VMEM_LIMIT, **kw)


def _row(ts, n):
    return pl.BlockSpec((ts, n), lambda i: (i, 0))


def _const(shape):
    nd = len(shape)
    return pl.BlockSpec(shape, lambda i: (0,) * nd, pipeline_mode=pl.Buffered(1))


def _acc(shape):
    nd = len(shape)
    return pl.BlockSpec(shape, lambda i: (0,) * nd)


def _dot(a, b):
    return jnp.dot(a, b, preferred_element_type=F32)


def _dot_tn(a, b):
    return lax.dot_general(a, b, (((0,), (0,)), ((), ())), preferred_element_type=F32)


def _dot_nt(a, b):
    return lax.dot_general(a, b, (((1,), (1,)), ((), ())), preferred_element_type=F32)


_G0 = math.sqrt(2.0 / math.pi)
_G1 = 0.044715


def _gelu(x):
    return 0.5 * x * (1.0 + jnp.tanh(_G0 * (x + _G1 * x * x * x)))


def _gelu_grad(x):
    t = jnp.tanh(_G0 * (x + _G1 * x * x * x))
    return 0.5 * (1.0 + t) + 0.5 * x * (1.0 - t * t) * (_G0 * (1.0 + 3.0 * _G1 * x * x))


def _sigmoid(x):
    return 1.0 / (1.0 + jnp.exp(-x))


def _rms(x):
    r = lax.rsqrt(jnp.mean(x * x, axis=-1, keepdims=True) + EPS)
    return x * r, r


def _rms_bwd(dh, n, r, g):
    dn = dh * g
    return r * (dn - n * jnp.mean(dn * n, axis=-1, keepdims=True)), dh * n


def _colsum8(v):
    rows, n = v.shape
    return jnp.sum(v.reshape(rows // SUBLANES, SUBLANES, n), axis=0)


def _cmul(ar, ai, br, bi):
    return ar * br - ai * bi, ar * bi + ai * br


def _cpow(ar, ai, n):
    assert n & (n - 1) == 0
    while n > 1:
        ar, ai = _cmul(ar, ai, ar, ai)
        n //= 2
    return ar, ai


N_USLAB = D_SSM // LANES
SEG = TS // SUBLANES
SLAB_STATES = N_STATE // N_USLAB
S5_IN = (N_USLAB, LANES, 2 * SLAB_STATES)
S5_OUT = (N_USLAB, 2 * SLAB_STATES, LANES)
STATE_TILE = (SUBLANES, N_STATE)


def _scan_order():
    p = jnp.arange(TS)
    src = (p % SUBLANES) * SEG + p // SUBLANES
    return (src[:, None] == jnp.arange(TS)[None, :]).astype(BF16)


def _to_scan_order(perm, v):
    hi = v.astype(BF16)
    lo = (v - hi.astype(F32)).astype(BF16)
    return _dot(perm, hi) + _dot(perm, lo)


def _scan_rows(k):
    return pl.ds(pl.multiple_of(k * SUBLANES, SUBLANES), SUBLANES)


def _lanes(v, j):
    return v[:, LANES * j:LANES * (j + 1)]


def _states(j):
    return pl.ds(SLAB_STATES * j, SLAB_STATES)


def _state_split(re_ref, im_ref, j, v):
    re_ref[:, _states(j)] = v[:, :SLAB_STATES]
    im_ref[:, _states(j)] = v[:, SLAB_STATES:]


def _state_cat(re_ref, im_ref, j):
    return jnp.concatenate([re_ref[:, _states(j)], im_ref[:, _states(j)]], axis=1).astype(BF16)


def _s5_fwd(u, p, tag):
    s = u.shape[0]
    seg = SEG

    def body(u_ref, perm_ref, bbt_ref, ar_ref, ai_ref, ct_ref, dsk_ref, wglu_ref, bglu_ref,
             oa_ref, y_ref, hr_ref, hi_ref, sr, si, er, ei, ir, ii, cr, ci):
        @pl.when(pl.program_id(0) == 0)
        def _():
            cr[...] = jnp.zeros_like(cr)
            ci[...] = jnp.zeros_like(ci)

        perm = perm_ref[...]
        uv = _to_scan_order(perm, u_ref[...])
        ub = uv.astype(BF16)
        for j in range(N_USLAB):
            _state_split(sr, si, j, _dot(_lanes(ub, j), bbt_ref[j]))
        ar = ar_ref[...]
        ai = ai_ref[...]

        def local(k, h):
            rows = _scan_rows(k)
            hr, hi = _cmul(ar, ai, h[0], h[1])
            return hr + sr[rows, :], hi + si[rows, :]

        zero = jnp.zeros(STATE_TILE, F32)
        e_r, e_i = lax.fori_loop(0, seg, local, (zero, zero))
        er[...] = e_r
        ei[...] = e_i
        pr, pi = _cpow(ar[0:1, :], ai[0:1, :], seg)
        c_r = cr[...]
        c_i = ci[...]
        for j in range(SUBLANES):
            ir[j:j + 1, :] = c_r
            ii[j:j + 1, :] = c_i
            n_r, n_i = _cmul(pr, pi, c_r, c_i)
            c_r = n_r + er[j:j + 1, :]
            c_i = n_i + ei[j:j + 1, :]
        cr[...] = c_r
        ci[...] = c_i

        def full(k, h):
            rows = _scan_rows(k)
            hr, hi = _cmul(ar, ai, h[0], h[1])
            hr = hr + sr[rows, :]
            hi = hi + si[rows, :]
            sr[rows, :] = hr
            si[rows, :] = hi
            return hr, hi

        lax.fori_loop(0, seg, full, (ir[...], ii[...]))
        hr_ref[...] = sr[...].astype(BF16)
        hi_ref[...] = si[...].astype(BF16)
        y = jnp.concatenate([_dot(_state_cat(sr, si, j), ct_ref[j]) for j in range(N_USLAB)], axis=1)
        y = y + dsk_ref[...] * uv
        y_ref[...] = y
        g = _gelu(y)
        pre = _dot(g.astype(BF16), wglu_ref[...]) + bglu_ref[...]
        oa_ref[...] = _dot_tn(perm, (g * _sigmoid(pre)).astype(BF16)).astype(BF16)

    return pl.pallas_call(
        body, grid=(s // TS,),
        in_specs=[_row(TS, D_SSM), _const((TS, TS)), _const(S5_IN), _const(STATE_TILE), _const(STATE_TILE),
                  _const(S5_OUT), _const((1, D_SSM)), _const((D_SSM, D_SSM)), _const((1, D_SSM))],
        out_specs=[_row(TS, D_SSM), _row(TS, D_SSM), _row(TS, N_STATE), _row(TS, N_STATE)],
        out_shape=[jax.ShapeDtypeStruct((s, D_SSM), BF16), jax.ShapeDtypeStruct((s, D_SSM), F32),
                   jax.ShapeDtypeStruct((s, N_STATE), BF16), jax.ShapeDtypeStruct((s, N_STATE), BF16)],
        scratch_shapes=[pltpu.VMEM((TS, N_STATE), F32), pltpu.VMEM((TS, N_STATE), F32),
                        pltpu.VMEM(STATE_TILE, F32), pltpu.VMEM(STATE_TILE, F32), pltpu.VMEM(STATE_TILE, F32),
                        pltpu.VMEM(STATE_TILE, F32), pltpu.VMEM((1, N_STATE), F32), pltpu.VMEM((1, N_STATE), F32)],
        name=f"s5_fwd_{tag}", compiler_params=_cp(dimension_semantics=("arbitrary",)),
    )(u, _scan_order(), p["bbt3"], p["a_re8"], p["a_im8"], p["ct3"], p["d_skip"], p["w_glu"], p["b_glu"])


def _pool_consts():
    w = jnp.repeat(jnp.asarray(POOL_WINDOWS, F32), POOL_GROUP)[None, :]
    return w


def _window_sum(buf, first, rows, wl, step):
    acc = buf[pl.ds(first, rows), :]
    for j in range(1, MAX_WINDOW):
        term = buf[pl.ds(first + step * j, rows), :]
        acc = acc + (term if j < min(POOL_WINDOWS) else term * (wl > j).astype(F32))
    return acc


def _pool_count(i, rows, wl, offset=0):
    t = (i * TS + offset + 1).astype(F32) + lax.broadcasted_iota(jnp.int32, (rows, 1), 0).astype(F32)
    return jnp.minimum(t, wl)


def _sgu_mix(vl, wpair_ref, lo, hi):
    rows = vl.shape[0]
    chunks = []
    for c in range(rows // CHUNK):
        vc = vl[CHUNK * c:CHUNK * (c + 1), :]
        parts = []
        for q in range(SGU_HEADS // 2):
            vq = vc[:, LANES * q:LANES * (q + 1)]
            rhs = jnp.concatenate([vq * lo, vq * hi], axis=0).astype(BF16)
            parts.append(_dot(wpair_ref[q], rhs))
        chunks.append(jnp.concatenate(parts, axis=1))
    return jnp.concatenate(chunks, axis=0)


def _sgu_front(zuv, lng, lnb):
    zu = zuv[:, :D_SGU]
    zv = zuv[:, D_SGU:]
    u = _gelu(zu)
    v = _gelu(zv)
    mu = jnp.mean(v, axis=-1, keepdims=True)
    vc = v - mu
    rs = lax.rsqrt(jnp.mean(vc * vc, axis=-1, keepdims=True) + EPS)
    vn = vc * rs
    return zu, zv, u, vn, rs, vn * lng + lnb


def _half_masks():
    lane = lax.broadcasted_iota(jnp.int32, (1, LANES), 1)
    lo = (lane < SGU_HEAD_DIM).astype(F32)
    return lo, 1.0 - lo


def _mix_fwd(x, p, tag):
    s = x.shape[0]

    def body(x_ref, g_ref, w_ref, wl_ref, wp_ref, sc_ref, lng_ref, lnb_ref, wsp_ref, bias_ref,
             za_ref, zuv_ref, h_ref, ob_ref, pooled_ref, oc_ref, buf):
        i = pl.program_id(0)

        @pl.when(i == 0)
        def _():
            buf[pl.ds(0, MAX_WINDOW), :] = jnp.zeros((MAX_WINDOW, D_POOL), F32)

        n, _ = _rms(x_ref[...])
        h = (n * g_ref[...]).astype(BF16)
        h_ref[...] = h
        z = _dot_nt(h, w_ref[...])
        za_ref[...] = z[:, :D_SSM]
        zb = z[:, D_SSM:D_SSM + D_POOL]
        zuv = z[:, D_SSM + D_POOL:]
        zuv_ref[...] = zuv
        buf[pl.ds(MAX_WINDOW, TS), :] = zb
        wl = wl_ref[...]
        pooled = (_window_sum(buf, MAX_WINDOW, TS, wl, -1) / _pool_count(i, TS, wl) - zb).astype(BF16)
        buf[pl.ds(0, MAX_WINDOW), :] = zb[TS - MAX_WINDOW:, :]
        pooled_ref[...] = pooled
        ob_ref[...] = (_dot(pooled, wp_ref[...]) * sc_ref[...]).astype(BF16)
        lo, hi = _half_masks()
        _, _, u, _, _, vl = _sgu_front(zuv, lng_ref[...], lnb_ref[...])
        mixed = _sgu_mix(vl, wsp_ref, lo, hi) + jnp.tile(bias_ref[...], (TS // CHUNK, 1))
        oc_ref[...] = (u * mixed).astype(BF16)

    return pl.pallas_call(
        body, grid=(s // TS,),
        in_specs=[_row(TS, D_MODEL), _const((1, D_MODEL)), _const((D_IN, D_MODEL)), _const((1, D_POOL)),
                  _const((D_POOL, D_POOL)), _const((1, D_POOL)), _const((1, D_SGU)), _const((1, D_SGU)),
                  _const((SGU_HEADS // 2, CHUNK, 2 * CHUNK)), _const((CHUNK, D_SGU))],
        out_specs=[_row(TS, D_SSM), _row(TS, 2 * D_SGU), _row(TS, D_MODEL), _row(TS, D_POOL), _row(TS, D_POOL),
                   _row(TS, D_SGU)],
        out_shape=[jax.ShapeDtypeStruct((s, D_SSM), F32), jax.ShapeDtypeStruct((s, 2 * D_SGU), F32),
                   jax.ShapeDtypeStruct((s, D_MODEL), BF16), jax.ShapeDtypeStruct((s, D_POOL), BF16),
                   jax.ShapeDtypeStruct((s, D_POOL), BF16), jax.ShapeDtypeStruct((s, D_SGU), BF16)],
        scratch_shapes=[pltpu.VMEM((TS + MAX_WINDOW, D_POOL), F32)],
        name=f"mix_fwd_{tag}", compiler_params=_cp(dimension_semantics=("arbitrary",)),
    )(x, p["g_mix"], p["w_in"], _pool_consts(), p["w_pool_bd"], p["pool_scale"], p["sgu_ln_g"], p["sgu_ln_b"],
      p["ws_pair"], p["bias_sp"])


def _blk_fwd(x0, oa, ob, oc, p, tag):
    s = x0.shape[0]
    ts = TS_FFN

    def body(x0_ref, oa_ref, ob_ref, oc_ref, wo_ref, g_ref, wg_ref, wu_ref, wd_ref,
             x1_ref, x2_ref, h2_ref, gt_ref, up_ref, ycat_ref):
        ycat = jnp.concatenate([oa_ref[...], ob_ref[...], oc_ref[...]], axis=1)
        ycat_ref[...] = ycat
        x1 = x0_ref[...] + _dot(ycat, wo_ref[...])
        x1_ref[...] = x1
        n, _ = _rms(x1)
        h2 = (n * g_ref[...]).astype(BF16)
        h2_ref[...] = h2
        gt = _dot_nt(h2, wg_ref[...])
        up = _dot_nt(h2, wu_ref[...])
        gt_ref[...] = gt.astype(BF16)
        up_ref[...] = up.astype(BF16)
        act = (gt * _sigmoid(gt) * up).astype(BF16)
        x2_ref[...] = x1 + _dot(act, wd_ref[...])

    return pl.pallas_call(
        body, grid=(s // ts,),
        in_specs=[_row(ts, D_MODEL), _row(ts, D_SSM), _row(ts, D_POOL), _row(ts, D_SGU),
                  _const((D_MODEL, D_MODEL)), _const((1, D_MODEL)), _const((D_FF, D_MODEL)),
                  _const((D_FF, D_MODEL)), _const((D_FF, D_MODEL))],
        out_specs=[_row(ts, D_MODEL), _row(ts, D_MODEL), _row(ts, D_MODEL), _row(ts, D_FF), _row(ts, D_FF),
                   _row(ts, D_MODEL)],
        out_shape=[jax.ShapeDtypeStruct((s, D_MODEL), F32), jax.ShapeDtypeStruct((s, D_MODEL), F32),
                   jax.ShapeDtypeStruct((s, D_MODEL), BF16), jax.ShapeDtypeStruct((s, D_FF), BF16),
                   jax.ShapeDtypeStruct((s, D_FF), BF16), jax.ShapeDtypeStruct((s, D_MODEL), BF16)],
        name=f"blk_fwd_{tag}", compiler_params=_cp(dimension_semantics=("arbitrary",)),
    )(x0, oa, ob, oc, p["w_out"], p["g_ffn"], p["w_gate"], p["w_up"], p["w_down"])


def _blk_bwd(dx2, x1, gt, up, p, tag):
    s = dx2.shape[0]
    ts = TS_FFN

    def body(dx2_ref, x1_ref, gt_ref, up_ref, wd_ref, wgt_ref, wut_ref, wo_ref, g_ref,
             dx1_ref, da_ref, db_ref, dc_ref, dgt_ref, dup_ref, act_ref, dg_ref):
        @pl.when(pl.program_id(0) == 0)
        def _():
            dg_ref[...] = jnp.zeros_like(dg_ref)

        dx2v = dx2_ref[...]
        dact = _dot_nt(dx2v.astype(BF16), wd_ref[...])
        gf = gt_ref[...].astype(F32)
        uf = up_ref[...].astype(F32)
        sg = _sigmoid(gf)
        sl = gf * sg
        act_ref[...] = (sl * uf).astype(BF16)
        dgt = (dact * uf * (sg * (1.0 + gf * (1.0 - sg)))).astype(BF16)
        dup = (dact * sl).astype(BF16)
        dgt_ref[...] = dgt
        dup_ref[...] = dup
        dh2 = _dot(dgt, wgt_ref[...]) + _dot(dup, wut_ref[...])
        n, r = _rms(x1_ref[...])
        dxn, dgp = _rms_bwd(dh2, n, r, g_ref[...])
        dg_ref[...] += _colsum8(dgp)
        dx1 = dx2v + dxn
        dx1_ref[...] = dx1
        dy = _dot_nt(dx1.astype(BF16), wo_ref[...])
        da_ref[...] = dy[:, :D_SSM]
        db_ref[...] = dy[:, D_SSM:D_SSM + D_POOL]
        dc_ref[...] = dy[:, D_SSM + D_POOL:]

    return pl.pallas_call(
        body, grid=(s // ts,),
        in_specs=[_row(ts, D_MODEL), _row(ts, D_MODEL), _row(ts, D_FF), _row(ts, D_FF),
                  _const((D_FF, D_MODEL)), _const((D_FF, D_MODEL)), _const((D_FF, D_MODEL)),
                  _const((D_MODEL, D_MODEL)), _const((1, D_MODEL))],
        out_specs=[_row(ts, D_MODEL), _row(ts, D_SSM), _row(ts, D_POOL), _row(ts, D_SGU), _row(ts, D_FF),
                   _row(ts, D_FF), _row(ts, D_FF), _acc((SUBLANES, D_MODEL))],
        out_shape=[jax.ShapeDtypeStruct((s, D_MODEL), F32), jax.ShapeDtypeStruct((s, D_SSM), F32),
                   jax.ShapeDtypeStruct((s, D_POOL), F32), jax.ShapeDtypeStruct((s, D_SGU), F32),
                   jax.ShapeDtypeStruct((s, D_FF), BF16), jax.ShapeDtypeStruct((s, D_FF), BF16),
                   jax.ShapeDtypeStruct((s, D_FF), BF16), jax.ShapeDtypeStruct((SUBLANES, D_MODEL), F32)],
        name=f"blk_bwd_{tag}", compiler_params=_cp(dimension_semantics=("arbitrary",)),
    )(dx2, x1, gt, up, p["w_down"], p["w_gate"], p["w_up"], p["w_out"], p["g_ffn"])


def _s5_bwd(dout, u, y, h_re, h_im, p, tag):
    s = u.shape[0]
    nt = s // TS
    seg = SEG

    def rev(n):
        return pl.BlockSpec((TS, n), lambda i: (nt - 1 - i, 0))

    def body(do_ref, u_ref, y_ref, hr_ref, hi_ref, perm_ref, ar_ref, ai_ref, cb_ref, bb_ref, dsk_ref,
             wglu_ref, bglu_ref,
             du_ref, dct_ref, dbb_ref, dar_ref, dai_ref, dd_ref, dwglu_ref, dbglu_ref,
             gr, gi, hsr, hsi, er, ei, jr, ji, cr, ci):
        @pl.when(pl.program_id(0) == 0)
        def _():
            for ref in (cr, ci, dct_ref, dbb_ref, dar_ref, dai_ref, dd_ref, dwglu_ref, dbglu_ref):
                ref[...] = jnp.zeros_like(ref)

        perm = perm_ref[...]
        uv = _to_scan_order(perm, u_ref[...])
        yv = y_ref[...]
        dov = _to_scan_order(perm, do_ref[...])
        g = _gelu(yv)
        gb = g.astype(BF16)
        sg = _sigmoid(_dot(gb, wglu_ref[...]) + bglu_ref[...])
        dpre = dov * g * sg * (1.0 - sg)
        dpb = dpre.astype(BF16)
        dwglu_ref[...] += _dot_tn(gb, dpb)
        dbglu_ref[...] += _colsum8(dpre)
        dy = (dov * sg + _dot_nt(dpb, wglu_ref[...])) * _gelu_grad(yv)
        dd_ref[...] += _colsum8(dy * uv)
        dyb = dy.astype(BF16)
        hsr[...] = hr_ref[...].astype(F32)
        hsi[...] = hi_ref[...].astype(F32)
        for j in range(N_USLAB):
            dct_ref[j] += _dot_tn(_state_cat(hsr, hsi, j), _lanes(dyb, j))
            _state_split(gr, gi, j, _dot(_lanes(dyb, j), cb_ref[j]))
        ar = ar_ref[...]
        ai = -ai_ref[...]

        def local(k, h):
            rows = _scan_rows(seg - 1 - k)
            nr, ni = _cmul(ar, ai, h[0], h[1])
            return nr + gr[rows, :], ni + gi[rows, :]

        zero = jnp.zeros(STATE_TILE, F32)
        e_r, e_i = lax.fori_loop(0, seg, local, (zero, zero))
        er[...] = e_r
        ei[...] = e_i
        pr, pi = _cpow(ar[0:1, :], ai[0:1, :], seg)
        c_r = cr[...]
        c_i = ci[...]
        for j in range(SUBLANES - 1, -1, -1):
            jr[j:j + 1, :] = c_r
            ji[j:j + 1, :] = c_i
            n_r, n_i = _cmul(pr, pi, c_r, c_i)
            c_r = n_r + er[j:j + 1, :]
            c_i = n_i + ei[j:j + 1, :]
        cr[...] = c_r
        ci[...] = c_i

        def full(k, carry):
            g_r, g_i, a_r, a_i = carry
            rows = _scan_rows(seg - 1 - k)
            h_r = hsr[rows, :]
            h_i = hsi[rows, :]
            a_r = a_r + g_r * h_r + g_i * h_i
            a_i = a_i + g_i * h_r - g_r * h_i
            nr, ni = _cmul(ar, ai, g_r, g_i)
            nr = nr + gr[rows, :]
            ni = ni + gi[rows, :]
            gr[rows, :] = nr
            gi[rows, :] = ni
            return nr, ni, a_r, a_i

        _, _, a_r, a_i = lax.fori_loop(0, seg, full, (jr[...], ji[...], zero, zero))
        dar_ref[...] += a_r
        dai_ref[...] += a_i
        ub = uv.astype(BF16)
        dus = []
        for j in range(N_USLAB):
            gb_j = _state_cat(gr, gi, j)
            dbb_ref[j] += _dot_tn(_lanes(ub, j), gb_j)
            dus.append(_dot(gb_j, bb_ref[j]))
        du = dy * dsk_ref[...] + jnp.concatenate(dus, axis=1)
        du_ref[...] = _dot_tn(perm, du.astype(BF16)).astype(BF16)

    big = (TS, N_STATE)
    return pl.pallas_call(
        body, grid=(nt,),
        in_specs=[rev(D_SSM), rev(D_SSM), rev(D_SSM), rev(N_STATE), rev(N_STATE), _const((TS, TS)),
                  _const(STATE_TILE), _const(STATE_TILE), _const(S5_IN), _const(S5_OUT), _const((1, D_SSM)),
                  _const((D_SSM, D_SSM)), _const((1, D_SSM))],
        out_specs=[rev(D_SSM), _acc(S5_OUT), _acc(S5_IN), _acc(STATE_TILE), _acc(STATE_TILE),
                   _acc((SUBLANES, D_SSM)), _acc((D_SSM, D_SSM)), _acc((SUBLANES, D_SSM))],
        out_shape=[jax.ShapeDtypeStruct((s, D_SSM), BF16), jax.ShapeDtypeStruct(S5_OUT, F32),
                   jax.ShapeDtypeStruct(S5_IN, F32), jax.ShapeDtypeStruct(STATE_TILE, F32),
                   jax.ShapeDtypeStruct(STATE_TILE, F32), jax.ShapeDtypeStruct((SUBLANES, D_SSM), F32),
                   jax.ShapeDtypeStruct((D_SSM, D_SSM), F32), jax.ShapeDtypeStruct((SUBLANES, D_SSM), F32)],
        scratch_shapes=[pltpu.VMEM(big, F32), pltpu.VMEM(big, F32), pltpu.VMEM(big, F32), pltpu.VMEM(big, F32),
                        pltpu.VMEM(STATE_TILE, F32), pltpu.VMEM(STATE_TILE, F32), pltpu.VMEM(STATE_TILE, F32),
                        pltpu.VMEM(STATE_TILE, F32), pltpu.VMEM((1, N_STATE), F32), pltpu.VMEM((1, N_STATE), F32)],
        name=f"s5_bwd_{tag}", compiler_params=_cp(dimension_semantics=("arbitrary",)),
    )(dout, u, y, h_re, h_im, _scan_order(), p["a_re8"], p["a_im8"], p["cb3"], p["bb3"], p["d_skip"], p["w_glu"],
      p["b_glu"])


def _mix_bwd(dza, db, dc, pooled, zuv, x0, dx1, p, tag):
    s = x0.shape[0]
    nt = s // TS

    def rev(n):
        return pl.BlockSpec((TS, n), lambda i: (nt - 1 - i, 0))

    def body(da_ref, db_ref, dc_ref, po_ref, z_ref, x_ref, dx1_ref, wl_ref, wp_ref, wpt_ref, sc_ref, lng_ref, lnb_ref,
             wsp_ref, wspt_ref, bias_ref, win_ref, g_ref,
             dx0_ref, dz_ref, dwp_ref, dsc_ref, dws_ref, dbias_ref, dlng_ref, dlnb_ref, dg_ref, buf):
        step = pl.program_id(0)
        i = nt - 1 - step

        @pl.when(step == 0)
        def _():
            for ref in (dwp_ref, dsc_ref, dws_ref, dbias_ref, dlng_ref, dlnb_ref, dg_ref):
                ref[...] = jnp.zeros_like(ref)
            buf[pl.ds(TS, MAX_WINDOW), :] = jnp.zeros((MAX_WINDOW, D_POOL), F32)

        wl = wl_ref[...]
        sc = sc_ref[...]
        dob = db_ref[...]
        pooled_b = po_ref[...]
        dsc_ref[...] += _colsum8(dob * _dot(pooled_b, wp_ref[...]))
        dmixb = (dob * sc).astype(BF16)
        dwp_ref[...] += _dot_tn(pooled_b, dmixb)
        dpool = _dot(dmixb, wpt_ref[...])
        dq = dpool / _pool_count(i, TS, wl)
        buf[pl.ds(0, TS), :] = dq
        dzb = _window_sum(buf, 0, TS, wl, 1) - dpool
        buf[pl.ds(TS, MAX_WINDOW), :] = dq[:MAX_WINDOW, :]

        lo, hi = _half_masks()
        lng = lng_ref[...]
        zu, zv, u, vn, rs, vl = _sgu_front(z_ref[...], lng, lnb_ref[...])
        mixed = _sgu_mix(vl, wsp_ref, lo, hi) + jnp.tile(bias_ref[...], (TS // CHUNK, 1))
        doc = dc_ref[...]
        dzu = doc * mixed * _gelu_grad(zu)
        dmix = doc * u
        dbias = dbias_ref[...]
        for c in range(TS // CHUNK):
            dmc = dmix[CHUNK * c:CHUNK * (c + 1), :]
            dbias = dbias + dmc
            vlc = vl[CHUNK * c:CHUNK * (c + 1), :].astype(BF16)
            for q in range(SGU_HEADS // 2):
                dmq = _lanes(dmc, q)
                vq = _lanes(vlc, q)
                dws_ref[2 * q] += _dot_nt((dmq * lo).astype(BF16), vq)
                dws_ref[2 * q + 1] += _dot_nt((dmq * hi).astype(BF16), vq)
        dbias_ref[...] = dbias
        dvl = _sgu_mix(dmix, wspt_ref, lo, hi)
        dlng_ref[...] += _colsum8(dvl * vn)
        dlnb_ref[...] += _colsum8(dvl)
        dvn = dvl * lng
        dv = rs * (dvn - jnp.mean(dvn, axis=-1, keepdims=True) - vn * jnp.mean(dvn * vn, axis=-1, keepdims=True))

        dz = jnp.concatenate([da_ref[...], dzb.astype(BF16), dzu.astype(BF16), (dv * _gelu_grad(zv)).astype(BF16)],
                             axis=1)
        dz_ref[...] = dz
        n, r = _rms(x_ref[...])
        dxn, dgp = _rms_bwd(_dot(dz, win_ref[...]), n, r, g_ref[...])
        dg_ref[...] += _colsum8(dgp)
        dx0_ref[...] = dx1_ref[...] + dxn

    pair = (SGU_HEADS // 2, CHUNK, 2 * CHUNK)
    return pl.pallas_call(
        body, grid=(nt,),
        in_specs=[rev(D_SSM), rev(D_POOL), rev(D_SGU), rev(D_POOL), rev(2 * D_SGU), rev(D_MODEL), rev(D_MODEL),
                  _const((1, D_POOL)), _const((D_POOL, D_POOL)), _const((D_POOL, D_POOL)), _const((1, D_POOL)),
                  _const((1, D_SGU)), _const((1, D_SGU)), _const(pair), _const(pair), _const((CHUNK, D_SGU)),
                  _const((D_IN, D_MODEL)), _const((1, D_MODEL))],
        out_specs=[rev(D_MODEL), rev(D_IN), _acc((D_POOL, D_POOL)), _acc((SUBLANES, D_POOL)),
                   _acc((SGU_HEADS, CHUNK, CHUNK)), _acc((CHUNK, D_SGU)), _acc((SUBLANES, D_SGU)),
                   _acc((SUBLANES, D_SGU)), _acc((SUBLANES, D_MODEL))],
        out_shape=[jax.ShapeDtypeStruct((s, D_MODEL), F32), jax.ShapeDtypeStruct((s, D_IN), BF16),
                   jax.ShapeDtypeStruct((D_POOL, D_POOL), F32), jax.ShapeDtypeStruct((SUBLANES, D_POOL), F32),
                   jax.ShapeDtypeStruct((SGU_HEADS, CHUNK, CHUNK), F32), jax.ShapeDtypeStruct((CHUNK, D_SGU), F32),
                   jax.ShapeDtypeStruct((SUBLANES, D_SGU), F32), jax.ShapeDtypeStruct((SUBLANES, D_SGU), F32),
                   jax.ShapeDtypeStruct((SUBLANES, D_MODEL), F32)],
        scratch_shapes=[pltpu.VMEM((TS + MAX_WINDOW, D_POOL), F32)],
        name=f"mix_bwd_{tag}", compiler_params=_cp(dimension_semantics=("arbitrary",)),
    )(dza, db, dc, pooled, zuv, x0, dx1, _pool_consts(), p["w_pool_bd"], p["w_pool_bd_t"], p["pool_scale"],
      p["sgu_ln_g"], p["sgu_ln_b"], p["ws_pair"], p["ws_pair_t"], p["bias_sp"], p["w_in"], p["g_mix"])


def _head(x, target, g):
    s = x.shape[0]

    def body(x_ref, t_ref, g_ref, dx_ref, loss_ref, dg_ref):
        @pl.when(pl.program_id(0) == 0)
        def _():
            loss_ref[...] = jnp.zeros_like(loss_ref)
            dg_ref[...] = jnp.zeros_like(dg_ref)

        gv = g_ref[...]
        n, r = _rms(x_ref[...])
        diff = n * gv - t_ref[...]
        loss_ref[...] += jnp.sum(diff * diff) * (0.5 / D_MODEL)
        dxn, dgp = _rms_bwd(diff * (1.0 / D_MODEL), n, r, gv)
        dg_ref[...] += _colsum8(dgp)
        dx_ref[...] = dxn

    return pl.pallas_call(
        body, grid=(s // TS,),
        in_specs=[_row(TS, D_MODEL), _row(TS, D_MODEL), _const((1, D_MODEL))],
        out_specs=[_row(TS, D_MODEL), _acc((SUBLANES, LANES)), _acc((SUBLANES, D_MODEL))],
        out_shape=[jax.ShapeDtypeStruct((s, D_MODEL), F32), jax.ShapeDtypeStruct((SUBLANES, LANES), F32),
                   jax.ShapeDtypeStruct((SUBLANES, D_MODEL), F32)],
        name="head", compiler_params=_cp(dimension_semantics=("arbitrary",)),
    )(x, target, g)


def _atb(a, b, tag, token=None):
    s, ka = a.shape
    kb = b.shape[1]
    ts = ATB_ROWS
    tn = min(kb, ATB_COLS)
    ns = s // ts
    after = [] if token is None else [token]

    def body(a_ref, b_ref, *rest):
        o_ref = rest[-1]

        @pl.when(pl.program_id(1) == 0)
        def _():
            o_ref[...] = jnp.zeros_like(o_ref)

        o_ref[...] += _dot_tn(a_ref[...].astype(BF16), b_ref[...].astype(BF16))

    return pl.pallas_call(
        body, grid=(kb // tn, ns),
        in_specs=[pl.BlockSpec((ts, ka), lambda j, i: (i, 0)), pl.BlockSpec((ts, tn), lambda j, i: (i, j))]
        + [pl.BlockSpec(memory_space=pl.ANY)] * len(after),
        out_specs=pl.BlockSpec((ka, tn), lambda j, i: (0, j)),
        out_shape=jax.ShapeDtypeStruct((ka, kb), F32),
        name=f"atb_{tag}", compiler_params=_cp(dimension_semantics=("arbitrary", "arbitrary")),
    )(a, b, *after)


def _s5_discretise(a_re, a_im, log_dt, b_re, b_im):
    dt = jnp.exp(log_dt)[:, None]
    mag = jnp.exp(a_re * dt)
    ar = mag * jnp.cos(a_im * dt)
    ai = mag * jnp.sin(a_im * dt)
    den = a_re * a_re + a_im * a_im
    f_re = ((ar - 1.0) * a_re + ai * a_im) / den
    f_im = (ai * a_re - (ar - 1.0) * a_im) / den
    bb_re = f_re[..., None] * b_re - f_im[..., None] * b_im
    bb_im = f_re[..., None] * b_im + f_im[..., None] * b_re
    return ar, ai, bb_re, bb_im


def _block_diag(blocks):
    g, r, c = blocks.shape
    eye = jnp.eye(g, dtype=blocks.dtype)
    return (blocks[:, :, None, :] * eye[:, None, :, None]).reshape(g * r, g * c)


def _block_diag_extract(m, g):
    r = m.shape[0] // g
    c = m.shape[1] // g
    eye = jnp.eye(g, dtype=m.dtype)
    return jnp.sum(m.reshape(g, r, g, c) * eye[:, None, :, None], axis=2)


GROUPS_PER_SLAB = N_GROUPS // N_USLAB


def _slab_diag(blocks):
    k = GROUPS_PER_SLAB
    _, r, c = blocks.shape
    eye = jnp.eye(k, dtype=blocks.dtype)
    spread = blocks.reshape(N_USLAB, k, r, 1, c) * eye[None, :, None, :, None]
    return spread.reshape(N_USLAB, k * r, k * c)


def _slab_diag_extract(m):
    k = GROUPS_PER_SLAB
    r, c = m.shape[1] // k, m.shape[2] // k
    eye = jnp.eye(k, dtype=m.dtype)
    return jnp.sum(m.reshape(N_USLAB, k, r, k, c) * eye[None, :, None, :, None], axis=3).reshape(N_GROUPS, r, c)


def _state_slabs(v):
    return jnp.broadcast_to(v.reshape(1, N_STATE), STATE_TILE)


def _tril():
    return jnp.tril(jnp.ones((CHUNK, CHUNK), dtype=bool))


def _layer_params(w, l):
    row = lambda v: v.reshape(1, -1)
    t = lambda m: jnp.swapaxes(m, -1, -2)
    ar, ai, bb_re, bb_im = _s5_discretise(w["A_re"][l], w["A_im"][l], w["log_dt"][l], w["B_re"][l], w["B_im"][l])
    bbt3 = jnp.concatenate([_slab_diag(t(bb_re)), _slab_diag(t(bb_im))], axis=2).astype(BF16)
    ct3 = jnp.concatenate([_slab_diag(t(w["C_re"][l])), -_slab_diag(t(w["C_im"][l]))], axis=1).astype(BF16)
    ws = jnp.where(_tril()[None], w["w_spatial"][l], 0.0)
    pair = lambda m: jnp.stack([jnp.concatenate([m[2 * q], m[2 * q + 1]], axis=1)
                                for q in range(SGU_HEADS // 2)]).astype(BF16)
    wp = _block_diag(w["w_pool"][l]).astype(BF16)
    p = dict(
        g_mix=row(w["g_mix"][l]), g_ffn=row(w["g_ffn"][l]), d_skip=row(w["D_skip"][l]), b_glu=row(w["b_glu"][l]),
        pool_scale=row(w["pool_scale"][l]), sgu_ln_g=row(w["sgu_ln_g"][l]), sgu_ln_b=row(w["sgu_ln_b"][l]),
        a_re8=_state_slabs(ar), a_im8=_state_slabs(ai),
        bbt3=bbt3, bb3=t(bbt3), ct3=ct3, cb3=t(ct3),
        w_pool_bd=wp, w_pool_bd_t=t(wp), ws_pair=pair(ws), ws_pair_t=pair(t(ws)),
        bias_sp=jnp.repeat(t(w["b_spatial"][l]), SGU_HEAD_DIM, axis=1),
    )
    return p


MIX_WEIGHTS = ("w_in", "w_glu")
FFN_WEIGHTS = ("w_out", "w_gate", "w_up", "w_down")


def _with_big(p, mats):
    p.update(mats)


def _rows_sum(v):
    return jnp.sum(v, axis=0)


ATB_COLS = 1024
ATB_ROWS = 1024


def _after(v, token):
    return v if token is None else v + token[0, 0]


def _join(a, b):
    return b if a is None else a if b is None else a + b


def _layer_bwd(dx2, sv, p, w, l, tag, hooks, token):
    t = lambda m: jnp.swapaxes(m, -1, -2)
    dx1, da, db, dc, dgt, dup, act, dg_ffn = _blk_bwd(dx2, sv["x1"], sv["gt"], sv["up"],
                                                      dict(p, g_ffn=_after(p["g_ffn"], token)), tag)
    token = hooks["tick"]([dx1])
    token = _join(token, hooks["on_grads"](l, "ffn", {
        "w_down": _atb(act, dx2, tag + "_wd", token), "w_gate": _atb(dgt, sv["h2"], tag + "_wg", token),
        "w_up": _atb(dup, sv["h2"], tag + "_wu", token), "w_out": _atb(sv["ycat"], dx1, tag + "_wo", token)}))
    g = {}
    g["g_ffn"] = _rows_sum(dg_ffn)
    dza, dct3, dbbt3, dar8, dai8, dd8, dwglu, dbglu8 = _s5_bwd(
        da, sv["za"], sv["y"], sv["h_re"], sv["h_im"], dict(p, d_skip=_after(p["d_skip"], token)), tag)
    token = hooks["tick"]([dza])
    dx0, dz, dwp, dsc8, dws, dbias, dlng8, dlnb8, dg_mix = _mix_bwd(
        dza, db, dc, sv["pooled"], sv["zuv"], sv["x0"], dx1, dict(p, pool_scale=_after(p["pool_scale"], token)), tag)
    g["g_mix"] = _rows_sum(dg_mix)
    g["b_glu"] = _rows_sum(dbglu8)
    g["D_skip"] = _rows_sum(dd8)
    half = N_STATE // N_USLAB
    g["C_re"] = t(_slab_diag_extract(dct3[:, :half, :]))
    g["C_im"] = -t(_slab_diag_extract(dct3[:, half:, :]))
    dar = jnp.sum(dar8, axis=0).reshape(N_GROUPS, SSM_STATE)
    dai = jnp.sum(dai8, axis=0).reshape(N_GROUPS, SSM_STATE)
    dbb_re = t(_slab_diag_extract(dbbt3[:, :, :half]))
    dbb_im = t(_slab_diag_extract(dbbt3[:, :, half:]))
    _, disc_vjp = jax.vjp(_s5_discretise, w["A_re"][l], w["A_im"][l], w["log_dt"][l], w["B_re"][l], w["B_im"][l])
    g["A_re"], g["A_im"], g["log_dt"], g["B_re"], g["B_im"] = disc_vjp((dar, dai, dbb_re, dbb_im))
    g["w_pool"] = _block_diag_extract(dwp, len(POOL_WINDOWS))
    g["pool_scale"] = _rows_sum(dsc8)
    g["sgu_ln_g"] = _rows_sum(dlng8)
    g["sgu_ln_b"] = _rows_sum(dlnb8)
    g["w_spatial"] = jnp.where(_tril()[None], dws, 0.0)
    g["b_spatial"] = t(jnp.sum(dbias.reshape(CHUNK, SGU_HEADS, SGU_HEAD_DIM), axis=-1))
    token = hooks["on_small"](l, g)
    token = hooks["on_grads"](l, "mix", {"w_in": _atb(dz, sv["h1"], tag + "_wi", token), "w_glu": dwglu})
    return dx0, token


def _local_step(x, target, w, hooks):
    params = [_layer_params(w, l) for l in range(DEPTH)]
    saved = []
    h = x
    for l in range(DEPTH):
        p, tag = params[l], f"l{l}"
        _with_big(p, hooks["get_big"](l, "mix", [h]))
        za, zuv, h1, ob, pooled, oc = _mix_fwd(h, p, tag)
        oa, y, h_re, h_im = _s5_fwd(za, p, tag)
        _with_big(p, hooks["get_big"](l, "ffn", [oa, ob, oc]))
        x1, x2, h2, gt, up, ycat = _blk_fwd(h, oa, ob, oc, p, tag)
        saved.append(dict(x0=h, za=za, zuv=zuv, h1=h1, ycat=ycat, y=y, h_re=h_re, h_im=h_im, pooled=pooled, x1=x1,
                          h2=h2, gt=gt, up=up))
        h = x2
    dx, loss8, dgf8 = _head(h, target, w["g_final"].reshape(1, -1))
    grads = [None] * DEPTH

    def on_small(l, g_l):
        grads[l] = g_l
        if l > 0:
            return None
        g = {n: jnp.stack([grads[k][n] for k in range(DEPTH)]) for n in SMALL if n != "g_final"}
        g["g_final"] = _rows_sum(dgf8)
        return hooks["on_small"](g, loss8[0, 0])

    token = None
    for l in reversed(range(DEPTH)):
        dx, token = _layer_bwd(dx, saved[l], params[l], w, l, f"l{l}", dict(hooks, on_small=on_small), token)
    return dx


_ANY = pl.BlockSpec(memory_space=pl.ANY)
_MESH = pl.DeviceIdType.MESH


def _place():
    return lax.axis_index("x"), lax.axis_index("y"), lax.axis_index("c")


def _other_chips(x, y):
    return [(1 - x, y), (x, 1 - y), (1 - x, 1 - y)]


def _dma_sems(n):
    return pltpu.SemaphoreType.DMA((n,))


def _remote(src, dst, send_sems, recv_sems, k, to):
    return pltpu.make_async_remote_copy(src_ref=src, dst_ref=dst, send_sem=send_sems.at[k], recv_sem=recv_sems.at[k],
                                        device_id=to, device_id_type=_MESH)


_HBM = pl.BlockSpec(memory_space=pltpu.HBM)
_SEM = pl.BlockSpec(memory_space=pltpu.SEMAPHORE)
_EFFECT = pltpu.SideEffectType.DATAFLOW_SIDE_EFFECTING
N_REL = N_CHIPS - 1


def _gather_plan(x, y, c, srcs, lands):
    plan = []
    for l in lands:
        r = l.shape[0] // N_CHIPS
        rows = l.at[pl.ds((2 * x + y) * r, r)]
        plan += [(rows, rows, (cx, cy, c)) for cx, cy in _other_chips(x, y)]
    return plan


def _sibling_plan(x, y, c, srcs, lands):
    return [(s.at[:, 1 - c], l, (x, y, 1 - c)) for s, l in zip(srcs, lands)]


def _slab_plan(x, y, c, srcs, lands):
    return [(s.at[2 * cx + cy], l.at[j], (cx, cy, c))
            for s, l in zip(srcs, lands) for j, (cx, cy) in enumerate(_other_chips(x, y))]


def _plan_copies(plan, srcs, lands, send_sems, recv_sems):
    x, y, c = _place()
    return [_remote(s, d, send_sems, recv_sems, k, to) for k, (s, d, to) in enumerate(plan(x, y, c, srcs, lands))]


def _hbm(a):
    return pltpu.with_memory_space_constraint(a, pltpu.HBM)


def _everyone_plan(x, y, c, srcs, lands):
    me = 4 * x + 2 * y + c
    peers = [(x, y, 1 - c)] + [(cx, cy, cc) for cx, cy in _other_chips(x, y) for cc in (c, 1 - c)]
    return [(s, l.at[me], peer) for s, l in zip(srcs, lands) for peer in peers]


def _copies_start(name, plan, srcs, lands, ncopies):
    ns, n = len(srcs), len(srcs) + len(lands)

    def body(*refs):
        for cp in _plan_copies(plan, refs[:ns], refs[ns:n], refs[n], refs[n + 1]):
            cp.start()
        refs[-1][...] = jnp.zeros_like(refs[-1])

    ref_out = [pltpu.HBM(a.shape, a.dtype) for a in (*srcs, *lands)]
    out = pl.pallas_call(
        body, name=name, in_specs=[_HBM] * n,
        out_shape=(_dma_sems(ncopies), _dma_sems(ncopies), *ref_out, jax.ShapeDtypeStruct((SUBLANES, LANES), F32)),
        out_specs=(_SEM, _SEM, *[_HBM] * n, pl.BlockSpec(memory_space=pltpu.VMEM)),
        input_output_aliases={i: 2 + i for i in range(n)},
        compiler_params=pltpu.CompilerParams(has_side_effects=_EFFECT),
    )(*[_hbm(a) for a in (*srcs, *lands)])
    return dict(name=name, plan=plan, sems=out[:2], srcs=out[2:2 + ns], lands=out[2 + ns:2 + n], token=out[-1])


def _copies_wait(started, after):
    ns = len(started["srcs"])
    n = ns + len(started["lands"])
    plan = started["plan"]

    def body(*refs):
        for cp in _plan_copies(plan, refs[:ns], refs[ns:n], refs[n], refs[n + 1]):
            cp.wait_send()
            cp.wait_recv()

    args = (*started["srcs"], *started["lands"])
    out = pl.pallas_call(
        body, name=started["name"] + "_wait", out_shape=[pltpu.HBM(a.shape, a.dtype) for a in args],
        in_specs=[_HBM] * n + [_SEM, _SEM] + [_ANY] * len(after), out_specs=[_HBM] * n,
        input_output_aliases={i: i for i in range(n)},
        compiler_params=pltpu.CompilerParams(has_side_effects=_EFFECT),
    )(*args, *started["sems"], *after)
    return out[:ns], out[ns:]


def _place_shards(ws, layer, sel, after, tag):
    nw = len(ws)

    def body(sel_ref, *refs):
        for i in range(nw):
            refs[nw + len(after) + i][...] = refs[i][...].astype(BF16)

    return pl.pallas_call(
        body, grid_spec=pltpu.PrefetchScalarGridSpec(
            num_scalar_prefetch=1, grid=(1,),
            in_specs=[pl.BlockSpec((None,) + a.shape[1:], lambda i, s: (layer, 0, 0)) for a in ws] + [_ANY] * len(after),
            out_specs=[pl.BlockSpec(a.shape[1:], lambda i, s: (s[1], 0)) for a in ws]),
        out_shape=[jax.ShapeDtypeStruct((N_CHIPS * a.shape[1], a.shape[2]), BF16) for a in ws],
        name=f"place_shards_{tag}", compiler_params=_cp(dimension_semantics=("arbitrary",)),
    )(sel, *ws, *after)


def _share_halves(fs, layer, tag):
    nw = len(fs)

    def body(*refs):
        ins = refs[:nw]
        send_sems, recv_sems = refs[2 * nw:]
        x, y, c = _place()

        def half(i, who):
            h = ins[i].shape[1] // 2
            return ins[i].at[layer, pl.ds(who * h, h)]

        sends = [_remote(half(i, c), half(i, c), send_sems, recv_sems, i, (x, y, 1 - c)) for i in range(nw)]
        for cp in sends:
            cp.start()
        for i in range(nw):
            sends[i].wait_send()
            _remote(half(i, c), half(i, 1 - c), send_sems, recv_sems, i, (x, y, 1 - c)).wait_recv()

    return pl.pallas_call(
        body, out_shape=[jax.ShapeDtypeStruct(f.shape, f.dtype) for f in fs], in_specs=[_ANY] * nw,
        out_specs=[_ANY] * nw, input_output_aliases={i: i for i in range(nw)},
        scratch_shapes=[_dma_sems(nw), _dma_sems(nw)], name=f"share_halves_{tag}",
    )(*fs)


def _add_halves(g4s, recvs, sel, tag):
    nw = len(g4s)

    def body(sel_ref, *refs):
        for i in range(nw):
            refs[2 * nw + i][...] = (refs[i][...] + refs[nw + i][...]).astype(BF16)

    mine = [pl.BlockSpec((None, None) + g.shape[2:], lambda k, s: (k, s[0], 0, 0)) for g in g4s]
    slab = [pl.BlockSpec((None,) + g.shape[2:], lambda k, s: (k, 0, 0)) for g in g4s]
    return pl.pallas_call(
        body, grid_spec=pltpu.PrefetchScalarGridSpec(num_scalar_prefetch=1, grid=(N_CHIPS,), in_specs=mine + slab,
                                                     out_specs=slab),
        out_shape=[jax.ShapeDtypeStruct(r.shape, BF16) for r in recvs], name=f"add_halves_{tag}",
        compiler_params=_cp(dimension_semantics=("arbitrary",)),
    )(sel, *g4s, *recvs)


def _add_chips(ps, slabs, fs, layer, sel, tag):
    nw = len(ps)
    old = [f for f in fs if f is not None]

    def body(sel_ref, *refs):
        outs = refs[2 * nw + len(old):]
        for i in range(nw):
            acc = refs[i][...].astype(F32)
            for j in range(N_REL):
                acc = acc + refs[nw + i][j].astype(F32)
            outs[i][...] = acc

    shapes = [(DEPTH, 2 * p.shape[1], p.shape[2]) for p in ps]
    in_specs = [pl.BlockSpec((None,) + p.shape[1:], lambda i, s: (s[1], 0, 0)) for p in ps]
    in_specs += [pl.BlockSpec(sl.shape, lambda i, s: (0, 0, 0)) for sl in slabs]
    in_specs += [_ANY] * len(old)
    first_old = 1 + 2 * nw
    aliases, k = {}, 0
    for i, f in enumerate(fs):
        if f is not None:
            aliases[first_old + k] = i
            k += 1
    return pl.pallas_call(
        body, grid_spec=pltpu.PrefetchScalarGridSpec(
            num_scalar_prefetch=1, grid=(1,), in_specs=in_specs,
            out_specs=[pl.BlockSpec((None,) + p.shape[1:], lambda i, s: (layer, s[0], 0)) for p in ps]),
        out_shape=[jax.ShapeDtypeStruct(sh, F32) for sh in shapes], input_output_aliases=aliases,
        name=f"add_chips_{tag}", compiler_params=_cp(dimension_semantics=("arbitrary",)),
    )(sel, *ps, *slabs, *old)


def _adamw_math(w, g, m, v):
    m = ADAM_B1 * m + (1.0 - ADAM_B1) * g
    v = ADAM_B2 * v + (1.0 - ADAM_B2) * (g * g)
    m_hat = m / (1.0 - ADAM_B1 ** ADAM_STEP)
    v_hat = v / (1.0 - ADAM_B2 ** ADAM_STEP)
    delta = -ADAM_LR * (m_hat / (jnp.sqrt(v_hat) + ADAM_EPS) + ADAM_WD * w)
    return delta, m, v


ADAM_ROWS = 512


def _row_tile(rows, most):
    return max(t for t in range(SUBLANES, most + 1, SUBLANES) if rows % t == 0)


def _adamw(w, g, m, v, tag):
    depth, rows, cols = w.shape
    tr = _row_tile(rows, ADAM_ROWS)

    def body(w_ref, g_ref, m_ref, v_ref, d_ref, nm_ref, nv_ref):
        d, nm, nv = _adamw_math(w_ref[...], g_ref[...], m_ref[...], v_ref[...])
        d_ref[...] = d
        nm_ref[...] = nm
        nv_ref[...] = nv

    spec = pl.BlockSpec((None, tr, cols), lambda l, i: (l, i, 0))
    return pl.pallas_call(
        body, grid=(depth, rows // tr), in_specs=[spec] * 4, out_specs=[spec] * 3,
        out_shape=[jax.ShapeDtypeStruct(w.shape, F32)] * 3, name=f"adamw_{tag}",
        compiler_params=_cp(dimension_semantics=("arbitrary", "arbitrary")),
    )(w, g, m, v)


SMALL_TILE = 384
PRECISE = ("g_final",)
COARSE = [n for n in SMALL if n not in PRECISE]


def _small_reduce_adamw(gathered, w, m, v, tag):
    rows = w.shape[0]
    tr = math.gcd(rows, SMALL_TILE)

    def body(ga_ref, w_ref, m_ref, v_ref, g_ref, d_ref, nm_ref, nv_ref):
        g = ga_ref[0].astype(F32)
        for k in range(1, N_DEV):
            g = g + ga_ref[k].astype(F32)
        g_ref[...] = g
        d, nm, nv = _adamw_math(w_ref[...], g, m_ref[...], v_ref[...])
        d_ref[...] = d
        nm_ref[...] = nm
        nv_ref[...] = nv

    spec = _row(tr, LANES)
    return pl.pallas_call(
        body, grid=(rows // tr,),
        in_specs=[pl.BlockSpec((N_DEV, tr, LANES), lambda i: (0, i, 0)), spec, spec, spec], out_specs=[spec] * 4,
        out_shape=[jax.ShapeDtypeStruct((rows, LANES), F32)] * 4, name=f"small_reduce_adamw_{tag}",
        compiler_params=_cp(dimension_semantics=("arbitrary",)),
    )(gathered, w, m, v)


def _exchange_form(n, a):
    return jnp.swapaxes(a, 1, 2) if n in TRANSPOSED else a


PACK_ROWS = 16


def _pack(vals, names, extra=None):
    parts = [vals[n].reshape(-1) for n in names] + ([] if extra is None else [extra.reshape(1)])
    flat = jnp.concatenate(parts)
    rows = -(-flat.size // (LANES * PACK_ROWS)) * PACK_ROWS
    return jnp.pad(flat, (0, rows * LANES - flat.size)).reshape(rows, LANES)


def _unpack(buf, like, names):
    flat = buf.reshape(-1)
    out, off = {}, 0
    for n in names:
        out[n] = flat[off:off + like[n].size].reshape(like[n].shape)
        off += like[n].size
    return out, flat[off:]


def kernel(x, g_mix, w_in, A_re, A_im, log_dt, B_re, B_im, C_re, C_im, D_skip, w_glu, b_glu, w_pool, pool_scale, sgu_ln_g, sgu_ln_b, w_spatial, b_spatial, w_out, g_ffn, w_gate, w_up, w_down, g_final, loss_target, m_g_mix, m_w_in, m_A_re, m_A_im, m_log_dt, m_B_re, m_B_im, m_C_re, m_C_im, m_D_skip, m_w_glu, m_b_glu, m_w_pool, m_pool_scale, m_sgu_ln_g, m_sgu_ln_b, m_w_spatial, m_b_spatial, m_w_out, m_g_ffn, m_w_gate, m_w_up, m_w_down, m_g_final, v_g_mix, v_w_in, v_A_re, v_A_im, v_log_dt, v_B_re, v_B_im, v_C_re, v_C_im, v_D_skip, v_w_glu, v_b_glu, v_w_pool, v_pool_scale, v_sgu_ln_g, v_sgu_ln_b, v_w_spatial, v_b_spatial, v_w_out, v_g_ffn, v_w_gate, v_w_up, v_w_down, v_g_final):
    loc = locals()
    w = {n: loc[n] for n in WEIGHTS}
    m = {n: loc["m_" + n] for n in WEIGHTS}
    v = {n: loc["v_" + n] for n in WEIGHTS}
    sel = jnp.stack([lax.axis_index("c"), 2 * lax.axis_index("x") + lax.axis_index("y")]).astype(jnp.int32)

    chip = sel[1]

    halves = [(l, half) for l in range(DEPTH) for half in ("mix", "ffn")]
    names = {"mix": MIX_WEIGHTS, "ffn": FFN_WEIGHTS}
    started = {}
    wx = {n: _exchange_form(n, w[n]) for n in BIG}
    chain = []
    for l, half in halves:
        lands = _place_shards([wx[n] for n in names[half]], l, sel, chain, f"l{l}_{half}")
        started[l, half] = _copies_start(f"weights_l{l}_{half}", _gather_plan, [], lands, N_REL * len(lands))
        chain = [started[l, half]["token"]]
    w = dict(w, g_mix=_after(w["g_mix"], started[halves[-1]]["token"]))

    def get_big(l, half, after):
        return dict(zip(names[half], _copies_wait(started[l, half], after)[1]))

    result = {n: None for n in BIG}
    stage = {"swap": None, "slabs": None}

    def advance(after):
        if stage["slabs"] is not None:
            ex, ns, l, tag = stage["slabs"]
            part, slabs = _copies_wait(ex, after)
            bufs = _add_chips(part, slabs, [result[n] for n in ns], l, sel, tag)
            for n, f in zip(ns, _share_halves(bufs, l, tag)):
                result[n] = f
            stage["slabs"] = None
        if stage["swap"] is None:
            return None
        sw, ns, l, tag = stage["swap"]
        part = _add_halves(*_copies_wait(sw, after), sel, tag)
        slabs = [lax.empty((N_REL,) + p.shape[1:], BF16) for p in part]
        ex = _copies_start(f"grads_{tag}", _slab_plan, part, slabs, N_REL * len(part))
        stage["slabs"], stage["swap"] = (ex, ns, l, tag), None
        return ex["token"]

    def on_grads(l, half, grads):
        ns = list(grads)
        tag = f"l{l}_{half}"
        token = advance([grads[ns[0]]])
        g4s = [grads[n].reshape(N_CHIPS, 2, grads[n].shape[0] // (2 * N_CHIPS), grads[n].shape[1]) for n in ns]
        recvs = [lax.empty((N_CHIPS,) + g4.shape[2:], F32) for g4 in g4s]
        sw = _copies_start(f"swap_{tag}", _sibling_plan, g4s, recvs, len(g4s))
        stage["swap"] = (sw, ns, l, tag)
        return _join(token, sw["token"])

    small = {}

    def on_small(g, loss_local):
        me = 2 * chip + sel[0]
        blocks = [_pack(g, COARSE).astype(BF16), _pack(g, PRECISE, loss_local)]
        lands = [lax.dynamic_update_slice(lax.empty((N_DEV,) + b.shape, b.dtype), b[None], (me, 0, 0)) for b in blocks]
        small.update(_copies_start("small_grads", _everyone_plan, blocks, lands, (N_DEV - 1) * len(blocks)))
        return small["token"]

    dx = _local_step(x[0], loss_target[0], w, dict(get_big=get_big, on_grads=on_grads, tick=advance, on_small=on_small))
    advance([])
    advance([])
    grads, deltas, new_m, new_v = {}, {}, {}, {}
    for n in BIG:
        outs = _adamw(wx[n], result[n], _exchange_form(n, m[n]), _exchange_form(n, v[n]), n)
        grads[n], deltas[n], new_m[n], new_v[n] = [_exchange_form(n, a) for a in (result[n], *outs)]

    _, gathered = _copies_wait(small, [new_v[n] for n in BIG])
    zero = jnp.zeros((), F32)
    loss = None
    for names_k, extra, block, tag in ((COARSE, None, gathered[0], "coarse"), (PRECISE, zero, gathered[1], "precise")):
        outs = _small_reduce_adamw(block, _pack(w, names_k, extra), _pack(m, names_k, extra), _pack(v, names_k, extra),
                                   tag)
        for store, buf in zip((grads, deltas, new_m, new_v), outs):
            vals, rest = _unpack(buf, w, names_k)
            store.update(vals)
            if store is grads and extra is not None:
                loss = rest[0]
    return (loss, dx[None], *[grads[n] for n in WEIGHTS], *[deltas[n] for n in WEIGHTS],
            *[new_m[n] for n in WEIGHTS], *[new_v[n] for n in WEIGHTS])
```

```python
import math

import jax
import jax.numpy as jnp
from jax import lax
from jax.experimental import pallas as pl
from jax.experimental.pallas import tpu as pltpu

F32 = jnp.float32
BF16 = jnp.bfloat16

D_MODEL = 1024
DEPTH = 2
D_SSM = 384
SSM_GROUP = 16
N_GROUPS = 24
SSM_STATE = 64
N_STATE = N_GROUPS * SSM_STATE
POOL_WINDOWS = (2, 4, 8, 16)
POOL_GROUP = 64
D_POOL = 256
MAX_WINDOW = 16
SGU_HEADS = 6
SGU_HEAD_DIM = 64
D_SGU = 384
CHUNK = 128
D_IN = D_SSM + D_POOL + 2 * D_SGU
D_FF = 2816
EPS = 1e-6

ADAM_LR = 0.001
ADAM_B1 = 0.9
ADAM_B2 = 0.999
ADAM_EPS = 1e-08
ADAM_WD = 0.01
ADAM_STEP = 10

LANES = 128
SUBLANES = 8
VMEM_LIMIT = 56 * 1024 * 1024

TS = 512
TS_FFN = 256

WEIGHTS = ['g_mix', 'w_in', 'A_re', 'A_im', 'log_dt', 'B_re', 'B_im', 'C_re', 'C_im', 'D_skip', 'w_glu', 'b_glu',
           'w_pool', 'pool_scale', 'sgu_ln_g', 'sgu_ln_b', 'w_spatial', 'b_spatial', 'w_out', 'g_ffn', 'w_gate',
           'w_up', 'w_down', 'g_final']
BIG = ['w_in', 'w_glu', 'w_out', 'w_gate', 'w_up', 'w_down']
SMALL = [n for n in WEIGHTS if n not in BIG]
TRANSPOSED = ("w_in", "w_gate", "w_up")
N_CHIPS = 4
N_DEV = 8


def _cp(**kw):
    return pltpu.CompilerParams(vmem_limit_bytes=VMEM_LIMIT, **kw)


def _row(ts, n):
    return pl.BlockSpec((ts, n), lambda i: (i, 0))


def _const(shape):
    nd = len(shape)
    return pl.BlockSpec(shape, lambda i: (0,) * nd, pipeline_mode=pl.Buffered(1))


def _acc(shape):
    nd = len(shape)
    return pl.BlockSpec(shape, lambda i: (0,) * nd)


def _dot(a, b):
    return jnp.dot(a, b, preferred_element_type=F32)


def _dot_tn(a, b):
    return lax.dot_general(a, b, (((0,), (0,)), ((), ())), preferred_element_type=F32)


def _dot_nt(a, b):
    return lax.dot_general(a, b, (((1,), (1,)), ((), ())), preferred_element_type=F32)


_G0 = math.sqrt(2.0 / math.pi)
_G1 = 0.044715


def _gelu(x):
    return 0.5 * x * (1.0 + jnp.tanh(_G0 * (x + _G1 * x * x * x)))


def _gelu_and_grad(x):
    t = jnp.tanh(_G0 * (x + _G1 * x * x * x))
    half = 0.5 * (1.0 + t)
    return x * half, half + 0.5 * x * (1.0 - t * t) * (_G0 * (1.0 + 3.0 * _G1 * x * x))


def _sigmoid(x):
    return 1.0 / (1.0 + jnp.exp(-x))


def _rms(x):
    r = lax.rsqrt(jnp.mean(x * x, axis=-1, keepdims=True) + EPS)
    return x * r, r


def _rms_bwd(dh, n, r, g):
    dn = dh * g
    return r * (dn - n * jnp.mean(dn * n, axis=-1, keepdims=True)), dh * n


def _colsum8(v):
    rows, n = v.shape
    return jnp.sum(v.reshape(rows // SUBLANES, SUBLANES, n), axis=0)


def _cmul(ar, ai, br, bi):
    return ar * br - ai * bi, ar * bi + ai * br


def _cpow(ar, ai, n):
    assert n & (n - 1) == 0
    while n > 1:
        ar, ai = _cmul(ar, ai, ar, ai)
        n //= 2
    return ar, ai


N_USLAB = D_SSM // LANES
SEG = TS // SUBLANES
SLAB_STATES = N_STATE // N_USLAB
S5_IN = (N_USLAB, LANES, 2 * SLAB_STATES)
S5_OUT = (N_USLAB, 2 * SLAB_STATES, LANES)
STATE_TILE = (SUBLANES, N_STATE)


def _scan_order():
    p = jnp.arange(TS)
    src = (p % SUBLANES) * SEG + p // SUBLANES
    return (src[:, None] == jnp.arange(TS)[None, :]).astype(BF16)


def _to_scan_order(perm, v):
    hi = v.astype(BF16)
    lo = (v - hi.astype(F32)).astype(BF16)
    return _dot(perm, hi) + _dot(perm, lo)


def _scan_rows(k):
    return pl.ds(pl.multiple_of(k * SUBLANES, SUBLANES), SUBLANES)


def _lanes(v, j):
    return v[:, LANES * j:LANES * (j + 1)]


def _states(j):
    return pl.ds(SLAB_STATES * j, SLAB_STATES)


def _state_split(re_ref, im_ref, j, v):
    re_ref[:, _states(j)] = v[:, :SLAB_STATES]
    im_ref[:, _states(j)] = v[:, SLAB_STATES:]


def _state_cat(re_ref, im_ref, j):
    return jnp.concatenate([re_ref[:, _states(j)], im_ref[:, _states(j)]], axis=1).astype(BF16)


def _s5_fwd(u, p, tag):
    s = u.shape[0]
    seg = SEG

    def body(u_ref, perm_ref, bbt_ref, ar_ref, ai_ref, ct_ref, dsk_ref, wglu_ref, bglu_ref,
             oa_ref, y_ref, hr_ref, hi_ref, sr, si, er, ei, ir, ii, cr, ci):
        @pl.when(pl.program_id(0) == 0)
        def _():
            cr[...] = jnp.zeros_like(cr)
            ci[...] = jnp.zeros_like(ci)

        perm = perm_ref[...]
        uv = _to_scan_order(perm, u_ref[...])
        ub = uv.astype(BF16)
        for j in range(N_USLAB):
            _state_split(sr, si, j, _dot(_lanes(ub, j), bbt_ref[j]))
        ar = ar_ref[...]
        ai = ai_ref[...]

        def local(k, h):
            rows = _scan_rows(k)
            hr, hi = _cmul(ar, ai, h[0], h[1])
            return hr + sr[rows, :], hi + si[rows, :]

        zero = jnp.zeros(STATE_TILE, F32)
        e_r, e_i = lax.fori_loop(0, seg, local, (zero, zero))
        er[...] = e_r
        ei[...] = e_i
        pr, pi = _cpow(ar[0:1, :], ai[0:1, :], seg)
        c_r = cr[...]
        c_i = ci[...]
        for j in range(SUBLANES):
            ir[j:j + 1, :] = c_r
            ii[j:j + 1, :] = c_i
            n_r, n_i = _cmul(pr, pi, c_r, c_i)
            c_r = n_r + er[j:j + 1, :]
            c_i = n_i + ei[j:j + 1, :]
        cr[...] = c_r
        ci[...] = c_i

        def full(k, h):
            rows = _scan_rows(k)
            hr, hi = _cmul(ar, ai, h[0], h[1])
            hr = hr + sr[rows, :]
            hi = hi + si[rows, :]
            sr[rows, :] = hr
            si[rows, :] = hi
            return hr, hi

        lax.fori_loop(0, seg, full, (ir[...], ii[...]))
        hr_ref[...] = sr[...].astype(BF16)
        hi_ref[...] = si[...].astype(BF16)
        y = jnp.concatenate([_dot(_state_cat(sr, si, j), ct_ref[j]) for j in range(N_USLAB)], axis=1)
        y = y + dsk_ref[...] * uv
        y_ref[...] = y
        g = _gelu(y)
        pre = _dot(g.astype(BF16), wglu_ref[...]) + bglu_ref[...]
        oa_ref[...] = _dot_tn(perm, (g * _sigmoid(pre)).astype(BF16)).astype(BF16)

    return pl.pallas_call(
        body, grid=(s // TS,),
        in_specs=[_row(TS, D_SSM), _const((TS, TS)), _const(S5_IN), _const(STATE_TILE), _const(STATE_TILE),
                  _const(S5_OUT), _const((1, D_SSM)), _const((D_SSM, D_SSM)), _const((1, D_SSM))],
        out_specs=[_row(TS, D_SSM), _row(TS, D_SSM), _row(TS, N_STATE), _row(TS, N_STATE)],
        out_shape=[jax.ShapeDtypeStruct((s, D_SSM), BF16), jax.ShapeDtypeStruct((s, D_SSM), F32),
                   jax.ShapeDtypeStruct((s, N_STATE), BF16), jax.ShapeDtypeStruct((s, N_STATE), BF16)],
        scratch_shapes=[pltpu.VMEM((TS, N_STATE), F32), pltpu.VMEM((TS, N_STATE), F32),
                        pltpu.VMEM(STATE_TILE, F32), pltpu.VMEM(STATE_TILE, F32), pltpu.VMEM(STATE_TILE, F32),
                        pltpu.VMEM(STATE_TILE, F32), pltpu.VMEM((1, N_STATE), F32), pltpu.VMEM((1, N_STATE), F32)],
        name=f"s5_fwd_{tag}", compiler_params=_cp(dimension_semantics=("arbitrary",)),
    )(u, _scan_order(), p["bbt3"], p["a_re8"], p["a_im8"], p["ct3"], p["d_skip"], p["w_glu"], p["b_glu"])


def _pool_consts():
    w = jnp.repeat(jnp.asarray(POOL_WINDOWS, F32), POOL_GROUP)[None, :]
    return w


POOL_PAD = SUBLANES
POOL_ROWS = TS + MAX_WINDOW + POOL_PAD


def _window_sum(buf, tmp, first, wl, step):
    assert POOL_WINDOWS == (2, 4, 8, 16)
    n = TS + MAX_WINDOW
    lo = first - MAX_WINDOW if step < 0 else first
    src = buf
    for k, dst in zip((1, 2, 4), tmp):
        dst[pl.ds(lo, n), :] = src[pl.ds(lo, n), :] + src[pl.ds(lo + step * k, n), :]
        src = dst
    s2, s4, s8 = (t[pl.ds(first, TS), :] for t in tmp)
    s16 = s8 + tmp[2][pl.ds(first + step * 8, TS), :]
    return jnp.where(wl == 2, s2, jnp.where(wl == 4, s4, jnp.where(wl == 8, s8, s16)))


def _pool_count(i, rows, wl):
    t = (i * TS + 1).astype(F32) + lax.broadcasted_iota(jnp.int32, (rows, 1), 0).astype(F32)
    return jnp.minimum(t, wl)


def _sgu_mix(vl, wpair_ref, lo, hi):
    rows = vl.shape[0]
    chunks = []
    for c in range(rows // CHUNK):
        vc = vl[CHUNK * c:CHUNK * (c + 1), :]
        parts = []
        for q in range(SGU_HEADS // 2):
            vq = vc[:, LANES * q:LANES * (q + 1)]
            rhs = jnp.concatenate([vq * lo, vq * hi], axis=0).astype(BF16)
            parts.append(_dot(wpair_ref[q], rhs))
        chunks.append(jnp.concatenate(parts, axis=1))
    return jnp.concatenate(chunks, axis=0)


def _sgu_front(zuv, lng, lnb, grads=False):
    gelu = _gelu_and_grad if grads else lambda z: (_gelu(z), None)
    u, du = gelu(zuv[:, :D_SGU])
    v, dv = gelu(zuv[:, D_SGU:])
    mu = jnp.mean(v, axis=-1, keepdims=True)
    vc = v - mu
    rs = lax.rsqrt(jnp.mean(vc * vc, axis=-1, keepdims=True) + EPS)
    vn = vc * rs
    return u, vn, rs, vn * lng + lnb, du, dv


def _half_masks():
    lane = lax.broadcasted_iota(jnp.int32, (1, LANES), 1)
    lo = (lane < SGU_HEAD_DIM).astype(F32)
    return lo, 1.0 - lo


def _mix_fwd(x, p, tag):
    s = x.shape[0]

    def body(x_ref, g_ref, w_ref, wl_ref, wp_ref, sc_ref, lng_ref, lnb_ref, wsp_ref, bias_ref,
             za_ref, zuv_ref, h_ref, ob_ref, pooled_ref, oc_ref, buf, *tmp):
        i = pl.program_id(0)
        tile0 = POOL_PAD + MAX_WINDOW

        @pl.when(i == 0)
        def _():
            for ref in (buf, *tmp):
                ref[pl.ds(0, tile0), :] = jnp.zeros((tile0, D_POOL), F32)

        n, _ = _rms(x_ref[...])
        h = (n * g_ref[...]).astype(BF16)
        h_ref[...] = h
        z = _dot_nt(h, w_ref[...])
        za_ref[...] = z[:, :D_SSM]
        zb = z[:, D_SSM:D_SSM + D_POOL]
        zuv = z[:, D_SSM + D_POOL:]
        zuv_ref[...] = zuv
        buf[pl.ds(tile0, TS), :] = zb
        wl = wl_ref[...]
        pooled = (_window_sum(buf, tmp, tile0, wl, -1) / _pool_count(i, TS, wl) - zb).astype(BF16)
        buf[pl.ds(POOL_PAD, MAX_WINDOW), :] = zb[TS - MAX_WINDOW:, :]
        pooled_ref[...] = pooled
        ob_ref[...] = (_dot(pooled, wp_ref[...]) * sc_ref[...]).astype(BF16)
        lo, hi = _half_masks()
        u, _, _, vl, _, _ = _sgu_front(zuv, lng_ref[...], lnb_ref[...])
        mixed = _sgu_mix(vl, wsp_ref, lo, hi) + jnp.tile(bias_ref[...], (TS // CHUNK, 1))
        oc_ref[...] = (u * mixed).astype(BF16)

    return pl.pallas_call(
        body, grid=(s // TS,),
        in_specs=[_row(TS, D_MODEL), _const((1, D_MODEL)), _const((D_IN, D_MODEL)), _const((1, D_POOL)),
                  _const((D_POOL, D_POOL)), _const((1, D_POOL)), _const((1, D_SGU)), _const((1, D_SGU)),
                  _const((SGU_HEADS // 2, CHUNK, 2 * CHUNK)), _const((CHUNK, D_SGU))],
        out_specs=[_row(TS, D_SSM), _row(TS, 2 * D_SGU), _row(TS, D_MODEL), _row(TS, D_POOL), _row(TS, D_POOL),
                   _row(TS, D_SGU)],
        out_shape=[jax.ShapeDtypeStruct((s, D_SSM), F32), jax.ShapeDtypeStruct((s, 2 * D_SGU), F32),
                   jax.ShapeDtypeStruct((s, D_MODEL), BF16), jax.ShapeDtypeStruct((s, D_POOL), BF16),
                   jax.ShapeDtypeStruct((s, D_POOL), BF16), jax.ShapeDtypeStruct((s, D_SGU), BF16)],
        scratch_shapes=[pltpu.VMEM((POOL_ROWS, D_POOL), F32)] * 4,
        name=f"mix_fwd_{tag}", compiler_params=_cp(dimension_semantics=("arbitrary",)),
    )(x, p["g_mix"], p["w_in"], _pool_consts(), p["w_pool_bd"], p["pool_scale"], p["sgu_ln_g"], p["sgu_ln_b"],
      p["ws_pair"], p["bias_sp"])


def _blk_fwd(x0, oa, ob, oc, p, tag, head=None):
    s = x0.shape[0]
    ts = TS_FFN
    n_head = 0 if head is None else len(head)

    def body(x0_ref, oa_ref, ob_ref, oc_ref, wo_ref, g_ref, wg_ref, wu_ref, wd_ref, *refs):
        x1_ref, x2_ref, h2_ref, gt_ref, up_ref, ycat_ref = refs[n_head:n_head + 6]
        ycat = jnp.concatenate([oa_ref[...], ob_ref[...], oc_ref[...]], axis=1)
        ycat_ref[...] = ycat
        x1 = x0_ref[...] + _dot(ycat, wo_ref[...])
        x1_ref[...] = x1
        n, _ = _rms(x1)
        h2 = (n * g_ref[...]).astype(BF16)
        h2_ref[...] = h2
        gt = _dot_nt(h2, wg_ref[...])
        up = _dot_nt(h2, wu_ref[...])
        gt_ref[...] = gt.astype(BF16)
        up_ref[...] = up.astype(BF16)
        act = (gt * _sigmoid(gt) * up).astype(BF16)
        x2 = x1 + _dot(act, wd_ref[...])
        if head is None:
            x2_ref[...] = x2
            return
        t_ref, gf_ref = refs[:n_head]
        loss_ref, dgf_ref = refs[n_head + 6:]

        @pl.when(pl.program_id(0) == 0)
        def _():
            loss_ref[...] = jnp.zeros_like(loss_ref)
            dgf_ref[...] = jnp.zeros_like(dgf_ref)

        gf = gf_ref[...]
        nf, rf = _rms(x2)
        diff = nf * gf - t_ref[...]
        loss_ref[...] += jnp.sum(diff * diff) * (0.5 / D_MODEL)
        dxn, dgp = _rms_bwd(diff * (1.0 / D_MODEL), nf, rf, gf)
        dgf_ref[...] += _colsum8(dgp)
        x2_ref[...] = dxn

    in_specs = [_row(ts, D_MODEL), _row(ts, D_SSM), _row(ts, D_POOL), _row(ts, D_SGU), _const((D_MODEL, D_MODEL)),
                _const((1, D_MODEL)), _const((D_FF, D_MODEL)), _const((D_FF, D_MODEL)), _const((D_FF, D_MODEL))]
    out_specs = [_row(ts, D_MODEL), _row(ts, D_MODEL), _row(ts, D_MODEL), _row(ts, D_FF), _row(ts, D_FF),
                 _row(ts, D_MODEL)]
    out_shape = [jax.ShapeDtypeStruct((s, D_MODEL), F32), jax.ShapeDtypeStruct((s, D_MODEL), F32),
                 jax.ShapeDtypeStruct((s, D_MODEL), BF16), jax.ShapeDtypeStruct((s, D_FF), BF16),
                 jax.ShapeDtypeStruct((s, D_FF), BF16), jax.ShapeDtypeStruct((s, D_MODEL), BF16)]
    args = (x0, oa, ob, oc, p["w_out"], p["g_ffn"], p["w_gate"], p["w_up"], p["w_down"])
    if head is not None:
        in_specs += [_row(ts, D_MODEL), _const((1, D_MODEL))]
        out_specs += [_acc((SUBLANES, LANES)), _acc((SUBLANES, D_MODEL))]
        out_shape += [jax.ShapeDtypeStruct((SUBLANES, LANES), F32), jax.ShapeDtypeStruct((SUBLANES, D_MODEL), F32)]
        args += tuple(head)
    return pl.pallas_call(
        body, grid=(s // ts,), in_specs=in_specs, out_specs=out_specs, out_shape=out_shape,
        name=f"blk_fwd_{tag}", compiler_params=_cp(dimension_semantics=("arbitrary",)),
    )(*args)


def _blk_bwd(dx2, x1, gt, up, p, tag):
    s = dx2.shape[0]
    ts = TS_FFN

    def body(dx2_ref, x1_ref, gt_ref, up_ref, wd_ref, wgt_ref, wut_ref, wo_ref, g_ref,
             dx1_ref, da_ref, db_ref, dc_ref, dgt_ref, dup_ref, act_ref, dg_ref):
        @pl.when(pl.program_id(0) == 0)
        def _():
            dg_ref[...] = jnp.zeros_like(dg_ref)

        dx2v = dx2_ref[...]
        dact = _dot_nt(dx2v.astype(BF16), wd_ref[...])
        gf = gt_ref[...].astype(F32)
        uf = up_ref[...].astype(F32)
        sg = _sigmoid(gf)
        sl = gf * sg
        act_ref[...] = (sl * uf).astype(BF16)
        dgt = (dact * uf * (sg * (1.0 + gf * (1.0 - sg)))).astype(BF16)
        dup = (dact * sl).astype(BF16)
        dgt_ref[...] = dgt
        dup_ref[...] = dup
        dh2 = _dot(dgt, wgt_ref[...]) + _dot(dup, wut_ref[...])
        n, r = _rms(x1_ref[...])
        dxn, dgp = _rms_bwd(dh2, n, r, g_ref[...])
        dg_ref[...] += _colsum8(dgp)
        dx1 = dx2v + dxn
        dx1_ref[...] = dx1
        dy = _dot_nt(dx1.astype(BF16), wo_ref[...])
        da_ref[...] = dy[:, :D_SSM]
        db_ref[...] = dy[:, D_SSM:D_SSM + D_POOL]
        dc_ref[...] = dy[:, D_SSM + D_POOL:]

    return pl.pallas_call(
        body, grid=(s // ts,),
        in_specs=[_row(ts, D_MODEL), _row(ts, D_MODEL), _row(ts, D_FF), _row(ts, D_FF),
                  _const((D_FF, D_MODEL)), _const((D_FF, D_MODEL)), _const((D_FF, D_MODEL)),
                  _const((D_MODEL, D_MODEL)), _const((1, D_MODEL))],
        out_specs=[_row(ts, D_MODEL), _row(ts, D_SSM), _row(ts, D_POOL), _row(ts, D_SGU), _row(ts, D_FF),
                   _row(ts, D_FF), _row(ts, D_FF), _acc((SUBLANES, D_MODEL))],
        out_shape=[jax.ShapeDtypeStruct((s, D_MODEL), F32), jax.ShapeDtypeStruct((s, D_SSM), F32),
                   jax.ShapeDtypeStruct((s, D_POOL), F32), jax.ShapeDtypeStruct((s, D_SGU), F32),
                   jax.ShapeDtypeStruct((s, D_FF), BF16), jax.ShapeDtypeStruct((s, D_FF), BF16),
                   jax.ShapeDtypeStruct((s, D_FF), BF16), jax.ShapeDtypeStruct((SUBLANES, D_MODEL), F32)],
        name=f"blk_bwd_{tag}", compiler_params=_cp(dimension_semantics=("arbitrary",)),
    )(dx2, x1, gt, up, p["w_down"], p["w_gate"], p["w_up"], p["w_out"], p["g_ffn"])


def _s5_bwd(dout, u, y, h_re, h_im, p, tag):
    s = u.shape[0]
    nt = s // TS
    seg = SEG

    def rev(n):
        return pl.BlockSpec((TS, n), lambda i: (nt - 1 - i, 0))

    def body(do_ref, u_ref, y_ref, hr_ref, hi_ref, perm_ref, ar_ref, ai_ref, cb_ref, bb_ref, dsk_ref,
             wglu_ref, bglu_ref,
             du_ref, dct_ref, dbb_ref, dar_ref, dai_ref, dd_ref, dwglu_ref, dbglu_ref,
             gr, gi, hsr, hsi, er, ei, jr, ji, cr, ci):
        @pl.when(pl.program_id(0) == 0)
        def _():
            for ref in (cr, ci, dct_ref, dbb_ref, dar_ref, dai_ref, dd_ref, dwglu_ref, dbglu_ref):
                ref[...] = jnp.zeros_like(ref)

        perm = perm_ref[...]
        uv = _to_scan_order(perm, u_ref[...])
        yv = y_ref[...]
        dov = _to_scan_order(perm, do_ref[...])
        g, gelu_dy = _gelu_and_grad(yv)
        gb = g.astype(BF16)
        sg = _sigmoid(_dot(gb, wglu_ref[...]) + bglu_ref[...])
        dpre = dov * g * sg * (1.0 - sg)
        dpb = dpre.astype(BF16)
        dwglu_ref[...] += _dot_tn(gb, dpb)
        dbglu_ref[...] += _colsum8(dpre)
        dy = (dov * sg + _dot_nt(dpb, wglu_ref[...])) * gelu_dy
        dd_ref[...] += _colsum8(dy * uv)
        dyb = dy.astype(BF16)
        hsr[...] = hr_ref[...].astype(F32)
        hsi[...] = hi_ref[...].astype(F32)
        for j in range(N_USLAB):
            dct_ref[j] += _dot_tn(_state_cat(hsr, hsi, j), _lanes(dyb, j))
            _state_split(gr, gi, j, _dot(_lanes(dyb, j), cb_ref[j]))
        ar = ar_ref[...]
        ai = -ai_ref[...]

        def local(k, h):
            rows = _scan_rows(seg - 1 - k)
            nr, ni = _cmul(ar, ai, h[0], h[1])
            return nr + gr[rows, :], ni + gi[rows, :]

        zero = jnp.zeros(STATE_TILE, F32)
        e_r, e_i = lax.fori_loop(0, seg, local, (zero, zero))
        er[...] = e_r
        ei[...] = e_i
        pr, pi = _cpow(ar[0:1, :], ai[0:1, :], seg)
        c_r = cr[...]
        c_i = ci[...]
        for j in range(SUBLANES - 1, -1, -1):
            jr[j:j + 1, :] = c_r
            ji[j:j + 1, :] = c_i
            n_r, n_i = _cmul(pr, pi, c_r, c_i)
            c_r = n_r + er[j:j + 1, :]
            c_i = n_i + ei[j:j + 1, :]
        cr[...] = c_r
        ci[...] = c_i

        def full(k, carry):
            g_r, g_i, a_r, a_i = carry
            rows = _scan_rows(seg - 1 - k)
            h_r = hsr[rows, :]
            h_i = hsi[rows, :]
            a_r = a_r + g_r * h_r + g_i * h_i
            a_i = a_i + g_i * h_r - g_r * h_i
            nr, ni = _cmul(ar, ai, g_r, g_i)
            nr = nr + gr[rows, :]
            ni = ni + gi[rows, :]
            gr[rows, :] = nr
            gi[rows, :] = ni
            return nr, ni, a_r, a_i

        _, _, a_r, a_i = lax.fori_loop(0, seg, full, (jr[...], ji[...], zero, zero))
        dar_ref[...] += a_r
        dai_ref[...] += a_i
        ub = uv.astype(BF16)
        dus = []
        for j in range(N_USLAB):
            gb_j = _state_cat(gr, gi, j)
            dbb_ref[j] += _dot_tn(_lanes(ub, j), gb_j)
            dus.append(_dot(gb_j, bb_ref[j]))
        du = dy * dsk_ref[...] + jnp.concatenate(dus, axis=1)
        du_ref[...] = _dot_tn(perm, du.astype(BF16)).astype(BF16)

    big = (TS, N_STATE)
    return pl.pallas_call(
        body, grid=(nt,),
        in_specs=[rev(D_SSM), rev(D_SSM), rev(D_SSM), rev(N_STATE), rev(N_STATE), _const((TS, TS)),
                  _const(STATE_TILE), _const(STATE_TILE), _const(S5_IN), _const(S5_OUT), _const((1, D_SSM)),
                  _const((D_SSM, D_SSM)), _const((1, D_SSM))],
        out_specs=[rev(D_SSM), _acc(S5_OUT), _acc(S5_IN), _acc(STATE_TILE), _acc(STATE_TILE),
                   _acc((SUBLANES, D_SSM)), _acc((D_SSM, D_SSM)), _acc((SUBLANES, D_SSM))],
        out_shape=[jax.ShapeDtypeStruct((s, D_SSM), BF16), jax.ShapeDtypeStruct(S5_OUT, F32),
                   jax.ShapeDtypeStruct(S5_IN, F32), jax.ShapeDtypeStruct(STATE_TILE, F32),
                   jax.ShapeDtypeStruct(STATE_TILE, F32), jax.ShapeDtypeStruct((SUBLANES, D_SSM), F32),
                   jax.ShapeDtypeStruct((D_SSM, D_SSM), F32), jax.ShapeDtypeStruct((SUBLANES, D_SSM), F32)],
        scratch_shapes=[pltpu.VMEM(big, F32), pltpu.VMEM(big, F32), pltpu.VMEM(big, F32), pltpu.VMEM(big, F32),
                        pltpu.VMEM(STATE_TILE, F32), pltpu.VMEM(STATE_TILE, F32), pltpu.VMEM(STATE_TILE, F32),
                        pltpu.VMEM(STATE_TILE, F32), pltpu.VMEM((1, N_STATE), F32), pltpu.VMEM((1, N_STATE), F32)],
        name=f"s5_bwd_{tag}", compiler_params=_cp(dimension_semantics=("arbitrary",)),
    )(dout, u, y, h_re, h_im, _scan_order(), p["a_re8"], p["a_im8"], p["cb3"], p["bb3"], p["d_skip"], p["w_glu"],
      p["b_glu"])


def _mix_bwd(dza, db, dc, pooled, zuv, x0, dx1, p, tag):
    s = x0.shape[0]
    nt = s // TS

    def rev(n):
        return pl.BlockSpec((TS, n), lambda i: (nt - 1 - i, 0))

    def body(da_ref, db_ref, dc_ref, po_ref, z_ref, x_ref, dx1_ref, wl_ref, wp_ref, wpt_ref, sc_ref, lng_ref, lnb_ref,
             wsp_ref, wspt_ref, bias_ref, win_ref, g_ref,
             dx0_ref, dz_ref, dwp_ref, dsc_ref, dws_ref, dbias_ref, dlng_ref, dlnb_ref, dg_ref, buf, *tmp):
        step = pl.program_id(0)
        i = nt - 1 - step

        @pl.when(step == 0)
        def _():
            for ref in (dwp_ref, dsc_ref, dws_ref, dbias_ref, dlng_ref, dlnb_ref, dg_ref):
                ref[...] = jnp.zeros_like(ref)
            for ref in (buf, *tmp):
                ref[pl.ds(TS, POOL_ROWS - TS), :] = jnp.zeros((POOL_ROWS - TS, D_POOL), F32)

        wl = wl_ref[...]
        sc = sc_ref[...]
        dob = db_ref[...]
        pooled_b = po_ref[...]
        dsc_ref[...] += _colsum8(dob * _dot(pooled_b, wp_ref[...]))
        dmixb = (dob * sc).astype(BF16)
        dwp_ref[...] += _dot_tn(pooled_b, dmixb)
        dpool = _dot(dmixb, wpt_ref[...])
        dq = dpool / _pool_count(i, TS, wl)
        buf[pl.ds(0, TS), :] = dq
        dzb = _window_sum(buf, tmp, 0, wl, 1) - dpool
        buf[pl.ds(TS, MAX_WINDOW), :] = dq[:MAX_WINDOW, :]

        lo, hi = _half_masks()
        lng = lng_ref[...]
        u, vn, rs, vl, gelu_du, gelu_dv = _sgu_front(z_ref[...], lng, lnb_ref[...], grads=True)
        mixed = _sgu_mix(vl, wsp_ref, lo, hi) + jnp.tile(bias_ref[...], (TS // CHUNK, 1))
        doc = dc_ref[...]
        dzu = doc * mixed * gelu_du
        dmix = doc * u
        dbias = dbias_ref[...]
        for c in range(TS // CHUNK):
            dmc = dmix[CHUNK * c:CHUNK * (c + 1), :]
            dbias = dbias + dmc
            vlc = vl[CHUNK * c:CHUNK * (c + 1), :].astype(BF16)
            for q in range(SGU_HEADS // 2):
                dmq = _lanes(dmc, q)
                vq = _lanes(vlc, q)
                dws_ref[2 * q] += _dot_nt((dmq * lo).astype(BF16), vq)
                dws_ref[2 * q + 1] += _dot_nt((dmq * hi).astype(BF16), vq)
        dbias_ref[...] = dbias
        dvl = _sgu_mix(dmix, wspt_ref, lo, hi)
        dlng_ref[...] += _colsum8(dvl * vn)
        dlnb_ref[...] += _colsum8(dvl)
        dvn = dvl * lng
        dv = rs * (dvn - jnp.mean(dvn, axis=-1, keepdims=True) - vn * jnp.mean(dvn * vn, axis=-1, keepdims=True))

        dz = jnp.concatenate([da_ref[...], dzb.astype(BF16), dzu.astype(BF16), (dv * gelu_dv).astype(BF16)], axis=1)
        dz_ref[...] = dz
        n, r = _rms(x_ref[...])
        dxn, dgp = _rms_bwd(_dot(dz, win_ref[...]), n, r, g_ref[...])
        dg_ref[...] += _colsum8(dgp)
        dx0_ref[...] = dx1_ref[...] + dxn

    pair = (SGU_HEADS // 2, CHUNK, 2 * CHUNK)
    return pl.pallas_call(
        body, grid=(nt,),
        in_specs=[rev(D_SSM), rev(D_POOL), rev(D_SGU), rev(D_POOL), rev(2 * D_SGU), rev(D_MODEL), rev(D_MODEL),
                  _const((1, D_POOL)), _const((D_POOL, D_POOL)), _const((D_POOL, D_POOL)), _const((1, D_POOL)),
                  _const((1, D_SGU)), _const((1, D_SGU)), _const(pair), _const(pair), _const((CHUNK, D_SGU)),
                  _const((D_IN, D_MODEL)), _const((1, D_MODEL))],
        out_specs=[rev(D_MODEL), rev(D_IN), _acc((D_POOL, D_POOL)), _acc((SUBLANES, D_POOL)),
                   _acc((SGU_HEADS, CHUNK, CHUNK)), _acc((CHUNK, D_SGU)), _acc((SUBLANES, D_SGU)),
                   _acc((SUBLANES, D_SGU)), _acc((SUBLANES, D_MODEL))],
        out_shape=[jax.ShapeDtypeStruct((s, D_MODEL), F32), jax.ShapeDtypeStruct((s, D_IN), BF16),
                   jax.ShapeDtypeStruct((D_POOL, D_POOL), F32), jax.ShapeDtypeStruct((SUBLANES, D_POOL), F32),
                   jax.ShapeDtypeStruct((SGU_HEADS, CHUNK, CHUNK), F32), jax.ShapeDtypeStruct((CHUNK, D_SGU), F32),
                   jax.ShapeDtypeStruct((SUBLANES, D_SGU), F32), jax.ShapeDtypeStruct((SUBLANES, D_SGU), F32),
                   jax.ShapeDtypeStruct((SUBLANES, D_MODEL), F32)],
        scratch_shapes=[pltpu.VMEM((POOL_ROWS, D_POOL), F32)] * 4,
        name=f"mix_bwd_{tag}", compiler_params=_cp(dimension_semantics=("arbitrary",)),
    )(dza, db, dc, pooled, zuv, x0, dx1, _pool_consts(), p["w_pool_bd"], p["w_pool_bd_t"], p["pool_scale"],
      p["sgu_ln_g"], p["sgu_ln_b"], p["ws_pair"], p["ws_pair_t"], p["bias_sp"], p["w_in"], p["g_mix"])


def _atb(a, b, tag, token=None):
    s, ka = a.shape
    kb = b.shape[1]
    ts = ATB_ROWS
    tn = min(kb, ATB_COLS)
    ns = s // ts
    after = [] if token is None else [token]

    def body(a_ref, b_ref, *rest):
        o_ref = rest[-1]

        @pl.when(pl.program_id(1) == 0)
        def _():
            o_ref[...] = jnp.zeros_like(o_ref)

        o_ref[...] += _dot_tn(a_ref[...].astype(BF16), b_ref[...].astype(BF16))

    return pl.pallas_call(
        body, grid=(kb // tn, ns),
        in_specs=[pl.BlockSpec((ts, ka), lambda j, i: (i, 0)), pl.BlockSpec((ts, tn), lambda j, i: (i, j))]
        + [pl.BlockSpec(memory_space=pl.ANY)] * len(after),
        out_specs=pl.BlockSpec((ka, tn), lambda j, i: (0, j)),
        out_shape=jax.ShapeDtypeStruct((ka, kb), F32),
        name=f"atb_{tag}", compiler_params=_cp(dimension_semantics=("arbitrary", "arbitrary")),
    )(a, b, *after)


def _s5_discretise(a_re, a_im, log_dt, b_re, b_im):
    dt = jnp.exp(log_dt)[:, None]
    mag = jnp.exp(a_re * dt)
    ar = mag * jnp.cos(a_im * dt)
    ai = mag * jnp.sin(a_im * dt)
    den = a_re * a_re + a_im * a_im
    f_re = ((ar - 1.0) * a_re + ai * a_im) / den
    f_im = (ai * a_re - (ar - 1.0) * a_im) / den
    bb_re = f_re[..., None] * b_re - f_im[..., None] * b_im
    bb_im = f_re[..., None] * b_im + f_im[..., None] * b_re
    return ar, ai, bb_re, bb_im


def _block_diag(blocks):
    g, r, c = blocks.shape
    eye = jnp.eye(g, dtype=blocks.dtype)
    return (blocks[:, :, None, :] * eye[:, None, :, None]).reshape(g * r, g * c)


def _block_diag_extract(m, g):
    r = m.shape[0] // g
    c = m.shape[1] // g
    eye = jnp.eye(g, dtype=m.dtype)
    return jnp.sum(m.reshape(g, r, g, c) * eye[:, None, :, None], axis=2)


GROUPS_PER_SLAB = N_GROUPS // N_USLAB


def _slab_diag(blocks):
    k = GROUPS_PER_SLAB
    _, r, c = blocks.shape
    eye = jnp.eye(k, dtype=blocks.dtype)
    spread = blocks.reshape(N_USLAB, k, r, 1, c) * eye[None, :, None, :, None]
    return spread.reshape(N_USLAB, k * r, k * c)


def _slab_diag_extract(m):
    k = GROUPS_PER_SLAB
    r, c = m.shape[1] // k, m.shape[2] // k
    eye = jnp.eye(k, dtype=m.dtype)
    return jnp.sum(m.reshape(N_USLAB, k, r, k, c) * eye[None, :, None, :, None], axis=3).reshape(N_GROUPS, r, c)


def _state_slabs(v):
    return jnp.broadcast_to(v.reshape(1, N_STATE), STATE_TILE)


def _tril():
    return jnp.tril(jnp.ones((CHUNK, CHUNK), dtype=bool))


def _layer_params(w, l):
    row = lambda v: v.reshape(1, -1)
    t = lambda m: jnp.swapaxes(m, -1, -2)
    ar, ai, bb_re, bb_im = _s5_discretise(w["A_re"][l], w["A_im"][l], w["log_dt"][l], w["B_re"][l], w["B_im"][l])
    bbt3 = jnp.concatenate([_slab_diag(t(bb_re)), _slab_diag(t(bb_im))], axis=2).astype(BF16)
    ct3 = jnp.concatenate([_slab_diag(t(w["C_re"][l])), -_slab_diag(t(w["C_im"][l]))], axis=1).astype(BF16)
    ws = jnp.where(_tril()[None], w["w_spatial"][l], 0.0)
    pair = lambda m: jnp.stack([jnp.concatenate([m[2 * q], m[2 * q + 1]], axis=1)
                                for q in range(SGU_HEADS // 2)]).astype(BF16)
    wp = _block_diag(w["w_pool"][l]).astype(BF16)
    p = dict(
        g_mix=row(w["g_mix"][l]), g_ffn=row(w["g_ffn"][l]), d_skip=row(w["D_skip"][l]), b_glu=row(w["b_glu"][l]),
        pool_scale=row(w["pool_scale"][l]), sgu_ln_g=row(w["sgu_ln_g"][l]), sgu_ln_b=row(w["sgu_ln_b"][l]),
        a_re8=_state_slabs(ar), a_im8=_state_slabs(ai),
        bbt3=bbt3, bb3=t(bbt3), ct3=ct3, cb3=t(ct3),
        w_pool_bd=wp, w_pool_bd_t=t(wp), ws_pair=pair(ws), ws_pair_t=pair(t(ws)),
        bias_sp=jnp.repeat(t(w["b_spatial"][l]), SGU_HEAD_DIM, axis=1),
    )
    return p


MIX_WEIGHTS = ("w_in", "w_glu")
FFN_WEIGHTS = ("w_out", "w_gate", "w_up", "w_down")


def _with_big(p, mats):
    p.update(mats)


def _rows_sum(v):
    return jnp.sum(v, axis=0)


ATB_COLS = 1024
ATB_ROWS = 1024


def _after(v, token):
    return v if token is None else v + token[0, 0]


def _join(a, b):
    return b if a is None else a if b is None else a + b


def _layer_bwd(dx2, sv, p, w, l, tag, hooks, token):
    t = lambda m: jnp.swapaxes(m, -1, -2)
    dx1, da, db, dc, dgt, dup, act, dg_ffn = _blk_bwd(dx2, sv["x1"], sv["gt"], sv["up"],
                                                      dict(p, g_ffn=_after(p["g_ffn"], token)), tag)
    token = hooks["tick"]([dx1])
    token = _join(token, hooks["on_grads"](l, "ffn", {
        "w_down": _atb(act, dx2, tag + "_wd", token), "w_gate": _atb(dgt, sv["h2"], tag + "_wg", token),
        "w_up": _atb(dup, sv["h2"], tag + "_wu", token), "w_out": _atb(sv["ycat"], dx1, tag + "_wo", token)}))
    g = {}
    g["g_ffn"] = _rows_sum(dg_ffn)
    dza, dct3, dbbt3, dar8, dai8, dd8, dwglu, dbglu8 = _s5_bwd(
        da, sv["za"], sv["y"], sv["h_re"], sv["h_im"], dict(p, d_skip=_after(p["d_skip"], token)), tag)
    token = hooks["tick"]([dza])
    dx0, dz, dwp, dsc8, dws, dbias, dlng8, dlnb8, dg_mix = _mix_bwd(
        dza, db, dc, sv["pooled"], sv["zuv"], sv["x0"], dx1, dict(p, pool_scale=_after(p["pool_scale"], token)), tag)
    g["g_mix"] = _rows_sum(dg_mix)
    g["b_glu"] = _rows_sum(dbglu8)
    g["D_skip"] = _rows_sum(dd8)
    half = N_STATE // N_USLAB
    g["C_re"] = t(_slab_diag_extract(dct3[:, :half, :]))
    g["C_im"] = -t(_slab_diag_extract(dct3[:, half:, :]))
    dar = jnp.sum(dar8, axis=0).reshape(N_GROUPS, SSM_STATE)
    dai = jnp.sum(dai8, axis=0).reshape(N_GROUPS, SSM_STATE)
    dbb_re = t(_slab_diag_extract(dbbt3[:, :, :half]))
    dbb_im = t(_slab_diag_extract(dbbt3[:, :, half:]))
    _, disc_vjp = jax.vjp(_s5_discretise, w["A_re"][l], w["A_im"][l], w["log_dt"][l], w["B_re"][l], w["B_im"][l])
    g["A_re"], g["A_im"], g["log_dt"], g["B_re"], g["B_im"] = disc_vjp((dar, dai, dbb_re, dbb_im))
    g["w_pool"] = _block_diag_extract(dwp, len(POOL_WINDOWS))
    g["pool_scale"] = _rows_sum(dsc8)
    g["sgu_ln_g"] = _rows_sum(dlng8)
    g["sgu_ln_b"] = _rows_sum(dlnb8)
    g["w_spatial"] = jnp.where(_tril()[None], dws, 0.0)
    g["b_spatial"] = t(jnp.sum(dbias.reshape(CHUNK, SGU_HEADS, SGU_HEAD_DIM), axis=-1))
    token = hooks["on_small"](l, g)
    token = hooks["on_grads"](l, "mix", {"w_in": _atb(dz, sv["h1"], tag + "_wi", token), "w_glu": dwglu})
    return dx0, token


def _local_step(x, target, w, hooks):
    params = [_layer_params(w, l) for l in range(DEPTH)]
    saved = []
    h = x
    for l in range(DEPTH):
        p, tag = params[l], f"l{l}"
        _with_big(p, hooks["get_big"](l, "mix", [h]))
        za, zuv, h1, ob, pooled, oc = _mix_fwd(h, p, tag)
        oa, y, h_re, h_im = _s5_fwd(za, p, tag)
        _with_big(p, hooks["get_big"](l, "ffn", [oa, ob, oc]))
        head = (target, w["g_final"].reshape(1, -1)) if l == DEPTH - 1 else None
        x1, x2, h2, gt, up, ycat, *loss_parts = _blk_fwd(h, oa, ob, oc, p, tag, head)
        saved.append(dict(x0=h, za=za, zuv=zuv, h1=h1, ycat=ycat, y=y, h_re=h_re, h_im=h_im, pooled=pooled, x1=x1,
                          h2=h2, gt=gt, up=up))
        h = x2
    dx = h
    loss8, dgf8 = loss_parts
    grads = [None] * DEPTH

    def on_small(l, g_l):
        grads[l] = g_l
        if l > 0:
            return None
        g = {n: jnp.stack([grads[k][n] for k in range(DEPTH)]) for n in SMALL if n != "g_final"}
        g["g_final"] = _rows_sum(dgf8)
        return hooks["on_small"](g, loss8[0, 0])

    token = None
    for l in reversed(range(DEPTH)):
        dx, token = _layer_bwd(dx, saved[l], params[l], w, l, f"l{l}", dict(hooks, on_small=on_small), token)
    return dx


_ANY = pl.BlockSpec(memory_space=pl.ANY)
_MESH = pl.DeviceIdType.MESH


def _place():
    return lax.axis_index("x"), lax.axis_index("y"), lax.axis_index("c")


def _other_chips(x, y):
    return [(1 - x, y), (x, 1 - y), (1 - x, 1 - y)]


def _dma_sems(n):
    return pltpu.SemaphoreType.DMA((n,))


def _remote(src, dst, send_sems, recv_sems, k, to):
    return pltpu.make_async_remote_copy(src_ref=src, dst_ref=dst, send_sem=send_sems.at[k], recv_sem=recv_sems.at[k],
                                        device_id=to, device_id_type=_MESH)


_HBM = pl.BlockSpec(memory_space=pltpu.HBM)
_SEM = pl.BlockSpec(memory_space=pltpu.SEMAPHORE)
_EFFECT = pltpu.SideEffectType.DATAFLOW_SIDE_EFFECTING
N_REL = N_CHIPS - 1


def _gather_plan(x, y, c, srcs, lands):
    plan = []
    for l in lands:
        r = l.shape[0] // N_CHIPS
        rows = l.at[pl.ds((2 * x + y) * r, r)]
        plan += [(rows, rows, (cx, cy, c)) for cx, cy in _other_chips(x, y)]
    return plan


def _sibling_plan(x, y, c, srcs, lands):
    return [(s.at[:, 1 - c], l, (x, y, 1 - c)) for s, l in zip(srcs, lands)]


def _slab_plan(x, y, c, srcs, lands):
    return [(s.at[2 * cx + cy], l.at[j], (cx, cy, c))
            for s, l in zip(srcs, lands) for j, (cx, cy) in enumerate(_other_chips(x, y))]


def _plan_copies(plan, srcs, lands, send_sems, recv_sems):
    x, y, c = _place()
    return [_remote(s, d, send_sems, recv_sems, k, to) for k, (s, d, to) in enumerate(plan(x, y, c, srcs, lands))]


def _hbm(a):
    return pltpu.with_memory_space_constraint(a, pltpu.HBM)


def _everyone_plan(x, y, c, srcs, lands):
    me = 4 * x + 2 * y + c
    peers = [(x, y, 1 - c)] + [(cx, cy, cc) for cx, cy in _other_chips(x, y) for cc in (c, 1 - c)]
    return [(s, l.at[me], peer) for s, l in zip(srcs, lands) for peer in peers]


def _copies_start(name, plan, srcs, lands, ncopies):
    ns, n = len(srcs), len(srcs) + len(lands)

    def body(*refs):
        for cp in _plan_copies(plan, refs[:ns], refs[ns:n], refs[n], refs[n + 1]):
            cp.start()
        refs[-1][...] = jnp.zeros_like(refs[-1])

    ref_out = [pltpu.HBM(a.shape, a.dtype) for a in (*srcs, *lands)]
    out = pl.pallas_call(
        body, name=name, in_specs=[_HBM] * n,
        out_shape=(_dma_sems(ncopies), _dma_sems(ncopies), *ref_out, jax.ShapeDtypeStruct((SUBLANES, LANES), F32)),
        out_specs=(_SEM, _SEM, *[_HBM] * n, pl.BlockSpec(memory_space=pltpu.VMEM)),
        input_output_aliases={i: 2 + i for i in range(n)},
        compiler_params=pltpu.CompilerParams(has_side_effects=_EFFECT),
    )(*[_hbm(a) for a in (*srcs, *lands)])
    return dict(name=name, plan=plan, sems=out[:2], srcs=out[2:2 + ns], lands=out[2 + ns:2 + n], token=out[-1])


def _copies_wait(started, after):
    ns = len(started["srcs"])
    n = ns + len(started["lands"])
    plan = started["plan"]

    def body(*refs):
        for cp in _plan_copies(plan, refs[:ns], refs[ns:n], refs[n], refs[n + 1]):
            cp.wait_send()
            cp.wait_recv()

    args = (*started["srcs"], *started["lands"])
    out = pl.pallas_call(
        body, name=started["name"] + "_wait", out_shape=[pltpu.HBM(a.shape, a.dtype) for a in args],
        in_specs=[_HBM] * n + [_SEM, _SEM] + [_ANY] * len(after), out_specs=[_HBM] * n,
        input_output_aliases={i: i for i in range(n)},
        compiler_params=pltpu.CompilerParams(has_side_effects=_EFFECT),
    )(*args, *started["sems"], *after)
    return out[:ns], out[ns:]


def _place_shards(ws, layer, sel, after, tag):
    nw = len(ws)

    def body(sel_ref, *refs):
        for i in range(nw):
            refs[nw + len(after) + i][...] = refs[i][...].astype(BF16)

    return pl.pallas_call(
        body, grid_spec=pltpu.PrefetchScalarGridSpec(
            num_scalar_prefetch=1, grid=(1,),
            in_specs=[pl.BlockSpec((None,) + a.shape[1:], lambda i, s: (layer, 0, 0)) for a in ws] + [_ANY] * len(after),
            out_specs=[pl.BlockSpec(a.shape[1:], lambda i, s: (s[1], 0)) for a in ws]),
        out_shape=[jax.ShapeDtypeStruct((N_CHIPS * a.shape[1], a.shape[2]), BF16) for a in ws],
        name=f"place_shards_{tag}", compiler_params=_cp(dimension_semantics=("arbitrary",)),
    )(sel, *ws, *after)


def _share_halves(fs, layer, tag):
    nw = len(fs)

    def body(*refs):
        ins = refs[:nw]
        send_sems, recv_sems = refs[2 * nw:]
        x, y, c = _place()

        def half(i, who):
            h = ins[i].shape[1] // 2
            return ins[i].at[layer, pl.ds(who * h, h)]

        sends = [_remote(half(i, c), half(i, c), send_sems, recv_sems, i, (x, y, 1 - c)) for i in range(nw)]
        for cp in sends:
            cp.start()
        for i in range(nw):
            sends[i].wait_send()
            _remote(half(i, c), half(i, 1 - c), send_sems, recv_sems, i, (x, y, 1 - c)).wait_recv()

    return pl.pallas_call(
        body, out_shape=[jax.ShapeDtypeStruct(f.shape, f.dtype) for f in fs], in_specs=[_ANY] * nw,
        out_specs=[_ANY] * nw, input_output_aliases={i: i for i in range(nw)},
        scratch_shapes=[_dma_sems(nw), _dma_sems(nw)], name=f"share_halves_{tag}",
    )(*fs)


def _add_halves(g4s, recvs, sel, tag):
    nw = len(g4s)

    def body(sel_ref, *refs):
        for i in range(nw):
            refs[2 * nw + i][...] = (refs[i][...] + refs[nw + i][...]).astype(BF16)

    mine = [pl.BlockSpec((None, None) + g.shape[2:], lambda k, s: (k, s[0], 0, 0)) for g in g4s]
    slab = [pl.BlockSpec((None,) + g.shape[2:], lambda k, s: (k, 0, 0)) for g in g4s]
    return pl.pallas_call(
        body, grid_spec=pltpu.PrefetchScalarGridSpec(num_scalar_prefetch=1, grid=(N_CHIPS,), in_specs=mine + slab,
                                                     out_specs=slab),
        out_shape=[jax.ShapeDtypeStruct(r.shape, BF16) for r in recvs], name=f"add_halves_{tag}",
        compiler_params=_cp(dimension_semantics=("arbitrary",)),
    )(sel, *g4s, *recvs)


def _add_chips(ps, slabs, fs, layer, sel, tag):
    nw = len(ps)
    old = [f for f in fs if f is not None]

    def body(sel_ref, *refs):
        outs = refs[2 * nw + len(old):]
        for i in range(nw):
            acc = refs[i][...].astype(F32)
            for j in range(N_REL):
                acc = acc + refs[nw + i][j].astype(F32)
            outs[i][...] = acc

    shapes = [(DEPTH, 2 * p.shape[1], p.shape[2]) for p in ps]
    in_specs = [pl.BlockSpec((None,) + p.shape[1:], lambda i, s: (s[1], 0, 0)) for p in ps]
    in_specs += [pl.BlockSpec(sl.shape, lambda i, s: (0, 0, 0)) for sl in slabs]
    in_specs += [_ANY] * len(old)
    first_old = 1 + 2 * nw
    aliases, k = {}, 0
    for i, f in enumerate(fs):
        if f is not None:
            aliases[first_old + k] = i
            k += 1
    return pl.pallas_call(
        body, grid_spec=pltpu.PrefetchScalarGridSpec(
            num_scalar_prefetch=1, grid=(1,), in_specs=in_specs,
            out_specs=[pl.BlockSpec((None,) + p.shape[1:], lambda i, s: (layer, s[0], 0)) for p in ps]),
        out_shape=[jax.ShapeDtypeStruct(sh, F32) for sh in shapes], input_output_aliases=aliases,
        name=f"add_chips_{tag}", compiler_params=_cp(dimension_semantics=("arbitrary",)),
    )(sel, *ps, *slabs, *old)


def _adamw_math(w, g, m, v):
    m = ADAM_B1 * m + (1.0 - ADAM_B1) * g
    v = ADAM_B2 * v + (1.0 - ADAM_B2) * (g * g)
    m_hat = m / (1.0 - ADAM_B1 ** ADAM_STEP)
    v_hat = v / (1.0 - ADAM_B2 ** ADAM_STEP)
    delta = -ADAM_LR * (m_hat / (jnp.sqrt(v_hat) + ADAM_EPS) + ADAM_WD * w)
    return delta, m, v


ADAM_ROWS = 512


def _row_tile(rows, most):
    return max(t for t in range(SUBLANES, most + 1, SUBLANES) if rows % t == 0)


def _adamw(w, g, m, v, tag):
    depth, rows, cols = w.shape
    tr = _row_tile(rows, ADAM_ROWS)

    def body(w_ref, g_ref, m_ref, v_ref, d_ref, nm_ref, nv_ref):
        d, nm, nv = _adamw_math(w_ref[...], g_ref[...], m_ref[...], v_ref[...])
        d_ref[...] = d
        nm_ref[...] = nm
        nv_ref[...] = nv

    spec = pl.BlockSpec((None, tr, cols), lambda l, i: (l, i, 0))
    return pl.pallas_call(
        body, grid=(depth, rows // tr), in_specs=[spec] * 4, out_specs=[spec] * 3,
        out_shape=[jax.ShapeDtypeStruct(w.shape, F32)] * 3, name=f"adamw_{tag}",
        compiler_params=_cp(dimension_semantics=("arbitrary", "arbitrary")),
    )(w, g, m, v)


SMALL_TILE = 384
PRECISE = ("g_final",)
COARSE = [n for n in SMALL if n not in PRECISE]


def _small_reduce_adamw(gathered, w, m, v, tag):
    rows = w.shape[0]
    tr = math.gcd(rows, SMALL_TILE)

    def body(ga_ref, w_ref, m_ref, v_ref, g_ref, d_ref, nm_ref, nv_ref):
        g = ga_ref[0].astype(F32)
        for k in range(1, N_DEV):
            g = g + ga_ref[k].astype(F32)
        g_ref[...] = g
        d, nm, nv = _adamw_math(w_ref[...], g, m_ref[...], v_ref[...])
        d_ref[...] = d
        nm_ref[...] = nm
        nv_ref[...] = nv

    spec = _row(tr, LANES)
    return pl.pallas_call(
        body, grid=(rows // tr,),
        in_specs=[pl.BlockSpec((N_DEV, tr, LANES), lambda i: (0, i, 0)), spec, spec, spec], out_specs=[spec] * 4,
        out_shape=[jax.ShapeDtypeStruct((rows, LANES), F32)] * 4, name=f"small_reduce_adamw_{tag}",
        compiler_params=_cp(dimension_semantics=("arbitrary",)),
    )(gathered, w, m, v)


def _exchange_form(n, a):
    return jnp.swapaxes(a, 1, 2) if n in TRANSPOSED else a


PACK_ROWS = 16


def _pack(vals, names, extra=None):
    parts = [vals[n].reshape(-1) for n in names] + ([] if extra is None else [extra.reshape(1)])
    flat = jnp.concatenate(parts)
    rows = -(-flat.size // (LANES * PACK_ROWS)) * PACK_ROWS
    return jnp.pad(flat, (0, rows * LANES - flat.size)).reshape(rows, LANES)


def _unpack(buf, like, names):
    flat = buf.reshape(-1)
    out, off = {}, 0
    for n in names:
        out[n] = flat[off:off + like[n].size].reshape(like[n].shape)
        off += like[n].size
    return out, flat[off:]


def kernel(x, g_mix, w_in, A_re, A_im, log_dt, B_re, B_im, C_re, C_im, D_skip, w_glu, b_glu, w_pool, pool_scale, sgu_ln_g, sgu_ln_b, w_spatial, b_spatial, w_out, g_ffn, w_gate, w_up, w_down, g_final, loss_target, m_g_mix, m_w_in, m_A_re, m_A_im, m_log_dt, m_B_re, m_B_im, m_C_re, m_C_im, m_D_skip, m_w_glu, m_b_glu, m_w_pool, m_pool_scale, m_sgu_ln_g, m_sgu_ln_b, m_w_spatial, m_b_spatial, m_w_out, m_g_ffn, m_w_gate, m_w_up, m_w_down, m_g_final, v_g_mix, v_w_in, v_A_re, v_A_im, v_log_dt, v_B_re, v_B_im, v_C_re, v_C_im, v_D_skip, v_w_glu, v_b_glu, v_w_pool, v_pool_scale, v_sgu_ln_g, v_sgu_ln_b, v_w_spatial, v_b_spatial, v_w_out, v_g_ffn, v_w_gate, v_w_up, v_w_down, v_g_final):
    loc = locals()
    w = {n: loc[n] for n in WEIGHTS}
    m = {n: loc["m_" + n] for n in WEIGHTS}
    v = {n: loc["v_" + n] for n in WEIGHTS}
    sel = jnp.stack([lax.axis_index("c"), 2 * lax.axis_index("x") + lax.axis_index("y")]).astype(jnp.int32)

    chip = sel[1]

    halves = [(l, half) for l in range(DEPTH) for half in ("mix", "ffn")]
    names = {"mix": MIX_WEIGHTS, "ffn": FFN_WEIGHTS}
    started = {}
    wx = {n: _exchange_form(n, w[n]) for n in BIG}
    chain = []
    for l, half in halves:
        lands = _place_shards([wx[n] for n in names[half]], l, sel, chain, f"l{l}_{half}")
        started[l, half] = _copies_start(f"weights_l{l}_{half}", _gather_plan, [], lands, N_REL * len(lands))
        chain = [started[l, half]["token"]]
    w = dict(w, g_mix=_after(w["g_mix"], started[halves[-1]]["token"]))

    def get_big(l, half, after):
        return dict(zip(names[half], _copies_wait(started[l, half], after)[1]))

    result = {n: None for n in BIG}
    stage = {"swap": None, "slabs": None}

    def advance(after):
        if stage["slabs"] is not None:
            ex, ns, l, tag = stage["slabs"]
            part, slabs = _copies_wait(ex, after)
            bufs = _add_chips(part, slabs, [result[n] for n in ns], l, sel, tag)
            for n, f in zip(ns, _share_halves(bufs, l, tag)):
                result[n] = f
            stage["slabs"] = None
        if stage["swap"] is None:
            return None
        sw, ns, l, tag = stage["swap"]
        part = _add_halves(*_copies_wait(sw, after), sel, tag)
        slabs = [lax.empty((N_REL,) + p.shape[1:], BF16) for p in part]
        ex = _copies_start(f"grads_{tag}", _slab_plan, part, slabs, N_REL * len(part))
        stage["slabs"], stage["swap"] = (ex, ns, l, tag), None
        return ex["token"]

    def on_grads(l, half, grads):
        ns = list(grads)
        tag = f"l{l}_{half}"
        token = advance([grads[ns[0]]])
        g4s = [grads[n].reshape(N_CHIPS, 2, grads[n].shape[0] // (2 * N_CHIPS), grads[n].shape[1]) for n in ns]
        recvs = [lax.empty((N_CHIPS,) + g4.shape[2:], F32) for g4 in g4s]
        sw = _copies_start(f"swap_{tag}", _sibling_plan, g4s, recvs, len(g4s))
        stage["swap"] = (sw, ns, l, tag)
        return _join(token, sw["token"])

    small = {}

    def on_small(g, loss_local):
        me = 2 * chip + sel[0]
        blocks = [_pack(g, COARSE).astype(BF16), _pack(g, PRECISE, loss_local)]
        lands = [lax.dynamic_update_slice(lax.empty((N_DEV,) + b.shape, b.dtype), b[None], (me, 0, 0)) for b in blocks]
        small.update(_copies_start("small_grads", _everyone_plan, blocks, lands, (N_DEV - 1) * len(blocks)))
        return small["token"]

    dx = _local_step(x[0], loss_target[0], w, dict(get_big=get_big, on_grads=on_grads, tick=advance, on_small=on_small))
    advance([])
    advance([])
    grads, deltas, new_m, new_v = {}, {}, {}, {}
    for n in BIG:
        outs = _adamw(wx[n], result[n], _exchange_form(n, m[n]), _exchange_form(n, v[n]), n)
        grads[n], deltas[n], new_m[n], new_v[n] = [_exchange_form(n, a) for a in (result[n], *outs)]

    _, gathered = _copies_wait(small, [new_v[n] for n in BIG])
    zero = jnp.zeros((), F32)
    loss = None
    for names_k, extra, block, tag in ((COARSE, None, gathered[0], "coarse"), (PRECISE, zero, gathered[1], "precise")):
        outs = _small_reduce_adamw(block, _pack(w, names_k, extra), _pack(m, names_k, extra), _pack(v, names_k, extra),
                                   tag)
        for store, buf in zip((grads, deltas, new_m, new_v), outs):
            vals, rest = _unpack(buf, w, names_k)
            store.update(vals)
            if store is grads and extra is not None:
                loss = rest[0]
    return (loss, dx[None], *[grads[n] for n in WEIGHTS], *[deltas[n] for n in WEIGHTS],
            *[new_m[n] for n in WEIGHTS], *[new_v[n] for n in WEIGHTS])
```

```python
import math

import jax
import jax.numpy as jnp
from jax import lax
from jax.experimental import pallas as pl
from jax.experimental.pallas import tpu as pltpu

F32 = jnp.float32
BF16 = jnp.bfloat16

D_MODEL = 1024
DEPTH = 2
D_SSM = 384
SSM_GROUP = 16
N_GROUPS = 24
SSM_STATE = 64
N_STATE = N_GROUPS * SSM_STATE
POOL_WINDOWS = (2, 4, 8, 16)
POOL_GROUP = 64
D_POOL = 256
MAX_WINDOW = 16
SGU_HEADS = 6
SGU_HEAD_DIM = 64
D_SGU = 384
CHUNK = 128
D_IN = D_SSM + D_POOL + 2 * D_SGU
D_FF = 2816
EPS = 1e-6

ADAM_LR = 0.001
ADAM_B1 = 0.9
ADAM_B2 = 0.999
ADAM_EPS = 1e-08
ADAM_WD = 0.01
ADAM_STEP = 10

LANES = 128
SUBLANES = 8
VMEM_LIMIT = 56 * 1024 * 1024

TS = 512
TS_FFN = 256

WEIGHTS = ['g_mix', 'w_in', 'A_re', 'A_im', 'log_dt', 'B_re', 'B_im', 'C_re', 'C_im', 'D_skip', 'w_glu', 'b_glu',
           'w_pool', 'pool_scale', 'sgu_ln_g', 'sgu_ln_b', 'w_spatial', 'b_spatial', 'w_out', 'g_ffn', 'w_gate',
           'w_up', 'w_down', 'g_final']
BIG = ['w_in', 'w_glu', 'w_out', 'w_gate', 'w_up', 'w_down']
SMALL = [n for n in WEIGHTS if n not in BIG]
TRANSPOSED = ("w_in", "w_gate", "w_up")
N_CHIPS = 4
N_DEV = 8


def _cp(**kw):
    return pltpu.CompilerParams(vmem_limit_bytes=VMEM_LIMIT, **kw)


def _row(ts, n):
    return pl.BlockSpec((ts, n), lambda i: (i, 0))


def _const(shape):
    nd = len(shape)
    return pl.BlockSpec(shape, lambda i: (0,) * nd, pipeline_mode=pl.Buffered(1))


def _acc(shape):
    nd = len(shape)
    return pl.BlockSpec(shape, lambda i: (0,) * nd)


def _dot(a, b):
    return jnp.dot(a, b, preferred_element_type=F32)


def _dot_tn(a, b):
    return lax.dot_general(a, b, (((0,), (0,)), ((), ())), preferred_element_type=F32)


def _dot_nt(a, b):
    return lax.dot_general(a, b, (((1,), (1,)), ((), ())), preferred_element_type=F32)


_G0 = math.sqrt(2.0 / math.pi)
_G1 = 0.044715


def _gelu(x):
    return 0.5 * x * (1.0 + jnp.tanh(_G0 * (x + _G1 * x * x * x)))


def _gelu_and_grad(x):
    t = jnp.tanh(_G0 * (x + _G1 * x * x * x))
    half = 0.5 * (1.0 + t)
    return x * half, half + 0.5 * x * (1.0 - t * t) * (_G0 * (1.0 + 3.0 * _G1 * x * x))


def _sigmoid(x):
    return 1.0 / (1.0 + jnp.exp(-x))


def _rms(x):
    r = lax.rsqrt(jnp.mean(x * x, axis=-1, keepdims=True) + EPS)
    return x * r, r


def _rms_bwd(dh, n, r, g):
    dn = dh * g
    return r * (dn - n * jnp.mean(dn * n, axis=-1, keepdims=True)), dh * n


def _colsum8(v):
    rows, n = v.shape
    return jnp.sum(v.reshape(rows // SUBLANES, SUBLANES, n), axis=0)


def _cmul(ar, ai, br, bi):
    return ar * br - ai * bi, ar * bi + ai * br


def _cpow(ar, ai, n):
    assert n & (n - 1) == 0
    while n > 1:
        ar, ai = _cmul(ar, ai, ar, ai)
        n //= 2
    return ar, ai


N_USLAB = D_SSM // LANES
SEG = TS // SUBLANES
SLAB_STATES = N_STATE // N_USLAB
S5_IN = (N_USLAB, LANES, 2 * SLAB_STATES)
S5_OUT = (N_USLAB, 2 * SLAB_STATES, LANES)
STATE_TILE = (SUBLANES, N_STATE)


def _scan_order():
    p = jnp.arange(TS)
    src = (p % SUBLANES) * SEG + p // SUBLANES
    return (src[:, None] == jnp.arange(TS)[None, :]).astype(BF16)


def _to_scan_order(perm, v):
    hi = v.astype(BF16)
    lo = (v - hi.astype(F32)).astype(BF16)
    return _dot(perm, hi) + _dot(perm, lo)


def _scan_rows(k):
    return pl.ds(pl.multiple_of(k * SUBLANES, SUBLANES), SUBLANES)


def _lanes(v, j):
    return v[:, LANES * j:LANES * (j + 1)]


def _states(j):
    return pl.ds(SLAB_STATES * j, SLAB_STATES)


def _state_split(re_ref, im_ref, j, v):
    re_ref[:, _states(j)] = v[:, :SLAB_STATES]
    im_ref[:, _states(j)] = v[:, SLAB_STATES:]


def _state_cat(re_ref, im_ref, j):
    return jnp.concatenate([re_ref[:, _states(j)], im_ref[:, _states(j)]], axis=1).astype(BF16)


def _s5_fwd(u, p, tag):
    s = u.shape[0]
    seg = SEG

    def body(u_ref, perm_ref, bbt_ref, ar_ref, ai_ref, ct_ref, dsk_ref, wglu_ref, bglu_ref,
             oa_ref, y_ref, hr_ref, hi_ref, sr, si, er, ei, ir, ii, cr, ci):
        @pl.when(pl.program_id(0) == 0)
        def _():
            cr[...] = jnp.zeros_like(cr)
            ci[...] = jnp.zeros_like(ci)

        perm = perm_ref[...]
        uv = _to_scan_order(perm, u_ref[...])
        ub = uv.astype(BF16)
        for j in range(N_USLAB):
            _state_split(sr, si, j, _dot(_lanes(ub, j), bbt_ref[j]))
        for j in range(N_USLAB):
            cols = _states(j)
            ar = ar_ref[:, cols]
            ai = ai_ref[:, cols]
            h_r = h_i = jnp.zeros((SUBLANES, SLAB_STATES), F32)
            for k in range(seg):
                rows = pl.ds(SUBLANES * k, SUBLANES)
                n_r, n_i = _cmul(ar, ai, h_r, h_i)
                h_r = n_r + sr[rows, cols]
                h_i = n_i + si[rows, cols]
            er[:, cols] = h_r
            ei[:, cols] = h_i
            pr, pi = _cpow(ar[0:1, :], ai[0:1, :], seg)
            c_r = cr[:, cols]
            c_i = ci[:, cols]
            for q in range(SUBLANES):
                ir[q:q + 1, cols] = c_r
                ii[q:q + 1, cols] = c_i
                n_r, n_i = _cmul(pr, pi, c_r, c_i)
                c_r = n_r + er[q:q + 1, cols]
                c_i = n_i + ei[q:q + 1, cols]
            cr[:, cols] = c_r
            ci[:, cols] = c_i
            h_r = ir[:, cols]
            h_i = ii[:, cols]
            for k in range(seg):
                rows = pl.ds(SUBLANES * k, SUBLANES)
                n_r, n_i = _cmul(ar, ai, h_r, h_i)
                h_r = n_r + sr[rows, cols]
                h_i = n_i + si[rows, cols]
                sr[rows, cols] = h_r
                si[rows, cols] = h_i
        hr_ref[...] = sr[...].astype(BF16)
        hi_ref[...] = si[...].astype(BF16)
        y = jnp.concatenate([_dot(_state_cat(sr, si, j), ct_ref[j]) for j in range(N_USLAB)], axis=1)
        y = y + dsk_ref[...] * uv
        y_ref[...] = y
        g = _gelu(y)
        pre = _dot(g.astype(BF16), wglu_ref[...]) + bglu_ref[...]
        oa_ref[...] = _dot_tn(perm, (g * _sigmoid(pre)).astype(BF16)).astype(BF16)

    return pl.pallas_call(
        body, grid=(s // TS,),
        in_specs=[_row(TS, D_SSM), _const((TS, TS)), _const(S5_IN), _const(STATE_TILE), _const(STATE_TILE),
                  _const(S5_OUT), _const((1, D_SSM)), _const((D_SSM, D_SSM)), _const((1, D_SSM))],
        out_specs=[_row(TS, D_SSM), _row(TS, D_SSM), _row(TS, N_STATE), _row(TS, N_STATE)],
        out_shape=[jax.ShapeDtypeStruct((s, D_SSM), BF16), jax.ShapeDtypeStruct((s, D_SSM), F32),
                   jax.ShapeDtypeStruct((s, N_STATE), BF16), jax.ShapeDtypeStruct((s, N_STATE), BF16)],
        scratch_shapes=[pltpu.VMEM((TS, N_STATE), F32), pltpu.VMEM((TS, N_STATE), F32),
                        pltpu.VMEM(STATE_TILE, F32), pltpu.VMEM(STATE_TILE, F32), pltpu.VMEM(STATE_TILE, F32),
                        pltpu.VMEM(STATE_TILE, F32), pltpu.VMEM((1, N_STATE), F32), pltpu.VMEM((1, N_STATE), F32)],
        name=f"s5_fwd_{tag}", compiler_params=_cp(dimension_semantics=("arbitrary",)),
    )(u, _scan_order(), p["bbt3"], p["a_re8"], p["a_im8"], p["ct3"], p["d_skip"], p["w_glu"], p["b_glu"])


def _pool_consts():
    w = jnp.repeat(jnp.asarray(POOL_WINDOWS, F32), POOL_GROUP)[None, :]
    return w


POOL_PAD = SUBLANES
POOL_ROWS = TS + MAX_WINDOW + POOL_PAD


def _window_sum(buf, tmp, first, wl, step):
    assert POOL_WINDOWS == (2, 4, 8, 16)
    n = TS + MAX_WINDOW
    lo = first - MAX_WINDOW if step < 0 else first
    src = buf
    for k, dst in zip((1, 2, 4), tmp):
        dst[pl.ds(lo, n), :] = src[pl.ds(lo, n), :] + src[pl.ds(lo + step * k, n), :]
        src = dst
    s2, s4, s8 = (t[pl.ds(first, TS), :] for t in tmp)
    s16 = s8 + tmp[2][pl.ds(first + step * 8, TS), :]
    return jnp.where(wl == 2, s2, jnp.where(wl == 4, s4, jnp.where(wl == 8, s8, s16)))


def _pool_count(i, rows, wl):
    t = (i * TS + 1).astype(F32) + lax.broadcasted_iota(jnp.int32, (rows, 1), 0).astype(F32)
    return jnp.minimum(t, wl)


def _sgu_mix(vl, wpair_ref, lo, hi):
    rows = vl.shape[0]
    chunks = []
    for c in range(rows // CHUNK):
        vc = vl[CHUNK * c:CHUNK * (c + 1), :]
        parts = []
        for q in range(SGU_HEADS // 2):
            vq = vc[:, LANES * q:LANES * (q + 1)]
            rhs = jnp.concatenate([vq * lo, vq * hi], axis=0).astype(BF16)
            parts.append(_dot(wpair_ref[q], rhs))
        chunks.append(jnp.concatenate(parts, axis=1))
    return jnp.concatenate(chunks, axis=0)


def _sgu_front(zuv, lng, lnb, grads=False):
    gelu = _gelu_and_grad if grads else lambda z: (_gelu(z), None)
    u, du = gelu(zuv[:, :D_SGU])
    v, dv = gelu(zuv[:, D_SGU:])
    mu = jnp.mean(v, axis=-1, keepdims=True)
    vc = v - mu
    rs = lax.rsqrt(jnp.mean(vc * vc, axis=-1, keepdims=True) + EPS)
    vn = vc * rs
    return u, vn, rs, vn * lng + lnb, du, dv


def _half_masks():
    lane = lax.broadcasted_iota(jnp.int32, (1, LANES), 1)
    lo = (lane < SGU_HEAD_DIM).astype(F32)
    return lo, 1.0 - lo


def _mix_fwd(x, p, tag):
    s = x.shape[0]

    def body(x_ref, g_ref, w_ref, wl_ref, wp_ref, sc_ref, lng_ref, lnb_ref, wsp_ref, bias_ref,
             za_ref, zuv_ref, h_ref, ob_ref, pooled_ref, oc_ref, buf, *tmp):
        i = pl.program_id(0)
        tile0 = POOL_PAD + MAX_WINDOW

        @pl.when(i == 0)
        def _():
            for ref in (buf, *tmp):
                ref[pl.ds(0, tile0), :] = jnp.zeros((tile0, D_POOL), F32)

        n, _ = _rms(x_ref[...])
        h = (n * g_ref[...]).astype(BF16)
        h_ref[...] = h
        z = _dot_nt(h, w_ref[...])
        za_ref[...] = z[:, :D_SSM]
        zb = z[:, D_SSM:D_SSM + D_POOL]
        zuv = z[:, D_SSM + D_POOL:]
        zuv_ref[...] = zuv
        buf[pl.ds(tile0, TS), :] = zb
        wl = wl_ref[...]
        pooled = (_window_sum(buf, tmp, tile0, wl, -1) / _pool_count(i, TS, wl) - zb).astype(BF16)
        buf[pl.ds(POOL_PAD, MAX_WINDOW), :] = zb[TS - MAX_WINDOW:, :]
        pooled_ref[...] = pooled
        ob_ref[...] = (_dot(pooled, wp_ref[...]) * sc_ref[...]).astype(BF16)
        lo, hi = _half_masks()
        u, _, _, vl, _, _ = _sgu_front(zuv, lng_ref[...], lnb_ref[...])
        mixed = _sgu_mix(vl, wsp_ref, lo, hi) + jnp.tile(bias_ref[...], (TS // CHUNK, 1))
        oc_ref[...] = (u * mixed).astype(BF16)

    return pl.pallas_call(
        body, grid=(s // TS,),
        in_specs=[_row(TS, D_MODEL), _const((1, D_MODEL)), _const((D_IN, D_MODEL)), _const((1, D_POOL)),
                  _const((D_POOL, D_POOL)), _const((1, D_POOL)), _const((1, D_SGU)), _const((1, D_SGU)),
                  _const((SGU_HEADS // 2, CHUNK, 2 * CHUNK)), _const((CHUNK, D_SGU))],
        out_specs=[_row(TS, D_SSM), _row(TS, 2 * D_SGU), _row(TS, D_MODEL), _row(TS, D_POOL), _row(TS, D_POOL),
                   _row(TS, D_SGU)],
        out_shape=[jax.ShapeDtypeStruct((s, D_SSM), F32), jax.ShapeDtypeStruct((s, 2 * D_SGU), F32),
                   jax.ShapeDtypeStruct((s, D_MODEL), BF16), jax.ShapeDtypeStruct((s, D_POOL), BF16),
                   jax.ShapeDtypeStruct((s, D_POOL), BF16), jax.ShapeDtypeStruct((s, D_SGU), BF16)],
        scratch_shapes=[pltpu.VMEM((POOL_ROWS, D_POOL), F32)] * 4,
        name=f"mix_fwd_{tag}", compiler_params=_cp(dimension_semantics=("arbitrary",)),
    )(x, p["g_mix"], p["w_in"], _pool_consts(), p["w_pool_bd"], p["pool_scale"], p["sgu_ln_g"], p["sgu_ln_b"],
      p["ws_pair"], p["bias_sp"])


def _blk_fwd(x0, oa, ob, oc, p, tag, head=None):
    s = x0.shape[0]
    ts = TS_FFN
    n_head = 0 if head is None else len(head)

    def body(x0_ref, oa_ref, ob_ref, oc_ref, wo_ref, g_ref, wg_ref, wu_ref, wd_ref, *refs):
        x1_ref, x2_ref, h2_ref, gt_ref, up_ref, ycat_ref = refs[n_head:n_head + 6]
        ycat = jnp.concatenate([oa_ref[...], ob_ref[...], oc_ref[...]], axis=1)
        ycat_ref[...] = ycat
        x1 = x0_ref[...] + _dot(ycat, wo_ref[...])
        x1_ref[...] = x1
        n, _ = _rms(x1)
        h2 = (n * g_ref[...]).astype(BF16)
        h2_ref[...] = h2
        gt = _dot_nt(h2, wg_ref[...])
        up = _dot_nt(h2, wu_ref[...])
        gt_ref[...] = gt.astype(BF16)
        up_ref[...] = up.astype(BF16)
        act = (gt * _sigmoid(gt) * up).astype(BF16)
        x2 = x1 + _dot(act, wd_ref[...])
        if head is None:
            x2_ref[...] = x2
            return
        t_ref, gf_ref = refs[:n_head]
        loss_ref, dgf_ref = refs[n_head + 6:]

        @pl.when(pl.program_id(0) == 0)
        def _():
            loss_ref[...] = jnp.zeros_like(loss_ref)
            dgf_ref[...] = jnp.zeros_like(dgf_ref)

        gf = gf_ref[...]
        nf, rf = _rms(x2)
        diff = nf * gf - t_ref[...]
        loss_ref[...] += jnp.sum(diff * diff) * (0.5 / D_MODEL)
        dxn, dgp = _rms_bwd(diff * (1.0 / D_MODEL), nf, rf, gf)
        dgf_ref[...] += _colsum8(dgp)
        x2_ref[...] = dxn

    in_specs = [_row(ts, D_MODEL), _row(ts, D_SSM), _row(ts, D_POOL), _row(ts, D_SGU), _const((D_MODEL, D_MODEL)),
                _const((1, D_MODEL)), _const((D_FF, D_MODEL)), _const((D_FF, D_MODEL)), _const((D_FF, D_MODEL))]
    out_specs = [_row(ts, D_MODEL), _row(ts, D_MODEL), _row(ts, D_MODEL), _row(ts, D_FF), _row(ts, D_FF),
                 _row(ts, D_MODEL)]
    out_shape = [jax.ShapeDtypeStruct((s, D_MODEL), F32), jax.ShapeDtypeStruct((s, D_MODEL), F32),
                 jax.ShapeDtypeStruct((s, D_MODEL), BF16), jax.ShapeDtypeStruct((s, D_FF), BF16),
                 jax.ShapeDtypeStruct((s, D_FF), BF16), jax.ShapeDtypeStruct((s, D_MODEL), BF16)]
    args = (x0, oa, ob, oc, p["w_out"], p["g_ffn"], p["w_gate"], p["w_up"], p["w_down"])
    if head is not None:
        in_specs += [_row(ts, D_MODEL), _const((1, D_MODEL))]
        out_specs += [_acc((SUBLANES, LANES)), _acc((SUBLANES, D_MODEL))]
        out_shape += [jax.ShapeDtypeStruct((SUBLANES, LANES), F32), jax.ShapeDtypeStruct((SUBLANES, D_MODEL), F32)]
        args += tuple(head)
    return pl.pallas_call(
        body, grid=(s // ts,), in_specs=in_specs, out_specs=out_specs, out_shape=out_shape,
        name=f"blk_fwd_{tag}", compiler_params=_cp(dimension_semantics=("arbitrary",)),
    )(*args)


def _blk_bwd(dx2, x1, gt, up, p, tag):
    s = dx2.shape[0]
    ts = TS_FFN

    def body(dx2_ref, x1_ref, gt_ref, up_ref, wd_ref, wgt_ref, wut_ref, wo_ref, g_ref,
             dx1_ref, da_ref, db_ref, dc_ref, dgt_ref, dup_ref, act_ref, dg_ref):
        @pl.when(pl.program_id(0) == 0)
        def _():
            dg_ref[...] = jnp.zeros_like(dg_ref)

        dx2v = dx2_ref[...]
        dact = _dot_nt(dx2v.astype(BF16), wd_ref[...])
        gf = gt_ref[...].astype(F32)
        uf = up_ref[...].astype(F32)
        sg = _sigmoid(gf)
        sl = gf * sg
        act_ref[...] = (sl * uf).astype(BF16)
        dgt = (dact * uf * (sg * (1.0 + gf * (1.0 - sg)))).astype(BF16)
        dup = (dact * sl).astype(BF16)
        dgt_ref[...] = dgt
        dup_ref[...] = dup
        dh2 = _dot(dgt, wgt_ref[...]) + _dot(dup, wut_ref[...])
        n, r = _rms(x1_ref[...])
        dxn, dgp = _rms_bwd(dh2, n, r, g_ref[...])
        dg_ref[...] += _colsum8(dgp)
        dx1 = dx2v + dxn
        dx1_ref[...] = dx1
        dy = _dot_nt(dx1.astype(BF16), wo_ref[...])
        da_ref[...] = dy[:, :D_SSM]
        db_ref[...] = dy[:, D_SSM:D_SSM + D_POOL]
        dc_ref[...] = dy[:, D_SSM + D_POOL:]

    return pl.pallas_call(
        body, grid=(s // ts,),
        in_specs=[_row(ts, D_MODEL), _row(ts, D_MODEL), _row(ts, D_FF), _row(ts, D_FF),
                  _const((D_FF, D_MODEL)), _const((D_FF, D_MODEL)), _const((D_FF, D_MODEL)),
                  _const((D_MODEL, D_MODEL)), _const((1, D_MODEL))],
        out_specs=[_row(ts, D_MODEL), _row(ts, D_SSM), _row(ts, D_POOL), _row(ts, D_SGU), _row(ts, D_FF),
                   _row(ts, D_FF), _row(ts, D_FF), _acc((SUBLANES, D_MODEL))],
        out_shape=[jax.ShapeDtypeStruct((s, D_MODEL), F32), jax.ShapeDtypeStruct((s, D_SSM), F32),
                   jax.ShapeDtypeStruct((s, D_POOL), F32), jax.ShapeDtypeStruct((s, D_SGU), F32),
                   jax.ShapeDtypeStruct((s, D_FF), BF16), jax.ShapeDtypeStruct((s, D_FF), BF16),
                   jax.ShapeDtypeStruct((s, D_FF), BF16), jax.ShapeDtypeStruct((SUBLANES, D_MODEL), F32)],
        name=f"blk_bwd_{tag}", compiler_params=_cp(dimension_semantics=("arbitrary",)),
    )(dx2, x1, gt, up, p["w_down"], p["w_gate"], p["w_up"], p["w_out"], p["g_ffn"])


def _s5_bwd(dout, u, y, h_re, h_im, p, tag):
    s = u.shape[0]
    nt = s // TS
    seg = SEG

    def rev(n):
        return pl.BlockSpec((TS, n), lambda i: (nt - 1 - i, 0))

    def body(do_ref, u_ref, y_ref, hr_ref, hi_ref, perm_ref, ar_ref, ai_ref, cb_ref, bb_ref, dsk_ref,
             wglu_ref, bglu_ref,
             du_ref, dct_ref, dbb_ref, dar_ref, dai_ref, dd_ref, dwglu_ref, dbglu_ref,
             gr, gi, hsr, hsi, er, ei, jr, ji, cr, ci):
        @pl.when(pl.program_id(0) == 0)
        def _():
            for ref in (cr, ci, dct_ref, dbb_ref, dar_ref, dai_ref, dd_ref, dwglu_ref, dbglu_ref):
                ref[...] = jnp.zeros_like(ref)

        perm = perm_ref[...]
        uv = _to_scan_order(perm, u_ref[...])
        yv = y_ref[...]
        dov = _to_scan_order(perm, do_ref[...])
        g, gelu_dy = _gelu_and_grad(yv)
        gb = g.astype(BF16)
        sg = _sigmoid(_dot(gb, wglu_ref[...]) + bglu_ref[...])
        dpre = dov * g * sg * (1.0 - sg)
        dpb = dpre.astype(BF16)
        dwglu_ref[...] += _dot_tn(gb, dpb)
        dbglu_ref[...] += _colsum8(dpre)
        dy = (dov * sg + _dot_nt(dpb, wglu_ref[...])) * gelu_dy
        dd_ref[...] += _colsum8(dy * uv)
        dyb = dy.astype(BF16)
        hsr[...] = hr_ref[...].astype(F32)
        hsi[...] = hi_ref[...].astype(F32)
        for j in range(N_USLAB):
            dct_ref[j] += _dot_tn(_state_cat(hsr, hsi, j), _lanes(dyb, j))
            _state_split(gr, gi, j, _dot(_lanes(dyb, j), cb_ref[j]))
        for j in range(N_USLAB):
            cols = _states(j)
            ar = ar_ref[:, cols]
            ai = -ai_ref[:, cols]
            g_r = g_i = jnp.zeros((SUBLANES, SLAB_STATES), F32)
            for k in range(seg - 1, -1, -1):
                rows = pl.ds(SUBLANES * k, SUBLANES)
                n_r, n_i = _cmul(ar, ai, g_r, g_i)
                g_r = n_r + gr[rows, cols]
                g_i = n_i + gi[rows, cols]
            er[:, cols] = g_r
            ei[:, cols] = g_i
            pr, pi = _cpow(ar[0:1, :], ai[0:1, :], seg)
            c_r = cr[:, cols]
            c_i = ci[:, cols]
            for q in range(SUBLANES - 1, -1, -1):
                jr[q:q + 1, cols] = c_r
                ji[q:q + 1, cols] = c_i
                n_r, n_i = _cmul(pr, pi, c_r, c_i)
                c_r = n_r + er[q:q + 1, cols]
                c_i = n_i + ei[q:q + 1, cols]
            cr[:, cols] = c_r
            ci[:, cols] = c_i
            g_r = jr[:, cols]
            g_i = ji[:, cols]
            a_r = a_i = jnp.zeros((SUBLANES, SLAB_STATES), F32)
            for k in range(seg - 1, -1, -1):
                rows = pl.ds(SUBLANES * k, SUBLANES)
                h_r = hsr[rows, cols]
                h_i = hsi[rows, cols]
                a_r = a_r + g_r * h_r + g_i * h_i
                a_i = a_i + g_i * h_r - g_r * h_i
                n_r, n_i = _cmul(ar, ai, g_r, g_i)
                g_r = n_r + gr[rows, cols]
                g_i = n_i + gi[rows, cols]
                gr[rows, cols] = g_r
                gi[rows, cols] = g_i
            dar_ref[:, cols] += a_r
            dai_ref[:, cols] += a_i
        ub = uv.astype(BF16)
        dus = []
        for j in range(N_USLAB):
            gb_j = _state_cat(gr, gi, j)
            dbb_ref[j] += _dot_tn(_lanes(ub, j), gb_j)
            dus.append(_dot(gb_j, bb_ref[j]))
        du = dy * dsk_ref[...] + jnp.concatenate(dus, axis=1)
        du_ref[...] = _dot_tn(perm, du.astype(BF16)).astype(BF16)

    big = (TS, N_STATE)
    return pl.pallas_call(
        body, grid=(nt,),
        in_specs=[rev(D_SSM), rev(D_SSM), rev(D_SSM), rev(N_STATE), rev(N_STATE), _const((TS, TS)),
                  _const(STATE_TILE), _const(STATE_TILE), _const(S5_IN), _const(S5_OUT), _const((1, D_SSM)),
                  _const((D_SSM, D_SSM)), _const((1, D_SSM))],
        out_specs=[rev(D_SSM), _acc(S5_OUT), _acc(S5_IN), _acc(STATE_TILE), _acc(STATE_TILE),
                   _acc((SUBLANES, D_SSM)), _acc((D_SSM, D_SSM)), _acc((SUBLANES, D_SSM))],
        out_shape=[jax.ShapeDtypeStruct((s, D_SSM), BF16), jax.ShapeDtypeStruct(S5_OUT, F32),
                   jax.ShapeDtypeStruct(S5_IN, F32), jax.ShapeDtypeStruct(STATE_TILE, F32),
                   jax.ShapeDtypeStruct(STATE_TILE, F32), jax.ShapeDtypeStruct((SUBLANES, D_SSM), F32),
                   jax.ShapeDtypeStruct((D_SSM, D_SSM), F32), jax.ShapeDtypeStruct((SUBLANES, D_SSM), F32)],
        scratch_shapes=[pltpu.VMEM(big, F32), pltpu.VMEM(big, F32), pltpu.VMEM(big, F32), pltpu.VMEM(big, F32),
                        pltpu.VMEM(STATE_TILE, F32), pltpu.VMEM(STATE_TILE, F32), pltpu.VMEM(STATE_TILE, F32),
                        pltpu.VMEM(STATE_TILE, F32), pltpu.VMEM((1, N_STATE), F32), pltpu.VMEM((1, N_STATE), F32)],
        name=f"s5_bwd_{tag}", compiler_params=_cp(dimension_semantics=("arbitrary",)),
    )(dout, u, y, h_re, h_im, _scan_order(), p["a_re8"], p["a_im8"], p["cb3"], p["bb3"], p["d_skip"], p["w_glu"],
      p["b_glu"])


def _mix_bwd(dza, db, dc, pooled, zuv, x0, dx1, p, tag):
    s = x0.shape[0]
    nt = s // TS

    def rev(n):
        return pl.BlockSpec((TS, n), lambda i: (nt - 1 - i, 0))

    def body(da_ref, db_ref, dc_ref, po_ref, z_ref, x_ref, dx1_ref, wl_ref, wp_ref, wpt_ref, sc_ref, lng_ref, lnb_ref,
             wsp_ref, wspt_ref, bias_ref, win_ref, g_ref,
             dx0_ref, dz_ref, dwp_ref, dsc_ref, dws_ref, dbias_ref, dlng_ref, dlnb_ref, dg_ref, buf, *tmp):
        step = pl.program_id(0)
        i = nt - 1 - step

        @pl.when(step == 0)
        def _():
            for ref in (dwp_ref, dsc_ref, dws_ref, dbias_ref, dlng_ref, dlnb_ref, dg_ref):
                ref[...] = jnp.zeros_like(ref)
            for ref in (buf, *tmp):
                ref[pl.ds(TS, POOL_ROWS - TS), :] = jnp.zeros((POOL_ROWS - TS, D_POOL), F32)

        wl = wl_ref[...]
        sc = sc_ref[...]
        dob = db_ref[...]
        pooled_b = po_ref[...]
        dsc_ref[...] += _colsum8(dob * _dot(pooled_b, wp_ref[...]))
        dmixb = (dob * sc).astype(BF16)
        dwp_ref[...] += _dot_tn(pooled_b, dmixb)
        dpool = _dot(dmixb, wpt_ref[...])
        dq = dpool / _pool_count(i, TS, wl)
        buf[pl.ds(0, TS), :] = dq
        dzb = _window_sum(buf, tmp, 0, wl, 1) - dpool
        buf[pl.ds(TS, MAX_WINDOW), :] = dq[:MAX_WINDOW, :]

        lo, hi = _half_masks()
        lng = lng_ref[...]
        u, vn, rs, vl, gelu_du, gelu_dv = _sgu_front(z_ref[...], lng, lnb_ref[...], grads=True)
        mixed = _sgu_mix(vl, wsp_ref, lo, hi) + jnp.tile(bias_ref[...], (TS // CHUNK, 1))
        doc = dc_ref[...]
        dzu = doc * mixed * gelu_du
        dmix = doc * u
        dbias = dbias_ref[...]
        for c in range(TS // CHUNK):
            dmc = dmix[CHUNK * c:CHUNK * (c + 1), :]
            dbias = dbias + dmc
            vlc = vl[CHUNK * c:CHUNK * (c + 1), :].astype(BF16)
            for q in range(SGU_HEADS // 2):
                dmq = _lanes(dmc, q)
                vq = _lanes(vlc, q)
                dws_ref[2 * q] += _dot_nt((dmq * lo).astype(BF16), vq)
                dws_ref[2 * q + 1] += _dot_nt((dmq * hi).astype(BF16), vq)
        dbias_ref[...] = dbias
        dvl = _sgu_mix(dmix, wspt_ref, lo, hi)
        dlng_ref[...] += _colsum8(dvl * vn)
        dlnb_ref[...] += _colsum8(dvl)
        dvn = dvl * lng
        dv = rs * (dvn - jnp.mean(dvn, axis=-1, keepdims=True) - vn * jnp.mean(dvn * vn, axis=-1, keepdims=True))

        dz = jnp.concatenate([da_ref[...], dzb.astype(BF16), dzu.astype(BF16), (dv * gelu_dv).astype(BF16)], axis=1)
        dz_ref[...] = dz
        n, r = _rms(x_ref[...])
        dxn, dgp = _rms_bwd(_dot(dz, win_ref[...]), n, r, g_ref[...])
        dg_ref[...] += _colsum8(dgp)
        dx0_ref[...] = dx1_ref[...] + dxn

    pair = (SGU_HEADS // 2, CHUNK, 2 * CHUNK)
    return pl.pallas_call(
        body, grid=(nt,),
        in_specs=[rev(D_SSM), rev(D_POOL), rev(D_SGU), rev(D_POOL), rev(2 * D_SGU), rev(D_MODEL), rev(D_MODEL),
                  _const((1, D_POOL)), _const((D_POOL, D_POOL)), _const((D_POOL, D_POOL)), _const((1, D_POOL)),
                  _const((1, D_SGU)), _const((1, D_SGU)), _const(pair), _const(pair), _const((CHUNK, D_SGU)),
                  _const((D_IN, D_MODEL)), _const((1, D_MODEL))],
        out_specs=[rev(D_MODEL), rev(D_IN), _acc((D_POOL, D_POOL)), _acc((SUBLANES, D_POOL)),
                   _acc((SGU_HEADS, CHUNK, CHUNK)), _acc((CHUNK, D_SGU)), _acc((SUBLANES, D_SGU)),
                   _acc((SUBLANES, D_SGU)), _acc((SUBLANES, D_MODEL))],
        out_shape=[jax.ShapeDtypeStruct((s, D_MODEL), F32), jax.ShapeDtypeStruct((s, D_IN), BF16),
                   jax.ShapeDtypeStruct((D_POOL, D_POOL), F32), jax.ShapeDtypeStruct((SUBLANES, D_POOL), F32),
                   jax.ShapeDtypeStruct((SGU_HEADS, CHUNK, CHUNK), F32), jax.ShapeDtypeStruct((CHUNK, D_SGU), F32),
                   jax.ShapeDtypeStruct((SUBLANES, D_SGU), F32), jax.ShapeDtypeStruct((SUBLANES, D_SGU), F32),
                   jax.ShapeDtypeStruct((SUBLANES, D_MODEL), F32)],
        scratch_shapes=[pltpu.VMEM((POOL_ROWS, D_POOL), F32)] * 4,
        name=f"mix_bwd_{tag}", compiler_params=_cp(dimension_semantics=("arbitrary",)),
    )(dza, db, dc, pooled, zuv, x0, dx1, _pool_consts(), p["w_pool_bd"], p["w_pool_bd_t"], p["pool_scale"],
      p["sgu_ln_g"], p["sgu_ln_b"], p["ws_pair"], p["ws_pair_t"], p["bias_sp"], p["w_in"], p["g_mix"])


def _atb(a, b, tag, token=None):
    s, ka = a.shape
    kb = b.shape[1]
    ts = ATB_ROWS
    tn = min(kb, ATB_COLS)
    ns = s // ts
    after = [] if token is None else [token]

    def body(a_ref, b_ref, *rest):
        o_ref = rest[-1]

        @pl.when(pl.program_id(1) == 0)
        def _():
            o_ref[...] = jnp.zeros_like(o_ref)

        o_ref[...] += _dot_tn(a_ref[...].astype(BF16), b_ref[...].astype(BF16))

    return pl.pallas_call(
        body, grid=(kb // tn, ns),
        in_specs=[pl.BlockSpec((ts, ka), lambda j, i: (i, 0)), pl.BlockSpec((ts, tn), lambda j, i: (i, j))]
        + [pl.BlockSpec(memory_space=pl.ANY)] * len(after),
        out_specs=pl.BlockSpec((ka, tn), lambda j, i: (0, j)),
        out_shape=jax.ShapeDtypeStruct((ka, kb), F32),
        name=f"atb_{tag}", compiler_params=_cp(dimension_semantics=("arbitrary", "arbitrary")),
    )(a, b, *after)


def _s5_discretise(a_re, a_im, log_dt, b_re, b_im):
    dt = jnp.exp(log_dt)[:, None]
    mag = jnp.exp(a_re * dt)
    ar = mag * jnp.cos(a_im * dt)
    ai = mag * jnp.sin(a_im * dt)
    den = a_re * a_re + a_im * a_im
    f_re = ((ar - 1.0) * a_re + ai * a_im) / den
    f_im = (ai * a_re - (ar - 1.0) * a_im) / den
    bb_re = f_re[..., None] * b_re - f_im[..., None] * b_im
    bb_im = f_re[..., None] * b_im + f_im[..., None] * b_re
    return ar, ai, bb_re, bb_im


def _block_diag(blocks):
    g, r, c = blocks.shape
    eye = jnp.eye(g, dtype=blocks.dtype)
    return (blocks[:, :, None, :] * eye[:, None, :, None]).reshape(g * r, g * c)


def _block_diag_extract(m, g):
    r = m.shape[0] // g
    c = m.shape[1] // g
    eye = jnp.eye(g, dtype=m.dtype)
    return jnp.sum(m.reshape(g, r, g, c) * eye[:, None, :, None], axis=2)


GROUPS_PER_SLAB = N_GROUPS // N_USLAB


def _slab_diag(blocks):
    k = GROUPS_PER_SLAB
    _, r, c = blocks.shape
    eye = jnp.eye(k, dtype=blocks.dtype)
    spread = blocks.reshape(N_USLAB, k, r, 1, c) * eye[None, :, None, :, None]
    return spread.reshape(N_USLAB, k * r, k * c)


def _slab_diag_extract(m):
    k = GROUPS_PER_SLAB
    r, c = m.shape[1] // k, m.shape[2] // k
    eye = jnp.eye(k, dtype=m.dtype)
    return jnp.sum(m.reshape(N_USLAB, k, r, k, c) * eye[None, :, None, :, None], axis=3).reshape(N_GROUPS, r, c)


def _state_slabs(v):
    return jnp.broadcast_to(v.reshape(1, N_STATE), STATE_TILE)


def _tril():
    return jnp.tril(jnp.ones((CHUNK, CHUNK), dtype=bool))


def _layer_params(w, l):
    row = lambda v: v.reshape(1, -1)
    t = lambda m: jnp.swapaxes(m, -1, -2)
    ar, ai, bb_re, bb_im = _s5_discretise(w["A_re"][l], w["A_im"][l], w["log_dt"][l], w["B_re"][l], w["B_im"][l])
    bbt3 = jnp.concatenate([_slab_diag(t(bb_re)), _slab_diag(t(bb_im))], axis=2).astype(BF16)
    ct3 = jnp.concatenate([_slab_diag(t(w["C_re"][l])), -_slab_diag(t(w["C_im"][l]))], axis=1).astype(BF16)
    ws = jnp.where(_tril()[None], w["w_spatial"][l], 0.0)
    pair = lambda m: jnp.stack([jnp.concatenate([m[2 * q], m[2 * q + 1]], axis=1)
                                for q in range(SGU_HEADS // 2)]).astype(BF16)
    wp = _block_diag(w["w_pool"][l]).astype(BF16)
    p = dict(
        g_mix=row(w["g_mix"][l]), g_ffn=row(w["g_ffn"][l]), d_skip=row(w["D_skip"][l]), b_glu=row(w["b_glu"][l]),
        pool_scale=row(w["pool_scale"][l]), sgu_ln_g=row(w["sgu_ln_g"][l]), sgu_ln_b=row(w["sgu_ln_b"][l]),
        a_re8=_state_slabs(ar), a_im8=_state_slabs(ai),
        bbt3=bbt3, bb3=t(bbt3), ct3=ct3, cb3=t(ct3),
        w_pool_bd=wp, w_pool_bd_t=t(wp), ws_pair=pair(ws), ws_pair_t=pair(t(ws)),
        bias_sp=jnp.repeat(t(w["b_spatial"][l]), SGU_HEAD_DIM, axis=1),
    )
    return p


MIX_WEIGHTS = ("w_in", "w_glu")
FFN_WEIGHTS = ("w_out", "w_gate", "w_up", "w_down")


def _with_big(p, mats):
    p.update(mats)


def _rows_sum(v):
    return jnp.sum(v, axis=0)


ATB_COLS = 1024
ATB_ROWS = 1024


def _after(v, token):
    return v if token is None else v + token[0, 0]


def _join(a, b):
    return b if a is None else a if b is None else a + b


def _layer_bwd(dx2, sv, p, w, l, tag, hooks, token):
    t = lambda m: jnp.swapaxes(m, -1, -2)
    dx1, da, db, dc, dgt, dup, act, dg_ffn = _blk_bwd(dx2, sv["x1"], sv["gt"], sv["up"],
                                                      dict(p, g_ffn=_after(p["g_ffn"], token)), tag)
    token = hooks["tick"]([dx1])
    token = _join(token, hooks["on_grads"](l, "ffn", {
        "w_down": _atb(act, dx2, tag + "_wd", token), "w_gate": _atb(dgt, sv["h2"], tag + "_wg", token),
        "w_up": _atb(dup, sv["h2"], tag + "_wu", token), "w_out": _atb(sv["ycat"], dx1, tag + "_wo", token)}))
    g = {}
    g["g_ffn"] = _rows_sum(dg_ffn)
    dza, dct3, dbbt3, dar8, dai8, dd8, dwglu, dbglu8 = _s5_bwd(
        da, sv["za"], sv["y"], sv["h_re"], sv["h_im"], dict(p, d_skip=_after(p["d_skip"], token)), tag)
    token = hooks["tick"]([dza])
    dx0, dz, dwp, dsc8, dws, dbias, dlng8, dlnb8, dg_mix = _mix_bwd(
        dza, db, dc, sv["pooled"], sv["zuv"], sv["x0"], dx1, dict(p, pool_scale=_after(p["pool_scale"], token)), tag)
    g["g_mix"] = _rows_sum(dg_mix)
    g["b_glu"] = _rows_sum(dbglu8)
    g["D_skip"] = _rows_sum(dd8)
    half = N_STATE // N_USLAB
    g["C_re"] = t(_slab_diag_extract(dct3[:, :half, :]))
    g["C_im"] = -t(_slab_diag_extract(dct3[:, half:, :]))
    dar = jnp.sum(dar8, axis=0).reshape(N_GROUPS, SSM_STATE)
    dai = jnp.sum(dai8, axis=0).reshape(N_GROUPS, SSM_STATE)
    dbb_re = t(_slab_diag_extract(dbbt3[:, :, :half]))
    dbb_im = t(_slab_diag_extract(dbbt3[:, :, half:]))
    _, disc_vjp = jax.vjp(_s5_discretise, w["A_re"][l], w["A_im"][l], w["log_dt"][l], w["B_re"][l], w["B_im"][l])
    g["A_re"], g["A_im"], g["log_dt"], g["B_re"], g["B_im"] = disc_vjp((dar, dai, dbb_re, dbb_im))
    g["w_pool"] = _block_diag_extract(dwp, len(POOL_WINDOWS))
    g["pool_scale"] = _rows_sum(dsc8)
    g["sgu_ln_g"] = _rows_sum(dlng8)
    g["sgu_ln_b"] = _rows_sum(dlnb8)
    g["w_spatial"] = jnp.where(_tril()[None], dws, 0.0)
    g["b_spatial"] = t(jnp.sum(dbias.reshape(CHUNK, SGU_HEADS, SGU_HEAD_DIM), axis=-1))
    token = hooks["on_small"](l, g)
    token = hooks["on_grads"](l, "mix", {"w_in": _atb(dz, sv["h1"], tag + "_wi", token), "w_glu": dwglu})
    return dx0, token


def _local_step(x, target, w, hooks):
    params = [_layer_params(w, l) for l in range(DEPTH)]
    saved = []
    h = x
    for l in range(DEPTH):
        p, tag = params[l], f"l{l}"
        _with_big(p, hooks["get_big"](l, "mix", [h]))
        za, zuv, h1, ob, pooled, oc = _mix_fwd(h, p, tag)
        oa, y, h_re, h_im = _s5_fwd(za, p, tag)
        _with_big(p, hooks["get_big"](l, "ffn", [oa, ob, oc]))
        head = (target, w["g_final"].reshape(1, -1)) if l == DEPTH - 1 else None
        x1, x2, h2, gt, up, ycat, *loss_parts = _blk_fwd(h, oa, ob, oc, p, tag, head)
        saved.append(dict(x0=h, za=za, zuv=zuv, h1=h1, ycat=ycat, y=y, h_re=h_re, h_im=h_im, pooled=pooled, x1=x1,
                          h2=h2, gt=gt, up=up))
        h = x2
    dx = h
    loss8, dgf8 = loss_parts
    grads = [None] * DEPTH

    def on_small(l, g_l):
        grads[l] = g_l
        if l > 0:
            return None
        g = {n: jnp.stack([grads[k][n] for k in range(DEPTH)]) for n in SMALL if n != "g_final"}
        g["g_final"] = _rows_sum(dgf8)
        return hooks["on_small"](g, loss8[0, 0])

    token = None
    for l in reversed(range(DEPTH)):
        dx, token = _layer_bwd(dx, saved[l], params[l], w, l, f"l{l}", dict(hooks, on_small=on_small), token)
    return dx


_ANY = pl.BlockSpec(memory_space=pl.ANY)
_MESH = pl.DeviceIdType.MESH


def _place():
    return lax.axis_index("x"), lax.axis_index("y"), lax.axis_index("c")


def _other_chips(x, y):
    return [(1 - x, y), (x, 1 - y), (1 - x, 1 - y)]


def _dma_sems(n):
    return pltpu.SemaphoreType.DMA((n,))


def _remote(src, dst, send_sems, recv_sems, k, to):
    return pltpu.make_async_remote_copy(src_ref=src, dst_ref=dst, send_sem=send_sems.at[k], recv_sem=recv_sems.at[k],
                                        device_id=to, device_id_type=_MESH)


_HBM = pl.BlockSpec(memory_space=pltpu.HBM)
_SEM = pl.BlockSpec(memory_space=pltpu.SEMAPHORE)
_EFFECT = pltpu.SideEffectType.DATAFLOW_SIDE_EFFECTING
N_REL = N_CHIPS - 1


def _gather_plan(x, y, c, srcs, lands):
    plan = []
    for l in lands:
        r = l.shape[0] // N_CHIPS
        rows = l.at[pl.ds((2 * x + y) * r, r)]
        plan += [(rows, rows, (cx, cy, c)) for cx, cy in _other_chips(x, y)]
    return plan


def _sibling_plan(x, y, c, srcs, lands):
    return [(s.at[:, 1 - c], l, (x, y, 1 - c)) for s, l in zip(srcs, lands)]


def _slab_plan(x, y, c, srcs, lands):
    return [(s.at[2 * cx + cy], l.at[j], (cx, cy, c))
            for s, l in zip(srcs, lands) for j, (cx, cy) in enumerate(_other_chips(x, y))]


def _plan_copies(plan, srcs, lands, send_sems, recv_sems):
    x, y, c = _place()
    return [_remote(s, d, send_sems, recv_sems, k, to) for k, (s, d, to) in enumerate(plan(x, y, c, srcs, lands))]


def _hbm(a):
    return pltpu.with_memory_space_constraint(a, pltpu.HBM)


def _everyone_plan(x, y, c, srcs, lands):
    me = 4 * x + 2 * y + c
    peers = [(x, y, 1 - c)] + [(cx, cy, cc) for cx, cy in _other_chips(x, y) for cc in (c, 1 - c)]
    return [(s, l.at[me], peer) for s, l in zip(srcs, lands) for peer in peers]


def _copies_start(name, plan, srcs, lands, ncopies):
    ns, n = len(srcs), len(srcs) + len(lands)

    def body(*refs):
        for cp in _plan_copies(plan, refs[:ns], refs[ns:n], refs[n], refs[n + 1]):
            cp.start()
        refs[-1][...] = jnp.zeros_like(refs[-1])

    ref_out = [pltpu.HBM(a.shape, a.dtype) for a in (*srcs, *lands)]
    out = pl.pallas_call(
        body, name=name, in_specs=[_HBM] * n,
        out_shape=(_dma_sems(ncopies), _dma_sems(ncopies), *ref_out, jax.ShapeDtypeStruct((SUBLANES, LANES), F32)),
        out_specs=(_SEM, _SEM, *[_HBM] * n, pl.BlockSpec(memory_space=pltpu.VMEM)),
        input_output_aliases={i: 2 + i for i in range(n)},
        compiler_params=pltpu.CompilerParams(has_side_effects=_EFFECT),
    )(*[_hbm(a) for a in (*srcs, *lands)])
    return dict(name=name, plan=plan, sems=out[:2], srcs=out[2:2 + ns], lands=out[2 + ns:2 + n], token=out[-1])


def _copies_wait(started, after):
    ns = len(started["srcs"])
    n = ns + len(started["lands"])
    plan = started["plan"]

    def body(*refs):
        for cp in _plan_copies(plan, refs[:ns], refs[ns:n], refs[n], refs[n + 1]):
            cp.wait_send()
            cp.wait_recv()

    args = (*started["srcs"], *started["lands"])
    out = pl.pallas_call(
        body, name=started["name"] + "_wait", out_shape=[pltpu.HBM(a.shape, a.dtype) for a in args],
        in_specs=[_HBM] * n + [_SEM, _SEM] + [_ANY] * len(after), out_specs=[_HBM] * n,
        input_output_aliases={i: i for i in range(n)},
        compiler_params=pltpu.CompilerParams(has_side_effects=_EFFECT),
    )(*args, *started["sems"], *after)
    return out[:ns], out[ns:]


def _place_shards(ws, layer, sel, after, tag):
    nw = len(ws)

    def body(sel_ref, *refs):
        for i in range(nw):
            refs[nw + len(after) + i][...] = refs[i][...].astype(BF16)

    return pl.pallas_call(
        body, grid_spec=pltpu.PrefetchScalarGridSpec(
            num_scalar_prefetch=1, grid=(1,),
            in_specs=[pl.BlockSpec((None,) + a.shape[1:], lambda i, s: (layer, 0, 0)) for a in ws] + [_ANY] * len(after),
            out_specs=[pl.BlockSpec(a.shape[1:], lambda i, s: (s[1], 0)) for a in ws]),
        out_shape=[jax.ShapeDtypeStruct((N_CHIPS * a.shape[1], a.shape[2]), BF16) for a in ws],
        name=f"place_shards_{tag}", compiler_params=_cp(dimension_semantics=("arbitrary",)),
    )(sel, *ws, *after)


def _share_halves(fs, layer, tag):
    nw = len(fs)

    def body(*refs):
        ins = refs[:nw]
        send_sems, recv_sems = refs[2 * nw:]
        x, y, c = _place()

        def half(i, who):
            h = ins[i].shape[1] // 2
            return ins[i].at[layer, pl.ds(who * h, h)]

        sends = [_remote(half(i, c), half(i, c), send_sems, recv_sems, i, (x, y, 1 - c)) for i in range(nw)]
        for cp in sends:
            cp.start()
        for i in range(nw):
            sends[i].wait_send()
            _remote(half(i, c), half(i, 1 - c), send_sems, recv_sems, i, (x, y, 1 - c)).wait_recv()

    return pl.pallas_call(
        body, out_shape=[jax.ShapeDtypeStruct(f.shape, f.dtype) for f in fs], in_specs=[_ANY] * nw,
        out_specs=[_ANY] * nw, input_output_aliases={i: i for i in range(nw)},
        scratch_shapes=[_dma_sems(nw), _dma_sems(nw)], name=f"share_halves_{tag}",
    )(*fs)


def _add_halves(g4s, recvs, sel, tag):
    nw = len(g4s)

    def body(sel_ref, *refs):
        for i in range(nw):
            refs[2 * nw + i][...] = (refs[i][...] + refs[nw + i][...]).astype(BF16)

    mine = [pl.BlockSpec((None, None) + g.shape[2:], lambda k, s: (k, s[0], 0, 0)) for g in g4s]
    slab = [pl.BlockSpec((None,) + g.shape[2:], lambda k, s: (k, 0, 0)) for g in g4s]
    return pl.pallas_call(
        body, grid_spec=pltpu.PrefetchScalarGridSpec(num_scalar_prefetch=1, grid=(N_CHIPS,), in_specs=mine + slab,
                                                     out_specs=slab),
        out_shape=[jax.ShapeDtypeStruct(r.shape, BF16) for r in recvs], name=f"add_halves_{tag}",
        compiler_params=_cp(dimension_semantics=("arbitrary",)),
    )(sel, *g4s, *recvs)


def _add_chips(ps, slabs, fs, layer, sel, tag):
    nw = len(ps)
    old = [f for f in fs if f is not None]

    def body(sel_ref, *refs):
        outs = refs[2 * nw + len(old):]
        for i in range(nw):
            acc = refs[i][...].astype(F32)
            for j in range(N_REL):
                acc = acc + refs[nw + i][j].astype(F32)
            outs[i][...] = acc

    shapes = [(DEPTH, 2 * p.shape[1], p.shape[2]) for p in ps]
    in_specs = [pl.BlockSpec((None,) + p.shape[1:], lambda i, s: (s[1], 0, 0)) for p in ps]
    in_specs += [pl.BlockSpec(sl.shape, lambda i, s: (0, 0, 0)) for sl in slabs]
    in_specs += [_ANY] * len(old)
    first_old = 1 + 2 * nw
    aliases, k = {}, 0
    for i, f in enumerate(fs):
        if f is not None:
            aliases[first_old + k] = i
            k += 1
    return pl.pallas_call(
        body, grid_spec=pltpu.PrefetchScalarGridSpec(
            num_scalar_prefetch=1, grid=(1,), in_specs=in_specs,
            out_specs=[pl.BlockSpec((None,) + p.shape[1:], lambda i, s: (layer, s[0], 0)) for p in ps]),
        out_shape=[jax.ShapeDtypeStruct(sh, F32) for sh in shapes], input_output_aliases=aliases,
        name=f"add_chips_{tag}", compiler_params=_cp(dimension_semantics=("arbitrary",)),
    )(sel, *ps, *slabs, *old)


def _adamw_math(w, g, m, v):
    m = ADAM_B1 * m + (1.0 - ADAM_B1) * g
    v = ADAM_B2 * v + (1.0 - ADAM_B2) * (g * g)
    m_hat = m / (1.0 - ADAM_B1 ** ADAM_STEP)
    v_hat = v / (1.0 - ADAM_B2 ** ADAM_STEP)
    delta = -ADAM_LR * (m_hat / (jnp.sqrt(v_hat) + ADAM_EPS) + ADAM_WD * w)
    return delta, m, v


ADAM_ROWS = 512


def _row_tile(rows, most):
    return max(t for t in range(SUBLANES, most + 1, SUBLANES) if rows % t == 0)


def _adamw(w, g, m, v, tag):
    depth, rows, cols = w.shape
    tr = _row_tile(rows, ADAM_ROWS)

    def body(w_ref, g_ref, m_ref, v_ref, d_ref, nm_ref, nv_ref):
        d, nm, nv = _adamw_math(w_ref[...], g_ref[...], m_ref[...], v_ref[...])
        d_ref[...] = d
        nm_ref[...] = nm
        nv_ref[...] = nv

    spec = pl.BlockSpec((None, tr, cols), lambda l, i: (l, i, 0))
    return pl.pallas_call(
        body, grid=(depth, rows // tr), in_specs=[spec] * 4, out_specs=[spec] * 3,
        out_shape=[jax.ShapeDtypeStruct(w.shape, F32)] * 3, name=f"adamw_{tag}",
        compiler_params=_cp(dimension_semantics=("arbitrary", "arbitrary")),
    )(w, g, m, v)


SMALL_TILE = 384
PRECISE = ("g_final",)
COARSE = [n for n in SMALL if n not in PRECISE]


def _small_reduce_adamw(gathered, w, m, v, tag):
    rows = w.shape[0]
    tr = math.gcd(rows, SMALL_TILE)

    def body(ga_ref, w_ref, m_ref, v_ref, g_ref, d_ref, nm_ref, nv_ref):
        g = ga_ref[0].astype(F32)
        for k in range(1, N_DEV):
            g = g + ga_ref[k].astype(F32)
        g_ref[...] = g
        d, nm, nv = _adamw_math(w_ref[...], g, m_ref[...], v_ref[...])
        d_ref[...] = d
        nm_ref[...] = nm
        nv_ref[...] = nv

    spec = _row(tr, LANES)
    return pl.pallas_call(
        body, grid=(rows // tr,),
        in_specs=[pl.BlockSpec((N_DEV, tr, LANES), lambda i: (0, i, 0)), spec, spec, spec], out_specs=[spec] * 4,
        out_shape=[jax.ShapeDtypeStruct((rows, LANES), F32)] * 4, name=f"small_reduce_adamw_{tag}",
        compiler_params=_cp(dimension_semantics=("arbitrary",)),
    )(gathered, w, m, v)


def _exchange_form(n, a):
    return jnp.swapaxes(a, 1, 2) if n in TRANSPOSED else a


PACK_ROWS = 16


def _pack(vals, names, extra=None):
    parts = [vals[n].reshape(-1) for n in names] + ([] if extra is None else [extra.reshape(1)])
    flat = jnp.concatenate(parts)
    rows = -(-flat.size // (LANES * PACK_ROWS)) * PACK_ROWS
    return jnp.pad(flat, (0, rows * LANES - flat.size)).reshape(rows, LANES)


def _unpack(buf, like, names):
    flat = buf.reshape(-1)
    out, off = {}, 0
    for n in names:
        out[n] = flat[off:off + like[n].size].reshape(like[n].shape)
        off += like[n].size
    return out, flat[off:]


def kernel(x, g_mix, w_in, A_re, A_im, log_dt, B_re, B_im, C_re, C_im, D_skip, w_glu, b_glu, w_pool, pool_scale, sgu_ln_g, sgu_ln_b, w_spatial, b_spatial, w_out, g_ffn, w_gate, w_up, w_down, g_final, loss_target, m_g_mix, m_w_in, m_A_re, m_A_im, m_log_dt, m_B_re, m_B_im, m_C_re, m_C_im, m_D_skip, m_w_glu, m_b_glu, m_w_pool, m_pool_scale, m_sgu_ln_g, m_sgu_ln_b, m_w_spatial, m_b_spatial, m_w_out, m_g_ffn, m_w_gate, m_w_up, m_w_down, m_g_final, v_g_mix, v_w_in, v_A_re, v_A_im, v_log_dt, v_B_re, v_B_im, v_C_re, v_C_im, v_D_skip, v_w_glu, v_b_glu, v_w_pool, v_pool_scale, v_sgu_ln_g, v_sgu_ln_b, v_w_spatial, v_b_spatial, v_w_out, v_g_ffn, v_w_gate, v_w_up, v_w_down, v_g_final):
    loc = locals()
    w = {n: loc[n] for n in WEIGHTS}
    m = {n: loc["m_" + n] for n in WEIGHTS}
    v = {n: loc["v_" + n] for n in WEIGHTS}
    sel = jnp.stack([lax.axis_index("c"), 2 * lax.axis_index("x") + lax.axis_index("y")]).astype(jnp.int32)

    chip = sel[1]

    halves = [(l, half) for l in range(DEPTH) for half in ("mix", "ffn")]
    names = {"mix": MIX_WEIGHTS, "ffn": FFN_WEIGHTS}
    started = {}
    wx = {n: _exchange_form(n, w[n]) for n in BIG}
    chain = []
    for l, half in halves:
        lands = _place_shards([wx[n] for n in names[half]], l, sel, chain, f"l{l}_{half}")
        started[l, half] = _copies_start(f"weights_l{l}_{half}", _gather_plan, [], lands, N_REL * len(lands))
        chain = [started[l, half]["token"]]
    w = dict(w, g_mix=_after(w["g_mix"], started[halves[-1]]["token"]))

    def get_big(l, half, after):
        return dict(zip(names[half], _copies_wait(started[l, half], after)[1]))

    result = {n: None for n in BIG}
    stage = {"swap": None, "slabs": None}

    def advance(after):
        if stage["slabs"] is not None:
            ex, ns, l, tag = stage["slabs"]
            part, slabs = _copies_wait(ex, after)
            bufs = _add_chips(part, slabs, [result[n] for n in ns], l, sel, tag)
            for n, f in zip(ns, _share_halves(bufs, l, tag)):
                result[n] = f
            stage["slabs"] = None
        if stage["swap"] is None:
            return None
        sw, ns, l, tag = stage["swap"]
        part = _add_halves(*_copies_wait(sw, after), sel, tag)
        slabs = [lax.empty((N_REL,) + p.shape[1:], BF16) for p in part]
        ex = _copies_start(f"grads_{tag}", _slab_plan, part, slabs, N_REL * len(part))
        stage["slabs"], stage["swap"] = (ex, ns, l, tag), None
        return ex["token"]

    def on_grads(l, half, grads):
        ns = list(grads)
        tag = f"l{l}_{half}"
        token = advance([grads[ns[0]]])
        g4s = [grads[n].reshape(N_CHIPS, 2, grads[n].shape[0] // (2 * N_CHIPS), grads[n].shape[1]) for n in ns]
        recvs = [lax.empty((N_CHIPS,) + g4.shape[2:], F32) for g4 in g4s]
        sw = _copies_start(f"swap_{tag}", _sibling_plan, g4s, recvs, len(g4s))
        stage["swap"] = (sw, ns, l, tag)
        return _join(token, sw["token"])

    small = {}

    def on_small(g, loss_local):
        me = 2 * chip + sel[0]
        blocks = [_pack(g, COARSE).astype(BF16), _pack(g, PRECISE, loss_local)]
        lands = [lax.dynamic_update_slice(lax.empty((N_DEV,) + b.shape, b.dtype), b[None], (me, 0, 0)) for b in blocks]
        small.update(_copies_start("small_grads", _everyone_plan, blocks, lands, (N_DEV - 1) * len(blocks)))
        return small["token"]

    dx = _local_step(x[0], loss_target[0], w, dict(get_big=get_big, on_grads=on_grads, tick=advance, on_small=on_small))
    advance([])
    advance([])
    grads, deltas, new_m, new_v = {}, {}, {}, {}
    for n in BIG:
        outs = _adamw(wx[n], result[n], _exchange_form(n, m[n]), _exchange_form(n, v[n]), n)
        grads[n], deltas[n], new_m[n], new_v[n] = [_exchange_form(n, a) for a in (result[n], *outs)]

    _, gathered = _copies_wait(small, [new_v[n] for n in BIG])
    zero = jnp.zeros((), F32)
    loss = None
    for names_k, extra, block, tag in ((COARSE, None, gathered[0], "coarse"), (PRECISE, zero, gathered[1], "precise")):
        outs = _small_reduce_adamw(block, _pack(w, names_k, extra), _pack(m, names_k, extra), _pack(v, names_k, extra),
                                   tag)
        for store, buf in zip((grads, deltas, new_m, new_v), outs):
            vals, rest = _unpack(buf, w, names_k)
            store.update(vals)
            if store is grads and extra is not None:
                loss = rest[0]
    return (loss, dx[None], *[grads[n] for n in WEIGHTS], *[deltas[n] for n in WEIGHTS],
            *[new_m[n] for n in WEIGHTS], *[new_v[n] for n in WEIGHTS])
```

```python
import math

import jax
import jax.numpy as jnp
from jax import lax
from jax.experimental import pallas as pl
from jax.experimental.pallas import tpu as pltpu

F32 = jnp.float32
BF16 = jnp.bfloat16

D_MODEL = 1024
DEPTH = 2
D_SSM = 384
SSM_GROUP = 16
N_GROUPS = 24
SSM_STATE = 64
N_STATE = N_GROUPS * SSM_STATE
POOL_WINDOWS = (2, 4, 8, 16)
POOL_GROUP = 64
D_POOL = 256
MAX_WINDOW = 16
SGU_HEADS = 6
SGU_HEAD_DIM = 64
D_SGU = 384
CHUNK = 128
D_IN = D_SSM + D_POOL + 2 * D_SGU
D_FF = 2816
EPS = 1e-6

ADAM_LR = 0.001
ADAM_B1 = 0.9
ADAM_B2 = 0.999
ADAM_EPS = 1e-08
ADAM_WD = 0.01
ADAM_STEP = 10

LANES = 128
SUBLANES = 8
VMEM_LIMIT = 56 * 1024 * 1024

TS = 512
TS_FFN = 256

WEIGHTS = ['g_mix', 'w_in', 'A_re', 'A_im', 'log_dt', 'B_re', 'B_im', 'C_re', 'C_im', 'D_skip', 'w_glu', 'b_glu',
           'w_pool', 'pool_scale', 'sgu_ln_g', 'sgu_ln_b', 'w_spatial', 'b_spatial', 'w_out', 'g_ffn', 'w_gate',
           'w_up', 'w_down', 'g_final']
BIG = ['w_in', 'w_glu', 'w_out', 'w_gate', 'w_up', 'w_down']
SMALL = [n for n in WEIGHTS if n not in BIG]
TRANSPOSED = ("w_in", "w_gate", "w_up")
N_CHIPS = 4
N_DEV = 8


def _cp(**kw):
    return pltpu.CompilerParams(vmem_limit_bytes=VMEM_LIMIT, **kw)


def _row(ts, n):
    return pl.BlockSpec((ts, n), lambda i: (i, 0))


def _const(shape):
    nd = len(shape)
    return pl.BlockSpec(shape, lambda i: (0,) * nd, pipeline_mode=pl.Buffered(1))


def _acc(shape):
    nd = len(shape)
    return pl.BlockSpec(shape, lambda i: (0,) * nd)


def _dot(a, b):
    return jnp.dot(a, b, preferred_element_type=F32)


def _dot_tn(a, b):
    return lax.dot_general(a, b, (((0,), (0,)), ((), ())), preferred_element_type=F32)


def _dot_nt(a, b):
    return lax.dot_general(a, b, (((1,), (1,)), ((), ())), preferred_element_type=F32)


_G0 = math.sqrt(2.0 / math.pi)
_G1 = 0.044715


def _gelu(x):
    return 0.5 * x * (1.0 + jnp.tanh(_G0 * (x + _G1 * x * x * x)))


def _gelu_and_grad(x):
    t = jnp.tanh(_G0 * (x + _G1 * x * x * x))
    half = 0.5 * (1.0 + t)
    return x * half, half + 0.5 * x * (1.0 - t * t) * (_G0 * (1.0 + 3.0 * _G1 * x * x))


def _sigmoid(x):
    return 1.0 / (1.0 + jnp.exp(-x))


def _rms(x):
    r = lax.rsqrt(jnp.mean(x * x, axis=-1, keepdims=True) + EPS)
    return x * r, r


def _rms_bwd(dh, n, r, g):
    dn = dh * g
    return r * (dn - n * jnp.mean(dn * n, axis=-1, keepdims=True)), dh * n


def _colsum8(v):
    rows, n = v.shape
    return jnp.sum(v.reshape(rows // SUBLANES, SUBLANES, n), axis=0)


def _cmul(ar, ai, br, bi):
    return ar * br - ai * bi, ar * bi + ai * br


def _cpow(ar, ai, n):
    assert n & (n - 1) == 0
    while n > 1:
        ar, ai = _cmul(ar, ai, ar, ai)
        n //= 2
    return ar, ai


N_USLAB = D_SSM // LANES
SEG = TS // SUBLANES
SLAB_STATES = N_STATE // N_USLAB
S5_IN = (N_USLAB, LANES, 2 * SLAB_STATES)
S5_OUT = (N_USLAB, 2 * SLAB_STATES, LANES)
STATE_TILE = (SUBLANES, N_STATE)


def _scan_order():
    p = jnp.arange(TS)
    src = (p % SUBLANES) * SEG + p // SUBLANES
    return (src[:, None] == jnp.arange(TS)[None, :]).astype(BF16)


def _to_scan_order(perm, v):
    hi = v.astype(BF16)
    lo = (v - hi.astype(F32)).astype(BF16)
    return _dot(perm, hi) + _dot(perm, lo)


def _scan_rows(k):
    return pl.ds(pl.multiple_of(k * SUBLANES, SUBLANES), SUBLANES)


def _lanes(v, j):
    return v[:, LANES * j:LANES * (j + 1)]


def _states(j):
    return pl.ds(SLAB_STATES * j, SLAB_STATES)


def _state_split(re_ref, im_ref, j, v):
    re_ref[:, _states(j)] = v[:, :SLAB_STATES]
    im_ref[:, _states(j)] = v[:, SLAB_STATES:]


def _state_cat(re_ref, im_ref, j):
    return jnp.concatenate([re_ref[:, _states(j)], im_ref[:, _states(j)]], axis=1).astype(BF16)


def _s5_fwd(u, p, tag):
    s = u.shape[0]
    seg = SEG

    def body(u_ref, perm_ref, bbt_ref, ar_ref, ai_ref, ct_ref, dsk_ref, wglu_ref, bglu_ref,
             oa_ref, y_ref, hr_ref, hi_ref, sr, si, er, ei, ir, ii, cr, ci):
        @pl.when(pl.program_id(0) == 0)
        def _():
            cr[...] = jnp.zeros_like(cr)
            ci[...] = jnp.zeros_like(ci)

        perm = perm_ref[...]
        uv = _to_scan_order(perm, u_ref[...])
        ub = uv.astype(BF16)
        for j in range(N_USLAB):
            _state_split(sr, si, j, _dot(_lanes(ub, j), bbt_ref[j]))
        for j in range(N_USLAB):
            cols = _states(j)
            ar = ar_ref[:, cols]
            ai = ai_ref[:, cols]
            h_r = h_i = jnp.zeros((SUBLANES, SLAB_STATES), F32)
            for k in range(seg):
                rows = pl.ds(SUBLANES * k, SUBLANES)
                n_r, n_i = _cmul(ar, ai, h_r, h_i)
                h_r = n_r + sr[rows, cols]
                h_i = n_i + si[rows, cols]
            er[:, cols] = h_r
            ei[:, cols] = h_i
            pr, pi = _cpow(ar[0:1, :], ai[0:1, :], seg)
            c_r = cr[:, cols]
            c_i = ci[:, cols]
            for q in range(SUBLANES):
                ir[q:q + 1, cols] = c_r
                ii[q:q + 1, cols] = c_i
                n_r, n_i = _cmul(pr, pi, c_r, c_i)
                c_r = n_r + er[q:q + 1, cols]
                c_i = n_i + ei[q:q + 1, cols]
            cr[:, cols] = c_r
            ci[:, cols] = c_i
            h_r = ir[:, cols]
            h_i = ii[:, cols]
            for k in range(seg):
                rows = pl.ds(SUBLANES * k, SUBLANES)
                n_r, n_i = _cmul(ar, ai, h_r, h_i)
                h_r = n_r + sr[rows, cols]
                h_i = n_i + si[rows, cols]
                sr[rows, cols] = h_r
                si[rows, cols] = h_i
        hr_ref[...] = sr[...].astype(BF16)
        hi_ref[...] = si[...].astype(BF16)
        y = jnp.concatenate([_dot(_state_cat(hr_ref, hi_ref, j), ct_ref[j]) for j in range(N_USLAB)], axis=1)
        y = y + dsk_ref[...] * uv
        y_ref[...] = y
        g = _gelu(y)
        pre = _dot(g.astype(BF16), wglu_ref[...]) + bglu_ref[...]
        oa_ref[...] = _dot_tn(perm, (g * _sigmoid(pre)).astype(BF16)).astype(BF16)

    return pl.pallas_call(
        body, grid=(s // TS,),
        in_specs=[_row(TS, D_SSM), _const((TS, TS)), _const(S5_IN), _const(STATE_TILE), _const(STATE_TILE),
                  _const(S5_OUT), _const((1, D_SSM)), _const((D_SSM, D_SSM)), _const((1, D_SSM))],
        out_specs=[_row(TS, D_SSM), _row(TS, D_SSM), _row(TS, N_STATE), _row(TS, N_STATE)],
        out_shape=[jax.ShapeDtypeStruct((s, D_SSM), BF16), jax.ShapeDtypeStruct((s, D_SSM), F32),
                   jax.ShapeDtypeStruct((s, N_STATE), BF16), jax.ShapeDtypeStruct((s, N_STATE), BF16)],
        scratch_shapes=[pltpu.VMEM((TS, N_STATE), F32), pltpu.VMEM((TS, N_STATE), F32),
                        pltpu.VMEM(STATE_TILE, F32), pltpu.VMEM(STATE_TILE, F32), pltpu.VMEM(STATE_TILE, F32),
                        pltpu.VMEM(STATE_TILE, F32), pltpu.VMEM((1, N_STATE), F32), pltpu.VMEM((1, N_STATE), F32)],
        name=f"s5_fwd_{tag}", compiler_params=_cp(dimension_semantics=("arbitrary",)),
    )(u, _scan_order(), p["bbt3"], p["a_re8"], p["a_im8"], p["ct3"], p["d_skip"], p["w_glu"], p["b_glu"])


def _pool_consts():
    w = jnp.repeat(jnp.asarray(POOL_WINDOWS, F32), POOL_GROUP)[None, :]
    return w


POOL_PAD = SUBLANES
POOL_ROWS = TS + MAX_WINDOW + POOL_PAD


def _window_sum(buf, tmp, first, wl, step):
    assert POOL_WINDOWS == (2, 4, 8, 16)
    n = TS + MAX_WINDOW
    lo = first - MAX_WINDOW if step < 0 else first
    src = buf
    for k, dst in zip((1, 2, 4), tmp):
        dst[pl.ds(lo, n), :] = src[pl.ds(lo, n), :] + src[pl.ds(lo + step * k, n), :]
        src = dst
    s2, s4, s8 = (t[pl.ds(first, TS), :] for t in tmp)
    s16 = s8 + tmp[2][pl.ds(first + step * 8, TS), :]
    return jnp.where(wl == 2, s2, jnp.where(wl == 4, s4, jnp.where(wl == 8, s8, s16)))


def _pool_count(i, rows, wl):
    t = (i * TS + 1).astype(F32) + lax.broadcasted_iota(jnp.int32, (rows, 1), 0).astype(F32)
    return jnp.minimum(t, wl)


def _sgu_mix(vl, wpair_ref, lo, hi):
    rows = vl.shape[0]
    chunks = []
    for c in range(rows // CHUNK):
        vc = vl[CHUNK * c:CHUNK * (c + 1), :]
        parts = []
        for q in range(SGU_HEADS // 2):
            vq = vc[:, LANES * q:LANES * (q + 1)]
            rhs = jnp.concatenate([vq * lo, vq * hi], axis=0).astype(BF16)
            parts.append(_dot(wpair_ref[q], rhs))
        chunks.append(jnp.concatenate(parts, axis=1))
    return jnp.concatenate(chunks, axis=0)


def _sgu_front(zuv, lng, lnb, grads=False):
    gelu = _gelu_and_grad if grads else lambda z: (_gelu(z), None)
    u, du = gelu(zuv[:, :D_SGU])
    v, dv = gelu(zuv[:, D_SGU:])
    mu = jnp.mean(v, axis=-1, keepdims=True)
    vc = v - mu
    rs = lax.rsqrt(jnp.mean(vc * vc, axis=-1, keepdims=True) + EPS)
    vn = vc * rs
    return u, vn, rs, vn * lng + lnb, du, dv


def _half_masks():
    lane = lax.broadcasted_iota(jnp.int32, (1, LANES), 1)
    lo = (lane < SGU_HEAD_DIM).astype(F32)
    return lo, 1.0 - lo


def _mix_fwd(x, p, tag):
    s = x.shape[0]

    def body(x_ref, g_ref, w_ref, wl_ref, wp_ref, sc_ref, lng_ref, lnb_ref, wsp_ref, bias_ref,
             za_ref, zuv_ref, h_ref, ob_ref, pooled_ref, oc_ref, buf, *tmp):
        i = pl.program_id(0)
        tile0 = POOL_PAD + MAX_WINDOW

        @pl.when(i == 0)
        def _():
            for ref in (buf, *tmp):
                ref[pl.ds(0, tile0), :] = jnp.zeros((tile0, D_POOL), F32)

        n, _ = _rms(x_ref[...])
        h = (n * g_ref[...]).astype(BF16)
        h_ref[...] = h
        z = _dot_nt(h, w_ref[...])
        za_ref[...] = z[:, :D_SSM]
        zb = z[:, D_SSM:D_SSM + D_POOL]
        zuv = z[:, D_SSM + D_POOL:]
        zuv_ref[...] = zuv
        buf[pl.ds(tile0, TS), :] = zb
        wl = wl_ref[...]
        pooled = (_window_sum(buf, tmp, tile0, wl, -1) / _pool_count(i, TS, wl) - zb).astype(BF16)
        buf[pl.ds(POOL_PAD, MAX_WINDOW), :] = zb[TS - MAX_WINDOW:, :]
        pooled_ref[...] = pooled
        ob_ref[...] = (_dot(pooled, wp_ref[...]) * sc_ref[...]).astype(BF16)
        lo, hi = _half_masks()
        u, _, _, vl, _, _ = _sgu_front(zuv, lng_ref[...], lnb_ref[...])
        mixed = _sgu_mix(vl, wsp_ref, lo, hi) + jnp.tile(bias_ref[...], (TS // CHUNK, 1))
        oc_ref[...] = (u * mixed).astype(BF16)

    return pl.pallas_call(
        body, grid=(s // TS,),
        in_specs=[_row(TS, D_MODEL), _const((1, D_MODEL)), _const((D_IN, D_MODEL)), _const((1, D_POOL)),
                  _const((D_POOL, D_POOL)), _const((1, D_POOL)), _const((1, D_SGU)), _const((1, D_SGU)),
                  _const((SGU_HEADS // 2, CHUNK, 2 * CHUNK)), _const((CHUNK, D_SGU))],
        out_specs=[_row(TS, D_SSM), _row(TS, 2 * D_SGU), _row(TS, D_MODEL), _row(TS, D_POOL), _row(TS, D_POOL),
                   _row(TS, D_SGU)],
        out_shape=[jax.ShapeDtypeStruct((s, D_SSM), F32), jax.ShapeDtypeStruct((s, 2 * D_SGU), F32),
                   jax.ShapeDtypeStruct((s, D_MODEL), BF16), jax.ShapeDtypeStruct((s, D_POOL), BF16),
                   jax.ShapeDtypeStruct((s, D_POOL), BF16), jax.ShapeDtypeStruct((s, D_SGU), BF16)],
        scratch_shapes=[pltpu.VMEM((POOL_ROWS, D_POOL), F32)] * 4,
        name=f"mix_fwd_{tag}", compiler_params=_cp(dimension_semantics=("arbitrary",)),
    )(x, p["g_mix"], p["w_in"], _pool_consts(), p["w_pool_bd"], p["pool_scale"], p["sgu_ln_g"], p["sgu_ln_b"],
      p["ws_pair"], p["bias_sp"])


def _blk_fwd(x0, oa, ob, oc, p, tag, head=None):
    s = x0.shape[0]
    ts = TS_FFN
    n_head = 0 if head is None else len(head)

    def body(x0_ref, oa_ref, ob_ref, oc_ref, wo_ref, g_ref, wg_ref, wu_ref, wd_ref, *refs):
        x1_ref, x2_ref, h2_ref, gt_ref, up_ref, ycat_ref = refs[n_head:n_head + 6]
        ycat = jnp.concatenate([oa_ref[...], ob_ref[...], oc_ref[...]], axis=1)
        ycat_ref[...] = ycat
        x1 = x0_ref[...] + _dot(ycat, wo_ref[...])
        x1_ref[...] = x1
        n, _ = _rms(x1)
        h2 = (n * g_ref[...]).astype(BF16)
        h2_ref[...] = h2
        gt = _dot_nt(h2, wg_ref[...])
        up = _dot_nt(h2, wu_ref[...])
        gt_ref[...] = gt.astype(BF16)
        up_ref[...] = up.astype(BF16)
        act = (gt * _sigmoid(gt) * up).astype(BF16)
        x2 = x1 + _dot(act, wd_ref[...])
        if head is None:
            x2_ref[...] = x2
            return
        t_ref, gf_ref = refs[:n_head]
        loss_ref, dgf_ref = refs[n_head + 6:]

        @pl.when(pl.program_id(0) == 0)
        def _():
            loss_ref[...] = jnp.zeros_like(loss_ref)
            dgf_ref[...] = jnp.zeros_like(dgf_ref)

        gf = gf_ref[...]
        nf, rf = _rms(x2)
        diff = nf * gf - t_ref[...]
        loss_ref[...] += jnp.sum(diff * diff) * (0.5 / D_MODEL)
        dxn, dgp = _rms_bwd(diff * (1.0 / D_MODEL), nf, rf, gf)
        dgf_ref[...] += _colsum8(dgp)
        x2_ref[...] = dxn

    in_specs = [_row(ts, D_MODEL), _row(ts, D_SSM), _row(ts, D_POOL), _row(ts, D_SGU), _const((D_MODEL, D_MODEL)),
                _const((1, D_MODEL)), _const((D_FF, D_MODEL)), _const((D_FF, D_MODEL)), _const((D_FF, D_MODEL))]
    out_specs = [_row(ts, D_MODEL), _row(ts, D_MODEL), _row(ts, D_MODEL), _row(ts, D_FF), _row(ts, D_FF),
                 _row(ts, D_MODEL)]
    out_shape = [jax.ShapeDtypeStruct((s, D_MODEL), F32), jax.ShapeDtypeStruct((s, D_MODEL), F32),
                 jax.ShapeDtypeStruct((s, D_MODEL), BF16), jax.ShapeDtypeStruct((s, D_FF), BF16),
                 jax.ShapeDtypeStruct((s, D_FF), BF16), jax.ShapeDtypeStruct((s, D_MODEL), BF16)]
    args = (x0, oa, ob, oc, p["w_out"], p["g_ffn"], p["w_gate"], p["w_up"], p["w_down"])
    if head is not None:
        in_specs += [_row(ts, D_MODEL), _const((1, D_MODEL))]
        out_specs += [_acc((SUBLANES, LANES)), _acc((SUBLANES, D_MODEL))]
        out_shape += [jax.ShapeDtypeStruct((SUBLANES, LANES), F32), jax.ShapeDtypeStruct((SUBLANES, D_MODEL), F32)]
        args += tuple(head)
    return pl.pallas_call(
        body, grid=(s // ts,), in_specs=in_specs, out_specs=out_specs, out_shape=out_shape,
        name=f"blk_fwd_{tag}", compiler_params=_cp(dimension_semantics=("arbitrary",)),
    )(*args)


def _blk_bwd(dx2, x1, gt, up, p, tag):
    s = dx2.shape[0]
    ts = TS_FFN

    def body(dx2_ref, x1_ref, gt_ref, up_ref, wd_ref, wgt_ref, wut_ref, wo_ref, g_ref,
             dx1_ref, da_ref, db_ref, dc_ref, dgt_ref, dup_ref, act_ref, dg_ref):
        @pl.when(pl.program_id(0) == 0)
        def _():
            dg_ref[...] = jnp.zeros_like(dg_ref)

        dx2v = dx2_ref[...]
        dact = _dot_nt(dx2v.astype(BF16), wd_ref[...])
        gf = gt_ref[...].astype(F32)
        uf = up_ref[...].astype(F32)
        sg = _sigmoid(gf)
        sl = gf * sg
        act_ref[...] = (sl * uf).astype(BF16)
        dgt = (dact * uf * (sg * (1.0 + gf * (1.0 - sg)))).astype(BF16)
        dup = (dact * sl).astype(BF16)
        dgt_ref[...] = dgt
        dup_ref[...] = dup
        dh2 = _dot(dgt, wgt_ref[...]) + _dot(dup, wut_ref[...])
        n, r = _rms(x1_ref[...])
        dxn, dgp = _rms_bwd(dh2, n, r, g_ref[...])
        dg_ref[...] += _colsum8(dgp)
        dx1 = dx2v + dxn
        dx1_ref[...] = dx1
        dy = _dot_nt(dx1.astype(BF16), wo_ref[...])
        da_ref[...] = dy[:, :D_SSM]
        db_ref[...] = dy[:, D_SSM:D_SSM + D_POOL]
        dc_ref[...] = dy[:, D_SSM + D_POOL:]

    return pl.pallas_call(
        body, grid=(s // ts,),
        in_specs=[_row(ts, D_MODEL), _row(ts, D_MODEL), _row(ts, D_FF), _row(ts, D_FF),
                  _const((D_FF, D_MODEL)), _const((D_FF, D_MODEL)), _const((D_FF, D_MODEL)),
                  _const((D_MODEL, D_MODEL)), _const((1, D_MODEL))],
        out_specs=[_row(ts, D_MODEL), _row(ts, D_SSM), _row(ts, D_POOL), _row(ts, D_SGU), _row(ts, D_FF),
                   _row(ts, D_FF), _row(ts, D_FF), _acc((SUBLANES, D_MODEL))],
        out_shape=[jax.ShapeDtypeStruct((s, D_MODEL), F32), jax.ShapeDtypeStruct((s, D_SSM), F32),
                   jax.ShapeDtypeStruct((s, D_POOL), F32), jax.ShapeDtypeStruct((s, D_SGU), F32),
                   jax.ShapeDtypeStruct((s, D_FF), BF16), jax.ShapeDtypeStruct((s, D_FF), BF16),
                   jax.ShapeDtypeStruct((s, D_FF), BF16), jax.ShapeDtypeStruct((SUBLANES, D_MODEL), F32)],
        name=f"blk_bwd_{tag}", compiler_params=_cp(dimension_semantics=("arbitrary",)),
    )(dx2, x1, gt, up, p["w_down"], p["w_gate"], p["w_up"], p["w_out"], p["g_ffn"])


def _s5_bwd(dout, u, y, h_re, h_im, p, tag):
    s = u.shape[0]
    nt = s // TS
    seg = SEG

    def rev(n):
        return pl.BlockSpec((TS, n), lambda i: (nt - 1 - i, 0))

    def body(do_ref, u_ref, y_ref, hr_ref, hi_ref, perm_ref, ar_ref, ai_ref, cb_ref, bb_ref, dsk_ref,
             wglu_ref, bglu_ref,
             du_ref, dct_ref, dbb_ref, dar_ref, dai_ref, dd_ref, dwglu_ref, dbglu_ref,
             gr, gi, hsr, hsi, er, ei, jr, ji, cr, ci):
        @pl.when(pl.program_id(0) == 0)
        def _():
            for ref in (cr, ci, dct_ref, dbb_ref, dar_ref, dai_ref, dd_ref, dwglu_ref, dbglu_ref):
                ref[...] = jnp.zeros_like(ref)

        perm = perm_ref[...]
        uv = _to_scan_order(perm, u_ref[...])
        yv = y_ref[...]
        dov = _to_scan_order(perm, do_ref[...])
        g, gelu_dy = _gelu_and_grad(yv)
        gb = g.astype(BF16)
        sg = _sigmoid(_dot(gb, wglu_ref[...]) + bglu_ref[...])
        dpre = dov * g * sg * (1.0 - sg)
        dpb = dpre.astype(BF16)
        dwglu_ref[...] += _dot_tn(gb, dpb)
        dbglu_ref[...] += _colsum8(dpre)
        dy = (dov * sg + _dot_nt(dpb, wglu_ref[...])) * gelu_dy
        dd_ref[...] += _colsum8(dy * uv)
        dyb = dy.astype(BF16)
        hsr[...] = hr_ref[...].astype(F32)
        hsi[...] = hi_ref[...].astype(F32)
        ub = uv.astype(BF16)
        dus = []

        def state_cotangents(j):
            dct_ref[j] += _dot_tn(_lanes(dyb, j), _state_cat(hr_ref, hi_ref, j))
            _state_split(gr, gi, j, _dot(_lanes(dyb, j), cb_ref[j]))

        def input_cotangents(j):
            gb_j = _state_cat(gr, gi, j)
            dbb_ref[j] += _dot_tn(_lanes(ub, j), gb_j)
            dus.append(_dot(gb_j, bb_ref[j]))

        def scan(j):
            cols = _states(j)
            ar = ar_ref[:, cols]
            ai = -ai_ref[:, cols]
            g_r = g_i = jnp.zeros((SUBLANES, SLAB_STATES), F32)
            for k in range(seg - 1, -1, -1):
                rows = pl.ds(SUBLANES * k, SUBLANES)
                n_r, n_i = _cmul(ar, ai, g_r, g_i)
                g_r = n_r + gr[rows, cols]
                g_i = n_i + gi[rows, cols]
            er[:, cols] = g_r
            ei[:, cols] = g_i
            pr, pi = _cpow(ar[0:1, :], ai[0:1, :], seg)
            c_r = cr[:, cols]
            c_i = ci[:, cols]
            for q in range(SUBLANES - 1, -1, -1):
                jr[q:q + 1, cols] = c_r
                ji[q:q + 1, cols] = c_i
                n_r, n_i = _cmul(pr, pi, c_r, c_i)
                c_r = n_r + er[q:q + 1, cols]
                c_i = n_i + ei[q:q + 1, cols]
            cr[:, cols] = c_r
            ci[:, cols] = c_i
            g_r = jr[:, cols]
            g_i = ji[:, cols]
            a_r = a_i = jnp.zeros((SUBLANES, SLAB_STATES), F32)
            for k in range(seg - 1, -1, -1):
                rows = pl.ds(SUBLANES * k, SUBLANES)
                h_r = hsr[rows, cols]
                h_i = hsi[rows, cols]
                a_r = a_r + g_r * h_r + g_i * h_i
                a_i = a_i + g_i * h_r - g_r * h_i
                n_r, n_i = _cmul(ar, ai, g_r, g_i)
                g_r = n_r + gr[rows, cols]
                g_i = n_i + gi[rows, cols]
                gr[rows, cols] = g_r
                gi[rows, cols] = g_i
            dar_ref[:, cols] += a_r
            dai_ref[:, cols] += a_i

        for stage in (state_cotangents, scan, input_cotangents):
            for j in range(N_USLAB):
                stage(j)
        du = dy * dsk_ref[...] + jnp.concatenate(dus, axis=1)
        du_ref[...] = _dot_tn(perm, du.astype(BF16)).astype(BF16)

    big = (TS, N_STATE)
    return pl.pallas_call(
        body, grid=(nt,),
        in_specs=[rev(D_SSM), rev(D_SSM), rev(D_SSM), rev(N_STATE), rev(N_STATE), _const((TS, TS)),
                  _const(STATE_TILE), _const(STATE_TILE), _const(S5_IN), _const(S5_OUT), _const((1, D_SSM)),
                  _const((D_SSM, D_SSM)), _const((1, D_SSM))],
        out_specs=[rev(D_SSM), _acc(S5_IN), _acc(S5_IN), _acc(STATE_TILE), _acc(STATE_TILE),
                   _acc((SUBLANES, D_SSM)), _acc((D_SSM, D_SSM)), _acc((SUBLANES, D_SSM))],
        out_shape=[jax.ShapeDtypeStruct((s, D_SSM), BF16), jax.ShapeDtypeStruct(S5_IN, F32),
                   jax.ShapeDtypeStruct(S5_IN, F32), jax.ShapeDtypeStruct(STATE_TILE, F32),
                   jax.ShapeDtypeStruct(STATE_TILE, F32), jax.ShapeDtypeStruct((SUBLANES, D_SSM), F32),
                   jax.ShapeDtypeStruct((D_SSM, D_SSM), F32), jax.ShapeDtypeStruct((SUBLANES, D_SSM), F32)],
        scratch_shapes=[pltpu.VMEM(big, F32), pltpu.VMEM(big, F32), pltpu.VMEM(big, F32), pltpu.VMEM(big, F32),
                        pltpu.VMEM(STATE_TILE, F32), pltpu.VMEM(STATE_TILE, F32), pltpu.VMEM(STATE_TILE, F32),
                        pltpu.VMEM(STATE_TILE, F32), pltpu.VMEM((1, N_STATE), F32), pltpu.VMEM((1, N_STATE), F32)],
        name=f"s5_bwd_{tag}", compiler_params=_cp(dimension_semantics=("arbitrary",)),
    )(dout, u, y, h_re, h_im, _scan_order(), p["a_re8"], p["a_im8"], p["cb3"], p["bb3"], p["d_skip"], p["w_glu"],
      p["b_glu"])


def _mix_bwd(dza, db, dc, pooled, zuv, x0, dx1, p, tag):
    s = x0.shape[0]
    nt = s // TS

    def rev(n):
        return pl.BlockSpec((TS, n), lambda i: (nt - 1 - i, 0))

    def body(da_ref, db_ref, dc_ref, po_ref, z_ref, x_ref, dx1_ref, wl_ref, wp_ref, wpt_ref, sc_ref, lng_ref, lnb_ref,
             wsp_ref, wspt_ref, bias_ref, win_ref, g_ref,
             dx0_ref, dz_ref, dwp_ref, dsc_ref, dws_ref, dbias_ref, dlng_ref, dlnb_ref, dg_ref, buf, *tmp):
        step = pl.program_id(0)
        i = nt - 1 - step

        @pl.when(step == 0)
        def _():
            for ref in (dwp_ref, dsc_ref, dws_ref, dbias_ref, dlng_ref, dlnb_ref, dg_ref):
                ref[...] = jnp.zeros_like(ref)
            for ref in (buf, *tmp):
                ref[pl.ds(TS, POOL_ROWS - TS), :] = jnp.zeros((POOL_ROWS - TS, D_POOL), F32)

        wl = wl_ref[...]
        sc = sc_ref[...]
        dob = db_ref[...]
        pooled_b = po_ref[...]
        dsc_ref[...] += _colsum8(dob * _dot(pooled_b, wp_ref[...]))
        dmixb = (dob * sc).astype(BF16)
        dwp_ref[...] += _dot_tn(pooled_b, dmixb)
        dpool = _dot(dmixb, wpt_ref[...])
        dq = dpool / _pool_count(i, TS, wl)
        buf[pl.ds(0, TS), :] = dq
        dzb = _window_sum(buf, tmp, 0, wl, 1) - dpool
        buf[pl.ds(TS, MAX_WINDOW), :] = dq[:MAX_WINDOW, :]

        lo, hi = _half_masks()
        lng = lng_ref[...]
        u, vn, rs, vl, gelu_du, gelu_dv = _sgu_front(z_ref[...], lng, lnb_ref[...], grads=True)
        mixed = _sgu_mix(vl, wsp_ref, lo, hi) + jnp.tile(bias_ref[...], (TS // CHUNK, 1))
        doc = dc_ref[...]
        dzu = doc * mixed * gelu_du
        dmix = doc * u
        dbias = dbias_ref[...]
        for c in range(TS // CHUNK):
            dmc = dmix[CHUNK * c:CHUNK * (c + 1), :]
            dbias = dbias + dmc
            vlc = vl[CHUNK * c:CHUNK * (c + 1), :].astype(BF16)
            for q in range(SGU_HEADS // 2):
                dmq = _lanes(dmc, q)
                vq = _lanes(vlc, q)
                dws_ref[2 * q] += _dot_nt((dmq * lo).astype(BF16), vq)
                dws_ref[2 * q + 1] += _dot_nt((dmq * hi).astype(BF16), vq)
        dbias_ref[...] = dbias
        dvl = _sgu_mix(dmix, wspt_ref, lo, hi)
        dlng_ref[...] += _colsum8(dvl * vn)
        dlnb_ref[...] += _colsum8(dvl)
        dvn = dvl * lng
        dv = rs * (dvn - jnp.mean(dvn, axis=-1, keepdims=True) - vn * jnp.mean(dvn * vn, axis=-1, keepdims=True))

        dz = jnp.concatenate([da_ref[...], dzb.astype(BF16), dzu.astype(BF16), (dv * gelu_dv).astype(BF16)], axis=1)
        dz_ref[...] = dz
        n, r = _rms(x_ref[...])
        dxn, dgp = _rms_bwd(_dot(dz, win_ref[...]), n, r, g_ref[...])
        dg_ref[...] += _colsum8(dgp)
        dx0_ref[...] = dx1_ref[...] + dxn

    pair = (SGU_HEADS // 2, CHUNK, 2 * CHUNK)
    return pl.pallas_call(
        body, grid=(nt,),
        in_specs=[rev(D_SSM), rev(D_POOL), rev(D_SGU), rev(D_POOL), rev(2 * D_SGU), rev(D_MODEL), rev(D_MODEL),
                  _const((1, D_POOL)), _const((D_POOL, D_POOL)), _const((D_POOL, D_POOL)), _const((1, D_POOL)),
                  _const((1, D_SGU)), _const((1, D_SGU)), _const(pair), _const(pair), _const((CHUNK, D_SGU)),
                  _const((D_IN, D_MODEL)), _const((1, D_MODEL))],
        out_specs=[rev(D_MODEL), rev(D_IN), _acc((D_POOL, D_POOL)), _acc((SUBLANES, D_POOL)),
                   _acc((SGU_HEADS, CHUNK, CHUNK)), _acc((CHUNK, D_SGU)), _acc((SUBLANES, D_SGU)),
                   _acc((SUBLANES, D_SGU)), _acc((SUBLANES, D_MODEL))],
        out_shape=[jax.ShapeDtypeStruct((s, D_MODEL), F32), jax.ShapeDtypeStruct((s, D_IN), BF16),
                   jax.ShapeDtypeStruct((D_POOL, D_POOL), F32), jax.ShapeDtypeStruct((SUBLANES, D_POOL), F32),
                   jax.ShapeDtypeStruct((SGU_HEADS, CHUNK, CHUNK), F32), jax.ShapeDtypeStruct((CHUNK, D_SGU), F32),
                   jax.ShapeDtypeStruct((SUBLANES, D_SGU), F32), jax.ShapeDtypeStruct((SUBLANES, D_SGU), F32),
                   jax.ShapeDtypeStruct((SUBLANES, D_MODEL), F32)],
        scratch_shapes=[pltpu.VMEM((POOL_ROWS, D_POOL), F32)] * 4,
        name=f"mix_bwd_{tag}", compiler_params=_cp(dimension_semantics=("arbitrary",)),
    )(dza, db, dc, pooled, zuv, x0, dx1, _pool_consts(), p["w_pool_bd"], p["w_pool_bd_t"], p["pool_scale"],
      p["sgu_ln_g"], p["sgu_ln_b"], p["ws_pair"], p["ws_pair_t"], p["bias_sp"], p["w_in"], p["g_mix"])


def _atb(a, b, tag, token=None):
    s, ka = a.shape
    kb = b.shape[1]
    ts = ATB_ROWS
    tn = min(kb, ATB_COLS)
    ns = s // ts
    after = [] if token is None else [token]

    def body(a_ref, b_ref, *rest):
        o_ref = rest[-1]

        @pl.when(pl.program_id(1) == 0)
        def _():
            o_ref[...] = jnp.zeros_like(o_ref)

        o_ref[...] += _dot_tn(a_ref[...].astype(BF16), b_ref[...].astype(BF16))

    return pl.pallas_call(
        body, grid=(kb // tn, ns),
        in_specs=[pl.BlockSpec((ts, ka), lambda j, i: (i, 0)), pl.BlockSpec((ts, tn), lambda j, i: (i, j))]
        + [pl.BlockSpec(memory_space=pl.ANY)] * len(after),
        out_specs=pl.BlockSpec((ka, tn), lambda j, i: (0, j)),
        out_shape=jax.ShapeDtypeStruct((ka, kb), F32),
        name=f"atb_{tag}", compiler_params=_cp(dimension_semantics=("arbitrary", "arbitrary")),
    )(a, b, *after)


def _s5_discretise(a_re, a_im, log_dt, b_re, b_im):
    dt = jnp.exp(log_dt)[:, None]
    mag = jnp.exp(a_re * dt)
    ar = mag * jnp.cos(a_im * dt)
    ai = mag * jnp.sin(a_im * dt)
    den = a_re * a_re + a_im * a_im
    f_re = ((ar - 1.0) * a_re + ai * a_im) / den
    f_im = (ai * a_re - (ar - 1.0) * a_im) / den
    bb_re = f_re[..., None] * b_re - f_im[..., None] * b_im
    bb_im = f_re[..., None] * b_im + f_im[..., None] * b_re
    return ar, ai, bb_re, bb_im


def _block_diag(blocks):
    g, r, c = blocks.shape
    eye = jnp.eye(g, dtype=blocks.dtype)
    return (blocks[:, :, None, :] * eye[:, None, :, None]).reshape(g * r, g * c)


def _block_diag_extract(m, g):
    r = m.shape[0] // g
    c = m.shape[1] // g
    eye = jnp.eye(g, dtype=m.dtype)
    return jnp.sum(m.reshape(g, r, g, c) * eye[:, None, :, None], axis=2)


GROUPS_PER_SLAB = N_GROUPS // N_USLAB


def _slab_diag(blocks):
    k = GROUPS_PER_SLAB
    _, r, c = blocks.shape
    eye = jnp.eye(k, dtype=blocks.dtype)
    spread = blocks.reshape(N_USLAB, k, r, 1, c) * eye[None, :, None, :, None]
    return spread.reshape(N_USLAB, k * r, k * c)


def _slab_diag_extract(m):
    k = GROUPS_PER_SLAB
    r, c = m.shape[1] // k, m.shape[2] // k
    eye = jnp.eye(k, dtype=m.dtype)
    return jnp.sum(m.reshape(N_USLAB, k, r, k, c) * eye[None, :, None, :, None], axis=3).reshape(N_GROUPS, r, c)


def _state_slabs(v):
    return jnp.broadcast_to(v.reshape(1, N_STATE), STATE_TILE)


def _tril():
    return jnp.tril(jnp.ones((CHUNK, CHUNK), dtype=bool))


def _layer_params(w, l):
    row = lambda v: v.reshape(1, -1)
    t = lambda m: jnp.swapaxes(m, -1, -2)
    ar, ai, bb_re, bb_im = _s5_discretise(w["A_re"][l], w["A_im"][l], w["log_dt"][l], w["B_re"][l], w["B_im"][l])
    bbt3 = jnp.concatenate([_slab_diag(t(bb_re)), _slab_diag(t(bb_im))], axis=2).astype(BF16)
    ct3 = jnp.concatenate([_slab_diag(t(w["C_re"][l])), -_slab_diag(t(w["C_im"][l]))], axis=1).astype(BF16)
    ws = jnp.where(_tril()[None], w["w_spatial"][l], 0.0)
    pair = lambda m: jnp.stack([jnp.concatenate([m[2 * q], m[2 * q + 1]], axis=1)
                                for q in range(SGU_HEADS // 2)]).astype(BF16)
    wp = _block_diag(w["w_pool"][l]).astype(BF16)
    p = dict(
        g_mix=row(w["g_mix"][l]), g_ffn=row(w["g_ffn"][l]), d_skip=row(w["D_skip"][l]), b_glu=row(w["b_glu"][l]),
        pool_scale=row(w["pool_scale"][l]), sgu_ln_g=row(w["sgu_ln_g"][l]), sgu_ln_b=row(w["sgu_ln_b"][l]),
        a_re8=_state_slabs(ar), a_im8=_state_slabs(ai),
        bbt3=bbt3, bb3=t(bbt3), ct3=ct3, cb3=t(ct3),
        w_pool_bd=wp, w_pool_bd_t=t(wp), ws_pair=pair(ws), ws_pair_t=pair(t(ws)),
        bias_sp=jnp.repeat(t(w["b_spatial"][l]), SGU_HEAD_DIM, axis=1),
    )
    return p


MIX_WEIGHTS = ("w_in", "w_glu")
FFN_WEIGHTS = ("w_out", "w_gate", "w_up", "w_down")


def _with_big(p, mats):
    p.update(mats)


def _rows_sum(v):
    return jnp.sum(v, axis=0)


ATB_COLS = 1024
ATB_ROWS = 1024


def _after(v, token):
    return v if token is None else v + token[0, 0]


def _join(a, b):
    return b if a is None else a if b is None else a + b


def _layer_bwd(dx2, sv, p, w, l, tag, hooks, token):
    t = lambda m: jnp.swapaxes(m, -1, -2)
    dx1, da, db, dc, dgt, dup, act, dg_ffn = _blk_bwd(dx2, sv["x1"], sv["gt"], sv["up"],
                                                      dict(p, g_ffn=_after(p["g_ffn"], token)), tag)
    token = hooks["tick"]([dx1])
    token = _join(token, hooks["on_grads"](l, "ffn", {
        "w_down": _atb(act, dx2, tag + "_wd", token), "w_gate": _atb(dgt, sv["h2"], tag + "_wg", token),
        "w_up": _atb(dup, sv["h2"], tag + "_wu", token), "w_out": _atb(sv["ycat"], dx1, tag + "_wo", token)}))
    g = {}
    g["g_ffn"] = _rows_sum(dg_ffn)
    dza, dc3, dbbt3, dar8, dai8, dd8, dwglu, dbglu8 = _s5_bwd(
        da, sv["za"], sv["y"], sv["h_re"], sv["h_im"], dict(p, d_skip=_after(p["d_skip"], token)), tag)
    token = hooks["tick"]([dza])
    dx0, dz, dwp, dsc8, dws, dbias, dlng8, dlnb8, dg_mix = _mix_bwd(
        dza, db, dc, sv["pooled"], sv["zuv"], sv["x0"], dx1, dict(p, pool_scale=_after(p["pool_scale"], token)), tag)
    g["g_mix"] = _rows_sum(dg_mix)
    g["b_glu"] = _rows_sum(dbglu8)
    g["D_skip"] = _rows_sum(dd8)
    half = N_STATE // N_USLAB
    g["C_re"] = _slab_diag_extract(dc3[:, :, :half])
    g["C_im"] = -_slab_diag_extract(dc3[:, :, half:])
    dar = jnp.sum(dar8, axis=0).reshape(N_GROUPS, SSM_STATE)
    dai = jnp.sum(dai8, axis=0).reshape(N_GROUPS, SSM_STATE)
    dbb_re = t(_slab_diag_extract(dbbt3[:, :, :half]))
    dbb_im = t(_slab_diag_extract(dbbt3[:, :, half:]))
    _, disc_vjp = jax.vjp(_s5_discretise, w["A_re"][l], w["A_im"][l], w["log_dt"][l], w["B_re"][l], w["B_im"][l])
    g["A_re"], g["A_im"], g["log_dt"], g["B_re"], g["B_im"] = disc_vjp((dar, dai, dbb_re, dbb_im))
    g["w_pool"] = _block_diag_extract(dwp, len(POOL_WINDOWS))
    g["pool_scale"] = _rows_sum(dsc8)
    g["sgu_ln_g"] = _rows_sum(dlng8)
    g["sgu_ln_b"] = _rows_sum(dlnb8)
    g["w_spatial"] = jnp.where(_tril()[None], dws, 0.0)
    g["b_spatial"] = t(jnp.sum(dbias.reshape(CHUNK, SGU_HEADS, SGU_HEAD_DIM), axis=-1))
    token = hooks["on_small"](l, g)
    token = hooks["on_grads"](l, "mix", {"w_in": _atb(dz, sv["h1"], tag + "_wi", token), "w_glu": dwglu})
    return dx0, token


def _local_step(x, target, w, hooks):
    params = [_layer_params(w, l) for l in range(DEPTH)]
    saved = []
    h = x
    for l in range(DEPTH):
        p, tag = params[l], f"l{l}"
        _with_big(p, hooks["get_big"](l, "mix", [h]))
        za, zuv, h1, ob, pooled, oc = _mix_fwd(h, p, tag)
        oa, y, h_re, h_im = _s5_fwd(za, p, tag)
        _with_big(p, hooks["get_big"](l, "ffn", [oa, ob, oc]))
        head = (target, w["g_final"].reshape(1, -1)) if l == DEPTH - 1 else None
        x1, x2, h2, gt, up, ycat, *loss_parts = _blk_fwd(h, oa, ob, oc, p, tag, head)
        saved.append(dict(x0=h, za=za, zuv=zuv, h1=h1, ycat=ycat, y=y, h_re=h_re, h_im=h_im, pooled=pooled, x1=x1,
                          h2=h2, gt=gt, up=up))
        h = x2
    dx = h
    loss8, dgf8 = loss_parts
    grads = [None] * DEPTH

    def on_small(l, g_l):
        grads[l] = g_l
        if l > 0:
            return None
        g = {n: jnp.stack([grads[k][n] for k in range(DEPTH)]) for n in SMALL if n != "g_final"}
        g["g_final"] = _rows_sum(dgf8)
        return hooks["on_small"](g, loss8[0, 0])

    token = None
    for l in reversed(range(DEPTH)):
        dx, token = _layer_bwd(dx, saved[l], params[l], w, l, f"l{l}", dict(hooks, on_small=on_small), token)
    return dx


_ANY = pl.BlockSpec(memory_space=pl.ANY)
_MESH = pl.DeviceIdType.MESH


def _place():
    return lax.axis_index("x"), lax.axis_index("y"), lax.axis_index("c")


def _other_chips(x, y):
    return [(1 - x, y), (x, 1 - y), (1 - x, 1 - y)]


def _dma_sems(n):
    return pltpu.SemaphoreType.DMA((n,))


def _remote(src, dst, send_sems, recv_sems, k, to):
    return pltpu.make_async_remote_copy(src_ref=src, dst_ref=dst, send_sem=send_sems.at[k], recv_sem=recv_sems.at[k],
                                        device_id=to, device_id_type=_MESH)


_HBM = pl.BlockSpec(memory_space=pltpu.HBM)
_SEM = pl.BlockSpec(memory_space=pltpu.SEMAPHORE)
_EFFECT = pltpu.SideEffectType.DATAFLOW_SIDE_EFFECTING
N_REL = N_CHIPS - 1


def _gather_plan(x, y, c, srcs, lands):
    plan = []
    for l in lands:
        r = l.shape[0] // N_CHIPS
        rows = l.at[pl.ds((2 * x + y) * r, r)]
        plan += [(rows, rows, (cx, cy, c)) for cx, cy in _other_chips(x, y)]
    return plan


def _sibling_plan(x, y, c, srcs, lands):
    return [(s.at[:, 1 - c], l, (x, y, 1 - c)) for s, l in zip(srcs, lands)]


def _slab_plan(x, y, c, srcs, lands):
    return [(s.at[2 * cx + cy], l.at[j], (cx, cy, c))
            for s, l in zip(srcs, lands) for j, (cx, cy) in enumerate(_other_chips(x, y))]


def _plan_copies(plan, srcs, lands, send_sems, recv_sems):
    x, y, c = _place()
    return [_remote(s, d, send_sems, recv_sems, k, to) for k, (s, d, to) in enumerate(plan(x, y, c, srcs, lands))]


def _hbm(a):
    return pltpu.with_memory_space_constraint(a, pltpu.HBM)


def _everyone_plan(x, y, c, srcs, lands):
    me = 4 * x + 2 * y + c
    peers = [(x, y, 1 - c)] + [(cx, cy, cc) for cx, cy in _other_chips(x, y) for cc in (c, 1 - c)]
    return [(s, l.at[me], peer) for s, l in zip(srcs, lands) for peer in peers]


def _copies_start(name, plan, srcs, lands, ncopies):
    ns, n = len(srcs), len(srcs) + len(lands)

    def body(*refs):
        for cp in _plan_copies(plan, refs[:ns], refs[ns:n], refs[n], refs[n + 1]):
            cp.start()
        refs[-1][...] = jnp.zeros_like(refs[-1])

    ref_out = [pltpu.HBM(a.shape, a.dtype) for a in (*srcs, *lands)]
    out = pl.pallas_call(
        body, name=name, in_specs=[_HBM] * n,
        out_shape=(_dma_sems(ncopies), _dma_sems(ncopies), *ref_out, jax.ShapeDtypeStruct((SUBLANES, LANES), F32)),
        out_specs=(_SEM, _SEM, *[_HBM] * n, pl.BlockSpec(memory_space=pltpu.VMEM)),
        input_output_aliases={i: 2 + i for i in range(n)},
        compiler_params=pltpu.CompilerParams(has_side_effects=_EFFECT),
    )(*[_hbm(a) for a in (*srcs, *lands)])
    return dict(name=name, plan=plan, sems=out[:2], srcs=out[2:2 + ns], lands=out[2 + ns:2 + n], token=out[-1])


def _copies_wait(started, after):
    ns = len(started["srcs"])
    n = ns + len(started["lands"])
    plan = started["plan"]

    def body(*refs):
        for cp in _plan_copies(plan, refs[:ns], refs[ns:n], refs[n], refs[n + 1]):
            cp.wait_send()
            cp.wait_recv()

    args = (*started["srcs"], *started["lands"])
    out = pl.pallas_call(
        body, name=started["name"] + "_wait", out_shape=[pltpu.HBM(a.shape, a.dtype) for a in args],
        in_specs=[_HBM] * n + [_SEM, _SEM] + [_ANY] * len(after), out_specs=[_HBM] * n,
        input_output_aliases={i: i for i in range(n)},
        compiler_params=pltpu.CompilerParams(has_side_effects=_EFFECT),
    )(*args, *started["sems"], *after)
    return out[:ns], out[ns:]


def _place_shards(ws, layer, sel, after, tag):
    nw = len(ws)

    def body(sel_ref, *refs):
        for i in range(nw):
            refs[nw + len(after) + i][...] = refs[i][...].astype(BF16)

    return pl.pallas_call(
        body, grid_spec=pltpu.PrefetchScalarGridSpec(
            num_scalar_prefetch=1, grid=(1,),
            in_specs=[pl.BlockSpec((None,) + a.shape[1:], lambda i, s: (layer, 0, 0)) for a in ws] + [_ANY] * len(after),
            out_specs=[pl.BlockSpec(a.shape[1:], lambda i, s: (s[1], 0)) for a in ws]),
        out_shape=[jax.ShapeDtypeStruct((N_CHIPS * a.shape[1], a.shape[2]), BF16) for a in ws],
        name=f"place_shards_{tag}", compiler_params=_cp(dimension_semantics=("arbitrary",)),
    )(sel, *ws, *after)


def _share_halves(fs, layer, tag):
    nw = len(fs)

    def body(*refs):
        ins = refs[:nw]
        send_sems, recv_sems = refs[2 * nw:]
        x, y, c = _place()

        def half(i, who):
            h = ins[i].shape[1] // 2
            return ins[i].at[layer, pl.ds(who * h, h)]

        sends = [_remote(half(i, c), half(i, c), send_sems, recv_sems, i, (x, y, 1 - c)) for i in range(nw)]
        for cp in sends:
            cp.start()
        for i in range(nw):
            sends[i].wait_send()
            _remote(half(i, c), half(i, 1 - c), send_sems, recv_sems, i, (x, y, 1 - c)).wait_recv()

    return pl.pallas_call(
        body, out_shape=[jax.ShapeDtypeStruct(f.shape, f.dtype) for f in fs], in_specs=[_ANY] * nw,
        out_specs=[_ANY] * nw, input_output_aliases={i: i for i in range(nw)},
        scratch_shapes=[_dma_sems(nw), _dma_sems(nw)], name=f"share_halves_{tag}",
    )(*fs)


def _add_halves(g4s, recvs, sel, tag):
    nw = len(g4s)

    def body(sel_ref, *refs):
        for i in range(nw):
            refs[2 * nw + i][...] = (refs[i][...] + refs[nw + i][...]).astype(BF16)

    mine = [pl.BlockSpec((None, None) + g.shape[2:], lambda k, s: (k, s[0], 0, 0)) for g in g4s]
    slab = [pl.BlockSpec((None,) + g.shape[2:], lambda k, s: (k, 0, 0)) for g in g4s]
    return pl.pallas_call(
        body, grid_spec=pltpu.PrefetchScalarGridSpec(num_scalar_prefetch=1, grid=(N_CHIPS,), in_specs=mine + slab,
                                                     out_specs=slab),
        out_shape=[jax.ShapeDtypeStruct(r.shape, BF16) for r in recvs], name=f"add_halves_{tag}",
        compiler_params=_cp(dimension_semantics=("arbitrary",)),
    )(sel, *g4s, *recvs)


def _add_chips(ps, slabs, fs, layer, sel, tag):
    nw = len(ps)
    old = [f for f in fs if f is not None]

    def body(sel_ref, *refs):
        outs = refs[2 * nw + len(old):]
        for i in range(nw):
            acc = refs[i][...].astype(F32)
            for j in range(N_REL):
                acc = acc + refs[nw + i][j].astype(F32)
            outs[i][...] = acc

    shapes = [(DEPTH, 2 * p.shape[1], p.shape[2]) for p in ps]
    in_specs = [pl.BlockSpec((None,) + p.shape[1:], lambda i, s: (s[1], 0, 0)) for p in ps]
    in_specs += [pl.BlockSpec(sl.shape, lambda i, s: (0, 0, 0)) for sl in slabs]
    in_specs += [_ANY] * len(old)
    first_old = 1 + 2 * nw
    aliases, k = {}, 0
    for i, f in enumerate(fs):
        if f is not None:
            aliases[first_old + k] = i
            k += 1
    return pl.pallas_call(
        body, grid_spec=pltpu.PrefetchScalarGridSpec(
            num_scalar_prefetch=1, grid=(1,), in_specs=in_specs,
            out_specs=[pl.BlockSpec((None,) + p.shape[1:], lambda i, s: (layer, s[0], 0)) for p in ps]),
        out_shape=[jax.ShapeDtypeStruct(sh, F32) for sh in shapes], input_output_aliases=aliases,
        name=f"add_chips_{tag}", compiler_params=_cp(dimension_semantics=("arbitrary",)),
    )(sel, *ps, *slabs, *old)


def _adamw_math(w, g, m, v):
    m = ADAM_B1 * m + (1.0 - ADAM_B1) * g
    v = ADAM_B2 * v + (1.0 - ADAM_B2) * (g * g)
    m_hat = m / (1.0 - ADAM_B1 ** ADAM_STEP)
    v_hat = v / (1.0 - ADAM_B2 ** ADAM_STEP)
    delta = -ADAM_LR * (m_hat / (jnp.sqrt(v_hat) + ADAM_EPS) + ADAM_WD * w)
    return delta, m, v


ADAM_ROWS = 512


def _row_tile(rows, most):
    return max(t for t in range(SUBLANES, most + 1, SUBLANES) if rows % t == 0)


def _adamw(w, g, m, v, tag):
    depth, rows, cols = w.shape
    tr = _row_tile(rows, ADAM_ROWS)

    def body(w_ref, g_ref, m_ref, v_ref, d_ref, nm_ref, nv_ref):
        d, nm, nv = _adamw_math(w_ref[...], g_ref[...], m_ref[...], v_ref[...])
        d_ref[...] = d
        nm_ref[...] = nm
        nv_ref[...] = nv

    spec = pl.BlockSpec((None, tr, cols), lambda l, i: (l, i, 0))
    return pl.pallas_call(
        body, grid=(depth, rows // tr), in_specs=[spec] * 4, out_specs=[spec] * 3,
        out_shape=[jax.ShapeDtypeStruct(w.shape, F32)] * 3, name=f"adamw_{tag}",
        compiler_params=_cp(dimension_semantics=("arbitrary", "arbitrary")),
    )(w, g, m, v)


SMALL_TILE = 512
PRECISE = ("g_final",)
COARSE = [n for n in SMALL if n not in PRECISE]


def _small_reduce_adamw(gathered, w, m, v, tag):
    rows = w.shape[0]
    tr = math.gcd(rows, SMALL_TILE)

    def body(ga_ref, w_ref, m_ref, v_ref, g_ref, d_ref, nm_ref, nv_ref):
        g = ga_ref[0].astype(F32)
        for k in range(1, N_DEV):
            g = g + ga_ref[k].astype(F32)
        g_ref[...] = g
        d, nm, nv = _adamw_math(w_ref[...], g, m_ref[...], v_ref[...])
        d_ref[...] = d
        nm_ref[...] = nm
        nv_ref[...] = nv

    spec = _row(tr, LANES)
    return pl.pallas_call(
        body, grid=(rows // tr,),
        in_specs=[pl.BlockSpec((N_DEV, tr, LANES), lambda i: (0, i, 0)), spec, spec, spec], out_specs=[spec] * 4,
        out_shape=[jax.ShapeDtypeStruct((rows, LANES), F32)] * 4, name=f"small_reduce_adamw_{tag}",
        compiler_params=_cp(dimension_semantics=("arbitrary",)),
    )(gathered, w, m, v)


def _exchange_form(n, a):
    return jnp.swapaxes(a, 1, 2) if n in TRANSPOSED else a


PACK_ROWS = 16


def _rows_of(size):
    return -(-size // (LANES * PACK_ROWS)) * PACK_ROWS


def _pack(vals, names, extra=None):
    parts = [vals[n].reshape(-1) for n in names] + ([] if extra is None else [extra.reshape(1)])
    tiles = [jnp.pad(a, (0, _rows_of(a.size) * LANES - a.size)).reshape(-1, LANES) for a in parts]
    rows = sum(t.shape[0] for t in tiles)
    if rows > SMALL_TILE:
        tiles.append(jnp.zeros((-rows % SMALL_TILE, LANES), tiles[0].dtype))
    return jnp.concatenate(tiles, axis=0)


def _unpack(buf, like, names):
    out, row = {}, 0
    for n in names:
        rows = _rows_of(like[n].size)
        out[n] = buf[row:row + rows].reshape(-1)[:like[n].size].reshape(like[n].shape)
        row += rows
    return out, buf[row:]


def kernel(x, g_mix, w_in, A_re, A_im, log_dt, B_re, B_im, C_re, C_im, D_skip, w_glu, b_glu, w_pool, pool_scale, sgu_ln_g, sgu_ln_b, w_spatial, b_spatial, w_out, g_ffn, w_gate, w_up, w_down, g_final, loss_target, m_g_mix, m_w_in, m_A_re, m_A_im, m_log_dt, m_B_re, m_B_im, m_C_re, m_C_im, m_D_skip, m_w_glu, m_b_glu, m_w_pool, m_pool_scale, m_sgu_ln_g, m_sgu_ln_b, m_w_spatial, m_b_spatial, m_w_out, m_g_ffn, m_w_gate, m_w_up, m_w_down, m_g_final, v_g_mix, v_w_in, v_A_re, v_A_im, v_log_dt, v_B_re, v_B_im, v_C_re, v_C_im, v_D_skip, v_w_glu, v_b_glu, v_w_pool, v_pool_scale, v_sgu_ln_g, v_sgu_ln_b, v_w_spatial, v_b_spatial, v_w_out, v_g_ffn, v_w_gate, v_w_up, v_w_down, v_g_final):
    loc = locals()
    w = {n: loc[n] for n in WEIGHTS}
    m = {n: loc["m_" + n] for n in WEIGHTS}
    v = {n: loc["v_" + n] for n in WEIGHTS}
    sel = jnp.stack([lax.axis_index("c"), 2 * lax.axis_index("x") + lax.axis_index("y")]).astype(jnp.int32)

    chip = sel[1]

    halves = [(l, half) for l in range(DEPTH) for half in ("mix", "ffn")]
    names = {"mix": MIX_WEIGHTS, "ffn": FFN_WEIGHTS}
    started = {}
    wx = {n: _exchange_form(n, w[n]) for n in BIG}
    chain = []
    for l, half in halves:
        lands = _place_shards([wx[n] for n in names[half]], l, sel, chain, f"l{l}_{half}")
        started[l, half] = _copies_start(f"weights_l{l}_{half}", _gather_plan, [], lands, N_REL * len(lands))
        chain = [started[l, half]["token"]]
    w = dict(w, g_mix=_after(w["g_mix"], started[halves[-1]]["token"]))

    def get_big(l, half, after):
        return dict(zip(names[half], _copies_wait(started[l, half], after)[1]))

    result = {n: None for n in BIG}
    stage = {"swap": None, "slabs": None}

    def advance(after):
        if stage["slabs"] is not None:
            ex, ns, l, tag = stage["slabs"]
            part, slabs = _copies_wait(ex, after)
            bufs = _add_chips(part, slabs, [result[n] for n in ns], l, sel, tag)
            for n, f in zip(ns, _share_halves(bufs, l, tag)):
                result[n] = f
            stage["slabs"] = None
        if stage["swap"] is None:
            return None
        sw, ns, l, tag = stage["swap"]
        part = _add_halves(*_copies_wait(sw, after), sel, tag)
        slabs = [lax.empty((N_REL,) + p.shape[1:], BF16) for p in part]
        ex = _copies_start(f"grads_{tag}", _slab_plan, part, slabs, N_REL * len(part))
        stage["slabs"], stage["swap"] = (ex, ns, l, tag), None
        return ex["token"]

    def on_grads(l, half, grads):
        ns = list(grads)
        tag = f"l{l}_{half}"
        token = advance([grads[ns[0]]])
        g4s = [grads[n].reshape(N_CHIPS, 2, grads[n].shape[0] // (2 * N_CHIPS), grads[n].shape[1]) for n in ns]
        recvs = [lax.empty((N_CHIPS,) + g4.shape[2:], F32) for g4 in g4s]
        sw = _copies_start(f"swap_{tag}", _sibling_plan, g4s, recvs, len(g4s))
        stage["swap"] = (sw, ns, l, tag)
        return _join(token, sw["token"])

    small = {}

    def on_small(g, loss_local):
        me = 2 * chip + sel[0]
        blocks = [_pack(g, COARSE).astype(BF16), _pack(g, PRECISE, loss_local)]
        lands = [lax.dynamic_update_slice(lax.empty((N_DEV,) + b.shape, b.dtype), b[None], (me, 0, 0)) for b in blocks]
        small.update(_copies_start("small_grads", _everyone_plan, blocks, lands, (N_DEV - 1) * len(blocks)))
        return small["token"]

    dx = _local_step(x[0], loss_target[0], w, dict(get_big=get_big, on_grads=on_grads, tick=advance, on_small=on_small))
    advance([])
    advance([])
    grads, deltas, new_m, new_v = {}, {}, {}, {}
    for n in BIG:
        outs = _adamw(wx[n], result[n], _exchange_form(n, m[n]), _exchange_form(n, v[n]), n)
        grads[n], deltas[n], new_m[n], new_v[n] = [_exchange_form(n, a) for a in (result[n], *outs)]

    _, gathered = _copies_wait(small, [new_v[n] for n in BIG])
    zero = jnp.zeros((), F32)
    loss = None
    for names_k, extra, block, tag in ((COARSE, None, gathered[0], "coarse"), (PRECISE, zero, gathered[1], "precise")):
        outs = _small_reduce_adamw(block, _pack(w, names_k, extra), _pack(m, names_k, extra), _pack(v, names_k, extra),
                                   tag)
        for store, buf in zip((grads, deltas, new_m, new_v), outs):
            vals, rest = _unpack(buf, w, names_k)
            store.update(vals)
            if store is grads and extra is not None:
                loss = rest[0, 0]
    return (loss, dx[None], *[grads[n] for n in WEIGHTS], *[deltas[n] for n in WEIGHTS],
            *[new_m[n] for n in WEIGHTS], *[new_v[n] for n in WEIGHTS])
```

```python
import math

import jax
import jax.numpy as jnp
from jax import lax
from jax.experimental import pallas as pl
from jax.experimental.pallas import tpu as pltpu

F32 = jnp.float32
BF16 = jnp.bfloat16

D_MODEL = 1024
DEPTH = 2
D_SSM = 384
SSM_GROUP = 16
N_GROUPS = 24
SSM_STATE = 64
N_STATE = N_GROUPS * SSM_STATE
POOL_WINDOWS = (2, 4, 8, 16)
POOL_GROUP = 64
D_POOL = 256
MAX_WINDOW = 16
SGU_HEADS = 6
SGU_HEAD_DIM = 64
D_SGU = 384
CHUNK = 128
D_IN = D_SSM + D_POOL + 2 * D_SGU
D_FF = 2816
EPS = 1e-6

ADAM_LR = 0.001
ADAM_B1 = 0.9
ADAM_B2 = 0.999
ADAM_EPS = 1e-08
ADAM_WD = 0.01
ADAM_STEP = 10

LANES = 128
SUBLANES = 8
VMEM_LIMIT = 56 * 1024 * 1024

TS = 512
TS_FFN = 256

WEIGHTS = ['g_mix', 'w_in', 'A_re', 'A_im', 'log_dt', 'B_re', 'B_im', 'C_re', 'C_im', 'D_skip', 'w_glu', 'b_glu',
           'w_pool', 'pool_scale', 'sgu_ln_g', 'sgu_ln_b', 'w_spatial', 'b_spatial', 'w_out', 'g_ffn', 'w_gate',
           'w_up', 'w_down', 'g_final']
BIG = ['w_in', 'w_glu', 'w_out', 'w_gate', 'w_up', 'w_down']
SMALL = [n for n in WEIGHTS if n not in BIG]
TRANSPOSED = ("w_in", "w_gate", "w_up")
N_CHIPS = 4
N_DEV = 8


def _cp(**kw):
    return pltpu.CompilerParams(vmem_limit_bytes=VMEM_LIMIT, **kw)


def _row(ts, n):
    return pl.BlockSpec((ts, n), lambda i: (i, 0))


def _const(shape):
    nd = len(shape)
    return pl.BlockSpec(shape, lambda i: (0,) * nd, pipeline_mode=pl.Buffered(1))


def _acc(shape):
    nd = len(shape)
    return pl.BlockSpec(shape, lambda i: (0,) * nd)


def _dot(a, b):
    return jnp.dot(a, b, preferred_element_type=F32)


def _dot_tn(a, b):
    return lax.dot_general(a, b, (((0,), (0,)), ((), ())), preferred_element_type=F32)


def _dot_nt(a, b):
    return lax.dot_general(a, b, (((1,), (1,)), ((), ())), preferred_element_type=F32)


_G0 = math.sqrt(2.0 / math.pi)
_G1 = 0.044715


def _gelu(x):
    return 0.5 * x * (1.0 + jnp.tanh(_G0 * (x + _G1 * x * x * x)))


def _gelu_and_grad(x):
    t = jnp.tanh(_G0 * (x + _G1 * x * x * x))
    half = 0.5 * (1.0 + t)
    return x * half, half + 0.5 * x * (1.0 - t * t) * (_G0 * (1.0 + 3.0 * _G1 * x * x))


def _sigmoid(x):
    return 1.0 / (1.0 + jnp.exp(-x))


def _rms(x):
    r = lax.rsqrt(jnp.mean(x * x, axis=-1, keepdims=True) + EPS)
    return x * r, r


def _rms_bwd(dh, n, r, g):
    dn = dh * g
    return r * (dn - n * jnp.mean(dn * n, axis=-1, keepdims=True)), dh * n


def _colsum8(v):
    rows, n = v.shape
    return jnp.sum(v.reshape(rows // SUBLANES, SUBLANES, n), axis=0)


def _cmul(ar, ai, br, bi):
    return ar * br - ai * bi, ar * bi + ai * br


def _cpow(ar, ai, n):
    assert n & (n - 1) == 0
    while n > 1:
        ar, ai = _cmul(ar, ai, ar, ai)
        n //= 2
    return ar, ai


N_USLAB = D_SSM // LANES
SEG = TS // SUBLANES
SLAB_STATES = N_STATE // N_USLAB
S5_IN = (N_USLAB, LANES, 2 * SLAB_STATES)
S5_OUT = (N_USLAB, 2 * SLAB_STATES, LANES)
STATE_TILE = (SUBLANES, N_STATE)


def _scan_order():
    p = jnp.arange(TS)
    src = (p % SUBLANES) * SEG + p // SUBLANES
    return (src[:, None] == jnp.arange(TS)[None, :]).astype(BF16)


def _to_scan_order(perm, v):
    hi = v.astype(BF16)
    lo = (v - hi.astype(F32)).astype(BF16)
    return _dot(perm, hi) + _dot(perm, lo)


def _scan_rows(k):
    return pl.ds(pl.multiple_of(k * SUBLANES, SUBLANES), SUBLANES)


def _lanes(v, j):
    return v[:, LANES * j:LANES * (j + 1)]


def _states(j):
    return pl.ds(SLAB_STATES * j, SLAB_STATES)


def _state_split(re_ref, im_ref, j, v):
    re_ref[:, _states(j)] = v[:, :SLAB_STATES]
    im_ref[:, _states(j)] = v[:, SLAB_STATES:]


def _state_cat(re_ref, im_ref, j):
    return jnp.concatenate([re_ref[:, _states(j)], im_ref[:, _states(j)]], axis=1).astype(BF16)


def _s5_fwd(u, p, tag):
    s = u.shape[0]
    seg = SEG

    def body(u_ref, perm_ref, bbt_ref, ar_ref, ai_ref, ct_ref, dsk_ref, wglu_ref, bglu_ref,
             oa_ref, y_ref, hr_ref, hi_ref, sr, si, er, ei, ir, ii, cr, ci):
        @pl.when(pl.program_id(0) == 0)
        def _():
            cr[...] = jnp.zeros_like(cr)
            ci[...] = jnp.zeros_like(ci)

        perm = perm_ref[...]
        uv = _to_scan_order(perm, u_ref[...])
        ub = uv.astype(BF16)
        for j in range(N_USLAB):
            _state_split(sr, si, j, _dot(_lanes(ub, j), bbt_ref[j]))
        for j in range(N_USLAB):
            cols = _states(j)
            ar = ar_ref[:, cols]
            ai = ai_ref[:, cols]
            h_r = h_i = jnp.zeros((SUBLANES, SLAB_STATES), F32)
            for k in range(seg):
                rows = pl.ds(SUBLANES * k, SUBLANES)
                n_r, n_i = _cmul(ar, ai, h_r, h_i)
                h_r = n_r + sr[rows, cols]
                h_i = n_i + si[rows, cols]
            er[:, cols] = h_r
            ei[:, cols] = h_i
            pr, pi = _cpow(ar[0:1, :], ai[0:1, :], seg)
            c_r = cr[:, cols]
            c_i = ci[:, cols]
            for q in range(SUBLANES):
                ir[q:q + 1, cols] = c_r
                ii[q:q + 1, cols] = c_i
                n_r, n_i = _cmul(pr, pi, c_r, c_i)
                c_r = n_r + er[q:q + 1, cols]
                c_i = n_i + ei[q:q + 1, cols]
            cr[:, cols] = c_r
            ci[:, cols] = c_i
            h_r = ir[:, cols]
            h_i = ii[:, cols]
            for k in range(seg):
                rows = pl.ds(SUBLANES * k, SUBLANES)
                n_r, n_i = _cmul(ar, ai, h_r, h_i)
                h_r = n_r + sr[rows, cols]
                h_i = n_i + si[rows, cols]
                sr[rows, cols] = h_r
                si[rows, cols] = h_i
        hr_ref[...] = sr[...].astype(BF16)
        hi_ref[...] = si[...].astype(BF16)
        y = jnp.concatenate([_dot(_state_cat(hr_ref, hi_ref, j), ct_ref[j]) for j in range(N_USLAB)], axis=1)
        y = y + dsk_ref[...] * uv
        y_ref[...] = y
        g = _gelu(y)
        pre = _dot(g.astype(BF16), wglu_ref[...]) + bglu_ref[...]
        oa_ref[...] = _dot_tn(perm, (g * _sigmoid(pre)).astype(BF16)).astype(BF16)

    return pl.pallas_call(
        body, grid=(s // TS,),
        in_specs=[_row(TS, D_SSM), _const((TS, TS)), _const(S5_IN), _const(STATE_TILE), _const(STATE_TILE),
                  _const(S5_OUT), _const((1, D_SSM)), _const((D_SSM, D_SSM)), _const((1, D_SSM))],
        out_specs=[_row(TS, D_SSM), _row(TS, D_SSM), _row(TS, N_STATE), _row(TS, N_STATE)],
        out_shape=[jax.ShapeDtypeStruct((s, D_SSM), BF16), jax.ShapeDtypeStruct((s, D_SSM), F32),
                   jax.ShapeDtypeStruct((s, N_STATE), BF16), jax.ShapeDtypeStruct((s, N_STATE), BF16)],
        scratch_shapes=[pltpu.VMEM((TS, N_STATE), F32), pltpu.VMEM((TS, N_STATE), F32),
                        pltpu.VMEM(STATE_TILE, F32), pltpu.VMEM(STATE_TILE, F32), pltpu.VMEM(STATE_TILE, F32),
                        pltpu.VMEM(STATE_TILE, F32), pltpu.VMEM((1, N_STATE), F32), pltpu.VMEM((1, N_STATE), F32)],
        name=f"s5_fwd_{tag}", compiler_params=_cp(dimension_semantics=("arbitrary",)),
    )(u, _scan_order(), p["bbt3"], p["a_re8"], p["a_im8"], p["ct3"], p["d_skip"], p["w_glu"], p["b_glu"])


def _pool_consts():
    w = jnp.repeat(jnp.asarray(POOL_WINDOWS, F32), POOL_GROUP)[None, :]
    return w


POOL_PAD = SUBLANES
POOL_ROWS = TS + MAX_WINDOW + POOL_PAD


def _window_sum(buf, tmp, first, wl, step):
    assert POOL_WINDOWS == (2, 4, 8, 16)
    n = TS + MAX_WINDOW
    lo = first - MAX_WINDOW if step < 0 else first
    src = buf
    for k, dst in zip((1, 2, 4), tmp):
        dst[pl.ds(lo, n), :] = src[pl.ds(lo, n), :] + src[pl.ds(lo + step * k, n), :]
        src = dst
    s2, s4, s8 = (t[pl.ds(first, TS), :] for t in tmp)
    s16 = s8 + tmp[2][pl.ds(first + step * 8, TS), :]
    return jnp.where(wl == 2, s2, jnp.where(wl == 4, s4, jnp.where(wl == 8, s8, s16)))


def _pool_count(i, rows, wl):
    t = (i * TS + 1).astype(F32) + lax.broadcasted_iota(jnp.int32, (rows, 1), 0).astype(F32)
    return jnp.minimum(t, wl)


def _sgu_mix(vl, wpair_ref, lo, hi):
    rows = vl.shape[0]
    chunks = []
    for c in range(rows // CHUNK):
        vc = vl[CHUNK * c:CHUNK * (c + 1), :]
        parts = []
        for q in range(SGU_HEADS // 2):
            vq = vc[:, LANES * q:LANES * (q + 1)]
            rhs = jnp.concatenate([vq * lo, vq * hi], axis=0).astype(BF16)
            parts.append(_dot(wpair_ref[q], rhs))
        chunks.append(jnp.concatenate(parts, axis=1))
    return jnp.concatenate(chunks, axis=0)


def _sgu_front(zuv, lng, lnb, grads=False):
    gelu = _gelu_and_grad if grads else lambda z: (_gelu(z), None)
    u, du = gelu(zuv[:, :D_SGU])
    v, dv = gelu(zuv[:, D_SGU:])
    mu = jnp.mean(v, axis=-1, keepdims=True)
    vc = v - mu
    rs = lax.rsqrt(jnp.mean(vc * vc, axis=-1, keepdims=True) + EPS)
    vn = vc * rs
    return u, vn, rs, vn * lng + lnb, du, dv


def _half_masks():
    lane = lax.broadcasted_iota(jnp.int32, (1, LANES), 1)
    lo = (lane < SGU_HEAD_DIM).astype(F32)
    return lo, 1.0 - lo


def _mix_fwd(x, p, tag):
    s = x.shape[0]

    def body(x_ref, g_ref, w_ref, wl_ref, wp_ref, sc_ref, lng_ref, lnb_ref, wsp_ref, bias_ref,
             za_ref, zuv_ref, h_ref, ob_ref, pooled_ref, oc_ref, buf, *tmp):
        i = pl.program_id(0)
        tile0 = POOL_PAD + MAX_WINDOW

        @pl.when(i == 0)
        def _():
            for ref in (buf, *tmp):
                ref[pl.ds(0, tile0), :] = jnp.zeros((tile0, D_POOL), F32)

        n, _ = _rms(x_ref[...])
        h = (n * g_ref[...]).astype(BF16)
        h_ref[...] = h
        z = _dot_nt(h, w_ref[...])
        za_ref[...] = z[:, :D_SSM]
        zb = z[:, D_SSM:D_SSM + D_POOL]
        zuv = z[:, D_SSM + D_POOL:]
        zuv_ref[...] = zuv
        buf[pl.ds(tile0, TS), :] = zb
        wl = wl_ref[...]
        pooled = (_window_sum(buf, tmp, tile0, wl, -1) / _pool_count(i, TS, wl) - zb).astype(BF16)
        buf[pl.ds(POOL_PAD, MAX_WINDOW), :] = zb[TS - MAX_WINDOW:, :]
        pooled_ref[...] = pooled
        ob_ref[...] = (_dot(pooled, wp_ref[...]) * sc_ref[...]).astype(BF16)
        lo, hi = _half_masks()
        u, _, _, vl, _, _ = _sgu_front(zuv, lng_ref[...], lnb_ref[...])
        mixed = _sgu_mix(vl, wsp_ref, lo, hi) + jnp.tile(bias_ref[...], (TS // CHUNK, 1))
        oc_ref[...] = (u * mixed).astype(BF16)

    return pl.pallas_call(
        body, grid=(s // TS,),
        in_specs=[_row(TS, D_MODEL), _const((1, D_MODEL)), _const((D_IN, D_MODEL)), _const((1, D_POOL)),
                  _const((D_POOL, D_POOL)), _const((1, D_POOL)), _const((1, D_SGU)), _const((1, D_SGU)),
                  _const((SGU_HEADS // 2, CHUNK, 2 * CHUNK)), _const((CHUNK, D_SGU))],
        out_specs=[_row(TS, D_SSM), _row(TS, 2 * D_SGU), _row(TS, D_MODEL), _row(TS, D_POOL), _row(TS, D_POOL),
                   _row(TS, D_SGU)],
        out_shape=[jax.ShapeDtypeStruct((s, D_SSM), F32), jax.ShapeDtypeStruct((s, 2 * D_SGU), F32),
                   jax.ShapeDtypeStruct((s, D_MODEL), BF16), jax.ShapeDtypeStruct((s, D_POOL), BF16),
                   jax.ShapeDtypeStruct((s, D_POOL), BF16), jax.ShapeDtypeStruct((s, D_SGU), BF16)],
        scratch_shapes=[pltpu.VMEM((POOL_ROWS, D_POOL), F32)] * 4,
        name=f"mix_fwd_{tag}", compiler_params=_cp(dimension_semantics=("arbitrary",)),
    )(x, p["g_mix"], p["w_in"], _pool_consts(), p["w_pool_bd"], p["pool_scale"], p["sgu_ln_g"], p["sgu_ln_b"],
      p["ws_pair"], p["bias_sp"])


def _blk_fwd(x0, oa, ob, oc, p, tag, head=None):
    s = x0.shape[0]
    ts = TS_FFN
    n_head = 0 if head is None else len(head)

    def body(x0_ref, oa_ref, ob_ref, oc_ref, wo_ref, g_ref, wg_ref, wu_ref, wd_ref, *refs):
        x1_ref, x2_ref, h2_ref, gt_ref, up_ref, ycat_ref = refs[n_head:n_head + 6]
        ycat = jnp.concatenate([oa_ref[...], ob_ref[...], oc_ref[...]], axis=1)
        ycat_ref[...] = ycat
        x1 = x0_ref[...] + _dot(ycat, wo_ref[...])
        x1_ref[...] = x1
        n, _ = _rms(x1)
        h2 = (n * g_ref[...]).astype(BF16)
        h2_ref[...] = h2
        gt = _dot_nt(h2, wg_ref[...])
        up = _dot_nt(h2, wu_ref[...])
        gt_ref[...] = gt.astype(BF16)
        up_ref[...] = up.astype(BF16)
        act = (gt * _sigmoid(gt) * up).astype(BF16)
        x2 = x1 + _dot(act, wd_ref[...])
        if head is None:
            x2_ref[...] = x2
            return
        t_ref, gf_ref = refs[:n_head]
        loss_ref, dgf_ref = refs[n_head + 6:]

        @pl.when(pl.program_id(0) == 0)
        def _():
            loss_ref[...] = jnp.zeros_like(loss_ref)
            dgf_ref[...] = jnp.zeros_like(dgf_ref)

        gf = gf_ref[...]
        nf, rf = _rms(x2)
        diff = nf * gf - t_ref[...]
        loss_ref[...] += jnp.sum(diff * diff) * (0.5 / D_MODEL)
        dxn, dgp = _rms_bwd(diff * (1.0 / D_MODEL), nf, rf, gf)
        dgf_ref[...] += _colsum8(dgp)
        x2_ref[...] = dxn

    in_specs = [_row(ts, D_MODEL), _row(ts, D_SSM), _row(ts, D_POOL), _row(ts, D_SGU), _const((D_MODEL, D_MODEL)),
                _const((1, D_MODEL)), _const((D_FF, D_MODEL)), _const((D_FF, D_MODEL)), _const((D_FF, D_MODEL))]
    out_specs = [_row(ts, D_MODEL), _row(ts, D_MODEL), _row(ts, D_MODEL), _row(ts, D_FF), _row(ts, D_FF),
                 _row(ts, D_MODEL)]
    out_shape = [jax.ShapeDtypeStruct((s, D_MODEL), F32), jax.ShapeDtypeStruct((s, D_MODEL), F32),
                 jax.ShapeDtypeStruct((s, D_MODEL), BF16), jax.ShapeDtypeStruct((s, D_FF), BF16),
                 jax.ShapeDtypeStruct((s, D_FF), BF16), jax.ShapeDtypeStruct((s, D_MODEL), BF16)]
    args = (x0, oa, ob, oc, p["w_out"], p["g_ffn"], p["w_gate"], p["w_up"], p["w_down"])
    if head is not None:
        in_specs += [_row(ts, D_MODEL), _const((1, D_MODEL))]
        out_specs += [_acc((SUBLANES, LANES)), _acc((SUBLANES, D_MODEL))]
        out_shape += [jax.ShapeDtypeStruct((SUBLANES, LANES), F32), jax.ShapeDtypeStruct((SUBLANES, D_MODEL), F32)]
        args += tuple(head)
    return pl.pallas_call(
        body, grid=(s // ts,), in_specs=in_specs, out_specs=out_specs, out_shape=out_shape,
        name=f"blk_fwd_{tag}", compiler_params=_cp(dimension_semantics=("arbitrary",)),
    )(*args)


def _blk_bwd(dx2, x1, gt, up, p, tag):
    s = dx2.shape[0]
    ts = TS_FFN

    def body(dx2_ref, x1_ref, gt_ref, up_ref, wd_ref, wgt_ref, wut_ref, wo_ref, g_ref,
             dx1_ref, da_ref, db_ref, dc_ref, dgt_ref, dup_ref, act_ref, dg_ref):
        @pl.when(pl.program_id(0) == 0)
        def _():
            dg_ref[...] = jnp.zeros_like(dg_ref)

        dx2v = dx2_ref[...]
        dact = _dot_nt(dx2v.astype(BF16), wd_ref[...])
        gf = gt_ref[...].astype(F32)
        uf = up_ref[...].astype(F32)
        sg = _sigmoid(gf)
        sl = gf * sg
        act_ref[...] = (sl * uf).astype(BF16)
        dgt = (dact * uf * (sg * (1.0 + gf * (1.0 - sg)))).astype(BF16)
        dup = (dact * sl).astype(BF16)
        dgt_ref[...] = dgt
        dup_ref[...] = dup
        dh2 = _dot(dgt, wgt_ref[...]) + _dot(dup, wut_ref[...])
        n, r = _rms(x1_ref[...])
        dxn, dgp = _rms_bwd(dh2, n, r, g_ref[...])
        dg_ref[...] += _colsum8(dgp)
        dx1 = dx2v + dxn
        dx1_ref[...] = dx1
        dy = _dot_nt(dx1.astype(BF16), wo_ref[...])
        da_ref[...] = dy[:, :D_SSM]
        db_ref[...] = dy[:, D_SSM:D_SSM + D_POOL]
        dc_ref[...] = dy[:, D_SSM + D_POOL:]

    return pl.pallas_call(
        body, grid=(s // ts,),
        in_specs=[_row(ts, D_MODEL), _row(ts, D_MODEL), _row(ts, D_FF), _row(ts, D_FF),
                  _const((D_FF, D_MODEL)), _const((D_FF, D_MODEL)), _const((D_FF, D_MODEL)),
                  _const((D_MODEL, D_MODEL)), _const((1, D_MODEL))],
        out_specs=[_row(ts, D_MODEL), _row(ts, D_SSM), _row(ts, D_POOL), _row(ts, D_SGU), _row(ts, D_FF),
                   _row(ts, D_FF), _row(ts, D_FF), _acc((SUBLANES, D_MODEL))],
        out_shape=[jax.ShapeDtypeStruct((s, D_MODEL), F32), jax.ShapeDtypeStruct((s, D_SSM), F32),
                   jax.ShapeDtypeStruct((s, D_POOL), F32), jax.ShapeDtypeStruct((s, D_SGU), F32),
                   jax.ShapeDtypeStruct((s, D_FF), BF16), jax.ShapeDtypeStruct((s, D_FF), BF16),
                   jax.ShapeDtypeStruct((s, D_FF), BF16), jax.ShapeDtypeStruct((SUBLANES, D_MODEL), F32)],
        name=f"blk_bwd_{tag}", compiler_params=_cp(dimension_semantics=("arbitrary",)),
    )(dx2, x1, gt, up, p["w_down"], p["w_gate"], p["w_up"], p["w_out"], p["g_ffn"])


def _s5_bwd(dout, u, y, h_re, h_im, p, tag):
    s = u.shape[0]
    nt = s // TS
    seg = SEG

    def rev(n):
        return pl.BlockSpec((TS, n), lambda i: (nt - 1 - i, 0))

    def body(do_ref, u_ref, y_ref, hr_ref, hi_ref, perm_ref, ar_ref, ai_ref, cb_ref, bb_ref, dsk_ref,
             wglu_ref, bglu_ref,
             du_ref, dct_ref, dbb_ref, dar_ref, dai_ref, dd_ref, dwglu_ref, dbglu_ref,
             gr, gi, hsr, hsi, er, ei, jr, ji, cr, ci):
        @pl.when(pl.program_id(0) == 0)
        def _():
            for ref in (cr, ci, dct_ref, dbb_ref, dar_ref, dai_ref, dd_ref, dwglu_ref, dbglu_ref):
                ref[...] = jnp.zeros_like(ref)

        perm = perm_ref[...]
        uv = _to_scan_order(perm, u_ref[...])
        yv = y_ref[...]
        dov = _to_scan_order(perm, do_ref[...])
        g, gelu_dy = _gelu_and_grad(yv)
        gb = g.astype(BF16)
        sg = _sigmoid(_dot(gb, wglu_ref[...]) + bglu_ref[...])
        dpre = dov * g * sg * (1.0 - sg)
        dpb = dpre.astype(BF16)
        dwglu_ref[...] += _dot_tn(gb, dpb)
        dbglu_ref[...] += _colsum8(dpre)
        dy = (dov * sg + _dot_nt(dpb, wglu_ref[...])) * gelu_dy
        dd_ref[...] += _colsum8(dy * uv)
        dyb = dy.astype(BF16)
        hsr[...] = hr_ref[...].astype(F32)
        hsi[...] = hi_ref[...].astype(F32)
        ub = uv.astype(BF16)
        dus = []

        def state_cotangents(j):
            dct_ref[j] += _dot_tn(_lanes(dyb, j), _state_cat(hr_ref, hi_ref, j))
            _state_split(gr, gi, j, _dot(_lanes(dyb, j), cb_ref[j]))

        def input_cotangents(j):
            gb_j = _state_cat(gr, gi, j)
            dbb_ref[j] += _dot_tn(_lanes(ub, j), gb_j)
            dus.append(_dot(gb_j, bb_ref[j]))

        def scan(j):
            cols = _states(j)
            ar = ar_ref[:, cols]
            ai = -ai_ref[:, cols]
            g_r = g_i = jnp.zeros((SUBLANES, SLAB_STATES), F32)
            for k in range(seg - 1, -1, -1):
                rows = pl.ds(SUBLANES * k, SUBLANES)
                n_r, n_i = _cmul(ar, ai, g_r, g_i)
                g_r = n_r + gr[rows, cols]
                g_i = n_i + gi[rows, cols]
            er[:, cols] = g_r
            ei[:, cols] = g_i
            pr, pi = _cpow(ar[0:1, :], ai[0:1, :], seg)
            c_r = cr[:, cols]
            c_i = ci[:, cols]
            for q in range(SUBLANES - 1, -1, -1):
                jr[q:q + 1, cols] = c_r
                ji[q:q + 1, cols] = c_i
                n_r, n_i = _cmul(pr, pi, c_r, c_i)
                c_r = n_r + er[q:q + 1, cols]
                c_i = n_i + ei[q:q + 1, cols]
            cr[:, cols] = c_r
            ci[:, cols] = c_i
            g_r = jr[:, cols]
            g_i = ji[:, cols]
            a_r = a_i = jnp.zeros((SUBLANES, SLAB_STATES), F32)
            for k in range(seg - 1, -1, -1):
                rows = pl.ds(SUBLANES * k, SUBLANES)
                h_r = hsr[rows, cols]
                h_i = hsi[rows, cols]
                a_r = a_r + g_r * h_r + g_i * h_i
                a_i = a_i + g_i * h_r - g_r * h_i
                n_r, n_i = _cmul(ar, ai, g_r, g_i)
                g_r = n_r + gr[rows, cols]
                g_i = n_i + gi[rows, cols]
                gr[rows, cols] = g_r
                gi[rows, cols] = g_i
            dar_ref[:, cols] += a_r
            dai_ref[:, cols] += a_i

        for stage in (state_cotangents, scan, input_cotangents):
            for j in range(N_USLAB):
                stage(j)
        du = dy * dsk_ref[...] + jnp.concatenate(dus, axis=1)
        du_ref[...] = _dot_tn(perm, du.astype(BF16)).astype(BF16)

    big = (TS, N_STATE)
    return pl.pallas_call(
        body, grid=(nt,),
        in_specs=[rev(D_SSM), rev(D_SSM), rev(D_SSM), rev(N_STATE), rev(N_STATE), _const((TS, TS)),
                  _const(STATE_TILE), _const(STATE_TILE), _const(S5_IN), _const(S5_OUT), _const((1, D_SSM)),
                  _const((D_SSM, D_SSM)), _const((1, D_SSM))],
        out_specs=[rev(D_SSM), _acc(S5_IN), _acc(S5_IN), _acc(STATE_TILE), _acc(STATE_TILE),
                   _acc((SUBLANES, D_SSM)), _acc((D_SSM, D_SSM)), _acc((SUBLANES, D_SSM))],
        out_shape=[jax.ShapeDtypeStruct((s, D_SSM), BF16), jax.ShapeDtypeStruct(S5_IN, F32),
                   jax.ShapeDtypeStruct(S5_IN, F32), jax.ShapeDtypeStruct(STATE_TILE, F32),
                   jax.ShapeDtypeStruct(STATE_TILE, F32), jax.ShapeDtypeStruct((SUBLANES, D_SSM), F32),
                   jax.ShapeDtypeStruct((D_SSM, D_SSM), F32), jax.ShapeDtypeStruct((SUBLANES, D_SSM), F32)],
        scratch_shapes=[pltpu.VMEM(big, F32), pltpu.VMEM(big, F32), pltpu.VMEM(big, F32), pltpu.VMEM(big, F32),
                        pltpu.VMEM(STATE_TILE, F32), pltpu.VMEM(STATE_TILE, F32), pltpu.VMEM(STATE_TILE, F32),
                        pltpu.VMEM(STATE_TILE, F32), pltpu.VMEM((1, N_STATE), F32), pltpu.VMEM((1, N_STATE), F32)],
        name=f"s5_bwd_{tag}", compiler_params=_cp(dimension_semantics=("arbitrary",)),
    )(dout, u, y, h_re, h_im, _scan_order(), p["a_re8"], p["a_im8"], p["cb3"], p["bb3"], p["d_skip"], p["w_glu"],
      p["b_glu"])


def _mix_bwd(dza, db, dc, pooled, zuv, x0, dx1, p, tag):
    s = x0.shape[0]
    nt = s // TS

    def rev(n):
        return pl.BlockSpec((TS, n), lambda i: (nt - 1 - i, 0))

    def body(da_ref, db_ref, dc_ref, po_ref, z_ref, x_ref, dx1_ref, wl_ref, wp_ref, wpt_ref, sc_ref, lng_ref, lnb_ref,
             wsp_ref, wspt_ref, bias_ref, win_ref, g_ref,
             dx0_ref, dz_ref, dwp_ref, dsc_ref, dws_ref, dbias_ref, dlng_ref, dlnb_ref, dg_ref, buf, *tmp):
        step = pl.program_id(0)
        i = nt - 1 - step

        @pl.when(step == 0)
        def _():
            for ref in (dwp_ref, dsc_ref, dws_ref, dbias_ref, dlng_ref, dlnb_ref, dg_ref):
                ref[...] = jnp.zeros_like(ref)
            for ref in (buf, *tmp):
                ref[pl.ds(TS, POOL_ROWS - TS), :] = jnp.zeros((POOL_ROWS - TS, D_POOL), F32)

        wl = wl_ref[...]
        sc = sc_ref[...]
        dob = db_ref[...]
        pooled_b = po_ref[...]
        dsc_ref[...] += _colsum8(dob * _dot(pooled_b, wp_ref[...]))
        dmixb = (dob * sc).astype(BF16)
        dwp_ref[...] += _dot_tn(pooled_b, dmixb)
        dpool = _dot(dmixb, wpt_ref[...])
        dq = dpool / _pool_count(i, TS, wl)
        buf[pl.ds(0, TS), :] = dq
        dzb = _window_sum(buf, tmp, 0, wl, 1) - dpool
        buf[pl.ds(TS, MAX_WINDOW), :] = dq[:MAX_WINDOW, :]

        lo, hi = _half_masks()
        lng = lng_ref[...]
        u, vn, rs, vl, gelu_du, gelu_dv = _sgu_front(z_ref[...], lng, lnb_ref[...], grads=True)
        mixed = _sgu_mix(vl, wsp_ref, lo, hi) + jnp.tile(bias_ref[...], (TS // CHUNK, 1))
        doc = dc_ref[...]
        dzu = doc * mixed * gelu_du
        dmix = doc * u
        dbias = dbias_ref[...]
        for c in range(TS // CHUNK):
            dmc = dmix[CHUNK * c:CHUNK * (c + 1), :]
            dbias = dbias + dmc
            vlc = vl[CHUNK * c:CHUNK * (c + 1), :].astype(BF16)
            for q in range(SGU_HEADS // 2):
                dmq = _lanes(dmc, q)
                vq = _lanes(vlc, q)
                dws_ref[2 * q] += _dot_nt((dmq * lo).astype(BF16), vq)
                dws_ref[2 * q + 1] += _dot_nt((dmq * hi).astype(BF16), vq)
        dbias_ref[...] = dbias
        dvl = _sgu_mix(dmix, wspt_ref, lo, hi)
        dlng_ref[...] += _colsum8(dvl * vn)
        dlnb_ref[...] += _colsum8(dvl)
        dvn = dvl * lng
        dv = rs * (dvn - jnp.mean(dvn, axis=-1, keepdims=True) - vn * jnp.mean(dvn * vn, axis=-1, keepdims=True))

        dz = jnp.concatenate([da_ref[...], dzb.astype(BF16), dzu.astype(BF16), (dv * gelu_dv).astype(BF16)], axis=1)
        dz_ref[...] = dz
        n, r = _rms(x_ref[...])
        dxn, dgp = _rms_bwd(_dot(dz, win_ref[...]), n, r, g_ref[...])
        dg_ref[...] += _colsum8(dgp)
        dx0_ref[...] = dx1_ref[...] + dxn

    pair = (SGU_HEADS // 2, CHUNK, 2 * CHUNK)
    return pl.pallas_call(
        body, grid=(nt,),
        in_specs=[rev(D_SSM), rev(D_POOL), rev(D_SGU), rev(D_POOL), rev(2 * D_SGU), rev(D_MODEL), rev(D_MODEL),
                  _const((1, D_POOL)), _const((D_POOL, D_POOL)), _const((D_POOL, D_POOL)), _const((1, D_POOL)),
                  _const((1, D_SGU)), _const((1, D_SGU)), _const(pair), _const(pair), _const((CHUNK, D_SGU)),
                  _const((D_IN, D_MODEL)), _const((1, D_MODEL))],
        out_specs=[rev(D_MODEL), rev(D_IN), _acc((D_POOL, D_POOL)), _acc((SUBLANES, D_POOL)),
                   _acc((SGU_HEADS, CHUNK, CHUNK)), _acc((CHUNK, D_SGU)), _acc((SUBLANES, D_SGU)),
                   _acc((SUBLANES, D_SGU)), _acc((SUBLANES, D_MODEL))],
        out_shape=[jax.ShapeDtypeStruct((s, D_MODEL), F32), jax.ShapeDtypeStruct((s, D_IN), BF16),
                   jax.ShapeDtypeStruct((D_POOL, D_POOL), F32), jax.ShapeDtypeStruct((SUBLANES, D_POOL), F32),
                   jax.ShapeDtypeStruct((SGU_HEADS, CHUNK, CHUNK), F32), jax.ShapeDtypeStruct((CHUNK, D_SGU), F32),
                   jax.ShapeDtypeStruct((SUBLANES, D_SGU), F32), jax.ShapeDtypeStruct((SUBLANES, D_SGU), F32),
                   jax.ShapeDtypeStruct((SUBLANES, D_MODEL), F32)],
        scratch_shapes=[pltpu.VMEM((POOL_ROWS, D_POOL), F32)] * 4,
        name=f"mix_bwd_{tag}", compiler_params=_cp(dimension_semantics=("arbitrary",)),
    )(dza, db, dc, pooled, zuv, x0, dx1, _pool_consts(), p["w_pool_bd"], p["w_pool_bd_t"], p["pool_scale"],
      p["sgu_ln_g"], p["sgu_ln_b"], p["ws_pair"], p["ws_pair_t"], p["bias_sp"], p["w_in"], p["g_mix"])


def _atb(a, b, tag, token=None):
    s, ka = a.shape
    kb = b.shape[1]
    ts = ATB_ROWS
    tn = min(kb, ATB_COLS)
    ns = s // ts
    after = [] if token is None else [token]

    def body(a_ref, b_ref, *rest):
        o_ref = rest[-1]

        @pl.when(pl.program_id(1) == 0)
        def _():
            o_ref[...] = jnp.zeros_like(o_ref)

        o_ref[...] += _dot_tn(a_ref[...].astype(BF16), b_ref[...].astype(BF16))

    return pl.pallas_call(
        body, grid=(kb // tn, ns),
        in_specs=[pl.BlockSpec((ts, ka), lambda j, i: (i, 0)), pl.BlockSpec((ts, tn), lambda j, i: (i, j))]
        + [pl.BlockSpec(memory_space=pl.ANY)] * len(after),
        out_specs=pl.BlockSpec((ka, tn), lambda j, i: (0, j)),
        out_shape=jax.ShapeDtypeStruct((ka, kb), F32),
        name=f"atb_{tag}", compiler_params=_cp(dimension_semantics=("arbitrary", "arbitrary")),
    )(a, b, *after)


def _s5_discretise(a_re, a_im, log_dt, b_re, b_im):
    dt = jnp.exp(log_dt)[:, None]
    mag = jnp.exp(a_re * dt)
    ar = mag * jnp.cos(a_im * dt)
    ai = mag * jnp.sin(a_im * dt)
    den = a_re * a_re + a_im * a_im
    f_re = ((ar - 1.0) * a_re + ai * a_im) / den
    f_im = (ai * a_re - (ar - 1.0) * a_im) / den
    bb_re = f_re[..., None] * b_re - f_im[..., None] * b_im
    bb_im = f_re[..., None] * b_im + f_im[..., None] * b_re
    return ar, ai, bb_re, bb_im


def _block_diag(blocks):
    g, r, c = blocks.shape
    eye = jnp.eye(g, dtype=blocks.dtype)
    return (blocks[:, :, None, :] * eye[:, None, :, None]).reshape(g * r, g * c)


def _block_diag_extract(m, g):
    r = m.shape[0] // g
    c = m.shape[1] // g
    eye = jnp.eye(g, dtype=m.dtype)
    return jnp.sum(m.reshape(g, r, g, c) * eye[:, None, :, None], axis=2)


GROUPS_PER_SLAB = N_GROUPS // N_USLAB


def _slab_diag(blocks):
    k = GROUPS_PER_SLAB
    _, r, c = blocks.shape
    eye = jnp.eye(k, dtype=blocks.dtype)
    spread = blocks.reshape(N_USLAB, k, r, 1, c) * eye[None, :, None, :, None]
    return spread.reshape(N_USLAB, k * r, k * c)


def _slab_diag_extract(m):
    k = GROUPS_PER_SLAB
    r, c = m.shape[1] // k, m.shape[2] // k
    eye = jnp.eye(k, dtype=m.dtype)
    return jnp.sum(m.reshape(N_USLAB, k, r, k, c) * eye[None, :, None, :, None], axis=3).reshape(N_GROUPS, r, c)


def _state_slabs(v):
    return jnp.broadcast_to(v.reshape(1, N_STATE), STATE_TILE)


def _tril():
    return jnp.tril(jnp.ones((CHUNK, CHUNK), dtype=bool))


def _layer_params(w, l):
    row = lambda v: v.reshape(1, -1)
    t = lambda m: jnp.swapaxes(m, -1, -2)
    ar, ai, bb_re, bb_im = _s5_discretise(w["A_re"][l], w["A_im"][l], w["log_dt"][l], w["B_re"][l], w["B_im"][l])
    bbt3 = jnp.concatenate([_slab_diag(t(bb_re)), _slab_diag(t(bb_im))], axis=2).astype(BF16)
    ct3 = jnp.concatenate([_slab_diag(t(w["C_re"][l])), -_slab_diag(t(w["C_im"][l]))], axis=1).astype(BF16)
    ws = jnp.where(_tril()[None], w["w_spatial"][l], 0.0)
    pair = lambda m: jnp.stack([jnp.concatenate([m[2 * q], m[2 * q + 1]], axis=1)
                                for q in range(SGU_HEADS // 2)]).astype(BF16)
    wp = _block_diag(w["w_pool"][l]).astype(BF16)
    p = dict(
        g_mix=row(w["g_mix"][l]), g_ffn=row(w["g_ffn"][l]), d_skip=row(w["D_skip"][l]), b_glu=row(w["b_glu"][l]),
        pool_scale=row(w["pool_scale"][l]), sgu_ln_g=row(w["sgu_ln_g"][l]), sgu_ln_b=row(w["sgu_ln_b"][l]),
        a_re8=_state_slabs(ar), a_im8=_state_slabs(ai),
        bbt3=bbt3, bb3=t(bbt3), ct3=ct3, cb3=t(ct3),
        w_pool_bd=wp, w_pool_bd_t=t(wp), ws_pair=pair(ws), ws_pair_t=pair(t(ws)),
        bias_sp=jnp.repeat(t(w["b_spatial"][l]), SGU_HEAD_DIM, axis=1),
    )
    return p


MIX_WEIGHTS = ("w_in", "w_glu")
FFN_WEIGHTS = ("w_out", "w_gate", "w_up", "w_down")


def _with_big(p, mats):
    p.update(mats)


def _rows_sum(v):
    return jnp.sum(v, axis=0)


ATB_COLS = 1024
ATB_ROWS = 1024


def _after(v, token):
    return v if token is None else v + token[0, 0]


def _join(a, b):
    return b if a is None else a if b is None else a + b


def _layer_bwd(dx2, sv, p, w, l, tag, hooks, token):
    t = lambda m: jnp.swapaxes(m, -1, -2)
    dx1, da, db, dc, dgt, dup, act, dg_ffn = _blk_bwd(dx2, sv["x1"], sv["gt"], sv["up"],
                                                      dict(p, g_ffn=_after(p["g_ffn"], token)), tag)
    token = hooks["tick"]([dx1])
    token = _join(token, hooks["on_grads"](l, "ffn", {
        "w_down": _atb(act, dx2, tag + "_wd", token), "w_gate": _atb(dgt, sv["h2"], tag + "_wg", token),
        "w_up": _atb(dup, sv["h2"], tag + "_wu", token), "w_out": _atb(sv["ycat"], dx1, tag + "_wo", token)}))
    g = {}
    g["g_ffn"] = _rows_sum(dg_ffn)
    dza, dc3, dbbt3, dar8, dai8, dd8, dwglu, dbglu8 = _s5_bwd(
        da, sv["za"], sv["y"], sv["h_re"], sv["h_im"], dict(p, d_skip=_after(p["d_skip"], token)), tag)
    token = hooks["tick"]([dza])
    dx0, dz, dwp, dsc8, dws, dbias, dlng8, dlnb8, dg_mix = _mix_bwd(
        dza, db, dc, sv["pooled"], sv["zuv"], sv["x0"], dx1, dict(p, pool_scale=_after(p["pool_scale"], token)), tag)
    g["g_mix"] = _rows_sum(dg_mix)
    g["b_glu"] = _rows_sum(dbglu8)
    g["D_skip"] = _rows_sum(dd8)
    half = N_STATE // N_USLAB
    g["C_re"] = _slab_diag_extract(dc3[:, :, :half])
    g["C_im"] = -_slab_diag_extract(dc3[:, :, half:])
    dar = jnp.sum(dar8, axis=0).reshape(N_GROUPS, SSM_STATE)
    dai = jnp.sum(dai8, axis=0).reshape(N_GROUPS, SSM_STATE)
    dbb_re = t(_slab_diag_extract(dbbt3[:, :, :half]))
    dbb_im = t(_slab_diag_extract(dbbt3[:, :, half:]))
    _, disc_vjp = jax.vjp(_s5_discretise, w["A_re"][l], w["A_im"][l], w["log_dt"][l], w["B_re"][l], w["B_im"][l])
    g["A_re"], g["A_im"], g["log_dt"], g["B_re"], g["B_im"] = disc_vjp((dar, dai, dbb_re, dbb_im))
    g["w_pool"] = _block_diag_extract(dwp, len(POOL_WINDOWS))
    g["pool_scale"] = _rows_sum(dsc8)
    g["sgu_ln_g"] = _rows_sum(dlng8)
    g["sgu_ln_b"] = _rows_sum(dlnb8)
    g["w_spatial"] = jnp.where(_tril()[None], dws, 0.0)
    g["b_spatial"] = t(jnp.sum(dbias.reshape(CHUNK, SGU_HEADS, SGU_HEAD_DIM), axis=-1))
    token = hooks["on_small"](l, g)
    token = hooks["on_grads"](l, "mix", {"w_in": _atb(dz, sv["h1"], tag + "_wi", token), "w_glu": dwglu})
    return dx0, token


def _local_step(x, target, w, hooks):
    params = [_layer_params(w, l) for l in range(DEPTH)]
    saved = []
    h = x
    for l in range(DEPTH):
        p, tag = params[l], f"l{l}"
        _with_big(p, hooks["get_big"](l, "mix", [h]))
        za, zuv, h1, ob, pooled, oc = _mix_fwd(h, p, tag)
        oa, y, h_re, h_im = _s5_fwd(za, p, tag)
        _with_big(p, hooks["get_big"](l, "ffn", [oa, ob, oc]))
        head = (target, w["g_final"].reshape(1, -1)) if l == DEPTH - 1 else None
        x1, x2, h2, gt, up, ycat, *loss_parts = _blk_fwd(h, oa, ob, oc, p, tag, head)
        saved.append(dict(x0=h, za=za, zuv=zuv, h1=h1, ycat=ycat, y=y, h_re=h_re, h_im=h_im, pooled=pooled, x1=x1,
                          h2=h2, gt=gt, up=up))
        h = x2
    dx = h
    loss8, dgf8 = loss_parts
    grads = [None] * DEPTH

    def on_small(l, g_l):
        grads[l] = g_l
        if l > 0:
            return None
        g = {n: jnp.stack([grads[k][n] for k in range(DEPTH)]) for n in SMALL if n != "g_final"}
        g["g_final"] = _rows_sum(dgf8)
        return hooks["on_small"](g, loss8[0, 0])

    token = None
    for l in reversed(range(DEPTH)):
        dx, token = _layer_bwd(dx, saved[l], params[l], w, l, f"l{l}", dict(hooks, on_small=on_small), token)
    return dx


_ANY = pl.BlockSpec(memory_space=pl.ANY)
_MESH = pl.DeviceIdType.MESH


def _place():
    return lax.axis_index("x"), lax.axis_index("y"), lax.axis_index("c")


def _other_chips(x, y):
    return [(1 - x, y), (x, 1 - y), (1 - x, 1 - y)]


def _dma_sems(n):
    return pltpu.SemaphoreType.DMA((n,))


def _remote(src, dst, send_sems, recv_sems, k, to):
    return pltpu.make_async_remote_copy(src_ref=src, dst_ref=dst, send_sem=send_sems.at[k], recv_sem=recv_sems.at[k],
                                        device_id=to, device_id_type=_MESH)


_HBM = pl.BlockSpec(memory_space=pltpu.HBM)
_SEM = pl.BlockSpec(memory_space=pltpu.SEMAPHORE)
_EFFECT = pltpu.SideEffectType.DATAFLOW_SIDE_EFFECTING
N_REL = N_CHIPS - 1


def _gather_plan(x, y, c, srcs, lands):
    plan = []
    for l in lands:
        r = l.shape[0] // N_CHIPS
        rows = l.at[pl.ds((2 * x + y) * r, r)]
        plan += [(rows, rows, (cx, cy, c)) for cx, cy in _other_chips(x, y)]
    return plan


def _half_rows(land, chip, c):
    h = land.shape[0] // (2 * N_CHIPS)
    return land.at[pl.ds((2 * chip + c) * h, h)]


def _gather_half_plan(x, y, c, srcs, lands):
    return [(_half_rows(l, 2 * x + y, c), _half_rows(l, 2 * x + y, c), (cx, cy, c))
            for l in lands for cx, cy in _other_chips(x, y)]


def _forward_halves(lands, tag):
    nw = len(lands)

    def body(*refs):
        ins = refs[:nw]
        send_sems, recv_sems = refs[2 * nw:]
        x, y, c = _place()
        chips = [2 * cx + cy for cx, cy in _other_chips(x, y)]
        sends = [_remote(_half_rows(ins[i], k, c), _half_rows(ins[i], k, c), send_sems, recv_sems, N_REL * i + j,
                         (x, y, 1 - c)) for i in range(nw) for j, k in enumerate(chips)]
        for cp in sends:
            cp.start()
        for i in range(nw):
            for j, k in enumerate(chips):
                sends[N_REL * i + j].wait_send()
                _remote(_half_rows(ins[i], k, c), _half_rows(ins[i], k, 1 - c), send_sems, recv_sems, N_REL * i + j,
                        (x, y, 1 - c)).wait_recv()

    return pl.pallas_call(
        body, out_shape=[jax.ShapeDtypeStruct(a.shape, a.dtype) for a in lands], in_specs=[_ANY] * nw,
        out_specs=[_ANY] * nw, input_output_aliases={i: i for i in range(nw)},
        scratch_shapes=[_dma_sems(N_REL * nw), _dma_sems(N_REL * nw)], name=f"forward_halves_{tag}",
    )(*lands)


def _sibling_plan(x, y, c, srcs, lands):
    return [(s.at[:, 1 - c], l, (x, y, 1 - c)) for s, l in zip(srcs, lands)]


def _slab_plan(x, y, c, srcs, lands):
    return [(s.at[2 * cx + cy], l.at[j], (cx, cy, c))
            for s, l in zip(srcs, lands) for j, (cx, cy) in enumerate(_other_chips(x, y))]


def _plan_copies(plan, srcs, lands, send_sems, recv_sems):
    x, y, c = _place()
    return [_remote(s, d, send_sems, recv_sems, k, to) for k, (s, d, to) in enumerate(plan(x, y, c, srcs, lands))]


def _hbm(a):
    return pltpu.with_memory_space_constraint(a, pltpu.HBM)


def _everyone_plan(x, y, c, srcs, lands):
    me = 4 * x + 2 * y + c
    peers = [(x, y, 1 - c)] + [(cx, cy, cc) for cx, cy in _other_chips(x, y) for cc in (c, 1 - c)]
    return [(s, l.at[me], peer) for s, l in zip(srcs, lands) for peer in peers]


def _copies_start(name, plan, srcs, lands, ncopies):
    ns, n = len(srcs), len(srcs) + len(lands)

    def body(*refs):
        for cp in _plan_copies(plan, refs[:ns], refs[ns:n], refs[n], refs[n + 1]):
            cp.start()
        refs[-1][...] = jnp.zeros_like(refs[-1])

    ref_out = [pltpu.HBM(a.shape, a.dtype) for a in (*srcs, *lands)]
    out = pl.pallas_call(
        body, name=name, in_specs=[_HBM] * n,
        out_shape=(_dma_sems(ncopies), _dma_sems(ncopies), *ref_out, jax.ShapeDtypeStruct((SUBLANES, LANES), F32)),
        out_specs=(_SEM, _SEM, *[_HBM] * n, pl.BlockSpec(memory_space=pltpu.VMEM)),
        input_output_aliases={i: 2 + i for i in range(n)},
        compiler_params=pltpu.CompilerParams(has_side_effects=_EFFECT),
    )(*[_hbm(a) for a in (*srcs, *lands)])
    return dict(name=name, plan=plan, sems=out[:2], srcs=out[2:2 + ns], lands=out[2 + ns:2 + n], token=out[-1])


def _copies_wait(started, after):
    ns = len(started["srcs"])
    n = ns + len(started["lands"])
    plan = started["plan"]

    def body(*refs):
        for cp in _plan_copies(plan, refs[:ns], refs[ns:n], refs[n], refs[n + 1]):
            cp.wait_send()
            cp.wait_recv()

    args = (*started["srcs"], *started["lands"])
    out = pl.pallas_call(
        body, name=started["name"] + "_wait", out_shape=[pltpu.HBM(a.shape, a.dtype) for a in args],
        in_specs=[_HBM] * n + [_SEM, _SEM] + [_ANY] * len(after), out_specs=[_HBM] * n,
        input_output_aliases={i: i for i in range(n)},
        compiler_params=pltpu.CompilerParams(has_side_effects=_EFFECT),
    )(*args, *started["sems"], *after)
    return out[:ns], out[ns:]


def _place_shards(ws, layer, sel, after, tag):
    nw = len(ws)

    def body(sel_ref, *refs):
        for i in range(nw):
            refs[nw + len(after) + i][...] = refs[i][...].astype(BF16)

    return pl.pallas_call(
        body, grid_spec=pltpu.PrefetchScalarGridSpec(
            num_scalar_prefetch=1, grid=(1,),
            in_specs=[pl.BlockSpec((None,) + a.shape[1:], lambda i, s: (layer, 0, 0)) for a in ws] + [_ANY] * len(after),
            out_specs=[pl.BlockSpec(a.shape[1:], lambda i, s: (s[1], 0)) for a in ws]),
        out_shape=[jax.ShapeDtypeStruct((N_CHIPS * a.shape[1], a.shape[2]), BF16) for a in ws],
        name=f"place_shards_{tag}", compiler_params=_cp(dimension_semantics=("arbitrary",)),
    )(sel, *ws, *after)


def _share_halves(fs, layer, tag):
    nw = len(fs)

    def body(*refs):
        ins = refs[:nw]
        send_sems, recv_sems = refs[2 * nw:]
        x, y, c = _place()

        def half(i, who):
            h = ins[i].shape[1] // 2
            return ins[i].at[layer, pl.ds(who * h, h)]

        sends = [_remote(half(i, c), half(i, c), send_sems, recv_sems, i, (x, y, 1 - c)) for i in range(nw)]
        for cp in sends:
            cp.start()
        for i in range(nw):
            sends[i].wait_send()
            _remote(half(i, c), half(i, 1 - c), send_sems, recv_sems, i, (x, y, 1 - c)).wait_recv()

    return pl.pallas_call(
        body, out_shape=[jax.ShapeDtypeStruct(f.shape, f.dtype) for f in fs], in_specs=[_ANY] * nw,
        out_specs=[_ANY] * nw, input_output_aliases={i: i for i in range(nw)},
        scratch_shapes=[_dma_sems(nw), _dma_sems(nw)], name=f"share_halves_{tag}",
    )(*fs)


def _add_halves(g4s, recvs, sel, tag):
    nw = len(g4s)

    def body(sel_ref, *refs):
        for i in range(nw):
            refs[2 * nw + i][...] = (refs[i][...] + refs[nw + i][...]).astype(BF16)

    mine = [pl.BlockSpec((None, None) + g.shape[2:], lambda k, s: (k, s[0], 0, 0)) for g in g4s]
    slab = [pl.BlockSpec((None,) + g.shape[2:], lambda k, s: (k, 0, 0)) for g in g4s]
    return pl.pallas_call(
        body, grid_spec=pltpu.PrefetchScalarGridSpec(num_scalar_prefetch=1, grid=(N_CHIPS,), in_specs=mine + slab,
                                                     out_specs=slab),
        out_shape=[jax.ShapeDtypeStruct(r.shape, BF16) for r in recvs], name=f"add_halves_{tag}",
        compiler_params=_cp(dimension_semantics=("arbitrary",)),
    )(sel, *g4s, *recvs)


def _add_chips(ps, slabs, fs, layer, sel, tag):
    nw = len(ps)
    old = [f for f in fs if f is not None]

    def body(sel_ref, *refs):
        outs = refs[2 * nw + len(old):]
        for i in range(nw):
            acc = refs[i][...].astype(F32)
            for j in range(N_REL):
                acc = acc + refs[nw + i][j].astype(F32)
            outs[i][...] = acc

    shapes = [(DEPTH, 2 * p.shape[1], p.shape[2]) for p in ps]
    in_specs = [pl.BlockSpec((None,) + p.shape[1:], lambda i, s: (s[1], 0, 0)) for p in ps]
    in_specs += [pl.BlockSpec(sl.shape, lambda i, s: (0, 0, 0)) for sl in slabs]
    in_specs += [_ANY] * len(old)
    first_old = 1 + 2 * nw
    aliases, k = {}, 0
    for i, f in enumerate(fs):
        if f is not None:
            aliases[first_old + k] = i
            k += 1
    return pl.pallas_call(
        body, grid_spec=pltpu.PrefetchScalarGridSpec(
            num_scalar_prefetch=1, grid=(1,), in_specs=in_specs,
            out_specs=[pl.BlockSpec((None,) + p.shape[1:], lambda i, s: (layer, s[0], 0)) for p in ps]),
        out_shape=[jax.ShapeDtypeStruct(sh, F32) for sh in shapes], input_output_aliases=aliases,
        name=f"add_chips_{tag}", compiler_params=_cp(dimension_semantics=("arbitrary",)),
    )(sel, *ps, *slabs, *old)


def _adamw_math(w, g, m, v):
    m = ADAM_B1 * m + (1.0 - ADAM_B1) * g
    v = ADAM_B2 * v + (1.0 - ADAM_B2) * (g * g)
    m_hat = m / (1.0 - ADAM_B1 ** ADAM_STEP)
    v_hat = v / (1.0 - ADAM_B2 ** ADAM_STEP)
    delta = -ADAM_LR * (m_hat / (jnp.sqrt(v_hat) + ADAM_EPS) + ADAM_WD * w)
    return delta, m, v


ADAM_ROWS = 512


def _row_tile(rows, most):
    return max(t for t in range(SUBLANES, most + 1, SUBLANES) if rows % t == 0)


def _adamw(w, g, m, v, tag):
    depth, rows, cols = w.shape
    tr = _row_tile(rows, ADAM_ROWS)

    def body(w_ref, g_ref, m_ref, v_ref, d_ref, nm_ref, nv_ref):
        d, nm, nv = _adamw_math(w_ref[...], g_ref[...], m_ref[...], v_ref[...])
        d_ref[...] = d
        nm_ref[...] = nm
        nv_ref[...] = nv

    spec = pl.BlockSpec((None, tr, cols), lambda l, i: (l, i, 0))
    return pl.pallas_call(
        body, grid=(depth, rows // tr), in_specs=[spec] * 4, out_specs=[spec] * 3,
        out_shape=[jax.ShapeDtypeStruct(w.shape, F32)] * 3, name=f"adamw_{tag}",
        compiler_params=_cp(dimension_semantics=("arbitrary", "arbitrary")),
    )(w, g, m, v)


SMALL_TILE = 512
PRECISE = ("g_final",)
COARSE = [n for n in SMALL if n not in PRECISE]


def _small_reduce_adamw(gathered, w, m, v, tag):
    rows = w.shape[0]
    tr = math.gcd(rows, SMALL_TILE)

    def body(ga_ref, w_ref, m_ref, v_ref, g_ref, d_ref, nm_ref, nv_ref):
        g = ga_ref[0].astype(F32)
        for k in range(1, N_DEV):
            g = g + ga_ref[k].astype(F32)
        g_ref[...] = g
        d, nm, nv = _adamw_math(w_ref[...], g, m_ref[...], v_ref[...])
        d_ref[...] = d
        nm_ref[...] = nm
        nv_ref[...] = nv

    spec = _row(tr, LANES)
    return pl.pallas_call(
        body, grid=(rows // tr,),
        in_specs=[pl.BlockSpec((N_DEV, tr, LANES), lambda i: (0, i, 0)), spec, spec, spec], out_specs=[spec] * 4,
        out_shape=[jax.ShapeDtypeStruct((rows, LANES), F32)] * 4, name=f"small_reduce_adamw_{tag}",
        compiler_params=_cp(dimension_semantics=("arbitrary",)),
    )(gathered, w, m, v)


def _exchange_form(n, a):
    return jnp.swapaxes(a, 1, 2) if n in TRANSPOSED else a


PACK_ROWS = 16


def _rows_of(size):
    return -(-size // (LANES * PACK_ROWS)) * PACK_ROWS


def _pack(vals, names, extra=None):
    parts = [vals[n].reshape(-1) for n in names] + ([] if extra is None else [extra.reshape(1)])
    tiles = [jnp.pad(a, (0, _rows_of(a.size) * LANES - a.size)).reshape(-1, LANES) for a in parts]
    rows = sum(t.shape[0] for t in tiles)
    if rows > SMALL_TILE:
        tiles.append(jnp.zeros((-rows % SMALL_TILE, LANES), tiles[0].dtype))
    return jnp.concatenate(tiles, axis=0)


def _unpack(buf, like, names):
    out, row = {}, 0
    for n in names:
        rows = _rows_of(like[n].size)
        out[n] = buf[row:row + rows].reshape(-1)[:like[n].size].reshape(like[n].shape)
        row += rows
    return out, buf[row:]


def kernel(x, g_mix, w_in, A_re, A_im, log_dt, B_re, B_im, C_re, C_im, D_skip, w_glu, b_glu, w_pool, pool_scale, sgu_ln_g, sgu_ln_b, w_spatial, b_spatial, w_out, g_ffn, w_gate, w_up, w_down, g_final, loss_target, m_g_mix, m_w_in, m_A_re, m_A_im, m_log_dt, m_B_re, m_B_im, m_C_re, m_C_im, m_D_skip, m_w_glu, m_b_glu, m_w_pool, m_pool_scale, m_sgu_ln_g, m_sgu_ln_b, m_w_spatial, m_b_spatial, m_w_out, m_g_ffn, m_w_gate, m_w_up, m_w_down, m_g_final, v_g_mix, v_w_in, v_A_re, v_A_im, v_log_dt, v_B_re, v_B_im, v_C_re, v_C_im, v_D_skip, v_w_glu, v_b_glu, v_w_pool, v_pool_scale, v_sgu_ln_g, v_sgu_ln_b, v_w_spatial, v_b_spatial, v_w_out, v_g_ffn, v_w_gate, v_w_up, v_w_down, v_g_final):
    loc = locals()
    w = {n: loc[n] for n in WEIGHTS}
    m = {n: loc["m_" + n] for n in WEIGHTS}
    v = {n: loc["v_" + n] for n in WEIGHTS}
    sel = jnp.stack([lax.axis_index("c"), 2 * lax.axis_index("x") + lax.axis_index("y")]).astype(jnp.int32)

    chip = sel[1]

    halves = [(l, half) for l in range(DEPTH) for half in ("mix", "ffn")]
    two_level = {(0, "ffn")}
    names = {"mix": MIX_WEIGHTS, "ffn": FFN_WEIGHTS}
    started = {}
    wx = {n: _exchange_form(n, w[n]) for n in BIG}
    chain = []
    for l, half in halves:
        lands = _place_shards([wx[n] for n in names[half]], l, sel, chain, f"l{l}_{half}")
        plan = _gather_half_plan if (l, half) in two_level else _gather_plan
        started[l, half] = _copies_start(f"weights_l{l}_{half}", plan, [], lands, N_REL * len(lands))
        chain = [started[l, half]["token"]]
    w = dict(w, g_mix=_after(w["g_mix"], started[halves[-1]]["token"]))

    def get_big(l, half, after):
        lands = _copies_wait(started[l, half], after)[1]
        if (l, half) in two_level:
            lands = _forward_halves(lands, f"l{l}_{half}")
        return dict(zip(names[half], lands))

    result = {n: None for n in BIG}
    stage = {"swap": None, "slabs": None}

    def advance(after):
        if stage["slabs"] is not None:
            ex, ns, l, tag = stage["slabs"]
            part, slabs = _copies_wait(ex, after)
            bufs = _add_chips(part, slabs, [result[n] for n in ns], l, sel, tag)
            for n, f in zip(ns, _share_halves(bufs, l, tag)):
                result[n] = f
            stage["slabs"] = None
        if stage["swap"] is None:
            return None
        sw, ns, l, tag = stage["swap"]
        part = _add_halves(*_copies_wait(sw, after), sel, tag)
        slabs = [lax.empty((N_REL,) + p.shape[1:], BF16) for p in part]
        ex = _copies_start(f"grads_{tag}", _slab_plan, part, slabs, N_REL * len(part))
        stage["slabs"], stage["swap"] = (ex, ns, l, tag), None
        return ex["token"]

    def on_grads(l, half, grads):
        ns = list(grads)
        tag = f"l{l}_{half}"
        token = advance([grads[ns[0]]])
        g4s = [grads[n].reshape(N_CHIPS, 2, grads[n].shape[0] // (2 * N_CHIPS), grads[n].shape[1]) for n in ns]
        recvs = [lax.empty((N_CHIPS,) + g4.shape[2:], F32) for g4 in g4s]
        sw = _copies_start(f"swap_{tag}", _sibling_plan, g4s, recvs, len(g4s))
        stage["swap"] = (sw, ns, l, tag)
        return _join(token, sw["token"])

    small = {}

    def on_small(g, loss_local):
        me = 2 * chip + sel[0]
        blocks = [_pack(g, COARSE).astype(BF16), _pack(g, PRECISE, loss_local)]
        lands = [lax.dynamic_update_slice(lax.empty((N_DEV,) + b.shape, b.dtype), b[None], (me, 0, 0)) for b in blocks]
        small.update(_copies_start("small_grads", _everyone_plan, blocks, lands, (N_DEV - 1) * len(blocks)))
        return small["token"]

    dx = _local_step(x[0], loss_target[0], w, dict(get_big=get_big, on_grads=on_grads, tick=advance, on_small=on_small))
    advance([])
    advance([])
    grads, deltas, new_m, new_v = {}, {}, {}, {}
    for n in BIG:
        outs = _adamw(wx[n], result[n], _exchange_form(n, m[n]), _exchange_form(n, v[n]), n)
        grads[n], deltas[n], new_m[n], new_v[n] = [_exchange_form(n, a) for a in (result[n], *outs)]

    _, gathered = _copies_wait(small, [new_v[n] for n in BIG])
    zero = jnp.zeros((), F32)
    loss = None
    for names_k, extra, block, tag in ((COARSE, None, gathered[0], "coarse"), (PRECISE, zero, gathered[1], "precise")):
        outs = _small_reduce_adamw(block, _pack(w, names_k, extra), _pack(m, names_k, extra), _pack(v, names_k, extra),
                                   tag)
        for store, buf in zip((grads, deltas, new_m, new_v), outs):
            vals, rest = _unpack(buf, w, names_k)
            store.update(vals)
            if store is grads and extra is not None:
                loss = rest[0, 0]
    return (loss, dx[None], *[grads[n] for n in WEIGHTS], *[deltas[n] for n in WEIGHTS],
            *[new_m[n] for n in WEIGHTS], *[new_v[n] for n in WEIGHTS])
```

```python
import math

import jax
import jax.numpy as jnp
from jax import lax
from jax.experimental import pallas as pl
from jax.experimental.pallas import tpu as pltpu

F32 = jnp.float32
BF16 = jnp.bfloat16

D_MODEL = 1024
DEPTH = 2
D_SSM = 384
SSM_GROUP = 16
N_GROUPS = 24
SSM_STATE = 64
N_STATE = N_GROUPS * SSM_STATE
POOL_WINDOWS = (2, 4, 8, 16)
POOL_GROUP = 64
D_POOL = 256
MAX_WINDOW = 16
SGU_HEADS = 6
SGU_HEAD_DIM = 64
D_SGU = 384
CHUNK = 128
D_IN = D_SSM + D_POOL + 2 * D_SGU
D_FF = 2816
EPS = 1e-6

ADAM_LR = 0.001
ADAM_B1 = 0.9
ADAM_B2 = 0.999
ADAM_EPS = 1e-08
ADAM_WD = 0.01
ADAM_STEP = 10

LANES = 128
SUBLANES = 8
VMEM_LIMIT = 56 * 1024 * 1024

TS = 512
TS_FFN = 256

WEIGHTS = ['g_mix', 'w_in', 'A_re', 'A_im', 'log_dt', 'B_re', 'B_im', 'C_re', 'C_im', 'D_skip', 'w_glu', 'b_glu',
           'w_pool', 'pool_scale', 'sgu_ln_g', 'sgu_ln_b', 'w_spatial', 'b_spatial', 'w_out', 'g_ffn', 'w_gate',
           'w_up', 'w_down', 'g_final']
BIG = ['w_in', 'w_glu', 'w_out', 'w_gate', 'w_up', 'w_down']
SMALL = [n for n in WEIGHTS if n not in BIG]
TRANSPOSED = ("w_in", "w_gate", "w_up")
N_CHIPS = 4
N_DEV = 8


def _cp(**kw):
    return pltpu.CompilerParams(vmem_limit_bytes=VMEM_LIMIT, **kw)


def _row(ts, n):
    return pl.BlockSpec((ts, n), lambda i: (i, 0))


def _const(shape):
    nd = len(shape)
    return pl.BlockSpec(shape, lambda i: (0,) * nd, pipeline_mode=pl.Buffered(1))


def _acc(shape):
    nd = len(shape)
    return pl.BlockSpec(shape, lambda i: (0,) * nd)


def _dot(a, b):
    return jnp.dot(a, b, preferred_element_type=F32)


def _dot_tn(a, b):
    return lax.dot_general(a, b, (((0,), (0,)), ((), ())), preferred_element_type=F32)


def _dot_nt(a, b):
    return lax.dot_general(a, b, (((1,), (1,)), ((), ())), preferred_element_type=F32)


_G0 = math.sqrt(2.0 / math.pi)
_G1 = 0.044715


def _gelu(x):
    return 0.5 * x * (1.0 + jnp.tanh(_G0 * (x + _G1 * x * x * x)))


def _gelu_and_grad(x):
    t = jnp.tanh(_G0 * (x + _G1 * x * x * x))
    half = 0.5 * (1.0 + t)
    return x * half, half + 0.5 * x * (1.0 - t * t) * (_G0 * (1.0 + 3.0 * _G1 * x * x))


def _sigmoid(x):
    return 1.0 / (1.0 + jnp.exp(-x))


def _rms(x):
    r = lax.rsqrt(jnp.mean(x * x, axis=-1, keepdims=True) + EPS)
    return x * r, r


def _rms_bwd(dh, n, r, g):
    dn = dh * g
    return r * (dn - n * jnp.mean(dn * n, axis=-1, keepdims=True)), dh * n


def _colsum8(v):
    rows, n = v.shape
    return jnp.sum(v.reshape(rows // SUBLANES, SUBLANES, n), axis=0)


def _cmul(ar, ai, br, bi):
    return ar * br - ai * bi, ar * bi + ai * br


def _cpow(ar, ai, n):
    assert n & (n - 1) == 0
    while n > 1:
        ar, ai = _cmul(ar, ai, ar, ai)
        n //= 2
    return ar, ai


N_USLAB = D_SSM // LANES
SEG = TS // SUBLANES
SLAB_STATES = N_STATE // N_USLAB
S5_IN = (N_USLAB, LANES, 2 * SLAB_STATES)
S5_OUT = (N_USLAB, 2 * SLAB_STATES, LANES)
STATE_TILE = (SUBLANES, N_STATE)


def _scan_order():
    p = jnp.arange(TS)
    src = (p % SUBLANES) * SEG + p // SUBLANES
    return (src[:, None] == jnp.arange(TS)[None, :]).astype(BF16)


def _to_scan_order(perm, v):
    hi = v.astype(BF16)
    lo = (v - hi.astype(F32)).astype(BF16)
    return _dot(perm, hi) + _dot(perm, lo)


def _scan_rows(k):
    return pl.ds(pl.multiple_of(k * SUBLANES, SUBLANES), SUBLANES)


def _lanes(v, j):
    return v[:, LANES * j:LANES * (j + 1)]


def _states(j):
    return pl.ds(SLAB_STATES * j, SLAB_STATES)


def _state_split(re_ref, im_ref, j, v):
    re_ref[:, _states(j)] = v[:, :SLAB_STATES]
    im_ref[:, _states(j)] = v[:, SLAB_STATES:]


def _state_cat(re_ref, im_ref, j):
    return jnp.concatenate([re_ref[:, _states(j)], im_ref[:, _states(j)]], axis=1).astype(BF16)


def _s5_fwd(u, p, tag):
    s = u.shape[0]
    seg = SEG

    def body(u_ref, perm_ref, bbt_ref, ar_ref, ai_ref, ct_ref, dsk_ref, wglu_ref, bglu_ref,
             oa_ref, y_ref, hr_ref, hi_ref, sr, si, er, ei, ir, ii, cr, ci):
        @pl.when(pl.program_id(0) == 0)
        def _():
            cr[...] = jnp.zeros_like(cr)
            ci[...] = jnp.zeros_like(ci)

        perm = perm_ref[...]
        uv = _to_scan_order(perm, u_ref[...])
        ub = uv.astype(BF16)
        for j in range(N_USLAB):
            _state_split(sr, si, j, _dot(_lanes(ub, j), bbt_ref[j]))
        for j in range(N_USLAB):
            cols = _states(j)
            ar = ar_ref[:, cols]
            ai = ai_ref[:, cols]
            h_r = h_i = jnp.zeros((SUBLANES, SLAB_STATES), F32)
            for k in range(seg):
                rows = pl.ds(SUBLANES * k, SUBLANES)
                n_r, n_i = _cmul(ar, ai, h_r, h_i)
                h_r = n_r + sr[rows, cols]
                h_i = n_i + si[rows, cols]
            er[:, cols] = h_r
            ei[:, cols] = h_i
            pr, pi = _cpow(ar[0:1, :], ai[0:1, :], seg)
            c_r = cr[:, cols]
            c_i = ci[:, cols]
            for q in range(SUBLANES):
                ir[q:q + 1, cols] = c_r
                ii[q:q + 1, cols] = c_i
                n_r, n_i = _cmul(pr, pi, c_r, c_i)
                c_r = n_r + er[q:q + 1, cols]
                c_i = n_i + ei[q:q + 1, cols]
            cr[:, cols] = c_r
            ci[:, cols] = c_i
            h_r = ir[:, cols]
            h_i = ii[:, cols]
            for k in range(seg):
                rows = pl.ds(SUBLANES * k, SUBLANES)
                n_r, n_i = _cmul(ar, ai, h_r, h_i)
                h_r = n_r + sr[rows, cols]
                h_i = n_i + si[rows, cols]
                sr[rows, cols] = h_r
                si[rows, cols] = h_i
        hr_ref[...] = sr[...].astype(BF16)
        hi_ref[...] = si[...].astype(BF16)
        y = jnp.concatenate([_dot(_state_cat(hr_ref, hi_ref, j), ct_ref[j]) for j in range(N_USLAB)], axis=1)
        y = y + dsk_ref[...] * uv
        y_ref[...] = y
        g = _gelu(y)
        pre = _dot(g.astype(BF16), wglu_ref[...]) + bglu_ref[...]
        oa_ref[...] = _dot_tn(perm, (g * _sigmoid(pre)).astype(BF16)).astype(BF16)

    return pl.pallas_call(
        body, grid=(s // TS,),
        in_specs=[_row(TS, D_SSM), _const((TS, TS)), _const(S5_IN), _const(STATE_TILE), _const(STATE_TILE),
                  _const(S5_OUT), _const((1, D_SSM)), _const((D_SSM, D_SSM)), _const((1, D_SSM))],
        out_specs=[_row(TS, D_SSM), _row(TS, D_SSM), _row(TS, N_STATE), _row(TS, N_STATE)],
        out_shape=[jax.ShapeDtypeStruct((s, D_SSM), BF16), jax.ShapeDtypeStruct((s, D_SSM), F32),
                   jax.ShapeDtypeStruct((s, N_STATE), BF16), jax.ShapeDtypeStruct((s, N_STATE), BF16)],
        scratch_shapes=[pltpu.VMEM((TS, N_STATE), F32), pltpu.VMEM((TS, N_STATE), F32),
                        pltpu.VMEM(STATE_TILE, F32), pltpu.VMEM(STATE_TILE, F32), pltpu.VMEM(STATE_TILE, F32),
                        pltpu.VMEM(STATE_TILE, F32), pltpu.VMEM((1, N_STATE), F32), pltpu.VMEM((1, N_STATE), F32)],
        name=f"s5_fwd_{tag}", compiler_params=_cp(dimension_semantics=("arbitrary",)),
    )(u, _scan_order(), p["bbt3"], p["a_re8"], p["a_im8"], p["ct3"], p["d_skip"], p["w_glu"], p["b_glu"])


def _pool_consts():
    w = jnp.repeat(jnp.asarray(POOL_WINDOWS, F32), POOL_GROUP)[None, :]
    return w


POOL_PAD = SUBLANES
POOL_ROWS = TS + MAX_WINDOW + POOL_PAD


def _window_sum(buf, tmp, first, wl, step):
    assert POOL_WINDOWS == (2, 4, 8, 16)
    n = TS + MAX_WINDOW
    lo = first - MAX_WINDOW if step < 0 else first
    src = buf
    for k, dst in zip((1, 2, 4), tmp):
        dst[pl.ds(lo, n), :] = src[pl.ds(lo, n), :] + src[pl.ds(lo + step * k, n), :]
        src = dst
    s2, s4, s8 = (t[pl.ds(first, TS), :] for t in tmp)
    s16 = s8 + tmp[2][pl.ds(first + step * 8, TS), :]
    return jnp.where(wl == 2, s2, jnp.where(wl == 4, s4, jnp.where(wl == 8, s8, s16)))


def _pool_count(i, rows, wl):
    t = (i * TS + 1).astype(F32) + lax.broadcasted_iota(jnp.int32, (rows, 1), 0).astype(F32)
    return jnp.minimum(t, wl)


def _sgu_mix(vl, wpair_ref, lo, hi):
    rows = vl.shape[0]
    chunks = []
    for c in range(rows // CHUNK):
        vc = vl[CHUNK * c:CHUNK * (c + 1), :]
        parts = []
        for q in range(SGU_HEADS // 2):
            vq = vc[:, LANES * q:LANES * (q + 1)]
            rhs = jnp.concatenate([vq * lo, vq * hi], axis=0).astype(BF16)
            parts.append(_dot(wpair_ref[q], rhs))
        chunks.append(jnp.concatenate(parts, axis=1))
    return jnp.concatenate(chunks, axis=0)


def _sgu_front(zuv, lng, lnb, grads=False):
    gelu = _gelu_and_grad if grads else lambda z: (_gelu(z), None)
    u, du = gelu(zuv[:, :D_SGU])
    v, dv = gelu(zuv[:, D_SGU:])
    mu = jnp.mean(v, axis=-1, keepdims=True)
    vc = v - mu
    rs = lax.rsqrt(jnp.mean(vc * vc, axis=-1, keepdims=True) + EPS)
    vn = vc * rs
    return u, vn, rs, vn * lng + lnb, du, dv


def _half_masks():
    lane = lax.broadcasted_iota(jnp.int32, (1, LANES), 1)
    lo = (lane < SGU_HEAD_DIM).astype(F32)
    return lo, 1.0 - lo


def _mix_fwd(x, p, tag):
    s = x.shape[0]

    def body(x_ref, g_ref, w_ref, wl_ref, wp_ref, sc_ref, lng_ref, lnb_ref, wsp_ref, bias_ref,
             za_ref, zuv_ref, h_ref, ob_ref, pooled_ref, oc_ref, buf, *tmp):
        i = pl.program_id(0)
        tile0 = POOL_PAD + MAX_WINDOW

        @pl.when(i == 0)
        def _():
            for ref in (buf, *tmp):
                ref[pl.ds(0, tile0), :] = jnp.zeros((tile0, D_POOL), F32)

        n, _ = _rms(x_ref[...])
        h = (n * g_ref[...]).astype(BF16)
        h_ref[...] = h
        z = _dot_nt(h, w_ref[...])
        za_ref[...] = z[:, :D_SSM]
        zb = z[:, D_SSM:D_SSM + D_POOL]
        zuv = z[:, D_SSM + D_POOL:]
        zuv_ref[...] = zuv
        buf[pl.ds(tile0, TS), :] = zb
        wl = wl_ref[...]
        pooled = (_window_sum(buf, tmp, tile0, wl, -1) / _pool_count(i, TS, wl) - zb).astype(BF16)
        buf[pl.ds(POOL_PAD, MAX_WINDOW), :] = zb[TS - MAX_WINDOW:, :]
        pooled_ref[...] = pooled
        ob_ref[...] = (_dot(pooled, wp_ref[...]) * sc_ref[...]).astype(BF16)
        lo, hi = _half_masks()
        u, _, _, vl, _, _ = _sgu_front(zuv, lng_ref[...], lnb_ref[...])
        mixed = _sgu_mix(vl, wsp_ref, lo, hi) + jnp.tile(bias_ref[...], (TS // CHUNK, 1))
        oc_ref[...] = (u * mixed).astype(BF16)

    return pl.pallas_call(
        body, grid=(s // TS,),
        in_specs=[_row(TS, D_MODEL), _const((1, D_MODEL)), _const((D_IN, D_MODEL)), _const((1, D_POOL)),
                  _const((D_POOL, D_POOL)), _const((1, D_POOL)), _const((1, D_SGU)), _const((1, D_SGU)),
                  _const((SGU_HEADS // 2, CHUNK, 2 * CHUNK)), _const((CHUNK, D_SGU))],
        out_specs=[_row(TS, D_SSM), _row(TS, 2 * D_SGU), _row(TS, D_MODEL), _row(TS, D_POOL), _row(TS, D_POOL),
                   _row(TS, D_SGU)],
        out_shape=[jax.ShapeDtypeStruct((s, D_SSM), F32), jax.ShapeDtypeStruct((s, 2 * D_SGU), F32),
                   jax.ShapeDtypeStruct((s, D_MODEL), BF16), jax.ShapeDtypeStruct((s, D_POOL), BF16),
                   jax.ShapeDtypeStruct((s, D_POOL), BF16), jax.ShapeDtypeStruct((s, D_SGU), BF16)],
        scratch_shapes=[pltpu.VMEM((POOL_ROWS, D_POOL), F32)] * 4,
        name=f"mix_fwd_{tag}", compiler_params=_cp(dimension_semantics=("arbitrary",)),
    )(x, p["g_mix"], p["w_in"], _pool_consts(), p["w_pool_bd"], p["pool_scale"], p["sgu_ln_g"], p["sgu_ln_b"],
      p["ws_pair"], p["bias_sp"])


def _blk_fwd(x0, oa, ob, oc, p, tag, head=None):
    s = x0.shape[0]
    ts = TS_FFN
    n_head = 0 if head is None else len(head)

    def body(x0_ref, oa_ref, ob_ref, oc_ref, wo_ref, g_ref, wg_ref, wu_ref, wd_ref, *refs):
        x1_ref, x2_ref, h2_ref, gt_ref, up_ref, ycat_ref = refs[n_head:n_head + 6]
        ycat = jnp.concatenate([oa_ref[...], ob_ref[...], oc_ref[...]], axis=1)
        ycat_ref[...] = ycat
        x1 = x0_ref[...] + _dot(ycat, wo_ref[...])
        x1_ref[...] = x1
        n, _ = _rms(x1)
        h2 = (n * g_ref[...]).astype(BF16)
        h2_ref[...] = h2
        gt = _dot_nt(h2, wg_ref[...])
        up = _dot_nt(h2, wu_ref[...])
        gt_ref[...] = gt.astype(BF16)
        up_ref[...] = up.astype(BF16)
        act = (gt * _sigmoid(gt) * up).astype(BF16)
        x2 = x1 + _dot(act, wd_ref[...])
        if head is None:
            x2_ref[...] = x2
            return
        t_ref, gf_ref = refs[:n_head]
        loss_ref, dgf_ref = refs[n_head + 6:]

        @pl.when(pl.program_id(0) == 0)
        def _():
            loss_ref[...] = jnp.zeros_like(loss_ref)
            dgf_ref[...] = jnp.zeros_like(dgf_ref)

        gf = gf_ref[...]
        nf, rf = _rms(x2)
        diff = nf * gf - t_ref[...]
        loss_ref[...] += jnp.sum(diff * diff) * (0.5 / D_MODEL)
        dxn, dgp = _rms_bwd(diff * (1.0 / D_MODEL), nf, rf, gf)
        dgf_ref[...] += _colsum8(dgp)
        x2_ref[...] = dxn

    in_specs = [_row(ts, D_MODEL), _row(ts, D_SSM), _row(ts, D_POOL), _row(ts, D_SGU), _const((D_MODEL, D_MODEL)),
                _const((1, D_MODEL)), _const((D_FF, D_MODEL)), _const((D_FF, D_MODEL)), _const((D_FF, D_MODEL))]
    out_specs = [_row(ts, D_MODEL), _row(ts, D_MODEL), _row(ts, D_MODEL), _row(ts, D_FF), _row(ts, D_FF),
                 _row(ts, D_MODEL)]
    out_shape = [jax.ShapeDtypeStruct((s, D_MODEL), F32), jax.ShapeDtypeStruct((s, D_MODEL), F32),
                 jax.ShapeDtypeStruct((s, D_MODEL), BF16), jax.ShapeDtypeStruct((s, D_FF), BF16),
                 jax.ShapeDtypeStruct((s, D_FF), BF16), jax.ShapeDtypeStruct((s, D_MODEL), BF16)]
    args = (x0, oa, ob, oc, p["w_out"], p["g_ffn"], p["w_gate"], p["w_up"], p["w_down"])
    if head is not None:
        in_specs += [_row(ts, D_MODEL), _const((1, D_MODEL))]
        out_specs += [_acc((SUBLANES, LANES)), _acc((SUBLANES, D_MODEL))]
        out_shape += [jax.ShapeDtypeStruct((SUBLANES, LANES), F32), jax.ShapeDtypeStruct((SUBLANES, D_MODEL), F32)]
        args += tuple(head)
    return pl.pallas_call(
        body, grid=(s // ts,), in_specs=in_specs, out_specs=out_specs, out_shape=out_shape,
        name=f"blk_fwd_{tag}", compiler_params=_cp(dimension_semantics=("arbitrary",)),
    )(*args)


def _blk_bwd(dx2, x1, gt, up, p, tag):
    s = dx2.shape[0]
    ts = TS_FFN

    def body(dx2_ref, x1_ref, gt_ref, up_ref, wd_ref, wgt_ref, wut_ref, wo_ref, g_ref,
             dx1_ref, da_ref, db_ref, dc_ref, dgt_ref, dup_ref, act_ref, dg_ref):
        @pl.when(pl.program_id(0) == 0)
        def _():
            dg_ref[...] = jnp.zeros_like(dg_ref)

        dx2v = dx2_ref[...]
        dact = _dot_nt(dx2v.astype(BF16), wd_ref[...])
        gf = gt_ref[...].astype(F32)
        uf = up_ref[...].astype(F32)
        sg = _sigmoid(gf)
        sl = gf * sg
        act_ref[...] = (sl * uf).astype(BF16)
        dgt = (dact * uf * (sg * (1.0 + gf * (1.0 - sg)))).astype(BF16)
        dup = (dact * sl).astype(BF16)
        dgt_ref[...] = dgt
        dup_ref[...] = dup
        dh2 = _dot(dgt, wgt_ref[...]) + _dot(dup, wut_ref[...])
        n, r = _rms(x1_ref[...])
        dxn, dgp = _rms_bwd(dh2, n, r, g_ref[...])
        dg_ref[...] += _colsum8(dgp)
        dx1 = dx2v + dxn
        dx1_ref[...] = dx1
        dy = _dot_nt(dx1.astype(BF16), wo_ref[...])
        da_ref[...] = dy[:, :D_SSM]
        db_ref[...] = dy[:, D_SSM:D_SSM + D_POOL]
        dc_ref[...] = dy[:, D_SSM + D_POOL:]

    return pl.pallas_call(
        body, grid=(s // ts,),
        in_specs=[_row(ts, D_MODEL), _row(ts, D_MODEL), _row(ts, D_FF), _row(ts, D_FF),
                  _const((D_FF, D_MODEL)), _const((D_FF, D_MODEL)), _const((D_FF, D_MODEL)),
                  _const((D_MODEL, D_MODEL)), _const((1, D_MODEL))],
        out_specs=[_row(ts, D_MODEL), _row(ts, D_SSM), _row(ts, D_POOL), _row(ts, D_SGU), _row(ts, D_FF),
                   _row(ts, D_FF), _row(ts, D_FF), _acc((SUBLANES, D_MODEL))],
        out_shape=[jax.ShapeDtypeStruct((s, D_MODEL), F32), jax.ShapeDtypeStruct((s, D_SSM), F32),
                   jax.ShapeDtypeStruct((s, D_POOL), F32), jax.ShapeDtypeStruct((s, D_SGU), F32),
                   jax.ShapeDtypeStruct((s, D_FF), BF16), jax.ShapeDtypeStruct((s, D_FF), BF16),
                   jax.ShapeDtypeStruct((s, D_FF), BF16), jax.ShapeDtypeStruct((SUBLANES, D_MODEL), F32)],
        name=f"blk_bwd_{tag}", compiler_params=_cp(dimension_semantics=("arbitrary",)),
    )(dx2, x1, gt, up, p["w_down"], p["w_gate"], p["w_up"], p["w_out"], p["g_ffn"])


def _s5_bwd(dout, u, y, h_re, h_im, p, tag):
    s = u.shape[0]
    nt = s // TS
    seg = SEG

    def rev(n):
        return pl.BlockSpec((TS, n), lambda i: (nt - 1 - i, 0))

    def body(do_ref, u_ref, y_ref, hr_ref, hi_ref, perm_ref, ar_ref, ai_ref, cb_ref, bb_ref, dsk_ref,
             wglu_ref, bglu_ref,
             du_ref, dct_ref, dbb_ref, dar_ref, dai_ref, dd_ref, dwglu_ref, dbglu_ref,
             gr, gi, hsr, hsi, er, ei, jr, ji, cr, ci):
        @pl.when(pl.program_id(0) == 0)
        def _():
            for ref in (cr, ci, dct_ref, dbb_ref, dar_ref, dai_ref, dd_ref, dwglu_ref, dbglu_ref):
                ref[...] = jnp.zeros_like(ref)

        perm = perm_ref[...]
        uv = _to_scan_order(perm, u_ref[...])
        yv = y_ref[...]
        dov = _to_scan_order(perm, do_ref[...])
        g, gelu_dy = _gelu_and_grad(yv)
        gb = g.astype(BF16)
        sg = _sigmoid(_dot(gb, wglu_ref[...]) + bglu_ref[...])
        dpre = dov * g * sg * (1.0 - sg)
        dpb = dpre.astype(BF16)
        dwglu_ref[...] += _dot_tn(gb, dpb)
        dbglu_ref[...] += _colsum8(dpre)
        dy = (dov * sg + _dot_nt(dpb, wglu_ref[...])) * gelu_dy
        dd_ref[...] += _colsum8(dy * uv)
        dyb = dy.astype(BF16)
        hsr[...] = hr_ref[...].astype(F32)
        hsi[...] = hi_ref[...].astype(F32)
        ub = uv.astype(BF16)
        dus = []

        def state_cotangents(j):
            dct_ref[j] += _dot_tn(_lanes(dyb, j), _state_cat(hr_ref, hi_ref, j))
            _state_split(gr, gi, j, _dot(_lanes(dyb, j), cb_ref[j]))

        def input_cotangents(j):
            gb_j = _state_cat(gr, gi, j)
            dbb_ref[j] += _dot_tn(_lanes(ub, j), gb_j)
            dus.append(_dot(gb_j, bb_ref[j]))

        def scan(j):
            cols = _states(j)
            ar = ar_ref[:, cols]
            ai = -ai_ref[:, cols]
            g_r = g_i = jnp.zeros((SUBLANES, SLAB_STATES), F32)
            for k in range(seg - 1, -1, -1):
                rows = pl.ds(SUBLANES * k, SUBLANES)
                n_r, n_i = _cmul(ar, ai, g_r, g_i)
                g_r = n_r + gr[rows, cols]
                g_i = n_i + gi[rows, cols]
            er[:, cols] = g_r
            ei[:, cols] = g_i
            pr, pi = _cpow(ar[0:1, :], ai[0:1, :], seg)
            c_r = cr[:, cols]
            c_i = ci[:, cols]
            for q in range(SUBLANES - 1, -1, -1):
                jr[q:q + 1, cols] = c_r
                ji[q:q + 1, cols] = c_i
                n_r, n_i = _cmul(pr, pi, c_r, c_i)
                c_r = n_r + er[q:q + 1, cols]
                c_i = n_i + ei[q:q + 1, cols]
            cr[:, cols] = c_r
            ci[:, cols] = c_i
            g_r = jr[:, cols]
            g_i = ji[:, cols]
            a_r = a_i = jnp.zeros((SUBLANES, SLAB_STATES), F32)
            for k in range(seg - 1, -1, -1):
                rows = pl.ds(SUBLANES * k, SUBLANES)
                h_r = hsr[rows, cols]
                h_i = hsi[rows, cols]
                a_r = a_r + g_r * h_r + g_i * h_i
                a_i = a_i + g_i * h_r - g_r * h_i
                n_r, n_i = _cmul(ar, ai, g_r, g_i)
                g_r = n_r + gr[rows, cols]
                g_i = n_i + gi[rows, cols]
                gr[rows, cols] = g_r
                gi[rows, cols] = g_i
            dar_ref[:, cols] += a_r
            dai_ref[:, cols] += a_i

        for stage in (state_cotangents, scan, input_cotangents):
            for j in range(N_USLAB):
                stage(j)
        du = dy * dsk_ref[...] + jnp.concatenate(dus, axis=1)
        du_ref[...] = _dot_tn(perm, du.astype(BF16)).astype(BF16)

    big = (TS, N_STATE)
    return pl.pallas_call(
        body, grid=(nt,),
        in_specs=[rev(D_SSM), rev(D_SSM), rev(D_SSM), rev(N_STATE), rev(N_STATE), _const((TS, TS)),
                  _const(STATE_TILE), _const(STATE_TILE), _const(S5_IN), _const(S5_OUT), _const((1, D_SSM)),
                  _const((D_SSM, D_SSM)), _const((1, D_SSM))],
        out_specs=[rev(D_SSM), _acc(S5_IN), _acc(S5_IN), _acc(STATE_TILE), _acc(STATE_TILE),
                   _acc((SUBLANES, D_SSM)), _acc((D_SSM, D_SSM)), _acc((SUBLANES, D_SSM))],
        out_shape=[jax.ShapeDtypeStruct((s, D_SSM), BF16), jax.ShapeDtypeStruct(S5_IN, F32),
                   jax.ShapeDtypeStruct(S5_IN, F32), jax.ShapeDtypeStruct(STATE_TILE, F32),
                   jax.ShapeDtypeStruct(STATE_TILE, F32), jax.ShapeDtypeStruct((SUBLANES, D_SSM), F32),
                   jax.ShapeDtypeStruct((D_SSM, D_SSM), F32), jax.ShapeDtypeStruct((SUBLANES, D_SSM), F32)],
        scratch_shapes=[pltpu.VMEM(big, F32), pltpu.VMEM(big, F32), pltpu.VMEM(big, F32), pltpu.VMEM(big, F32),
                        pltpu.VMEM(STATE_TILE, F32), pltpu.VMEM(STATE_TILE, F32), pltpu.VMEM(STATE_TILE, F32),
                        pltpu.VMEM(STATE_TILE, F32), pltpu.VMEM((1, N_STATE), F32), pltpu.VMEM((1, N_STATE), F32)],
        name=f"s5_bwd_{tag}", compiler_params=_cp(dimension_semantics=("arbitrary",)),
    )(dout, u, y, h_re, h_im, _scan_order(), p["a_re8"], p["a_im8"], p["cb3"], p["bb3"], p["d_skip"], p["w_glu"],
      p["b_glu"])


def _mix_bwd(dza, db, dc, pooled, zuv, x0, dx1, p, tag):
    s = x0.shape[0]
    nt = s // TS

    def rev(n):
        return pl.BlockSpec((TS, n), lambda i: (nt - 1 - i, 0))

    def body(da_ref, db_ref, dc_ref, po_ref, z_ref, x_ref, dx1_ref, wl_ref, wp_ref, wpt_ref, sc_ref, lng_ref, lnb_ref,
             wsp_ref, wspt_ref, bias_ref, win_ref, g_ref,
             dx0_ref, dz_ref, dwp_ref, dsc_ref, dws_ref, dbias_ref, dlng_ref, dlnb_ref, dg_ref, buf, *tmp):
        step = pl.program_id(0)
        i = nt - 1 - step

        @pl.when(step == 0)
        def _():
            for ref in (dwp_ref, dsc_ref, dws_ref, dbias_ref, dlng_ref, dlnb_ref, dg_ref):
                ref[...] = jnp.zeros_like(ref)
            for ref in (buf, *tmp):
                ref[pl.ds(TS, POOL_ROWS - TS), :] = jnp.zeros((POOL_ROWS - TS, D_POOL), F32)

        wl = wl_ref[...]
        sc = sc_ref[...]
        dob = db_ref[...]
        pooled_b = po_ref[...]
        dsc_ref[...] += _colsum8(dob * _dot(pooled_b, wp_ref[...]))
        dmixb = (dob * sc).astype(BF16)
        dwp_ref[...] += _dot_tn(pooled_b, dmixb)
        dpool = _dot(dmixb, wpt_ref[...])
        dq = dpool / _pool_count(i, TS, wl)
        buf[pl.ds(0, TS), :] = dq
        dzb = _window_sum(buf, tmp, 0, wl, 1) - dpool
        buf[pl.ds(TS, MAX_WINDOW), :] = dq[:MAX_WINDOW, :]

        lo, hi = _half_masks()
        lng = lng_ref[...]
        u, vn, rs, vl, gelu_du, gelu_dv = _sgu_front(z_ref[...], lng, lnb_ref[...], grads=True)
        mixed = _sgu_mix(vl, wsp_ref, lo, hi) + jnp.tile(bias_ref[...], (TS // CHUNK, 1))
        doc = dc_ref[...]
        dzu = doc * mixed * gelu_du
        dmix = doc * u
        dbias = dbias_ref[...]
        for c in range(TS // CHUNK):
            dmc = dmix[CHUNK * c:CHUNK * (c + 1), :]
            dbias = dbias + dmc
            vlc = vl[CHUNK * c:CHUNK * (c + 1), :].astype(BF16)
            for q in range(SGU_HEADS // 2):
                dmq = _lanes(dmc, q)
                vq = _lanes(vlc, q)
                dws_ref[2 * q] += _dot_nt((dmq * lo).astype(BF16), vq)
                dws_ref[2 * q + 1] += _dot_nt((dmq * hi).astype(BF16), vq)
        dbias_ref[...] = dbias
        dvl = _sgu_mix(dmix, wspt_ref, lo, hi)
        dlng_ref[...] += _colsum8(dvl * vn)
        dlnb_ref[...] += _colsum8(dvl)
        dvn = dvl * lng
        dv = rs * (dvn - jnp.mean(dvn, axis=-1, keepdims=True) - vn * jnp.mean(dvn * vn, axis=-1, keepdims=True))

        dz = jnp.concatenate([da_ref[...], dzb.astype(BF16), dzu.astype(BF16), (dv * gelu_dv).astype(BF16)], axis=1)
        dz_ref[...] = dz
        n, r = _rms(x_ref[...])
        dxn, dgp = _rms_bwd(_dot(dz, win_ref[...]), n, r, g_ref[...])
        dg_ref[...] += _colsum8(dgp)
        dx0_ref[...] = dx1_ref[...] + dxn

    pair = (SGU_HEADS // 2, CHUNK, 2 * CHUNK)
    return pl.pallas_call(
        body, grid=(nt,),
        in_specs=[rev(D_SSM), rev(D_POOL), rev(D_SGU), rev(D_POOL), rev(2 * D_SGU), rev(D_MODEL), rev(D_MODEL),
                  _const((1, D_POOL)), _const((D_POOL, D_POOL)), _const((D_POOL, D_POOL)), _const((1, D_POOL)),
                  _const((1, D_SGU)), _const((1, D_SGU)), _const(pair), _const(pair), _const((CHUNK, D_SGU)),
                  _const((D_IN, D_MODEL)), _const((1, D_MODEL))],
        out_specs=[rev(D_MODEL), rev(D_IN), _acc((D_POOL, D_POOL)), _acc((SUBLANES, D_POOL)),
                   _acc((SGU_HEADS, CHUNK, CHUNK)), _acc((CHUNK, D_SGU)), _acc((SUBLANES, D_SGU)),
                   _acc((SUBLANES, D_SGU)), _acc((SUBLANES, D_MODEL))],
        out_shape=[jax.ShapeDtypeStruct((s, D_MODEL), F32), jax.ShapeDtypeStruct((s, D_IN), BF16),
                   jax.ShapeDtypeStruct((D_POOL, D_POOL), F32), jax.ShapeDtypeStruct((SUBLANES, D_POOL), F32),
                   jax.ShapeDtypeStruct((SGU_HEADS, CHUNK, CHUNK), F32), jax.ShapeDtypeStruct((CHUNK, D_SGU), F32),
                   jax.ShapeDtypeStruct((SUBLANES, D_SGU), F32), jax.ShapeDtypeStruct((SUBLANES, D_SGU), F32),
                   jax.ShapeDtypeStruct((SUBLANES, D_MODEL), F32)],
        scratch_shapes=[pltpu.VMEM((POOL_ROWS, D_POOL), F32)] * 4,
        name=f"mix_bwd_{tag}", compiler_params=_cp(dimension_semantics=("arbitrary",)),
    )(dza, db, dc, pooled, zuv, x0, dx1, _pool_consts(), p["w_pool_bd"], p["w_pool_bd_t"], p["pool_scale"],
      p["sgu_ln_g"], p["sgu_ln_b"], p["ws_pair"], p["ws_pair_t"], p["bias_sp"], p["w_in"], p["g_mix"])


def _atb(a, b, tag, token=None):
    s, ka = a.shape
    kb = b.shape[1]
    ts = ATB_ROWS
    tn = min(kb, ATB_COLS)
    ns = s // ts
    after = [] if token is None else [token]

    def body(a_ref, b_ref, *rest):
        o_ref = rest[-1]

        @pl.when(pl.program_id(1) == 0)
        def _():
            o_ref[...] = jnp.zeros_like(o_ref)

        o_ref[...] += _dot_tn(a_ref[...].astype(BF16), b_ref[...].astype(BF16))

    return pl.pallas_call(
        body, grid=(kb // tn, ns),
        in_specs=[pl.BlockSpec((ts, ka), lambda j, i: (i, 0)), pl.BlockSpec((ts, tn), lambda j, i: (i, j))]
        + [pl.BlockSpec(memory_space=pl.ANY)] * len(after),
        out_specs=pl.BlockSpec((ka, tn), lambda j, i: (0, j)),
        out_shape=jax.ShapeDtypeStruct((ka, kb), F32),
        name=f"atb_{tag}", compiler_params=_cp(dimension_semantics=("arbitrary", "arbitrary")),
    )(a, b, *after)


def _s5_discretise(a_re, a_im, log_dt, b_re, b_im):
    dt = jnp.exp(log_dt)[:, None]
    mag = jnp.exp(a_re * dt)
    ar = mag * jnp.cos(a_im * dt)
    ai = mag * jnp.sin(a_im * dt)
    den = a_re * a_re + a_im * a_im
    f_re = ((ar - 1.0) * a_re + ai * a_im) / den
    f_im = (ai * a_re - (ar - 1.0) * a_im) / den
    bb_re = f_re[..., None] * b_re - f_im[..., None] * b_im
    bb_im = f_re[..., None] * b_im + f_im[..., None] * b_re
    return ar, ai, bb_re, bb_im


def _block_diag(blocks):
    g, r, c = blocks.shape
    eye = jnp.eye(g, dtype=blocks.dtype)
    return (blocks[:, :, None, :] * eye[:, None, :, None]).reshape(g * r, g * c)


def _block_diag_extract(m, g):
    r = m.shape[0] // g
    c = m.shape[1] // g
    eye = jnp.eye(g, dtype=m.dtype)
    return jnp.sum(m.reshape(g, r, g, c) * eye[:, None, :, None], axis=2)


GROUPS_PER_SLAB = N_GROUPS // N_USLAB


def _slab_diag(blocks):
    k = GROUPS_PER_SLAB
    _, r, c = blocks.shape
    eye = jnp.eye(k, dtype=blocks.dtype)
    spread = blocks.reshape(N_USLAB, k, r, 1, c) * eye[None, :, None, :, None]
    return spread.reshape(N_USLAB, k * r, k * c)


def _slab_diag_extract(m):
    k = GROUPS_PER_SLAB
    r, c = m.shape[1] // k, m.shape[2] // k
    eye = jnp.eye(k, dtype=m.dtype)
    return jnp.sum(m.reshape(N_USLAB, k, r, k, c) * eye[None, :, None, :, None], axis=3).reshape(N_GROUPS, r, c)


def _state_slabs(v):
    return jnp.broadcast_to(v.reshape(1, N_STATE), STATE_TILE)


def _tril():
    return jnp.tril(jnp.ones((CHUNK, CHUNK), dtype=bool))


def _layer_params(w, l):
    row = lambda v: v.reshape(1, -1)
    t = lambda m: jnp.swapaxes(m, -1, -2)
    ar, ai, bb_re, bb_im = _s5_discretise(w["A_re"][l], w["A_im"][l], w["log_dt"][l], w["B_re"][l], w["B_im"][l])
    bbt3 = jnp.concatenate([_slab_diag(t(bb_re)), _slab_diag(t(bb_im))], axis=2).astype(BF16)
    ct3 = jnp.concatenate([_slab_diag(t(w["C_re"][l])), -_slab_diag(t(w["C_im"][l]))], axis=1).astype(BF16)
    ws = jnp.where(_tril()[None], w["w_spatial"][l], 0.0)
    pair = lambda m: jnp.stack([jnp.concatenate([m[2 * q], m[2 * q + 1]], axis=1)
                                for q in range(SGU_HEADS // 2)]).astype(BF16)
    wp = _block_diag(w["w_pool"][l]).astype(BF16)
    p = dict(
        g_mix=row(w["g_mix"][l]), g_ffn=row(w["g_ffn"][l]), d_skip=row(w["D_skip"][l]), b_glu=row(w["b_glu"][l]),
        pool_scale=row(w["pool_scale"][l]), sgu_ln_g=row(w["sgu_ln_g"][l]), sgu_ln_b=row(w["sgu_ln_b"][l]),
        a_re8=_state_slabs(ar), a_im8=_state_slabs(ai),
        bbt3=bbt3, bb3=t(bbt3), ct3=ct3, cb3=t(ct3),
        w_pool_bd=wp, w_pool_bd_t=t(wp), ws_pair=pair(ws), ws_pair_t=pair(t(ws)),
        bias_sp=jnp.repeat(t(w["b_spatial"][l]), SGU_HEAD_DIM, axis=1),
    )
    return p


MIX_WEIGHTS = ("w_in", "w_glu")
FFN_WEIGHTS = ("w_out", "w_gate", "w_up", "w_down")


def _with_big(p, mats):
    p.update(mats)


def _rows_sum(v):
    return jnp.sum(v, axis=0)


ATB_COLS = 1024
ATB_ROWS = 1024


def _after(v, token):
    return v if token is None else v + token[0, 0]


def _join(a, b):
    return b if a is None else a if b is None else a + b


def _layer_bwd(dx2, sv, p, w, l, tag, hooks, token):
    t = lambda m: jnp.swapaxes(m, -1, -2)
    dx1, da, db, dc, dgt, dup, act, dg_ffn = _blk_bwd(dx2, sv["x1"], sv["gt"], sv["up"],
                                                      dict(p, g_ffn=_after(p["g_ffn"], token)), tag)
    token = hooks["tick"]([dx1])
    token = _join(token, hooks["on_grads"](l, "ffn", {
        "w_down": _atb(act, dx2, tag + "_wd", token), "w_gate": _atb(dgt, sv["h2"], tag + "_wg", token),
        "w_up": _atb(dup, sv["h2"], tag + "_wu", token), "w_out": _atb(sv["ycat"], dx1, tag + "_wo", token)}))
    g = {}
    g["g_ffn"] = _rows_sum(dg_ffn)
    dza, dc3, dbbt3, dar8, dai8, dd8, dwglu, dbglu8 = _s5_bwd(
        da, sv["za"], sv["y"], sv["h_re"], sv["h_im"], dict(p, d_skip=_after(p["d_skip"], token)), tag)
    token = hooks["tick"]([dza])
    dx0, dz, dwp, dsc8, dws, dbias, dlng8, dlnb8, dg_mix = _mix_bwd(
        dza, db, dc, sv["pooled"], sv["zuv"], sv["x0"], dx1, dict(p, pool_scale=_after(p["pool_scale"], token)), tag)
    g["g_mix"] = _rows_sum(dg_mix)
    g["b_glu"] = _rows_sum(dbglu8)
    g["D_skip"] = _rows_sum(dd8)
    half = N_STATE // N_USLAB
    g["C_re"] = _slab_diag_extract(dc3[:, :, :half])
    g["C_im"] = -_slab_diag_extract(dc3[:, :, half:])
    dar = jnp.sum(dar8, axis=0).reshape(N_GROUPS, SSM_STATE)
    dai = jnp.sum(dai8, axis=0).reshape(N_GROUPS, SSM_STATE)
    dbb_re = t(_slab_diag_extract(dbbt3[:, :, :half]))
    dbb_im = t(_slab_diag_extract(dbbt3[:, :, half:]))
    _, disc_vjp = jax.vjp(_s5_discretise, w["A_re"][l], w["A_im"][l], w["log_dt"][l], w["B_re"][l], w["B_im"][l])
    g["A_re"], g["A_im"], g["log_dt"], g["B_re"], g["B_im"] = disc_vjp((dar, dai, dbb_re, dbb_im))
    g["w_pool"] = _block_diag_extract(dwp, len(POOL_WINDOWS))
    g["pool_scale"] = _rows_sum(dsc8)
    g["sgu_ln_g"] = _rows_sum(dlng8)
    g["sgu_ln_b"] = _rows_sum(dlnb8)
    g["w_spatial"] = jnp.where(_tril()[None], dws, 0.0)
    g["b_spatial"] = t(jnp.sum(dbias.reshape(CHUNK, SGU_HEADS, SGU_HEAD_DIM), axis=-1))
    token = hooks["on_small"](l, g)
    token = hooks["on_grads"](l, "mix", {"w_in": _atb(dz, sv["h1"], tag + "_wi", token), "w_glu": dwglu})
    return dx0, token


def _local_step(x, target, w, hooks):
    params = [_layer_params(w, l) for l in range(DEPTH)]
    saved = []
    h = x
    for l in range(DEPTH):
        p, tag = params[l], f"l{l}"
        _with_big(p, hooks["get_big"](l, "mix", [h]))
        za, zuv, h1, ob, pooled, oc = _mix_fwd(h, p, tag)
        oa, y, h_re, h_im = _s5_fwd(za, p, tag)
        _with_big(p, hooks["get_big"](l, "ffn", [oa, ob, oc]))
        head = (target, w["g_final"].reshape(1, -1)) if l == DEPTH - 1 else None
        x1, x2, h2, gt, up, ycat, *loss_parts = _blk_fwd(h, oa, ob, oc, p, tag, head)
        saved.append(dict(x0=h, za=za, zuv=zuv, h1=h1, ycat=ycat, y=y, h_re=h_re, h_im=h_im, pooled=pooled, x1=x1,
                          h2=h2, gt=gt, up=up))
        h = x2
    dx = h
    loss8, dgf8 = loss_parts
    grads = [None] * DEPTH

    def on_small(l, g_l):
        grads[l] = g_l
        if l > 0:
            return None
        g = {n: jnp.stack([grads[k][n] for k in range(DEPTH)]) for n in SMALL if n != "g_final"}
        g["g_final"] = _rows_sum(dgf8)
        return hooks["on_small"](g, loss8[0, 0])

    token = None
    for l in reversed(range(DEPTH)):
        dx, token = _layer_bwd(dx, saved[l], params[l], w, l, f"l{l}", dict(hooks, on_small=on_small), token)
    return dx


_ANY = pl.BlockSpec(memory_space=pl.ANY)
_MESH = pl.DeviceIdType.MESH


def _place():
    return lax.axis_index("x"), lax.axis_index("y"), lax.axis_index("c")


def _other_chips(x, y):
    return [(1 - x, y), (x, 1 - y), (1 - x, 1 - y)]


def _dma_sems(n):
    return pltpu.SemaphoreType.DMA((n,))


def _remote(src, dst, send_sems, recv_sems, k, to):
    return pltpu.make_async_remote_copy(src_ref=src, dst_ref=dst, send_sem=send_sems.at[k], recv_sem=recv_sems.at[k],
                                        device_id=to, device_id_type=_MESH)


_HBM = pl.BlockSpec(memory_space=pltpu.HBM)
_SEM = pl.BlockSpec(memory_space=pltpu.SEMAPHORE)
_EFFECT = pltpu.SideEffectType.DATAFLOW_SIDE_EFFECTING
N_REL = N_CHIPS - 1


def _gather_plan(x, y, c, srcs, lands):
    plan = []
    for l in lands:
        r = l.shape[0] // N_CHIPS
        rows = l.at[pl.ds((2 * x + y) * r, r)]
        plan += [(rows, rows, (cx, cy, c)) for cx, cy in _other_chips(x, y)]
    return plan


def _half_rows(land, chip, c):
    h = land.shape[0] // (2 * N_CHIPS)
    return land.at[pl.ds((2 * chip + c) * h, h)]


def _gather_half_plan(x, y, c, srcs, lands):
    return [(_half_rows(l, 2 * x + y, c), _half_rows(l, 2 * x + y, c), (cx, cy, c))
            for l in lands for cx, cy in _other_chips(x, y)]


def _forward_halves(lands, tag):
    nw = len(lands)

    def body(*refs):
        ins = refs[:nw]
        send_sems, recv_sems = refs[2 * nw:]
        x, y, c = _place()
        chips = [2 * cx + cy for cx, cy in _other_chips(x, y)]
        sends = [_remote(_half_rows(ins[i], k, c), _half_rows(ins[i], k, c), send_sems, recv_sems, N_REL * i + j,
                         (x, y, 1 - c)) for i in range(nw) for j, k in enumerate(chips)]
        for cp in sends:
            cp.start()
        for i in range(nw):
            for j, k in enumerate(chips):
                sends[N_REL * i + j].wait_send()
                _remote(_half_rows(ins[i], k, c), _half_rows(ins[i], k, 1 - c), send_sems, recv_sems, N_REL * i + j,
                        (x, y, 1 - c)).wait_recv()

    return pl.pallas_call(
        body, out_shape=[jax.ShapeDtypeStruct(a.shape, a.dtype) for a in lands], in_specs=[_ANY] * nw,
        out_specs=[_ANY] * nw, input_output_aliases={i: i for i in range(nw)},
        scratch_shapes=[_dma_sems(N_REL * nw), _dma_sems(N_REL * nw)], name=f"forward_halves_{tag}",
    )(*lands)


def _sibling_plan(x, y, c, srcs, lands):
    return [(s.at[:, 1 - c], l, (x, y, 1 - c)) for s, l in zip(srcs, lands)]


def _slab_plan(x, y, c, srcs, lands):
    return [(s.at[2 * cx + cy], l.at[j], (cx, cy, c))
            for s, l in zip(srcs, lands) for j, (cx, cy) in enumerate(_other_chips(x, y))]


def _plan_copies(plan, srcs, lands, send_sems, recv_sems):
    x, y, c = _place()
    return [_remote(s, d, send_sems, recv_sems, k, to) for k, (s, d, to) in enumerate(plan(x, y, c, srcs, lands))]


def _hbm(a):
    return pltpu.with_memory_space_constraint(a, pltpu.HBM)


def _everyone_plan(x, y, c, srcs, lands):
    me = 4 * x + 2 * y + c
    peers = [(x, y, 1 - c)] + [(cx, cy, cc) for cx, cy in _other_chips(x, y) for cc in (c, 1 - c)]
    return [(s, l.at[me], peer) for s, l in zip(srcs, lands) for peer in peers]


def _copies_start(name, plan, srcs, lands, ncopies):
    ns, n = len(srcs), len(srcs) + len(lands)

    def body(*refs):
        for cp in _plan_copies(plan, refs[:ns], refs[ns:n], refs[n], refs[n + 1]):
            cp.start()
        refs[-1][...] = jnp.zeros_like(refs[-1])

    ref_out = [pltpu.HBM(a.shape, a.dtype) for a in (*srcs, *lands)]
    out = pl.pallas_call(
        body, name=name, in_specs=[_HBM] * n,
        out_shape=(_dma_sems(ncopies), _dma_sems(ncopies), *ref_out, jax.ShapeDtypeStruct((SUBLANES, LANES), F32)),
        out_specs=(_SEM, _SEM, *[_HBM] * n, pl.BlockSpec(memory_space=pltpu.VMEM)),
        input_output_aliases={i: 2 + i for i in range(n)},
        compiler_params=pltpu.CompilerParams(has_side_effects=_EFFECT),
    )(*[_hbm(a) for a in (*srcs, *lands)])
    return dict(name=name, plan=plan, sems=out[:2], srcs=out[2:2 + ns], lands=out[2 + ns:2 + n], token=out[-1])


def _copies_wait(started, after):
    ns = len(started["srcs"])
    n = ns + len(started["lands"])
    plan = started["plan"]

    def body(*refs):
        for cp in _plan_copies(plan, refs[:ns], refs[ns:n], refs[n], refs[n + 1]):
            cp.wait_send()
            cp.wait_recv()

    args = (*started["srcs"], *started["lands"])
    out = pl.pallas_call(
        body, name=started["name"] + "_wait", out_shape=[pltpu.HBM(a.shape, a.dtype) for a in args],
        in_specs=[_HBM] * n + [_SEM, _SEM] + [_ANY] * len(after), out_specs=[_HBM] * n,
        input_output_aliases={i: i for i in range(n)},
        compiler_params=pltpu.CompilerParams(has_side_effects=_EFFECT),
    )(*args, *started["sems"], *after)
    return out[:ns], out[ns:]


def _place_shards(ws, layer, sel, after, tag):
    nw = len(ws)

    def body(sel_ref, *refs):
        for i in range(nw):
            refs[nw + len(after) + i][...] = refs[i][...].astype(BF16)

    return pl.pallas_call(
        body, grid_spec=pltpu.PrefetchScalarGridSpec(
            num_scalar_prefetch=1, grid=(1,),
            in_specs=[pl.BlockSpec((None,) + a.shape[1:], lambda i, s: (layer, 0, 0)) for a in ws] + [_ANY] * len(after),
            out_specs=[pl.BlockSpec(a.shape[1:], lambda i, s: (s[1], 0)) for a in ws]),
        out_shape=[jax.ShapeDtypeStruct((N_CHIPS * a.shape[1], a.shape[2]), BF16) for a in ws],
        name=f"place_shards_{tag}", compiler_params=_cp(dimension_semantics=("arbitrary",)),
    )(sel, *ws, *after)


def _share_halves(fs, layer, tag):
    nw = len(fs)

    def body(*refs):
        ins = refs[:nw]
        send_sems, recv_sems = refs[2 * nw:]
        x, y, c = _place()

        def half(i, who):
            h = ins[i].shape[1] // 2
            return ins[i].at[layer, pl.ds(who * h, h)]

        sends = [_remote(half(i, c), half(i, c), send_sems, recv_sems, i, (x, y, 1 - c)) for i in range(nw)]
        for cp in sends:
            cp.start()
        for i in range(nw):
            sends[i].wait_send()
            _remote(half(i, c), half(i, 1 - c), send_sems, recv_sems, i, (x, y, 1 - c)).wait_recv()

    return pl.pallas_call(
        body, out_shape=[jax.ShapeDtypeStruct(f.shape, f.dtype) for f in fs], in_specs=[_ANY] * nw,
        out_specs=[_ANY] * nw, input_output_aliases={i: i for i in range(nw)},
        scratch_shapes=[_dma_sems(nw), _dma_sems(nw)], name=f"share_halves_{tag}",
    )(*fs)


def _add_halves(g4s, recvs, sel, tag):
    nw = len(g4s)

    def body(sel_ref, *refs):
        for i in range(nw):
            refs[2 * nw + i][...] = (refs[i][...] + refs[nw + i][...]).astype(BF16)

    mine = [pl.BlockSpec((None, None) + g.shape[2:], lambda k, s: (k, s[0], 0, 0)) for g in g4s]
    slab = [pl.BlockSpec((None,) + g.shape[2:], lambda k, s: (k, 0, 0)) for g in g4s]
    return pl.pallas_call(
        body, grid_spec=pltpu.PrefetchScalarGridSpec(num_scalar_prefetch=1, grid=(N_CHIPS,), in_specs=mine + slab,
                                                     out_specs=slab),
        out_shape=[jax.ShapeDtypeStruct(r.shape, BF16) for r in recvs], name=f"add_halves_{tag}",
        compiler_params=_cp(dimension_semantics=("arbitrary",)),
    )(sel, *g4s, *recvs)


def _add_chips(ps, slabs, fs, layer, sel, tag):
    nw = len(ps)
    old = [f for f in fs if f is not None]

    def body(sel_ref, *refs):
        outs = refs[2 * nw + len(old):]
        for i in range(nw):
            acc = refs[i][...].astype(F32)
            for j in range(N_REL):
                acc = acc + refs[nw + i][j].astype(F32)
            outs[i][...] = acc

    shapes = [(DEPTH, 2 * p.shape[1], p.shape[2]) for p in ps]
    in_specs = [pl.BlockSpec((None,) + p.shape[1:], lambda i, s: (s[1], 0, 0)) for p in ps]
    in_specs += [pl.BlockSpec(sl.shape, lambda i, s: (0, 0, 0)) for sl in slabs]
    in_specs += [_ANY] * len(old)
    first_old = 1 + 2 * nw
    aliases, k = {}, 0
    for i, f in enumerate(fs):
        if f is not None:
            aliases[first_old + k] = i
            k += 1
    return pl.pallas_call(
        body, grid_spec=pltpu.PrefetchScalarGridSpec(
            num_scalar_prefetch=1, grid=(1,), in_specs=in_specs,
            out_specs=[pl.BlockSpec((None,) + p.shape[1:], lambda i, s: (layer, s[0], 0)) for p in ps]),
        out_shape=[jax.ShapeDtypeStruct(sh, F32) for sh in shapes], input_output_aliases=aliases,
        name=f"add_chips_{tag}", compiler_params=_cp(dimension_semantics=("arbitrary",)),
    )(sel, *ps, *slabs, *old)


def _adamw_math(w, g, m, v):
    m = ADAM_B1 * m + (1.0 - ADAM_B1) * g
    v = ADAM_B2 * v + (1.0 - ADAM_B2) * (g * g)
    m_hat = m / (1.0 - ADAM_B1 ** ADAM_STEP)
    v_hat = v / (1.0 - ADAM_B2 ** ADAM_STEP)
    delta = -ADAM_LR * (m_hat / (jnp.sqrt(v_hat) + ADAM_EPS) + ADAM_WD * w)
    return delta, m, v


ADAM_ROWS = 512


def _row_tile(rows, most):
    return max(t for t in range(SUBLANES, most + 1, SUBLANES) if rows % t == 0)


def _adamw(w, g, m, v, tag, after=()):
    depth, rows, cols = w.shape
    tr = _row_tile(rows, ADAM_ROWS)

    def body(w_ref, g_ref, m_ref, v_ref, *rest):
        d_ref, nm_ref, nv_ref = rest[len(after):]
        d, nm, nv = _adamw_math(w_ref[...], g_ref[...], m_ref[...], v_ref[...])
        d_ref[...] = d
        nm_ref[...] = nm
        nv_ref[...] = nv

    spec = pl.BlockSpec((None, tr, cols), lambda l, i: (l, i, 0))
    return pl.pallas_call(
        body, grid=(depth, rows // tr), in_specs=[spec] * 4 + [_ANY] * len(after), out_specs=[spec] * 3,
        out_shape=[jax.ShapeDtypeStruct(w.shape, F32)] * 3, name=f"adamw_{tag}",
        compiler_params=_cp(dimension_semantics=("arbitrary", "arbitrary")),
    )(w, g, m, v, *after)


SMALL_TILE = 512
PRECISE = ("g_final",)
COARSE = [n for n in SMALL if n not in PRECISE]


def _small_reduce_adamw(gathered, w, m, v, tag):
    rows = w.shape[0]
    tr = math.gcd(rows, SMALL_TILE)

    def body(ga_ref, w_ref, m_ref, v_ref, g_ref, d_ref, nm_ref, nv_ref):
        g = ga_ref[0].astype(F32)
        for k in range(1, N_DEV):
            g = g + ga_ref[k].astype(F32)
        g_ref[...] = g
        d, nm, nv = _adamw_math(w_ref[...], g, m_ref[...], v_ref[...])
        d_ref[...] = d
        nm_ref[...] = nm
        nv_ref[...] = nv

    spec = _row(tr, LANES)
    return pl.pallas_call(
        body, grid=(rows // tr,),
        in_specs=[pl.BlockSpec((N_DEV, tr, LANES), lambda i: (0, i, 0)), spec, spec, spec], out_specs=[spec] * 4,
        out_shape=[jax.ShapeDtypeStruct((rows, LANES), F32)] * 4, name=f"small_reduce_adamw_{tag}",
        compiler_params=_cp(dimension_semantics=("arbitrary",)),
    )(gathered, w, m, v)


def _exchange_form(n, a):
    return jnp.swapaxes(a, 1, 2) if n in TRANSPOSED else a


PACK_ROWS = 16


def _rows_of(size):
    return -(-size // (LANES * PACK_ROWS)) * PACK_ROWS


def _pack(vals, names, extra=None):
    parts = [vals[n].reshape(-1) for n in names] + ([] if extra is None else [extra.reshape(1)])
    tiles = [jnp.pad(a, (0, _rows_of(a.size) * LANES - a.size)).reshape(-1, LANES) for a in parts]
    rows = sum(t.shape[0] for t in tiles)
    if rows > SMALL_TILE:
        tiles.append(jnp.zeros((-rows % SMALL_TILE, LANES), tiles[0].dtype))
    return jnp.concatenate(tiles, axis=0)


def _unpack(buf, like, names):
    out, row = {}, 0
    for n in names:
        rows = _rows_of(like[n].size)
        out[n] = buf[row:row + rows].reshape(-1)[:like[n].size].reshape(like[n].shape)
        row += rows
    return out, buf[row:]


def kernel(x, g_mix, w_in, A_re, A_im, log_dt, B_re, B_im, C_re, C_im, D_skip, w_glu, b_glu, w_pool, pool_scale, sgu_ln_g, sgu_ln_b, w_spatial, b_spatial, w_out, g_ffn, w_gate, w_up, w_down, g_final, loss_target, m_g_mix, m_w_in, m_A_re, m_A_im, m_log_dt, m_B_re, m_B_im, m_C_re, m_C_im, m_D_skip, m_w_glu, m_b_glu, m_w_pool, m_pool_scale, m_sgu_ln_g, m_sgu_ln_b, m_w_spatial, m_b_spatial, m_w_out, m_g_ffn, m_w_gate, m_w_up, m_w_down, m_g_final, v_g_mix, v_w_in, v_A_re, v_A_im, v_log_dt, v_B_re, v_B_im, v_C_re, v_C_im, v_D_skip, v_w_glu, v_b_glu, v_w_pool, v_pool_scale, v_sgu_ln_g, v_sgu_ln_b, v_w_spatial, v_b_spatial, v_w_out, v_g_ffn, v_w_gate, v_w_up, v_w_down, v_g_final):
    loc = locals()
    w = {n: loc[n] for n in WEIGHTS}
    m = {n: loc["m_" + n] for n in WEIGHTS}
    v = {n: loc["v_" + n] for n in WEIGHTS}
    sel = jnp.stack([lax.axis_index("c"), 2 * lax.axis_index("x") + lax.axis_index("y")]).astype(jnp.int32)

    chip = sel[1]

    halves = [(l, half) for l in range(DEPTH) for half in ("mix", "ffn")]
    two_level = {(0, "ffn")}
    names = {"mix": MIX_WEIGHTS, "ffn": FFN_WEIGHTS}
    started = {}
    wx = {n: _exchange_form(n, w[n]) for n in BIG}
    chain = []
    for l, half in halves:
        lands = _place_shards([wx[n] for n in names[half]], l, sel, chain, f"l{l}_{half}")
        plan = _gather_half_plan if (l, half) in two_level else _gather_plan
        started[l, half] = _copies_start(f"weights_l{l}_{half}", plan, [], lands, N_REL * len(lands))
        chain = [started[l, half]["token"]]
    w = dict(w, g_mix=_after(w["g_mix"], started[halves[-1]]["token"]))

    def get_big(l, half, after):
        lands = _copies_wait(started[l, half], after)[1]
        if (l, half) in two_level:
            lands = _forward_halves(lands, f"l{l}_{half}")
        return dict(zip(names[half], lands))

    result = {n: None for n in BIG}
    stage = {"swap": None, "slabs": None}

    def advance(after):
        if stage["slabs"] is not None:
            ex, ns, l, tag = stage["slabs"]
            part, slabs = _copies_wait(ex, after)
            bufs = _add_chips(part, slabs, [result[n] for n in ns], l, sel, tag)
            for n, f in zip(ns, _share_halves(bufs, l, tag)):
                result[n] = f
            stage["slabs"] = None
        if stage["swap"] is None:
            return None
        sw, ns, l, tag = stage["swap"]
        part = _add_halves(*_copies_wait(sw, after), sel, tag)
        slabs = [lax.empty((N_REL,) + p.shape[1:], BF16) for p in part]
        ex = _copies_start(f"grads_{tag}", _slab_plan, part, slabs, N_REL * len(part))
        stage["slabs"], stage["swap"] = (ex, ns, l, tag), None
        return ex["token"]

    def on_grads(l, half, grads):
        ns = list(grads)
        tag = f"l{l}_{half}"
        token = advance([grads[ns[0]]])
        g4s = [grads[n].reshape(N_CHIPS, 2, grads[n].shape[0] // (2 * N_CHIPS), grads[n].shape[1]) for n in ns]
        recvs = [lax.empty((N_CHIPS,) + g4.shape[2:], F32) for g4 in g4s]
        sw = _copies_start(f"swap_{tag}", _sibling_plan, g4s, recvs, len(g4s))
        stage["swap"] = (sw, ns, l, tag)
        return _join(token, sw["token"])

    small = {}

    def on_small(g, loss_local):
        me = 2 * chip + sel[0]
        blocks = [_pack(g, COARSE).astype(BF16), _pack(g, PRECISE, loss_local)]
        lands = [lax.dynamic_update_slice(lax.empty((N_DEV,) + b.shape, b.dtype), b[None], (me, 0, 0)) for b in blocks]
        small.update(_copies_start("small_grads", _everyone_plan, blocks, lands, (N_DEV - 1) * len(blocks)))
        return small["token"]

    dx = _local_step(x[0], loss_target[0], w, dict(get_big=get_big, on_grads=on_grads, tick=advance, on_small=on_small))
    grads, deltas, new_m, new_v = {}, {}, {}, {}

    def update_big(ns, after):
        for n in ns:
            outs = _adamw(wx[n], result[n], _exchange_form(n, m[n]), _exchange_form(n, v[n]), n, after)
            grads[n], deltas[n], new_m[n], new_v[n] = [_exchange_form(n, a) for a in (result[n], *outs)]
            after = [outs[-1]]
        return after

    last_sent = advance([])
    advance(update_big(FFN_WEIGHTS, [last_sent]))
    update_big(MIX_WEIGHTS, [])

    _, gathered = _copies_wait(small, [new_v[n] for n in BIG])
    zero = jnp.zeros((), F32)
    loss = None
    for names_k, extra, block, tag in ((COARSE, None, gathered[0], "coarse"), (PRECISE, zero, gathered[1], "precise")):
        outs = _small_reduce_adamw(block, _pack(w, names_k, extra), _pack(m, names_k, extra), _pack(v, names_k, extra),
                                   tag)
        for store, buf in zip((grads, deltas, new_m, new_v), outs):
            vals, rest = _unpack(buf, w, names_k)
            store.update(vals)
            if store is grads and extra is not None:
                loss = rest[0, 0]
    return (loss, dx[None], *[grads[n] for n in WEIGHTS], *[deltas[n] for n in WEIGHTS],
            *[new_m[n] for n in WEIGHTS], *[new_v[n] for n in WEIGHTS])
```

```python
import math

import jax
import jax.numpy as jnp
from jax import lax
from jax.experimental import pallas as pl
from jax.experimental.pallas import tpu as pltpu

F32 = jnp.float32
BF16 = jnp.bfloat16

D_MODEL = 1024
DEPTH = 2
D_SSM = 384
SSM_GROUP = 16
N_GROUPS = 24
SSM_STATE = 64
N_STATE = N_GROUPS * SSM_STATE
POOL_WINDOWS = (2, 4, 8, 16)
POOL_GROUP = 64
D_POOL = 256
MAX_WINDOW = 16
SGU_HEADS = 6
SGU_HEAD_DIM = 64
D_SGU = 384
CHUNK = 128
D_IN = D_SSM + D_POOL + 2 * D_SGU
D_FF = 2816
EPS = 1e-6

ADAM_LR = 0.001
ADAM_B1 = 0.9
ADAM_B2 = 0.999
ADAM_EPS = 1e-08
ADAM_WD = 0.01
ADAM_STEP = 10

LANES = 128
SUBLANES = 8
VMEM_LIMIT = 56 * 1024 * 1024

TS = 512
TS_FFN = 256

WEIGHTS = ['g_mix', 'w_in', 'A_re', 'A_im', 'log_dt', 'B_re', 'B_im', 'C_re', 'C_im', 'D_skip', 'w_glu', 'b_glu',
           'w_pool', 'pool_scale', 'sgu_ln_g', 'sgu_ln_b', 'w_spatial', 'b_spatial', 'w_out', 'g_ffn', 'w_gate',
           'w_up', 'w_down', 'g_final']
BIG = ['w_in', 'w_glu', 'w_out', 'w_gate', 'w_up', 'w_down']
SMALL = [n for n in WEIGHTS if n not in BIG]
TRANSPOSED = ("w_in", "w_gate", "w_up")
N_CHIPS = 4
N_DEV = 8


def _cp(**kw):
    return pltpu.CompilerParams(vmem_limit_bytes=VMEM_LIMIT, **kw)


def _row(ts, n):
    return pl.BlockSpec((ts, n), lambda i: (i, 0))


def _const(shape):
    nd = len(shape)
    return pl.BlockSpec(shape, lambda i: (0,) * nd, pipeline_mode=pl.Buffered(1))


def _acc(shape):
    nd = len(shape)
    return pl.BlockSpec(shape, lambda i: (0,) * nd)


def _dot(a, b):
    return jnp.dot(a, b, preferred_element_type=F32)


def _dot_tn(a, b):
    return lax.dot_general(a, b, (((0,), (0,)), ((), ())), preferred_element_type=F32)


def _dot_nt(a, b):
    return lax.dot_general(a, b, (((1,), (1,)), ((), ())), preferred_element_type=F32)


_G0 = math.sqrt(2.0 / math.pi)
_G1 = 0.044715


def _gelu(x):
    return 0.5 * x * (1.0 + jnp.tanh(_G0 * (x + _G1 * x * x * x)))


def _gelu_and_grad(x):
    t = jnp.tanh(_G0 * (x + _G1 * x * x * x))
    half = 0.5 * (1.0 + t)
    return x * half, half + 0.5 * x * (1.0 - t * t) * (_G0 * (1.0 + 3.0 * _G1 * x * x))


def _sigmoid(x):
    return 1.0 / (1.0 + jnp.exp(-x))


def _rms(x):
    r = lax.rsqrt(jnp.mean(x * x, axis=-1, keepdims=True) + EPS)
    return x * r, r


def _rms_bwd(dh, n, r, g):
    dn = dh * g
    return r * (dn - n * jnp.mean(dn * n, axis=-1, keepdims=True)), dh * n


def _colsum8(v):
    rows, n = v.shape
    return jnp.sum(v.reshape(rows // SUBLANES, SUBLANES, n), axis=0)


def _cmul(ar, ai, br, bi):
    return ar * br - ai * bi, ar * bi + ai * br


def _cpow(ar, ai, n):
    assert n & (n - 1) == 0
    while n > 1:
        ar, ai = _cmul(ar, ai, ar, ai)
        n //= 2
    return ar, ai


N_USLAB = D_SSM // LANES
SEG = TS // SUBLANES
SLAB_STATES = N_STATE // N_USLAB
S5_IN = (N_USLAB, LANES, 2 * SLAB_STATES)
S5_OUT = (N_USLAB, 2 * SLAB_STATES, LANES)
STATE_TILE = (SUBLANES, N_STATE)


def _scan_order():
    p = jnp.arange(TS)
    src = (p % SUBLANES) * SEG + p // SUBLANES
    return (src[:, None] == jnp.arange(TS)[None, :]).astype(BF16)


def _to_scan_order(perm, v):
    hi = v.astype(BF16)
    lo = (v - hi.astype(F32)).astype(BF16)
    return _dot(perm, hi) + _dot(perm, lo)


def _scan_rows(k):
    return pl.ds(pl.multiple_of(k * SUBLANES, SUBLANES), SUBLANES)


def _lanes(v, j):
    return v[:, LANES * j:LANES * (j + 1)]


def _states(j):
    return pl.ds(SLAB_STATES * j, SLAB_STATES)


def _state_split(re_ref, im_ref, j, v):
    re_ref[:, _states(j)] = v[:, :SLAB_STATES]
    im_ref[:, _states(j)] = v[:, SLAB_STATES:]


def _state_cat(re_ref, im_ref, j):
    return jnp.concatenate([re_ref[:, _states(j)], im_ref[:, _states(j)]], axis=1).astype(BF16)


def _s5_fwd(u, p, tag):
    s = u.shape[0]
    seg = SEG

    def body(u_ref, perm_ref, bbt_ref, ar_ref, ai_ref, ct_ref, dsk_ref, wglu_ref, bglu_ref,
             oa_ref, y_ref, hr_ref, hi_ref, sr, si, er, ei, ir, ii, cr, ci):
        @pl.when(pl.program_id(0) == 0)
        def _():
            cr[...] = jnp.zeros_like(cr)
            ci[...] = jnp.zeros_like(ci)

        perm = perm_ref[...]
        uv = _to_scan_order(perm, u_ref[...])
        ub = uv.astype(BF16)
        for j in range(N_USLAB):
            _state_split(sr, si, j, _dot(_lanes(ub, j), bbt_ref[j]))
        for j in range(N_USLAB):
            cols = _states(j)
            ar = ar_ref[:, cols]
            ai = ai_ref[:, cols]
            h_r = h_i = jnp.zeros((SUBLANES, SLAB_STATES), F32)
            for k in range(seg):
                rows = pl.ds(SUBLANES * k, SUBLANES)
                n_r, n_i = _cmul(ar, ai, h_r, h_i)
                h_r = n_r + sr[rows, cols]
                h_i = n_i + si[rows, cols]
            er[:, cols] = h_r
            ei[:, cols] = h_i
            pr, pi = _cpow(ar[0:1, :], ai[0:1, :], seg)
            c_r = cr[:, cols]
            c_i = ci[:, cols]
            for q in range(SUBLANES):
                ir[q:q + 1, cols] = c_r
                ii[q:q + 1, cols] = c_i
                n_r, n_i = _cmul(pr, pi, c_r, c_i)
                c_r = n_r + er[q:q + 1, cols]
                c_i = n_i + ei[q:q + 1, cols]
            cr[:, cols] = c_r
            ci[:, cols] = c_i
            h_r = ir[:, cols]
            h_i = ii[:, cols]
            for k in range(seg):
                rows = pl.ds(SUBLANES * k, SUBLANES)
                n_r, n_i = _cmul(ar, ai, h_r, h_i)
                h_r = n_r + sr[rows, cols]
                h_i = n_i + si[rows, cols]
                sr[rows, cols] = h_r
                si[rows, cols] = h_i
        hr_ref[...] = sr[...].astype(BF16)
        hi_ref[...] = si[...].astype(BF16)
        y = jnp.concatenate([_dot(_state_cat(hr_ref, hi_ref, j), ct_ref[j]) for j in range(N_USLAB)], axis=1)
        y = y + dsk_ref[...] * uv
        y_ref[...] = y
        g = _gelu(y)
        pre = _dot(g.astype(BF16), wglu_ref[...]) + bglu_ref[...]
        oa_ref[...] = _dot_tn(perm, (g * _sigmoid(pre)).astype(BF16)).astype(BF16)

    return pl.pallas_call(
        body, grid=(s // TS,),
        in_specs=[_row(TS, D_SSM), _const((TS, TS)), _const(S5_IN), _const(STATE_TILE), _const(STATE_TILE),
                  _const(S5_OUT), _const((1, D_SSM)), _const((D_SSM, D_SSM)), _const((1, D_SSM))],
        out_specs=[_row(TS, D_SSM), _row(TS, D_SSM), _row(TS, N_STATE), _row(TS, N_STATE)],
        out_shape=[jax.ShapeDtypeStruct((s, D_SSM), BF16), jax.ShapeDtypeStruct((s, D_SSM), F32),
                   jax.ShapeDtypeStruct((s, N_STATE), BF16), jax.ShapeDtypeStruct((s, N_STATE), BF16)],
        scratch_shapes=[pltpu.VMEM((TS, N_STATE), F32), pltpu.VMEM((TS, N_STATE), F32),
                        pltpu.VMEM(STATE_TILE, F32), pltpu.VMEM(STATE_TILE, F32), pltpu.VMEM(STATE_TILE, F32),
                        pltpu.VMEM(STATE_TILE, F32), pltpu.VMEM((1, N_STATE), F32), pltpu.VMEM((1, N_STATE), F32)],
        name=f"s5_fwd_{tag}", compiler_params=_cp(dimension_semantics=("arbitrary",)),
    )(u, _scan_order(), p["bbt3"], p["a_re8"], p["a_im8"], p["ct3"], p["d_skip"], p["w_glu"], p["b_glu"])


def _pool_consts():
    w = jnp.repeat(jnp.asarray(POOL_WINDOWS, F32), POOL_GROUP)[None, :]
    return w


POOL_PAD = SUBLANES
POOL_ROWS = TS + MAX_WINDOW + POOL_PAD


def _window_sum(buf, tmp, first, wl, step):
    assert POOL_WINDOWS == (2, 4, 8, 16)
    n = TS + MAX_WINDOW
    lo = first - MAX_WINDOW if step < 0 else first
    src = buf
    for k, dst in zip((1, 2, 4), tmp):
        dst[pl.ds(lo, n), :] = src[pl.ds(lo, n), :] + src[pl.ds(lo + step * k, n), :]
        src = dst
    s2, s4, s8 = (t[pl.ds(first, TS), :] for t in tmp)
    s16 = s8 + tmp[2][pl.ds(first + step * 8, TS), :]
    return jnp.where(wl == 2, s2, jnp.where(wl == 4, s4, jnp.where(wl == 8, s8, s16)))


def _pool_count(i, rows, wl):
    t = (i * TS + 1).astype(F32) + lax.broadcasted_iota(jnp.int32, (rows, 1), 0).astype(F32)
    return jnp.minimum(t, wl)


def _sgu_mix(vl, wpair_ref, lo, hi):
    rows = vl.shape[0]
    chunks = []
    for c in range(rows // CHUNK):
        vc = vl[CHUNK * c:CHUNK * (c + 1), :]
        parts = []
        for q in range(SGU_HEADS // 2):
            vq = vc[:, LANES * q:LANES * (q + 1)]
            rhs = jnp.concatenate([vq * lo, vq * hi], axis=0).astype(BF16)
            parts.append(_dot(wpair_ref[q], rhs))
        chunks.append(jnp.concatenate(parts, axis=1))
    return jnp.concatenate(chunks, axis=0)


def _sgu_front(zuv, lng, lnb, grads=False):
    gelu = _gelu_and_grad if grads else lambda z: (_gelu(z), None)
    u, du = gelu(zuv[:, :D_SGU])
    v, dv = gelu(zuv[:, D_SGU:])
    mu = jnp.mean(v, axis=-1, keepdims=True)
    vc = v - mu
    rs = lax.rsqrt(jnp.mean(vc * vc, axis=-1, keepdims=True) + EPS)
    vn = vc * rs
    return u, vn, rs, vn * lng + lnb, du, dv


def _half_masks():
    lane = lax.broadcasted_iota(jnp.int32, (1, LANES), 1)
    lo = (lane < SGU_HEAD_DIM).astype(F32)
    return lo, 1.0 - lo


def _mix_fwd(x, p, tag):
    s = x.shape[0]

    def body(x_ref, g_ref, w_ref, wl_ref, wp_ref, sc_ref, lng_ref, lnb_ref, wsp_ref, bias_ref,
             za_ref, zuv_ref, h_ref, ob_ref, pooled_ref, oc_ref, buf, *tmp):
        i = pl.program_id(0)
        tile0 = POOL_PAD + MAX_WINDOW

        @pl.when(i == 0)
        def _():
            for ref in (buf, *tmp):
                ref[pl.ds(0, tile0), :] = jnp.zeros((tile0, D_POOL), F32)

        n, _ = _rms(x_ref[...])
        h = (n * g_ref[...]).astype(BF16)
        h_ref[...] = h
        z = _dot_nt(h, w_ref[...])
        za_ref[...] = z[:, :D_SSM]
        zb = z[:, D_SSM:D_SSM + D_POOL]
        zuv = z[:, D_SSM + D_POOL:]
        zuv_ref[...] = zuv
        buf[pl.ds(tile0, TS), :] = zb
        wl = wl_ref[...]
        pooled = (_window_sum(buf, tmp, tile0, wl, -1) / _pool_count(i, TS, wl) - zb).astype(BF16)
        buf[pl.ds(POOL_PAD, MAX_WINDOW), :] = zb[TS - MAX_WINDOW:, :]
        pooled_ref[...] = pooled
        ob_ref[...] = (_dot(pooled, wp_ref[...]) * sc_ref[...]).astype(BF16)
        lo, hi = _half_masks()
        u, _, _, vl, _, _ = _sgu_front(zuv, lng_ref[...], lnb_ref[...])
        mixed = _sgu_mix(vl, wsp_ref, lo, hi) + jnp.tile(bias_ref[...], (TS // CHUNK, 1))
        oc_ref[...] = (u * mixed).astype(BF16)

    return pl.pallas_call(
        body, grid=(s // TS,),
        in_specs=[_row(TS, D_MODEL), _const((1, D_MODEL)), _const((D_IN, D_MODEL)), _const((1, D_POOL)),
                  _const((D_POOL, D_POOL)), _const((1, D_POOL)), _const((1, D_SGU)), _const((1, D_SGU)),
                  _const((SGU_HEADS // 2, CHUNK, 2 * CHUNK)), _const((CHUNK, D_SGU))],
        out_specs=[_row(TS, D_SSM), _row(TS, 2 * D_SGU), _row(TS, D_MODEL), _row(TS, D_POOL), _row(TS, D_POOL),
                   _row(TS, D_SGU)],
        out_shape=[jax.ShapeDtypeStruct((s, D_SSM), F32), jax.ShapeDtypeStruct((s, 2 * D_SGU), F32),
                   jax.ShapeDtypeStruct((s, D_MODEL), BF16), jax.ShapeDtypeStruct((s, D_POOL), BF16),
                   jax.ShapeDtypeStruct((s, D_POOL), BF16), jax.ShapeDtypeStruct((s, D_SGU), BF16)],
        scratch_shapes=[pltpu.VMEM((POOL_ROWS, D_POOL), F32)] * 4,
        name=f"mix_fwd_{tag}", compiler_params=_cp(dimension_semantics=("arbitrary",)),
    )(x, p["g_mix"], p["w_in"], _pool_consts(), p["w_pool_bd"], p["pool_scale"], p["sgu_ln_g"], p["sgu_ln_b"],
      p["ws_pair"], p["bias_sp"])


def _blk_fwd(x0, oa, ob, oc, p, tag, head=None):
    s = x0.shape[0]
    ts = TS_FFN
    n_head = 0 if head is None else len(head)

    def body(x0_ref, oa_ref, ob_ref, oc_ref, wo_ref, g_ref, wg_ref, wu_ref, wd_ref, *refs):
        x1_ref, x2_ref, h2_ref, gt_ref, up_ref, ycat_ref = refs[n_head:n_head + 6]
        ycat = jnp.concatenate([oa_ref[...], ob_ref[...], oc_ref[...]], axis=1)
        ycat_ref[...] = ycat
        x1 = x0_ref[...] + _dot(ycat, wo_ref[...])
        x1_ref[...] = x1
        n, _ = _rms(x1)
        h2 = (n * g_ref[...]).astype(BF16)
        h2_ref[...] = h2
        gt = _dot_nt(h2, wg_ref[...])
        up = _dot_nt(h2, wu_ref[...])
        gt_ref[...] = gt.astype(BF16)
        up_ref[...] = up.astype(BF16)
        act = (gt * _sigmoid(gt) * up).astype(BF16)
        x2 = x1 + _dot(act, wd_ref[...])
        if head is None:
            x2_ref[...] = x2
            return
        t_ref, gf_ref = refs[:n_head]
        loss_ref, dgf_ref = refs[n_head + 6:]

        @pl.when(pl.program_id(0) == 0)
        def _():
            loss_ref[...] = jnp.zeros_like(loss_ref)
            dgf_ref[...] = jnp.zeros_like(dgf_ref)

        gf = gf_ref[...]
        nf, rf = _rms(x2)
        diff = nf * gf - t_ref[...]
        loss_ref[...] += jnp.sum(diff * diff) * (0.5 / D_MODEL)
        dxn, dgp = _rms_bwd(diff * (1.0 / D_MODEL), nf, rf, gf)
        dgf_ref[...] += _colsum8(dgp)
        x2_ref[...] = dxn

    in_specs = [_row(ts, D_MODEL), _row(ts, D_SSM), _row(ts, D_POOL), _row(ts, D_SGU), _const((D_MODEL, D_MODEL)),
                _const((1, D_MODEL)), _const((D_FF, D_MODEL)), _const((D_FF, D_MODEL)), _const((D_FF, D_MODEL))]
    out_specs = [_row(ts, D_MODEL), _row(ts, D_MODEL), _row(ts, D_MODEL), _row(ts, D_FF), _row(ts, D_FF),
                 _row(ts, D_MODEL)]
    out_shape = [jax.ShapeDtypeStruct((s, D_MODEL), F32), jax.ShapeDtypeStruct((s, D_MODEL), F32),
                 jax.ShapeDtypeStruct((s, D_MODEL), BF16), jax.ShapeDtypeStruct((s, D_FF), BF16),
                 jax.ShapeDtypeStruct((s, D_FF), BF16), jax.ShapeDtypeStruct((s, D_MODEL), BF16)]
    args = (x0, oa, ob, oc, p["w_out"], p["g_ffn"], p["w_gate"], p["w_up"], p["w_down"])
    if head is not None:
        in_specs += [_row(ts, D_MODEL), _const((1, D_MODEL))]
        out_specs += [_acc((SUBLANES, LANES)), _acc((SUBLANES, D_MODEL))]
        out_shape += [jax.ShapeDtypeStruct((SUBLANES, LANES), F32), jax.ShapeDtypeStruct((SUBLANES, D_MODEL), F32)]
        args += tuple(head)
    return pl.pallas_call(
        body, grid=(s // ts,), in_specs=in_specs, out_specs=out_specs, out_shape=out_shape,
        name=f"blk_fwd_{tag}", compiler_params=_cp(dimension_semantics=("arbitrary",)),
    )(*args)


def _blk_bwd(dx2, x1, gt, up, p, tag):
    s = dx2.shape[0]
    ts = TS_FFN

    def body(dx2_ref, x1_ref, gt_ref, up_ref, wd_ref, wgt_ref, wut_ref, wo_ref, g_ref,
             dx1_ref, da_ref, db_ref, dc_ref, dgt_ref, dup_ref, act_ref, dg_ref):
        @pl.when(pl.program_id(0) == 0)
        def _():
            dg_ref[...] = jnp.zeros_like(dg_ref)

        dx2v = dx2_ref[...]
        dact = _dot_nt(dx2v.astype(BF16), wd_ref[...])
        gf = gt_ref[...].astype(F32)
        uf = up_ref[...].astype(F32)
        sg = _sigmoid(gf)
        sl = gf * sg
        act_ref[...] = (sl * uf).astype(BF16)
        dgt = (dact * uf * (sg * (1.0 + gf * (1.0 - sg)))).astype(BF16)
        dup = (dact * sl).astype(BF16)
        dgt_ref[...] = dgt
        dup_ref[...] = dup
        dh2 = _dot(dgt, wgt_ref[...]) + _dot(dup, wut_ref[...])
        n, r = _rms(x1_ref[...])
        dxn, dgp = _rms_bwd(dh2, n, r, g_ref[...])
        dg_ref[...] += _colsum8(dgp)
        dx1 = dx2v + dxn
        dx1_ref[...] = dx1
        dy = _dot_nt(dx1.astype(BF16), wo_ref[...])
        da_ref[...] = dy[:, :D_SSM]
        db_ref[...] = dy[:, D_SSM:D_SSM + D_POOL]
        dc_ref[...] = dy[:, D_SSM + D_POOL:]

    return pl.pallas_call(
        body, grid=(s // ts,),
        in_specs=[_row(ts, D_MODEL), _row(ts, D_MODEL), _row(ts, D_FF), _row(ts, D_FF),
                  _const((D_FF, D_MODEL)), _const((D_FF, D_MODEL)), _const((D_FF, D_MODEL)),
                  _const((D_MODEL, D_MODEL)), _const((1, D_MODEL))],
        out_specs=[_row(ts, D_MODEL), _row(ts, D_SSM), _row(ts, D_POOL), _row(ts, D_SGU), _row(ts, D_FF),
                   _row(ts, D_FF), _row(ts, D_FF), _acc((SUBLANES, D_MODEL))],
        out_shape=[jax.ShapeDtypeStruct((s, D_MODEL), F32), jax.ShapeDtypeStruct((s, D_SSM), F32),
                   jax.ShapeDtypeStruct((s, D_POOL), F32), jax.ShapeDtypeStruct((s, D_SGU), F32),
                   jax.ShapeDtypeStruct((s, D_FF), BF16), jax.ShapeDtypeStruct((s, D_FF), BF16),
                   jax.ShapeDtypeStruct((s, D_FF), BF16), jax.ShapeDtypeStruct((SUBLANES, D_MODEL), F32)],
        name=f"blk_bwd_{tag}", compiler_params=_cp(dimension_semantics=("arbitrary",)),
    )(dx2, x1, gt, up, p["w_down"], p["w_gate"], p["w_up"], p["w_out"], p["g_ffn"])


def _s5_bwd(dout, u, y, h_re, h_im, p, tag):
    s = u.shape[0]
    nt = s // TS
    seg = SEG

    def rev(n):
        return pl.BlockSpec((TS, n), lambda i: (nt - 1 - i, 0))

    def body(do_ref, u_ref, y_ref, hr_ref, hi_ref, perm_ref, ar_ref, ai_ref, cb_ref, bb_ref, dsk_ref,
             wglu_ref, bglu_ref,
             du_ref, dct_ref, dbb_ref, dar_ref, dai_ref, dd_ref, dwglu_ref, dbglu_ref,
             gr, gi, hsr, hsi, er, ei, jr, ji, cr, ci):
        @pl.when(pl.program_id(0) == 0)
        def _():
            for ref in (cr, ci, dct_ref, dbb_ref, dar_ref, dai_ref, dd_ref, dwglu_ref, dbglu_ref):
                ref[...] = jnp.zeros_like(ref)

        perm = perm_ref[...]
        uv = _to_scan_order(perm, u_ref[...])
        yv = y_ref[...]
        dov = _to_scan_order(perm, do_ref[...])
        g, gelu_dy = _gelu_and_grad(yv)
        gb = g.astype(BF16)
        sg = _sigmoid(_dot(gb, wglu_ref[...]) + bglu_ref[...])
        dpre = dov * g * sg * (1.0 - sg)
        dpb = dpre.astype(BF16)
        dwglu_ref[...] += _dot_tn(gb, dpb)
        dbglu_ref[...] += _colsum8(dpre)
        dy = (dov * sg + _dot_nt(dpb, wglu_ref[...])) * gelu_dy
        dd_ref[...] += _colsum8(dy * uv)
        dyb = dy.astype(BF16)
        hsr[...] = hr_ref[...].astype(F32)
        hsi[...] = hi_ref[...].astype(F32)
        ub = uv.astype(BF16)
        dus = []

        def state_cotangents(j):
            dct_ref[j] += _dot_tn(_lanes(dyb, j), _state_cat(hr_ref, hi_ref, j))
            _state_split(gr, gi, j, _dot(_lanes(dyb, j), cb_ref[j]))

        def input_cotangents(j):
            gb_j = _state_cat(gr, gi, j)
            dbb_ref[j] += _dot_tn(_lanes(ub, j), gb_j)
            dus.append(_dot(gb_j, bb_ref[j]))

        def scan(j):
            cols = _states(j)
            ar = ar_ref[:, cols]
            ai = -ai_ref[:, cols]
            g_r = g_i = jnp.zeros((SUBLANES, SLAB_STATES), F32)
            for k in range(seg - 1, -1, -1):
                rows = pl.ds(SUBLANES * k, SUBLANES)
                n_r, n_i = _cmul(ar, ai, g_r, g_i)
                g_r = n_r + gr[rows, cols]
                g_i = n_i + gi[rows, cols]
            er[:, cols] = g_r
            ei[:, cols] = g_i
            pr, pi = _cpow(ar[0:1, :], ai[0:1, :], seg)
            c_r = cr[:, cols]
            c_i = ci[:, cols]
            for q in range(SUBLANES - 1, -1, -1):
                jr[q:q + 1, cols] = c_r
                ji[q:q + 1, cols] = c_i
                n_r, n_i = _cmul(pr, pi, c_r, c_i)
                c_r = n_r + er[q:q + 1, cols]
                c_i = n_i + ei[q:q + 1, cols]
            cr[:, cols] = c_r
            ci[:, cols] = c_i
            g_r = jr[:, cols]
            g_i = ji[:, cols]
            a_r = a_i = jnp.zeros((SUBLANES, SLAB_STATES), F32)
            for k in range(seg - 1, -1, -1):
                rows = pl.ds(SUBLANES * k, SUBLANES)
                h_r = hsr[rows, cols]
                h_i = hsi[rows, cols]
                a_r = a_r + g_r * h_r + g_i * h_i
                a_i = a_i + g_i * h_r - g_r * h_i
                n_r, n_i = _cmul(ar, ai, g_r, g_i)
                g_r = n_r + gr[rows, cols]
                g_i = n_i + gi[rows, cols]
                gr[rows, cols] = g_r
                gi[rows, cols] = g_i
            dar_ref[:, cols] += a_r
            dai_ref[:, cols] += a_i

        for stage in (state_cotangents, scan, input_cotangents):
            for j in range(N_USLAB):
                stage(j)
        du = dy * dsk_ref[...] + jnp.concatenate(dus, axis=1)
        du_ref[...] = _dot_tn(perm, du.astype(BF16)).astype(BF16)

    big = (TS, N_STATE)
    return pl.pallas_call(
        body, grid=(nt,),
        in_specs=[rev(D_SSM), rev(D_SSM), rev(D_SSM), rev(N_STATE), rev(N_STATE), _const((TS, TS)),
                  _const(STATE_TILE), _const(STATE_TILE), _const(S5_IN), _const(S5_OUT), _const((1, D_SSM)),
                  _const((D_SSM, D_SSM)), _const((1, D_SSM))],
        out_specs=[rev(D_SSM), _acc(S5_IN), _acc(S5_IN), _acc(STATE_TILE), _acc(STATE_TILE),
                   _acc((SUBLANES, D_SSM)), _acc((D_SSM, D_SSM)), _acc((SUBLANES, D_SSM))],
        out_shape=[jax.ShapeDtypeStruct((s, D_SSM), BF16), jax.ShapeDtypeStruct(S5_IN, F32),
                   jax.ShapeDtypeStruct(S5_IN, F32), jax.ShapeDtypeStruct(STATE_TILE, F32),
                   jax.ShapeDtypeStruct(STATE_TILE, F32), jax.ShapeDtypeStruct((SUBLANES, D_SSM), F32),
                   jax.ShapeDtypeStruct((D_SSM, D_SSM), F32), jax.ShapeDtypeStruct((SUBLANES, D_SSM), F32)],
        scratch_shapes=[pltpu.VMEM(big, F32), pltpu.VMEM(big, F32), pltpu.VMEM(big, F32), pltpu.VMEM(big, F32),
                        pltpu.VMEM(STATE_TILE, F32), pltpu.VMEM(STATE_TILE, F32), pltpu.VMEM(STATE_TILE, F32),
                        pltpu.VMEM(STATE_TILE, F32), pltpu.VMEM((1, N_STATE), F32), pltpu.VMEM((1, N_STATE), F32)],
        name=f"s5_bwd_{tag}", compiler_params=_cp(dimension_semantics=("arbitrary",)),
    )(dout, u, y, h_re, h_im, _scan_order(), p["a_re8"], p["a_im8"], p["cb3"], p["bb3"], p["d_skip"], p["w_glu"],
      p["b_glu"])


def _mix_bwd(dza, db, dc, pooled, zuv, x0, dx1, p, tag):
    s = x0.shape[0]
    nt = s // TS

    def rev(n):
        return pl.BlockSpec((TS, n), lambda i: (nt - 1 - i, 0))

    def body(da_ref, db_ref, dc_ref, po_ref, z_ref, x_ref, dx1_ref, wl_ref, wp_ref, wpt_ref, sc_ref, lng_ref, lnb_ref,
             wsp_ref, wspt_ref, bias_ref, win_ref, g_ref,
             dx0_ref, dz_ref, dwp_ref, dsc_ref, dws_ref, dbias_ref, dlng_ref, dlnb_ref, dg_ref, buf, *tmp):
        step = pl.program_id(0)
        i = nt - 1 - step

        @pl.when(step == 0)
        def _():
            for ref in (dwp_ref, dsc_ref, dws_ref, dbias_ref, dlng_ref, dlnb_ref, dg_ref):
                ref[...] = jnp.zeros_like(ref)
            for ref in (buf, *tmp):
                ref[pl.ds(TS, POOL_ROWS - TS), :] = jnp.zeros((POOL_ROWS - TS, D_POOL), F32)

        wl = wl_ref[...]
        sc = sc_ref[...]
        dob = db_ref[...]
        pooled_b = po_ref[...]
        dsc_ref[...] += _colsum8(dob * _dot(pooled_b, wp_ref[...]))
        dmixb = (dob * sc).astype(BF16)
        dwp_ref[...] += _dot_tn(pooled_b, dmixb)
        dpool = _dot(dmixb, wpt_ref[...])
        dq = dpool / _pool_count(i, TS, wl)
        buf[pl.ds(0, TS), :] = dq
        dzb = _window_sum(buf, tmp, 0, wl, 1) - dpool
        buf[pl.ds(TS, MAX_WINDOW), :] = dq[:MAX_WINDOW, :]

        lo, hi = _half_masks()
        lng = lng_ref[...]
        u, vn, rs, vl, gelu_du, gelu_dv = _sgu_front(z_ref[...], lng, lnb_ref[...], grads=True)
        mixed = _sgu_mix(vl, wsp_ref, lo, hi) + jnp.tile(bias_ref[...], (TS // CHUNK, 1))
        doc = dc_ref[...]
        dzu = doc * mixed * gelu_du
        dmix = doc * u
        dbias = dbias_ref[...]
        for c in range(TS // CHUNK):
            dmc = dmix[CHUNK * c:CHUNK * (c + 1), :]
            dbias = dbias + dmc
            vlc = vl[CHUNK * c:CHUNK * (c + 1), :].astype(BF16)
            for q in range(SGU_HEADS // 2):
                dmq = _lanes(dmc, q)
                vq = _lanes(vlc, q)
                dws_ref[2 * q] += _dot_nt((dmq * lo).astype(BF16), vq)
                dws_ref[2 * q + 1] += _dot_nt((dmq * hi).astype(BF16), vq)
        dbias_ref[...] = dbias
        dvl = _sgu_mix(dmix, wspt_ref, lo, hi)
        dlng_ref[...] += _colsum8(dvl * vn)
        dlnb_ref[...] += _colsum8(dvl)
        dvn = dvl * lng
        dv = rs * (dvn - jnp.mean(dvn, axis=-1, keepdims=True) - vn * jnp.mean(dvn * vn, axis=-1, keepdims=True))

        dz = jnp.concatenate([da_ref[...], dzb.astype(BF16), dzu.astype(BF16), (dv * gelu_dv).astype(BF16)], axis=1)
        dz_ref[...] = dz
        n, r = _rms(x_ref[...])
        dxn, dgp = _rms_bwd(_dot(dz, win_ref[...]), n, r, g_ref[...])
        dg_ref[...] += _colsum8(dgp)
        dx0_ref[...] = dx1_ref[...] + dxn

    pair = (SGU_HEADS // 2, CHUNK, 2 * CHUNK)
    return pl.pallas_call(
        body, grid=(nt,),
        in_specs=[rev(D_SSM), rev(D_POOL), rev(D_SGU), rev(D_POOL), rev(2 * D_SGU), rev(D_MODEL), rev(D_MODEL),
                  _const((1, D_POOL)), _const((D_POOL, D_POOL)), _const((D_POOL, D_POOL)), _const((1, D_POOL)),
                  _const((1, D_SGU)), _const((1, D_SGU)), _const(pair), _const(pair), _const((CHUNK, D_SGU)),
                  _const((D_IN, D_MODEL)), _const((1, D_MODEL))],
        out_specs=[rev(D_MODEL), rev(D_IN), _acc((D_POOL, D_POOL)), _acc((SUBLANES, D_POOL)),
                   _acc((SGU_HEADS, CHUNK, CHUNK)), _acc((CHUNK, D_SGU)), _acc((SUBLANES, D_SGU)),
                   _acc((SUBLANES, D_SGU)), _acc((SUBLANES, D_MODEL))],
        out_shape=[jax.ShapeDtypeStruct((s, D_MODEL), F32), jax.ShapeDtypeStruct((s, D_IN), BF16),
                   jax.ShapeDtypeStruct((D_POOL, D_POOL), F32), jax.ShapeDtypeStruct((SUBLANES, D_POOL), F32),
                   jax.ShapeDtypeStruct((SGU_HEADS, CHUNK, CHUNK), F32), jax.ShapeDtypeStruct((CHUNK, D_SGU), F32),
                   jax.ShapeDtypeStruct((SUBLANES, D_SGU), F32), jax.ShapeDtypeStruct((SUBLANES, D_SGU), F32),
                   jax.ShapeDtypeStruct((SUBLANES, D_MODEL), F32)],
        scratch_shapes=[pltpu.VMEM((POOL_ROWS, D_POOL), F32)] * 4,
        name=f"mix_bwd_{tag}", compiler_params=_cp(dimension_semantics=("arbitrary",)),
    )(dza, db, dc, pooled, zuv, x0, dx1, _pool_consts(), p["w_pool_bd"], p["w_pool_bd_t"], p["pool_scale"],
      p["sgu_ln_g"], p["sgu_ln_b"], p["ws_pair"], p["ws_pair_t"], p["bias_sp"], p["w_in"], p["g_mix"])


def _atb(a, b, tag, token=None):
    s, ka = a.shape
    kb = b.shape[1]
    ts = ATB_ROWS
    tn = min(kb, ATB_COLS)
    ns = s // ts
    after = [] if token is None else [token]

    def body(a_ref, b_ref, *rest):
        o_ref = rest[-1]

        @pl.when(pl.program_id(1) == 0)
        def _():
            o_ref[...] = jnp.zeros_like(o_ref)

        o_ref[...] += _dot_tn(a_ref[...].astype(BF16), b_ref[...].astype(BF16))

    return pl.pallas_call(
        body, grid=(kb // tn, ns),
        in_specs=[pl.BlockSpec((ts, ka), lambda j, i: (i, 0)), pl.BlockSpec((ts, tn), lambda j, i: (i, j))]
        + [pl.BlockSpec(memory_space=pl.ANY)] * len(after),
        out_specs=pl.BlockSpec((ka, tn), lambda j, i: (0, j)),
        out_shape=jax.ShapeDtypeStruct((ka, kb), F32),
        name=f"atb_{tag}", compiler_params=_cp(dimension_semantics=("arbitrary", "arbitrary")),
    )(a, b, *after)


def _s5_discretise(a_re, a_im, log_dt, b_re, b_im):
    dt = jnp.exp(log_dt)[:, None]
    mag = jnp.exp(a_re * dt)
    ar = mag * jnp.cos(a_im * dt)
    ai = mag * jnp.sin(a_im * dt)
    den = a_re * a_re + a_im * a_im
    f_re = ((ar - 1.0) * a_re + ai * a_im) / den
    f_im = (ai * a_re - (ar - 1.0) * a_im) / den
    bb_re = f_re[..., None] * b_re - f_im[..., None] * b_im
    bb_im = f_re[..., None] * b_im + f_im[..., None] * b_re
    return ar, ai, bb_re, bb_im


def _block_diag(blocks):
    g, r, c = blocks.shape
    eye = jnp.eye(g, dtype=blocks.dtype)
    return (blocks[:, :, None, :] * eye[:, None, :, None]).reshape(g * r, g * c)


def _block_diag_extract(m, g):
    r = m.shape[0] // g
    c = m.shape[1] // g
    eye = jnp.eye(g, dtype=m.dtype)
    return jnp.sum(m.reshape(g, r, g, c) * eye[:, None, :, None], axis=2)


GROUPS_PER_SLAB = N_GROUPS // N_USLAB


def _slab_diag(blocks):
    k = GROUPS_PER_SLAB
    _, r, c = blocks.shape
    eye = jnp.eye(k, dtype=blocks.dtype)
    spread = blocks.reshape(N_USLAB, k, r, 1, c) * eye[None, :, None, :, None]
    return spread.reshape(N_USLAB, k * r, k * c)


def _slab_diag_extract(m):
    k = GROUPS_PER_SLAB
    r, c = m.shape[1] // k, m.shape[2] // k
    eye = jnp.eye(k, dtype=m.dtype)
    return jnp.sum(m.reshape(N_USLAB, k, r, k, c) * eye[None, :, None, :, None], axis=3).reshape(N_GROUPS, r, c)


def _state_slabs(v):
    return jnp.broadcast_to(v.reshape(1, N_STATE), STATE_TILE)


def _tril():
    return jnp.tril(jnp.ones((CHUNK, CHUNK), dtype=bool))


def _layer_params(w, l):
    row = lambda v: v.reshape(1, -1)
    t = lambda m: jnp.swapaxes(m, -1, -2)
    ar, ai, bb_re, bb_im = _s5_discretise(w["A_re"][l], w["A_im"][l], w["log_dt"][l], w["B_re"][l], w["B_im"][l])
    bbt3 = jnp.concatenate([_slab_diag(t(bb_re)), _slab_diag(t(bb_im))], axis=2).astype(BF16)
    ct3 = jnp.concatenate([_slab_diag(t(w["C_re"][l])), -_slab_diag(t(w["C_im"][l]))], axis=1).astype(BF16)
    ws = jnp.where(_tril()[None], w["w_spatial"][l], 0.0)
    pair = lambda m: jnp.stack([jnp.concatenate([m[2 * q], m[2 * q + 1]], axis=1)
                                for q in range(SGU_HEADS // 2)]).astype(BF16)
    wp = _block_diag(w["w_pool"][l]).astype(BF16)
    p = dict(
        g_mix=row(w["g_mix"][l]), g_ffn=row(w["g_ffn"][l]), d_skip=row(w["D_skip"][l]), b_glu=row(w["b_glu"][l]),
        pool_scale=row(w["pool_scale"][l]), sgu_ln_g=row(w["sgu_ln_g"][l]), sgu_ln_b=row(w["sgu_ln_b"][l]),
        a_re8=_state_slabs(ar), a_im8=_state_slabs(ai),
        bbt3=bbt3, bb3=t(bbt3), ct3=ct3, cb3=t(ct3),
        w_pool_bd=wp, w_pool_bd_t=t(wp), ws_pair=pair(ws), ws_pair_t=pair(t(ws)),
        bias_sp=jnp.repeat(t(w["b_spatial"][l]), SGU_HEAD_DIM, axis=1),
    )
    return p


MIX_WEIGHTS = ("w_in", "w_glu")
FFN_WEIGHTS = ("w_out", "w_gate", "w_up", "w_down")


def _with_big(p, mats):
    p.update(mats)


def _rows_sum(v):
    return jnp.sum(v, axis=0)


ATB_COLS = 1024
ATB_ROWS = 1024


def _after(v, token):
    return v if token is None else v + token[0, 0]


def _join(a, b):
    return b if a is None else a if b is None else a + b


def _layer_bwd(dx2, sv, p, w, l, tag, hooks, token):
    t = lambda m: jnp.swapaxes(m, -1, -2)
    dx1, da, db, dc, dgt, dup, act, dg_ffn = _blk_bwd(dx2, sv["x1"], sv["gt"], sv["up"],
                                                      dict(p, g_ffn=_after(p["g_ffn"], token)), tag)
    token = hooks["tick"]([dx1])
    token = _join(token, hooks["on_grads"](l, "ffn", {
        "w_down": _atb(act, dx2, tag + "_wd", token), "w_gate": _atb(dgt, sv["h2"], tag + "_wg", token),
        "w_up": _atb(dup, sv["h2"], tag + "_wu", token), "w_out": _atb(sv["ycat"], dx1, tag + "_wo", token)}))
    g = {}
    g["g_ffn"] = _rows_sum(dg_ffn)
    dza, dc3, dbbt3, dar8, dai8, dd8, dwglu, dbglu8 = _s5_bwd(
        da, sv["za"], sv["y"], sv["h_re"], sv["h_im"], dict(p, d_skip=_after(p["d_skip"], token)), tag)
    token = hooks["tick"]([dza])
    dx0, dz, dwp, dsc8, dws, dbias, dlng8, dlnb8, dg_mix = _mix_bwd(
        dza, db, dc, sv["pooled"], sv["zuv"], sv["x0"], dx1, dict(p, pool_scale=_after(p["pool_scale"], token)), tag)
    g["g_mix"] = _rows_sum(dg_mix)
    g["b_glu"] = _rows_sum(dbglu8)
    g["D_skip"] = _rows_sum(dd8)
    half = N_STATE // N_USLAB
    g["C_re"] = _slab_diag_extract(dc3[:, :, :half])
    g["C_im"] = -_slab_diag_extract(dc3[:, :, half:])
    dar = jnp.sum(dar8, axis=0).reshape(N_GROUPS, SSM_STATE)
    dai = jnp.sum(dai8, axis=0).reshape(N_GROUPS, SSM_STATE)
    dbb_re = t(_slab_diag_extract(dbbt3[:, :, :half]))
    dbb_im = t(_slab_diag_extract(dbbt3[:, :, half:]))
    _, disc_vjp = jax.vjp(_s5_discretise, w["A_re"][l], w["A_im"][l], w["log_dt"][l], w["B_re"][l], w["B_im"][l])
    g["A_re"], g["A_im"], g["log_dt"], g["B_re"], g["B_im"] = disc_vjp((dar, dai, dbb_re, dbb_im))
    g["w_pool"] = _block_diag_extract(dwp, len(POOL_WINDOWS))
    g["pool_scale"] = _rows_sum(dsc8)
    g["sgu_ln_g"] = _rows_sum(dlng8)
    g["sgu_ln_b"] = _rows_sum(dlnb8)
    g["w_spatial"] = jnp.where(_tril()[None], dws, 0.0)
    g["b_spatial"] = t(jnp.sum(dbias.reshape(CHUNK, SGU_HEADS, SGU_HEAD_DIM), axis=-1))
    token = hooks["on_small"](l, g)
    token = hooks["on_grads"](l, "mix", {"w_in": _atb(dz, sv["h1"], tag + "_wi", token), "w_glu": dwglu})
    return dx0, token


def _local_step(x, target, w, hooks):
    params = [_layer_params(w, l) for l in range(DEPTH)]
    saved = []
    h = x
    for l in range(DEPTH):
        p, tag = params[l], f"l{l}"
        _with_big(p, hooks["get_big"](l, "mix", [h]))
        za, zuv, h1, ob, pooled, oc = _mix_fwd(h, p, tag)
        oa, y, h_re, h_im = _s5_fwd(za, p, tag)
        _with_big(p, hooks["get_big"](l, "ffn", [oa, ob, oc]))
        head = (target, w["g_final"].reshape(1, -1)) if l == DEPTH - 1 else None
        x1, x2, h2, gt, up, ycat, *loss_parts = _blk_fwd(h, oa, ob, oc, p, tag, head)
        saved.append(dict(x0=h, za=za, zuv=zuv, h1=h1, ycat=ycat, y=y, h_re=h_re, h_im=h_im, pooled=pooled, x1=x1,
                          h2=h2, gt=gt, up=up))
        h = x2
    dx = h
    loss8, dgf8 = loss_parts
    grads = [None] * DEPTH

    def on_small(l, g_l):
        grads[l] = g_l
        if l > 0:
            return None
        g = {n: jnp.stack([grads[k][n] for k in range(DEPTH)]) for n in SMALL if n != "g_final"}
        g["g_final"] = _rows_sum(dgf8)
        return hooks["on_small"](g, loss8[0, 0])

    token = None
    for l in reversed(range(DEPTH)):
        dx, token = _layer_bwd(dx, saved[l], params[l], w, l, f"l{l}", dict(hooks, on_small=on_small), token)
    return dx


_ANY = pl.BlockSpec(memory_space=pl.ANY)
_MESH = pl.DeviceIdType.MESH


def _place():
    return lax.axis_index("x"), lax.axis_index("y"), lax.axis_index("c")


def _other_chips(x, y):
    return [(1 - x, y), (x, 1 - y), (1 - x, 1 - y)]


def _dma_sems(n):
    return pltpu.SemaphoreType.DMA((n,))


def _remote(src, dst, send_sems, recv_sems, k, to):
    return pltpu.make_async_remote_copy(src_ref=src, dst_ref=dst, send_sem=send_sems.at[k], recv_sem=recv_sems.at[k],
                                        device_id=to, device_id_type=_MESH)


_HBM = pl.BlockSpec(memory_space=pltpu.HBM)
_SEM = pl.BlockSpec(memory_space=pltpu.SEMAPHORE)
_EFFECT = pltpu.SideEffectType.DATAFLOW_SIDE_EFFECTING
N_REL = N_CHIPS - 1


def _gather_plan(x, y, c, srcs, lands):
    plan = []
    for l in lands:
        r = l.shape[0] // N_CHIPS
        rows = l.at[pl.ds((2 * x + y) * r, r)]
        plan += [(rows, rows, (cx, cy, c)) for cx, cy in _other_chips(x, y)]
    return plan


def _half_rows(land, chip, c):
    h = land.shape[0] // (2 * N_CHIPS)
    return land.at[pl.ds((2 * chip + c) * h, h)]


def _gather_half_plan(x, y, c, srcs, lands):
    return [(_half_rows(l, 2 * x + y, c), _half_rows(l, 2 * x + y, c), (cx, cy, c))
            for l in lands for cx, cy in _other_chips(x, y)]


def _forward_halves(lands, tag):
    nw = len(lands)

    def body(*refs):
        ins = refs[:nw]
        send_sems, recv_sems = refs[2 * nw:]
        x, y, c = _place()
        chips = [2 * cx + cy for cx, cy in _other_chips(x, y)]
        sends = [_remote(_half_rows(ins[i], k, c), _half_rows(ins[i], k, c), send_sems, recv_sems, N_REL * i + j,
                         (x, y, 1 - c)) for i in range(nw) for j, k in enumerate(chips)]
        for cp in sends:
            cp.start()
        for i in range(nw):
            for j, k in enumerate(chips):
                sends[N_REL * i + j].wait_send()
                _remote(_half_rows(ins[i], k, c), _half_rows(ins[i], k, 1 - c), send_sems, recv_sems, N_REL * i + j,
                        (x, y, 1 - c)).wait_recv()

    return pl.pallas_call(
        body, out_shape=[jax.ShapeDtypeStruct(a.shape, a.dtype) for a in lands], in_specs=[_ANY] * nw,
        out_specs=[_ANY] * nw, input_output_aliases={i: i for i in range(nw)},
        scratch_shapes=[_dma_sems(N_REL * nw), _dma_sems(N_REL * nw)], name=f"forward_halves_{tag}",
    )(*lands)


def _sibling_plan(x, y, c, srcs, lands):
    return [(s.at[:, 1 - c], l, (x, y, 1 - c)) for s, l in zip(srcs, lands)]


def _slab_plan(x, y, c, srcs, lands):
    return [(s.at[2 * cx + cy], l.at[j], (cx, cy, c))
            for s, l in zip(srcs, lands) for j, (cx, cy) in enumerate(_other_chips(x, y))]


def _plan_copies(plan, srcs, lands, send_sems, recv_sems):
    x, y, c = _place()
    return [_remote(s, d, send_sems, recv_sems, k, to) for k, (s, d, to) in enumerate(plan(x, y, c, srcs, lands))]


def _hbm(a):
    return pltpu.with_memory_space_constraint(a, pltpu.HBM)


def _everyone_plan(x, y, c, srcs, lands):
    me = 4 * x + 2 * y + c
    peers = [(x, y, 1 - c)] + [(cx, cy, cc) for cx, cy in _other_chips(x, y) for cc in (c, 1 - c)]
    return [(s, l.at[me], peer) for s, l in zip(srcs, lands) for peer in peers]


def _copies_start(name, plan, srcs, lands, ncopies):
    ns, n = len(srcs), len(srcs) + len(lands)

    def body(*refs):
        for cp in _plan_copies(plan, refs[:ns], refs[ns:n], refs[n], refs[n + 1]):
            cp.start()
        refs[-1][...] = jnp.zeros_like(refs[-1])

    ref_out = [pltpu.HBM(a.shape, a.dtype) for a in (*srcs, *lands)]
    out = pl.pallas_call(
        body, name=name, in_specs=[_HBM] * n,
        out_shape=(_dma_sems(ncopies), _dma_sems(ncopies), *ref_out, jax.ShapeDtypeStruct((SUBLANES, LANES), F32)),
        out_specs=(_SEM, _SEM, *[_HBM] * n, pl.BlockSpec(memory_space=pltpu.VMEM)),
        input_output_aliases={i: 2 + i for i in range(n)},
        compiler_params=pltpu.CompilerParams(has_side_effects=_EFFECT),
    )(*[_hbm(a) for a in (*srcs, *lands)])
    return dict(name=name, plan=plan, sems=out[:2], srcs=out[2:2 + ns], lands=out[2 + ns:2 + n], token=out[-1])


def _copies_wait(started, after):
    ns = len(started["srcs"])
    n = ns + len(started["lands"])
    plan = started["plan"]

    def body(*refs):
        for cp in _plan_copies(plan, refs[:ns], refs[ns:n], refs[n], refs[n + 1]):
            cp.wait_send()
            cp.wait_recv()

    args = (*started["srcs"], *started["lands"])
    out = pl.pallas_call(
        body, name=started["name"] + "_wait", out_shape=[pltpu.HBM(a.shape, a.dtype) for a in args],
        in_specs=[_HBM] * n + [_SEM, _SEM] + [_ANY] * len(after), out_specs=[_HBM] * n,
        input_output_aliases={i: i for i in range(n)},
        compiler_params=pltpu.CompilerParams(has_side_effects=_EFFECT),
    )(*args, *started["sems"], *after)
    return out[:ns], out[ns:]


def _place_shards(ws, layer, sel, after, tag):
    nw = len(ws)

    def body(sel_ref, *refs):
        for i in range(nw):
            refs[nw + len(after) + i][...] = refs[i][...].astype(BF16)

    return pl.pallas_call(
        body, grid_spec=pltpu.PrefetchScalarGridSpec(
            num_scalar_prefetch=1, grid=(1,),
            in_specs=[pl.BlockSpec((None,) + a.shape[1:], lambda i, s: (layer, 0, 0)) for a in ws] + [_ANY] * len(after),
            out_specs=[pl.BlockSpec(a.shape[1:], lambda i, s: (s[1], 0)) for a in ws]),
        out_shape=[jax.ShapeDtypeStruct((N_CHIPS * a.shape[1], a.shape[2]), BF16) for a in ws],
        name=f"place_shards_{tag}", compiler_params=_cp(dimension_semantics=("arbitrary",)),
    )(sel, *ws, *after)


def _share_halves(fs, layer, tag):
    nw = len(fs)

    def body(*refs):
        ins = refs[:nw]
        send_sems, recv_sems = refs[2 * nw:]
        x, y, c = _place()

        def half(i, who):
            h = ins[i].shape[1] // 2
            return ins[i].at[layer, pl.ds(who * h, h)]

        sends = [_remote(half(i, c), half(i, c), send_sems, recv_sems, i, (x, y, 1 - c)) for i in range(nw)]
        for cp in sends:
            cp.start()
        for i in range(nw):
            sends[i].wait_send()
            _remote(half(i, c), half(i, 1 - c), send_sems, recv_sems, i, (x, y, 1 - c)).wait_recv()

    return pl.pallas_call(
        body, out_shape=[jax.ShapeDtypeStruct(f.shape, f.dtype) for f in fs], in_specs=[_ANY] * nw,
        out_specs=[_ANY] * nw, input_output_aliases={i: i for i in range(nw)},
        scratch_shapes=[_dma_sems(nw), _dma_sems(nw)], name=f"share_halves_{tag}",
    )(*fs)


def _add_halves(g4s, recvs, sel, tag):
    nw = len(g4s)

    def body(sel_ref, *refs):
        for i in range(nw):
            refs[2 * nw + i][...] = (refs[i][...] + refs[nw + i][...]).astype(BF16)

    mine = [pl.BlockSpec((None, None) + g.shape[2:], lambda k, s: (k, s[0], 0, 0)) for g in g4s]
    slab = [pl.BlockSpec((None,) + g.shape[2:], lambda k, s: (k, 0, 0)) for g in g4s]
    return pl.pallas_call(
        body, grid_spec=pltpu.PrefetchScalarGridSpec(num_scalar_prefetch=1, grid=(N_CHIPS,), in_specs=mine + slab,
                                                     out_specs=slab),
        out_shape=[jax.ShapeDtypeStruct(r.shape, BF16) for r in recvs], name=f"add_halves_{tag}",
        compiler_params=_cp(dimension_semantics=("arbitrary",)),
    )(sel, *g4s, *recvs)


def _add_chips(ps, slabs, fs, layer, sel, tag):
    nw = len(ps)
    old = [f for f in fs if f is not None]

    def body(sel_ref, *refs):
        outs = refs[2 * nw + len(old):]
        for i in range(nw):
            acc = refs[i][...].astype(F32)
            for j in range(N_REL):
                acc = acc + refs[nw + i][j].astype(F32)
            outs[i][...] = acc

    shapes = [(DEPTH, 2 * p.shape[1], p.shape[2]) for p in ps]
    in_specs = [pl.BlockSpec((None,) + p.shape[1:], lambda i, s: (s[1], 0, 0)) for p in ps]
    in_specs += [pl.BlockSpec(sl.shape, lambda i, s: (0, 0, 0)) for sl in slabs]
    in_specs += [_ANY] * len(old)
    first_old = 1 + 2 * nw
    aliases, k = {}, 0
    for i, f in enumerate(fs):
        if f is not None:
            aliases[first_old + k] = i
            k += 1
    return pl.pallas_call(
        body, grid_spec=pltpu.PrefetchScalarGridSpec(
            num_scalar_prefetch=1, grid=(1,), in_specs=in_specs,
            out_specs=[pl.BlockSpec((None,) + p.shape[1:], lambda i, s: (layer, s[0], 0)) for p in ps]),
        out_shape=[jax.ShapeDtypeStruct(sh, F32) for sh in shapes], input_output_aliases=aliases,
        name=f"add_chips_{tag}", compiler_params=_cp(dimension_semantics=("arbitrary",)),
    )(sel, *ps, *slabs, *old)


def _adamw_math(w, g, m, v):
    m = ADAM_B1 * m + (1.0 - ADAM_B1) * g
    v = ADAM_B2 * v + (1.0 - ADAM_B2) * (g * g)
    m_hat = m / (1.0 - ADAM_B1 ** ADAM_STEP)
    v_hat = v / (1.0 - ADAM_B2 ** ADAM_STEP)
    delta = -ADAM_LR * (m_hat / (jnp.sqrt(v_hat) + ADAM_EPS) + ADAM_WD * w)
    return delta, m, v


ADAM_ROWS = 512


def _row_tile(rows, most):
    return max(t for t in range(SUBLANES, most + 1, SUBLANES) if rows % t == 0)


def _adamw(w, g, m, v, tag, after=()):
    depth, rows, cols = w.shape
    tr = _row_tile(rows, ADAM_ROWS)

    def body(w_ref, g_ref, m_ref, v_ref, *rest):
        d_ref, nm_ref, nv_ref = rest[len(after):]
        d, nm, nv = _adamw_math(w_ref[...], g_ref[...], m_ref[...], v_ref[...])
        d_ref[...] = d
        nm_ref[...] = nm
        nv_ref[...] = nv

    spec = pl.BlockSpec((None, tr, cols), lambda l, i: (l, i, 0))
    return pl.pallas_call(
        body, grid=(depth, rows // tr), in_specs=[spec] * 4 + [_ANY] * len(after), out_specs=[spec] * 3,
        out_shape=[jax.ShapeDtypeStruct(w.shape, F32)] * 3, name=f"adamw_{tag}",
        compiler_params=_cp(dimension_semantics=("arbitrary", "arbitrary")),
    )(w, g, m, v, *after)


SMALL_TILE = 512
PRECISE = ("g_final",)
COARSE = [n for n in SMALL if n not in PRECISE]


def _small_reduce(gathered):
    n = len(gathered)

    def body(*refs):
        for ga_ref, g_ref in zip(refs[:n], refs[n:]):
            g = ga_ref[0].astype(F32)
            for k in range(1, N_DEV):
                g = g + ga_ref[k].astype(F32)
            g_ref[...] = g

    return pl.pallas_call(
        body, out_shape=[jax.ShapeDtypeStruct(b.shape[1:], F32) for b in gathered], name="small_reduce",
        compiler_params=_cp(),
    )(*gathered)


def _small_adamw(g, w, m, v):
    names = list(g)
    n = len(names)
    shapes = {k: g[k].shape if g[k].ndim > 1 else (1,) + g[k].shape for k in names}

    def body(*refs):
        ins, outs = refs[:4 * n], refs[4 * n:]
        for i in range(n):
            d, nm, nv = _adamw_math(ins[n + i][...], ins[i][...], ins[2 * n + i][...], ins[3 * n + i][...])
            outs[i][...] = d
            outs[n + i][...] = nm
            outs[2 * n + i][...] = nv

    out = pl.pallas_call(
        body, out_shape=[jax.ShapeDtypeStruct(shapes[k], F32) for _ in range(3) for k in names], name="adamw_small",
        compiler_params=_cp(),
    )(*[src[k].reshape(shapes[k]) for src in (g, w, m, v) for k in names])
    return [{k: out[j * n + i].reshape(g[k].shape) for i, k in enumerate(names)} for j in range(3)]


def _exchange_form(n, a):
    return jnp.swapaxes(a, 1, 2) if n in TRANSPOSED else a


PACK_ROWS = 16


def _rows_of(size):
    return -(-size // (LANES * PACK_ROWS)) * PACK_ROWS


def _pack(vals, names, extra=None):
    parts = [vals[n].reshape(-1) for n in names] + ([] if extra is None else [extra.reshape(1)])
    tiles = [jnp.pad(a, (0, _rows_of(a.size) * LANES - a.size)).reshape(-1, LANES) for a in parts]
    rows = sum(t.shape[0] for t in tiles)
    if rows > SMALL_TILE:
        tiles.append(jnp.zeros((-rows % SMALL_TILE, LANES), tiles[0].dtype))
    return jnp.concatenate(tiles, axis=0)


def _unpack(buf, like, names):
    out, row = {}, 0
    for n in names:
        rows = _rows_of(like[n].size)
        out[n] = buf[row:row + rows].reshape(-1)[:like[n].size].reshape(like[n].shape)
        row += rows
    return out, buf[row:]


def kernel(x, g_mix, w_in, A_re, A_im, log_dt, B_re, B_im, C_re, C_im, D_skip, w_glu, b_glu, w_pool, pool_scale, sgu_ln_g, sgu_ln_b, w_spatial, b_spatial, w_out, g_ffn, w_gate, w_up, w_down, g_final, loss_target, m_g_mix, m_w_in, m_A_re, m_A_im, m_log_dt, m_B_re, m_B_im, m_C_re, m_C_im, m_D_skip, m_w_glu, m_b_glu, m_w_pool, m_pool_scale, m_sgu_ln_g, m_sgu_ln_b, m_w_spatial, m_b_spatial, m_w_out, m_g_ffn, m_w_gate, m_w_up, m_w_down, m_g_final, v_g_mix, v_w_in, v_A_re, v_A_im, v_log_dt, v_B_re, v_B_im, v_C_re, v_C_im, v_D_skip, v_w_glu, v_b_glu, v_w_pool, v_pool_scale, v_sgu_ln_g, v_sgu_ln_b, v_w_spatial, v_b_spatial, v_w_out, v_g_ffn, v_w_gate, v_w_up, v_w_down, v_g_final):
    loc = locals()
    w = {n: loc[n] for n in WEIGHTS}
    m = {n: loc["m_" + n] for n in WEIGHTS}
    v = {n: loc["v_" + n] for n in WEIGHTS}
    sel = jnp.stack([lax.axis_index("c"), 2 * lax.axis_index("x") + lax.axis_index("y")]).astype(jnp.int32)

    chip = sel[1]

    halves = [(l, half) for l in range(DEPTH) for half in ("mix", "ffn")]
    two_level = {(0, "ffn")}
    names = {"mix": MIX_WEIGHTS, "ffn": FFN_WEIGHTS}
    started = {}
    wx = {n: _exchange_form(n, w[n]) for n in BIG}
    chain = []
    for l, half in halves:
        lands = _place_shards([wx[n] for n in names[half]], l, sel, chain, f"l{l}_{half}")
        plan = _gather_half_plan if (l, half) in two_level else _gather_plan
        started[l, half] = _copies_start(f"weights_l{l}_{half}", plan, [], lands, N_REL * len(lands))
        chain = [started[l, half]["token"]]
    w = dict(w, g_mix=_after(w["g_mix"], started[halves[-1]]["token"]))

    def get_big(l, half, after):
        lands = _copies_wait(started[l, half], after)[1]
        if (l, half) in two_level:
            lands = _forward_halves(lands, f"l{l}_{half}")
        return dict(zip(names[half], lands))

    result = {n: None for n in BIG}
    stage = {"swap": None, "slabs": None}

    def advance(after):
        if stage["slabs"] is not None:
            ex, ns, l, tag = stage["slabs"]
            part, slabs = _copies_wait(ex, after)
            bufs = _add_chips(part, slabs, [result[n] for n in ns], l, sel, tag)
            for n, f in zip(ns, _share_halves(bufs, l, tag)):
                result[n] = f
            stage["slabs"] = None
        if stage["swap"] is None:
            return None
        sw, ns, l, tag = stage["swap"]
        part = _add_halves(*_copies_wait(sw, after), sel, tag)
        slabs = [lax.empty((N_REL,) + p.shape[1:], BF16) for p in part]
        ex = _copies_start(f"grads_{tag}", _slab_plan, part, slabs, N_REL * len(part))
        stage["slabs"], stage["swap"] = (ex, ns, l, tag), None
        return ex["token"]

    def on_grads(l, half, grads):
        ns = list(grads)
        tag = f"l{l}_{half}"
        token = advance([grads[ns[0]]])
        g4s = [grads[n].reshape(N_CHIPS, 2, grads[n].shape[0] // (2 * N_CHIPS), grads[n].shape[1]) for n in ns]
        recvs = [lax.empty((N_CHIPS,) + g4.shape[2:], F32) for g4 in g4s]
        sw = _copies_start(f"swap_{tag}", _sibling_plan, g4s, recvs, len(g4s))
        stage["swap"] = (sw, ns, l, tag)
        return _join(token, sw["token"])

    small = {}

    def on_small(g, loss_local):
        me = 2 * chip + sel[0]
        blocks = [_pack(g, COARSE).astype(BF16), _pack(g, PRECISE, loss_local)]
        lands = [lax.dynamic_update_slice(lax.empty((N_DEV,) + b.shape, b.dtype), b[None], (me, 0, 0)) for b in blocks]
        small.update(_copies_start("small_grads", _everyone_plan, blocks, lands, (N_DEV - 1) * len(blocks)))
        return small["token"]

    dx = _local_step(x[0], loss_target[0], w, dict(get_big=get_big, on_grads=on_grads, tick=advance, on_small=on_small))
    grads, deltas, new_m, new_v = {}, {}, {}, {}

    def update_big(ns, after):
        for n in ns:
            outs = _adamw(wx[n], result[n], _exchange_form(n, m[n]), _exchange_form(n, v[n]), n, after)
            grads[n], deltas[n], new_m[n], new_v[n] = [_exchange_form(n, a) for a in (result[n], *outs)]
            after = [outs[-1]]
        return after

    last_sent = advance([])
    advance(update_big(FFN_WEIGHTS, [last_sent]))
    update_big(MIX_WEIGHTS, [])

    _, gathered = _copies_wait(small, [new_v[n] for n in BIG])
    coarse, precise = _small_reduce(gathered)
    small_g, _ = _unpack(coarse, w, COARSE)
    precise_g, rest = _unpack(precise, w, PRECISE)
    small_g.update(precise_g)
    loss = rest[0, 0]
    grads.update(small_g)
    for store, vals in zip((deltas, new_m, new_v), _small_adamw(small_g, w, m, v)):
        store.update(vals)
    return (loss, dx[None], *[grads[n] for n in WEIGHTS], *[deltas[n] for n in WEIGHTS],
            *[new_m[n] for n in WEIGHTS], *[new_v[n] for n in WEIGHTS])
```

```python
import math

import jax
import jax.numpy as jnp
from jax import lax
from jax.experimental import pallas as pl
from jax.experimental.pallas import tpu as pltpu

F32 = jnp.float32
BF16 = jnp.bfloat16

D_MODEL = 1024
DEPTH = 2
D_SSM = 384
SSM_GROUP = 16
N_GROUPS = 24
SSM_STATE = 64
N_STATE = N_GROUPS * SSM_STATE
POOL_WINDOWS = (2, 4, 8, 16)
POOL_GROUP = 64
D_POOL = 256
MAX_WINDOW = 16
SGU_HEADS = 6
SGU_HEAD_DIM = 64
D_SGU = 384
CHUNK = 128
D_IN = D_SSM + D_POOL + 2 * D_SGU
D_FF = 2816
EPS = 1e-6

ADAM_LR = 0.001
ADAM_B1 = 0.9
ADAM_B2 = 0.999
ADAM_EPS = 1e-08
ADAM_WD = 0.01
ADAM_STEP = 10

LANES = 128
SUBLANES = 8
VMEM_LIMIT = 56 * 1024 * 1024

TS = 512
TS_FFN = 256

WEIGHTS = ['g_mix', 'w_in', 'A_re', 'A_im', 'log_dt', 'B_re', 'B_im', 'C_re', 'C_im', 'D_skip', 'w_glu', 'b_glu',
           'w_pool', 'pool_scale', 'sgu_ln_g', 'sgu_ln_b', 'w_spatial', 'b_spatial', 'w_out', 'g_ffn', 'w_gate',
           'w_up', 'w_down', 'g_final']
BIG = ['w_in', 'w_glu', 'w_out', 'w_gate', 'w_up', 'w_down']
SMALL = [n for n in WEIGHTS if n not in BIG]
TRANSPOSED = ("w_in", "w_gate", "w_up")
N_CHIPS = 4
N_DEV = 8


def _cp(**kw):
    return pltpu.CompilerParams(vmem_limit_bytes=VMEM_LIMIT, **kw)


def _row(ts, n):
    return pl.BlockSpec((ts, n), lambda i: (i, 0))


def _const(shape):
    nd = len(shape)
    return pl.BlockSpec(shape, lambda i: (0,) * nd, pipeline_mode=pl.Buffered(1))


def _acc(shape):
    nd = len(shape)
    return pl.BlockSpec(shape, lambda i: (0,) * nd)


def _dot(a, b):
    return jnp.dot(a, b, preferred_element_type=F32)


def _dot_tn(a, b):
    return lax.dot_general(a, b, (((0,), (0,)), ((), ())), preferred_element_type=F32)


def _dot_nt(a, b):
    return lax.dot_general(a, b, (((1,), (1,)), ((), ())), preferred_element_type=F32)


_G0 = math.sqrt(2.0 / math.pi)
_G1 = 0.044715


def _gelu(x):
    return 0.5 * x * (1.0 + jnp.tanh(_G0 * (x + _G1 * x * x * x)))


def _gelu_and_grad(x):
    t = jnp.tanh(_G0 * (x + _G1 * x * x * x))
    half = 0.5 * (1.0 + t)
    return x * half, half + 0.5 * x * (1.0 - t * t) * (_G0 * (1.0 + 3.0 * _G1 * x * x))


def _sigmoid(x):
    return 1.0 / (1.0 + jnp.exp(-x))


def _rms(x):
    r = lax.rsqrt(jnp.mean(x * x, axis=-1, keepdims=True) + EPS)
    return x * r, r


def _rms_bwd(dh, n, r, g):
    dn = dh * g
    return r * (dn - n * jnp.mean(dn * n, axis=-1, keepdims=True)), dh * n


def _colsum8(v):
    rows, n = v.shape
    return jnp.sum(v.reshape(rows // SUBLANES, SUBLANES, n), axis=0)


def _cmul(ar, ai, br, bi):
    return ar * br - ai * bi, ar * bi + ai * br


def _cpow(ar, ai, n):
    assert n & (n - 1) == 0
    while n > 1:
        ar, ai = _cmul(ar, ai, ar, ai)
        n //= 2
    return ar, ai


N_USLAB = D_SSM // LANES
SEG = TS // SUBLANES
SLAB_STATES = N_STATE // N_USLAB
S5_IN = (N_USLAB, LANES, 2 * SLAB_STATES)
S5_OUT = (N_USLAB, 2 * SLAB_STATES, LANES)
STATE_TILE = (SUBLANES, N_STATE)


def _scan_order():
    p = jnp.arange(TS)
    src = (p % SUBLANES) * SEG + p // SUBLANES
    return (src[:, None] == jnp.arange(TS)[None, :]).astype(BF16)


def _to_scan_order(perm, v):
    hi = v.astype(BF16)
    lo = (v - hi.astype(F32)).astype(BF16)
    return _dot(perm, hi) + _dot(perm, lo)


def _scan_rows(k):
    return pl.ds(pl.multiple_of(k * SUBLANES, SUBLANES), SUBLANES)


def _lanes(v, j):
    return v[:, LANES * j:LANES * (j + 1)]


def _states(j):
    return pl.ds(SLAB_STATES * j, SLAB_STATES)


def _state_split(re_ref, im_ref, j, v):
    re_ref[:, _states(j)] = v[:, :SLAB_STATES]
    im_ref[:, _states(j)] = v[:, SLAB_STATES:]


def _state_cat(re_ref, im_ref, j):
    return jnp.concatenate([re_ref[:, _states(j)], im_ref[:, _states(j)]], axis=1).astype(BF16)


def _s5_fwd(u, p, tag):
    s = u.shape[0]
    seg = SEG

    def body(u_ref, perm_ref, bbt_ref, ar_ref, ai_ref, ct_ref, dsk_ref, wglu_ref, bglu_ref,
             oa_ref, y_ref, hr_ref, hi_ref, sr, si, er, ei, ir, ii, cr, ci):
        @pl.when(pl.program_id(0) == 0)
        def _():
            cr[...] = jnp.zeros_like(cr)
            ci[...] = jnp.zeros_like(ci)

        perm = perm_ref[...]
        uv = _to_scan_order(perm, u_ref[...])
        ub = uv.astype(BF16)
        for j in range(N_USLAB):
            _state_split(sr, si, j, _dot(_lanes(ub, j), bbt_ref[j]))
        for j in range(N_USLAB):
            cols = _states(j)
            ar = ar_ref[:, cols]
            ai = ai_ref[:, cols]
            h_r = h_i = jnp.zeros((SUBLANES, SLAB_STATES), F32)
            for k in range(seg):
                rows = pl.ds(SUBLANES * k, SUBLANES)
                n_r, n_i = _cmul(ar, ai, h_r, h_i)
                h_r = n_r + sr[rows, cols]
                h_i = n_i + si[rows, cols]
            er[:, cols] = h_r
            ei[:, cols] = h_i
            pr, pi = _cpow(ar[0:1, :], ai[0:1, :], seg)
            c_r = cr[:, cols]
            c_i = ci[:, cols]
            for q in range(SUBLANES):
                ir[q:q + 1, cols] = c_r
                ii[q:q + 1, cols] = c_i
                n_r, n_i = _cmul(pr, pi, c_r, c_i)
                c_r = n_r + er[q:q + 1, cols]
                c_i = n_i + ei[q:q + 1, cols]
            cr[:, cols] = c_r
            ci[:, cols] = c_i
            h_r = ir[:, cols]
            h_i = ii[:, cols]
            for k in range(seg):
                rows = pl.ds(SUBLANES * k, SUBLANES)
                n_r, n_i = _cmul(ar, ai, h_r, h_i)
                h_r = n_r + sr[rows, cols]
                h_i = n_i + si[rows, cols]
                sr[rows, cols] = h_r
                si[rows, cols] = h_i
        hr_ref[...] = sr[...].astype(BF16)
        hi_ref[...] = si[...].astype(BF16)
        y = jnp.concatenate([_dot(_state_cat(hr_ref, hi_ref, j), ct_ref[j]) for j in range(N_USLAB)], axis=1)
        y = y + dsk_ref[...] * uv
        y_ref[...] = y
        g = _gelu(y)
        pre = _dot(g.astype(BF16), wglu_ref[...]) + bglu_ref[...]
        oa_ref[...] = _dot_tn(perm, (g * _sigmoid(pre)).astype(BF16)).astype(BF16)

    return pl.pallas_call(
        body, grid=(s // TS,),
        in_specs=[_row(TS, D_SSM), _const((TS, TS)), _const(S5_IN), _const(STATE_TILE), _const(STATE_TILE),
                  _const(S5_OUT), _const((1, D_SSM)), _const((D_SSM, D_SSM)), _const((1, D_SSM))],
        out_specs=[_row(TS, D_SSM), _row(TS, D_SSM), _row(TS, N_STATE), _row(TS, N_STATE)],
        out_shape=[jax.ShapeDtypeStruct((s, D_SSM), BF16), jax.ShapeDtypeStruct((s, D_SSM), F32),
                   jax.ShapeDtypeStruct((s, N_STATE), BF16), jax.ShapeDtypeStruct((s, N_STATE), BF16)],
        scratch_shapes=[pltpu.VMEM((TS, N_STATE), F32), pltpu.VMEM((TS, N_STATE), F32),
                        pltpu.VMEM(STATE_TILE, F32), pltpu.VMEM(STATE_TILE, F32), pltpu.VMEM(STATE_TILE, F32),
                        pltpu.VMEM(STATE_TILE, F32), pltpu.VMEM((1, N_STATE), F32), pltpu.VMEM((1, N_STATE), F32)],
        name=f"s5_fwd_{tag}", compiler_params=_cp(dimension_semantics=("arbitrary",)),
    )(u, _scan_order(), p["bbt3"], p["a_re8"], p["a_im8"], p["ct3"], p["d_skip"], p["w_glu"], p["b_glu"])


def _pool_consts():
    w = jnp.repeat(jnp.asarray(POOL_WINDOWS, F32), POOL_GROUP)[None, :]
    return w


POOL_PAD = SUBLANES
POOL_ROWS = TS + MAX_WINDOW + POOL_PAD


def _window_sum(buf, tmp, first, wl, step):
    assert POOL_WINDOWS == (2, 4, 8, 16)
    n = TS + MAX_WINDOW
    lo = first - MAX_WINDOW if step < 0 else first
    src = buf
    for k, dst in zip((1, 2, 4), tmp):
        dst[pl.ds(lo, n), :] = src[pl.ds(lo, n), :] + src[pl.ds(lo + step * k, n), :]
        src = dst
    s2, s4, s8 = (t[pl.ds(first, TS), :] for t in tmp)
    s16 = s8 + tmp[2][pl.ds(first + step * 8, TS), :]
    return jnp.where(wl == 2, s2, jnp.where(wl == 4, s4, jnp.where(wl == 8, s8, s16)))


def _pool_count(i, rows, wl):
    t = (i * TS + 1).astype(F32) + lax.broadcasted_iota(jnp.int32, (rows, 1), 0).astype(F32)
    return jnp.minimum(t, wl)


def _sgu_mix(vl, wpair_ref, lo, hi):
    rows = vl.shape[0]
    chunks = []
    for c in range(rows // CHUNK):
        vc = vl[CHUNK * c:CHUNK * (c + 1), :]
        parts = []
        for q in range(SGU_HEADS // 2):
            vq = vc[:, LANES * q:LANES * (q + 1)]
            rhs = jnp.concatenate([vq * lo, vq * hi], axis=0).astype(BF16)
            parts.append(_dot(wpair_ref[q], rhs))
        chunks.append(jnp.concatenate(parts, axis=1))
    return jnp.concatenate(chunks, axis=0)


def _sgu_front(zuv, lng, lnb, grads=False):
    gelu = _gelu_and_grad if grads else lambda z: (_gelu(z), None)
    u, du = gelu(zuv[:, :D_SGU])
    v, dv = gelu(zuv[:, D_SGU:])
    mu = jnp.mean(v, axis=-1, keepdims=True)
    vc = v - mu
    rs = lax.rsqrt(jnp.mean(vc * vc, axis=-1, keepdims=True) + EPS)
    vn = vc * rs
    return u, vn, rs, vn * lng + lnb, du, dv


def _half_masks():
    lane = lax.broadcasted_iota(jnp.int32, (1, LANES), 1)
    lo = (lane < SGU_HEAD_DIM).astype(F32)
    return lo, 1.0 - lo


def _mix_fwd(x, p, tag):
    s = x.shape[0]

    def body(x_ref, g_ref, w_ref, wl_ref, wp_ref, sc_ref, lng_ref, lnb_ref, wsp_ref, bias_ref,
             za_ref, zuv_ref, h_ref, ob_ref, pooled_ref, oc_ref, buf, *tmp):
        i = pl.program_id(0)
        tile0 = POOL_PAD + MAX_WINDOW

        @pl.when(i == 0)
        def _():
            for ref in (buf, *tmp):
                ref[pl.ds(0, tile0), :] = jnp.zeros((tile0, D_POOL), F32)

        n, _ = _rms(x_ref[...])
        h = (n * g_ref[...]).astype(BF16)
        h_ref[...] = h
        z = _dot_nt(h, w_ref[...])
        za_ref[...] = z[:, :D_SSM]
        zb = z[:, D_SSM:D_SSM + D_POOL]
        zuv = z[:, D_SSM + D_POOL:]
        zuv_ref[...] = zuv
        buf[pl.ds(tile0, TS), :] = zb
        wl = wl_ref[...]
        pooled = (_window_sum(buf, tmp, tile0, wl, -1) / _pool_count(i, TS, wl) - zb).astype(BF16)
        buf[pl.ds(POOL_PAD, MAX_WINDOW), :] = zb[TS - MAX_WINDOW:, :]
        pooled_ref[...] = pooled
        ob_ref[...] = (_dot(pooled, wp_ref[...]) * sc_ref[...]).astype(BF16)
        lo, hi = _half_masks()
        u, _, _, vl, _, _ = _sgu_front(zuv, lng_ref[...], lnb_ref[...])
        mixed = _sgu_mix(vl, wsp_ref, lo, hi) + jnp.tile(bias_ref[...], (TS // CHUNK, 1))
        oc_ref[...] = (u * mixed).astype(BF16)

    return pl.pallas_call(
        body, grid=(s // TS,),
        in_specs=[_row(TS, D_MODEL), _const((1, D_MODEL)), _const((D_IN, D_MODEL)), _const((1, D_POOL)),
                  _const((D_POOL, D_POOL)), _const((1, D_POOL)), _const((1, D_SGU)), _const((1, D_SGU)),
                  _const((SGU_HEADS // 2, CHUNK, 2 * CHUNK)), _const((CHUNK, D_SGU))],
        out_specs=[_row(TS, D_SSM), _row(TS, 2 * D_SGU), _row(TS, D_MODEL), _row(TS, D_POOL), _row(TS, D_POOL),
                   _row(TS, D_SGU)],
        out_shape=[jax.ShapeDtypeStruct((s, D_SSM), F32), jax.ShapeDtypeStruct((s, 2 * D_SGU), F32),
                   jax.ShapeDtypeStruct((s, D_MODEL), BF16), jax.ShapeDtypeStruct((s, D_POOL), BF16),
                   jax.ShapeDtypeStruct((s, D_POOL), BF16), jax.ShapeDtypeStruct((s, D_SGU), BF16)],
        scratch_shapes=[pltpu.VMEM((POOL_ROWS, D_POOL), F32)] * 4,
        name=f"mix_fwd_{tag}", compiler_params=_cp(dimension_semantics=("arbitrary",)),
    )(x, p["g_mix"], p["w_in"], _pool_consts(), p["w_pool_bd"], p["pool_scale"], p["sgu_ln_g"], p["sgu_ln_b"],
      p["ws_pair"], p["bias_sp"])


def _blk_fwd(x0, oa, ob, oc, p, tag, head=None):
    s = x0.shape[0]
    ts = TS_FFN
    n_head = 0 if head is None else len(head)

    def body(x0_ref, oa_ref, ob_ref, oc_ref, wo_ref, g_ref, wg_ref, wu_ref, wd_ref, *refs):
        x1_ref, x2_ref, h2_ref, gt_ref, up_ref, ycat_ref = refs[n_head:n_head + 6]
        ycat = jnp.concatenate([oa_ref[...], ob_ref[...], oc_ref[...]], axis=1)
        ycat_ref[...] = ycat
        x1 = x0_ref[...] + _dot(ycat, wo_ref[...])
        x1_ref[...] = x1
        n, _ = _rms(x1)
        h2 = (n * g_ref[...]).astype(BF16)
        h2_ref[...] = h2
        gt = _dot_nt(h2, wg_ref[...])
        up = _dot_nt(h2, wu_ref[...])
        gt_ref[...] = gt.astype(BF16)
        up_ref[...] = up.astype(BF16)
        act = (gt * _sigmoid(gt) * up).astype(BF16)
        x2 = x1 + _dot(act, wd_ref[...])
        if head is None:
            x2_ref[...] = x2
            return
        t_ref, gf_ref = refs[:n_head]
        loss_ref, dgf_ref = refs[n_head + 6:]

        @pl.when(pl.program_id(0) == 0)
        def _():
            loss_ref[...] = jnp.zeros_like(loss_ref)
            dgf_ref[...] = jnp.zeros_like(dgf_ref)

        gf = gf_ref[...]
        nf, rf = _rms(x2)
        diff = nf * gf - t_ref[...]
        loss_ref[...] += jnp.sum(diff * diff) * (0.5 / D_MODEL)
        dxn, dgp = _rms_bwd(diff * (1.0 / D_MODEL), nf, rf, gf)
        dgf_ref[...] += _colsum8(dgp)
        x2_ref[...] = dxn

    in_specs = [_row(ts, D_MODEL), _row(ts, D_SSM), _row(ts, D_POOL), _row(ts, D_SGU), _const((D_MODEL, D_MODEL)),
                _const((1, D_MODEL)), _const((D_FF, D_MODEL)), _const((D_FF, D_MODEL)), _const((D_FF, D_MODEL))]
    out_specs = [_row(ts, D_MODEL), _row(ts, D_MODEL), _row(ts, D_MODEL), _row(ts, D_FF), _row(ts, D_FF),
                 _row(ts, D_MODEL)]
    out_shape = [jax.ShapeDtypeStruct((s, D_MODEL), F32), jax.ShapeDtypeStruct((s, D_MODEL), F32),
                 jax.ShapeDtypeStruct((s, D_MODEL), BF16), jax.ShapeDtypeStruct((s, D_FF), BF16),
                 jax.ShapeDtypeStruct((s, D_FF), BF16), jax.ShapeDtypeStruct((s, D_MODEL), BF16)]
    args = (x0, oa, ob, oc, p["w_out"], p["g_ffn"], p["w_gate"], p["w_up"], p["w_down"])
    if head is not None:
        in_specs += [_row(ts, D_MODEL), _const((1, D_MODEL))]
        out_specs += [_acc((SUBLANES, LANES)), _acc((SUBLANES, D_MODEL))]
        out_shape += [jax.ShapeDtypeStruct((SUBLANES, LANES), F32), jax.ShapeDtypeStruct((SUBLANES, D_MODEL), F32)]
        args += tuple(head)
    return pl.pallas_call(
        body, grid=(s // ts,), in_specs=in_specs, out_specs=out_specs, out_shape=out_shape,
        name=f"blk_fwd_{tag}", compiler_params=_cp(dimension_semantics=("arbitrary",)),
    )(*args)


def _blk_bwd(dx2, x1, gt, up, p, tag):
    s = dx2.shape[0]
    ts = TS_FFN

    def body(dx2_ref, x1_ref, gt_ref, up_ref, wd_ref, wgt_ref, wut_ref, wo_ref, g_ref,
             dx1_ref, da_ref, db_ref, dc_ref, dgt_ref, dup_ref, act_ref, dg_ref):
        @pl.when(pl.program_id(0) == 0)
        def _():
            dg_ref[...] = jnp.zeros_like(dg_ref)

        dx2v = dx2_ref[...]
        dact = _dot_nt(dx2v.astype(BF16), wd_ref[...])
        gf = gt_ref[...].astype(F32)
        uf = up_ref[...].astype(F32)
        sg = _sigmoid(gf)
        sl = gf * sg
        act_ref[...] = (sl * uf).astype(BF16)
        dgt = (dact * uf * (sg * (1.0 + gf * (1.0 - sg)))).astype(BF16)
        dup = (dact * sl).astype(BF16)
        dgt_ref[...] = dgt
        dup_ref[...] = dup
        dh2 = _dot(dgt, wgt_ref[...]) + _dot(dup, wut_ref[...])
        n, r = _rms(x1_ref[...])
        dxn, dgp = _rms_bwd(dh2, n, r, g_ref[...])
        dg_ref[...] += _colsum8(dgp)
        dx1 = dx2v + dxn
        dx1_ref[...] = dx1
        dy = _dot_nt(dx1.astype(BF16), wo_ref[...])
        da_ref[...] = dy[:, :D_SSM]
        db_ref[...] = dy[:, D_SSM:D_SSM + D_POOL]
        dc_ref[...] = dy[:, D_SSM + D_POOL:]

    return pl.pallas_call(
        body, grid=(s // ts,),
        in_specs=[_row(ts, D_MODEL), _row(ts, D_MODEL), _row(ts, D_FF), _row(ts, D_FF),
                  _const((D_FF, D_MODEL)), _const((D_FF, D_MODEL)), _const((D_FF, D_MODEL)),
                  _const((D_MODEL, D_MODEL)), _const((1, D_MODEL))],
        out_specs=[_row(ts, D_MODEL), _row(ts, D_SSM), _row(ts, D_POOL), _row(ts, D_SGU), _row(ts, D_FF),
                   _row(ts, D_FF), _row(ts, D_FF), _acc((SUBLANES, D_MODEL))],
        out_shape=[jax.ShapeDtypeStruct((s, D_MODEL), F32), jax.ShapeDtypeStruct((s, D_SSM), F32),
                   jax.ShapeDtypeStruct((s, D_POOL), F32), jax.ShapeDtypeStruct((s, D_SGU), F32),
                   jax.ShapeDtypeStruct((s, D_FF), BF16), jax.ShapeDtypeStruct((s, D_FF), BF16),
                   jax.ShapeDtypeStruct((s, D_FF), BF16), jax.ShapeDtypeStruct((SUBLANES, D_MODEL), F32)],
        name=f"blk_bwd_{tag}", compiler_params=_cp(dimension_semantics=("arbitrary",)),
    )(dx2, x1, gt, up, p["w_down"], p["w_gate"], p["w_up"], p["w_out"], p["g_ffn"])


def _s5_bwd(dout, u, y, h_re, h_im, p, tag):
    s = u.shape[0]
    nt = s // TS
    seg = SEG

    def rev(n):
        return pl.BlockSpec((TS, n), lambda i: (nt - 1 - i, 0))

    def body(do_ref, u_ref, y_ref, hr_ref, hi_ref, perm_ref, ar_ref, ai_ref, cb_ref, bb_ref, dsk_ref,
             wglu_ref, bglu_ref,
             du_ref, dct_ref, dbb_ref, dar_ref, dai_ref, dd_ref, dwglu_ref, dbglu_ref,
             gr, gi, hsr, hsi, er, ei, jr, ji, cr, ci):
        @pl.when(pl.program_id(0) == 0)
        def _():
            for ref in (cr, ci, dct_ref, dbb_ref, dar_ref, dai_ref, dd_ref, dwglu_ref, dbglu_ref):
                ref[...] = jnp.zeros_like(ref)

        perm = perm_ref[...]
        uv = _to_scan_order(perm, u_ref[...])
        yv = y_ref[...]
        dov = _to_scan_order(perm, do_ref[...])
        g, gelu_dy = _gelu_and_grad(yv)
        gb = g.astype(BF16)
        sg = _sigmoid(_dot(gb, wglu_ref[...]) + bglu_ref[...])
        dpre = dov * g * sg * (1.0 - sg)
        dpb = dpre.astype(BF16)
        dwglu_ref[...] += _dot_tn(gb, dpb)
        dbglu_ref[...] += _colsum8(dpre)
        dy = (dov * sg + _dot_nt(dpb, wglu_ref[...])) * gelu_dy
        dd_ref[...] += _colsum8(dy * uv)
        dyb = dy.astype(BF16)
        hsr[...] = hr_ref[...].astype(F32)
        hsi[...] = hi_ref[...].astype(F32)
        ub = uv.astype(BF16)
        dus = []

        def state_cotangents(j):
            dct_ref[j] += _dot_tn(_lanes(dyb, j), _state_cat(hr_ref, hi_ref, j))
            _state_split(gr, gi, j, _dot(_lanes(dyb, j), cb_ref[j]))

        def input_cotangents(j):
            gb_j = _state_cat(gr, gi, j)
            dbb_ref[j] += _dot_tn(_lanes(ub, j), gb_j)
            dus.append(_dot(gb_j, bb_ref[j]))

        def scan(j):
            cols = _states(j)
            ar = ar_ref[:, cols]
            ai = -ai_ref[:, cols]
            g_r = g_i = jnp.zeros((SUBLANES, SLAB_STATES), F32)
            for k in range(seg - 1, -1, -1):
                rows = pl.ds(SUBLANES * k, SUBLANES)
                n_r, n_i = _cmul(ar, ai, g_r, g_i)
                g_r = n_r + gr[rows, cols]
                g_i = n_i + gi[rows, cols]
            er[:, cols] = g_r
            ei[:, cols] = g_i
            pr, pi = _cpow(ar[0:1, :], ai[0:1, :], seg)
            c_r = cr[:, cols]
            c_i = ci[:, cols]
            for q in range(SUBLANES - 1, -1, -1):
                jr[q:q + 1, cols] = c_r
                ji[q:q + 1, cols] = c_i
                n_r, n_i = _cmul(pr, pi, c_r, c_i)
                c_r = n_r + er[q:q + 1, cols]
                c_i = n_i + ei[q:q + 1, cols]
            cr[:, cols] = c_r
            ci[:, cols] = c_i
            g_r = jr[:, cols]
            g_i = ji[:, cols]
            a_r = a_i = jnp.zeros((SUBLANES, SLAB_STATES), F32)
            for k in range(seg - 1, -1, -1):
                rows = pl.ds(SUBLANES * k, SUBLANES)
                h_r = hsr[rows, cols]
                h_i = hsi[rows, cols]
                a_r = a_r + g_r * h_r + g_i * h_i
                a_i = a_i + g_i * h_r - g_r * h_i
                n_r, n_i = _cmul(ar, ai, g_r, g_i)
                g_r = n_r + gr[rows, cols]
                g_i = n_i + gi[rows, cols]
                gr[rows, cols] = g_r
                gi[rows, cols] = g_i
            dar_ref[:, cols] += a_r
            dai_ref[:, cols] += a_i

        for stage in (state_cotangents, scan, input_cotangents):
            for j in range(N_USLAB):
                stage(j)
        du = dy * dsk_ref[...] + jnp.concatenate(dus, axis=1)
        du_ref[...] = _dot_tn(perm, du.astype(BF16)).astype(BF16)

    big = (TS, N_STATE)
    return pl.pallas_call(
        body, grid=(nt,),
        in_specs=[rev(D_SSM), rev(D_SSM), rev(D_SSM), rev(N_STATE), rev(N_STATE), _const((TS, TS)),
                  _const(STATE_TILE), _const(STATE_TILE), _const(S5_IN), _const(S5_OUT), _const((1, D_SSM)),
                  _const((D_SSM, D_SSM)), _const((1, D_SSM))],
        out_specs=[rev(D_SSM), _acc(S5_IN), _acc(S5_IN), _acc(STATE_TILE), _acc(STATE_TILE),
                   _acc((SUBLANES, D_SSM)), _acc((D_SSM, D_SSM)), _acc((SUBLANES, D_SSM))],
        out_shape=[jax.ShapeDtypeStruct((s, D_SSM), BF16), jax.ShapeDtypeStruct(S5_IN, F32),
                   jax.ShapeDtypeStruct(S5_IN, F32), jax.ShapeDtypeStruct(STATE_TILE, F32),
                   jax.ShapeDtypeStruct(STATE_TILE, F32), jax.ShapeDtypeStruct((SUBLANES, D_SSM), F32),
                   jax.ShapeDtypeStruct((D_SSM, D_SSM), F32), jax.ShapeDtypeStruct((SUBLANES, D_SSM), F32)],
        scratch_shapes=[pltpu.VMEM(big, F32), pltpu.VMEM(big, F32), pltpu.VMEM(big, F32), pltpu.VMEM(big, F32),
                        pltpu.VMEM(STATE_TILE, F32), pltpu.VMEM(STATE_TILE, F32), pltpu.VMEM(STATE_TILE, F32),
                        pltpu.VMEM(STATE_TILE, F32), pltpu.VMEM((1, N_STATE), F32), pltpu.VMEM((1, N_STATE), F32)],
        name=f"s5_bwd_{tag}", compiler_params=_cp(dimension_semantics=("arbitrary",)),
    )(dout, u, y, h_re, h_im, _scan_order(), p["a_re8"], p["a_im8"], p["cb3"], p["bb3"], p["d_skip"], p["w_glu"],
      p["b_glu"])


def _mix_bwd(dza, db, dc, pooled, zuv, x0, dx1, p, tag):
    s = x0.shape[0]
    nt = s // TS

    def rev(n):
        return pl.BlockSpec((TS, n), lambda i: (nt - 1 - i, 0))

    def body(da_ref, db_ref, dc_ref, po_ref, z_ref, x_ref, dx1_ref, wl_ref, wp_ref, wpt_ref, sc_ref, lng_ref, lnb_ref,
             wsp_ref, wspt_ref, bias_ref, win_ref, g_ref,
             dx0_ref, dz_ref, dwp_ref, dsc_ref, dws_ref, dbias_ref, dlng_ref, dlnb_ref, dg_ref, buf, *tmp):
        step = pl.program_id(0)
        i = nt - 1 - step

        @pl.when(step == 0)
        def _():
            for ref in (dwp_ref, dsc_ref, dws_ref, dbias_ref, dlng_ref, dlnb_ref, dg_ref):
                ref[...] = jnp.zeros_like(ref)
            for ref in (buf, *tmp):
                ref[pl.ds(TS, POOL_ROWS - TS), :] = jnp.zeros((POOL_ROWS - TS, D_POOL), F32)

        wl = wl_ref[...]
        sc = sc_ref[...]
        dob = db_ref[...]
        pooled_b = po_ref[...]
        dsc_ref[...] += _colsum8(dob * _dot(pooled_b, wp_ref[...]))
        dmixb = (dob * sc).astype(BF16)
        dwp_ref[...] += _dot_tn(pooled_b, dmixb)
        dpool = _dot(dmixb, wpt_ref[...])
        dq = dpool / _pool_count(i, TS, wl)
        buf[pl.ds(0, TS), :] = dq
        dzb = _window_sum(buf, tmp, 0, wl, 1) - dpool
        buf[pl.ds(TS, MAX_WINDOW), :] = dq[:MAX_WINDOW, :]

        lo, hi = _half_masks()
        lng = lng_ref[...]
        u, vn, rs, vl, gelu_du, gelu_dv = _sgu_front(z_ref[...], lng, lnb_ref[...], grads=True)
        mixed = _sgu_mix(vl, wsp_ref, lo, hi) + jnp.tile(bias_ref[...], (TS // CHUNK, 1))
        doc = dc_ref[...]
        dzu = doc * mixed * gelu_du
        dmix = doc * u
        dbias = dbias_ref[...]
        for c in range(TS // CHUNK):
            dmc = dmix[CHUNK * c:CHUNK * (c + 1), :]
            dbias = dbias + dmc
            vlc = vl[CHUNK * c:CHUNK * (c + 1), :].astype(BF16)
            for q in range(SGU_HEADS // 2):
                dmq = _lanes(dmc, q)
                vq = _lanes(vlc, q)
                dws_ref[2 * q] += _dot_nt((dmq * lo).astype(BF16), vq)
                dws_ref[2 * q + 1] += _dot_nt((dmq * hi).astype(BF16), vq)
        dbias_ref[...] = dbias
        dvl = _sgu_mix(dmix, wspt_ref, lo, hi)
        dlng_ref[...] += _colsum8(dvl * vn)
        dlnb_ref[...] += _colsum8(dvl)
        dvn = dvl * lng
        dv = rs * (dvn - jnp.mean(dvn, axis=-1, keepdims=True) - vn * jnp.mean(dvn * vn, axis=-1, keepdims=True))

        dz = jnp.concatenate([da_ref[...], dzb.astype(BF16), dzu.astype(BF16), (dv * gelu_dv).astype(BF16)], axis=1)
        dz_ref[...] = dz
        n, r = _rms(x_ref[...])
        dxn, dgp = _rms_bwd(_dot(dz, win_ref[...]), n, r, g_ref[...])
        dg_ref[...] += _colsum8(dgp)
        dx0_ref[...] = dx1_ref[...] + dxn

    pair = (SGU_HEADS // 2, CHUNK, 2 * CHUNK)
    return pl.pallas_call(
        body, grid=(nt,),
        in_specs=[rev(D_SSM), rev(D_POOL), rev(D_SGU), rev(D_POOL), rev(2 * D_SGU), rev(D_MODEL), rev(D_MODEL),
                  _const((1, D_POOL)), _const((D_POOL, D_POOL)), _const((D_POOL, D_POOL)), _const((1, D_POOL)),
                  _const((1, D_SGU)), _const((1, D_SGU)), _const(pair), _const(pair), _const((CHUNK, D_SGU)),
                  _const((D_IN, D_MODEL)), _const((1, D_MODEL))],
        out_specs=[rev(D_MODEL), rev(D_IN), _acc((D_POOL, D_POOL)), _acc((SUBLANES, D_POOL)),
                   _acc((SGU_HEADS, CHUNK, CHUNK)), _acc((CHUNK, D_SGU)), _acc((SUBLANES, D_SGU)),
                   _acc((SUBLANES, D_SGU)), _acc((SUBLANES, D_MODEL))],
        out_shape=[jax.ShapeDtypeStruct((s, D_MODEL), F32), jax.ShapeDtypeStruct((s, D_IN), BF16),
                   jax.ShapeDtypeStruct((D_POOL, D_POOL), F32), jax.ShapeDtypeStruct((SUBLANES, D_POOL), F32),
                   jax.ShapeDtypeStruct((SGU_HEADS, CHUNK, CHUNK), F32), jax.ShapeDtypeStruct((CHUNK, D_SGU), F32),
                   jax.ShapeDtypeStruct((SUBLANES, D_SGU), F32), jax.ShapeDtypeStruct((SUBLANES, D_SGU), F32),
                   jax.ShapeDtypeStruct((SUBLANES, D_MODEL), F32)],
        scratch_shapes=[pltpu.VMEM((POOL_ROWS, D_POOL), F32)] * 4,
        name=f"mix_bwd_{tag}", compiler_params=_cp(dimension_semantics=("arbitrary",)),
    )(dza, db, dc, pooled, zuv, x0, dx1, _pool_consts(), p["w_pool_bd"], p["w_pool_bd_t"], p["pool_scale"],
      p["sgu_ln_g"], p["sgu_ln_b"], p["ws_pair"], p["ws_pair_t"], p["bias_sp"], p["w_in"], p["g_mix"])


def _atb(a, b, tag, token=None):
    s, ka = a.shape
    kb = b.shape[1]
    ts = ATB_ROWS
    tn = min(kb, ATB_COLS)
    ns = s // ts
    after = [] if token is None else [token]

    def body(a_ref, b_ref, *rest):
        o_ref = rest[-1]

        @pl.when(pl.program_id(1) == 0)
        def _():
            o_ref[...] = jnp.zeros_like(o_ref)

        o_ref[...] += _dot_tn(a_ref[...].astype(BF16), b_ref[...].astype(BF16))

    return pl.pallas_call(
        body, grid=(kb // tn, ns),
        in_specs=[pl.BlockSpec((ts, ka), lambda j, i: (i, 0)), pl.BlockSpec((ts, tn), lambda j, i: (i, j))]
        + [pl.BlockSpec(memory_space=pl.ANY)] * len(after),
        out_specs=pl.BlockSpec((ka, tn), lambda j, i: (0, j)),
        out_shape=jax.ShapeDtypeStruct((ka, kb), F32),
        name=f"atb_{tag}", compiler_params=_cp(dimension_semantics=("arbitrary", "arbitrary")),
    )(a, b, *after)


def _s5_discretise(a_re, a_im, log_dt, b_re, b_im):
    dt = jnp.exp(log_dt)[:, None]
    mag = jnp.exp(a_re * dt)
    ar = mag * jnp.cos(a_im * dt)
    ai = mag * jnp.sin(a_im * dt)
    den = a_re * a_re + a_im * a_im
    f_re = ((ar - 1.0) * a_re + ai * a_im) / den
    f_im = (ai * a_re - (ar - 1.0) * a_im) / den
    bb_re = f_re[..., None] * b_re - f_im[..., None] * b_im
    bb_im = f_re[..., None] * b_im + f_im[..., None] * b_re
    return ar, ai, bb_re, bb_im


def _block_diag(blocks):
    g, r, c = blocks.shape
    eye = jnp.eye(g, dtype=blocks.dtype)
    return (blocks[:, :, None, :] * eye[:, None, :, None]).reshape(g * r, g * c)


def _block_diag_extract(m, g):
    r = m.shape[0] // g
    c = m.shape[1] // g
    eye = jnp.eye(g, dtype=m.dtype)
    return jnp.sum(m.reshape(g, r, g, c) * eye[:, None, :, None], axis=2)


GROUPS_PER_SLAB = N_GROUPS // N_USLAB


def _slab_diag(blocks):
    k = GROUPS_PER_SLAB
    _, r, c = blocks.shape
    eye = jnp.eye(k, dtype=blocks.dtype)
    spread = blocks.reshape(N_USLAB, k, r, 1, c) * eye[None, :, None, :, None]
    return spread.reshape(N_USLAB, k * r, k * c)


def _slab_diag_extract(m):
    k = GROUPS_PER_SLAB
    r, c = m.shape[1] // k, m.shape[2] // k
    eye = jnp.eye(k, dtype=m.dtype)
    return jnp.sum(m.reshape(N_USLAB, k, r, k, c) * eye[None, :, None, :, None], axis=3).reshape(N_GROUPS, r, c)


def _state_slabs(v):
    return jnp.broadcast_to(v.reshape(1, N_STATE), STATE_TILE)


def _tril():
    return jnp.tril(jnp.ones((CHUNK, CHUNK), dtype=bool))


def _layer_params(w, l):
    row = lambda v: v.reshape(1, -1)
    t = lambda m: jnp.swapaxes(m, -1, -2)
    ar, ai, bb_re, bb_im = _s5_discretise(w["A_re"][l], w["A_im"][l], w["log_dt"][l], w["B_re"][l], w["B_im"][l])
    bbt3 = jnp.concatenate([_slab_diag(t(bb_re)), _slab_diag(t(bb_im))], axis=2).astype(BF16)
    ct3 = jnp.concatenate([_slab_diag(t(w["C_re"][l])), -_slab_diag(t(w["C_im"][l]))], axis=1).astype(BF16)
    ws = jnp.where(_tril()[None], w["w_spatial"][l], 0.0)
    pair = lambda m: jnp.stack([jnp.concatenate([m[2 * q], m[2 * q + 1]], axis=1)
                                for q in range(SGU_HEADS // 2)]).astype(BF16)
    wp = _block_diag(w["w_pool"][l]).astype(BF16)
    p = dict(
        g_mix=row(w["g_mix"][l]), g_ffn=row(w["g_ffn"][l]), d_skip=row(w["D_skip"][l]), b_glu=row(w["b_glu"][l]),
        pool_scale=row(w["pool_scale"][l]), sgu_ln_g=row(w["sgu_ln_g"][l]), sgu_ln_b=row(w["sgu_ln_b"][l]),
        a_re8=_state_slabs(ar), a_im8=_state_slabs(ai),
        bbt3=bbt3, bb3=t(bbt3), ct3=ct3, cb3=t(ct3),
        w_pool_bd=wp, w_pool_bd_t=t(wp), ws_pair=pair(ws), ws_pair_t=pair(t(ws)),
        bias_sp=jnp.repeat(t(w["b_spatial"][l]), SGU_HEAD_DIM, axis=1),
    )
    return p


MIX_WEIGHTS = ("w_in", "w_glu")
FFN_WEIGHTS = ("w_out", "w_gate", "w_up", "w_down")


def _with_big(p, mats):
    p.update(mats)


def _rows_sum(v):
    return jnp.sum(v, axis=0)


ATB_COLS = 1024
ATB_ROWS = 1024


def _after(v, token):
    return v if token is None else v + token[0, 0]


def _join(a, b):
    return b if a is None else a if b is None else a + b


def _layer_bwd(dx2, sv, p, w, l, tag, hooks, token):
    t = lambda m: jnp.swapaxes(m, -1, -2)
    dx1, da, db, dc, dgt, dup, act, dg_ffn = _blk_bwd(dx2, sv["x1"], sv["gt"], sv["up"],
                                                      dict(p, g_ffn=_after(p["g_ffn"], token)), tag)
    token = hooks["tick"]([dx1])
    token = _join(token, hooks["on_grads"](l, "ffn", {
        "w_down": _atb(act, dx2, tag + "_wd", token), "w_gate": _atb(dgt, sv["h2"], tag + "_wg", token),
        "w_up": _atb(dup, sv["h2"], tag + "_wu", token), "w_out": _atb(sv["ycat"], dx1, tag + "_wo", token)}))
    g = {}
    g["g_ffn"] = _rows_sum(dg_ffn)
    dza, dc3, dbbt3, dar8, dai8, dd8, dwglu, dbglu8 = _s5_bwd(
        da, sv["za"], sv["y"], sv["h_re"], sv["h_im"], dict(p, d_skip=_after(p["d_skip"], token)), tag)
    token = hooks["tick"]([dza])
    dx0, dz, dwp, dsc8, dws, dbias, dlng8, dlnb8, dg_mix = _mix_bwd(
        dza, db, dc, sv["pooled"], sv["zuv"], sv["x0"], dx1, dict(p, pool_scale=_after(p["pool_scale"], token)), tag)
    g["g_mix"] = _rows_sum(dg_mix)
    g["b_glu"] = _rows_sum(dbglu8)
    g["D_skip"] = _rows_sum(dd8)
    half = N_STATE // N_USLAB
    g["C_re"] = _slab_diag_extract(dc3[:, :, :half])
    g["C_im"] = -_slab_diag_extract(dc3[:, :, half:])
    dar = jnp.sum(dar8, axis=0).reshape(N_GROUPS, SSM_STATE)
    dai = jnp.sum(dai8, axis=0).reshape(N_GROUPS, SSM_STATE)
    dbb_re = t(_slab_diag_extract(dbbt3[:, :, :half]))
    dbb_im = t(_slab_diag_extract(dbbt3[:, :, half:]))
    _, disc_vjp = jax.vjp(_s5_discretise, w["A_re"][l], w["A_im"][l], w["log_dt"][l], w["B_re"][l], w["B_im"][l])
    g["A_re"], g["A_im"], g["log_dt"], g["B_re"], g["B_im"] = disc_vjp((dar, dai, dbb_re, dbb_im))
    g["w_pool"] = _block_diag_extract(dwp, len(POOL_WINDOWS))
    g["pool_scale"] = _rows_sum(dsc8)
    g["sgu_ln_g"] = _rows_sum(dlng8)
    g["sgu_ln_b"] = _rows_sum(dlnb8)
    g["w_spatial"] = jnp.where(_tril()[None], dws, 0.0)
    g["b_spatial"] = t(jnp.sum(dbias.reshape(CHUNK, SGU_HEADS, SGU_HEAD_DIM), axis=-1))
    token = hooks["on_small"](l, g)
    token = hooks["on_grads"](l, "mix", {"w_in": _atb(dz, sv["h1"], tag + "_wi", token), "w_glu": dwglu})
    return dx0, token


def _local_step(x, target, w, hooks):
    params = [_layer_params(w, l) for l in range(DEPTH)]
    saved = []
    h = x
    for l in range(DEPTH):
        p, tag = params[l], f"l{l}"
        _with_big(p, hooks["get_big"](l, "mix", [h]))
        za, zuv, h1, ob, pooled, oc = _mix_fwd(h, p, tag)
        oa, y, h_re, h_im = _s5_fwd(za, p, tag)
        _with_big(p, hooks["get_big"](l, "ffn", [oa, ob, oc]))
        head = (target, w["g_final"].reshape(1, -1)) if l == DEPTH - 1 else None
        x1, x2, h2, gt, up, ycat, *loss_parts = _blk_fwd(h, oa, ob, oc, p, tag, head)
        saved.append(dict(x0=h, za=za, zuv=zuv, h1=h1, ycat=ycat, y=y, h_re=h_re, h_im=h_im, pooled=pooled, x1=x1,
                          h2=h2, gt=gt, up=up))
        h = x2
    dx = h
    loss8, dgf8 = loss_parts
    grads = [None] * DEPTH

    def on_small(l, g_l):
        grads[l] = g_l
        if l > 0:
            return None
        g = {n: jnp.stack([grads[k][n] for k in range(DEPTH)]) for n in SMALL if n != "g_final"}
        g["g_final"] = _rows_sum(dgf8)
        return hooks["on_small"](g, loss8[0, 0])

    token = None
    for l in reversed(range(DEPTH)):
        dx, token = _layer_bwd(dx, saved[l], params[l], w, l, f"l{l}", dict(hooks, on_small=on_small), token)
    return dx


_ANY = pl.BlockSpec(memory_space=pl.ANY)
_MESH = pl.DeviceIdType.MESH


def _place():
    return lax.axis_index("x"), lax.axis_index("y"), lax.axis_index("c")


def _other_chips(x, y):
    return [(1 - x, y), (x, 1 - y), (1 - x, 1 - y)]


def _dma_sems(n):
    return pltpu.SemaphoreType.DMA((n,))


def _remote(src, dst, send_sems, recv_sems, k, to):
    return pltpu.make_async_remote_copy(src_ref=src, dst_ref=dst, send_sem=send_sems.at[k], recv_sem=recv_sems.at[k],
                                        device_id=to, device_id_type=_MESH)


_HBM = pl.BlockSpec(memory_space=pltpu.HBM)
_SEM = pl.BlockSpec(memory_space=pltpu.SEMAPHORE)
_EFFECT = pltpu.SideEffectType.DATAFLOW_SIDE_EFFECTING
N_REL = N_CHIPS - 1


def _gather_plan(x, y, c, srcs, lands):
    plan = []
    for l in lands:
        r = l.shape[0] // N_CHIPS
        rows = l.at[pl.ds((2 * x + y) * r, r)]
        plan += [(rows, rows, (cx, cy, c)) for cx, cy in _other_chips(x, y)]
    return plan


def _half_rows(land, chip, c):
    h = land.shape[0] // (2 * N_CHIPS)
    return land.at[pl.ds((2 * chip + c) * h, h)]


def _gather_half_plan(x, y, c, srcs, lands):
    return [(_half_rows(l, 2 * x + y, c), _half_rows(l, 2 * x + y, c), (cx, cy, c))
            for l in lands for cx, cy in _other_chips(x, y)]


def _forward_halves(lands, tag):
    nw = len(lands)

    def body(*refs):
        ins = refs[:nw]
        send_sems, recv_sems = refs[2 * nw:]
        x, y, c = _place()
        chips = [2 * cx + cy for cx, cy in _other_chips(x, y)]
        sends = [_remote(_half_rows(ins[i], k, c), _half_rows(ins[i], k, c), send_sems, recv_sems, N_REL * i + j,
                         (x, y, 1 - c)) for i in range(nw) for j, k in enumerate(chips)]
        for cp in sends:
            cp.start()
        for i in range(nw):
            for j, k in enumerate(chips):
                sends[N_REL * i + j].wait_send()
                _remote(_half_rows(ins[i], k, c), _half_rows(ins[i], k, 1 - c), send_sems, recv_sems, N_REL * i + j,
                        (x, y, 1 - c)).wait_recv()

    return pl.pallas_call(
        body, out_shape=[jax.ShapeDtypeStruct(a.shape, a.dtype) for a in lands], in_specs=[_ANY] * nw,
        out_specs=[_ANY] * nw, input_output_aliases={i: i for i in range(nw)},
        scratch_shapes=[_dma_sems(N_REL * nw), _dma_sems(N_REL * nw)], name=f"forward_halves_{tag}",
    )(*lands)


def _sibling_plan(x, y, c, srcs, lands):
    return [(s.at[:, 1 - c], l, (x, y, 1 - c)) for s, l in zip(srcs, lands)]


def _slab_plan(x, y, c, srcs, lands):
    return [(s.at[2 * cx + cy], l.at[j], (cx, cy, c))
            for s, l in zip(srcs, lands) for j, (cx, cy) in enumerate(_other_chips(x, y))]


def _plan_copies(plan, srcs, lands, send_sems, recv_sems):
    x, y, c = _place()
    return [_remote(s, d, send_sems, recv_sems, k, to) for k, (s, d, to) in enumerate(plan(x, y, c, srcs, lands))]


def _hbm(a):
    return pltpu.with_memory_space_constraint(a, pltpu.HBM)


def _everyone_plan(x, y, c, srcs, lands):
    me = 4 * x + 2 * y + c
    peers = [(x, y, 1 - c)] + [(cx, cy, cc) for cx, cy in _other_chips(x, y) for cc in (c, 1 - c)]
    return [(s, l.at[me], peer) for s, l in zip(srcs, lands) for peer in peers]


def _copies_start(name, plan, srcs, lands, ncopies):
    ns, n = len(srcs), len(srcs) + len(lands)

    def body(*refs):
        for cp in _plan_copies(plan, refs[:ns], refs[ns:n], refs[n], refs[n + 1]):
            cp.start()
        refs[-1][...] = jnp.zeros_like(refs[-1])

    ref_out = [pltpu.HBM(a.shape, a.dtype) for a in (*srcs, *lands)]
    out = pl.pallas_call(
        body, name=name, in_specs=[_HBM] * n,
        out_shape=(_dma_sems(ncopies), _dma_sems(ncopies), *ref_out, jax.ShapeDtypeStruct((SUBLANES, LANES), F32)),
        out_specs=(_SEM, _SEM, *[_HBM] * n, pl.BlockSpec(memory_space=pltpu.VMEM)),
        input_output_aliases={i: 2 + i for i in range(n)},
        compiler_params=pltpu.CompilerParams(has_side_effects=_EFFECT),
    )(*[_hbm(a) for a in (*srcs, *lands)])
    return dict(name=name, plan=plan, sems=out[:2], srcs=out[2:2 + ns], lands=out[2 + ns:2 + n], token=out[-1])


def _copies_wait(started, after):
    ns = len(started["srcs"])
    n = ns + len(started["lands"])
    plan = started["plan"]

    def body(*refs):
        for cp in _plan_copies(plan, refs[:ns], refs[ns:n], refs[n], refs[n + 1]):
            cp.wait_send()
            cp.wait_recv()

    args = (*started["srcs"], *started["lands"])
    out = pl.pallas_call(
        body, name=started["name"] + "_wait", out_shape=[pltpu.HBM(a.shape, a.dtype) for a in args],
        in_specs=[_HBM] * n + [_SEM, _SEM] + [_ANY] * len(after), out_specs=[_HBM] * n,
        input_output_aliases={i: i for i in range(n)},
        compiler_params=pltpu.CompilerParams(has_side_effects=_EFFECT),
    )(*args, *started["sems"], *after)
    return out[:ns], out[ns:]


def _place_shards(ws, layer, sel, after, tag):
    nw = len(ws)

    def body(sel_ref, *refs):
        for i in range(nw):
            refs[nw + len(after) + i][...] = refs[i][...].astype(BF16)

    return pl.pallas_call(
        body, grid_spec=pltpu.PrefetchScalarGridSpec(
            num_scalar_prefetch=1, grid=(1,),
            in_specs=[pl.BlockSpec((None,) + a.shape[1:], lambda i, s: (layer, 0, 0)) for a in ws] + [_ANY] * len(after),
            out_specs=[pl.BlockSpec(a.shape[1:], lambda i, s: (s[1], 0)) for a in ws]),
        out_shape=[jax.ShapeDtypeStruct((N_CHIPS * a.shape[1], a.shape[2]), BF16) for a in ws],
        name=f"place_shards_{tag}", compiler_params=_cp(dimension_semantics=("arbitrary",)),
    )(sel, *ws, *after)


def _share_halves(fs, layer, tag):
    nw = len(fs)

    def body(*refs):
        ins = refs[:nw]
        send_sems, recv_sems = refs[2 * nw:]
        x, y, c = _place()

        def half(i, who):
            h = ins[i].shape[1] // 2
            return ins[i].at[layer, pl.ds(who * h, h)]

        sends = [_remote(half(i, c), half(i, c), send_sems, recv_sems, i, (x, y, 1 - c)) for i in range(nw)]
        for cp in sends:
            cp.start()
        for i in range(nw):
            sends[i].wait_send()
            _remote(half(i, c), half(i, 1 - c), send_sems, recv_sems, i, (x, y, 1 - c)).wait_recv()

    return pl.pallas_call(
        body, out_shape=[jax.ShapeDtypeStruct(f.shape, f.dtype) for f in fs], in_specs=[_ANY] * nw,
        out_specs=[_ANY] * nw, input_output_aliases={i: i for i in range(nw)},
        scratch_shapes=[_dma_sems(nw), _dma_sems(nw)], name=f"share_halves_{tag}",
    )(*fs)


def _add_halves(g4s, recvs, sel, tag):
    nw = len(g4s)

    def body(sel_ref, *refs):
        for i in range(nw):
            refs[2 * nw + i][...] = (refs[i][...] + refs[nw + i][...]).astype(BF16)

    mine = [pl.BlockSpec((None, None) + g.shape[2:], lambda k, s: (k, s[0], 0, 0)) for g in g4s]
    slab = [pl.BlockSpec((None,) + g.shape[2:], lambda k, s: (k, 0, 0)) for g in g4s]
    return pl.pallas_call(
        body, grid_spec=pltpu.PrefetchScalarGridSpec(num_scalar_prefetch=1, grid=(N_CHIPS,), in_specs=mine + slab,
                                                     out_specs=slab),
        out_shape=[jax.ShapeDtypeStruct(r.shape, BF16) for r in recvs], name=f"add_halves_{tag}",
        compiler_params=_cp(dimension_semantics=("arbitrary",)),
    )(sel, *g4s, *recvs)


def _add_chips(ps, slabs, fs, layer, sel, tag):
    nw = len(ps)
    old = [f for f in fs if f is not None]

    def body(sel_ref, *refs):
        outs = refs[2 * nw + len(old):]
        for i in range(nw):
            acc = refs[i][...].astype(F32)
            for j in range(N_REL):
                acc = acc + refs[nw + i][j].astype(F32)
            outs[i][...] = acc

    shapes = [(DEPTH, 2 * p.shape[1], p.shape[2]) for p in ps]
    in_specs = [pl.BlockSpec((None,) + p.shape[1:], lambda i, s: (s[1], 0, 0)) for p in ps]
    in_specs += [pl.BlockSpec(sl.shape, lambda i, s: (0, 0, 0)) for sl in slabs]
    in_specs += [_ANY] * len(old)
    first_old = 1 + 2 * nw
    aliases, k = {}, 0
    for i, f in enumerate(fs):
        if f is not None:
            aliases[first_old + k] = i
            k += 1
    return pl.pallas_call(
        body, grid_spec=pltpu.PrefetchScalarGridSpec(
            num_scalar_prefetch=1, grid=(1,), in_specs=in_specs,
            out_specs=[pl.BlockSpec((None,) + p.shape[1:], lambda i, s: (layer, s[0], 0)) for p in ps]),
        out_shape=[jax.ShapeDtypeStruct(sh, F32) for sh in shapes], input_output_aliases=aliases,
        name=f"add_chips_{tag}", compiler_params=_cp(dimension_semantics=("arbitrary",)),
    )(sel, *ps, *slabs, *old)


def _adamw_math(w, g, m, v):
    m = ADAM_B1 * m + (1.0 - ADAM_B1) * g
    v = ADAM_B2 * v + (1.0 - ADAM_B2) * (g * g)
    m_hat = m / (1.0 - ADAM_B1 ** ADAM_STEP)
    v_hat = v / (1.0 - ADAM_B2 ** ADAM_STEP)
    delta = -ADAM_LR * (m_hat / (jnp.sqrt(v_hat) + ADAM_EPS) + ADAM_WD * w)
    return delta, m, v


ADAM_ROWS = 512


def _row_tile(rows, most):
    return max(t for t in range(SUBLANES, most + 1, SUBLANES) if rows % t == 0)


def _adamw(w, g, m, v, tag, after=()):
    depth, rows, cols = w.shape
    tr = _row_tile(rows, ADAM_ROWS)

    def body(w_ref, g_ref, m_ref, v_ref, *rest):
        d_ref, nm_ref, nv_ref = rest[len(after):]
        d, nm, nv = _adamw_math(w_ref[...], g_ref[...], m_ref[...], v_ref[...])
        d_ref[...] = d
        nm_ref[...] = nm
        nv_ref[...] = nv

    spec = pl.BlockSpec((None, tr, cols), lambda l, i: (l, i, 0))
    return pl.pallas_call(
        body, grid=(depth, rows // tr), in_specs=[spec] * 4 + [_ANY] * len(after), out_specs=[spec] * 3,
        out_shape=[jax.ShapeDtypeStruct(w.shape, F32)] * 3, name=f"adamw_{tag}",
        compiler_params=_cp(dimension_semantics=("arbitrary", "arbitrary")),
    )(w, g, m, v, *after)


SMALL_TILE = 512
PRECISE = ("g_final",)
COARSE = [n for n in SMALL if n not in PRECISE]


def _small_reduce(gathered):
    n = len(gathered)

    def body(*refs):
        for ga_ref, g_ref in zip(refs[:n], refs[n:]):
            g = ga_ref[0].astype(F32)
            for k in range(1, N_DEV):
                g = g + ga_ref[k].astype(F32)
            g_ref[...] = g

    return pl.pallas_call(
        body, out_shape=[jax.ShapeDtypeStruct(b.shape[1:], F32) for b in gathered], name="small_reduce",
        compiler_params=_cp(),
    )(*gathered)


def _small_adamw(g, w, m, v):
    names = list(g)
    n = len(names)
    shapes = {k: g[k].shape if g[k].ndim > 1 else (1,) + g[k].shape for k in names}

    def body(*refs):
        ins, outs = refs[:4 * n], refs[4 * n:]
        for i in range(n):
            d, nm, nv = _adamw_math(ins[n + i][...], ins[i][...], ins[2 * n + i][...], ins[3 * n + i][...])
            outs[i][...] = d
            outs[n + i][...] = nm
            outs[2 * n + i][...] = nv

    out = pl.pallas_call(
        body, out_shape=[jax.ShapeDtypeStruct(shapes[k], F32) for _ in range(3) for k in names], name="adamw_small",
        compiler_params=_cp(),
    )(*[src[k].reshape(shapes[k]) for src in (g, w, m, v) for k in names])
    return [{k: out[j * n + i].reshape(g[k].shape) for i, k in enumerate(names)} for j in range(3)]


def _exchange_form(n, a):
    return jnp.swapaxes(a, 1, 2) if n in TRANSPOSED else a


PACK_ROWS = 16


def _rows_of(size):
    return -(-size // (LANES * PACK_ROWS)) * PACK_ROWS


SMALL_VIEW = {"B_re": (0, 1, 3, 2), "B_im": (0, 1, 3, 2), "b_spatial": (1, 0, 2)}
assert all(tuple(order[i] for i in order) == tuple(range(len(order))) for order in SMALL_VIEW.values())


def _view(n, a):
    return jnp.transpose(a, SMALL_VIEW[n]) if n in SMALL_VIEW else a


def _pack(vals, names, extra=None):
    parts = [_view(n, vals[n]).reshape(-1) for n in names] + ([] if extra is None else [extra.reshape(1)])
    tiles = [jnp.pad(a, (0, _rows_of(a.size) * LANES - a.size)).reshape(-1, LANES) for a in parts]
    rows = sum(t.shape[0] for t in tiles)
    if rows > SMALL_TILE:
        tiles.append(jnp.zeros((-rows % SMALL_TILE, LANES), tiles[0].dtype))
    return jnp.concatenate(tiles, axis=0)


def _unpack(buf, like, names):
    out, row = {}, 0
    for n in names:
        rows = _rows_of(like[n].size)
        shape = tuple(like[n].shape[i] for i in SMALL_VIEW.get(n, range(like[n].ndim)))
        out[n] = buf[row:row + rows].reshape(-1)[:like[n].size].reshape(shape)
        row += rows
    return out, buf[row:]


def kernel(x, g_mix, w_in, A_re, A_im, log_dt, B_re, B_im, C_re, C_im, D_skip, w_glu, b_glu, w_pool, pool_scale, sgu_ln_g, sgu_ln_b, w_spatial, b_spatial, w_out, g_ffn, w_gate, w_up, w_down, g_final, loss_target, m_g_mix, m_w_in, m_A_re, m_A_im, m_log_dt, m_B_re, m_B_im, m_C_re, m_C_im, m_D_skip, m_w_glu, m_b_glu, m_w_pool, m_pool_scale, m_sgu_ln_g, m_sgu_ln_b, m_w_spatial, m_b_spatial, m_w_out, m_g_ffn, m_w_gate, m_w_up, m_w_down, m_g_final, v_g_mix, v_w_in, v_A_re, v_A_im, v_log_dt, v_B_re, v_B_im, v_C_re, v_C_im, v_D_skip, v_w_glu, v_b_glu, v_w_pool, v_pool_scale, v_sgu_ln_g, v_sgu_ln_b, v_w_spatial, v_b_spatial, v_w_out, v_g_ffn, v_w_gate, v_w_up, v_w_down, v_g_final):
    loc = locals()
    w = {n: loc[n] for n in WEIGHTS}
    m = {n: loc["m_" + n] for n in WEIGHTS}
    v = {n: loc["v_" + n] for n in WEIGHTS}
    sel = jnp.stack([lax.axis_index("c"), 2 * lax.axis_index("x") + lax.axis_index("y")]).astype(jnp.int32)

    chip = sel[1]

    halves = [(l, half) for l in range(DEPTH) for half in ("mix", "ffn")]
    two_level = {(0, "ffn")}
    names = {"mix": MIX_WEIGHTS, "ffn": FFN_WEIGHTS}
    started = {}
    wx = {n: _exchange_form(n, w[n]) for n in BIG}
    chain = []
    for l, half in halves:
        lands = _place_shards([wx[n] for n in names[half]], l, sel, chain, f"l{l}_{half}")
        plan = _gather_half_plan if (l, half) in two_level else _gather_plan
        started[l, half] = _copies_start(f"weights_l{l}_{half}", plan, [], lands, N_REL * len(lands))
        chain = [started[l, half]["token"]]
    w = dict(w, g_mix=_after(w["g_mix"], started[halves[-1]]["token"]))

    def get_big(l, half, after):
        lands = _copies_wait(started[l, half], after)[1]
        if (l, half) in two_level:
            lands = _forward_halves(lands, f"l{l}_{half}")
        return dict(zip(names[half], lands))

    result = {n: None for n in BIG}
    stage = {"swap": None, "slabs": None}

    def advance(after):
        if stage["slabs"] is not None:
            ex, ns, l, tag = stage["slabs"]
            part, slabs = _copies_wait(ex, after)
            bufs = _add_chips(part, slabs, [result[n] for n in ns], l, sel, tag)
            for n, f in zip(ns, _share_halves(bufs, l, tag)):
                result[n] = f
            stage["slabs"] = None
        if stage["swap"] is None:
            return None
        sw, ns, l, tag = stage["swap"]
        part = _add_halves(*_copies_wait(sw, after), sel, tag)
        slabs = [lax.empty((N_REL,) + p.shape[1:], BF16) for p in part]
        ex = _copies_start(f"grads_{tag}", _slab_plan, part, slabs, N_REL * len(part))
        stage["slabs"], stage["swap"] = (ex, ns, l, tag), None
        return ex["token"]

    def on_grads(l, half, grads):
        ns = list(grads)
        tag = f"l{l}_{half}"
        token = advance([grads[ns[0]]])
        g4s = [grads[n].reshape(N_CHIPS, 2, grads[n].shape[0] // (2 * N_CHIPS), grads[n].shape[1]) for n in ns]
        recvs = [lax.empty((N_CHIPS,) + g4.shape[2:], F32) for g4 in g4s]
        sw = _copies_start(f"swap_{tag}", _sibling_plan, g4s, recvs, len(g4s))
        stage["swap"] = (sw, ns, l, tag)
        return _join(token, sw["token"])

    small = {}

    def on_small(g, loss_local):
        me = 2 * chip + sel[0]
        blocks = [_pack(g, COARSE).astype(BF16), _pack(g, PRECISE, loss_local)]
        lands = [lax.dynamic_update_slice(lax.empty((N_DEV,) + b.shape, b.dtype), b[None], (me, 0, 0)) for b in blocks]
        small.update(_copies_start("small_grads", _everyone_plan, blocks, lands, (N_DEV - 1) * len(blocks)))
        return small["token"]

    dx = _local_step(x[0], loss_target[0], w, dict(get_big=get_big, on_grads=on_grads, tick=advance, on_small=on_small))
    grads, deltas, new_m, new_v = {}, {}, {}, {}

    def update_big(ns, after):
        for n in ns:
            outs = _adamw(wx[n], result[n], _exchange_form(n, m[n]), _exchange_form(n, v[n]), n, after)
            grads[n], deltas[n], new_m[n], new_v[n] = [_exchange_form(n, a) for a in (result[n], *outs)]
            after = [outs[-1]]
        return after

    last_sent = advance([])
    advance(update_big(FFN_WEIGHTS, [last_sent]))
    update_big(MIX_WEIGHTS, [])

    _, gathered = _copies_wait(small, [new_v[n] for n in BIG])
    coarse, precise = _small_reduce(gathered)
    small_g, _ = _unpack(coarse, w, COARSE)
    precise_g, rest = _unpack(precise, w, PRECISE)
    small_g.update(precise_g)
    loss = rest[0, 0]
    views = [{n: _view(n, src[n]) for n in small_g} for src in (w, m, v)]
    for store, vals in zip((grads, deltas, new_m, new_v), (small_g, *_small_adamw(small_g, *views))):
        store.update({n: _view(n, a) for n, a in vals.items()})
    return (loss, dx[None], *[grads[n] for n in WEIGHTS], *[deltas[n] for n in WEIGHTS],
            *[new_m[n] for n in WEIGHTS], *[new_v[n] for n in WEIGHTS])
```

```python
import math

import jax
import jax.numpy as jnp
from jax import lax
from jax.experimental import pallas as pl
from jax.experimental.pallas import tpu as pltpu

F32 = jnp.float32
BF16 = jnp.bfloat16

D_MODEL = 1024
DEPTH = 2
D_SSM = 384
SSM_GROUP = 16
N_GROUPS = 24
SSM_STATE = 64
N_STATE = N_GROUPS * SSM_STATE
POOL_WINDOWS = (2, 4, 8, 16)
POOL_GROUP = 64
D_POOL = 256
MAX_WINDOW = 16
SGU_HEADS = 6
SGU_HEAD_DIM = 64
D_SGU = 384
CHUNK = 128
D_IN = D_SSM + D_POOL + 2 * D_SGU
D_FF = 2816
EPS = 1e-6

ADAM_LR = 0.001
ADAM_B1 = 0.9
ADAM_B2 = 0.999
ADAM_EPS = 1e-08
ADAM_WD = 0.01
ADAM_STEP = 10

LANES = 128
SUBLANES = 8
VMEM_LIMIT = 56 * 1024 * 1024

TS = 512
TS_FFN = 256

WEIGHTS = ['g_mix', 'w_in', 'A_re', 'A_im', 'log_dt', 'B_re', 'B_im', 'C_re', 'C_im', 'D_skip', 'w_glu', 'b_glu',
           'w_pool', 'pool_scale', 'sgu_ln_g', 'sgu_ln_b', 'w_spatial', 'b_spatial', 'w_out', 'g_ffn', 'w_gate',
           'w_up', 'w_down', 'g_final']
BIG = ['w_in', 'w_glu', 'w_out', 'w_gate', 'w_up', 'w_down']
SMALL = [n for n in WEIGHTS if n not in BIG]
TRANSPOSED = ("w_in", "w_gate", "w_up")
N_CHIPS = 4
N_DEV = 8


def _cp(**kw):
    return pltpu.CompilerParams(vmem_limit_bytes=VMEM_LIMIT, **kw)


def _row(ts, n):
    return pl.BlockSpec((ts, n), lambda i: (i, 0))


def _const(shape):
    nd = len(shape)
    return pl.BlockSpec(shape, lambda i: (0,) * nd, pipeline_mode=pl.Buffered(1))


def _acc(shape):
    nd = len(shape)
    return pl.BlockSpec(shape, lambda i: (0,) * nd)


def _following(body, after):
    k = len(after)
    return lambda *refs: body(*refs[k:])


def _anywhere(after):
    return [pl.BlockSpec(memory_space=pl.ANY)] * len(after)


def _dot(a, b):
    return jnp.dot(a, b, preferred_element_type=F32)


def _dot_tn(a, b):
    return lax.dot_general(a, b, (((0,), (0,)), ((), ())), preferred_element_type=F32)


def _dot_nt(a, b):
    return lax.dot_general(a, b, (((1,), (1,)), ((), ())), preferred_element_type=F32)


_G0 = math.sqrt(2.0 / math.pi)
_G1 = 0.044715


def _gelu(x):
    return 0.5 * x * (1.0 + jnp.tanh(_G0 * (x + _G1 * x * x * x)))


def _gelu_and_grad(x):
    t = jnp.tanh(_G0 * (x + _G1 * x * x * x))
    half = 0.5 * (1.0 + t)
    return x * half, half + 0.5 * x * (1.0 - t * t) * (_G0 * (1.0 + 3.0 * _G1 * x * x))


def _sigmoid(x):
    return 1.0 / (1.0 + jnp.exp(-x))


def _rms(x):
    r = lax.rsqrt(jnp.mean(x * x, axis=-1, keepdims=True) + EPS)
    return x * r, r


def _rms_bwd(dh, n, r, g):
    dn = dh * g
    return r * (dn - n * jnp.mean(dn * n, axis=-1, keepdims=True)), dh * n


def _colsum8(v):
    rows, n = v.shape
    return jnp.sum(v.reshape(rows // SUBLANES, SUBLANES, n), axis=0)


def _cmul(ar, ai, br, bi):
    return ar * br - ai * bi, ar * bi + ai * br


def _cpow(ar, ai, n):
    assert n & (n - 1) == 0
    while n > 1:
        ar, ai = _cmul(ar, ai, ar, ai)
        n //= 2
    return ar, ai


N_USLAB = D_SSM // LANES
SEG = TS // SUBLANES
SLAB_STATES = N_STATE // N_USLAB
S5_IN = (N_USLAB, LANES, 2 * SLAB_STATES)
S5_OUT = (N_USLAB, 2 * SLAB_STATES, LANES)
STATE_TILE = (SUBLANES, N_STATE)


def _scan_order():
    p = jnp.arange(TS)
    src = (p % SUBLANES) * SEG + p // SUBLANES
    return (src[:, None] == jnp.arange(TS)[None, :]).astype(BF16)


def _to_scan_order(perm, v):
    hi = v.astype(BF16)
    lo = (v - hi.astype(F32)).astype(BF16)
    return _dot(perm, hi) + _dot(perm, lo)


def _scan_rows(k):
    return pl.ds(pl.multiple_of(k * SUBLANES, SUBLANES), SUBLANES)


def _lanes(v, j):
    return v[:, LANES * j:LANES * (j + 1)]


def _states(j):
    return pl.ds(SLAB_STATES * j, SLAB_STATES)


def _state_split(re_ref, im_ref, j, v):
    re_ref[:, _states(j)] = v[:, :SLAB_STATES]
    im_ref[:, _states(j)] = v[:, SLAB_STATES:]


def _state_cat(re_ref, im_ref, j):
    return jnp.concatenate([re_ref[:, _states(j)], im_ref[:, _states(j)]], axis=1).astype(BF16)


def _s5_fwd(u, p, tag):
    s = u.shape[0]
    seg = SEG

    def body(u_ref, perm_ref, bbt_ref, ar_ref, ai_ref, ct_ref, dsk_ref, wglu_ref, bglu_ref,
             oa_ref, y_ref, hr_ref, hi_ref, sr, si, er, ei, ir, ii, cr, ci):
        @pl.when(pl.program_id(0) == 0)
        def _():
            cr[...] = jnp.zeros_like(cr)
            ci[...] = jnp.zeros_like(ci)

        perm = perm_ref[...]
        uv = _to_scan_order(perm, u_ref[...])
        ub = uv.astype(BF16)
        for j in range(N_USLAB):
            _state_split(sr, si, j, _dot(_lanes(ub, j), bbt_ref[j]))
        for j in range(N_USLAB):
            cols = _states(j)
            ar = ar_ref[:, cols]
            ai = ai_ref[:, cols]
            h_r = h_i = jnp.zeros((SUBLANES, SLAB_STATES), F32)
            for k in range(seg):
                rows = pl.ds(SUBLANES * k, SUBLANES)
                n_r, n_i = _cmul(ar, ai, h_r, h_i)
                h_r = n_r + sr[rows, cols]
                h_i = n_i + si[rows, cols]
            er[:, cols] = h_r
            ei[:, cols] = h_i
            pr, pi = _cpow(ar[0:1, :], ai[0:1, :], seg)
            c_r = cr[:, cols]
            c_i = ci[:, cols]
            for q in range(SUBLANES):
                ir[q:q + 1, cols] = c_r
                ii[q:q + 1, cols] = c_i
                n_r, n_i = _cmul(pr, pi, c_r, c_i)
                c_r = n_r + er[q:q + 1, cols]
                c_i = n_i + ei[q:q + 1, cols]
            cr[:, cols] = c_r
            ci[:, cols] = c_i
            h_r = ir[:, cols]
            h_i = ii[:, cols]
            for k in range(seg):
                rows = pl.ds(SUBLANES * k, SUBLANES)
                n_r, n_i = _cmul(ar, ai, h_r, h_i)
                h_r = n_r + sr[rows, cols]
                h_i = n_i + si[rows, cols]
                sr[rows, cols] = h_r
                si[rows, cols] = h_i
        hr_ref[...] = sr[...].astype(BF16)
        hi_ref[...] = si[...].astype(BF16)
        y = jnp.concatenate([_dot(_state_cat(hr_ref, hi_ref, j), ct_ref[j]) for j in range(N_USLAB)], axis=1)
        y = y + dsk_ref[...] * uv
        y_ref[...] = y
        g = _gelu(y)
        pre = _dot(g.astype(BF16), wglu_ref[...]) + bglu_ref[...]
        oa_ref[...] = _dot_tn(perm, (g * _sigmoid(pre)).astype(BF16)).astype(BF16)

    return pl.pallas_call(
        body, grid=(s // TS,),
        in_specs=[_row(TS, D_SSM), _const((TS, TS)), _const(S5_IN), _const(STATE_TILE), _const(STATE_TILE),
                  _const(S5_OUT), _const((1, D_SSM)), _const((D_SSM, D_SSM)), _const((1, D_SSM))],
        out_specs=[_row(TS, D_SSM), _row(TS, D_SSM), _row(TS, N_STATE), _row(TS, N_STATE)],
        out_shape=[jax.ShapeDtypeStruct((s, D_SSM), BF16), jax.ShapeDtypeStruct((s, D_SSM), F32),
                   jax.ShapeDtypeStruct((s, N_STATE), BF16), jax.ShapeDtypeStruct((s, N_STATE), BF16)],
        scratch_shapes=[pltpu.VMEM((TS, N_STATE), F32), pltpu.VMEM((TS, N_STATE), F32),
                        pltpu.VMEM(STATE_TILE, F32), pltpu.VMEM(STATE_TILE, F32), pltpu.VMEM(STATE_TILE, F32),
                        pltpu.VMEM(STATE_TILE, F32), pltpu.VMEM((1, N_STATE), F32), pltpu.VMEM((1, N_STATE), F32)],
        name=f"s5_fwd_{tag}", compiler_params=_cp(dimension_semantics=("arbitrary",)),
    )(u, _scan_order(), p["bbt3"], p["a_re8"], p["a_im8"], p["ct3"], p["d_skip"], p["w_glu"], p["b_glu"])


def _pool_consts():
    w = jnp.repeat(jnp.asarray(POOL_WINDOWS, F32), POOL_GROUP)[None, :]
    return w


POOL_PAD = SUBLANES
POOL_ROWS = TS + MAX_WINDOW + POOL_PAD


def _window_sum(buf, tmp, first, wl, step):
    assert POOL_WINDOWS == (2, 4, 8, 16)
    n = TS + MAX_WINDOW
    lo = first - MAX_WINDOW if step < 0 else first
    src = buf
    for k, dst in zip((1, 2, 4), tmp):
        dst[pl.ds(lo, n), :] = src[pl.ds(lo, n), :] + src[pl.ds(lo + step * k, n), :]
        src = dst
    s2, s4, s8 = (t[pl.ds(first, TS), :] for t in tmp)
    s16 = s8 + tmp[2][pl.ds(first + step * 8, TS), :]
    return jnp.where(wl == 2, s2, jnp.where(wl == 4, s4, jnp.where(wl == 8, s8, s16)))


def _pool_count(i, rows, wl):
    t = (i * TS + 1).astype(F32) + lax.broadcasted_iota(jnp.int32, (rows, 1), 0).astype(F32)
    return jnp.minimum(t, wl)


def _sgu_mix(vl, wpair_ref, lo, hi):
    rows = vl.shape[0]
    chunks = []
    for c in range(rows // CHUNK):
        vc = vl[CHUNK * c:CHUNK * (c + 1), :]
        parts = []
        for q in range(SGU_HEADS // 2):
            vq = vc[:, LANES * q:LANES * (q + 1)]
            rhs = jnp.concatenate([vq * lo, vq * hi], axis=0).astype(BF16)
            parts.append(_dot(wpair_ref[q], rhs))
        chunks.append(jnp.concatenate(parts, axis=1))
    return jnp.concatenate(chunks, axis=0)


def _sgu_front(zuv, lng, lnb, grads=False):
    gelu = _gelu_and_grad if grads else lambda z: (_gelu(z), None)
    u, du = gelu(zuv[:, :D_SGU])
    v, dv = gelu(zuv[:, D_SGU:])
    mu = jnp.mean(v, axis=-1, keepdims=True)
    vc = v - mu
    rs = lax.rsqrt(jnp.mean(vc * vc, axis=-1, keepdims=True) + EPS)
    vn = vc * rs
    return u, vn, rs, vn * lng + lnb, du, dv


def _half_masks():
    lane = lax.broadcasted_iota(jnp.int32, (1, LANES), 1)
    lo = (lane < SGU_HEAD_DIM).astype(F32)
    return lo, 1.0 - lo


def _mix_fwd(x, p, tag):
    s = x.shape[0]

    def body(x_ref, g_ref, w_ref, wl_ref, wp_ref, sc_ref, lng_ref, lnb_ref, wsp_ref, bias_ref,
             za_ref, zuv_ref, h_ref, ob_ref, pooled_ref, oc_ref, buf, *tmp):
        i = pl.program_id(0)
        tile0 = POOL_PAD + MAX_WINDOW

        @pl.when(i == 0)
        def _():
            for ref in (buf, *tmp):
                ref[pl.ds(0, tile0), :] = jnp.zeros((tile0, D_POOL), F32)

        n, _ = _rms(x_ref[...])
        h = (n * g_ref[...]).astype(BF16)
        h_ref[...] = h
        z = _dot_nt(h, w_ref[...])
        za_ref[...] = z[:, :D_SSM]
        zb = z[:, D_SSM:D_SSM + D_POOL]
        zuv = z[:, D_SSM + D_POOL:]
        zuv_ref[...] = zuv
        buf[pl.ds(tile0, TS), :] = zb
        wl = wl_ref[...]
        pooled = (_window_sum(buf, tmp, tile0, wl, -1) / _pool_count(i, TS, wl) - zb).astype(BF16)
        buf[pl.ds(POOL_PAD, MAX_WINDOW), :] = zb[TS - MAX_WINDOW:, :]
        pooled_ref[...] = pooled
        ob_ref[...] = (_dot(pooled, wp_ref[...]) * sc_ref[...]).astype(BF16)
        lo, hi = _half_masks()
        u, _, _, vl, _, _ = _sgu_front(zuv, lng_ref[...], lnb_ref[...])
        mixed = _sgu_mix(vl, wsp_ref, lo, hi) + jnp.tile(bias_ref[...], (TS // CHUNK, 1))
        oc_ref[...] = (u * mixed).astype(BF16)

    return pl.pallas_call(
        body, grid=(s // TS,),
        in_specs=[_row(TS, D_MODEL), _const((1, D_MODEL)), _const((D_IN, D_MODEL)), _const((1, D_POOL)),
                  _const((D_POOL, D_POOL)), _const((1, D_POOL)), _const((1, D_SGU)), _const((1, D_SGU)),
                  _const((SGU_HEADS // 2, CHUNK, 2 * CHUNK)), _const((CHUNK, D_SGU))],
        out_specs=[_row(TS, D_SSM), _row(TS, 2 * D_SGU), _row(TS, D_MODEL), _row(TS, D_POOL), _row(TS, D_POOL),
                   _row(TS, D_SGU)],
        out_shape=[jax.ShapeDtypeStruct((s, D_SSM), F32), jax.ShapeDtypeStruct((s, 2 * D_SGU), F32),
                   jax.ShapeDtypeStruct((s, D_MODEL), BF16), jax.ShapeDtypeStruct((s, D_POOL), BF16),
                   jax.ShapeDtypeStruct((s, D_POOL), BF16), jax.ShapeDtypeStruct((s, D_SGU), BF16)],
        scratch_shapes=[pltpu.VMEM((POOL_ROWS, D_POOL), F32)] * 4,
        name=f"mix_fwd_{tag}", compiler_params=_cp(dimension_semantics=("arbitrary",)),
    )(x, p["g_mix"], p["w_in"], _pool_consts(), p["w_pool_bd"], p["pool_scale"], p["sgu_ln_g"], p["sgu_ln_b"],
      p["ws_pair"], p["bias_sp"])


def _blk_fwd(x0, oa, ob, oc, p, tag, head=None):
    s = x0.shape[0]
    ts = TS_FFN
    n_head = 0 if head is None else len(head)

    def body(x0_ref, oa_ref, ob_ref, oc_ref, wo_ref, g_ref, wg_ref, wu_ref, wd_ref, *refs):
        x1_ref, x2_ref, h2_ref, gt_ref, up_ref, ycat_ref = refs[n_head:n_head + 6]
        ycat = jnp.concatenate([oa_ref[...], ob_ref[...], oc_ref[...]], axis=1)
        ycat_ref[...] = ycat
        x1 = x0_ref[...] + _dot(ycat, wo_ref[...])
        x1_ref[...] = x1
        n, _ = _rms(x1)
        h2 = (n * g_ref[...]).astype(BF16)
        h2_ref[...] = h2
        gt = _dot_nt(h2, wg_ref[...])
        up = _dot_nt(h2, wu_ref[...])
        gt_ref[...] = gt.astype(BF16)
        up_ref[...] = up.astype(BF16)
        act = (gt * _sigmoid(gt) * up).astype(BF16)
        x2 = x1 + _dot(act, wd_ref[...])
        if head is None:
            x2_ref[...] = x2
            return
        t_ref, gf_ref = refs[:n_head]
        loss_ref, dgf_ref = refs[n_head + 6:]

        @pl.when(pl.program_id(0) == 0)
        def _():
            loss_ref[...] = jnp.zeros_like(loss_ref)
            dgf_ref[...] = jnp.zeros_like(dgf_ref)

        gf = gf_ref[...]
        nf, rf = _rms(x2)
        diff = nf * gf - t_ref[...]
        loss_ref[...] += jnp.sum(diff * diff) * (0.5 / D_MODEL)
        dxn, dgp = _rms_bwd(diff * (1.0 / D_MODEL), nf, rf, gf)
        dgf_ref[...] += _colsum8(dgp)
        x2_ref[...] = dxn

    in_specs = [_row(ts, D_MODEL), _row(ts, D_SSM), _row(ts, D_POOL), _row(ts, D_SGU), _const((D_MODEL, D_MODEL)),
                _const((1, D_MODEL)), _const((D_FF, D_MODEL)), _const((D_FF, D_MODEL)), _const((D_FF, D_MODEL))]
    out_specs = [_row(ts, D_MODEL), _row(ts, D_MODEL), _row(ts, D_MODEL), _row(ts, D_FF), _row(ts, D_FF),
                 _row(ts, D_MODEL)]
    out_shape = [jax.ShapeDtypeStruct((s, D_MODEL), F32), jax.ShapeDtypeStruct((s, D_MODEL), F32),
                 jax.ShapeDtypeStruct((s, D_MODEL), BF16), jax.ShapeDtypeStruct((s, D_FF), BF16),
                 jax.ShapeDtypeStruct((s, D_FF), BF16), jax.ShapeDtypeStruct((s, D_MODEL), BF16)]
    args = (x0, oa, ob, oc, p["w_out"], p["g_ffn"], p["w_gate"], p["w_up"], p["w_down"])
    if head is not None:
        in_specs += [_row(ts, D_MODEL), _const((1, D_MODEL))]
        out_specs += [_acc((SUBLANES, LANES)), _acc((SUBLANES, D_MODEL))]
        out_shape += [jax.ShapeDtypeStruct((SUBLANES, LANES), F32), jax.ShapeDtypeStruct((SUBLANES, D_MODEL), F32)]
        args += tuple(head)
    return pl.pallas_call(
        body, grid=(s // ts,), in_specs=in_specs, out_specs=out_specs, out_shape=out_shape,
        name=f"blk_fwd_{tag}", compiler_params=_cp(dimension_semantics=("arbitrary",)),
    )(*args)


def _blk_bwd(dx2, x1, gt, up, p, tag, after=()):
    s = dx2.shape[0]
    ts = TS_FFN

    def body(dx2_ref, x1_ref, gt_ref, up_ref, wd_ref, wgt_ref, wut_ref, wo_ref, g_ref,
             dx1_ref, da_ref, db_ref, dc_ref, dgt_ref, dup_ref, act_ref, dg_ref):
        @pl.when(pl.program_id(0) == 0)
        def _():
            dg_ref[...] = jnp.zeros_like(dg_ref)

        dx2v = dx2_ref[...]
        dact = _dot_nt(dx2v.astype(BF16), wd_ref[...])
        gf = gt_ref[...].astype(F32)
        uf = up_ref[...].astype(F32)
        sg = _sigmoid(gf)
        sl = gf * sg
        act_ref[...] = (sl * uf).astype(BF16)
        dgt = (dact * uf * (sg * (1.0 + gf * (1.0 - sg)))).astype(BF16)
        dup = (dact * sl).astype(BF16)
        dgt_ref[...] = dgt
        dup_ref[...] = dup
        dh2 = _dot(dgt, wgt_ref[...]) + _dot(dup, wut_ref[...])
        n, r = _rms(x1_ref[...])
        dxn, dgp = _rms_bwd(dh2, n, r, g_ref[...])
        dg_ref[...] += _colsum8(dgp)
        dx1 = dx2v + dxn
        dx1_ref[...] = dx1
        dy = _dot_nt(dx1.astype(BF16), wo_ref[...])
        da_ref[...] = dy[:, :D_SSM]
        db_ref[...] = dy[:, D_SSM:D_SSM + D_POOL]
        dc_ref[...] = dy[:, D_SSM + D_POOL:]

    return pl.pallas_call(
        _following(body, after), grid=(s // ts,),
        in_specs=_anywhere(after) + [_row(ts, D_MODEL), _row(ts, D_MODEL), _row(ts, D_FF), _row(ts, D_FF),
                  _const((D_FF, D_MODEL)), _const((D_FF, D_MODEL)), _const((D_FF, D_MODEL)),
                  _const((D_MODEL, D_MODEL)), _const((1, D_MODEL))],
        out_specs=[_row(ts, D_MODEL), _row(ts, D_SSM), _row(ts, D_POOL), _row(ts, D_SGU), _row(ts, D_FF),
                   _row(ts, D_FF), _row(ts, D_FF), _acc((SUBLANES, D_MODEL))],
        out_shape=[jax.ShapeDtypeStruct((s, D_MODEL), F32), jax.ShapeDtypeStruct((s, D_SSM), F32),
                   jax.ShapeDtypeStruct((s, D_POOL), F32), jax.ShapeDtypeStruct((s, D_SGU), F32),
                   jax.ShapeDtypeStruct((s, D_FF), BF16), jax.ShapeDtypeStruct((s, D_FF), BF16),
                   jax.ShapeDtypeStruct((s, D_FF), BF16), jax.ShapeDtypeStruct((SUBLANES, D_MODEL), F32)],
        name=f"blk_bwd_{tag}", compiler_params=_cp(dimension_semantics=("arbitrary",)),
    )(*after, dx2, x1, gt, up, p["w_down"], p["w_gate"], p["w_up"], p["w_out"], p["g_ffn"])


def _s5_bwd(dout, u, y, h_re, h_im, p, tag, after=()):
    s = u.shape[0]
    nt = s // TS
    seg = SEG

    def rev(n):
        return pl.BlockSpec((TS, n), lambda i: (nt - 1 - i, 0))

    def body(do_ref, u_ref, y_ref, hr_ref, hi_ref, perm_ref, ar_ref, ai_ref, cb_ref, bb_ref, dsk_ref,
             wglu_ref, bglu_ref,
             du_ref, dct_ref, dbb_ref, dar_ref, dai_ref, dd_ref, dwglu_ref, dbglu_ref,
             gr, gi, hsr, hsi, er, ei, jr, ji, cr, ci):
        @pl.when(pl.program_id(0) == 0)
        def _():
            for ref in (cr, ci, dct_ref, dbb_ref, dar_ref, dai_ref, dd_ref, dwglu_ref, dbglu_ref):
                ref[...] = jnp.zeros_like(ref)

        perm = perm_ref[...]
        uv = _to_scan_order(perm, u_ref[...])
        yv = y_ref[...]
        dov = _to_scan_order(perm, do_ref[...])
        g, gelu_dy = _gelu_and_grad(yv)
        gb = g.astype(BF16)
        sg = _sigmoid(_dot(gb, wglu_ref[...]) + bglu_ref[...])
        dpre = dov * g * sg * (1.0 - sg)
        dpb = dpre.astype(BF16)
        dwglu_ref[...] += _dot_tn(gb, dpb)
        dbglu_ref[...] += _colsum8(dpre)
        dy = (dov * sg + _dot_nt(dpb, wglu_ref[...])) * gelu_dy
        dd_ref[...] += _colsum8(dy * uv)
        dyb = dy.astype(BF16)
        hsr[...] = hr_ref[...].astype(F32)
        hsi[...] = hi_ref[...].astype(F32)
        ub = uv.astype(BF16)
        dus = []

        def state_cotangents(j):
            dct_ref[j] += _dot_tn(_lanes(dyb, j), _state_cat(hr_ref, hi_ref, j))
            _state_split(gr, gi, j, _dot(_lanes(dyb, j), cb_ref[j]))

        def input_cotangents(j):
            gb_j = _state_cat(gr, gi, j)
            dbb_ref[j] += _dot_tn(_lanes(ub, j), gb_j)
            dus.append(_dot(gb_j, bb_ref[j]))

        def scan(j):
            cols = _states(j)
            ar = ar_ref[:, cols]
            ai = -ai_ref[:, cols]
            g_r = g_i = jnp.zeros((SUBLANES, SLAB_STATES), F32)
            for k in range(seg - 1, -1, -1):
                rows = pl.ds(SUBLANES * k, SUBLANES)
                n_r, n_i = _cmul(ar, ai, g_r, g_i)
                g_r = n_r + gr[rows, cols]
                g_i = n_i + gi[rows, cols]
            er[:, cols] = g_r
            ei[:, cols] = g_i
            pr, pi = _cpow(ar[0:1, :], ai[0:1, :], seg)
            c_r = cr[:, cols]
            c_i = ci[:, cols]
            for q in range(SUBLANES - 1, -1, -1):
                jr[q:q + 1, cols] = c_r
                ji[q:q + 1, cols] = c_i
                n_r, n_i = _cmul(pr, pi, c_r, c_i)
                c_r = n_r + er[q:q + 1, cols]
                c_i = n_i + ei[q:q + 1, cols]
            cr[:, cols] = c_r
            ci[:, cols] = c_i
            g_r = jr[:, cols]
            g_i = ji[:, cols]
            a_r = a_i = jnp.zeros((SUBLANES, SLAB_STATES), F32)
            for k in range(seg - 1, -1, -1):
                rows = pl.ds(SUBLANES * k, SUBLANES)
                h_r = hsr[rows, cols]
                h_i = hsi[rows, cols]
                a_r = a_r + g_r * h_r + g_i * h_i
                a_i = a_i + g_i * h_r - g_r * h_i
                n_r, n_i = _cmul(ar, ai, g_r, g_i)
                g_r = n_r + gr[rows, cols]
                g_i = n_i + gi[rows, cols]
                gr[rows, cols] = g_r
                gi[rows, cols] = g_i
            dar_ref[:, cols] += a_r
            dai_ref[:, cols] += a_i

        for stage in (state_cotangents, scan, input_cotangents):
            for j in range(N_USLAB):
                stage(j)
        du = dy * dsk_ref[...] + jnp.concatenate(dus, axis=1)
        du_ref[...] = _dot_tn(perm, du.astype(BF16)).astype(BF16)

    big = (TS, N_STATE)
    return pl.pallas_call(
        _following(body, after), grid=(nt,),
        in_specs=_anywhere(after) + [rev(D_SSM), rev(D_SSM), rev(D_SSM), rev(N_STATE), rev(N_STATE), _const((TS, TS)),
                  _const(STATE_TILE), _const(STATE_TILE), _const(S5_IN), _const(S5_OUT), _const((1, D_SSM)),
                  _const((D_SSM, D_SSM)), _const((1, D_SSM))],
        out_specs=[rev(D_SSM), _acc(S5_IN), _acc(S5_IN), _acc(STATE_TILE), _acc(STATE_TILE),
                   _acc((SUBLANES, D_SSM)), _acc((D_SSM, D_SSM)), _acc((SUBLANES, D_SSM))],
        out_shape=[jax.ShapeDtypeStruct((s, D_SSM), BF16), jax.ShapeDtypeStruct(S5_IN, F32),
                   jax.ShapeDtypeStruct(S5_IN, F32), jax.ShapeDtypeStruct(STATE_TILE, F32),
                   jax.ShapeDtypeStruct(STATE_TILE, F32), jax.ShapeDtypeStruct((SUBLANES, D_SSM), F32),
                   jax.ShapeDtypeStruct((D_SSM, D_SSM), F32), jax.ShapeDtypeStruct((SUBLANES, D_SSM), F32)],
        scratch_shapes=[pltpu.VMEM(big, F32), pltpu.VMEM(big, F32), pltpu.VMEM(big, F32), pltpu.VMEM(big, F32),
                        pltpu.VMEM(STATE_TILE, F32), pltpu.VMEM(STATE_TILE, F32), pltpu.VMEM(STATE_TILE, F32),
                        pltpu.VMEM(STATE_TILE, F32), pltpu.VMEM((1, N_STATE), F32), pltpu.VMEM((1, N_STATE), F32)],
        name=f"s5_bwd_{tag}", compiler_params=_cp(dimension_semantics=("arbitrary",)),
    )(*after, dout, u, y, h_re, h_im, _scan_order(), p["a_re8"], p["a_im8"], p["cb3"], p["bb3"], p["d_skip"], p["w_glu"],
      p["b_glu"])


def _mix_bwd(dza, db, dc, pooled, zuv, x0, dx1, p, tag, after=()):
    s = x0.shape[0]
    nt = s // TS

    def rev(n):
        return pl.BlockSpec((TS, n), lambda i: (nt - 1 - i, 0))

    def body(da_ref, db_ref, dc_ref, po_ref, z_ref, x_ref, dx1_ref, wl_ref, wp_ref, wpt_ref, sc_ref, lng_ref, lnb_ref,
             wsp_ref, wspt_ref, bias_ref, win_ref, g_ref,
             dx0_ref, dz_ref, dwp_ref, dsc_ref, dws_ref, dbias_ref, dlng_ref, dlnb_ref, dg_ref, buf, *tmp):
        step = pl.program_id(0)
        i = nt - 1 - step

        @pl.when(step == 0)
        def _():
            for ref in (dwp_ref, dsc_ref, dws_ref, dbias_ref, dlng_ref, dlnb_ref, dg_ref):
                ref[...] = jnp.zeros_like(ref)
            for ref in (buf, *tmp):
                ref[pl.ds(TS, POOL_ROWS - TS), :] = jnp.zeros((POOL_ROWS - TS, D_POOL), F32)

        wl = wl_ref[...]
        sc = sc_ref[...]
        dob = db_ref[...]
        pooled_b = po_ref[...]
        dsc_ref[...] += _colsum8(dob * _dot(pooled_b, wp_ref[...]))
        dmixb = (dob * sc).astype(BF16)
        dwp_ref[...] += _dot_tn(pooled_b, dmixb)
        dpool = _dot(dmixb, wpt_ref[...])
        dq = dpool / _pool_count(i, TS, wl)
        buf[pl.ds(0, TS), :] = dq
        dzb = _window_sum(buf, tmp, 0, wl, 1) - dpool
        buf[pl.ds(TS, MAX_WINDOW), :] = dq[:MAX_WINDOW, :]

        lo, hi = _half_masks()
        lng = lng_ref[...]
        u, vn, rs, vl, gelu_du, gelu_dv = _sgu_front(z_ref[...], lng, lnb_ref[...], grads=True)
        mixed = _sgu_mix(vl, wsp_ref, lo, hi) + jnp.tile(bias_ref[...], (TS // CHUNK, 1))
        doc = dc_ref[...]
        dzu = doc * mixed * gelu_du
        dmix = doc * u
        dbias = dbias_ref[...]
        for c in range(TS // CHUNK):
            dmc = dmix[CHUNK * c:CHUNK * (c + 1), :]
            dbias = dbias + dmc
            vlc = vl[CHUNK * c:CHUNK * (c + 1), :].astype(BF16)
            for q in range(SGU_HEADS // 2):
                dmq = _lanes(dmc, q)
                vq = _lanes(vlc, q)
                dws_ref[2 * q] += _dot_nt((dmq * lo).astype(BF16), vq)
                dws_ref[2 * q + 1] += _dot_nt((dmq * hi).astype(BF16), vq)
        dbias_ref[...] = dbias
        dvl = _sgu_mix(dmix, wspt_ref, lo, hi)
        dlng_ref[...] += _colsum8(dvl * vn)
        dlnb_ref[...] += _colsum8(dvl)
        dvn = dvl * lng
        dv = rs * (dvn - jnp.mean(dvn, axis=-1, keepdims=True) - vn * jnp.mean(dvn * vn, axis=-1, keepdims=True))

        dz = jnp.concatenate([da_ref[...], dzb.astype(BF16), dzu.astype(BF16), (dv * gelu_dv).astype(BF16)], axis=1)
        dz_ref[...] = dz
        n, r = _rms(x_ref[...])
        dxn, dgp = _rms_bwd(_dot(dz, win_ref[...]), n, r, g_ref[...])
        dg_ref[...] += _colsum8(dgp)
        dx0_ref[...] = dx1_ref[...] + dxn

    pair = (SGU_HEADS // 2, CHUNK, 2 * CHUNK)
    return pl.pallas_call(
        _following(body, after), grid=(nt,),
        in_specs=_anywhere(after) + [rev(D_SSM), rev(D_POOL), rev(D_SGU), rev(D_POOL), rev(2 * D_SGU), rev(D_MODEL), rev(D_MODEL),
                  _const((1, D_POOL)), _const((D_POOL, D_POOL)), _const((D_POOL, D_POOL)), _const((1, D_POOL)),
                  _const((1, D_SGU)), _const((1, D_SGU)), _const(pair), _const(pair), _const((CHUNK, D_SGU)),
                  _const((D_IN, D_MODEL)), _const((1, D_MODEL))],
        out_specs=[rev(D_MODEL), rev(D_IN), _acc((D_POOL, D_POOL)), _acc((SUBLANES, D_POOL)),
                   _acc((SGU_HEADS, CHUNK, CHUNK)), _acc((CHUNK, D_SGU)), _acc((SUBLANES, D_SGU)),
                   _acc((SUBLANES, D_SGU)), _acc((SUBLANES, D_MODEL))],
        out_shape=[jax.ShapeDtypeStruct((s, D_MODEL), F32), jax.ShapeDtypeStruct((s, D_IN), BF16),
                   jax.ShapeDtypeStruct((D_POOL, D_POOL), F32), jax.ShapeDtypeStruct((SUBLANES, D_POOL), F32),
                   jax.ShapeDtypeStruct((SGU_HEADS, CHUNK, CHUNK), F32), jax.ShapeDtypeStruct((CHUNK, D_SGU), F32),
                   jax.ShapeDtypeStruct((SUBLANES, D_SGU), F32), jax.ShapeDtypeStruct((SUBLANES, D_SGU), F32),
                   jax.ShapeDtypeStruct((SUBLANES, D_MODEL), F32)],
        scratch_shapes=[pltpu.VMEM((POOL_ROWS, D_POOL), F32)] * 4,
        name=f"mix_bwd_{tag}", compiler_params=_cp(dimension_semantics=("arbitrary",)),
    )(*after, dza, db, dc, pooled, zuv, x0, dx1, _pool_consts(), p["w_pool_bd"], p["w_pool_bd_t"], p["pool_scale"],
      p["sgu_ln_g"], p["sgu_ln_b"], p["ws_pair"], p["ws_pair_t"], p["bias_sp"], p["w_in"], p["g_mix"])


def _atb(a, b, tag, after=()):
    s, ka = a.shape
    kb = b.shape[1]
    ts = ATB_ROWS
    tn = min(kb, ATB_COLS)
    ns = s // ts

    def body(a_ref, b_ref, *rest):
        o_ref = rest[-1]

        @pl.when(pl.program_id(1) == 0)
        def _():
            o_ref[...] = jnp.zeros_like(o_ref)

        o_ref[...] += _dot_tn(a_ref[...].astype(BF16), b_ref[...].astype(BF16))

    return pl.pallas_call(
        body, grid=(kb // tn, ns),
        in_specs=[pl.BlockSpec((ts, ka), lambda j, i: (i, 0)), pl.BlockSpec((ts, tn), lambda j, i: (i, j))]
        + [pl.BlockSpec(memory_space=pl.ANY)] * len(after),
        out_specs=pl.BlockSpec((ka, tn), lambda j, i: (0, j)),
        out_shape=jax.ShapeDtypeStruct((ka, kb), F32),
        name=f"atb_{tag}", compiler_params=_cp(dimension_semantics=("arbitrary", "arbitrary")),
    )(a, b, *after)


def _s5_discretise(a_re, a_im, log_dt, b_re, b_im):
    dt = jnp.exp(log_dt)[:, None]
    mag = jnp.exp(a_re * dt)
    ar = mag * jnp.cos(a_im * dt)
    ai = mag * jnp.sin(a_im * dt)
    den = a_re * a_re + a_im * a_im
    f_re = ((ar - 1.0) * a_re + ai * a_im) / den
    f_im = (ai * a_re - (ar - 1.0) * a_im) / den
    bb_re = f_re[..., None] * b_re - f_im[..., None] * b_im
    bb_im = f_re[..., None] * b_im + f_im[..., None] * b_re
    return ar, ai, bb_re, bb_im


def _block_diag(blocks):
    g, r, c = blocks.shape
    eye = jnp.eye(g, dtype=blocks.dtype)
    return (blocks[:, :, None, :] * eye[:, None, :, None]).reshape(g * r, g * c)


def _block_diag_extract(m, g):
    r = m.shape[0] // g
    c = m.shape[1] // g
    eye = jnp.eye(g, dtype=m.dtype)
    return jnp.sum(m.reshape(g, r, g, c) * eye[:, None, :, None], axis=2)


GROUPS_PER_SLAB = N_GROUPS // N_USLAB


def _slab_diag(blocks):
    k = GROUPS_PER_SLAB
    _, r, c = blocks.shape
    eye = jnp.eye(k, dtype=blocks.dtype)
    spread = blocks.reshape(N_USLAB, k, r, 1, c) * eye[None, :, None, :, None]
    return spread.reshape(N_USLAB, k * r, k * c)


def _slab_diag_extract(m):
    k = GROUPS_PER_SLAB
    r, c = m.shape[1] // k, m.shape[2] // k
    eye = jnp.eye(k, dtype=m.dtype)
    return jnp.sum(m.reshape(N_USLAB, k, r, k, c) * eye[None, :, None, :, None], axis=3).reshape(N_GROUPS, r, c)


def _state_slabs(v):
    return jnp.broadcast_to(v.reshape(1, N_STATE), STATE_TILE)


def _tril():
    return jnp.tril(jnp.ones((CHUNK, CHUNK), dtype=bool))


def _layer_params(w, l):
    row = lambda v: v.reshape(1, -1)
    t = lambda m: jnp.swapaxes(m, -1, -2)
    ar, ai, bb_re, bb_im = _s5_discretise(w["A_re"][l], w["A_im"][l], w["log_dt"][l], w["B_re"][l], w["B_im"][l])
    bbt3 = jnp.concatenate([_slab_diag(t(bb_re)), _slab_diag(t(bb_im))], axis=2).astype(BF16)
    ct3 = jnp.concatenate([_slab_diag(t(w["C_re"][l])), -_slab_diag(t(w["C_im"][l]))], axis=1).astype(BF16)
    ws = jnp.where(_tril()[None], w["w_spatial"][l], 0.0)
    pair = lambda m: jnp.stack([jnp.concatenate([m[2 * q], m[2 * q + 1]], axis=1)
                                for q in range(SGU_HEADS // 2)]).astype(BF16)
    wp = _block_diag(w["w_pool"][l]).astype(BF16)
    p = dict(
        g_mix=row(w["g_mix"][l]), g_ffn=row(w["g_ffn"][l]), d_skip=row(w["D_skip"][l]), b_glu=row(w["b_glu"][l]),
        pool_scale=row(w["pool_scale"][l]), sgu_ln_g=row(w["sgu_ln_g"][l]), sgu_ln_b=row(w["sgu_ln_b"][l]),
        a_re8=_state_slabs(ar), a_im8=_state_slabs(ai),
        bbt3=bbt3, bb3=t(bbt3), ct3=ct3, cb3=t(ct3),
        w_pool_bd=wp, w_pool_bd_t=t(wp), ws_pair=pair(ws), ws_pair_t=pair(t(ws)),
        bias_sp=jnp.repeat(t(w["b_spatial"][l]), SGU_HEAD_DIM, axis=1),
    )
    return p


MIX_WEIGHTS = ("w_in", "w_glu")
FFN_WEIGHTS = ("w_out", "w_gate", "w_up", "w_down")


def _with_big(p, mats):
    p.update(mats)


def _rows_sum(v):
    return jnp.sum(v, axis=0)


ATB_COLS = 1024
ATB_ROWS = 1024


def _after(v, token):
    return v + token[0, 0]


def _layer_bwd(dx2, sv, p, w, l, tag, hooks, after):
    t = lambda m: jnp.swapaxes(m, -1, -2)
    dx1, da, db, dc, dgt, dup, act, dg_ffn = _blk_bwd(dx2, sv["x1"], sv["gt"], sv["up"], p, tag, after)
    after = hooks["tick"]([dx1])
    after = after + hooks["on_grads"](l, "ffn", {
        "w_down": _atb(act, dx2, tag + "_wd", after), "w_gate": _atb(dgt, sv["h2"], tag + "_wg", after),
        "w_up": _atb(dup, sv["h2"], tag + "_wu", after), "w_out": _atb(sv["ycat"], dx1, tag + "_wo", after)})
    g = {}
    g["g_ffn"] = _rows_sum(dg_ffn)
    dza, dc3, dbbt3, dar8, dai8, dd8, dwglu, dbglu8 = _s5_bwd(
        da, sv["za"], sv["y"], sv["h_re"], sv["h_im"], p, tag, after)
    after = hooks["tick"]([dza])
    dx0, dz, dwp, dsc8, dws, dbias, dlng8, dlnb8, dg_mix = _mix_bwd(
        dza, db, dc, sv["pooled"], sv["zuv"], sv["x0"], dx1, p, tag, after)
    g["g_mix"] = _rows_sum(dg_mix)
    g["b_glu"] = _rows_sum(dbglu8)
    g["D_skip"] = _rows_sum(dd8)
    half = N_STATE // N_USLAB
    g["C_re"] = _slab_diag_extract(dc3[:, :, :half])
    g["C_im"] = -_slab_diag_extract(dc3[:, :, half:])
    dar = jnp.sum(dar8, axis=0).reshape(N_GROUPS, SSM_STATE)
    dai = jnp.sum(dai8, axis=0).reshape(N_GROUPS, SSM_STATE)
    dbb_re = t(_slab_diag_extract(dbbt3[:, :, :half]))
    dbb_im = t(_slab_diag_extract(dbbt3[:, :, half:]))
    _, disc_vjp = jax.vjp(_s5_discretise, w["A_re"][l], w["A_im"][l], w["log_dt"][l], w["B_re"][l], w["B_im"][l])
    g["A_re"], g["A_im"], g["log_dt"], g["B_re"], g["B_im"] = disc_vjp((dar, dai, dbb_re, dbb_im))
    g["w_pool"] = _block_diag_extract(dwp, len(POOL_WINDOWS))
    g["pool_scale"] = _rows_sum(dsc8)
    g["sgu_ln_g"] = _rows_sum(dlng8)
    g["sgu_ln_b"] = _rows_sum(dlnb8)
    g["w_spatial"] = jnp.where(_tril()[None], dws, 0.0)
    g["b_spatial"] = t(jnp.sum(dbias.reshape(CHUNK, SGU_HEADS, SGU_HEAD_DIM), axis=-1))
    after = hooks["on_small"](l, g)
    after = hooks["on_grads"](l, "mix", {"w_in": _atb(dz, sv["h1"], tag + "_wi", after), "w_glu": dwglu})
    return dx0, after


def _local_step(x, target, w, hooks):
    params = [_layer_params(w, l) for l in range(DEPTH)]
    saved = []
    h = x
    for l in range(DEPTH):
        p, tag = params[l], f"l{l}"
        _with_big(p, hooks["get_big"](l, "mix", [h]))
        za, zuv, h1, ob, pooled, oc = _mix_fwd(h, p, tag)
        oa, y, h_re, h_im = _s5_fwd(za, p, tag)
        _with_big(p, hooks["get_big"](l, "ffn", [oa, ob, oc]))
        head = (target, w["g_final"].reshape(1, -1)) if l == DEPTH - 1 else None
        x1, x2, h2, gt, up, ycat, *loss_parts = _blk_fwd(h, oa, ob, oc, p, tag, head)
        saved.append(dict(x0=h, za=za, zuv=zuv, h1=h1, ycat=ycat, y=y, h_re=h_re, h_im=h_im, pooled=pooled, x1=x1,
                          h2=h2, gt=gt, up=up))
        h = x2
    dx = h
    loss8, dgf8 = loss_parts
    grads = [None] * DEPTH

    def on_small(l, g_l):
        grads[l] = g_l
        if l > 0:
            return []
        g = {n: jnp.stack([grads[k][n] for k in range(DEPTH)]) for n in SMALL if n != "g_final"}
        g["g_final"] = _rows_sum(dgf8)
        return hooks["on_small"](g, loss8[0, 0])

    after = []
    for l in reversed(range(DEPTH)):
        dx, after = _layer_bwd(dx, saved[l], params[l], w, l, f"l{l}", dict(hooks, on_small=on_small), after)
    return dx


_ANY = pl.BlockSpec(memory_space=pl.ANY)
_MESH = pl.DeviceIdType.MESH


def _place():
    return lax.axis_index("x"), lax.axis_index("y"), lax.axis_index("c")


def _other_chips(x, y):
    return [(1 - x, y), (x, 1 - y), (1 - x, 1 - y)]


def _dma_sems(n):
    return pltpu.SemaphoreType.DMA((n,))


def _remote(src, dst, send_sems, recv_sems, k, to):
    return pltpu.make_async_remote_copy(src_ref=src, dst_ref=dst, send_sem=send_sems.at[k], recv_sem=recv_sems.at[k],
                                        device_id=to, device_id_type=_MESH)


_HBM = pl.BlockSpec(memory_space=pltpu.HBM)
_SEM = pl.BlockSpec(memory_space=pltpu.SEMAPHORE)
_EFFECT = pltpu.SideEffectType.DATAFLOW_SIDE_EFFECTING
N_REL = N_CHIPS - 1


def _gather_plan(x, y, c, srcs, lands):
    plan = []
    for l in lands:
        r = l.shape[0] // N_CHIPS
        rows = l.at[pl.ds((2 * x + y) * r, r)]
        plan += [(rows, rows, (cx, cy, c)) for cx, cy in _other_chips(x, y)]
    return plan


def _half_rows(land, chip, c):
    h = land.shape[0] // (2 * N_CHIPS)
    return land.at[pl.ds((2 * chip + c) * h, h)]


def _gather_half_plan(x, y, c, srcs, lands):
    return [(_half_rows(l, 2 * x + y, c), _half_rows(l, 2 * x + y, c), (cx, cy, c))
            for l in lands for cx, cy in _other_chips(x, y)]


def _forward_halves(lands, tag):
    nw = len(lands)

    def body(*refs):
        ins = refs[:nw]
        send_sems, recv_sems = refs[2 * nw:]
        x, y, c = _place()
        chips = [2 * cx + cy for cx, cy in _other_chips(x, y)]
        sends = [_remote(_half_rows(ins[i], k, c), _half_rows(ins[i], k, c), send_sems, recv_sems, N_REL * i + j,
                         (x, y, 1 - c)) for i in range(nw) for j, k in enumerate(chips)]
        for cp in sends:
            cp.start()
        for i in range(nw):
            for j, k in enumerate(chips):
                sends[N_REL * i + j].wait_send()
                _remote(_half_rows(ins[i], k, c), _half_rows(ins[i], k, 1 - c), send_sems, recv_sems, N_REL * i + j,
                        (x, y, 1 - c)).wait_recv()

    return pl.pallas_call(
        body, out_shape=[jax.ShapeDtypeStruct(a.shape, a.dtype) for a in lands], in_specs=[_ANY] * nw,
        out_specs=[_ANY] * nw, input_output_aliases={i: i for i in range(nw)},
        scratch_shapes=[_dma_sems(N_REL * nw), _dma_sems(N_REL * nw)], name=f"forward_halves_{tag}",
    )(*lands)


def _sibling_plan(x, y, c, srcs, lands):
    return [(s.at[:, 1 - c], l, (x, y, 1 - c)) for s, l in zip(srcs, lands)]


def _slab_plan(x, y, c, srcs, lands):
    return [(s.at[2 * cx + cy], l.at[j], (cx, cy, c))
            for s, l in zip(srcs, lands) for j, (cx, cy) in enumerate(_other_chips(x, y))]


def _plan_copies(plan, srcs, lands, send_sems, recv_sems):
    x, y, c = _place()
    return [_remote(s, d, send_sems, recv_sems, k, to) for k, (s, d, to) in enumerate(plan(x, y, c, srcs, lands))]


def _hbm(a):
    return pltpu.with_memory_space_constraint(a, pltpu.HBM)


def _everyone_plan(x, y, c, srcs, lands):
    me = 4 * x + 2 * y + c
    peers = [(x, y, 1 - c)] + [(cx, cy, cc) for cx, cy in _other_chips(x, y) for cc in (c, 1 - c)]
    return [(s, l.at[me], peer) for s, l in zip(srcs, lands) for peer in peers]


def _copies_start(name, plan, srcs, lands, ncopies):
    ns, n = len(srcs), len(srcs) + len(lands)

    def body(*refs):
        for cp in _plan_copies(plan, refs[:ns], refs[ns:n], refs[n], refs[n + 1]):
            cp.start()
        refs[-1][...] = jnp.zeros_like(refs[-1])

    ref_out = [pltpu.HBM(a.shape, a.dtype) for a in (*srcs, *lands)]
    out = pl.pallas_call(
        body, name=name, in_specs=[_HBM] * n,
        out_shape=(_dma_sems(ncopies), _dma_sems(ncopies), *ref_out, jax.ShapeDtypeStruct((SUBLANES, LANES), F32)),
        out_specs=(_SEM, _SEM, *[_HBM] * n, pl.BlockSpec(memory_space=pltpu.VMEM)),
        input_output_aliases={i: 2 + i for i in range(n)},
        compiler_params=pltpu.CompilerParams(has_side_effects=_EFFECT),
    )(*[_hbm(a) for a in (*srcs, *lands)])
    return dict(name=name, plan=plan, sems=out[:2], srcs=out[2:2 + ns], lands=out[2 + ns:2 + n], token=out[-1])


def _copies_wait(started, after):
    ns = len(started["srcs"])
    n = ns + len(started["lands"])
    plan = started["plan"]

    def body(*refs):
        for cp in _plan_copies(plan, refs[:ns], refs[ns:n], refs[n], refs[n + 1]):
            cp.wait_send()
            cp.wait_recv()

    args = (*started["srcs"], *started["lands"])
    out = pl.pallas_call(
        body, name=started["name"] + "_wait", out_shape=[pltpu.HBM(a.shape, a.dtype) for a in args],
        in_specs=[_HBM] * n + [_SEM, _SEM] + [_ANY] * len(after), out_specs=[_HBM] * n,
        input_output_aliases={i: i for i in range(n)},
        compiler_params=pltpu.CompilerParams(has_side_effects=_EFFECT),
    )(*args, *started["sems"], *after)
    return out[:ns], out[ns:]


def _place_shards(ws, layer, sel, after, tag):
    nw = len(ws)

    def body(sel_ref, *refs):
        for i in range(nw):
            refs[nw + len(after) + i][...] = refs[i][...].astype(BF16)

    return pl.pallas_call(
        body, grid_spec=pltpu.PrefetchScalarGridSpec(
            num_scalar_prefetch=1, grid=(1,),
            in_specs=[pl.BlockSpec((None,) + a.shape[1:], lambda i, s: (layer, 0, 0)) for a in ws] + [_ANY] * len(after),
            out_specs=[pl.BlockSpec(a.shape[1:], lambda i, s: (s[1], 0)) for a in ws]),
        out_shape=[jax.ShapeDtypeStruct((N_CHIPS * a.shape[1], a.shape[2]), BF16) for a in ws],
        name=f"place_shards_{tag}", compiler_params=_cp(dimension_semantics=("arbitrary",)),
    )(sel, *ws, *after)


def _share_halves(fs, layer, tag):
    nw = len(fs)

    def body(*refs):
        ins = refs[:nw]
        send_sems, recv_sems = refs[2 * nw:]
        x, y, c = _place()

        def half(i, who):
            h = ins[i].shape[1] // 2
            return ins[i].at[layer, pl.ds(who * h, h)]

        sends = [_remote(half(i, c), half(i, c), send_sems, recv_sems, i, (x, y, 1 - c)) for i in range(nw)]
        for cp in sends:
            cp.start()
        for i in range(nw):
            sends[i].wait_send()
            _remote(half(i, c), half(i, 1 - c), send_sems, recv_sems, i, (x, y, 1 - c)).wait_recv()

    return pl.pallas_call(
        body, out_shape=[jax.ShapeDtypeStruct(f.shape, f.dtype) for f in fs], in_specs=[_ANY] * nw,
        out_specs=[_ANY] * nw, input_output_aliases={i: i for i in range(nw)},
        scratch_shapes=[_dma_sems(nw), _dma_sems(nw)], name=f"share_halves_{tag}",
    )(*fs)


def _add_halves(g4s, recvs, sel, tag):
    nw = len(g4s)

    def body(sel_ref, *refs):
        for i in range(nw):
            refs[2 * nw + i][...] = (refs[i][...] + refs[nw + i][...]).astype(BF16)

    mine = [pl.BlockSpec((None, None) + g.shape[2:], lambda k, s: (k, s[0], 0, 0)) for g in g4s]
    slab = [pl.BlockSpec((None,) + g.shape[2:], lambda k, s: (k, 0, 0)) for g in g4s]
    return pl.pallas_call(
        body, grid_spec=pltpu.PrefetchScalarGridSpec(num_scalar_prefetch=1, grid=(N_CHIPS,), in_specs=mine + slab,
                                                     out_specs=slab),
        out_shape=[jax.ShapeDtypeStruct(r.shape, BF16) for r in recvs], name=f"add_halves_{tag}",
        compiler_params=_cp(dimension_semantics=("arbitrary",)),
    )(sel, *g4s, *recvs)


def _add_chips(ps, slabs, fs, layer, sel, tag):
    nw = len(ps)
    old = [f for f in fs if f is not None]

    def body(sel_ref, *refs):
        outs = refs[2 * nw + len(old):]
        for i in range(nw):
            acc = refs[i][...].astype(F32)
            for j in range(N_REL):
                acc = acc + refs[nw + i][j].astype(F32)
            outs[i][...] = acc

    shapes = [(DEPTH, 2 * p.shape[1], p.shape[2]) for p in ps]
    in_specs = [pl.BlockSpec((None,) + p.shape[1:], lambda i, s: (s[1], 0, 0)) for p in ps]
    in_specs += [pl.BlockSpec(sl.shape, lambda i, s: (0, 0, 0)) for sl in slabs]
    in_specs += [_ANY] * len(old)
    first_old = 1 + 2 * nw
    aliases, k = {}, 0
    for i, f in enumerate(fs):
        if f is not None:
            aliases[first_old + k] = i
            k += 1
    return pl.pallas_call(
        body, grid_spec=pltpu.PrefetchScalarGridSpec(
            num_scalar_prefetch=1, grid=(1,), in_specs=in_specs,
            out_specs=[pl.BlockSpec((None,) + p.shape[1:], lambda i, s: (layer, s[0], 0)) for p in ps]),
        out_shape=[jax.ShapeDtypeStruct(sh, F32) for sh in shapes], input_output_aliases=aliases,
        name=f"add_chips_{tag}", compiler_params=_cp(dimension_semantics=("arbitrary",)),
    )(sel, *ps, *slabs, *old)


def _adamw_math(w, g, m, v):
    m = ADAM_B1 * m + (1.0 - ADAM_B1) * g
    v = ADAM_B2 * v + (1.0 - ADAM_B2) * (g * g)
    m_hat = m / (1.0 - ADAM_B1 ** ADAM_STEP)
    v_hat = v / (1.0 - ADAM_B2 ** ADAM_STEP)
    delta = -ADAM_LR * (m_hat / (jnp.sqrt(v_hat) + ADAM_EPS) + ADAM_WD * w)
    return delta, m, v


ADAM_ROWS = 512


def _row_tile(rows, most):
    return max(t for t in range(SUBLANES, most + 1, SUBLANES) if rows % t == 0)


def _adamw(w, g, m, v, tag, after=()):
    depth, rows, cols = w.shape
    tr = _row_tile(rows, ADAM_ROWS)

    def body(w_ref, g_ref, m_ref, v_ref, *rest):
        d_ref, nm_ref, nv_ref = rest[len(after):]
        d, nm, nv = _adamw_math(w_ref[...], g_ref[...], m_ref[...], v_ref[...])
        d_ref[...] = d
        nm_ref[...] = nm
        nv_ref[...] = nv

    spec = pl.BlockSpec((None, tr, cols), lambda l, i: (l, i, 0))
    return pl.pallas_call(
        body, grid=(depth, rows // tr), in_specs=[spec] * 4 + [_ANY] * len(after), out_specs=[spec] * 3,
        out_shape=[jax.ShapeDtypeStruct(w.shape, F32)] * 3, name=f"adamw_{tag}",
        compiler_params=_cp(dimension_semantics=("arbitrary", "arbitrary")),
    )(w, g, m, v, *after)


SMALL_TILE = 512
PRECISE = ("g_final",)
COARSE = [n for n in SMALL if n not in PRECISE]


def _small_reduce(gathered):
    n = len(gathered)

    def body(*refs):
        for ga_ref, g_ref in zip(refs[:n], refs[n:]):
            g = ga_ref[0].astype(F32)
            for k in range(1, N_DEV):
                g = g + ga_ref[k].astype(F32)
            g_ref[...] = g

    return pl.pallas_call(
        body, out_shape=[jax.ShapeDtypeStruct(b.shape[1:], F32) for b in gathered], name="small_reduce",
        compiler_params=_cp(),
    )(*gathered)


def _small_adamw(g, w, m, v):
    names = list(g)
    n = len(names)
    shapes = {k: g[k].shape if g[k].ndim > 1 else (1,) + g[k].shape for k in names}

    def body(*refs):
        ins, outs = refs[:4 * n], refs[4 * n:]
        for i in range(n):
            d, nm, nv = _adamw_math(ins[n + i][...], ins[i][...], ins[2 * n + i][...], ins[3 * n + i][...])
            outs[i][...] = d
            outs[n + i][...] = nm
            outs[2 * n + i][...] = nv

    out = pl.pallas_call(
        body, out_shape=[jax.ShapeDtypeStruct(shapes[k], F32) for _ in range(3) for k in names], name="adamw_small",
        compiler_params=_cp(),
    )(*[src[k].reshape(shapes[k]) for src in (g, w, m, v) for k in names])
    return [{k: out[j * n + i].reshape(g[k].shape) for i, k in enumerate(names)} for j in range(3)]


def _exchange_form(n, a):
    return jnp.swapaxes(a, 1, 2) if n in TRANSPOSED else a


PACK_ROWS = 16


def _rows_of(size):
    return -(-size // (LANES * PACK_ROWS)) * PACK_ROWS


SMALL_VIEW = {"B_re": (0, 1, 3, 2), "B_im": (0, 1, 3, 2), "b_spatial": (1, 0, 2)}
assert all(tuple(order[i] for i in order) == tuple(range(len(order))) for order in SMALL_VIEW.values())


def _view(n, a):
    return jnp.transpose(a, SMALL_VIEW[n]) if n in SMALL_VIEW else a


def _pack(vals, names, extra=None):
    parts = [_view(n, vals[n]).reshape(-1) for n in names] + ([] if extra is None else [extra.reshape(1)])
    tiles = [jnp.pad(a, (0, _rows_of(a.size) * LANES - a.size)).reshape(-1, LANES) for a in parts]
    rows = sum(t.shape[0] for t in tiles)
    if rows > SMALL_TILE:
        tiles.append(jnp.zeros((-rows % SMALL_TILE, LANES), tiles[0].dtype))
    return jnp.concatenate(tiles, axis=0)


def _unpack(buf, like, names):
    out, row = {}, 0
    for n in names:
        rows = _rows_of(like[n].size)
        shape = tuple(like[n].shape[i] for i in SMALL_VIEW.get(n, range(like[n].ndim)))
        out[n] = buf[row:row + rows].reshape(-1)[:like[n].size].reshape(shape)
        row += rows
    return out, buf[row:]


def kernel(x, g_mix, w_in, A_re, A_im, log_dt, B_re, B_im, C_re, C_im, D_skip, w_glu, b_glu, w_pool, pool_scale, sgu_ln_g, sgu_ln_b, w_spatial, b_spatial, w_out, g_ffn, w_gate, w_up, w_down, g_final, loss_target, m_g_mix, m_w_in, m_A_re, m_A_im, m_log_dt, m_B_re, m_B_im, m_C_re, m_C_im, m_D_skip, m_w_glu, m_b_glu, m_w_pool, m_pool_scale, m_sgu_ln_g, m_sgu_ln_b, m_w_spatial, m_b_spatial, m_w_out, m_g_ffn, m_w_gate, m_w_up, m_w_down, m_g_final, v_g_mix, v_w_in, v_A_re, v_A_im, v_log_dt, v_B_re, v_B_im, v_C_re, v_C_im, v_D_skip, v_w_glu, v_b_glu, v_w_pool, v_pool_scale, v_sgu_ln_g, v_sgu_ln_b, v_w_spatial, v_b_spatial, v_w_out, v_g_ffn, v_w_gate, v_w_up, v_w_down, v_g_final):
    loc = locals()
    w = {n: loc[n] for n in WEIGHTS}
    m = {n: loc["m_" + n] for n in WEIGHTS}
    v = {n: loc["v_" + n] for n in WEIGHTS}
    sel = jnp.stack([lax.axis_index("c"), 2 * lax.axis_index("x") + lax.axis_index("y")]).astype(jnp.int32)

    chip = sel[1]

    halves = [(l, half) for l in range(DEPTH) for half in ("mix", "ffn")]
    two_level = {(0, "ffn")}
    names = {"mix": MIX_WEIGHTS, "ffn": FFN_WEIGHTS}
    started = {}
    wx = {n: _exchange_form(n, w[n]) for n in BIG}
    chain = []
    for l, half in halves:
        lands = _place_shards([wx[n] for n in names[half]], l, sel, chain, f"l{l}_{half}")
        plan = _gather_half_plan if (l, half) in two_level else _gather_plan
        started[l, half] = _copies_start(f"weights_l{l}_{half}", plan, [], lands, N_REL * len(lands))
        chain = [started[l, half]["token"]]
    w = dict(w, g_mix=_after(w["g_mix"], started[halves[-1]]["token"]))

    def get_big(l, half, after):
        lands = _copies_wait(started[l, half], after)[1]
        if (l, half) in two_level:
            lands = _forward_halves(lands, f"l{l}_{half}")
        return dict(zip(names[half], lands))

    result = {n: None for n in BIG}
    stage = {"swap": None, "slabs": None}

    def advance(after):
        if stage["slabs"] is not None:
            ex, ns, l, tag = stage["slabs"]
            part, slabs = _copies_wait(ex, after)
            bufs = _add_chips(part, slabs, [result[n] for n in ns], l, sel, tag)
            for n, f in zip(ns, _share_halves(bufs, l, tag)):
                result[n] = f
            stage["slabs"] = None
        if stage["swap"] is None:
            return []
        sw, ns, l, tag = stage["swap"]
        part = _add_halves(*_copies_wait(sw, after), sel, tag)
        slabs = [lax.empty((N_REL,) + p.shape[1:], BF16) for p in part]
        ex = _copies_start(f"grads_{tag}", _slab_plan, part, slabs, N_REL * len(part))
        stage["slabs"], stage["swap"] = (ex, ns, l, tag), None
        return [ex["token"]]

    def on_grads(l, half, grads):
        ns = list(grads)
        tag = f"l{l}_{half}"
        tokens = advance([grads[ns[0]]])
        g4s = [grads[n].reshape(N_CHIPS, 2, grads[n].shape[0] // (2 * N_CHIPS), grads[n].shape[1]) for n in ns]
        recvs = [lax.empty((N_CHIPS,) + g4.shape[2:], F32) for g4 in g4s]
        sw = _copies_start(f"swap_{tag}", _sibling_plan, g4s, recvs, len(g4s))
        stage["swap"] = (sw, ns, l, tag)
        return tokens + [sw["token"]]

    small = {}

    def on_small(g, loss_local):
        me = 2 * chip + sel[0]
        blocks = [_pack(g, COARSE).astype(BF16), _pack(g, PRECISE, loss_local)]
        lands = [lax.dynamic_update_slice(lax.empty((N_DEV,) + b.shape, b.dtype), b[None], (me, 0, 0)) for b in blocks]
        small.update(_copies_start("small_grads", _everyone_plan, blocks, lands, (N_DEV - 1) * len(blocks)))
        return [small["token"]]

    dx = _local_step(x[0], loss_target[0], w, dict(get_big=get_big, on_grads=on_grads, tick=advance, on_small=on_small))
    grads, deltas, new_m, new_v = {}, {}, {}, {}

    def update_big(ns, after):
        for n in ns:
            outs = _adamw(wx[n], result[n], _exchange_form(n, m[n]), _exchange_form(n, v[n]), n, after)
            grads[n], deltas[n], new_m[n], new_v[n] = [_exchange_form(n, a) for a in (result[n], *outs)]
            after = [outs[-1]]
        return after

    last_swap = stage["swap"][0]["token"]
    last_sent = advance(update_big(FFN_WEIGHTS[:1], [last_swap]))
    advance(update_big(FFN_WEIGHTS[1:], last_sent))
    _, gathered = _copies_wait(small, update_big(MIX_WEIGHTS, []))
    coarse, precise = _small_reduce(gathered)
    small_g, _ = _unpack(coarse, w, COARSE)
    precise_g, rest = _unpack(precise, w, PRECISE)
    small_g.update(precise_g)
    loss = rest[0, 0]
    views = [{n: _view(n, src[n]) for n in small_g} for src in (w, m, v)]
    for store, vals in zip((grads, deltas, new_m, new_v), (small_g, *_small_adamw(small_g, *views))):
        store.update({n: _view(n, a) for n, a in vals.items()})
    return (loss, dx[None], *[grads[n] for n in WEIGHTS], *[deltas[n] for n in WEIGHTS],
            *[new_m[n] for n in WEIGHTS], *[new_v[n] for n in WEIGHTS])
```

```python
import math

import jax
import jax.numpy as jnp
from jax import lax
from jax.experimental import pallas as pl
from jax.experimental.pallas import tpu as pltpu

F32 = jnp.float32
BF16 = jnp.bfloat16

D_MODEL = 1024
DEPTH = 2
D_SSM = 384
SSM_GROUP = 16
N_GROUPS = 24
SSM_STATE = 64
N_STATE = N_GROUPS * SSM_STATE
POOL_WINDOWS = (2, 4, 8, 16)
POOL_GROUP = 64
D_POOL = 256
MAX_WINDOW = 16
SGU_HEADS = 6
SGU_HEAD_DIM = 64
D_SGU = 384
CHUNK = 128
D_IN = D_SSM + D_POOL + 2 * D_SGU
D_FF = 2816
EPS = 1e-6

ADAM_LR = 0.001
ADAM_B1 = 0.9
ADAM_B2 = 0.999
ADAM_EPS = 1e-08
ADAM_WD = 0.01
ADAM_STEP = 10

LANES = 128
SUBLANES = 8
VMEM_LIMIT = 56 * 1024 * 1024

TS = 512
TS_FFN = 256

WEIGHTS = ['g_mix', 'w_in', 'A_re', 'A_im', 'log_dt', 'B_re', 'B_im', 'C_re', 'C_im', 'D_skip', 'w_glu', 'b_glu',
           'w_pool', 'pool_scale', 'sgu_ln_g', 'sgu_ln_b', 'w_spatial', 'b_spatial', 'w_out', 'g_ffn', 'w_gate',
           'w_up', 'w_down', 'g_final']
BIG = ['w_in', 'w_glu', 'w_out', 'w_gate', 'w_up', 'w_down']
SMALL = [n for n in WEIGHTS if n not in BIG]
TRANSPOSED = ("w_in", "w_gate", "w_up")
N_CHIPS = 4
N_DEV = 8


def _cp(**kw):
    return pltpu.CompilerParams(vmem_limit_bytes=VMEM_LIMIT, **kw)


def _row(ts, n):
    return pl.BlockSpec((ts, n), lambda i: (i, 0))


def _const(shape):
    nd = len(shape)
    return pl.BlockSpec(shape, lambda i: (0,) * nd, pipeline_mode=pl.Buffered(1))


def _acc(shape):
    nd = len(shape)
    return pl.BlockSpec(shape, lambda i: (0,) * nd)


def _following(body, after):
    k = len(after)
    return lambda *refs: body(*refs[k:])


def _anywhere(after):
    return [pl.BlockSpec(memory_space=pl.ANY)] * len(after)


def _dot(a, b):
    return jnp.dot(a, b, preferred_element_type=F32)


def _dot_tn(a, b):
    return lax.dot_general(a, b, (((0,), (0,)), ((), ())), preferred_element_type=F32)


def _dot_nt(a, b):
    return lax.dot_general(a, b, (((1,), (1,)), ((), ())), preferred_element_type=F32)


_G0 = math.sqrt(2.0 / math.pi)
_G1 = 0.044715


def _gelu(x):
    return 0.5 * x * (1.0 + jnp.tanh(_G0 * (x + _G1 * x * x * x)))


def _gelu_and_grad(x):
    t = jnp.tanh(_G0 * (x + _G1 * x * x * x))
    half = 0.5 * (1.0 + t)
    return x * half, half + 0.5 * x * (1.0 - t * t) * (_G0 * (1.0 + 3.0 * _G1 * x * x))


def _sigmoid(x):
    return 1.0 / (1.0 + jnp.exp(-x))


def _rms(x):
    r = lax.rsqrt(jnp.mean(x * x, axis=-1, keepdims=True) + EPS)
    return x * r, r


def _rms_bwd(dh, n, r, g):
    dn = dh * g
    return r * (dn - n * jnp.mean(dn * n, axis=-1, keepdims=True)), dh * n


def _colsum8(v):
    rows, n = v.shape
    return jnp.sum(v.reshape(rows // SUBLANES, SUBLANES, n), axis=0)


def _cmul(ar, ai, br, bi):
    return ar * br - ai * bi, ar * bi + ai * br


def _cpow(ar, ai, n):
    assert n & (n - 1) == 0
    while n > 1:
        ar, ai = _cmul(ar, ai, ar, ai)
        n //= 2
    return ar, ai


N_USLAB = D_SSM // LANES
SEG = TS // SUBLANES
SLAB_STATES = N_STATE // N_USLAB
S5_IN = (N_USLAB, LANES, 2 * SLAB_STATES)
S5_OUT = (N_USLAB, 2 * SLAB_STATES, LANES)
STATE_TILE = (SUBLANES, N_STATE)


def _scan_order():
    p = jnp.arange(TS)
    src = (p % SUBLANES) * SEG + p // SUBLANES
    return (src[:, None] == jnp.arange(TS)[None, :]).astype(BF16)


def _to_scan_order(perm, v):
    hi = v.astype(BF16)
    lo = (v - hi.astype(F32)).astype(BF16)
    return _dot(perm, hi) + _dot(perm, lo)


def _scan_rows(k):
    return pl.ds(pl.multiple_of(k * SUBLANES, SUBLANES), SUBLANES)


def _lanes(v, j):
    return v[:, LANES * j:LANES * (j + 1)]


def _states(j):
    return pl.ds(SLAB_STATES * j, SLAB_STATES)


def _state_split(re_ref, im_ref, j, v):
    re_ref[:, _states(j)] = v[:, :SLAB_STATES]
    im_ref[:, _states(j)] = v[:, SLAB_STATES:]


def _state_cat(re_ref, im_ref, j):
    return jnp.concatenate([re_ref[:, _states(j)], im_ref[:, _states(j)]], axis=1).astype(BF16)


def _s5_fwd(u, p, tag):
    s = u.shape[0]
    seg = SEG

    def body(u_ref, perm_ref, bbt_ref, ar_ref, ai_ref, ct_ref, dsk_ref, wglu_ref, bglu_ref,
             oa_ref, y_ref, hr_ref, hi_ref, sr, si, er, ei, ir, ii, cr, ci):
        @pl.when(pl.program_id(0) == 0)
        def _():
            cr[...] = jnp.zeros_like(cr)
            ci[...] = jnp.zeros_like(ci)

        perm = perm_ref[...]
        uv = _to_scan_order(perm, u_ref[...])
        ub = uv.astype(BF16)
        for j in range(N_USLAB):
            _state_split(sr, si, j, _dot(_lanes(ub, j), bbt_ref[j]))
        for j in range(N_USLAB):
            cols = _states(j)
            ar = ar_ref[:, cols]
            ai = ai_ref[:, cols]
            h_r = h_i = jnp.zeros((SUBLANES, SLAB_STATES), F32)
            for k in range(seg):
                rows = pl.ds(SUBLANES * k, SUBLANES)
                n_r, n_i = _cmul(ar, ai, h_r, h_i)
                h_r = n_r + sr[rows, cols]
                h_i = n_i + si[rows, cols]
            er[:, cols] = h_r
            ei[:, cols] = h_i
            pr, pi = _cpow(ar[0:1, :], ai[0:1, :], seg)
            c_r = cr[:, cols]
            c_i = ci[:, cols]
            for q in range(SUBLANES):
                ir[q:q + 1, cols] = c_r
                ii[q:q + 1, cols] = c_i
                n_r, n_i = _cmul(pr, pi, c_r, c_i)
                c_r = n_r + er[q:q + 1, cols]
                c_i = n_i + ei[q:q + 1, cols]
            cr[:, cols] = c_r
            ci[:, cols] = c_i
            h_r = ir[:, cols]
            h_i = ii[:, cols]
            for k in range(seg):
                rows = pl.ds(SUBLANES * k, SUBLANES)
                n_r, n_i = _cmul(ar, ai, h_r, h_i)
                h_r = n_r + sr[rows, cols]
                h_i = n_i + si[rows, cols]
                sr[rows, cols] = h_r
                si[rows, cols] = h_i
        hr_ref[...] = sr[...].astype(BF16)
        hi_ref[...] = si[...].astype(BF16)
        y = jnp.concatenate([_dot(_state_cat(hr_ref, hi_ref, j), ct_ref[j]) for j in range(N_USLAB)], axis=1)
        y = y + dsk_ref[...] * uv
        y_ref[...] = y
        g = _gelu(y)
        pre = _dot(g.astype(BF16), wglu_ref[...]) + bglu_ref[...]
        oa_ref[...] = _dot_tn(perm, (g * _sigmoid(pre)).astype(BF16)).astype(BF16)

    return pl.pallas_call(
        body, grid=(s // TS,),
        in_specs=[_row(TS, D_SSM), _const((TS, TS)), _const(S5_IN), _const(STATE_TILE), _const(STATE_TILE),
                  _const(S5_OUT), _const((1, D_SSM)), _const((D_SSM, D_SSM)), _const((1, D_SSM))],
        out_specs=[_row(TS, D_SSM), _row(TS, D_SSM), _row(TS, N_STATE), _row(TS, N_STATE)],
        out_shape=[jax.ShapeDtypeStruct((s, D_SSM), BF16), jax.ShapeDtypeStruct((s, D_SSM), F32),
                   jax.ShapeDtypeStruct((s, N_STATE), BF16), jax.ShapeDtypeStruct((s, N_STATE), BF16)],
        scratch_shapes=[pltpu.VMEM((TS, N_STATE), F32), pltpu.VMEM((TS, N_STATE), F32),
                        pltpu.VMEM(STATE_TILE, F32), pltpu.VMEM(STATE_TILE, F32), pltpu.VMEM(STATE_TILE, F32),
                        pltpu.VMEM(STATE_TILE, F32), pltpu.VMEM((1, N_STATE), F32), pltpu.VMEM((1, N_STATE), F32)],
        name=f"s5_fwd_{tag}", compiler_params=_cp(dimension_semantics=("arbitrary",)),
    )(u, _scan_order(), p["bbt3"], p["a_re8"], p["a_im8"], p["ct3"], p["d_skip"], p["w_glu"], p["b_glu"])


def _pool_consts():
    w = jnp.repeat(jnp.asarray(POOL_WINDOWS, F32), POOL_GROUP)[None, :]
    return w


POOL_PAD = SUBLANES
POOL_ROWS = TS + MAX_WINDOW + POOL_PAD


def _window_sum(buf, tmp, first, wl, step):
    assert POOL_WINDOWS == (2, 4, 8, 16)
    n = TS + MAX_WINDOW
    lo = first - MAX_WINDOW if step < 0 else first
    src = buf
    for k, dst in zip((1, 2, 4), tmp):
        dst[pl.ds(lo, n), :] = src[pl.ds(lo, n), :] + src[pl.ds(lo + step * k, n), :]
        src = dst
    s2, s4, s8 = (t[pl.ds(first, TS), :] for t in tmp)
    s16 = s8 + tmp[2][pl.ds(first + step * 8, TS), :]
    return jnp.where(wl == 2, s2, jnp.where(wl == 4, s4, jnp.where(wl == 8, s8, s16)))


def _pool_count(i, rows, wl):
    t = (i * TS + 1).astype(F32) + lax.broadcasted_iota(jnp.int32, (rows, 1), 0).astype(F32)
    return jnp.minimum(t, wl)


def _sgu_mix(vl, wpair_ref, lo, hi):
    rows = vl.shape[0]
    chunks = []
    for c in range(rows // CHUNK):
        vc = vl[CHUNK * c:CHUNK * (c + 1), :]
        parts = []
        for q in range(SGU_HEADS // 2):
            vq = vc[:, LANES * q:LANES * (q + 1)]
            rhs = jnp.concatenate([vq * lo, vq * hi], axis=0).astype(BF16)
            parts.append(_dot(wpair_ref[q], rhs))
        chunks.append(jnp.concatenate(parts, axis=1))
    return jnp.concatenate(chunks, axis=0)


def _sgu_front(zuv, lng, lnb, grads=False):
    gelu = _gelu_and_grad if grads else lambda z: (_gelu(z), None)
    u, du = gelu(zuv[:, :D_SGU])
    v, dv = gelu(zuv[:, D_SGU:])
    mu = jnp.mean(v, axis=-1, keepdims=True)
    vc = v - mu
    rs = lax.rsqrt(jnp.mean(vc * vc, axis=-1, keepdims=True) + EPS)
    vn = vc * rs
    return u, vn, rs, vn * lng + lnb, du, dv


def _half_masks():
    lane = lax.broadcasted_iota(jnp.int32, (1, LANES), 1)
    lo = (lane < SGU_HEAD_DIM).astype(F32)
    return lo, 1.0 - lo


def _mix_fwd(x, p, tag):
    s = x.shape[0]

    def body(x_ref, g_ref, w_ref, wl_ref, wp_ref, sc_ref, lng_ref, lnb_ref, wsp_ref, bias_ref,
             za_ref, zuv_ref, h_ref, ob_ref, pooled_ref, oc_ref, buf, *tmp):
        i = pl.program_id(0)
        tile0 = POOL_PAD + MAX_WINDOW

        @pl.when(i == 0)
        def _():
            for ref in (buf, *tmp):
                ref[pl.ds(0, tile0), :] = jnp.zeros((tile0, D_POOL), F32)

        n, _ = _rms(x_ref[...])
        h = (n * g_ref[...]).astype(BF16)
        h_ref[...] = h
        z = _dot_nt(h, w_ref[...])
        za_ref[...] = z[:, :D_SSM]
        zb = z[:, D_SSM:D_SSM + D_POOL]
        zuv = z[:, D_SSM + D_POOL:]
        zuv_ref[...] = zuv
        buf[pl.ds(tile0, TS), :] = zb
        wl = wl_ref[...]
        pooled = (_window_sum(buf, tmp, tile0, wl, -1) / _pool_count(i, TS, wl) - zb).astype(BF16)
        buf[pl.ds(POOL_PAD, MAX_WINDOW), :] = zb[TS - MAX_WINDOW:, :]
        pooled_ref[...] = pooled
        ob_ref[...] = (_dot(pooled, wp_ref[...]) * sc_ref[...]).astype(BF16)
        lo, hi = _half_masks()
        u, _, _, vl, _, _ = _sgu_front(zuv, lng_ref[...], lnb_ref[...])
        mixed = _sgu_mix(vl, wsp_ref, lo, hi) + jnp.tile(bias_ref[...], (TS // CHUNK, 1))
        oc_ref[...] = (u * mixed).astype(BF16)

    return pl.pallas_call(
        body, grid=(s // TS,),
        in_specs=[_row(TS, D_MODEL), _const((1, D_MODEL)), _const((D_IN, D_MODEL)), _const((1, D_POOL)),
                  _const((D_POOL, D_POOL)), _const((1, D_POOL)), _const((1, D_SGU)), _const((1, D_SGU)),
                  _const((SGU_HEADS // 2, CHUNK, 2 * CHUNK)), _const((CHUNK, D_SGU))],
        out_specs=[_row(TS, D_SSM), _row(TS, 2 * D_SGU), _row(TS, D_MODEL), _row(TS, D_POOL), _row(TS, D_POOL),
                   _row(TS, D_SGU)],
        out_shape=[jax.ShapeDtypeStruct((s, D_SSM), F32), jax.ShapeDtypeStruct((s, 2 * D_SGU), F32),
                   jax.ShapeDtypeStruct((s, D_MODEL), BF16), jax.ShapeDtypeStruct((s, D_POOL), BF16),
                   jax.ShapeDtypeStruct((s, D_POOL), BF16), jax.ShapeDtypeStruct((s, D_SGU), BF16)],
        scratch_shapes=[pltpu.VMEM((POOL_ROWS, D_POOL), F32)] * 4,
        name=f"mix_fwd_{tag}", compiler_params=_cp(dimension_semantics=("arbitrary",)),
    )(x, p["g_mix"], p["w_in"], _pool_consts(), p["w_pool_bd"], p["pool_scale"], p["sgu_ln_g"], p["sgu_ln_b"],
      p["ws_pair"], p["bias_sp"])


def _blk_fwd(x0, oa, ob, oc, p, tag, head=None):
    s = x0.shape[0]
    ts = TS_FFN
    n_head = 0 if head is None else len(head)

    def body(x0_ref, oa_ref, ob_ref, oc_ref, wo_ref, g_ref, wg_ref, wu_ref, wd_ref, *refs):
        x1_ref, x2_ref, h2_ref, gt_ref, up_ref, ycat_ref = refs[n_head:n_head + 6]
        ycat = jnp.concatenate([oa_ref[...], ob_ref[...], oc_ref[...]], axis=1)
        ycat_ref[...] = ycat
        x1 = x0_ref[...] + _dot(ycat, wo_ref[...])
        x1_ref[...] = x1
        n, _ = _rms(x1)
        h2 = (n * g_ref[...]).astype(BF16)
        h2_ref[...] = h2
        gt = _dot_nt(h2, wg_ref[...])
        up = _dot_nt(h2, wu_ref[...])
        gt_ref[...] = gt.astype(BF16)
        up_ref[...] = up.astype(BF16)
        act = (gt * _sigmoid(gt) * up).astype(BF16)
        x2 = x1 + _dot(act, wd_ref[...])
        if head is None:
            x2_ref[...] = x2
            return
        t_ref, gf_ref = refs[:n_head]
        loss_ref, dgf_ref = refs[n_head + 6:]

        @pl.when(pl.program_id(0) == 0)
        def _():
            loss_ref[...] = jnp.zeros_like(loss_ref)
            dgf_ref[...] = jnp.zeros_like(dgf_ref)

        gf = gf_ref[...]
        nf, rf = _rms(x2)
        diff = nf * gf - t_ref[...]
        loss_ref[...] += jnp.sum(diff * diff) * (0.5 / D_MODEL)
        dxn, dgp = _rms_bwd(diff * (1.0 / D_MODEL), nf, rf, gf)
        dgf_ref[...] += _colsum8(dgp)
        x2_ref[...] = dxn

    in_specs = [_row(ts, D_MODEL), _row(ts, D_SSM), _row(ts, D_POOL), _row(ts, D_SGU), _const((D_MODEL, D_MODEL)),
                _const((1, D_MODEL)), _const((D_FF, D_MODEL)), _const((D_FF, D_MODEL)), _const((D_FF, D_MODEL))]
    out_specs = [_row(ts, D_MODEL), _row(ts, D_MODEL), _row(ts, D_MODEL), _row(ts, D_FF), _row(ts, D_FF),
                 _row(ts, D_MODEL)]
    out_shape = [jax.ShapeDtypeStruct((s, D_MODEL), F32), jax.ShapeDtypeStruct((s, D_MODEL), F32),
                 jax.ShapeDtypeStruct((s, D_MODEL), BF16), jax.ShapeDtypeStruct((s, D_FF), BF16),
                 jax.ShapeDtypeStruct((s, D_FF), BF16), jax.ShapeDtypeStruct((s, D_MODEL), BF16)]
    args = (x0, oa, ob, oc, p["w_out"], p["g_ffn"], p["w_gate"], p["w_up"], p["w_down"])
    if head is not None:
        in_specs += [_row(ts, D_MODEL), _const((1, D_MODEL))]
        out_specs += [_acc((SUBLANES, LANES)), _acc((SUBLANES, D_MODEL))]
        out_shape += [jax.ShapeDtypeStruct((SUBLANES, LANES), F32), jax.ShapeDtypeStruct((SUBLANES, D_MODEL), F32)]
        args += tuple(head)
    return pl.pallas_call(
        body, grid=(s // ts,), in_specs=in_specs, out_specs=out_specs, out_shape=out_shape,
        name=f"blk_fwd_{tag}", compiler_params=_cp(dimension_semantics=("arbitrary",)),
    )(*args)


def _blk_bwd(dx2, x1, gt, up, p, tag, after=()):
    s = dx2.shape[0]
    ts = TS_FFN

    def body(dx2_ref, x1_ref, gt_ref, up_ref, wd_ref, wgt_ref, wut_ref, wo_ref, g_ref,
             dx1_ref, da_ref, db_ref, dc_ref, dgt_ref, dup_ref, act_ref, dg_ref):
        @pl.when(pl.program_id(0) == 0)
        def _():
            dg_ref[...] = jnp.zeros_like(dg_ref)

        dx2v = dx2_ref[...]
        dact = _dot_nt(dx2v.astype(BF16), wd_ref[...])
        gf = gt_ref[...].astype(F32)
        uf = up_ref[...].astype(F32)
        sg = _sigmoid(gf)
        sl = gf * sg
        act_ref[...] = (sl * uf).astype(BF16)
        dgt = (dact * uf * (sg * (1.0 + gf * (1.0 - sg)))).astype(BF16)
        dup = (dact * sl).astype(BF16)
        dgt_ref[...] = dgt
        dup_ref[...] = dup
        dh2 = _dot(dgt, wgt_ref[...]) + _dot(dup, wut_ref[...])
        n, r = _rms(x1_ref[...])
        dxn, dgp = _rms_bwd(dh2, n, r, g_ref[...])
        dg_ref[...] += _colsum8(dgp)
        dx1 = dx2v + dxn
        dx1_ref[...] = dx1
        dy = _dot_nt(dx1.astype(BF16), wo_ref[...])
        da_ref[...] = dy[:, :D_SSM]
        db_ref[...] = dy[:, D_SSM:D_SSM + D_POOL]
        dc_ref[...] = dy[:, D_SSM + D_POOL:]

    return pl.pallas_call(
        _following(body, after), grid=(s // ts,),
        in_specs=_anywhere(after) + [_row(ts, D_MODEL), _row(ts, D_MODEL), _row(ts, D_FF), _row(ts, D_FF),
                  _const((D_FF, D_MODEL)), _const((D_FF, D_MODEL)), _const((D_FF, D_MODEL)),
                  _const((D_MODEL, D_MODEL)), _const((1, D_MODEL))],
        out_specs=[_row(ts, D_MODEL), _row(ts, D_SSM), _row(ts, D_POOL), _row(ts, D_SGU), _row(ts, D_FF),
                   _row(ts, D_FF), _row(ts, D_FF), _acc((SUBLANES, D_MODEL))],
        out_shape=[jax.ShapeDtypeStruct((s, D_MODEL), F32), jax.ShapeDtypeStruct((s, D_SSM), F32),
                   jax.ShapeDtypeStruct((s, D_POOL), F32), jax.ShapeDtypeStruct((s, D_SGU), F32),
                   jax.ShapeDtypeStruct((s, D_FF), BF16), jax.ShapeDtypeStruct((s, D_FF), BF16),
                   jax.ShapeDtypeStruct((s, D_FF), BF16), jax.ShapeDtypeStruct((SUBLANES, D_MODEL), F32)],
        name=f"blk_bwd_{tag}", compiler_params=_cp(dimension_semantics=("arbitrary",)),
    )(*after, dx2, x1, gt, up, p["w_down"], p["w_gate"], p["w_up"], p["w_out"], p["g_ffn"])


def _s5_bwd(dout, u, y, h_re, h_im, p, tag, after=()):
    s = u.shape[0]
    nt = s // TS
    seg = SEG

    def rev(n):
        return pl.BlockSpec((TS, n), lambda i: (nt - 1 - i, 0))

    def body(do_ref, u_ref, y_ref, hr_ref, hi_ref, perm_ref, ar_ref, ai_ref, cb_ref, bb_ref, dsk_ref,
             wglu_ref, bglu_ref,
             du_ref, dct_ref, dbb_ref, dar_ref, dai_ref, dd_ref, dwglu_ref, dbglu_ref,
             gr, gi, hsr, hsi, er, ei, jr, ji, cr, ci):
        @pl.when(pl.program_id(0) == 0)
        def _():
            for ref in (cr, ci, dct_ref, dbb_ref, dar_ref, dai_ref, dd_ref, dwglu_ref, dbglu_ref):
                ref[...] = jnp.zeros_like(ref)

        perm = perm_ref[...]
        uv = _to_scan_order(perm, u_ref[...])
        yv = y_ref[...]
        dov = _to_scan_order(perm, do_ref[...])
        g, gelu_dy = _gelu_and_grad(yv)
        gb = g.astype(BF16)
        sg = _sigmoid(_dot(gb, wglu_ref[...]) + bglu_ref[...])
        dpre = dov * g * sg * (1.0 - sg)
        dpb = dpre.astype(BF16)
        dwglu_ref[...] += _dot_tn(gb, dpb)
        dbglu_ref[...] += _colsum8(dpre)
        dy = (dov * sg + _dot_nt(dpb, wglu_ref[...])) * gelu_dy
        dd_ref[...] += _colsum8(dy * uv)
        dyb = dy.astype(BF16)
        hsr[...] = hr_ref[...].astype(F32)
        hsi[...] = hi_ref[...].astype(F32)
        ub = uv.astype(BF16)
        dus = []

        def state_cotangents(j):
            dct_ref[j] += _dot_tn(_lanes(dyb, j), _state_cat(hr_ref, hi_ref, j))
            _state_split(gr, gi, j, _dot(_lanes(dyb, j), cb_ref[j]))

        def input_cotangents(j):
            gb_j = _state_cat(gr, gi, j)
            dbb_ref[j] += _dot_tn(_lanes(ub, j), gb_j)
            dus.append(_dot(gb_j, bb_ref[j]))

        def scan(j):
            cols = _states(j)
            ar = ar_ref[:, cols]
            ai = -ai_ref[:, cols]
            g_r = g_i = jnp.zeros((SUBLANES, SLAB_STATES), F32)
            for k in range(seg - 1, -1, -1):
                rows = pl.ds(SUBLANES * k, SUBLANES)
                n_r, n_i = _cmul(ar, ai, g_r, g_i)
                g_r = n_r + gr[rows, cols]
                g_i = n_i + gi[rows, cols]
            er[:, cols] = g_r
            ei[:, cols] = g_i
            pr, pi = _cpow(ar[0:1, :], ai[0:1, :], seg)
            c_r = cr[:, cols]
            c_i = ci[:, cols]
            for q in range(SUBLANES - 1, -1, -1):
                jr[q:q + 1, cols] = c_r
                ji[q:q + 1, cols] = c_i
                n_r, n_i = _cmul(pr, pi, c_r, c_i)
                c_r = n_r + er[q:q + 1, cols]
                c_i = n_i + ei[q:q + 1, cols]
            cr[:, cols] = c_r
            ci[:, cols] = c_i
            g_r = jr[:, cols]
            g_i = ji[:, cols]
            a_r = a_i = jnp.zeros((SUBLANES, SLAB_STATES), F32)
            for k in range(seg - 1, -1, -1):
                rows = pl.ds(SUBLANES * k, SUBLANES)
                h_r = hsr[rows, cols]
                h_i = hsi[rows, cols]
                a_r = a_r + g_r * h_r + g_i * h_i
                a_i = a_i + g_i * h_r - g_r * h_i
                n_r, n_i = _cmul(ar, ai, g_r, g_i)
                g_r = n_r + gr[rows, cols]
                g_i = n_i + gi[rows, cols]
                gr[rows, cols] = g_r
                gi[rows, cols] = g_i
            dar_ref[:, cols] += a_r
            dai_ref[:, cols] += a_i

        for stage in (state_cotangents, scan, input_cotangents):
            for j in range(N_USLAB):
                stage(j)
        du = dy * dsk_ref[...] + jnp.concatenate(dus, axis=1)
        du_ref[...] = _dot_tn(perm, du.astype(BF16)).astype(BF16)

    big = (TS, N_STATE)
    return pl.pallas_call(
        _following(body, after), grid=(nt,),
        in_specs=_anywhere(after) + [rev(D_SSM), rev(D_SSM), rev(D_SSM), rev(N_STATE), rev(N_STATE), _const((TS, TS)),
                  _const(STATE_TILE), _const(STATE_TILE), _const(S5_IN), _const(S5_OUT), _const((1, D_SSM)),
                  _const((D_SSM, D_SSM)), _const((1, D_SSM))],
        out_specs=[rev(D_SSM), _acc(S5_IN), _acc(S5_IN), _acc(STATE_TILE), _acc(STATE_TILE),
                   _acc((SUBLANES, D_SSM)), _acc((D_SSM, D_SSM)), _acc((SUBLANES, D_SSM))],
        out_shape=[jax.ShapeDtypeStruct((s, D_SSM), BF16), jax.ShapeDtypeStruct(S5_IN, F32),
                   jax.ShapeDtypeStruct(S5_IN, F32), jax.ShapeDtypeStruct(STATE_TILE, F32),
                   jax.ShapeDtypeStruct(STATE_TILE, F32), jax.ShapeDtypeStruct((SUBLANES, D_SSM), F32),
                   jax.ShapeDtypeStruct((D_SSM, D_SSM), F32), jax.ShapeDtypeStruct((SUBLANES, D_SSM), F32)],
        scratch_shapes=[pltpu.VMEM(big, F32), pltpu.VMEM(big, F32), pltpu.VMEM(big, F32), pltpu.VMEM(big, F32),
                        pltpu.VMEM(STATE_TILE, F32), pltpu.VMEM(STATE_TILE, F32), pltpu.VMEM(STATE_TILE, F32),
                        pltpu.VMEM(STATE_TILE, F32), pltpu.VMEM((1, N_STATE), F32), pltpu.VMEM((1, N_STATE), F32)],
        name=f"s5_bwd_{tag}", compiler_params=_cp(dimension_semantics=("arbitrary",)),
    )(*after, dout, u, y, h_re, h_im, _scan_order(), p["a_re8"], p["a_im8"], p["cb3"], p["bb3"], p["d_skip"], p["w_glu"],
      p["b_glu"])


def _mix_bwd(dza, db, dc, pooled, zuv, x0, dx1, p, tag, after=()):
    s = x0.shape[0]
    nt = s // TS

    def rev(n):
        return pl.BlockSpec((TS, n), lambda i: (nt - 1 - i, 0))

    def body(da_ref, db_ref, dc_ref, po_ref, z_ref, x_ref, dx1_ref, wl_ref, wp_ref, wpt_ref, sc_ref, lng_ref, lnb_ref,
             wsp_ref, wspt_ref, bias_ref, win_ref, g_ref,
             dx0_ref, dz_ref, dwp_ref, dsc_ref, dws_ref, dbias_ref, dlng_ref, dlnb_ref, dg_ref, buf, *tmp):
        step = pl.program_id(0)
        i = nt - 1 - step

        @pl.when(step == 0)
        def _():
            for ref in (dwp_ref, dsc_ref, dws_ref, dbias_ref, dlng_ref, dlnb_ref, dg_ref):
                ref[...] = jnp.zeros_like(ref)
            for ref in (buf, *tmp):
                ref[pl.ds(TS, POOL_ROWS - TS), :] = jnp.zeros((POOL_ROWS - TS, D_POOL), F32)

        wl = wl_ref[...]
        sc = sc_ref[...]
        dob = db_ref[...]
        pooled_b = po_ref[...]
        dsc_ref[...] += _colsum8(dob * _dot(pooled_b, wp_ref[...]))
        dmixb = (dob * sc).astype(BF16)
        dwp_ref[...] += _dot_tn(pooled_b, dmixb)
        dpool = _dot(dmixb, wpt_ref[...])
        dq = dpool / _pool_count(i, TS, wl)
        buf[pl.ds(0, TS), :] = dq
        dzb = _window_sum(buf, tmp, 0, wl, 1) - dpool
        buf[pl.ds(TS, MAX_WINDOW), :] = dq[:MAX_WINDOW, :]

        lo, hi = _half_masks()
        lng = lng_ref[...]
        u, vn, rs, vl, gelu_du, gelu_dv = _sgu_front(z_ref[...], lng, lnb_ref[...], grads=True)
        mixed = _sgu_mix(vl, wsp_ref, lo, hi) + jnp.tile(bias_ref[...], (TS // CHUNK, 1))
        doc = dc_ref[...]
        dzu = doc * mixed * gelu_du
        dmix = doc * u
        dbias = dbias_ref[...]
        for c in range(TS // CHUNK):
            dmc = dmix[CHUNK * c:CHUNK * (c + 1), :]
            dbias = dbias + dmc
            vlc = vl[CHUNK * c:CHUNK * (c + 1), :].astype(BF16)
            for q in range(SGU_HEADS // 2):
                dmq = _lanes(dmc, q)
                vq = _lanes(vlc, q)
                dws_ref[2 * q] += _dot_nt((dmq * lo).astype(BF16), vq)
                dws_ref[2 * q + 1] += _dot_nt((dmq * hi).astype(BF16), vq)
        dbias_ref[...] = dbias
        dvl = _sgu_mix(dmix, wspt_ref, lo, hi)
        dlng_ref[...] += _colsum8(dvl * vn)
        dlnb_ref[...] += _colsum8(dvl)
        dvn = dvl * lng
        dv = rs * (dvn - jnp.mean(dvn, axis=-1, keepdims=True) - vn * jnp.mean(dvn * vn, axis=-1, keepdims=True))

        dz = jnp.concatenate([da_ref[...], dzb.astype(BF16), dzu.astype(BF16), (dv * gelu_dv).astype(BF16)], axis=1)
        dz_ref[...] = dz
        n, r = _rms(x_ref[...])
        dxn, dgp = _rms_bwd(_dot(dz, win_ref[...]), n, r, g_ref[...])
        dg_ref[...] += _colsum8(dgp)
        dx0_ref[...] = dx1_ref[...] + dxn

    pair = (SGU_HEADS // 2, CHUNK, 2 * CHUNK)
    return pl.pallas_call(
        _following(body, after), grid=(nt,),
        in_specs=_anywhere(after) + [rev(D_SSM), rev(D_POOL), rev(D_SGU), rev(D_POOL), rev(2 * D_SGU), rev(D_MODEL), rev(D_MODEL),
                  _const((1, D_POOL)), _const((D_POOL, D_POOL)), _const((D_POOL, D_POOL)), _const((1, D_POOL)),
                  _const((1, D_SGU)), _const((1, D_SGU)), _const(pair), _const(pair), _const((CHUNK, D_SGU)),
                  _const((D_IN, D_MODEL)), _const((1, D_MODEL))],
        out_specs=[rev(D_MODEL), rev(D_IN), _acc((D_POOL, D_POOL)), _acc((SUBLANES, D_POOL)),
                   _acc((SGU_HEADS, CHUNK, CHUNK)), _acc((CHUNK, D_SGU)), _acc((SUBLANES, D_SGU)),
                   _acc((SUBLANES, D_SGU)), _acc((SUBLANES, D_MODEL))],
        out_shape=[jax.ShapeDtypeStruct((s, D_MODEL), F32), jax.ShapeDtypeStruct((s, D_IN), BF16),
                   jax.ShapeDtypeStruct((D_POOL, D_POOL), F32), jax.ShapeDtypeStruct((SUBLANES, D_POOL), F32),
                   jax.ShapeDtypeStruct((SGU_HEADS, CHUNK, CHUNK), F32), jax.ShapeDtypeStruct((CHUNK, D_SGU), F32),
                   jax.ShapeDtypeStruct((SUBLANES, D_SGU), F32), jax.ShapeDtypeStruct((SUBLANES, D_SGU), F32),
                   jax.ShapeDtypeStruct((SUBLANES, D_MODEL), F32)],
        scratch_shapes=[pltpu.VMEM((POOL_ROWS, D_POOL), F32)] * 4,
        name=f"mix_bwd_{tag}", compiler_params=_cp(dimension_semantics=("arbitrary",)),
    )(*after, dza, db, dc, pooled, zuv, x0, dx1, _pool_consts(), p["w_pool_bd"], p["w_pool_bd_t"], p["pool_scale"],
      p["sgu_ln_g"], p["sgu_ln_b"], p["ws_pair"], p["ws_pair_t"], p["bias_sp"], p["w_in"], p["g_mix"])


def _atb(a, b, tag, after=()):
    s, ka = a.shape
    kb = b.shape[1]
    ts = ATB_ROWS
    tn = min(kb, ATB_COLS)
    ns = s // ts

    def body(a_ref, b_ref, *rest):
        o_ref = rest[-1]

        @pl.when(pl.program_id(1) == 0)
        def _():
            o_ref[...] = jnp.zeros_like(o_ref)

        o_ref[...] += _dot_tn(a_ref[...].astype(BF16), b_ref[...].astype(BF16))

    return pl.pallas_call(
        body, grid=(kb // tn, ns),
        in_specs=[pl.BlockSpec((ts, ka), lambda j, i: (i, 0)), pl.BlockSpec((ts, tn), lambda j, i: (i, j))]
        + [pl.BlockSpec(memory_space=pl.ANY)] * len(after),
        out_specs=pl.BlockSpec((ka, tn), lambda j, i: (0, j)),
        out_shape=jax.ShapeDtypeStruct((ka, kb), F32),
        name=f"atb_{tag}", compiler_params=_cp(dimension_semantics=("arbitrary", "arbitrary")),
    )(a, b, *after)


def _s5_discretise(a_re, a_im, log_dt, b_re, b_im):
    dt = jnp.exp(log_dt)[:, None]
    mag = jnp.exp(a_re * dt)
    ar = mag * jnp.cos(a_im * dt)
    ai = mag * jnp.sin(a_im * dt)
    den = a_re * a_re + a_im * a_im
    f_re = ((ar - 1.0) * a_re + ai * a_im) / den
    f_im = (ai * a_re - (ar - 1.0) * a_im) / den
    bb_re = f_re[..., None] * b_re - f_im[..., None] * b_im
    bb_im = f_re[..., None] * b_im + f_im[..., None] * b_re
    return ar, ai, bb_re, bb_im


def _block_diag(blocks):
    g, r, c = blocks.shape
    eye = jnp.eye(g, dtype=blocks.dtype)
    return (blocks[:, :, None, :] * eye[:, None, :, None]).reshape(g * r, g * c)


def _block_diag_extract(m, g):
    r = m.shape[0] // g
    c = m.shape[1] // g
    eye = jnp.eye(g, dtype=m.dtype)
    return jnp.sum(m.reshape(g, r, g, c) * eye[:, None, :, None], axis=2)


GROUPS_PER_SLAB = N_GROUPS // N_USLAB


def _slab_diag(blocks):
    k = GROUPS_PER_SLAB
    _, r, c = blocks.shape
    eye = jnp.eye(k, dtype=blocks.dtype)
    spread = blocks.reshape(N_USLAB, k, r, 1, c) * eye[None, :, None, :, None]
    return spread.reshape(N_USLAB, k * r, k * c)


def _slab_diag_extract(m):
    k = GROUPS_PER_SLAB
    r, c = m.shape[1] // k, m.shape[2] // (2 * k)
    eye = jnp.eye(k, dtype=m.dtype)
    d = jnp.sum(m.reshape(N_USLAB, k, r, 2, k, c) * eye[None, :, None, None, :, None], axis=4)
    return d[:, :, :, 0].reshape(N_GROUPS, r, c), d[:, :, :, 1].reshape(N_GROUPS, r, c)


def _state_slabs(v):
    return jnp.broadcast_to(v.reshape(1, N_STATE), STATE_TILE)


def _tril():
    return jnp.tril(jnp.ones((CHUNK, CHUNK), dtype=bool))


def _layer_params(w, l):
    row = lambda v: v.reshape(1, -1)
    t = lambda m: jnp.swapaxes(m, -1, -2)
    ar, ai, bb_re, bb_im = _s5_discretise(w["A_re"][l], w["A_im"][l], w["log_dt"][l], w["B_re"][l], w["B_im"][l])
    bbt3 = jnp.concatenate([_slab_diag(t(bb_re)), _slab_diag(t(bb_im))], axis=2).astype(BF16)
    ct3 = jnp.concatenate([_slab_diag(t(w["C_re"][l])), -_slab_diag(t(w["C_im"][l]))], axis=1).astype(BF16)
    ws = jnp.where(_tril()[None], w["w_spatial"][l], 0.0)
    pair = lambda m: jnp.stack([jnp.concatenate([m[2 * q], m[2 * q + 1]], axis=1)
                                for q in range(SGU_HEADS // 2)]).astype(BF16)
    wp = _block_diag(w["w_pool"][l]).astype(BF16)
    p = dict(
        g_mix=row(w["g_mix"][l]), g_ffn=row(w["g_ffn"][l]), d_skip=row(w["D_skip"][l]), b_glu=row(w["b_glu"][l]),
        pool_scale=row(w["pool_scale"][l]), sgu_ln_g=row(w["sgu_ln_g"][l]), sgu_ln_b=row(w["sgu_ln_b"][l]),
        a_re8=_state_slabs(ar), a_im8=_state_slabs(ai),
        bbt3=bbt3, bb3=t(bbt3), ct3=ct3, cb3=t(ct3),
        w_pool_bd=wp, w_pool_bd_t=t(wp), ws_pair=pair(ws), ws_pair_t=pair(t(ws)),
        bias_sp=jnp.repeat(t(w["b_spatial"][l]), SGU_HEAD_DIM, axis=1),
    )
    return p


MIX_WEIGHTS = ("w_in", "w_glu")
FFN_WEIGHTS = ("w_out", "w_gate", "w_up", "w_down")


def _with_big(p, mats):
    p.update(mats)


def _rows_sum(v):
    return jnp.sum(v, axis=0)


ATB_COLS = 1024
ATB_ROWS = 1024


def _after(v, token):
    return v + token[0, 0]


def _layer_bwd(dx2, sv, p, w, l, tag, hooks, after):
    t = lambda m: jnp.swapaxes(m, -1, -2)
    dx1, da, db, dc, dgt, dup, act, dg_ffn = _blk_bwd(dx2, sv["x1"], sv["gt"], sv["up"], p, tag, after)
    after = hooks["tick"]([dx1])
    after = after + hooks["on_grads"](l, "ffn", {
        "w_down": _atb(act, dx2, tag + "_wd", after), "w_gate": _atb(dgt, sv["h2"], tag + "_wg", after),
        "w_up": _atb(dup, sv["h2"], tag + "_wu", after), "w_out": _atb(sv["ycat"], dx1, tag + "_wo", after)})
    g = {}
    g["g_ffn"] = _rows_sum(dg_ffn)
    dza, dc3, dbbt3, dar8, dai8, dd8, dwglu, dbglu8 = _s5_bwd(
        da, sv["za"], sv["y"], sv["h_re"], sv["h_im"], p, tag, after)
    after = hooks["tick"]([dza])
    dx0, dz, dwp, dsc8, dws, dbias, dlng8, dlnb8, dg_mix = _mix_bwd(
        dza, db, dc, sv["pooled"], sv["zuv"], sv["x0"], dx1, p, tag, after)
    g["g_mix"] = _rows_sum(dg_mix)
    g["b_glu"] = _rows_sum(dbglu8)
    g["D_skip"] = _rows_sum(dd8)
    dc_re, dc_im = _slab_diag_extract(dc3)
    g["C_re"] = dc_re
    g["C_im"] = -dc_im
    dar = jnp.sum(dar8, axis=0).reshape(N_GROUPS, SSM_STATE)
    dai = jnp.sum(dai8, axis=0).reshape(N_GROUPS, SSM_STATE)
    dbb_re, dbb_im = [t(d) for d in _slab_diag_extract(dbbt3)]
    _, disc_vjp = jax.vjp(_s5_discretise, w["A_re"][l], w["A_im"][l], w["log_dt"][l], w["B_re"][l], w["B_im"][l])
    g["A_re"], g["A_im"], g["log_dt"], g["B_re"], g["B_im"] = disc_vjp((dar, dai, dbb_re, dbb_im))
    g["w_pool"] = _block_diag_extract(dwp, len(POOL_WINDOWS))
    g["pool_scale"] = _rows_sum(dsc8)
    g["sgu_ln_g"] = _rows_sum(dlng8)
    g["sgu_ln_b"] = _rows_sum(dlnb8)
    g["w_spatial"] = jnp.where(_tril()[None], dws, 0.0)
    g["b_spatial"] = t(jnp.sum(dbias.reshape(CHUNK, SGU_HEADS, SGU_HEAD_DIM), axis=-1))
    after = hooks["on_small"](l, g)
    after = hooks["on_grads"](l, "mix", {"w_in": _atb(dz, sv["h1"], tag + "_wi", after), "w_glu": dwglu})
    return dx0, after


def _local_step(x, target, w, hooks):
    params = [_layer_params(w, l) for l in range(DEPTH)]
    saved = []
    h = x
    for l in range(DEPTH):
        p, tag = params[l], f"l{l}"
        _with_big(p, hooks["get_big"](l, "mix", [h]))
        za, zuv, h1, ob, pooled, oc = _mix_fwd(h, p, tag)
        oa, y, h_re, h_im = _s5_fwd(za, p, tag)
        _with_big(p, hooks["get_big"](l, "ffn", [oa, ob, oc]))
        head = (target, w["g_final"].reshape(1, -1)) if l == DEPTH - 1 else None
        x1, x2, h2, gt, up, ycat, *loss_parts = _blk_fwd(h, oa, ob, oc, p, tag, head)
        saved.append(dict(x0=h, za=za, zuv=zuv, h1=h1, ycat=ycat, y=y, h_re=h_re, h_im=h_im, pooled=pooled, x1=x1,
                          h2=h2, gt=gt, up=up))
        h = x2
    dx = h
    loss8, dgf8 = loss_parts
    grads = [None] * DEPTH

    def on_small(l, g_l):
        grads[l] = g_l
        if l > 0:
            return []
        g = {n: jnp.stack([grads[k][n] for k in range(DEPTH)]) for n in SMALL if n != "g_final"}
        g["g_final"] = _rows_sum(dgf8)
        return hooks["on_small"](g, loss8[0, 0])

    after = []
    for l in reversed(range(DEPTH)):
        dx, after = _layer_bwd(dx, saved[l], params[l], w, l, f"l{l}", dict(hooks, on_small=on_small), after)
    return dx


_ANY = pl.BlockSpec(memory_space=pl.ANY)
_MESH = pl.DeviceIdType.MESH


def _place():
    return lax.axis_index("x"), lax.axis_index("y"), lax.axis_index("c")


def _other_chips(x, y):
    return [(1 - x, y), (x, 1 - y), (1 - x, 1 - y)]


def _dma_sems(n):
    return pltpu.SemaphoreType.DMA((n,))


def _remote(src, dst, send_sems, recv_sems, k, to):
    return pltpu.make_async_remote_copy(src_ref=src, dst_ref=dst, send_sem=send_sems.at[k], recv_sem=recv_sems.at[k],
                                        device_id=to, device_id_type=_MESH)


_HBM = pl.BlockSpec(memory_space=pltpu.HBM)
_SEM = pl.BlockSpec(memory_space=pltpu.SEMAPHORE)
_EFFECT = pltpu.SideEffectType.DATAFLOW_SIDE_EFFECTING
N_REL = N_CHIPS - 1


def _gather_plan(x, y, c, srcs, lands):
    plan = []
    for l in lands:
        r = l.shape[0] // N_CHIPS
        rows = l.at[pl.ds((2 * x + y) * r, r)]
        plan += [(rows, rows, (cx, cy, c)) for cx, cy in _other_chips(x, y)]
    return plan


def _half_rows(land, chip, c):
    h = land.shape[0] // (2 * N_CHIPS)
    return land.at[pl.ds((2 * chip + c) * h, h)]


def _gather_half_plan(x, y, c, srcs, lands):
    return [(_half_rows(l, 2 * x + y, c), _half_rows(l, 2 * x + y, c), (cx, cy, c))
            for l in lands for cx, cy in _other_chips(x, y)]


def _forward_halves(lands, tag):
    nw = len(lands)

    def body(*refs):
        ins = refs[:nw]
        send_sems, recv_sems = refs[2 * nw:]
        x, y, c = _place()
        chips = [2 * cx + cy for cx, cy in _other_chips(x, y)]
        sends = [_remote(_half_rows(ins[i], k, c), _half_rows(ins[i], k, c), send_sems, recv_sems, N_REL * i + j,
                         (x, y, 1 - c)) for i in range(nw) for j, k in enumerate(chips)]
        for cp in sends:
            cp.start()
        for i in range(nw):
            for j, k in enumerate(chips):
                sends[N_REL * i + j].wait_send()
                _remote(_half_rows(ins[i], k, c), _half_rows(ins[i], k, 1 - c), send_sems, recv_sems, N_REL * i + j,
                        (x, y, 1 - c)).wait_recv()

    return pl.pallas_call(
        body, out_shape=[jax.ShapeDtypeStruct(a.shape, a.dtype) for a in lands], in_specs=[_ANY] * nw,
        out_specs=[_ANY] * nw, input_output_aliases={i: i for i in range(nw)},
        scratch_shapes=[_dma_sems(N_REL * nw), _dma_sems(N_REL * nw)], name=f"forward_halves_{tag}",
    )(*lands)


def _sibling_plan(x, y, c, srcs, lands):
    return [(s.at[:, 1 - c], l, (x, y, 1 - c)) for s, l in zip(srcs, lands)]


def _slab_plan(x, y, c, srcs, lands):
    return [(s.at[2 * cx + cy], l.at[j], (cx, cy, c))
            for s, l in zip(srcs, lands) for j, (cx, cy) in enumerate(_other_chips(x, y))]


def _plan_copies(plan, srcs, lands, send_sems, recv_sems):
    x, y, c = _place()
    return [_remote(s, d, send_sems, recv_sems, k, to) for k, (s, d, to) in enumerate(plan(x, y, c, srcs, lands))]


def _hbm(a):
    return pltpu.with_memory_space_constraint(a, pltpu.HBM)


def _everyone_plan(x, y, c, srcs, lands):
    me = 4 * x + 2 * y + c
    peers = [(x, y, 1 - c)] + [(cx, cy, cc) for cx, cy in _other_chips(x, y) for cc in (c, 1 - c)]
    return [(s, l.at[me], peer) for s, l in zip(srcs, lands) for peer in peers]


def _copies_start(name, plan, srcs, lands, ncopies):
    ns, n = len(srcs), len(srcs) + len(lands)

    def body(*refs):
        for cp in _plan_copies(plan, refs[:ns], refs[ns:n], refs[n], refs[n + 1]):
            cp.start()
        refs[-1][...] = jnp.zeros_like(refs[-1])

    ref_out = [pltpu.HBM(a.shape, a.dtype) for a in (*srcs, *lands)]
    out = pl.pallas_call(
        body, name=name, in_specs=[_HBM] * n,
        out_shape=(_dma_sems(ncopies), _dma_sems(ncopies), *ref_out, jax.ShapeDtypeStruct((SUBLANES, LANES), F32)),
        out_specs=(_SEM, _SEM, *[_HBM] * n, pl.BlockSpec(memory_space=pltpu.VMEM)),
        input_output_aliases={i: 2 + i for i in range(n)},
        compiler_params=pltpu.CompilerParams(has_side_effects=_EFFECT),
    )(*[_hbm(a) for a in (*srcs, *lands)])
    return dict(name=name, plan=plan, sems=out[:2], srcs=out[2:2 + ns], lands=out[2 + ns:2 + n], token=out[-1])


def _copies_wait(started, after):
    ns = len(started["srcs"])
    n = ns + len(started["lands"])
    plan = started["plan"]

    def body(*refs):
        for cp in _plan_copies(plan, refs[:ns], refs[ns:n], refs[n], refs[n + 1]):
            cp.wait_send()
            cp.wait_recv()

    args = (*started["srcs"], *started["lands"])
    out = pl.pallas_call(
        body, name=started["name"] + "_wait", out_shape=[pltpu.HBM(a.shape, a.dtype) for a in args],
        in_specs=[_HBM] * n + [_SEM, _SEM] + [_ANY] * len(after), out_specs=[_HBM] * n,
        input_output_aliases={i: i for i in range(n)},
        compiler_params=pltpu.CompilerParams(has_side_effects=_EFFECT),
    )(*args, *started["sems"], *after)
    return out[:ns], out[ns:]


def _place_shards(ws, layer, sel, after, tag):
    nw = len(ws)

    def body(sel_ref, *refs):
        for i in range(nw):
            refs[nw + len(after) + i][...] = refs[i][...].astype(BF16)

    return pl.pallas_call(
        body, grid_spec=pltpu.PrefetchScalarGridSpec(
            num_scalar_prefetch=1, grid=(1,),
            in_specs=[pl.BlockSpec((None,) + a.shape[1:], lambda i, s: (layer, 0, 0)) for a in ws] + [_ANY] * len(after),
            out_specs=[pl.BlockSpec(a.shape[1:], lambda i, s: (s[1], 0)) for a in ws]),
        out_shape=[jax.ShapeDtypeStruct((N_CHIPS * a.shape[1], a.shape[2]), BF16) for a in ws],
        name=f"place_shards_{tag}", compiler_params=_cp(dimension_semantics=("arbitrary",)),
    )(sel, *ws, *after)


def _share_halves(fs, layer, tag):
    nw = len(fs)

    def body(*refs):
        ins = refs[:nw]
        send_sems, recv_sems = refs[2 * nw:]
        x, y, c = _place()

        def half(i, who):
            h = ins[i].shape[1] // 2
            return ins[i].at[layer, pl.ds(who * h, h)]

        sends = [_remote(half(i, c), half(i, c), send_sems, recv_sems, i, (x, y, 1 - c)) for i in range(nw)]
        for cp in sends:
            cp.start()
        for i in range(nw):
            sends[i].wait_send()
            _remote(half(i, c), half(i, 1 - c), send_sems, recv_sems, i, (x, y, 1 - c)).wait_recv()

    return pl.pallas_call(
        body, out_shape=[jax.ShapeDtypeStruct(f.shape, f.dtype) for f in fs], in_specs=[_ANY] * nw,
        out_specs=[_ANY] * nw, input_output_aliases={i: i for i in range(nw)},
        scratch_shapes=[_dma_sems(nw), _dma_sems(nw)], name=f"share_halves_{tag}",
    )(*fs)


def _add_halves(g4s, recvs, sel, tag):
    nw = len(g4s)

    def body(sel_ref, *refs):
        for i in range(nw):
            refs[2 * nw + i][...] = (refs[i][...] + refs[nw + i][...]).astype(BF16)

    mine = [pl.BlockSpec((None, None) + g.shape[2:], lambda k, s: (k, s[0], 0, 0)) for g in g4s]
    slab = [pl.BlockSpec((None,) + g.shape[2:], lambda k, s: (k, 0, 0)) for g in g4s]
    return pl.pallas_call(
        body, grid_spec=pltpu.PrefetchScalarGridSpec(num_scalar_prefetch=1, grid=(N_CHIPS,), in_specs=mine + slab,
                                                     out_specs=slab),
        out_shape=[jax.ShapeDtypeStruct(r.shape, BF16) for r in recvs], name=f"add_halves_{tag}",
        compiler_params=_cp(dimension_semantics=("arbitrary",)),
    )(sel, *g4s, *recvs)


def _add_chips(ps, slabs, fs, layer, sel, tag):
    nw = len(ps)
    old = [f for f in fs if f is not None]

    def body(sel_ref, *refs):
        outs = refs[2 * nw + len(old):]
        for i in range(nw):
            acc = refs[i][...].astype(F32)
            for j in range(N_REL):
                acc = acc + refs[nw + i][j].astype(F32)
            outs[i][...] = acc

    shapes = [(DEPTH, 2 * p.shape[1], p.shape[2]) for p in ps]
    in_specs = [pl.BlockSpec((None,) + p.shape[1:], lambda i, s: (s[1], 0, 0)) for p in ps]
    in_specs += [pl.BlockSpec(sl.shape, lambda i, s: (0, 0, 0)) for sl in slabs]
    in_specs += [_ANY] * len(old)
    first_old = 1 + 2 * nw
    aliases, k = {}, 0
    for i, f in enumerate(fs):
        if f is not None:
            aliases[first_old + k] = i
            k += 1
    return pl.pallas_call(
        body, grid_spec=pltpu.PrefetchScalarGridSpec(
            num_scalar_prefetch=1, grid=(1,), in_specs=in_specs,
            out_specs=[pl.BlockSpec((None,) + p.shape[1:], lambda i, s: (layer, s[0], 0)) for p in ps]),
        out_shape=[jax.ShapeDtypeStruct(sh, F32) for sh in shapes], input_output_aliases=aliases,
        name=f"add_chips_{tag}", compiler_params=_cp(dimension_semantics=("arbitrary",)),
    )(sel, *ps, *slabs, *old)


def _adamw_math(w, g, m, v):
    m = ADAM_B1 * m + (1.0 - ADAM_B1) * g
    v = ADAM_B2 * v + (1.0 - ADAM_B2) * (g * g)
    m_hat = m / (1.0 - ADAM_B1 ** ADAM_STEP)
    v_hat = v / (1.0 - ADAM_B2 ** ADAM_STEP)
    delta = -ADAM_LR * (m_hat / (jnp.sqrt(v_hat) + ADAM_EPS) + ADAM_WD * w)
    return delta, m, v


ADAM_ROWS = 512


def _row_tile(rows, most):
    return max(t for t in range(SUBLANES, most + 1, SUBLANES) if rows % t == 0)


def _adamw(w, g, m, v, tag, after=()):
    depth, rows, cols = w.shape
    tr = _row_tile(rows, ADAM_ROWS)

    def body(w_ref, g_ref, m_ref, v_ref, *rest):
        d_ref, nm_ref, nv_ref = rest[len(after):]
        d, nm, nv = _adamw_math(w_ref[...], g_ref[...], m_ref[...], v_ref[...])
        d_ref[...] = d
        nm_ref[...] = nm
        nv_ref[...] = nv

    spec = pl.BlockSpec((None, tr, cols), lambda l, i: (l, i, 0))
    return pl.pallas_call(
        body, grid=(depth, rows // tr), in_specs=[spec] * 4 + [_ANY] * len(after), out_specs=[spec] * 3,
        out_shape=[jax.ShapeDtypeStruct(w.shape, F32)] * 3, name=f"adamw_{tag}",
        compiler_params=_cp(dimension_semantics=("arbitrary", "arbitrary")),
    )(w, g, m, v, *after)


SMALL_TILE = 512
PRECISE = ("g_final",)
COARSE = [n for n in SMALL if n not in PRECISE]


def _small_reduce(gathered):
    n = len(gathered)

    def body(*refs):
        for ga_ref, g_ref in zip(refs[:n], refs[n:]):
            g = ga_ref[0].astype(F32)
            for k in range(1, N_DEV):
                g = g + ga_ref[k].astype(F32)
            g_ref[...] = g

    return pl.pallas_call(
        body, out_shape=[jax.ShapeDtypeStruct(b.shape[1:], F32) for b in gathered], name="small_reduce",
        compiler_params=_cp(),
    )(*gathered)


def _small_adamw(g, w, m, v):
    names = list(g)
    n = len(names)
    shapes = {k: g[k].shape if g[k].ndim > 1 else (1,) + g[k].shape for k in names}

    def body(*refs):
        ins, outs = refs[:4 * n], refs[4 * n:]
        for i in range(n):
            d, nm, nv = _adamw_math(ins[n + i][...], ins[i][...], ins[2 * n + i][...], ins[3 * n + i][...])
            outs[i][...] = d
            outs[n + i][...] = nm
            outs[2 * n + i][...] = nv

    out = pl.pallas_call(
        body, out_shape=[jax.ShapeDtypeStruct(shapes[k], F32) for _ in range(3) for k in names], name="adamw_small",
        compiler_params=_cp(),
    )(*[src[k].reshape(shapes[k]) for src in (g, w, m, v) for k in names])
    return [{k: out[j * n + i].reshape(g[k].shape) for i, k in enumerate(names)} for j in range(3)]


def _exchange_form(n, a):
    return jnp.swapaxes(a, 1, 2) if n in TRANSPOSED else a


PACK_ROWS = 16


def _rows_of(size):
    return -(-size // (LANES * PACK_ROWS)) * PACK_ROWS


SMALL_VIEW = {"B_re": (0, 1, 3, 2), "B_im": (0, 1, 3, 2), "b_spatial": (1, 0, 2)}
assert all(tuple(order[i] for i in order) == tuple(range(len(order))) for order in SMALL_VIEW.values())


def _view(n, a):
    return jnp.transpose(a, SMALL_VIEW[n]) if n in SMALL_VIEW else a


def _pack(vals, names, extra=None):
    parts = [_view(n, vals[n]).reshape(-1) for n in names] + ([] if extra is None else [extra.reshape(1)])
    tiles = [jnp.pad(a, (0, _rows_of(a.size) * LANES - a.size)).reshape(-1, LANES) for a in parts]
    rows = sum(t.shape[0] for t in tiles)
    if rows > SMALL_TILE:
        tiles.append(jnp.zeros((-rows % SMALL_TILE, LANES), tiles[0].dtype))
    return jnp.concatenate(tiles, axis=0)


def _unpack(buf, like, names):
    out, row = {}, 0
    for n in names:
        rows = _rows_of(like[n].size)
        shape = tuple(like[n].shape[i] for i in SMALL_VIEW.get(n, range(like[n].ndim)))
        out[n] = buf[row:row + rows].reshape(-1)[:like[n].size].reshape(shape)
        row += rows
    return out, buf[row:]


def kernel(x, g_mix, w_in, A_re, A_im, log_dt, B_re, B_im, C_re, C_im, D_skip, w_glu, b_glu, w_pool, pool_scale, sgu_ln_g, sgu_ln_b, w_spatial, b_spatial, w_out, g_ffn, w_gate, w_up, w_down, g_final, loss_target, m_g_mix, m_w_in, m_A_re, m_A_im, m_log_dt, m_B_re, m_B_im, m_C_re, m_C_im, m_D_skip, m_w_glu, m_b_glu, m_w_pool, m_pool_scale, m_sgu_ln_g, m_sgu_ln_b, m_w_spatial, m_b_spatial, m_w_out, m_g_ffn, m_w_gate, m_w_up, m_w_down, m_g_final, v_g_mix, v_w_in, v_A_re, v_A_im, v_log_dt, v_B_re, v_B_im, v_C_re, v_C_im, v_D_skip, v_w_glu, v_b_glu, v_w_pool, v_pool_scale, v_sgu_ln_g, v_sgu_ln_b, v_w_spatial, v_b_spatial, v_w_out, v_g_ffn, v_w_gate, v_w_up, v_w_down, v_g_final):
    loc = locals()
    w = {n: loc[n] for n in WEIGHTS}
    m = {n: loc["m_" + n] for n in WEIGHTS}
    v = {n: loc["v_" + n] for n in WEIGHTS}
    sel = jnp.stack([lax.axis_index("c"), 2 * lax.axis_index("x") + lax.axis_index("y")]).astype(jnp.int32)

    chip = sel[1]

    halves = [(l, half) for l in range(DEPTH) for half in ("mix", "ffn")]
    two_level = {(0, "ffn")}
    names = {"mix": MIX_WEIGHTS, "ffn": FFN_WEIGHTS}
    started = {}
    wx = {n: _exchange_form(n, w[n]) for n in BIG}
    chain = []
    for l, half in halves:
        lands = _place_shards([wx[n] for n in names[half]], l, sel, chain, f"l{l}_{half}")
        plan = _gather_half_plan if (l, half) in two_level else _gather_plan
        started[l, half] = _copies_start(f"weights_l{l}_{half}", plan, [], lands, N_REL * len(lands))
        chain = [started[l, half]["token"]]
    w = dict(w, g_mix=_after(w["g_mix"], started[halves[-1]]["token"]))

    def get_big(l, half, after):
        lands = _copies_wait(started[l, half], after)[1]
        if (l, half) in two_level:
            lands = _forward_halves(lands, f"l{l}_{half}")
        return dict(zip(names[half], lands))

    result = {n: None for n in BIG}
    stage = {"swap": None, "slabs": None}

    def advance(after):
        if stage["slabs"] is not None:
            ex, ns, l, tag = stage["slabs"]
            part, slabs = _copies_wait(ex, after)
            bufs = _add_chips(part, slabs, [result[n] for n in ns], l, sel, tag)
            for n, f in zip(ns, _share_halves(bufs, l, tag)):
                result[n] = f
            stage["slabs"] = None
        if stage["swap"] is None:
            return []
        sw, ns, l, tag = stage["swap"]
        part = _add_halves(*_copies_wait(sw, after), sel, tag)
        slabs = [lax.empty((N_REL,) + p.shape[1:], BF16) for p in part]
        ex = _copies_start(f"grads_{tag}", _slab_plan, part, slabs, N_REL * len(part))
        stage["slabs"], stage["swap"] = (ex, ns, l, tag), None
        return [ex["token"]]

    def on_grads(l, half, grads):
        ns = list(grads)
        tag = f"l{l}_{half}"
        tokens = advance([grads[ns[0]]])
        g4s = [grads[n].reshape(N_CHIPS, 2, grads[n].shape[0] // (2 * N_CHIPS), grads[n].shape[1]) for n in ns]
        recvs = [lax.empty((N_CHIPS,) + g4.shape[2:], F32) for g4 in g4s]
        sw = _copies_start(f"swap_{tag}", _sibling_plan, g4s, recvs, len(g4s))
        stage["swap"] = (sw, ns, l, tag)
        return tokens + [sw["token"]]

    small = {}

    def on_small(g, loss_local):
        me = 2 * chip + sel[0]
        blocks = [_pack(g, COARSE).astype(BF16), _pack(g, PRECISE, loss_local)]
        lands = [lax.dynamic_update_slice(lax.empty((N_DEV,) + b.shape, b.dtype), b[None], (me, 0, 0)) for b in blocks]
        small.update(_copies_start("small_grads", _everyone_plan, blocks, lands, (N_DEV - 1) * len(blocks)))
        return [small["token"]]

    dx = _local_step(x[0], loss_target[0], w, dict(get_big=get_big, on_grads=on_grads, tick=advance, on_small=on_small))
    grads, deltas, new_m, new_v = {}, {}, {}, {}

    def update_big(ns, after):
        for n in ns:
            outs = _adamw(wx[n], result[n], _exchange_form(n, m[n]), _exchange_form(n, v[n]), n, after)
            grads[n], deltas[n], new_m[n], new_v[n] = [_exchange_form(n, a) for a in (result[n], *outs)]
            after = [outs[-1]]
        return after

    last_swap = stage["swap"][0]["token"]
    last_sent = advance(update_big(FFN_WEIGHTS[:1], [last_swap]))
    advance(update_big(FFN_WEIGHTS[1:], last_sent))
    _, gathered = _copies_wait(small, update_big(MIX_WEIGHTS, []))
    coarse, precise = _small_reduce(gathered)
    small_g, _ = _unpack(coarse, w, COARSE)
    precise_g, rest = _unpack(precise, w, PRECISE)
    small_g.update(precise_g)
    loss = rest[0, 0]
    views = [{n: _view(n, src[n]) for n in small_g} for src in (w, m, v)]
    for store, vals in zip((grads, deltas, new_m, new_v), (small_g, *_small_adamw(small_g, *views))):
        store.update({n: _view(n, a) for n, a in vals.items()})
    return (loss, dx[None], *[grads[n] for n in WEIGHTS], *[deltas[n] for n in WEIGHTS],
            *[new_m[n] for n in WEIGHTS], *[new_v[n] for n in WEIGHTS])
```

```python
import math

import jax
import jax.numpy as jnp
from jax import lax
from jax.experimental import pallas as pl
from jax.experimental.pallas import tpu as pltpu

F32 = jnp.float32
BF16 = jnp.bfloat16

D_MODEL = 1024
DEPTH = 2
D_SSM = 384
SSM_GROUP = 16
N_GROUPS = 24
SSM_STATE = 64
N_STATE = N_GROUPS * SSM_STATE
POOL_WINDOWS = (2, 4, 8, 16)
POOL_GROUP = 64
D_POOL = 256
MAX_WINDOW = 16
SGU_HEADS = 6
SGU_HEAD_DIM = 64
D_SGU = 384
CHUNK = 128
D_IN = D_SSM + D_POOL + 2 * D_SGU
D_FF = 2816
EPS = 1e-6

ADAM_LR = 0.001
ADAM_B1 = 0.9
ADAM_B2 = 0.999
ADAM_EPS = 1e-08
ADAM_WD = 0.01
ADAM_STEP = 10

LANES = 128
SUBLANES = 8
VMEM_LIMIT = 56 * 1024 * 1024

TS = 512
TS_FFN = 256

WEIGHTS = ['g_mix', 'w_in', 'A_re', 'A_im', 'log_dt', 'B_re', 'B_im', 'C_re', 'C_im', 'D_skip', 'w_glu', 'b_glu',
           'w_pool', 'pool_scale', 'sgu_ln_g', 'sgu_ln_b', 'w_spatial', 'b_spatial', 'w_out', 'g_ffn', 'w_gate',
           'w_up', 'w_down', 'g_final']
BIG = ['w_in', 'w_glu', 'w_out', 'w_gate', 'w_up', 'w_down']
SMALL = [n for n in WEIGHTS if n not in BIG]
TRANSPOSED = ("w_in", "w_gate", "w_up")
N_CHIPS = 4
N_DEV = 8


def _cp(**kw):
    return pltpu.CompilerParams(vmem_limit_bytes=VMEM_LIMIT, **kw)


def _row(ts, n):
    return pl.BlockSpec((ts, n), lambda i: (i, 0))


def _const(shape):
    nd = len(shape)
    return pl.BlockSpec(shape, lambda i: (0,) * nd, pipeline_mode=pl.Buffered(1))


def _acc(shape):
    nd = len(shape)
    return pl.BlockSpec(shape, lambda i: (0,) * nd)


def _following(body, after):
    k = len(after)
    return lambda *refs: body(*refs[k:])


def _anywhere(after):
    return [pl.BlockSpec(memory_space=pl.ANY)] * len(after)


def _dot(a, b):
    return jnp.dot(a, b, preferred_element_type=F32)


def _dot_tn(a, b):
    return lax.dot_general(a, b, (((0,), (0,)), ((), ())), preferred_element_type=F32)


def _dot_nt(a, b):
    return lax.dot_general(a, b, (((1,), (1,)), ((), ())), preferred_element_type=F32)


_G0 = math.sqrt(2.0 / math.pi)
_G1 = 0.044715


def _gelu(x):
    return 0.5 * x * (1.0 + jnp.tanh(_G0 * (x + _G1 * x * x * x)))


def _gelu_and_grad(x):
    t = jnp.tanh(_G0 * (x + _G1 * x * x * x))
    half = 0.5 * (1.0 + t)
    return x * half, half + 0.5 * x * (1.0 - t * t) * (_G0 * (1.0 + 3.0 * _G1 * x * x))


def _sigmoid(x):
    return 1.0 / (1.0 + jnp.exp(-x))


def _rms(x):
    r = lax.rsqrt(jnp.mean(x * x, axis=-1, keepdims=True) + EPS)
    return x * r, r


def _rms_bwd(dh, n, r, g):
    dn = dh * g
    return r * (dn - n * jnp.mean(dn * n, axis=-1, keepdims=True)), dh * n


def _colsum8(v):
    rows, n = v.shape
    return jnp.sum(v.reshape(rows // SUBLANES, SUBLANES, n), axis=0)


def _cmul(ar, ai, br, bi):
    return ar * br - ai * bi, ar * bi + ai * br


def _cpow(ar, ai, n):
    assert n & (n - 1) == 0
    while n > 1:
        ar, ai = _cmul(ar, ai, ar, ai)
        n //= 2
    return ar, ai


N_USLAB = D_SSM // LANES
SEG = TS // SUBLANES
SLAB_STATES = N_STATE // N_USLAB
S5_IN = (N_USLAB, LANES, 2 * SLAB_STATES)
S5_OUT = (N_USLAB, 2 * SLAB_STATES, LANES)
STATE_TILE = (SUBLANES, N_STATE)


def _scan_order():
    p = jnp.arange(TS)
    src = (p % SUBLANES) * SEG + p // SUBLANES
    return (src[:, None] == jnp.arange(TS)[None, :]).astype(BF16)


def _to_scan_order(perm, v):
    hi = v.astype(BF16)
    lo = (v - hi.astype(F32)).astype(BF16)
    return _dot(perm, hi) + _dot(perm, lo)


def _scan_rows(k):
    return pl.ds(pl.multiple_of(k * SUBLANES, SUBLANES), SUBLANES)


def _lanes(v, j):
    return v[:, LANES * j:LANES * (j + 1)]


def _states(j):
    return pl.ds(SLAB_STATES * j, SLAB_STATES)


def _state_split(re_ref, im_ref, j, v):
    re_ref[:, _states(j)] = v[:, :SLAB_STATES]
    im_ref[:, _states(j)] = v[:, SLAB_STATES:]


def _state_cat(re_ref, im_ref, j):
    return jnp.concatenate([re_ref[:, _states(j)], im_ref[:, _states(j)]], axis=1).astype(BF16)


def _s5_fwd(u, p, tag):
    s = u.shape[0]
    seg = SEG

    def body(u_ref, perm_ref, bbt_ref, ar_ref, ai_ref, ct_ref, dsk_ref, wglu_ref, bglu_ref,
             oa_ref, y_ref, hr_ref, hi_ref, sr, si, er, ei, ir, ii, cr, ci):
        @pl.when(pl.program_id(0) == 0)
        def _():
            cr[...] = jnp.zeros_like(cr)
            ci[...] = jnp.zeros_like(ci)

        perm = perm_ref[...]
        uv = _to_scan_order(perm, u_ref[...])
        ub = uv.astype(BF16)
        for j in range(N_USLAB):
            _state_split(sr, si, j, _dot(_lanes(ub, j), bbt_ref[j]))
        for j in range(N_USLAB):
            cols = _states(j)
            ar = ar_ref[:, cols]
            ai = ai_ref[:, cols]
            h_r = h_i = jnp.zeros((SUBLANES, SLAB_STATES), F32)
            for k in range(seg):
                rows = pl.ds(SUBLANES * k, SUBLANES)
                n_r, n_i = _cmul(ar, ai, h_r, h_i)
                h_r = n_r + sr[rows, cols]
                h_i = n_i + si[rows, cols]
            er[:, cols] = h_r
            ei[:, cols] = h_i
            pr, pi = _cpow(ar[0:1, :], ai[0:1, :], seg)
            c_r = cr[:, cols]
            c_i = ci[:, cols]
            for q in range(SUBLANES):
                ir[q:q + 1, cols] = c_r
                ii[q:q + 1, cols] = c_i
                n_r, n_i = _cmul(pr, pi, c_r, c_i)
                c_r = n_r + er[q:q + 1, cols]
                c_i = n_i + ei[q:q + 1, cols]
            cr[:, cols] = c_r
            ci[:, cols] = c_i
            h_r = ir[:, cols]
            h_i = ii[:, cols]
            for k in range(seg):
                rows = pl.ds(SUBLANES * k, SUBLANES)
                n_r, n_i = _cmul(ar, ai, h_r, h_i)
                h_r = n_r + sr[rows, cols]
                h_i = n_i + si[rows, cols]
                sr[rows, cols] = h_r
                si[rows, cols] = h_i
        hr_ref[...] = sr[...].astype(BF16)
        hi_ref[...] = si[...].astype(BF16)
        y = jnp.concatenate([_dot(_state_cat(hr_ref, hi_ref, j), ct_ref[j]) for j in range(N_USLAB)], axis=1)
        y = y + dsk_ref[...] * uv
        y_ref[...] = y
        g = _gelu(y)
        pre = _dot(g.astype(BF16), wglu_ref[...]) + bglu_ref[...]
        oa_ref[...] = _dot_tn(perm, (g * _sigmoid(pre)).astype(BF16)).astype(BF16)

    return pl.pallas_call(
        body, grid=(s // TS,),
        in_specs=[_row(TS, D_SSM), _const((TS, TS)), _const(S5_IN), _const(STATE_TILE), _const(STATE_TILE),
                  _const(S5_OUT), _const((1, D_SSM)), _const((D_SSM, D_SSM)), _const((1, D_SSM))],
        out_specs=[_row(TS, D_SSM), _row(TS, D_SSM), _row(TS, N_STATE), _row(TS, N_STATE)],
        out_shape=[jax.ShapeDtypeStruct((s, D_SSM), BF16), jax.ShapeDtypeStruct((s, D_SSM), F32),
                   jax.ShapeDtypeStruct((s, N_STATE), BF16), jax.ShapeDtypeStruct((s, N_STATE), BF16)],
        scratch_shapes=[pltpu.VMEM((TS, N_STATE), F32), pltpu.VMEM((TS, N_STATE), F32),
                        pltpu.VMEM(STATE_TILE, F32), pltpu.VMEM(STATE_TILE, F32), pltpu.VMEM(STATE_TILE, F32),
                        pltpu.VMEM(STATE_TILE, F32), pltpu.VMEM((1, N_STATE), F32), pltpu.VMEM((1, N_STATE), F32)],
        name=f"s5_fwd_{tag}", compiler_params=_cp(dimension_semantics=("arbitrary",)),
    )(u, _scan_order(), p["bbt3"], p["a_re8"], p["a_im8"], p["ct3"], p["d_skip"], p["w_glu"], p["b_glu"])


def _pool_consts():
    w = jnp.repeat(jnp.asarray(POOL_WINDOWS, F32), POOL_GROUP)[None, :]
    return w


POOL_PAD = SUBLANES
POOL_ROWS = TS + MAX_WINDOW + POOL_PAD


def _window_sum(buf, tmp, first, wl, step):
    assert POOL_WINDOWS == (2, 4, 8, 16)
    n = TS + MAX_WINDOW
    lo = first - MAX_WINDOW if step < 0 else first
    src = buf
    for k, dst in zip((1, 2, 4), tmp):
        dst[pl.ds(lo, n), :] = src[pl.ds(lo, n), :] + src[pl.ds(lo + step * k, n), :]
        src = dst
    s2, s4, s8 = (t[pl.ds(first, TS), :] for t in tmp)
    s16 = s8 + tmp[2][pl.ds(first + step * 8, TS), :]
    return jnp.where(wl == 2, s2, jnp.where(wl == 4, s4, jnp.where(wl == 8, s8, s16)))


def _pool_count(i, rows, wl):
    t = (i * TS + 1).astype(F32) + lax.broadcasted_iota(jnp.int32, (rows, 1), 0).astype(F32)
    return jnp.minimum(t, wl)


def _sgu_mix(vl, wpair_ref, lo, hi):
    rows = vl.shape[0]
    chunks = []
    for c in range(rows // CHUNK):
        vc = vl[CHUNK * c:CHUNK * (c + 1), :]
        parts = []
        for q in range(SGU_HEADS // 2):
            vq = vc[:, LANES * q:LANES * (q + 1)]
            rhs = jnp.concatenate([vq * lo, vq * hi], axis=0).astype(BF16)
            parts.append(_dot(wpair_ref[q], rhs))
        chunks.append(jnp.concatenate(parts, axis=1))
    return jnp.concatenate(chunks, axis=0)


def _sgu_front(zuv, lng, lnb, grads=False):
    gelu = _gelu_and_grad if grads else lambda z: (_gelu(z), None)
    u, du = gelu(zuv[:, :D_SGU])
    v, dv = gelu(zuv[:, D_SGU:])
    mu = jnp.mean(v, axis=-1, keepdims=True)
    vc = v - mu
    rs = lax.rsqrt(jnp.mean(vc * vc, axis=-1, keepdims=True) + EPS)
    vn = vc * rs
    return u, vn, rs, vn * lng + lnb, du, dv


def _half_masks():
    lane = lax.broadcasted_iota(jnp.int32, (1, LANES), 1)
    lo = (lane < SGU_HEAD_DIM).astype(F32)
    return lo, 1.0 - lo


def _mix_fwd(x, p, tag):
    s = x.shape[0]

    def body(x_ref, g_ref, w_ref, wl_ref, wp_ref, sc_ref, lng_ref, lnb_ref, wsp_ref, bias_ref,
             za_ref, zuv_ref, h_ref, ob_ref, pooled_ref, oc_ref, buf, *tmp):
        i = pl.program_id(0)
        tile0 = POOL_PAD + MAX_WINDOW

        @pl.when(i == 0)
        def _():
            for ref in (buf, *tmp):
                ref[pl.ds(0, tile0), :] = jnp.zeros((tile0, D_POOL), F32)

        n, _ = _rms(x_ref[...])
        h = (n * g_ref[...]).astype(BF16)
        h_ref[...] = h
        z = _dot_nt(h, w_ref[...])
        za_ref[...] = z[:, :D_SSM]
        zb = z[:, D_SSM:D_SSM + D_POOL]
        zuv = z[:, D_SSM + D_POOL:]
        zuv_ref[...] = zuv
        buf[pl.ds(tile0, TS), :] = zb
        wl = wl_ref[...]
        pooled = (_window_sum(buf, tmp, tile0, wl, -1) / _pool_count(i, TS, wl) - zb).astype(BF16)
        buf[pl.ds(POOL_PAD, MAX_WINDOW), :] = zb[TS - MAX_WINDOW:, :]
        pooled_ref[...] = pooled
        ob_ref[...] = (_dot(pooled, wp_ref[...]) * sc_ref[...]).astype(BF16)
        lo, hi = _half_masks()
        u, _, _, vl, _, _ = _sgu_front(zuv, lng_ref[...], lnb_ref[...])
        mixed = _sgu_mix(vl, wsp_ref, lo, hi) + jnp.tile(bias_ref[...], (TS // CHUNK, 1))
        oc_ref[...] = (u * mixed).astype(BF16)

    return pl.pallas_call(
        body, grid=(s // TS,),
        in_specs=[_row(TS, D_MODEL), _const((1, D_MODEL)), _const((D_IN, D_MODEL)), _const((1, D_POOL)),
                  _const((D_POOL, D_POOL)), _const((1, D_POOL)), _const((1, D_SGU)), _const((1, D_SGU)),
                  _const((SGU_HEADS // 2, CHUNK, 2 * CHUNK)), _const((CHUNK, D_SGU))],
        out_specs=[_row(TS, D_SSM), _row(TS, 2 * D_SGU), _row(TS, D_MODEL), _row(TS, D_POOL), _row(TS, D_POOL),
                   _row(TS, D_SGU)],
        out_shape=[jax.ShapeDtypeStruct((s, D_SSM), F32), jax.ShapeDtypeStruct((s, 2 * D_SGU), F32),
                   jax.ShapeDtypeStruct((s, D_MODEL), BF16), jax.ShapeDtypeStruct((s, D_POOL), BF16),
                   jax.ShapeDtypeStruct((s, D_POOL), BF16), jax.ShapeDtypeStruct((s, D_SGU), BF16)],
        scratch_shapes=[pltpu.VMEM((POOL_ROWS, D_POOL), F32)] * 4,
        name=f"mix_fwd_{tag}", compiler_params=_cp(dimension_semantics=("arbitrary",)),
    )(x, p["g_mix"], p["w_in"], _pool_consts(), p["w_pool_bd"], p["pool_scale"], p["sgu_ln_g"], p["sgu_ln_b"],
      p["ws_pair"], p["bias_sp"])


def _blk_fwd(x0, oa, ob, oc, p, tag, head=None):
    s = x0.shape[0]
    ts = TS_FFN
    n_head = 0 if head is None else len(head)

    def body(x0_ref, oa_ref, ob_ref, oc_ref, wo_ref, g_ref, wg_ref, wu_ref, wd_ref, *refs):
        x1_ref, x2_ref, h2_ref, gt_ref, up_ref, ycat_ref = refs[n_head:n_head + 6]
        ycat = jnp.concatenate([oa_ref[...], ob_ref[...], oc_ref[...]], axis=1)
        ycat_ref[...] = ycat
        x1 = x0_ref[...] + _dot(ycat, wo_ref[...])
        x1_ref[...] = x1
        n, _ = _rms(x1)
        h2 = (n * g_ref[...]).astype(BF16)
        h2_ref[...] = h2
        gt = _dot_nt(h2, wg_ref[...])
        up = _dot_nt(h2, wu_ref[...])
        gt_ref[...] = gt.astype(BF16)
        up_ref[...] = up.astype(BF16)
        act = (gt * _sigmoid(gt) * up).astype(BF16)
        x2 = x1 + _dot(act, wd_ref[...])
        if head is None:
            x2_ref[...] = x2
            return
        t_ref, gf_ref = refs[:n_head]
        loss_ref, dgf_ref = refs[n_head + 6:]

        @pl.when(pl.program_id(0) == 0)
        def _():
            loss_ref[...] = jnp.zeros_like(loss_ref)
            dgf_ref[...] = jnp.zeros_like(dgf_ref)

        gf = gf_ref[...]
        nf, rf = _rms(x2)
        diff = nf * gf - t_ref[...]
        loss_ref[...] += jnp.sum(diff * diff) * (0.5 / D_MODEL)
        dxn, dgp = _rms_bwd(diff * (1.0 / D_MODEL), nf, rf, gf)
        dgf_ref[...] += _colsum8(dgp)
        x2_ref[...] = dxn

    in_specs = [_row(ts, D_MODEL), _row(ts, D_SSM), _row(ts, D_POOL), _row(ts, D_SGU), _const((D_MODEL, D_MODEL)),
                _const((1, D_MODEL)), _const((D_FF, D_MODEL)), _const((D_FF, D_MODEL)), _const((D_FF, D_MODEL))]
    out_specs = [_row(ts, D_MODEL), _row(ts, D_MODEL), _row(ts, D_MODEL), _row(ts, D_FF), _row(ts, D_FF),
                 _row(ts, D_MODEL)]
    out_shape = [jax.ShapeDtypeStruct((s, D_MODEL), F32), jax.ShapeDtypeStruct((s, D_MODEL), F32),
                 jax.ShapeDtypeStruct((s, D_MODEL), BF16), jax.ShapeDtypeStruct((s, D_FF), BF16),
                 jax.ShapeDtypeStruct((s, D_FF), BF16), jax.ShapeDtypeStruct((s, D_MODEL), BF16)]
    args = (x0, oa, ob, oc, p["w_out"], p["g_ffn"], p["w_gate"], p["w_up"], p["w_down"])
    if head is not None:
        in_specs += [_row(ts, D_MODEL), _const((1, D_MODEL))]
        out_specs += [_acc((SUBLANES, LANES)), _acc((SUBLANES, D_MODEL))]
        out_shape += [jax.ShapeDtypeStruct((SUBLANES, LANES), F32), jax.ShapeDtypeStruct((SUBLANES, D_MODEL), F32)]
        args += tuple(head)
    return pl.pallas_call(
        body, grid=(s // ts,), in_specs=in_specs, out_specs=out_specs, out_shape=out_shape,
        name=f"blk_fwd_{tag}", compiler_params=_cp(dimension_semantics=("arbitrary",)),
    )(*args)


def _blk_bwd(dx2, x1, gt, up, p, tag, after=()):
    s = dx2.shape[0]
    ts = TS_FFN

    def body(dx2_ref, x1_ref, gt_ref, up_ref, wd_ref, wgt_ref, wut_ref, wo_ref, g_ref,
             dx1_ref, da_ref, db_ref, dc_ref, dgt_ref, dup_ref, act_ref, dg_ref):
        @pl.when(pl.program_id(0) == 0)
        def _():
            dg_ref[...] = jnp.zeros_like(dg_ref)

        dx2v = dx2_ref[...]
        dact = _dot_nt(dx2v.astype(BF16), wd_ref[...])
        gf = gt_ref[...].astype(F32)
        uf = up_ref[...].astype(F32)
        sg = _sigmoid(gf)
        sl = gf * sg
        act_ref[...] = (sl * uf).astype(BF16)
        dgt = (dact * uf * (sg * (1.0 + gf * (1.0 - sg)))).astype(BF16)
        dup = (dact * sl).astype(BF16)
        dgt_ref[...] = dgt
        dup_ref[...] = dup
        dh2 = _dot(dgt, wgt_ref[...]) + _dot(dup, wut_ref[...])
        n, r = _rms(x1_ref[...])
        dxn, dgp = _rms_bwd(dh2, n, r, g_ref[...])
        dg_ref[...] += _colsum8(dgp)
        dx1 = dx2v + dxn
        dx1_ref[...] = dx1
        dy = _dot_nt(dx1.astype(BF16), wo_ref[...])
        da_ref[...] = dy[:, :D_SSM]
        db_ref[...] = dy[:, D_SSM:D_SSM + D_POOL]
        dc_ref[...] = dy[:, D_SSM + D_POOL:]

    return pl.pallas_call(
        _following(body, after), grid=(s // ts,),
        in_specs=_anywhere(after) + [_row(ts, D_MODEL), _row(ts, D_MODEL), _row(ts, D_FF), _row(ts, D_FF),
                  _const((D_FF, D_MODEL)), _const((D_FF, D_MODEL)), _const((D_FF, D_MODEL)),
                  _const((D_MODEL, D_MODEL)), _const((1, D_MODEL))],
        out_specs=[_row(ts, D_MODEL), _row(ts, D_SSM), _row(ts, D_POOL), _row(ts, D_SGU), _row(ts, D_FF),
                   _row(ts, D_FF), _row(ts, D_FF), _acc((SUBLANES, D_MODEL))],
        out_shape=[jax.ShapeDtypeStruct((s, D_MODEL), F32), jax.ShapeDtypeStruct((s, D_SSM), F32),
                   jax.ShapeDtypeStruct((s, D_POOL), F32), jax.ShapeDtypeStruct((s, D_SGU), F32),
                   jax.ShapeDtypeStruct((s, D_FF), BF16), jax.ShapeDtypeStruct((s, D_FF), BF16),
                   jax.ShapeDtypeStruct((s, D_FF), BF16), jax.ShapeDtypeStruct((SUBLANES, D_MODEL), F32)],
        name=f"blk_bwd_{tag}", compiler_params=_cp(dimension_semantics=("arbitrary",)),
    )(*after, dx2, x1, gt, up, p["w_down"], p["w_gate"], p["w_up"], p["w_out"], p["g_ffn"])


S5_OWN = (N_USLAB, LANES, LANES)
assert 2 * SSM_STATE == LANES


def _own_blocks(acc_ref, j):
    pairs = SLAB_STATES // LANES
    group = lax.broadcasted_iota(jnp.int32, (LANES, LANES), 0) // (LANES // (2 * pairs))
    lane = lax.broadcasted_iota(jnp.int32, (LANES, LANES), 1)
    parts = []
    for half in range(2):
        own = jnp.zeros((LANES, LANES), F32)
        for q in range(pairs):
            own = jnp.where(group // 2 == q, acc_ref[j, :, pl.ds(half * SLAB_STATES + q * LANES, LANES)], own)
        parts.append(jnp.where(group % 2 == 0, own, pltpu.roll(own, SSM_STATE, 1)))
    return jnp.where(lane < SSM_STATE, parts[0], pltpu.roll(parts[1], SSM_STATE, 1))


def _own_blocks_split(d):
    d = d.reshape(N_GROUPS, D_SSM // N_GROUPS, 2, SSM_STATE)
    return d[:, :, 0], d[:, :, 1]


def _s5_bwd(dout, u, y, h_re, h_im, p, tag, after=()):
    s = u.shape[0]
    nt = s // TS
    seg = SEG

    def rev(n):
        return pl.BlockSpec((TS, n), lambda i: (nt - 1 - i, 0))

    def body(do_ref, u_ref, y_ref, hr_ref, hi_ref, perm_ref, ar_ref, ai_ref, cb_ref, bb_ref, dsk_ref,
             wglu_ref, bglu_ref,
             du_ref, dct_own_ref, dbb_own_ref, dar_ref, dai_ref, dd_ref, dwglu_ref, dbglu_ref,
             gr, gi, hsr, hsi, er, ei, jr, ji, cr, ci, dct_ref, dbb_ref):
        @pl.when(pl.program_id(0) == 0)
        def _():
            for ref in (cr, ci, dct_ref, dbb_ref, dar_ref, dai_ref, dd_ref, dwglu_ref, dbglu_ref):
                ref[...] = jnp.zeros_like(ref)

        perm = perm_ref[...]
        uv = _to_scan_order(perm, u_ref[...])
        yv = y_ref[...]
        dov = _to_scan_order(perm, do_ref[...])
        g, gelu_dy = _gelu_and_grad(yv)
        gb = g.astype(BF16)
        sg = _sigmoid(_dot(gb, wglu_ref[...]) + bglu_ref[...])
        dpre = dov * g * sg * (1.0 - sg)
        dpb = dpre.astype(BF16)
        dwglu_ref[...] += _dot_tn(gb, dpb)
        dbglu_ref[...] += _colsum8(dpre)
        dy = (dov * sg + _dot_nt(dpb, wglu_ref[...])) * gelu_dy
        dd_ref[...] += _colsum8(dy * uv)
        dyb = dy.astype(BF16)
        hsr[...] = hr_ref[...].astype(F32)
        hsi[...] = hi_ref[...].astype(F32)
        ub = uv.astype(BF16)
        dus = []

        def state_cotangents(j):
            dct_ref[j] += _dot_tn(_lanes(dyb, j), _state_cat(hr_ref, hi_ref, j))
            _state_split(gr, gi, j, _dot(_lanes(dyb, j), cb_ref[j]))

        def input_cotangents(j):
            gb_j = _state_cat(gr, gi, j)
            dbb_ref[j] += _dot_tn(_lanes(ub, j), gb_j)
            dus.append(_dot(gb_j, bb_ref[j]))

        def scan(j):
            cols = _states(j)
            ar = ar_ref[:, cols]
            ai = -ai_ref[:, cols]
            g_r = g_i = jnp.zeros((SUBLANES, SLAB_STATES), F32)
            for k in range(seg - 1, -1, -1):
                rows = pl.ds(SUBLANES * k, SUBLANES)
                n_r, n_i = _cmul(ar, ai, g_r, g_i)
                g_r = n_r + gr[rows, cols]
                g_i = n_i + gi[rows, cols]
            er[:, cols] = g_r
            ei[:, cols] = g_i
            pr, pi = _cpow(ar[0:1, :], ai[0:1, :], seg)
            c_r = cr[:, cols]
            c_i = ci[:, cols]
            for q in range(SUBLANES - 1, -1, -1):
                jr[q:q + 1, cols] = c_r
                ji[q:q + 1, cols] = c_i
                n_r, n_i = _cmul(pr, pi, c_r, c_i)
                c_r = n_r + er[q:q + 1, cols]
                c_i = n_i + ei[q:q + 1, cols]
            cr[:, cols] = c_r
            ci[:, cols] = c_i
            g_r = jr[:, cols]
            g_i = ji[:, cols]
            a_r = a_i = jnp.zeros((SUBLANES, SLAB_STATES), F32)
            for k in range(seg - 1, -1, -1):
                rows = pl.ds(SUBLANES * k, SUBLANES)
                h_r = hsr[rows, cols]
                h_i = hsi[rows, cols]
                a_r = a_r + g_r * h_r + g_i * h_i
                a_i = a_i + g_i * h_r - g_r * h_i
                n_r, n_i = _cmul(ar, ai, g_r, g_i)
                g_r = n_r + gr[rows, cols]
                g_i = n_i + gi[rows, cols]
                gr[rows, cols] = g_r
                gi[rows, cols] = g_i
            dar_ref[:, cols] += a_r
            dai_ref[:, cols] += a_i

        for stage in (state_cotangents, scan, input_cotangents):
            for j in range(N_USLAB):
                stage(j)
        du = dy * dsk_ref[...] + jnp.concatenate(dus, axis=1)
        du_ref[...] = _dot_tn(perm, du.astype(BF16)).astype(BF16)

        @pl.when(pl.program_id(0) == nt - 1)
        def _():
            for j in range(N_USLAB):
                dct_own_ref[j] = _own_blocks(dct_ref, j)
                dbb_own_ref[j] = _own_blocks(dbb_ref, j)

    big = (TS, N_STATE)
    return pl.pallas_call(
        _following(body, after), grid=(nt,),
        in_specs=_anywhere(after) + [rev(D_SSM), rev(D_SSM), rev(D_SSM), rev(N_STATE), rev(N_STATE), _const((TS, TS)),
                  _const(STATE_TILE), _const(STATE_TILE), _const(S5_IN), _const(S5_OUT), _const((1, D_SSM)),
                  _const((D_SSM, D_SSM)), _const((1, D_SSM))],
        out_specs=[rev(D_SSM), _acc(S5_OWN), _acc(S5_OWN), _acc(STATE_TILE), _acc(STATE_TILE),
                   _acc((SUBLANES, D_SSM)), _acc((D_SSM, D_SSM)), _acc((SUBLANES, D_SSM))],
        out_shape=[jax.ShapeDtypeStruct((s, D_SSM), BF16), jax.ShapeDtypeStruct(S5_OWN, F32),
                   jax.ShapeDtypeStruct(S5_OWN, F32), jax.ShapeDtypeStruct(STATE_TILE, F32),
                   jax.ShapeDtypeStruct(STATE_TILE, F32), jax.ShapeDtypeStruct((SUBLANES, D_SSM), F32),
                   jax.ShapeDtypeStruct((D_SSM, D_SSM), F32), jax.ShapeDtypeStruct((SUBLANES, D_SSM), F32)],
        scratch_shapes=[pltpu.VMEM(big, F32), pltpu.VMEM(big, F32), pltpu.VMEM(big, F32), pltpu.VMEM(big, F32),
                        pltpu.VMEM(STATE_TILE, F32), pltpu.VMEM(STATE_TILE, F32), pltpu.VMEM(STATE_TILE, F32),
                        pltpu.VMEM(STATE_TILE, F32), pltpu.VMEM((1, N_STATE), F32), pltpu.VMEM((1, N_STATE), F32),
                        pltpu.VMEM(S5_IN, F32), pltpu.VMEM(S5_IN, F32)],
        name=f"s5_bwd_{tag}", compiler_params=_cp(dimension_semantics=("arbitrary",)),
    )(*after, dout, u, y, h_re, h_im, _scan_order(), p["a_re8"], p["a_im8"], p["cb3"], p["bb3"], p["d_skip"], p["w_glu"],
      p["b_glu"])


def _mix_bwd(dza, db, dc, pooled, zuv, x0, dx1, p, tag, after=()):
    s = x0.shape[0]
    nt = s // TS

    def rev(n):
        return pl.BlockSpec((TS, n), lambda i: (nt - 1 - i, 0))

    def body(da_ref, db_ref, dc_ref, po_ref, z_ref, x_ref, dx1_ref, wl_ref, wp_ref, wpt_ref, sc_ref, lng_ref, lnb_ref,
             wsp_ref, wspt_ref, bias_ref, win_ref, g_ref,
             dx0_ref, dz_ref, dwp_ref, dsc_ref, dws_ref, dbias_ref, dlng_ref, dlnb_ref, dg_ref, buf, *tmp):
        step = pl.program_id(0)
        i = nt - 1 - step

        @pl.when(step == 0)
        def _():
            for ref in (dwp_ref, dsc_ref, dws_ref, dbias_ref, dlng_ref, dlnb_ref, dg_ref):
                ref[...] = jnp.zeros_like(ref)
            for ref in (buf, *tmp):
                ref[pl.ds(TS, POOL_ROWS - TS), :] = jnp.zeros((POOL_ROWS - TS, D_POOL), F32)

        wl = wl_ref[...]
        sc = sc_ref[...]
        dob = db_ref[...]
        pooled_b = po_ref[...]
        dsc_ref[...] += _colsum8(dob * _dot(pooled_b, wp_ref[...]))
        dmixb = (dob * sc).astype(BF16)
        dwp_ref[...] += _dot_tn(pooled_b, dmixb)
        dpool = _dot(dmixb, wpt_ref[...])
        dq = dpool / _pool_count(i, TS, wl)
        buf[pl.ds(0, TS), :] = dq
        dzb = _window_sum(buf, tmp, 0, wl, 1) - dpool
        buf[pl.ds(TS, MAX_WINDOW), :] = dq[:MAX_WINDOW, :]

        lo, hi = _half_masks()
        lng = lng_ref[...]
        u, vn, rs, vl, gelu_du, gelu_dv = _sgu_front(z_ref[...], lng, lnb_ref[...], grads=True)
        mixed = _sgu_mix(vl, wsp_ref, lo, hi) + jnp.tile(bias_ref[...], (TS // CHUNK, 1))
        doc = dc_ref[...]
        dzu = doc * mixed * gelu_du
        dmix = doc * u
        dbias = dbias_ref[...]
        for c in range(TS // CHUNK):
            dmc = dmix[CHUNK * c:CHUNK * (c + 1), :]
            dbias = dbias + dmc
            vlc = vl[CHUNK * c:CHUNK * (c + 1), :].astype(BF16)
            for q in range(SGU_HEADS // 2):
                dmq = _lanes(dmc, q)
                vq = _lanes(vlc, q)
                dws_ref[2 * q] += _dot_nt((dmq * lo).astype(BF16), vq)
                dws_ref[2 * q + 1] += _dot_nt((dmq * hi).astype(BF16), vq)
        dbias_ref[...] = dbias
        dvl = _sgu_mix(dmix, wspt_ref, lo, hi)
        dlng_ref[...] += _colsum8(dvl * vn)
        dlnb_ref[...] += _colsum8(dvl)
        dvn = dvl * lng
        dv = rs * (dvn - jnp.mean(dvn, axis=-1, keepdims=True) - vn * jnp.mean(dvn * vn, axis=-1, keepdims=True))

        dz = jnp.concatenate([da_ref[...], dzb.astype(BF16), dzu.astype(BF16), (dv * gelu_dv).astype(BF16)], axis=1)
        dz_ref[...] = dz
        n, r = _rms(x_ref[...])
        dxn, dgp = _rms_bwd(_dot(dz, win_ref[...]), n, r, g_ref[...])
        dg_ref[...] += _colsum8(dgp)
        dx0_ref[...] = dx1_ref[...] + dxn

    pair = (SGU_HEADS // 2, CHUNK, 2 * CHUNK)
    return pl.pallas_call(
        _following(body, after), grid=(nt,),
        in_specs=_anywhere(after) + [rev(D_SSM), rev(D_POOL), rev(D_SGU), rev(D_POOL), rev(2 * D_SGU), rev(D_MODEL), rev(D_MODEL),
                  _const((1, D_POOL)), _const((D_POOL, D_POOL)), _const((D_POOL, D_POOL)), _const((1, D_POOL)),
                  _const((1, D_SGU)), _const((1, D_SGU)), _const(pair), _const(pair), _const((CHUNK, D_SGU)),
                  _const((D_IN, D_MODEL)), _const((1, D_MODEL))],
        out_specs=[rev(D_MODEL), rev(D_IN), _acc((D_POOL, D_POOL)), _acc((SUBLANES, D_POOL)),
                   _acc((SGU_HEADS, CHUNK, CHUNK)), _acc((CHUNK, D_SGU)), _acc((SUBLANES, D_SGU)),
                   _acc((SUBLANES, D_SGU)), _acc((SUBLANES, D_MODEL))],
        out_shape=[jax.ShapeDtypeStruct((s, D_MODEL), F32), jax.ShapeDtypeStruct((s, D_IN), BF16),
                   jax.ShapeDtypeStruct((D_POOL, D_POOL), F32), jax.ShapeDtypeStruct((SUBLANES, D_POOL), F32),
                   jax.ShapeDtypeStruct((SGU_HEADS, CHUNK, CHUNK), F32), jax.ShapeDtypeStruct((CHUNK, D_SGU), F32),
                   jax.ShapeDtypeStruct((SUBLANES, D_SGU), F32), jax.ShapeDtypeStruct((SUBLANES, D_SGU), F32),
                   jax.ShapeDtypeStruct((SUBLANES, D_MODEL), F32)],
        scratch_shapes=[pltpu.VMEM((POOL_ROWS, D_POOL), F32)] * 4,
        name=f"mix_bwd_{tag}", compiler_params=_cp(dimension_semantics=("arbitrary",)),
    )(*after, dza, db, dc, pooled, zuv, x0, dx1, _pool_consts(), p["w_pool_bd"], p["w_pool_bd_t"], p["pool_scale"],
      p["sgu_ln_g"], p["sgu_ln_b"], p["ws_pair"], p["ws_pair_t"], p["bias_sp"], p["w_in"], p["g_mix"])


def _atb(a, b, tag, after=()):
    s, ka = a.shape
    kb = b.shape[1]
    ts = ATB_ROWS
    tn = min(kb, ATB_COLS)
    ns = s // ts

    def body(a_ref, b_ref, *rest):
        o_ref = rest[-1]

        @pl.when(pl.program_id(1) == 0)
        def _():
            o_ref[...] = jnp.zeros_like(o_ref)

        o_ref[...] += _dot_tn(a_ref[...].astype(BF16), b_ref[...].astype(BF16))

    return pl.pallas_call(
        body, grid=(kb // tn, ns),
        in_specs=[pl.BlockSpec((ts, ka), lambda j, i: (i, 0)), pl.BlockSpec((ts, tn), lambda j, i: (i, j))]
        + [pl.BlockSpec(memory_space=pl.ANY)] * len(after),
        out_specs=pl.BlockSpec((ka, tn), lambda j, i: (0, j)),
        out_shape=jax.ShapeDtypeStruct((ka, kb), F32),
        name=f"atb_{tag}", compiler_params=_cp(dimension_semantics=("arbitrary", "arbitrary")),
    )(a, b, *after)


def _s5_discretise(a_re, a_im, log_dt, b_re, b_im):
    dt = jnp.exp(log_dt)[:, None]
    mag = jnp.exp(a_re * dt)
    ar = mag * jnp.cos(a_im * dt)
    ai = mag * jnp.sin(a_im * dt)
    den = a_re * a_re + a_im * a_im
    f_re = ((ar - 1.0) * a_re + ai * a_im) / den
    f_im = (ai * a_re - (ar - 1.0) * a_im) / den
    bb_re = f_re[..., None] * b_re - f_im[..., None] * b_im
    bb_im = f_re[..., None] * b_im + f_im[..., None] * b_re
    return ar, ai, bb_re, bb_im


def _block_diag(blocks):
    g, r, c = blocks.shape
    eye = jnp.eye(g, dtype=blocks.dtype)
    return (blocks[:, :, None, :] * eye[:, None, :, None]).reshape(g * r, g * c)


def _block_diag_extract(m, g):
    r = m.shape[0] // g
    c = m.shape[1] // g
    eye = jnp.eye(g, dtype=m.dtype)
    return jnp.sum(m.reshape(g, r, g, c) * eye[:, None, :, None], axis=2)


GROUPS_PER_SLAB = N_GROUPS // N_USLAB


def _slab_diag(blocks):
    k = GROUPS_PER_SLAB
    _, r, c = blocks.shape
    eye = jnp.eye(k, dtype=blocks.dtype)
    spread = blocks.reshape(N_USLAB, k, r, 1, c) * eye[None, :, None, :, None]
    return spread.reshape(N_USLAB, k * r, k * c)


def _state_slabs(v):
    return jnp.broadcast_to(v.reshape(1, N_STATE), STATE_TILE)


def _tril():
    return jnp.tril(jnp.ones((CHUNK, CHUNK), dtype=bool))


def _layer_params(w, l):
    row = lambda v: v.reshape(1, -1)
    t = lambda m: jnp.swapaxes(m, -1, -2)
    ar, ai, bb_re, bb_im = _s5_discretise(w["A_re"][l], w["A_im"][l], w["log_dt"][l], w["B_re"][l], w["B_im"][l])
    bbt3 = jnp.concatenate([_slab_diag(t(bb_re)), _slab_diag(t(bb_im))], axis=2).astype(BF16)
    ct3 = jnp.concatenate([_slab_diag(t(w["C_re"][l])), -_slab_diag(t(w["C_im"][l]))], axis=1).astype(BF16)
    ws = jnp.where(_tril()[None], w["w_spatial"][l], 0.0)
    pair = lambda m: jnp.stack([jnp.concatenate([m[2 * q], m[2 * q + 1]], axis=1)
                                for q in range(SGU_HEADS // 2)]).astype(BF16)
    wp = _block_diag(w["w_pool"][l]).astype(BF16)
    p = dict(
        g_mix=row(w["g_mix"][l]), g_ffn=row(w["g_ffn"][l]), d_skip=row(w["D_skip"][l]), b_glu=row(w["b_glu"][l]),
        pool_scale=row(w["pool_scale"][l]), sgu_ln_g=row(w["sgu_ln_g"][l]), sgu_ln_b=row(w["sgu_ln_b"][l]),
        a_re8=_state_slabs(ar), a_im8=_state_slabs(ai),
        bbt3=bbt3, bb3=t(bbt3), ct3=ct3, cb3=t(ct3),
        w_pool_bd=wp, w_pool_bd_t=t(wp), ws_pair=pair(ws), ws_pair_t=pair(t(ws)),
        bias_sp=jnp.repeat(t(w["b_spatial"][l]), SGU_HEAD_DIM, axis=1),
    )
    return p


MIX_WEIGHTS = ("w_in", "w_glu")
FFN_WEIGHTS = ("w_out", "w_gate", "w_up", "w_down")


def _with_big(p, mats):
    p.update(mats)


def _rows_sum(v):
    return jnp.sum(v, axis=0)


ATB_COLS = 1024
ATB_ROWS = 1024


def _after(v, token):
    return v + token[0, 0]


def _layer_bwd(dx2, sv, p, w, l, tag, hooks, after):
    t = lambda m: jnp.swapaxes(m, -1, -2)
    dx1, da, db, dc, dgt, dup, act, dg_ffn = _blk_bwd(dx2, sv["x1"], sv["gt"], sv["up"], p, tag, after)
    after = hooks["tick"]([dx1])
    after = after + hooks["on_grads"](l, "ffn", {
        "w_down": _atb(act, dx2, tag + "_wd", after), "w_gate": _atb(dgt, sv["h2"], tag + "_wg", after),
        "w_up": _atb(dup, sv["h2"], tag + "_wu", after), "w_out": _atb(sv["ycat"], dx1, tag + "_wo", after)})
    g = {}
    g["g_ffn"] = _rows_sum(dg_ffn)
    dza, dc3, dbbt3, dar8, dai8, dd8, dwglu, dbglu8 = _s5_bwd(
        da, sv["za"], sv["y"], sv["h_re"], sv["h_im"], p, tag, after)
    after = hooks["tick"]([dza])
    dx0, dz, dwp, dsc8, dws, dbias, dlng8, dlnb8, dg_mix = _mix_bwd(
        dza, db, dc, sv["pooled"], sv["zuv"], sv["x0"], dx1, p, tag, after)
    g["g_mix"] = _rows_sum(dg_mix)
    g["b_glu"] = _rows_sum(dbglu8)
    g["D_skip"] = _rows_sum(dd8)
    dc_re, dc_im = _own_blocks_split(dc3)
    g["C_re"] = dc_re
    g["C_im"] = -dc_im
    dar = jnp.sum(dar8, axis=0).reshape(N_GROUPS, SSM_STATE)
    dai = jnp.sum(dai8, axis=0).reshape(N_GROUPS, SSM_STATE)
    dbb_re, dbb_im = [t(d) for d in _own_blocks_split(dbbt3)]
    _, disc_vjp = jax.vjp(_s5_discretise, w["A_re"][l], w["A_im"][l], w["log_dt"][l], w["B_re"][l], w["B_im"][l])
    g["A_re"], g["A_im"], g["log_dt"], g["B_re"], g["B_im"] = disc_vjp((dar, dai, dbb_re, dbb_im))
    g["w_pool"] = _block_diag_extract(dwp, len(POOL_WINDOWS))
    g["pool_scale"] = _rows_sum(dsc8)
    g["sgu_ln_g"] = _rows_sum(dlng8)
    g["sgu_ln_b"] = _rows_sum(dlnb8)
    g["w_spatial"] = jnp.where(_tril()[None], dws, 0.0)
    g["b_spatial"] = t(jnp.sum(dbias.reshape(CHUNK, SGU_HEADS, SGU_HEAD_DIM), axis=-1))
    after = hooks["on_small"](l, g)
    after = hooks["on_grads"](l, "mix", {"w_in": _atb(dz, sv["h1"], tag + "_wi", after), "w_glu": dwglu})
    return dx0, after


def _local_step(x, target, w, hooks):
    params = [_layer_params(w, l) for l in range(DEPTH)]
    saved = []
    h = x
    for l in range(DEPTH):
        p, tag = params[l], f"l{l}"
        _with_big(p, hooks["get_big"](l, "mix", [h]))
        za, zuv, h1, ob, pooled, oc = _mix_fwd(h, p, tag)
        oa, y, h_re, h_im = _s5_fwd(za, p, tag)
        _with_big(p, hooks["get_big"](l, "ffn", [oa, ob, oc]))
        head = (target, w["g_final"].reshape(1, -1)) if l == DEPTH - 1 else None
        x1, x2, h2, gt, up, ycat, *loss_parts = _blk_fwd(h, oa, ob, oc, p, tag, head)
        saved.append(dict(x0=h, za=za, zuv=zuv, h1=h1, ycat=ycat, y=y, h_re=h_re, h_im=h_im, pooled=pooled, x1=x1,
                          h2=h2, gt=gt, up=up))
        h = x2
    dx = h
    loss8, dgf8 = loss_parts
    grads = [None] * DEPTH

    def on_small(l, g_l):
        grads[l] = g_l
        if l > 0:
            return []
        g = {n: jnp.stack([grads[k][n] for k in range(DEPTH)]) for n in SMALL if n != "g_final"}
        g["g_final"] = _rows_sum(dgf8)
        return hooks["on_small"](g, loss8[0, 0])

    after = []
    for l in reversed(range(DEPTH)):
        dx, after = _layer_bwd(dx, saved[l], params[l], w, l, f"l{l}", dict(hooks, on_small=on_small), after)
    return dx


_ANY = pl.BlockSpec(memory_space=pl.ANY)
_MESH = pl.DeviceIdType.MESH


def _place():
    return lax.axis_index("x"), lax.axis_index("y"), lax.axis_index("c")


def _other_chips(x, y):
    return [(1 - x, y), (x, 1 - y), (1 - x, 1 - y)]


def _dma_sems(n):
    return pltpu.SemaphoreType.DMA((n,))


def _remote(src, dst, send_sems, recv_sems, k, to):
    return pltpu.make_async_remote_copy(src_ref=src, dst_ref=dst, send_sem=send_sems.at[k], recv_sem=recv_sems.at[k],
                                        device_id=to, device_id_type=_MESH)


_HBM = pl.BlockSpec(memory_space=pltpu.HBM)
_SEM = pl.BlockSpec(memory_space=pltpu.SEMAPHORE)
_EFFECT = pltpu.SideEffectType.DATAFLOW_SIDE_EFFECTING
N_REL = N_CHIPS - 1


def _gather_plan(x, y, c, srcs, lands):
    plan = []
    for l in lands:
        r = l.shape[0] // N_CHIPS
        rows = l.at[pl.ds((2 * x + y) * r, r)]
        plan += [(rows, rows, (cx, cy, c)) for cx, cy in _other_chips(x, y)]
    return plan


def _half_rows(land, chip, c):
    h = land.shape[0] // (2 * N_CHIPS)
    return land.at[pl.ds((2 * chip + c) * h, h)]


def _gather_half_plan(x, y, c, srcs, lands):
    return [(_half_rows(l, 2 * x + y, c), _half_rows(l, 2 * x + y, c), (cx, cy, c))
            for l in lands for cx, cy in _other_chips(x, y)]


def _forward_halves(lands, tag):
    nw = len(lands)

    def body(*refs):
        ins = refs[:nw]
        send_sems, recv_sems = refs[2 * nw:]
        x, y, c = _place()
        chips = [2 * cx + cy for cx, cy in _other_chips(x, y)]
        sends = [_remote(_half_rows(ins[i], k, c), _half_rows(ins[i], k, c), send_sems, recv_sems, N_REL * i + j,
                         (x, y, 1 - c)) for i in range(nw) for j, k in enumerate(chips)]
        for cp in sends:
            cp.start()
        for i in range(nw):
            for j, k in enumerate(chips):
                sends[N_REL * i + j].wait_send()
                _remote(_half_rows(ins[i], k, c), _half_rows(ins[i], k, 1 - c), send_sems, recv_sems, N_REL * i + j,
                        (x, y, 1 - c)).wait_recv()

    return pl.pallas_call(
        body, out_shape=[jax.ShapeDtypeStruct(a.shape, a.dtype) for a in lands], in_specs=[_ANY] * nw,
        out_specs=[_ANY] * nw, input_output_aliases={i: i for i in range(nw)},
        scratch_shapes=[_dma_sems(N_REL * nw), _dma_sems(N_REL * nw)], name=f"forward_halves_{tag}",
    )(*lands)


def _sibling_plan(x, y, c, srcs, lands):
    return [(s.at[:, 1 - c], l, (x, y, 1 - c)) for s, l in zip(srcs, lands)]


def _slab_plan(x, y, c, srcs, lands):
    return [(s.at[2 * cx + cy], l.at[j], (cx, cy, c))
            for s, l in zip(srcs, lands) for j, (cx, cy) in enumerate(_other_chips(x, y))]


def _plan_copies(plan, srcs, lands, send_sems, recv_sems):
    x, y, c = _place()
    return [_remote(s, d, send_sems, recv_sems, k, to) for k, (s, d, to) in enumerate(plan(x, y, c, srcs, lands))]


def _hbm(a):
    return pltpu.with_memory_space_constraint(a, pltpu.HBM)


def _everyone_plan(x, y, c, srcs, lands):
    me = 4 * x + 2 * y + c
    peers = [(x, y, 1 - c)] + [(cx, cy, cc) for cx, cy in _other_chips(x, y) for cc in (c, 1 - c)]
    return [(s, l.at[me], peer) for s, l in zip(srcs, lands) for peer in peers]


def _copies_start(name, plan, srcs, lands, ncopies):
    ns, n = len(srcs), len(srcs) + len(lands)

    def body(*refs):
        for cp in _plan_copies(plan, refs[:ns], refs[ns:n], refs[n], refs[n + 1]):
            cp.start()
        refs[-1][...] = jnp.zeros_like(refs[-1])

    ref_out = [pltpu.HBM(a.shape, a.dtype) for a in (*srcs, *lands)]
    out = pl.pallas_call(
        body, name=name, in_specs=[_HBM] * n,
        out_shape=(_dma_sems(ncopies), _dma_sems(ncopies), *ref_out, jax.ShapeDtypeStruct((SUBLANES, LANES), F32)),
        out_specs=(_SEM, _SEM, *[_HBM] * n, pl.BlockSpec(memory_space=pltpu.VMEM)),
        input_output_aliases={i: 2 + i for i in range(n)},
        compiler_params=pltpu.CompilerParams(has_side_effects=_EFFECT),
    )(*[_hbm(a) for a in (*srcs, *lands)])
    return dict(name=name, plan=plan, sems=out[:2], srcs=out[2:2 + ns], lands=out[2 + ns:2 + n], token=out[-1])


def _copies_wait(started, after):
    ns = len(started["srcs"])
    n = ns + len(started["lands"])
    plan = started["plan"]

    def body(*refs):
        for cp in _plan_copies(plan, refs[:ns], refs[ns:n], refs[n], refs[n + 1]):
            cp.wait_send()
            cp.wait_recv()

    args = (*started["srcs"], *started["lands"])
    out = pl.pallas_call(
        body, name=started["name"] + "_wait", out_shape=[pltpu.HBM(a.shape, a.dtype) for a in args],
        in_specs=[_HBM] * n + [_SEM, _SEM] + [_ANY] * len(after), out_specs=[_HBM] * n,
        input_output_aliases={i: i for i in range(n)},
        compiler_params=pltpu.CompilerParams(has_side_effects=_EFFECT),
    )(*args, *started["sems"], *after)
    return out[:ns], out[ns:]


def _place_shards(ws, layer, sel, after, tag):
    nw = len(ws)

    def body(sel_ref, *refs):
        for i in range(nw):
            refs[nw + len(after) + i][...] = refs[i][...].astype(BF16)

    return pl.pallas_call(
        body, grid_spec=pltpu.PrefetchScalarGridSpec(
            num_scalar_prefetch=1, grid=(1,),
            in_specs=[pl.BlockSpec((None,) + a.shape[1:], lambda i, s: (layer, 0, 0)) for a in ws] + [_ANY] * len(after),
            out_specs=[pl.BlockSpec(a.shape[1:], lambda i, s: (s[1], 0)) for a in ws]),
        out_shape=[jax.ShapeDtypeStruct((N_CHIPS * a.shape[1], a.shape[2]), BF16) for a in ws],
        name=f"place_shards_{tag}", compiler_params=_cp(dimension_semantics=("arbitrary",)),
    )(sel, *ws, *after)


def _share_halves(fs, layer, tag):
    nw = len(fs)

    def body(*refs):
        ins = refs[:nw]
        send_sems, recv_sems = refs[2 * nw:]
        x, y, c = _place()

        def half(i, who):
            h = ins[i].shape[1] // 2
            return ins[i].at[layer, pl.ds(who * h, h)]

        sends = [_remote(half(i, c), half(i, c), send_sems, recv_sems, i, (x, y, 1 - c)) for i in range(nw)]
        for cp in sends:
            cp.start()
        for i in range(nw):
            sends[i].wait_send()
            _remote(half(i, c), half(i, 1 - c), send_sems, recv_sems, i, (x, y, 1 - c)).wait_recv()

    return pl.pallas_call(
        body, out_shape=[jax.ShapeDtypeStruct(f.shape, f.dtype) for f in fs], in_specs=[_ANY] * nw,
        out_specs=[_ANY] * nw, input_output_aliases={i: i for i in range(nw)},
        scratch_shapes=[_dma_sems(nw), _dma_sems(nw)], name=f"share_halves_{tag}",
    )(*fs)


def _add_halves(g4s, recvs, sel, tag):
    nw = len(g4s)

    def body(sel_ref, *refs):
        for i in range(nw):
            refs[2 * nw + i][...] = (refs[i][...] + refs[nw + i][...]).astype(BF16)

    mine = [pl.BlockSpec((None, None) + g.shape[2:], lambda k, s: (k, s[0], 0, 0)) for g in g4s]
    slab = [pl.BlockSpec((None,) + g.shape[2:], lambda k, s: (k, 0, 0)) for g in g4s]
    return pl.pallas_call(
        body, grid_spec=pltpu.PrefetchScalarGridSpec(num_scalar_prefetch=1, grid=(N_CHIPS,), in_specs=mine + slab,
                                                     out_specs=slab),
        out_shape=[jax.ShapeDtypeStruct(r.shape, BF16) for r in recvs], name=f"add_halves_{tag}",
        compiler_params=_cp(dimension_semantics=("arbitrary",)),
    )(sel, *g4s, *recvs)


def _add_chips(ps, slabs, fs, layer, sel, tag):
    nw = len(ps)
    old = [f for f in fs if f is not None]

    def body(sel_ref, *refs):
        outs = refs[2 * nw + len(old):]
        for i in range(nw):
            acc = refs[i][...].astype(F32)
            for j in range(N_REL):
                acc = acc + refs[nw + i][j].astype(F32)
            outs[i][...] = acc

    shapes = [(DEPTH, 2 * p.shape[1], p.shape[2]) for p in ps]
    in_specs = [pl.BlockSpec((None,) + p.shape[1:], lambda i, s: (s[1], 0, 0)) for p in ps]
    in_specs += [pl.BlockSpec(sl.shape, lambda i, s: (0, 0, 0)) for sl in slabs]
    in_specs += [_ANY] * len(old)
    first_old = 1 + 2 * nw
    aliases, k = {}, 0
    for i, f in enumerate(fs):
        if f is not None:
            aliases[first_old + k] = i
            k += 1
    return pl.pallas_call(
        body, grid_spec=pltpu.PrefetchScalarGridSpec(
            num_scalar_prefetch=1, grid=(1,), in_specs=in_specs,
            out_specs=[pl.BlockSpec((None,) + p.shape[1:], lambda i, s: (layer, s[0], 0)) for p in ps]),
        out_shape=[jax.ShapeDtypeStruct(sh, F32) for sh in shapes], input_output_aliases=aliases,
        name=f"add_chips_{tag}", compiler_params=_cp(dimension_semantics=("arbitrary",)),
    )(sel, *ps, *slabs, *old)


def _adamw_math(w, g, m, v):
    m = ADAM_B1 * m + (1.0 - ADAM_B1) * g
    v = ADAM_B2 * v + (1.0 - ADAM_B2) * (g * g)
    m_hat = m / (1.0 - ADAM_B1 ** ADAM_STEP)
    v_hat = v / (1.0 - ADAM_B2 ** ADAM_STEP)
    delta = -ADAM_LR * (m_hat / (jnp.sqrt(v_hat) + ADAM_EPS) + ADAM_WD * w)
    return delta, m, v


ADAM_ROWS = 512


def _row_tile(rows, most):
    return max(t for t in range(SUBLANES, most + 1, SUBLANES) if rows % t == 0)


def _adamw(w, g, m, v, tag, after=()):
    depth, rows, cols = w.shape
    tr = _row_tile(rows, ADAM_ROWS)

    def body(w_ref, g_ref, m_ref, v_ref, *rest):
        d_ref, nm_ref, nv_ref = rest[len(after):]
        d, nm, nv = _adamw_math(w_ref[...], g_ref[...], m_ref[...], v_ref[...])
        d_ref[...] = d
        nm_ref[...] = nm
        nv_ref[...] = nv

    spec = pl.BlockSpec((None, tr, cols), lambda l, i: (l, i, 0))
    return pl.pallas_call(
        body, grid=(depth, rows // tr), in_specs=[spec] * 4 + [_ANY] * len(after), out_specs=[spec] * 3,
        out_shape=[jax.ShapeDtypeStruct(w.shape, F32)] * 3, name=f"adamw_{tag}",
        compiler_params=_cp(dimension_semantics=("arbitrary", "arbitrary")),
    )(w, g, m, v, *after)


SMALL_TILE = 512
PRECISE = ("g_final",)
COARSE = [n for n in SMALL if n not in PRECISE]


def _small_reduce(gathered):
    n = len(gathered)

    def body(*refs):
        for ga_ref, g_ref in zip(refs[:n], refs[n:]):
            g = ga_ref[0].astype(F32)
            for k in range(1, N_DEV):
                g = g + ga_ref[k].astype(F32)
            g_ref[...] = g

    return pl.pallas_call(
        body, out_shape=[jax.ShapeDtypeStruct(b.shape[1:], F32) for b in gathered], name="small_reduce",
        compiler_params=_cp(),
    )(*gathered)


def _small_adamw(g, w, m, v):
    names = list(g)
    n = len(names)
    shapes = {k: g[k].shape if g[k].ndim > 1 else (1,) + g[k].shape for k in names}

    def body(*refs):
        ins, outs = refs[:4 * n], refs[4 * n:]
        for i in range(n):
            d, nm, nv = _adamw_math(ins[n + i][...], ins[i][...], ins[2 * n + i][...], ins[3 * n + i][...])
            outs[i][...] = d
            outs[n + i][...] = nm
            outs[2 * n + i][...] = nv

    out = pl.pallas_call(
        body, out_shape=[jax.ShapeDtypeStruct(shapes[k], F32) for _ in range(3) for k in names], name="adamw_small",
        compiler_params=_cp(),
    )(*[src[k].reshape(shapes[k]) for src in (g, w, m, v) for k in names])
    return [{k: out[j * n + i].reshape(g[k].shape) for i, k in enumerate(names)} for j in range(3)]


def _exchange_form(n, a):
    return jnp.swapaxes(a, 1, 2) if n in TRANSPOSED else a


PACK_ROWS = 16


def _rows_of(size):
    return -(-size // (LANES * PACK_ROWS)) * PACK_ROWS


SMALL_VIEW = {"B_re": (0, 1, 3, 2), "B_im": (0, 1, 3, 2), "b_spatial": (1, 0, 2)}
assert all(tuple(order[i] for i in order) == tuple(range(len(order))) for order in SMALL_VIEW.values())


def _view(n, a):
    return jnp.transpose(a, SMALL_VIEW[n]) if n in SMALL_VIEW else a


def _pack(vals, names, extra=None):
    parts = [_view(n, vals[n]).reshape(-1) for n in names] + ([] if extra is None else [extra.reshape(1)])
    tiles = [jnp.pad(a, (0, _rows_of(a.size) * LANES - a.size)).reshape(-1, LANES) for a in parts]
    rows = sum(t.shape[0] for t in tiles)
    if rows > SMALL_TILE:
        tiles.append(jnp.zeros((-rows % SMALL_TILE, LANES), tiles[0].dtype))
    return jnp.concatenate(tiles, axis=0)


def _unpack(buf, like, names):
    out, row = {}, 0
    for n in names:
        rows = _rows_of(like[n].size)
        shape = tuple(like[n].shape[i] for i in SMALL_VIEW.get(n, range(like[n].ndim)))
        out[n] = buf[row:row + rows].reshape(-1)[:like[n].size].reshape(shape)
        row += rows
    return out, buf[row:]


def kernel(x, g_mix, w_in, A_re, A_im, log_dt, B_re, B_im, C_re, C_im, D_skip, w_glu, b_glu, w_pool, pool_scale, sgu_ln_g, sgu_ln_b, w_spatial, b_spatial, w_out, g_ffn, w_gate, w_up, w_down, g_final, loss_target, m_g_mix, m_w_in, m_A_re, m_A_im, m_log_dt, m_B_re, m_B_im, m_C_re, m_C_im, m_D_skip, m_w_glu, m_b_glu, m_w_pool, m_pool_scale, m_sgu_ln_g, m_sgu_ln_b, m_w_spatial, m_b_spatial, m_w_out, m_g_ffn, m_w_gate, m_w_up, m_w_down, m_g_final, v_g_mix, v_w_in, v_A_re, v_A_im, v_log_dt, v_B_re, v_B_im, v_C_re, v_C_im, v_D_skip, v_w_glu, v_b_glu, v_w_pool, v_pool_scale, v_sgu_ln_g, v_sgu_ln_b, v_w_spatial, v_b_spatial, v_w_out, v_g_ffn, v_w_gate, v_w_up, v_w_down, v_g_final):
    loc = locals()
    w = {n: loc[n] for n in WEIGHTS}
    m = {n: loc["m_" + n] for n in WEIGHTS}
    v = {n: loc["v_" + n] for n in WEIGHTS}
    sel = jnp.stack([lax.axis_index("c"), 2 * lax.axis_index("x") + lax.axis_index("y")]).astype(jnp.int32)

    chip = sel[1]

    halves = [(l, half) for l in range(DEPTH) for half in ("mix", "ffn")]
    two_level = {(0, "ffn")}
    names = {"mix": MIX_WEIGHTS, "ffn": FFN_WEIGHTS}
    started = {}
    wx = {n: _exchange_form(n, w[n]) for n in BIG}
    chain = []
    for l, half in halves:
        lands = _place_shards([wx[n] for n in names[half]], l, sel, chain, f"l{l}_{half}")
        plan = _gather_half_plan if (l, half) in two_level else _gather_plan
        started[l, half] = _copies_start(f"weights_l{l}_{half}", plan, [], lands, N_REL * len(lands))
        chain = [started[l, half]["token"]]
    w = dict(w, g_mix=_after(w["g_mix"], started[halves[-1]]["token"]))

    def get_big(l, half, after):
        lands = _copies_wait(started[l, half], after)[1]
        if (l, half) in two_level:
            lands = _forward_halves(lands, f"l{l}_{half}")
        return dict(zip(names[half], lands))

    result = {n: None for n in BIG}
    stage = {"swap": None, "slabs": None}

    def advance(after):
        if stage["slabs"] is not None:
            ex, ns, l, tag = stage["slabs"]
            part, slabs = _copies_wait(ex, after)
            bufs = _add_chips(part, slabs, [result[n] for n in ns], l, sel, tag)
            for n, f in zip(ns, _share_halves(bufs, l, tag)):
                result[n] = f
            stage["slabs"] = None
        if stage["swap"] is None:
            return []
        sw, ns, l, tag = stage["swap"]
        part = _add_halves(*_copies_wait(sw, after), sel, tag)
        slabs = [lax.empty((N_REL,) + p.shape[1:], BF16) for p in part]
        ex = _copies_start(f"grads_{tag}", _slab_plan, part, slabs, N_REL * len(part))
        stage["slabs"], stage["swap"] = (ex, ns, l, tag), None
        return [ex["token"]]

    def on_grads(l, half, grads):
        ns = list(grads)
        tag = f"l{l}_{half}"
        tokens = advance([grads[ns[0]]])
        g4s = [grads[n].reshape(N_CHIPS, 2, grads[n].shape[0] // (2 * N_CHIPS), grads[n].shape[1]) for n in ns]
        recvs = [lax.empty((N_CHIPS,) + g4.shape[2:], F32) for g4 in g4s]
        sw = _copies_start(f"swap_{tag}", _sibling_plan, g4s, recvs, len(g4s))
        stage["swap"] = (sw, ns, l, tag)
        return tokens + [sw["token"]]

    small = {}

    def on_small(g, loss_local):
        me = 2 * chip + sel[0]
        blocks = [_pack(g, COARSE).astype(BF16), _pack(g, PRECISE, loss_local)]
        lands = [lax.dynamic_update_slice(lax.empty((N_DEV,) + b.shape, b.dtype), b[None], (me, 0, 0)) for b in blocks]
        small.update(_copies_start("small_grads", _everyone_plan, blocks, lands, (N_DEV - 1) * len(blocks)))
        return [small["token"]]

    dx = _local_step(x[0], loss_target[0], w, dict(get_big=get_big, on_grads=on_grads, tick=advance, on_small=on_small))
    grads, deltas, new_m, new_v = {}, {}, {}, {}

    def update_big(ns, after):
        for n in ns:
            outs = _adamw(wx[n], result[n], _exchange_form(n, m[n]), _exchange_form(n, v[n]), n, after)
            grads[n], deltas[n], new_m[n], new_v[n] = [_exchange_form(n, a) for a in (result[n], *outs)]
            after = [outs[-1]]
        return after

    last_swap = stage["swap"][0]["token"]
    last_sent = advance(update_big(FFN_WEIGHTS[:1], [last_swap]))
    advance(update_big(FFN_WEIGHTS[1:], last_sent))
    _, gathered = _copies_wait(small, update_big(MIX_WEIGHTS, []))
    coarse, precise = _small_reduce(gathered)
    small_g, _ = _unpack(coarse, w, COARSE)
    precise_g, rest = _unpack(precise, w, PRECISE)
    small_g.update(precise_g)
    loss = rest[0, 0]
    views = [{n: _view(n, src[n]) for n in small_g} for src in (w, m, v)]
    for store, vals in zip((grads, deltas, new_m, new_v), (small_g, *_small_adamw(small_g, *views))):
        store.update({n: _view(n, a) for n, a in vals.items()})
    return (loss, dx[None], *[grads[n] for n in WEIGHTS], *[deltas[n] for n in WEIGHTS],
            *[new_m[n] for n in WEIGHTS], *[new_v[n] for n in WEIGHTS])
```

```python
import math

import jax
import jax.numpy as jnp
from jax import lax
from jax.experimental import pallas as pl
from jax.experimental.pallas import tpu as pltpu

F32 = jnp.float32
BF16 = jnp.bfloat16

D_MODEL = 1024
DEPTH = 2
D_SSM = 384
SSM_GROUP = 16
N_GROUPS = 24
SSM_STATE = 64
N_STATE = N_GROUPS * SSM_STATE
POOL_WINDOWS = (2, 4, 8, 16)
POOL_GROUP = 64
D_POOL = 256
MAX_WINDOW = 16
SGU_HEADS = 6
SGU_HEAD_DIM = 64
D_SGU = 384
CHUNK = 128
D_IN = D_SSM + D_POOL + 2 * D_SGU
D_FF = 2816
EPS = 1e-6

ADAM_LR = 0.001
ADAM_B1 = 0.9
ADAM_B2 = 0.999
ADAM_EPS = 1e-08
ADAM_WD = 0.01
ADAM_STEP = 10

LANES = 128
SUBLANES = 8
VMEM_LIMIT = 56 * 1024 * 1024

TS = 512
TS_FFN = 256

WEIGHTS = ['g_mix', 'w_in', 'A_re', 'A_im', 'log_dt', 'B_re', 'B_im', 'C_re', 'C_im', 'D_skip', 'w_glu', 'b_glu',
           'w_pool', 'pool_scale', 'sgu_ln_g', 'sgu_ln_b', 'w_spatial', 'b_spatial', 'w_out', 'g_ffn', 'w_gate',
           'w_up', 'w_down', 'g_final']
BIG = ['w_in', 'w_glu', 'w_out', 'w_gate', 'w_up', 'w_down']
SMALL = [n for n in WEIGHTS if n not in BIG]
TRANSPOSED = ("w_in", "w_gate", "w_up")
N_CHIPS = 4
N_DEV = 8


def _cp(**kw):
    return pltpu.CompilerParams(vmem_limit_bytes=VMEM_LIMIT, **kw)


def _row(ts, n):
    return pl.BlockSpec((ts, n), lambda i: (i, 0))


def _const(shape):
    nd = len(shape)
    return pl.BlockSpec(shape, lambda i: (0,) * nd, pipeline_mode=pl.Buffered(1))


def _acc(shape):
    nd = len(shape)
    return pl.BlockSpec(shape, lambda i: (0,) * nd)


def _following(body, after):
    k = len(after)
    return lambda *refs: body(*refs[k:])


def _anywhere(after):
    return [pl.BlockSpec(memory_space=pl.ANY)] * len(after)


def _dot(a, b):
    return jnp.dot(a, b, preferred_element_type=F32)


def _dot_tn(a, b):
    return lax.dot_general(a, b, (((0,), (0,)), ((), ())), preferred_element_type=F32)


def _dot_nt(a, b):
    return lax.dot_general(a, b, (((1,), (1,)), ((), ())), preferred_element_type=F32)


_G0 = math.sqrt(2.0 / math.pi)
_G1 = 0.044715


def _gelu(x):
    return 0.5 * x * (1.0 + jnp.tanh(_G0 * (x + _G1 * x * x * x)))


def _gelu_and_grad(x):
    t = jnp.tanh(_G0 * (x + _G1 * x * x * x))
    half = 0.5 * (1.0 + t)
    return x * half, half + 0.5 * x * (1.0 - t * t) * (_G0 * (1.0 + 3.0 * _G1 * x * x))


def _sigmoid(x):
    return 1.0 / (1.0 + jnp.exp(-x))


def _rms(x):
    r = lax.rsqrt(jnp.mean(x * x, axis=-1, keepdims=True) + EPS)
    return x * r, r


def _rms_bwd(dh, n, r, g):
    dn = dh * g
    return r * (dn - n * jnp.mean(dn * n, axis=-1, keepdims=True)), dh * n


def _colsum8(v):
    rows, n = v.shape
    return jnp.sum(v.reshape(rows // SUBLANES, SUBLANES, n), axis=0)


def _cmul(ar, ai, br, bi):
    return ar * br - ai * bi, ar * bi + ai * br


def _cpow(ar, ai, n):
    assert n & (n - 1) == 0
    while n > 1:
        ar, ai = _cmul(ar, ai, ar, ai)
        n //= 2
    return ar, ai


N_USLAB = D_SSM // LANES
SEG = TS // SUBLANES
SLAB_STATES = N_STATE // N_USLAB
S5_IN = (N_USLAB, LANES, 2 * SLAB_STATES)
S5_OUT = (N_USLAB, 2 * SLAB_STATES, LANES)
STATE_TILE = (SUBLANES, N_STATE)


def _scan_order():
    p = jnp.arange(TS)
    src = (p % SUBLANES) * SEG + p // SUBLANES
    return (src[:, None] == jnp.arange(TS)[None, :]).astype(BF16)


def _to_scan_order(perm, v):
    hi = v.astype(BF16)
    lo = (v - hi.astype(F32)).astype(BF16)
    return _dot(perm, hi) + _dot(perm, lo)


def _scan_rows(k):
    return pl.ds(pl.multiple_of(k * SUBLANES, SUBLANES), SUBLANES)


def _lanes(v, j):
    return v[:, LANES * j:LANES * (j + 1)]


def _states(j):
    return pl.ds(SLAB_STATES * j, SLAB_STATES)


def _state_split(re_ref, im_ref, j, v):
    re_ref[:, _states(j)] = v[:, :SLAB_STATES]
    im_ref[:, _states(j)] = v[:, SLAB_STATES:]


def _state_cat(re_ref, im_ref, j):
    return jnp.concatenate([re_ref[:, _states(j)], im_ref[:, _states(j)]], axis=1).astype(BF16)


def _s5_fwd(u, p, tag):
    s = u.shape[0]
    seg = SEG

    def body(u_ref, perm_ref, bbt_ref, ar_ref, ai_ref, ct_ref, dsk_ref, wglu_ref, bglu_ref,
             oa_ref, y_ref, hr_ref, hi_ref, sr, si, er, ei, ir, ii, cr, ci):
        @pl.when(pl.program_id(0) == 0)
        def _():
            cr[...] = jnp.zeros_like(cr)
            ci[...] = jnp.zeros_like(ci)

        perm = perm_ref[...]
        uv = _to_scan_order(perm, u_ref[...])
        ub = uv.astype(BF16)
        for j in range(N_USLAB):
            _state_split(sr, si, j, _dot(_lanes(ub, j), bbt_ref[j]))
        for j in range(N_USLAB):
            cols = _states(j)
            ar = ar_ref[:, cols]
            ai = ai_ref[:, cols]
            h_r = h_i = jnp.zeros((SUBLANES, SLAB_STATES), F32)
            for k in range(seg):
                rows = pl.ds(SUBLANES * k, SUBLANES)
                n_r, n_i = _cmul(ar, ai, h_r, h_i)
                h_r = n_r + sr[rows, cols]
                h_i = n_i + si[rows, cols]
            er[:, cols] = h_r
            ei[:, cols] = h_i
            pr, pi = _cpow(ar[0:1, :], ai[0:1, :], seg)
            c_r = cr[:, cols]
            c_i = ci[:, cols]
            for q in range(SUBLANES):
                ir[q:q + 1, cols] = c_r
                ii[q:q + 1, cols] = c_i
                n_r, n_i = _cmul(pr, pi, c_r, c_i)
                c_r = n_r + er[q:q + 1, cols]
                c_i = n_i + ei[q:q + 1, cols]
            cr[:, cols] = c_r
            ci[:, cols] = c_i
            h_r = ir[:, cols]
            h_i = ii[:, cols]
            for k in range(seg):
                rows = pl.ds(SUBLANES * k, SUBLANES)
                n_r, n_i = _cmul(ar, ai, h_r, h_i)
                h_r = n_r + sr[rows, cols]
                h_i = n_i + si[rows, cols]
                sr[rows, cols] = h_r
                si[rows, cols] = h_i
        hr_ref[...] = sr[...].astype(BF16)
        hi_ref[...] = si[...].astype(BF16)
        y = jnp.concatenate([_dot(_state_cat(hr_ref, hi_ref, j), ct_ref[j]) for j in range(N_USLAB)], axis=1)
        y = y + dsk_ref[...] * uv
        y_ref[...] = y
        g = _gelu(y)
        pre = _dot(g.astype(BF16), wglu_ref[...]) + bglu_ref[...]
        oa_ref[...] = _dot_tn(perm, (g * _sigmoid(pre)).astype(BF16)).astype(BF16)

    return pl.pallas_call(
        body, grid=(s // TS,),
        in_specs=[_row(TS, D_SSM), _const((TS, TS)), _const(S5_IN), _const(STATE_TILE), _const(STATE_TILE),
                  _const(S5_OUT), _const((1, D_SSM)), _const((D_SSM, D_SSM)), _const((1, D_SSM))],
        out_specs=[_row(TS, D_SSM), _row(TS, D_SSM), _row(TS, N_STATE), _row(TS, N_STATE)],
        out_shape=[jax.ShapeDtypeStruct((s, D_SSM), BF16), jax.ShapeDtypeStruct((s, D_SSM), F32),
                   jax.ShapeDtypeStruct((s, N_STATE), BF16), jax.ShapeDtypeStruct((s, N_STATE), BF16)],
        scratch_shapes=[pltpu.VMEM((TS, N_STATE), F32), pltpu.VMEM((TS, N_STATE), F32),
                        pltpu.VMEM(STATE_TILE, F32), pltpu.VMEM(STATE_TILE, F32), pltpu.VMEM(STATE_TILE, F32),
                        pltpu.VMEM(STATE_TILE, F32), pltpu.VMEM((1, N_STATE), F32), pltpu.VMEM((1, N_STATE), F32)],
        name=f"s5_fwd_{tag}", compiler_params=_cp(dimension_semantics=("arbitrary",)),
    )(u, _scan_order(), p["bbt3"], p["a_re8"], p["a_im8"], p["ct3"], p["d_skip"], p["w_glu"], p["b_glu"])


def _pool_consts():
    w = jnp.repeat(jnp.asarray(POOL_WINDOWS, F32), POOL_GROUP)[None, :]
    return w


POOL_PAD = SUBLANES
POOL_ROWS = TS + MAX_WINDOW + POOL_PAD


def _window_sum(buf, tmp, first, wl, step):
    assert POOL_WINDOWS == (2, 4, 8, 16)
    n = TS + MAX_WINDOW
    lo = first - MAX_WINDOW if step < 0 else first
    src = buf
    for k, dst in zip((1, 2, 4), tmp):
        dst[pl.ds(lo, n), :] = src[pl.ds(lo, n), :] + src[pl.ds(lo + step * k, n), :]
        src = dst
    s2, s4, s8 = (t[pl.ds(first, TS), :] for t in tmp)
    s16 = s8 + tmp[2][pl.ds(first + step * 8, TS), :]
    return jnp.where(wl == 2, s2, jnp.where(wl == 4, s4, jnp.where(wl == 8, s8, s16)))


def _pool_count(i, rows, wl):
    t = (i * TS + 1).astype(F32) + lax.broadcasted_iota(jnp.int32, (rows, 1), 0).astype(F32)
    return jnp.minimum(t, wl)


def _sgu_mix(vl, wpair_ref, lo, hi):
    rows = vl.shape[0]
    chunks = []
    for c in range(rows // CHUNK):
        vc = vl[CHUNK * c:CHUNK * (c + 1), :]
        parts = []
        for q in range(SGU_HEADS // 2):
            vq = vc[:, LANES * q:LANES * (q + 1)]
            rhs = jnp.concatenate([vq * lo, vq * hi], axis=0).astype(BF16)
            parts.append(_dot(wpair_ref[q], rhs))
        chunks.append(jnp.concatenate(parts, axis=1))
    return jnp.concatenate(chunks, axis=0)


def _sgu_front(zuv, lng, lnb, grads=False):
    gelu = _gelu_and_grad if grads else lambda z: (_gelu(z), None)
    u, du = gelu(zuv[:, :D_SGU])
    v, dv = gelu(zuv[:, D_SGU:])
    mu = jnp.mean(v, axis=-1, keepdims=True)
    vc = v - mu
    rs = lax.rsqrt(jnp.mean(vc * vc, axis=-1, keepdims=True) + EPS)
    vn = vc * rs
    return u, vn, rs, vn * lng + lnb, du, dv


def _half_masks():
    lane = lax.broadcasted_iota(jnp.int32, (1, LANES), 1)
    lo = (lane < SGU_HEAD_DIM).astype(F32)
    return lo, 1.0 - lo


def _mix_fwd(x, p, tag):
    s = x.shape[0]

    def body(x_ref, g_ref, w_ref, wl_ref, wp_ref, sc_ref, lng_ref, lnb_ref, wsp_ref, bias_ref,
             za_ref, zuv_ref, h_ref, ob_ref, pooled_ref, oc_ref, buf, *tmp):
        i = pl.program_id(0)
        tile0 = POOL_PAD + MAX_WINDOW

        @pl.when(i == 0)
        def _():
            for ref in (buf, *tmp):
                ref[pl.ds(0, tile0), :] = jnp.zeros((tile0, D_POOL), F32)

        n, _ = _rms(x_ref[...])
        h = (n * g_ref[...]).astype(BF16)
        h_ref[...] = h
        z = _dot_nt(h, w_ref[...])
        za_ref[...] = z[:, :D_SSM]
        zb = z[:, D_SSM:D_SSM + D_POOL]
        zuv = z[:, D_SSM + D_POOL:]
        zuv_ref[...] = zuv
        buf[pl.ds(tile0, TS), :] = zb
        wl = wl_ref[...]
        pooled = (_window_sum(buf, tmp, tile0, wl, -1) / _pool_count(i, TS, wl) - zb).astype(BF16)
        buf[pl.ds(POOL_PAD, MAX_WINDOW), :] = zb[TS - MAX_WINDOW:, :]
        pooled_ref[...] = pooled
        ob_ref[...] = (_dot(pooled, wp_ref[...]) * sc_ref[...]).astype(BF16)
        lo, hi = _half_masks()
        u, _, _, vl, _, _ = _sgu_front(zuv, lng_ref[...], lnb_ref[...])
        mixed = _sgu_mix(vl, wsp_ref, lo, hi) + jnp.tile(bias_ref[...], (TS // CHUNK, 1))
        oc_ref[...] = (u * mixed).astype(BF16)

    return pl.pallas_call(
        body, grid=(s // TS,),
        in_specs=[_row(TS, D_MODEL), _const((1, D_MODEL)), _const((D_IN, D_MODEL)), _const((1, D_POOL)),
                  _const((D_POOL, D_POOL)), _const((1, D_POOL)), _const((1, D_SGU)), _const((1, D_SGU)),
                  _const((SGU_HEADS // 2, CHUNK, 2 * CHUNK)), _const((CHUNK, D_SGU))],
        out_specs=[_row(TS, D_SSM), _row(TS, 2 * D_SGU), _row(TS, D_MODEL), _row(TS, D_POOL), _row(TS, D_POOL),
                   _row(TS, D_SGU)],
        out_shape=[jax.ShapeDtypeStruct((s, D_SSM), F32), jax.ShapeDtypeStruct((s, 2 * D_SGU), F32),
                   jax.ShapeDtypeStruct((s, D_MODEL), BF16), jax.ShapeDtypeStruct((s, D_POOL), BF16),
                   jax.ShapeDtypeStruct((s, D_POOL), BF16), jax.ShapeDtypeStruct((s, D_SGU), BF16)],
        scratch_shapes=[pltpu.VMEM((POOL_ROWS, D_POOL), F32)] * 4,
        name=f"mix_fwd_{tag}", compiler_params=_cp(dimension_semantics=("arbitrary",)),
    )(x, p["g_mix"], p["w_in"], _pool_consts(), p["w_pool_bd"], p["pool_scale"], p["sgu_ln_g"], p["sgu_ln_b"],
      p["ws_pair"], p["bias_sp"])


def _blk_fwd(x0, oa, ob, oc, p, tag, head=None):
    s = x0.shape[0]
    ts = TS_FFN
    n_head = 0 if head is None else len(head)

    def body(x0_ref, oa_ref, ob_ref, oc_ref, wo_ref, g_ref, wg_ref, wu_ref, wd_ref, *refs):
        x1_ref, x2_ref, h2_ref, gt_ref, up_ref, ycat_ref = refs[n_head:n_head + 6]
        ycat = jnp.concatenate([oa_ref[...], ob_ref[...], oc_ref[...]], axis=1)
        ycat_ref[...] = ycat
        x1 = x0_ref[...] + _dot(ycat, wo_ref[...])
        x1_ref[...] = x1
        n, _ = _rms(x1)
        h2 = (n * g_ref[...]).astype(BF16)
        h2_ref[...] = h2
        gt = _dot_nt(h2, wg_ref[...])
        up = _dot_nt(h2, wu_ref[...])
        gt_ref[...] = gt.astype(BF16)
        up_ref[...] = up.astype(BF16)
        act = (gt * _sigmoid(gt) * up).astype(BF16)
        x2 = x1 + _dot(act, wd_ref[...])
        if head is None:
            x2_ref[...] = x2
            return
        t_ref, gf_ref = refs[:n_head]
        loss_ref, dgf_ref = refs[n_head + 6:]

        @pl.when(pl.program_id(0) == 0)
        def _():
            loss_ref[...] = jnp.zeros_like(loss_ref)
            dgf_ref[...] = jnp.zeros_like(dgf_ref)

        gf = gf_ref[...]
        nf, rf = _rms(x2)
        diff = nf * gf - t_ref[...]
        loss_ref[...] += jnp.sum(diff * diff) * (0.5 / D_MODEL)
        dxn, dgp = _rms_bwd(diff * (1.0 / D_MODEL), nf, rf, gf)
        dgf_ref[...] += _colsum8(dgp)
        x2_ref[...] = dxn

    in_specs = [_row(ts, D_MODEL), _row(ts, D_SSM), _row(ts, D_POOL), _row(ts, D_SGU), _const((D_MODEL, D_MODEL)),
                _const((1, D_MODEL)), _const((D_FF, D_MODEL)), _const((D_FF, D_MODEL)), _const((D_FF, D_MODEL))]
    out_specs = [_row(ts, D_MODEL), _row(ts, D_MODEL), _row(ts, D_MODEL), _row(ts, D_FF), _row(ts, D_FF),
                 _row(ts, D_MODEL)]
    out_shape = [jax.ShapeDtypeStruct((s, D_MODEL), F32), jax.ShapeDtypeStruct((s, D_MODEL), F32),
                 jax.ShapeDtypeStruct((s, D_MODEL), BF16), jax.ShapeDtypeStruct((s, D_FF), BF16),
                 jax.ShapeDtypeStruct((s, D_FF), BF16), jax.ShapeDtypeStruct((s, D_MODEL), BF16)]
    args = (x0, oa, ob, oc, p["w_out"], p["g_ffn"], p["w_gate"], p["w_up"], p["w_down"])
    if head is not None:
        in_specs += [_row(ts, D_MODEL), _const((1, D_MODEL))]
        out_specs += [_acc((SUBLANES, LANES)), _acc((SUBLANES, D_MODEL))]
        out_shape += [jax.ShapeDtypeStruct((SUBLANES, LANES), F32), jax.ShapeDtypeStruct((SUBLANES, D_MODEL), F32)]
        args += tuple(head)
    return pl.pallas_call(
        body, grid=(s // ts,), in_specs=in_specs, out_specs=out_specs, out_shape=out_shape,
        name=f"blk_fwd_{tag}", compiler_params=_cp(dimension_semantics=("arbitrary",)),
    )(*args)


def _blk_bwd(dx2, x1, gt, up, p, tag, after=()):
    s = dx2.shape[0]
    ts = TS_FFN

    def body(dx2_ref, x1_ref, gt_ref, up_ref, wd_ref, wgt_ref, wut_ref, wo_ref, g_ref,
             dx1_ref, da_ref, db_ref, dc_ref, dgt_ref, dup_ref, act_ref, dg_ref):
        @pl.when(pl.program_id(0) == 0)
        def _():
            dg_ref[...] = jnp.zeros_like(dg_ref)

        dx2v = dx2_ref[...]
        dact = _dot_nt(dx2v.astype(BF16), wd_ref[...])
        gf = gt_ref[...].astype(F32)
        uf = up_ref[...].astype(F32)
        sg = _sigmoid(gf)
        sl = gf * sg
        act_ref[...] = (sl * uf).astype(BF16)
        dgt = (dact * uf * (sg * (1.0 + gf * (1.0 - sg)))).astype(BF16)
        dup = (dact * sl).astype(BF16)
        dgt_ref[...] = dgt
        dup_ref[...] = dup
        dh2 = _dot(dgt, wgt_ref[...]) + _dot(dup, wut_ref[...])
        n, r = _rms(x1_ref[...])
        dxn, dgp = _rms_bwd(dh2, n, r, g_ref[...])
        dg_ref[...] += _colsum8(dgp)
        dx1 = dx2v + dxn
        dx1_ref[...] = dx1
        dy = _dot_nt(dx1.astype(BF16), wo_ref[...])
        da_ref[...] = dy[:, :D_SSM]
        db_ref[...] = dy[:, D_SSM:D_SSM + D_POOL]
        dc_ref[...] = dy[:, D_SSM + D_POOL:]

    return pl.pallas_call(
        _following(body, after), grid=(s // ts,),
        in_specs=_anywhere(after) + [_row(ts, D_MODEL), _row(ts, D_MODEL), _row(ts, D_FF), _row(ts, D_FF),
                  _const((D_FF, D_MODEL)), _const((D_FF, D_MODEL)), _const((D_FF, D_MODEL)),
                  _const((D_MODEL, D_MODEL)), _const((1, D_MODEL))],
        out_specs=[_row(ts, D_MODEL), _row(ts, D_SSM), _row(ts, D_POOL), _row(ts, D_SGU), _row(ts, D_FF),
                   _row(ts, D_FF), _row(ts, D_FF), _acc((SUBLANES, D_MODEL))],
        out_shape=[jax.ShapeDtypeStruct((s, D_MODEL), F32), jax.ShapeDtypeStruct((s, D_SSM), F32),
                   jax.ShapeDtypeStruct((s, D_POOL), F32), jax.ShapeDtypeStruct((s, D_SGU), F32),
                   jax.ShapeDtypeStruct((s, D_FF), BF16), jax.ShapeDtypeStruct((s, D_FF), BF16),
                   jax.ShapeDtypeStruct((s, D_FF), BF16), jax.ShapeDtypeStruct((SUBLANES, D_MODEL), F32)],
        name=f"blk_bwd_{tag}", compiler_params=_cp(dimension_semantics=("arbitrary",)),
    )(*after, dx2, x1, gt, up, p["w_down"], p["w_gate"], p["w_up"], p["w_out"], p["g_ffn"])


S5_OWN = (N_USLAB, LANES, LANES)
assert 2 * SSM_STATE == LANES


def _own_blocks(acc_ref, j):
    pairs = SLAB_STATES // LANES
    group = lax.broadcasted_iota(jnp.int32, (LANES, LANES), 0) // (LANES // (2 * pairs))
    lane = lax.broadcasted_iota(jnp.int32, (LANES, LANES), 1)
    parts = []
    for half in range(2):
        own = jnp.zeros((LANES, LANES), F32)
        for q in range(pairs):
            own = jnp.where(group // 2 == q, acc_ref[j, :, pl.ds(half * SLAB_STATES + q * LANES, LANES)], own)
        parts.append(jnp.where(group % 2 == 0, own, pltpu.roll(own, SSM_STATE, 1)))
    return jnp.where(lane < SSM_STATE, parts[0], pltpu.roll(parts[1], SSM_STATE, 1))


def _own_blocks_split(d):
    d = d.reshape(N_GROUPS, D_SSM // N_GROUPS, 2, SSM_STATE)
    return d[:, :, 0], d[:, :, 1]


def _s5_bwd(dout, u, y, h_re, h_im, p, tag, after=()):
    s = u.shape[0]
    nt = s // TS
    seg = SEG

    def rev(n):
        return pl.BlockSpec((TS, n), lambda i: (nt - 1 - i, 0))

    def body(do_ref, u_ref, y_ref, hr_ref, hi_ref, perm_ref, ar_ref, ai_ref, cb_ref, bb_ref, dsk_ref,
             wglu_ref, bglu_ref,
             du_ref, dct_own_ref, dbb_own_ref, dar_ref, dai_ref, dd_ref, dwglu_ref, dbglu_ref,
             gr, gi, hsr, hsi, er, ei, jr, ji, cr, ci, dct_ref, dbb_ref):
        @pl.when(pl.program_id(0) == 0)
        def _():
            for ref in (cr, ci, dct_ref, dbb_ref, dar_ref, dai_ref, dd_ref, dwglu_ref, dbglu_ref):
                ref[...] = jnp.zeros_like(ref)

        perm = perm_ref[...]
        uv = _to_scan_order(perm, u_ref[...])
        yv = y_ref[...]
        dov = _to_scan_order(perm, do_ref[...])
        g, gelu_dy = _gelu_and_grad(yv)
        gb = g.astype(BF16)
        sg = _sigmoid(_dot(gb, wglu_ref[...]) + bglu_ref[...])
        dpre = dov * g * sg * (1.0 - sg)
        dpb = dpre.astype(BF16)
        dwglu_ref[...] += _dot_tn(gb, dpb)
        dbglu_ref[...] += _colsum8(dpre)
        dy = (dov * sg + _dot_nt(dpb, wglu_ref[...])) * gelu_dy
        dd_ref[...] += _colsum8(dy * uv)
        dyb = dy.astype(BF16)
        hsr[...] = hr_ref[...].astype(F32)
        hsi[...] = hi_ref[...].astype(F32)
        ub = uv.astype(BF16)
        dus = []

        def state_cotangents(j):
            dct_ref[j] += _dot_tn(_lanes(dyb, j), _state_cat(hr_ref, hi_ref, j))
            _state_split(gr, gi, j, _dot(_lanes(dyb, j), cb_ref[j]))

        def input_cotangents(j):
            gb_j = _state_cat(gr, gi, j)
            dbb_ref[j] += _dot_tn(_lanes(ub, j), gb_j)
            dus.append(_dot(gb_j, bb_ref[j]))

        def scan(j):
            cols = _states(j)
            ar = ar_ref[:, cols]
            ai = -ai_ref[:, cols]
            g_r = g_i = jnp.zeros((SUBLANES, SLAB_STATES), F32)
            for k in range(seg - 1, -1, -1):
                rows = pl.ds(SUBLANES * k, SUBLANES)
                n_r, n_i = _cmul(ar, ai, g_r, g_i)
                g_r = n_r + gr[rows, cols]
                g_i = n_i + gi[rows, cols]
            er[:, cols] = g_r
            ei[:, cols] = g_i
            pr, pi = _cpow(ar[0:1, :], ai[0:1, :], seg)
            c_r = cr[:, cols]
            c_i = ci[:, cols]
            for q in range(SUBLANES - 1, -1, -1):
                jr[q:q + 1, cols] = c_r
                ji[q:q + 1, cols] = c_i
                n_r, n_i = _cmul(pr, pi, c_r, c_i)
                c_r = n_r + er[q:q + 1, cols]
                c_i = n_i + ei[q:q + 1, cols]
            cr[:, cols] = c_r
            ci[:, cols] = c_i
            g_r = jr[:, cols]
            g_i = ji[:, cols]
            a_r = a_i = jnp.zeros((SUBLANES, SLAB_STATES), F32)
            for k in range(seg - 1, -1, -1):
                rows = pl.ds(SUBLANES * k, SUBLANES)
                h_r = hsr[rows, cols]
                h_i = hsi[rows, cols]
                a_r = a_r + g_r * h_r + g_i * h_i
                a_i = a_i + g_i * h_r - g_r * h_i
                n_r, n_i = _cmul(ar, ai, g_r, g_i)
                g_r = n_r + gr[rows, cols]
                g_i = n_i + gi[rows, cols]
                gr[rows, cols] = g_r
                gi[rows, cols] = g_i
            dar_ref[:, cols] += a_r
            dai_ref[:, cols] += a_i

        for stage in (state_cotangents, scan, input_cotangents):
            for j in range(N_USLAB):
                stage(j)
        du = dy * dsk_ref[...] + jnp.concatenate(dus, axis=1)
        du_ref[...] = _dot_tn(perm, du.astype(BF16)).astype(BF16)

        @pl.when(pl.program_id(0) == nt - 1)
        def _():
            for j in range(N_USLAB):
                dct_own_ref[j] = _own_blocks(dct_ref, j)
                dbb_own_ref[j] = _own_blocks(dbb_ref, j)

    big = (TS, N_STATE)
    return pl.pallas_call(
        _following(body, after), grid=(nt,),
        in_specs=_anywhere(after) + [rev(D_SSM), rev(D_SSM), rev(D_SSM), rev(N_STATE), rev(N_STATE), _const((TS, TS)),
                  _const(STATE_TILE), _const(STATE_TILE), _const(S5_IN), _const(S5_OUT), _const((1, D_SSM)),
                  _const((D_SSM, D_SSM)), _const((1, D_SSM))],
        out_specs=[rev(D_SSM), _acc(S5_OWN), _acc(S5_OWN), _acc(STATE_TILE), _acc(STATE_TILE),
                   _acc((SUBLANES, D_SSM)), _acc((D_SSM, D_SSM)), _acc((SUBLANES, D_SSM))],
        out_shape=[jax.ShapeDtypeStruct((s, D_SSM), BF16), jax.ShapeDtypeStruct(S5_OWN, F32),
                   jax.ShapeDtypeStruct(S5_OWN, F32), jax.ShapeDtypeStruct(STATE_TILE, F32),
                   jax.ShapeDtypeStruct(STATE_TILE, F32), jax.ShapeDtypeStruct((SUBLANES, D_SSM), F32),
                   jax.ShapeDtypeStruct((D_SSM, D_SSM), F32), jax.ShapeDtypeStruct((SUBLANES, D_SSM), F32)],
        scratch_shapes=[pltpu.VMEM(big, F32), pltpu.VMEM(big, F32), pltpu.VMEM(big, F32), pltpu.VMEM(big, F32),
                        pltpu.VMEM(STATE_TILE, F32), pltpu.VMEM(STATE_TILE, F32), pltpu.VMEM(STATE_TILE, F32),
                        pltpu.VMEM(STATE_TILE, F32), pltpu.VMEM((1, N_STATE), F32), pltpu.VMEM((1, N_STATE), F32),
                        pltpu.VMEM(S5_IN, F32), pltpu.VMEM(S5_IN, F32)],
        name=f"s5_bwd_{tag}", compiler_params=_cp(dimension_semantics=("arbitrary",)),
    )(*after, dout, u, y, h_re, h_im, _scan_order(), p["a_re8"], p["a_im8"], p["cb3"], p["bb3"], p["d_skip"], p["w_glu"],
      p["b_glu"])


def _mix_bwd(dza, db, dc, pooled, zuv, x0, dx1, p, tag, after=()):
    s = x0.shape[0]
    nt = s // TS

    def rev(n):
        return pl.BlockSpec((TS, n), lambda i: (nt - 1 - i, 0))

    def body(da_ref, db_ref, dc_ref, po_ref, z_ref, x_ref, dx1_ref, wl_ref, wp_ref, wpt_ref, sc_ref, lng_ref, lnb_ref,
             wsp_ref, wspt_ref, bias_ref, win_ref, g_ref,
             dx0_ref, dz_ref, dwp_ref, dsc_ref, dws_ref, dbias_ref, dlng_ref, dlnb_ref, dg_ref, buf, *tmp):
        step = pl.program_id(0)
        i = nt - 1 - step

        @pl.when(step == 0)
        def _():
            for ref in (dwp_ref, dsc_ref, dws_ref, dbias_ref, dlng_ref, dlnb_ref, dg_ref):
                ref[...] = jnp.zeros_like(ref)
            for ref in (buf, *tmp):
                ref[pl.ds(TS, POOL_ROWS - TS), :] = jnp.zeros((POOL_ROWS - TS, D_POOL), F32)

        wl = wl_ref[...]
        sc = sc_ref[...]
        dob = db_ref[...]
        pooled_b = po_ref[...]
        dsc_ref[...] += _colsum8(dob * _dot(pooled_b, wp_ref[...]))
        dmixb = (dob * sc).astype(BF16)
        dwp_ref[...] += _dot_tn(pooled_b, dmixb)
        dpool = _dot(dmixb, wpt_ref[...])
        dq = dpool / _pool_count(i, TS, wl)
        buf[pl.ds(0, TS), :] = dq
        dzb = _window_sum(buf, tmp, 0, wl, 1) - dpool
        buf[pl.ds(TS, MAX_WINDOW), :] = dq[:MAX_WINDOW, :]

        lo, hi = _half_masks()
        lng = lng_ref[...]
        u, vn, rs, vl, gelu_du, gelu_dv = _sgu_front(z_ref[...], lng, lnb_ref[...], grads=True)
        mixed = _sgu_mix(vl, wsp_ref, lo, hi) + jnp.tile(bias_ref[...], (TS // CHUNK, 1))
        doc = dc_ref[...]
        dzu = doc * mixed * gelu_du
        dmix = doc * u
        dbias = dbias_ref[...]
        for c in range(TS // CHUNK):
            dmc = dmix[CHUNK * c:CHUNK * (c + 1), :]
            dbias = dbias + dmc
            vlc = vl[CHUNK * c:CHUNK * (c + 1), :].astype(BF16)
            for q in range(SGU_HEADS // 2):
                dmq = _lanes(dmc, q)
                vq = _lanes(vlc, q)
                dws_ref[2 * q] += _dot_nt((dmq * lo).astype(BF16), vq)
                dws_ref[2 * q + 1] += _dot_nt((dmq * hi).astype(BF16), vq)
        dbias_ref[...] = dbias
        dvl = _sgu_mix(dmix, wspt_ref, lo, hi)
        dlng_ref[...] += _colsum8(dvl * vn)
        dlnb_ref[...] += _colsum8(dvl)
        dvn = dvl * lng
        dv = rs * (dvn - jnp.mean(dvn, axis=-1, keepdims=True) - vn * jnp.mean(dvn * vn, axis=-1, keepdims=True))

        dz = jnp.concatenate([da_ref[...], dzb.astype(BF16), dzu.astype(BF16), (dv * gelu_dv).astype(BF16)], axis=1)
        dz_ref[...] = dz
        n, r = _rms(x_ref[...])
        dxn, dgp = _rms_bwd(_dot(dz, win_ref[...]), n, r, g_ref[...])
        dg_ref[...] += _colsum8(dgp)
        dx0_ref[...] = dx1_ref[...] + dxn

    pair = (SGU_HEADS // 2, CHUNK, 2 * CHUNK)
    return pl.pallas_call(
        _following(body, after), grid=(nt,),
        in_specs=_anywhere(after) + [rev(D_SSM), rev(D_POOL), rev(D_SGU), rev(D_POOL), rev(2 * D_SGU), rev(D_MODEL), rev(D_MODEL),
                  _const((1, D_POOL)), _const((D_POOL, D_POOL)), _const((D_POOL, D_POOL)), _const((1, D_POOL)),
                  _const((1, D_SGU)), _const((1, D_SGU)), _const(pair), _const(pair), _const((CHUNK, D_SGU)),
                  _const((D_IN, D_MODEL)), _const((1, D_MODEL))],
        out_specs=[rev(D_MODEL), rev(D_IN), _acc((D_POOL, D_POOL)), _acc((SUBLANES, D_POOL)),
                   _acc((SGU_HEADS, CHUNK, CHUNK)), _acc((CHUNK, D_SGU)), _acc((SUBLANES, D_SGU)),
                   _acc((SUBLANES, D_SGU)), _acc((SUBLANES, D_MODEL))],
        out_shape=[jax.ShapeDtypeStruct((s, D_MODEL), F32), jax.ShapeDtypeStruct((s, D_IN), BF16),
                   jax.ShapeDtypeStruct((D_POOL, D_POOL), F32), jax.ShapeDtypeStruct((SUBLANES, D_POOL), F32),
                   jax.ShapeDtypeStruct((SGU_HEADS, CHUNK, CHUNK), F32), jax.ShapeDtypeStruct((CHUNK, D_SGU), F32),
                   jax.ShapeDtypeStruct((SUBLANES, D_SGU), F32), jax.ShapeDtypeStruct((SUBLANES, D_SGU), F32),
                   jax.ShapeDtypeStruct((SUBLANES, D_MODEL), F32)],
        scratch_shapes=[pltpu.VMEM((POOL_ROWS, D_POOL), F32)] * 4,
        name=f"mix_bwd_{tag}", compiler_params=_cp(dimension_semantics=("arbitrary",)),
    )(*after, dza, db, dc, pooled, zuv, x0, dx1, _pool_consts(), p["w_pool_bd"], p["w_pool_bd_t"], p["pool_scale"],
      p["sgu_ln_g"], p["sgu_ln_b"], p["ws_pair"], p["ws_pair_t"], p["bias_sp"], p["w_in"], p["g_mix"])


def _atb(a, b, tag, after=()):
    s, ka = a.shape
    kb = b.shape[1]
    ts = ATB_ROWS
    tn = min(kb, ATB_COLS)
    ns = s // ts

    def body(a_ref, b_ref, *rest):
        o_ref = rest[-1]

        @pl.when(pl.program_id(1) == 0)
        def _():
            o_ref[...] = jnp.zeros_like(o_ref)

        o_ref[...] += _dot_tn(a_ref[...].astype(BF16), b_ref[...].astype(BF16))

    return pl.pallas_call(
        body, grid=(kb // tn, ns),
        in_specs=[pl.BlockSpec((ts, ka), lambda j, i: (i, 0)), pl.BlockSpec((ts, tn), lambda j, i: (i, j))]
        + [pl.BlockSpec(memory_space=pl.ANY)] * len(after),
        out_specs=pl.BlockSpec((ka, tn), lambda j, i: (0, j)),
        out_shape=jax.ShapeDtypeStruct((ka, kb), F32),
        name=f"atb_{tag}", compiler_params=_cp(dimension_semantics=("arbitrary", "arbitrary")),
    )(a, b, *after)


def _s5_discretise(a_re, a_im, log_dt, b_re, b_im):
    dt = jnp.exp(log_dt)[:, None]
    mag = jnp.exp(a_re * dt)
    ar = mag * jnp.cos(a_im * dt)
    ai = mag * jnp.sin(a_im * dt)
    den = a_re * a_re + a_im * a_im
    f_re = ((ar - 1.0) * a_re + ai * a_im) / den
    f_im = (ai * a_re - (ar - 1.0) * a_im) / den
    bb_re = f_re[..., None] * b_re - f_im[..., None] * b_im
    bb_im = f_re[..., None] * b_im + f_im[..., None] * b_re
    return ar, ai, bb_re, bb_im


def _block_diag(blocks):
    g, r, c = blocks.shape
    eye = jnp.eye(g, dtype=blocks.dtype)
    return (blocks[:, :, None, :] * eye[:, None, :, None]).reshape(g * r, g * c)


def _block_diag_extract(m, g):
    r = m.shape[0] // g
    c = m.shape[1] // g
    eye = jnp.eye(g, dtype=m.dtype)
    return jnp.sum(m.reshape(g, r, g, c) * eye[:, None, :, None], axis=2)


GROUPS_PER_SLAB = N_GROUPS // N_USLAB


def _slab_diag(blocks):
    k = GROUPS_PER_SLAB
    _, r, c = blocks.shape
    eye = jnp.eye(k, dtype=blocks.dtype)
    spread = blocks.reshape(N_USLAB, k, r, 1, c) * eye[None, :, None, :, None]
    return spread.reshape(N_USLAB, k * r, k * c)


def _state_slabs(v):
    return jnp.broadcast_to(v.reshape(1, N_STATE), STATE_TILE)


def _tril():
    return jnp.tril(jnp.ones((CHUNK, CHUNK), dtype=bool))


def _layer_params(w, l):
    row = lambda v: v.reshape(1, -1)
    t = lambda m: jnp.swapaxes(m, -1, -2)
    ar, ai, bb_re, bb_im = _s5_discretise(w["A_re"][l], w["A_im"][l], w["log_dt"][l], w["B_re"][l], w["B_im"][l])
    bbt3 = jnp.concatenate([_slab_diag(t(bb_re)), _slab_diag(t(bb_im))], axis=2).astype(BF16)
    ct3 = jnp.concatenate([_slab_diag(t(w["C_re"][l])), -_slab_diag(t(w["C_im"][l]))], axis=1).astype(BF16)
    ws = jnp.where(_tril()[None], w["w_spatial"][l], 0.0)
    pair = lambda m: jnp.stack([jnp.concatenate([m[2 * q], m[2 * q + 1]], axis=1)
                                for q in range(SGU_HEADS // 2)]).astype(BF16)
    wp = _block_diag(w["w_pool"][l]).astype(BF16)
    p = dict(
        g_mix=row(w["g_mix"][l]), g_ffn=row(w["g_ffn"][l]), d_skip=row(w["D_skip"][l]), b_glu=row(w["b_glu"][l]),
        pool_scale=row(w["pool_scale"][l]), sgu_ln_g=row(w["sgu_ln_g"][l]), sgu_ln_b=row(w["sgu_ln_b"][l]),
        a_re8=_state_slabs(ar), a_im8=_state_slabs(ai),
        bbt3=bbt3, bb3=t(bbt3), ct3=ct3, cb3=t(ct3),
        w_pool_bd=wp, w_pool_bd_t=t(wp), ws_pair=pair(ws), ws_pair_t=pair(t(ws)),
        bias_sp=jnp.repeat(t(w["b_spatial"][l]), SGU_HEAD_DIM, axis=1),
    )
    return p


MIX_WEIGHTS = ("w_in", "w_glu")
FFN_WEIGHTS = ("w_out", "w_gate", "w_up", "w_down")


def _with_big(p, mats):
    p.update(mats)


def _rows_sum(v):
    return jnp.sum(v, axis=0)


ATB_COLS = 1024
ATB_ROWS = 1024


def _after(v, token):
    return v + token[0, 0]


def _layer_bwd(dx2, sv, p, w, l, tag, hooks, after):
    t = lambda m: jnp.swapaxes(m, -1, -2)
    dx1, da, db, dc, dgt, dup, act, dg_ffn = _blk_bwd(dx2, sv["x1"], sv["gt"], sv["up"], p, tag, after)
    after = hooks["tick"]([dx1])
    after = after + hooks["on_grads"](l, "ffn", {
        "w_down": _atb(act, dx2, tag + "_wd", after), "w_gate": _atb(dgt, sv["h2"], tag + "_wg", after),
        "w_up": _atb(dup, sv["h2"], tag + "_wu", after), "w_out": _atb(sv["ycat"], dx1, tag + "_wo", after)})
    g = {}
    g["g_ffn"] = _rows_sum(dg_ffn)
    dza, dc3, dbbt3, dar8, dai8, dd8, dwglu, dbglu8 = _s5_bwd(
        da, sv["za"], sv["y"], sv["h_re"], sv["h_im"], p, tag, after)
    after = hooks["tick"]([dza])
    dx0, dz, dwp, dsc8, dws, dbias, dlng8, dlnb8, dg_mix = _mix_bwd(
        dza, db, dc, sv["pooled"], sv["zuv"], sv["x0"], dx1, p, tag, after)
    g["g_mix"] = _rows_sum(dg_mix)
    g["b_glu"] = _rows_sum(dbglu8)
    g["D_skip"] = _rows_sum(dd8)
    dc_re, dc_im = _own_blocks_split(dc3)
    g["C_re"] = dc_re
    g["C_im"] = -dc_im
    dar = jnp.sum(dar8, axis=0).reshape(N_GROUPS, SSM_STATE)
    dai = jnp.sum(dai8, axis=0).reshape(N_GROUPS, SSM_STATE)
    dbb_re, dbb_im = [t(d) for d in _own_blocks_split(dbbt3)]
    _, disc_vjp = jax.vjp(_s5_discretise, w["A_re"][l], w["A_im"][l], w["log_dt"][l], w["B_re"][l], w["B_im"][l])
    g["A_re"], g["A_im"], g["log_dt"], g["B_re"], g["B_im"] = disc_vjp((dar, dai, dbb_re, dbb_im))
    g["w_pool"] = _block_diag_extract(dwp, len(POOL_WINDOWS))
    g["pool_scale"] = _rows_sum(dsc8)
    g["sgu_ln_g"] = _rows_sum(dlng8)
    g["sgu_ln_b"] = _rows_sum(dlnb8)
    g["w_spatial"] = jnp.where(_tril()[None], dws, 0.0)
    g["b_spatial"] = t(jnp.sum(dbias.reshape(CHUNK, SGU_HEADS, SGU_HEAD_DIM), axis=-1))
    after = hooks["on_small"](l, g)
    after = hooks["on_grads"](l, "mix", {"w_in": _atb(dz, sv["h1"], tag + "_wi", after), "w_glu": dwglu})
    return dx0, after


def _local_step(x, target, w, hooks):
    params = [_layer_params(w, l) for l in range(DEPTH)]
    saved = []
    h = x
    for l in range(DEPTH):
        p, tag = params[l], f"l{l}"
        _with_big(p, hooks["get_big"](l, "mix", [h]))
        za, zuv, h1, ob, pooled, oc = _mix_fwd(h, p, tag)
        oa, y, h_re, h_im = _s5_fwd(za, p, tag)
        _with_big(p, hooks["get_big"](l, "ffn", [oa, ob, oc]))
        head = (target, w["g_final"].reshape(1, -1)) if l == DEPTH - 1 else None
        x1, x2, h2, gt, up, ycat, *loss_parts = _blk_fwd(h, oa, ob, oc, p, tag, head)
        saved.append(dict(x0=h, za=za, zuv=zuv, h1=h1, ycat=ycat, y=y, h_re=h_re, h_im=h_im, pooled=pooled, x1=x1,
                          h2=h2, gt=gt, up=up))
        h = x2
    dx = h
    loss8, dgf8 = loss_parts
    grads = [None] * DEPTH

    def on_small(l, g_l):
        grads[l] = g_l
        if l > 0:
            return []
        g = {n: jnp.stack([grads[k][n] for k in range(DEPTH)]) for n in SMALL if n != "g_final"}
        g["g_final"] = _rows_sum(dgf8)
        return hooks["on_small"](g, loss8[0, 0])

    after = []
    for l in reversed(range(DEPTH)):
        dx, after = _layer_bwd(dx, saved[l], params[l], w, l, f"l{l}", dict(hooks, on_small=on_small), after)
    return dx


_ANY = pl.BlockSpec(memory_space=pl.ANY)
_MESH = pl.DeviceIdType.MESH


def _place():
    return lax.axis_index("x"), lax.axis_index("y"), lax.axis_index("c")


def _other_chips(x, y):
    return [(1 - x, y), (x, 1 - y), (1 - x, 1 - y)]


def _dma_sems(n):
    return pltpu.SemaphoreType.DMA((n,))


def _remote(src, dst, send_sems, recv_sems, k, to):
    return pltpu.make_async_remote_copy(src_ref=src, dst_ref=dst, send_sem=send_sems.at[k], recv_sem=recv_sems.at[k],
                                        device_id=to, device_id_type=_MESH)


_HBM = pl.BlockSpec(memory_space=pltpu.HBM)
_SEM = pl.BlockSpec(memory_space=pltpu.SEMAPHORE)
_EFFECT = pltpu.SideEffectType.DATAFLOW_SIDE_EFFECTING
N_REL = N_CHIPS - 1


def _gather_plan(x, y, c, srcs, lands):
    plan = []
    for l in lands:
        r = l.shape[0] // N_CHIPS
        rows = l.at[pl.ds((2 * x + y) * r, r)]
        plan += [(rows, rows, (cx, cy, c)) for cx, cy in _other_chips(x, y)]
    return plan


def _half_rows(land, chip, c):
    h = land.shape[0] // (2 * N_CHIPS)
    return land.at[pl.ds((2 * chip + c) * h, h)]


def _gather_half_plan(x, y, c, srcs, lands):
    return [(_half_rows(l, 2 * x + y, c), _half_rows(l, 2 * x + y, c), (cx, cy, c))
            for l in lands for cx, cy in _other_chips(x, y)]


def _forward_halves(lands, tag):
    nw = len(lands)

    def body(*refs):
        ins = refs[:nw]
        send_sems, recv_sems = refs[2 * nw:]
        x, y, c = _place()
        chips = [2 * cx + cy for cx, cy in _other_chips(x, y)]
        sends = [_remote(_half_rows(ins[i], k, c), _half_rows(ins[i], k, c), send_sems, recv_sems, N_REL * i + j,
                         (x, y, 1 - c)) for i in range(nw) for j, k in enumerate(chips)]
        for cp in sends:
            cp.start()
        for i in range(nw):
            for j, k in enumerate(chips):
                sends[N_REL * i + j].wait_send()
                _remote(_half_rows(ins[i], k, c), _half_rows(ins[i], k, 1 - c), send_sems, recv_sems, N_REL * i + j,
                        (x, y, 1 - c)).wait_recv()

    return pl.pallas_call(
        body, out_shape=[jax.ShapeDtypeStruct(a.shape, a.dtype) for a in lands], in_specs=[_ANY] * nw,
        out_specs=[_ANY] * nw, input_output_aliases={i: i for i in range(nw)},
        scratch_shapes=[_dma_sems(N_REL * nw), _dma_sems(N_REL * nw)], name=f"forward_halves_{tag}",
    )(*lands)


def _sibling_plan(x, y, c, srcs, lands):
    return [(s.at[:, 1 - c], l, (x, y, 1 - c)) for s, l in zip(srcs, lands)]


def _slab_plan(x, y, c, srcs, lands):
    return [(s.at[2 * cx + cy], l.at[j], (cx, cy, c))
            for s, l in zip(srcs, lands) for j, (cx, cy) in enumerate(_other_chips(x, y))]


def _plan_copies(plan, srcs, lands, send_sems, recv_sems):
    x, y, c = _place()
    return [_remote(s, d, send_sems, recv_sems, k, to) for k, (s, d, to) in enumerate(plan(x, y, c, srcs, lands))]


def _hbm(a):
    return pltpu.with_memory_space_constraint(a, pltpu.HBM)


def _everyone_plan(x, y, c, srcs, lands):
    me = 4 * x + 2 * y + c
    peers = [(x, y, 1 - c)] + [(cx, cy, cc) for cx, cy in _other_chips(x, y) for cc in (c, 1 - c)]
    return [(s, l.at[me], peer) for s, l in zip(srcs, lands) for peer in peers]


def _copies_start(name, plan, srcs, lands, ncopies):
    ns, n = len(srcs), len(srcs) + len(lands)

    def body(*refs):
        for cp in _plan_copies(plan, refs[:ns], refs[ns:n], refs[n], refs[n + 1]):
            cp.start()
        refs[-1][...] = jnp.zeros_like(refs[-1])

    ref_out = [pltpu.HBM(a.shape, a.dtype) for a in (*srcs, *lands)]
    out = pl.pallas_call(
        body, name=name, in_specs=[_HBM] * n,
        out_shape=(_dma_sems(ncopies), _dma_sems(ncopies), *ref_out, jax.ShapeDtypeStruct((SUBLANES, LANES), F32)),
        out_specs=(_SEM, _SEM, *[_HBM] * n, pl.BlockSpec(memory_space=pltpu.VMEM)),
        input_output_aliases={i: 2 + i for i in range(n)},
        compiler_params=pltpu.CompilerParams(has_side_effects=_EFFECT),
    )(*[_hbm(a) for a in (*srcs, *lands)])
    return dict(name=name, plan=plan, sems=out[:2], srcs=out[2:2 + ns], lands=out[2 + ns:2 + n], token=out[-1])


def _copies_wait(started, after):
    ns = len(started["srcs"])
    n = ns + len(started["lands"])
    plan = started["plan"]

    def body(*refs):
        for cp in _plan_copies(plan, refs[:ns], refs[ns:n], refs[n], refs[n + 1]):
            cp.wait_send()
            cp.wait_recv()

    args = (*started["srcs"], *started["lands"])
    out = pl.pallas_call(
        body, name=started["name"] + "_wait", out_shape=[pltpu.HBM(a.shape, a.dtype) for a in args],
        in_specs=[_HBM] * n + [_SEM, _SEM] + [_ANY] * len(after), out_specs=[_HBM] * n,
        input_output_aliases={i: i for i in range(n)},
        compiler_params=pltpu.CompilerParams(has_side_effects=_EFFECT),
    )(*args, *started["sems"], *after)
    return out[:ns], out[ns:]


def _place_shards(ws, layer, sel, after, tag):
    nw = len(ws)

    def body(sel_ref, *refs):
        for i in range(nw):
            refs[nw + len(after) + i][...] = refs[i][...].astype(BF16)

    return pl.pallas_call(
        body, grid_spec=pltpu.PrefetchScalarGridSpec(
            num_scalar_prefetch=1, grid=(1,),
            in_specs=[pl.BlockSpec((None,) + a.shape[1:], lambda i, s: (layer, 0, 0)) for a in ws] + [_ANY] * len(after),
            out_specs=[pl.BlockSpec(a.shape[1:], lambda i, s: (s[1], 0)) for a in ws]),
        out_shape=[jax.ShapeDtypeStruct((N_CHIPS * a.shape[1], a.shape[2]), BF16) for a in ws],
        name=f"place_shards_{tag}", compiler_params=_cp(dimension_semantics=("arbitrary",)),
    )(sel, *ws, *after)


def _share_halves(fs, layer, tag):
    nw = len(fs)

    def body(*refs):
        ins = refs[:nw]
        send_sems, recv_sems = refs[2 * nw:]
        x, y, c = _place()

        def half(i, who):
            h = ins[i].shape[1] // 2
            return ins[i].at[layer, pl.ds(who * h, h)]

        sends = [_remote(half(i, c), half(i, c), send_sems, recv_sems, i, (x, y, 1 - c)) for i in range(nw)]
        for cp in sends:
            cp.start()
        for i in range(nw):
            sends[i].wait_send()
            _remote(half(i, c), half(i, 1 - c), send_sems, recv_sems, i, (x, y, 1 - c)).wait_recv()

    return pl.pallas_call(
        body, out_shape=[jax.ShapeDtypeStruct(f.shape, f.dtype) for f in fs], in_specs=[_ANY] * nw,
        out_specs=[_ANY] * nw, input_output_aliases={i: i for i in range(nw)},
        scratch_shapes=[_dma_sems(nw), _dma_sems(nw)], name=f"share_halves_{tag}",
    )(*fs)


def _add_halves(g4s, recvs, sel, tag):
    nw = len(g4s)

    def body(sel_ref, *refs):
        for i in range(nw):
            refs[2 * nw + i][...] = (refs[i][...] + refs[nw + i][...]).astype(BF16)

    mine = [pl.BlockSpec((None, None) + g.shape[2:], lambda k, s: (k, s[0], 0, 0)) for g in g4s]
    slab = [pl.BlockSpec((None,) + g.shape[2:], lambda k, s: (k, 0, 0)) for g in g4s]
    return pl.pallas_call(
        body, grid_spec=pltpu.PrefetchScalarGridSpec(num_scalar_prefetch=1, grid=(N_CHIPS,), in_specs=mine + slab,
                                                     out_specs=slab),
        out_shape=[jax.ShapeDtypeStruct(r.shape, BF16) for r in recvs], name=f"add_halves_{tag}",
        compiler_params=_cp(dimension_semantics=("arbitrary",)),
    )(sel, *g4s, *recvs)


def _add_chips(ps, slabs, fs, layer, sel, tag):
    nw = len(ps)
    old = [f for f in fs if f is not None]

    def body(sel_ref, *refs):
        outs = refs[2 * nw + len(old):]
        for i in range(nw):
            acc = refs[i][...].astype(F32)
            for j in range(N_REL):
                acc = acc + refs[nw + i][j].astype(F32)
            outs[i][...] = acc

    shapes = [(DEPTH, 2 * p.shape[1], p.shape[2]) for p in ps]
    in_specs = [pl.BlockSpec((None,) + p.shape[1:], lambda i, s: (s[1], 0, 0)) for p in ps]
    in_specs += [pl.BlockSpec(sl.shape, lambda i, s: (0, 0, 0)) for sl in slabs]
    in_specs += [_ANY] * len(old)
    first_old = 1 + 2 * nw
    aliases, k = {}, 0
    for i, f in enumerate(fs):
        if f is not None:
            aliases[first_old + k] = i
            k += 1
    return pl.pallas_call(
        body, grid_spec=pltpu.PrefetchScalarGridSpec(
            num_scalar_prefetch=1, grid=(1,), in_specs=in_specs,
            out_specs=[pl.BlockSpec((None,) + p.shape[1:], lambda i, s: (layer, s[0], 0)) for p in ps]),
        out_shape=[jax.ShapeDtypeStruct(sh, F32) for sh in shapes], input_output_aliases=aliases,
        name=f"add_chips_{tag}", compiler_params=_cp(dimension_semantics=("arbitrary",)),
    )(sel, *ps, *slabs, *old)


def _adamw_math(w, g, m, v):
    m = ADAM_B1 * m + (1.0 - ADAM_B1) * g
    v = ADAM_B2 * v + (1.0 - ADAM_B2) * (g * g)
    m_hat = m / (1.0 - ADAM_B1 ** ADAM_STEP)
    v_hat = v / (1.0 - ADAM_B2 ** ADAM_STEP)
    delta = -ADAM_LR * (m_hat / (jnp.sqrt(v_hat) + ADAM_EPS) + ADAM_WD * w)
    return delta, m, v


ADAM_ROWS = 256


def _row_tile(rows, most):
    return max(t for t in range(SUBLANES, most + 1, SUBLANES) if rows % t == 0)


def _adamw(w, g, m, v, tag, after=()):
    depth, rows, cols = w.shape
    tr = _row_tile(rows, ADAM_ROWS)

    def body(w_ref, g_ref, m_ref, v_ref, *rest):
        d_ref, nm_ref, nv_ref = rest[len(after):]
        d, nm, nv = _adamw_math(w_ref[...], g_ref[...], m_ref[...], v_ref[...])
        d_ref[...] = d
        nm_ref[...] = nm
        nv_ref[...] = nv

    spec = pl.BlockSpec((None, tr, cols), lambda l, i: (l, i, 0))
    return pl.pallas_call(
        body, grid=(depth, rows // tr), in_specs=[spec] * 4 + [_ANY] * len(after), out_specs=[spec] * 3,
        out_shape=[jax.ShapeDtypeStruct(w.shape, F32)] * 3, name=f"adamw_{tag}",
        compiler_params=_cp(dimension_semantics=("arbitrary", "arbitrary")),
    )(w, g, m, v, *after)


SMALL_TILE = 512
PRECISE = ("g_final",)
COARSE = [n for n in SMALL if n not in PRECISE]


def _small_reduce(gathered):
    n = len(gathered)

    def body(*refs):
        for ga_ref, g_ref in zip(refs[:n], refs[n:]):
            g = ga_ref[0].astype(F32)
            for k in range(1, N_DEV):
                g = g + ga_ref[k].astype(F32)
            g_ref[...] = g

    return pl.pallas_call(
        body, out_shape=[jax.ShapeDtypeStruct(b.shape[1:], F32) for b in gathered], name="small_reduce",
        compiler_params=_cp(),
    )(*gathered)


def _small_adamw(g, w, m, v):
    names = list(g)
    n = len(names)
    shapes = {k: g[k].shape if g[k].ndim > 1 else (1,) + g[k].shape for k in names}

    def body(*refs):
        ins, outs = refs[:4 * n], refs[4 * n:]
        for i in range(n):
            d, nm, nv = _adamw_math(ins[n + i][...], ins[i][...], ins[2 * n + i][...], ins[3 * n + i][...])
            outs[i][...] = d
            outs[n + i][...] = nm
            outs[2 * n + i][...] = nv

    out = pl.pallas_call(
        body, out_shape=[jax.ShapeDtypeStruct(shapes[k], F32) for _ in range(3) for k in names], name="adamw_small",
        compiler_params=_cp(),
    )(*[src[k].reshape(shapes[k]) for src in (g, w, m, v) for k in names])
    return [{k: out[j * n + i].reshape(g[k].shape) for i, k in enumerate(names)} for j in range(3)]


def _exchange_form(n, a):
    return jnp.swapaxes(a, 1, 2) if n in TRANSPOSED else a


PACK_ROWS = 16


def _rows_of(size):
    return -(-size // (LANES * PACK_ROWS)) * PACK_ROWS


SMALL_VIEW = {"B_re": (0, 1, 3, 2), "B_im": (0, 1, 3, 2), "b_spatial": (1, 0, 2)}
assert all(tuple(order[i] for i in order) == tuple(range(len(order))) for order in SMALL_VIEW.values())


def _view(n, a):
    return jnp.transpose(a, SMALL_VIEW[n]) if n in SMALL_VIEW else a


def _pack(vals, names, extra=None):
    parts = [_view(n, vals[n]).reshape(-1) for n in names] + ([] if extra is None else [extra.reshape(1)])
    tiles = [jnp.pad(a, (0, _rows_of(a.size) * LANES - a.size)).reshape(-1, LANES) for a in parts]
    rows = sum(t.shape[0] for t in tiles)
    if rows > SMALL_TILE:
        tiles.append(jnp.zeros((-rows % SMALL_TILE, LANES), tiles[0].dtype))
    return jnp.concatenate(tiles, axis=0)


def _unpack(buf, like, names):
    out, row = {}, 0
    for n in names:
        rows = _rows_of(like[n].size)
        shape = tuple(like[n].shape[i] for i in SMALL_VIEW.get(n, range(like[n].ndim)))
        out[n] = buf[row:row + rows].reshape(-1)[:like[n].size].reshape(shape)
        row += rows
    return out, buf[row:]


def kernel(x, g_mix, w_in, A_re, A_im, log_dt, B_re, B_im, C_re, C_im, D_skip, w_glu, b_glu, w_pool, pool_scale, sgu_ln_g, sgu_ln_b, w_spatial, b_spatial, w_out, g_ffn, w_gate, w_up, w_down, g_final, loss_target, m_g_mix, m_w_in, m_A_re, m_A_im, m_log_dt, m_B_re, m_B_im, m_C_re, m_C_im, m_D_skip, m_w_glu, m_b_glu, m_w_pool, m_pool_scale, m_sgu_ln_g, m_sgu_ln_b, m_w_spatial, m_b_spatial, m_w_out, m_g_ffn, m_w_gate, m_w_up, m_w_down, m_g_final, v_g_mix, v_w_in, v_A_re, v_A_im, v_log_dt, v_B_re, v_B_im, v_C_re, v_C_im, v_D_skip, v_w_glu, v_b_glu, v_w_pool, v_pool_scale, v_sgu_ln_g, v_sgu_ln_b, v_w_spatial, v_b_spatial, v_w_out, v_g_ffn, v_w_gate, v_w_up, v_w_down, v_g_final):
    loc = locals()
    w = {n: loc[n] for n in WEIGHTS}
    m = {n: loc["m_" + n] for n in WEIGHTS}
    v = {n: loc["v_" + n] for n in WEIGHTS}
    sel = jnp.stack([lax.axis_index("c"), 2 * lax.axis_index("x") + lax.axis_index("y")]).astype(jnp.int32)

    chip = sel[1]

    halves = [(l, half) for l in range(DEPTH) for half in ("mix", "ffn")]
    two_level = {(0, "ffn")}
    names = {"mix": MIX_WEIGHTS, "ffn": FFN_WEIGHTS}
    started = {}
    wx = {n: _exchange_form(n, w[n]) for n in BIG}
    chain = []
    for l, half in halves:
        lands = _place_shards([wx[n] for n in names[half]], l, sel, chain, f"l{l}_{half}")
        plan = _gather_half_plan if (l, half) in two_level else _gather_plan
        started[l, half] = _copies_start(f"weights_l{l}_{half}", plan, [], lands, N_REL * len(lands))
        chain = [started[l, half]["token"]]
    w = dict(w, g_mix=_after(w["g_mix"], started[halves[-1]]["token"]))

    def get_big(l, half, after):
        lands = _copies_wait(started[l, half], after)[1]
        if (l, half) in two_level:
            lands = _forward_halves(lands, f"l{l}_{half}")
        return dict(zip(names[half], lands))

    result = {n: None for n in BIG}
    stage = {"swap": None, "slabs": None}

    def advance(after):
        if stage["slabs"] is not None:
            ex, ns, l, tag = stage["slabs"]
            part, slabs = _copies_wait(ex, after)
            bufs = _add_chips(part, slabs, [result[n] for n in ns], l, sel, tag)
            for n, f in zip(ns, _share_halves(bufs, l, tag)):
                result[n] = f
            stage["slabs"] = None
        if stage["swap"] is None:
            return []
        sw, ns, l, tag = stage["swap"]
        part = _add_halves(*_copies_wait(sw, after), sel, tag)
        slabs = [lax.empty((N_REL,) + p.shape[1:], BF16) for p in part]
        ex = _copies_start(f"grads_{tag}", _slab_plan, part, slabs, N_REL * len(part))
        stage["slabs"], stage["swap"] = (ex, ns, l, tag), None
        return [ex["token"]]

    def on_grads(l, half, grads):
        ns = list(grads)
        tag = f"l{l}_{half}"
        tokens = advance([grads[ns[0]]])
        g4s = [grads[n].reshape(N_CHIPS, 2, grads[n].shape[0] // (2 * N_CHIPS), grads[n].shape[1]) for n in ns]
        recvs = [lax.empty((N_CHIPS,) + g4.shape[2:], F32) for g4 in g4s]
        sw = _copies_start(f"swap_{tag}", _sibling_plan, g4s, recvs, len(g4s))
        stage["swap"] = (sw, ns, l, tag)
        return tokens + [sw["token"]]

    small = {}

    def on_small(g, loss_local):
        me = 2 * chip + sel[0]
        blocks = [_pack(g, COARSE).astype(BF16), _pack(g, PRECISE, loss_local)]
        lands = [lax.dynamic_update_slice(lax.empty((N_DEV,) + b.shape, b.dtype), b[None], (me, 0, 0)) for b in blocks]
        small.update(_copies_start("small_grads", _everyone_plan, blocks, lands, (N_DEV - 1) * len(blocks)))
        return [small["token"]]

    dx = _local_step(x[0], loss_target[0], w, dict(get_big=get_big, on_grads=on_grads, tick=advance, on_small=on_small))
    grads, deltas, new_m, new_v = {}, {}, {}, {}

    def update_big(ns, after):
        for n in ns:
            outs = _adamw(wx[n], result[n], _exchange_form(n, m[n]), _exchange_form(n, v[n]), n, after)
            grads[n], deltas[n], new_m[n], new_v[n] = [_exchange_form(n, a) for a in (result[n], *outs)]
            after = [outs[-1]]
        return after

    last_swap = stage["swap"][0]["token"]
    last_sent = advance(update_big(FFN_WEIGHTS[:1], [last_swap]))
    advance(update_big(FFN_WEIGHTS[1:], last_sent))
    _, gathered = _copies_wait(small, update_big(MIX_WEIGHTS, []))
    coarse, precise = _small_reduce(gathered)
    small_g, _ = _unpack(coarse, w, COARSE)
    precise_g, rest = _unpack(precise, w, PRECISE)
    small_g.update(precise_g)
    loss = rest[0, 0]
    views = [{n: _view(n, src[n]) for n in small_g} for src in (w, m, v)]
    for store, vals in zip((grads, deltas, new_m, new_v), (small_g, *_small_adamw(small_g, *views))):
        store.update({n: _view(n, a) for n, a in vals.items()})
    return (loss, dx[None], *[grads[n] for n in WEIGHTS], *[deltas[n] for n in WEIGHTS],
            *[new_m[n] for n in WEIGHTS], *[new_v[n] for n in WEIGHTS])
```

```python
import math

import jax
import jax.numpy as jnp
from jax import lax
from jax.experimental import pallas as pl
from jax.experimental.pallas import tpu as pltpu

F32 = jnp.float32
BF16 = jnp.bfloat16

D_MODEL = 1024
DEPTH = 2
D_SSM = 384
SSM_GROUP = 16
N_GROUPS = 24
SSM_STATE = 64
N_STATE = N_GROUPS * SSM_STATE
POOL_WINDOWS = (2, 4, 8, 16)
POOL_GROUP = 64
D_POOL = 256
MAX_WINDOW = 16
SGU_HEADS = 6
SGU_HEAD_DIM = 64
D_SGU = 384
CHUNK = 128
D_IN = D_SSM + D_POOL + 2 * D_SGU
D_FF = 2816
EPS = 1e-6

ADAM_LR = 0.001
ADAM_B1 = 0.9
ADAM_B2 = 0.999
ADAM_EPS = 1e-08
ADAM_WD = 0.01
ADAM_STEP = 10

LANES = 128
SUBLANES = 8
VMEM_LIMIT = 56 * 1024 * 1024

TS = 512
TS_FFN = 256

WEIGHTS = ['g_mix', 'w_in', 'A_re', 'A_im', 'log_dt', 'B_re', 'B_im', 'C_re', 'C_im', 'D_skip', 'w_glu', 'b_glu',
           'w_pool', 'pool_scale', 'sgu_ln_g', 'sgu_ln_b', 'w_spatial', 'b_spatial', 'w_out', 'g_ffn', 'w_gate',
           'w_up', 'w_down', 'g_final']
BIG = ['w_in', 'w_glu', 'w_out', 'w_gate', 'w_up', 'w_down']
SMALL = [n for n in WEIGHTS if n not in BIG]
TRANSPOSED = ("w_in", "w_gate", "w_up")
N_CHIPS = 4
N_DEV = 8


def _cp(**kw):
    return pltpu.CompilerParams(vmem_limit_bytes=VMEM_LIMIT, **kw)


def _row(ts, n):
    return pl.BlockSpec((ts, n), lambda i: (i, 0))


def _const(shape):
    nd = len(shape)
    return pl.BlockSpec(shape, lambda i: (0,) * nd, pipeline_mode=pl.Buffered(1))


def _acc(shape):
    nd = len(shape)
    return pl.BlockSpec(shape, lambda i: (0,) * nd)


def _following(body, after):
    k = len(after)
    return lambda *refs: body(*refs[k:])


def _anywhere(after):
    return [pl.BlockSpec(memory_space=pl.ANY)] * len(after)


def _dot(a, b):
    return jnp.dot(a, b, preferred_element_type=F32)


def _dot_tn(a, b):
    return lax.dot_general(a, b, (((0,), (0,)), ((), ())), preferred_element_type=F32)


def _dot_nt(a, b):
    return lax.dot_general(a, b, (((1,), (1,)), ((), ())), preferred_element_type=F32)


_G0 = math.sqrt(2.0 / math.pi)
_G1 = 0.044715


def _gelu(x):
    return 0.5 * x * (1.0 + jnp.tanh(_G0 * (x + _G1 * x * x * x)))


def _gelu_and_grad(x):
    t = jnp.tanh(_G0 * (x + _G1 * x * x * x))
    half = 0.5 * (1.0 + t)
    return x * half, half + 0.5 * x * (1.0 - t * t) * (_G0 * (1.0 + 3.0 * _G1 * x * x))


def _sigmoid(x):
    return 1.0 / (1.0 + jnp.exp(-x))


def _rms(x):
    r = lax.rsqrt(jnp.mean(x * x, axis=-1, keepdims=True) + EPS)
    return x * r, r


def _rms_bwd(dh, n, r, g):
    dn = dh * g
    return r * (dn - n * jnp.mean(dn * n, axis=-1, keepdims=True)), dh * n


def _colsum8(v):
    rows, n = v.shape
    return jnp.sum(v.reshape(rows // SUBLANES, SUBLANES, n), axis=0)


def _cmul(ar, ai, br, bi):
    return ar * br - ai * bi, ar * bi + ai * br


def _cpow(ar, ai, n):
    assert n & (n - 1) == 0
    while n > 1:
        ar, ai = _cmul(ar, ai, ar, ai)
        n //= 2
    return ar, ai


N_USLAB = D_SSM // LANES
SEG = TS // SUBLANES
SLAB_STATES = N_STATE // N_USLAB
S5_IN = (N_USLAB, LANES, 2 * SLAB_STATES)
S5_OUT = (N_USLAB, 2 * SLAB_STATES, LANES)
STATE_TILE = (SUBLANES, N_STATE)


def _scan_order():
    p = jnp.arange(TS)
    src = (p % SUBLANES) * SEG + p // SUBLANES
    return (src[:, None] == jnp.arange(TS)[None, :]).astype(BF16)


def _to_scan_order(perm, v):
    hi = v.astype(BF16)
    lo = (v - hi.astype(F32)).astype(BF16)
    return _dot(perm, hi) + _dot(perm, lo)


def _scan_rows(k):
    return pl.ds(pl.multiple_of(k * SUBLANES, SUBLANES), SUBLANES)


def _lanes(v, j):
    return v[:, LANES * j:LANES * (j + 1)]


def _states(j):
    return pl.ds(SLAB_STATES * j, SLAB_STATES)


def _state_split(re_ref, im_ref, j, v):
    re_ref[:, _states(j)] = v[:, :SLAB_STATES]
    im_ref[:, _states(j)] = v[:, SLAB_STATES:]


def _state_cat(re_ref, im_ref, j):
    return jnp.concatenate([re_ref[:, _states(j)], im_ref[:, _states(j)]], axis=1).astype(BF16)


def _s5_fwd(u, p, tag):
    s = u.shape[0]
    seg = SEG

    def body(u_ref, perm_ref, bbt_ref, ar_ref, ai_ref, ct_ref, dsk_ref, wglu_ref, bglu_ref,
             oa_ref, y_ref, hr_ref, hi_ref, sr, si, er, ei, ir, ii, cr, ci):
        @pl.when(pl.program_id(0) == 0)
        def _():
            cr[...] = jnp.zeros_like(cr)
            ci[...] = jnp.zeros_like(ci)

        perm = perm_ref[...]
        uv = _to_scan_order(perm, u_ref[...])
        ub = uv.astype(BF16)
        for j in range(N_USLAB):
            _state_split(sr, si, j, _dot(_lanes(ub, j), bbt_ref[j]))
        for j in range(N_USLAB):
            cols = _states(j)
            ar = ar_ref[:, cols]
            ai = ai_ref[:, cols]
            h_r = h_i = jnp.zeros((SUBLANES, SLAB_STATES), F32)
            for k in range(seg):
                rows = pl.ds(SUBLANES * k, SUBLANES)
                n_r, n_i = _cmul(ar, ai, h_r, h_i)
                h_r = n_r + sr[rows, cols]
                h_i = n_i + si[rows, cols]
            er[:, cols] = h_r
            ei[:, cols] = h_i
            pr, pi = _cpow(ar[0:1, :], ai[0:1, :], seg)
            c_r = cr[:, cols]
            c_i = ci[:, cols]
            for q in range(SUBLANES):
                ir[q:q + 1, cols] = c_r
                ii[q:q + 1, cols] = c_i
                n_r, n_i = _cmul(pr, pi, c_r, c_i)
                c_r = n_r + er[q:q + 1, cols]
                c_i = n_i + ei[q:q + 1, cols]
            cr[:, cols] = c_r
            ci[:, cols] = c_i
            h_r = ir[:, cols]
            h_i = ii[:, cols]
            for k in range(seg):
                rows = pl.ds(SUBLANES * k, SUBLANES)
                n_r, n_i = _cmul(ar, ai, h_r, h_i)
                h_r = n_r + sr[rows, cols]
                h_i = n_i + si[rows, cols]
                sr[rows, cols] = h_r
                si[rows, cols] = h_i
        hr_ref[...] = sr[...].astype(BF16)
        hi_ref[...] = si[...].astype(BF16)
        y = jnp.concatenate([_dot(_state_cat(hr_ref, hi_ref, j), ct_ref[j]) for j in range(N_USLAB)], axis=1)
        y = y + dsk_ref[...] * uv
        y_ref[...] = y
        g = _gelu(y)
        pre = _dot(g.astype(BF16), wglu_ref[...]) + bglu_ref[...]
        oa_ref[...] = _dot_tn(perm, (g * _sigmoid(pre)).astype(BF16)).astype(BF16)

    return pl.pallas_call(
        body, grid=(s // TS,),
        in_specs=[_row(TS, D_SSM), _const((TS, TS)), _const(S5_IN), _const(STATE_TILE), _const(STATE_TILE),
                  _const(S5_OUT), _const((1, D_SSM)), _const((D_SSM, D_SSM)), _const((1, D_SSM))],
        out_specs=[_row(TS, D_SSM), _row(TS, D_SSM), _row(TS, N_STATE), _row(TS, N_STATE)],
        out_shape=[jax.ShapeDtypeStruct((s, D_SSM), BF16), jax.ShapeDtypeStruct((s, D_SSM), F32),
                   jax.ShapeDtypeStruct((s, N_STATE), BF16), jax.ShapeDtypeStruct((s, N_STATE), BF16)],
        scratch_shapes=[pltpu.VMEM((TS, N_STATE), F32), pltpu.VMEM((TS, N_STATE), F32),
                        pltpu.VMEM(STATE_TILE, F32), pltpu.VMEM(STATE_TILE, F32), pltpu.VMEM(STATE_TILE, F32),
                        pltpu.VMEM(STATE_TILE, F32), pltpu.VMEM((1, N_STATE), F32), pltpu.VMEM((1, N_STATE), F32)],
        name=f"s5_fwd_{tag}", compiler_params=_cp(dimension_semantics=("arbitrary",)),
    )(u, _scan_order(), p["bbt3"], p["a_re8"], p["a_im8"], p["ct3"], p["d_skip"], p["w_glu"], p["b_glu"])


def _pool_consts():
    w = jnp.repeat(jnp.asarray(POOL_WINDOWS, F32), POOL_GROUP)[None, :]
    return w


POOL_PAD = SUBLANES
POOL_ROWS = TS + MAX_WINDOW + POOL_PAD


def _window_sum(buf, tmp, first, wl, step):
    assert POOL_WINDOWS == (2, 4, 8, 16)
    n = TS + MAX_WINDOW
    lo = first - MAX_WINDOW if step < 0 else first
    src = buf
    for k, dst in zip((1, 2, 4), tmp):
        dst[pl.ds(lo, n), :] = src[pl.ds(lo, n), :] + src[pl.ds(lo + step * k, n), :]
        src = dst
    s2, s4, s8 = (t[pl.ds(first, TS), :] for t in tmp)
    s16 = s8 + tmp[2][pl.ds(first + step * 8, TS), :]
    return jnp.where(wl == 2, s2, jnp.where(wl == 4, s4, jnp.where(wl == 8, s8, s16)))


def _pool_count(i, rows, wl):
    t = (i * TS + 1).astype(F32) + lax.broadcasted_iota(jnp.int32, (rows, 1), 0).astype(F32)
    return jnp.minimum(t, wl)


def _sgu_mix(vl, wpair_ref, lo, hi):
    rows = vl.shape[0]
    chunks = []
    for c in range(rows // CHUNK):
        vc = vl[CHUNK * c:CHUNK * (c + 1), :]
        parts = []
        for q in range(SGU_HEADS // 2):
            vq = vc[:, LANES * q:LANES * (q + 1)]
            rhs = jnp.concatenate([vq * lo, vq * hi], axis=0).astype(BF16)
            parts.append(_dot(wpair_ref[q], rhs))
        chunks.append(jnp.concatenate(parts, axis=1))
    return jnp.concatenate(chunks, axis=0)


def _sgu_front(zuv, lng, lnb, grads=False):
    gelu = _gelu_and_grad if grads else lambda z: (_gelu(z), None)
    u, du = gelu(zuv[:, :D_SGU])
    v, dv = gelu(zuv[:, D_SGU:])
    mu = jnp.mean(v, axis=-1, keepdims=True)
    vc = v - mu
    rs = lax.rsqrt(jnp.mean(vc * vc, axis=-1, keepdims=True) + EPS)
    vn = vc * rs
    return u, vn, rs, vn * lng + lnb, du, dv


def _half_masks():
    lane = lax.broadcasted_iota(jnp.int32, (1, LANES), 1)
    lo = (lane < SGU_HEAD_DIM).astype(F32)
    return lo, 1.0 - lo


def _mix_fwd(x, p, tag):
    s = x.shape[0]

    def body(x_ref, g_ref, w_ref, wl_ref, wp_ref, sc_ref, lng_ref, lnb_ref, wsp_ref, bias_ref,
             za_ref, zuv_ref, h_ref, ob_ref, pooled_ref, oc_ref, buf, *tmp):
        i = pl.program_id(0)
        tile0 = POOL_PAD + MAX_WINDOW

        @pl.when(i == 0)
        def _():
            for ref in (buf, *tmp):
                ref[pl.ds(0, tile0), :] = jnp.zeros((tile0, D_POOL), F32)

        n, _ = _rms(x_ref[...])
        h = (n * g_ref[...]).astype(BF16)
        h_ref[...] = h
        z = _dot_nt(h, w_ref[...])
        za_ref[...] = z[:, :D_SSM]
        zb = z[:, D_SSM:D_SSM + D_POOL]
        zuv = z[:, D_SSM + D_POOL:]
        zuv_ref[...] = zuv
        buf[pl.ds(tile0, TS), :] = zb
        wl = wl_ref[...]
        pooled = (_window_sum(buf, tmp, tile0, wl, -1) / _pool_count(i, TS, wl) - zb).astype(BF16)
        buf[pl.ds(POOL_PAD, MAX_WINDOW), :] = zb[TS - MAX_WINDOW:, :]
        pooled_ref[...] = pooled
        ob_ref[...] = (_dot(pooled, wp_ref[...]) * sc_ref[...]).astype(BF16)
        lo, hi = _half_masks()
        u, _, _, vl, _, _ = _sgu_front(zuv, lng_ref[...], lnb_ref[...])
        mixed = _sgu_mix(vl, wsp_ref, lo, hi) + jnp.tile(bias_ref[...], (TS // CHUNK, 1))
        oc_ref[...] = (u * mixed).astype(BF16)

    return pl.pallas_call(
        body, grid=(s // TS,),
        in_specs=[_row(TS, D_MODEL), _const((1, D_MODEL)), _const((D_IN, D_MODEL)), _const((1, D_POOL)),
                  _const((D_POOL, D_POOL)), _const((1, D_POOL)), _const((1, D_SGU)), _const((1, D_SGU)),
                  _const((SGU_HEADS // 2, CHUNK, 2 * CHUNK)), _const((CHUNK, D_SGU))],
        out_specs=[_row(TS, D_SSM), _row(TS, 2 * D_SGU), _row(TS, D_MODEL), _row(TS, D_POOL), _row(TS, D_POOL),
                   _row(TS, D_SGU)],
        out_shape=[jax.ShapeDtypeStruct((s, D_SSM), F32), jax.ShapeDtypeStruct((s, 2 * D_SGU), F32),
                   jax.ShapeDtypeStruct((s, D_MODEL), BF16), jax.ShapeDtypeStruct((s, D_POOL), BF16),
                   jax.ShapeDtypeStruct((s, D_POOL), BF16), jax.ShapeDtypeStruct((s, D_SGU), BF16)],
        scratch_shapes=[pltpu.VMEM((POOL_ROWS, D_POOL), F32)] * 4,
        name=f"mix_fwd_{tag}", compiler_params=_cp(dimension_semantics=("arbitrary",)),
    )(x, p["g_mix"], p["w_in"], _pool_consts(), p["w_pool_bd"], p["pool_scale"], p["sgu_ln_g"], p["sgu_ln_b"],
      p["ws_pair"], p["bias_sp"])


def _blk_fwd(x0, oa, ob, oc, p, tag, head=None):
    s = x0.shape[0]
    ts = TS_FFN
    n_head = 0 if head is None else len(head)

    def body(x0_ref, oa_ref, ob_ref, oc_ref, wo_ref, g_ref, wg_ref, wu_ref, wd_ref, *refs):
        x1_ref, x2_ref, h2_ref, gt_ref, up_ref, ycat_ref = refs[n_head:n_head + 6]
        ycat = jnp.concatenate([oa_ref[...], ob_ref[...], oc_ref[...]], axis=1)
        ycat_ref[...] = ycat
        x1 = x0_ref[...] + _dot(ycat, wo_ref[...])
        x1_ref[...] = x1
        n, _ = _rms(x1)
        h2 = (n * g_ref[...]).astype(BF16)
        h2_ref[...] = h2
        gt = _dot_nt(h2, wg_ref[...])
        up = _dot_nt(h2, wu_ref[...])
        gt_ref[...] = gt.astype(BF16)
        up_ref[...] = up.astype(BF16)
        act = (gt * _sigmoid(gt) * up).astype(BF16)
        x2 = x1 + _dot(act, wd_ref[...])
        if head is None:
            x2_ref[...] = x2
            return
        t_ref, gf_ref = refs[:n_head]
        loss_ref, dgf_ref = refs[n_head + 6:]

        @pl.when(pl.program_id(0) == 0)
        def _():
            loss_ref[...] = jnp.zeros_like(loss_ref)
            dgf_ref[...] = jnp.zeros_like(dgf_ref)

        gf = gf_ref[...]
        nf, rf = _rms(x2)
        diff = nf * gf - t_ref[...]
        loss_ref[...] += jnp.sum(diff * diff) * (0.5 / D_MODEL)
        dxn, dgp = _rms_bwd(diff * (1.0 / D_MODEL), nf, rf, gf)
        dgf_ref[...] += _colsum8(dgp)
        x2_ref[...] = dxn

    in_specs = [_row(ts, D_MODEL), _row(ts, D_SSM), _row(ts, D_POOL), _row(ts, D_SGU), _const((D_MODEL, D_MODEL)),
                _const((1, D_MODEL)), _const((D_FF, D_MODEL)), _const((D_FF, D_MODEL)), _const((D_FF, D_MODEL))]
    out_specs = [_row(ts, D_MODEL), _row(ts, D_MODEL), _row(ts, D_MODEL), _row(ts, D_FF), _row(ts, D_FF),
                 _row(ts, D_MODEL)]
    out_shape = [jax.ShapeDtypeStruct((s, D_MODEL), F32), jax.ShapeDtypeStruct((s, D_MODEL), F32),
                 jax.ShapeDtypeStruct((s, D_MODEL), BF16), jax.ShapeDtypeStruct((s, D_FF), BF16),
                 jax.ShapeDtypeStruct((s, D_FF), BF16), jax.ShapeDtypeStruct((s, D_MODEL), BF16)]
    args = (x0, oa, ob, oc, p["w_out"], p["g_ffn"], p["w_gate"], p["w_up"], p["w_down"])
    if head is not None:
        in_specs += [_row(ts, D_MODEL), _const((1, D_MODEL))]
        out_specs += [_acc((SUBLANES, LANES)), _acc((SUBLANES, D_MODEL))]
        out_shape += [jax.ShapeDtypeStruct((SUBLANES, LANES), F32), jax.ShapeDtypeStruct((SUBLANES, D_MODEL), F32)]
        args += tuple(head)
    return pl.pallas_call(
        body, grid=(s // ts,), in_specs=in_specs, out_specs=out_specs, out_shape=out_shape,
        name=f"blk_fwd_{tag}", compiler_params=_cp(dimension_semantics=("arbitrary",)),
    )(*args)


def _blk_bwd(dx2, x1, gt, up, p, tag, after=()):
    s = dx2.shape[0]
    ts = TS_FFN

    def body(dx2_ref, x1_ref, gt_ref, up_ref, wd_ref, wgt_ref, wut_ref, wo_ref, g_ref,
             dx1_ref, da_ref, db_ref, dc_ref, dgt_ref, dup_ref, act_ref, dg_ref):
        @pl.when(pl.program_id(0) == 0)
        def _():
            dg_ref[...] = jnp.zeros_like(dg_ref)

        dx2v = dx2_ref[...]
        dact = _dot_nt(dx2v.astype(BF16), wd_ref[...])
        gf = gt_ref[...].astype(F32)
        uf = up_ref[...].astype(F32)
        sg = _sigmoid(gf)
        sl = gf * sg
        act_ref[...] = (sl * uf).astype(BF16)
        dgt = (dact * uf * (sg * (1.0 + gf * (1.0 - sg)))).astype(BF16)
        dup = (dact * sl).astype(BF16)
        dgt_ref[...] = dgt
        dup_ref[...] = dup
        dh2 = _dot(dgt, wgt_ref[...]) + _dot(dup, wut_ref[...])
        n, r = _rms(x1_ref[...])
        dxn, dgp = _rms_bwd(dh2, n, r, g_ref[...])
        dg_ref[...] += _colsum8(dgp)
        dx1 = dx2v + dxn
        dx1_ref[...] = dx1
        dy = _dot_nt(dx1.astype(BF16), wo_ref[...])
        da_ref[...] = dy[:, :D_SSM]
        db_ref[...] = dy[:, D_SSM:D_SSM + D_POOL]
        dc_ref[...] = dy[:, D_SSM + D_POOL:]

    return pl.pallas_call(
        _following(body, after), grid=(s // ts,),
        in_specs=_anywhere(after) + [_row(ts, D_MODEL), _row(ts, D_MODEL), _row(ts, D_FF), _row(ts, D_FF),
                  _const((D_FF, D_MODEL)), _const((D_FF, D_MODEL)), _const((D_FF, D_MODEL)),
                  _const((D_MODEL, D_MODEL)), _const((1, D_MODEL))],
        out_specs=[_row(ts, D_MODEL), _row(ts, D_SSM), _row(ts, D_POOL), _row(ts, D_SGU), _row(ts, D_FF),
                   _row(ts, D_FF), _row(ts, D_FF), _acc((SUBLANES, D_MODEL))],
        out_shape=[jax.ShapeDtypeStruct((s, D_MODEL), F32), jax.ShapeDtypeStruct((s, D_SSM), F32),
                   jax.ShapeDtypeStruct((s, D_POOL), F32), jax.ShapeDtypeStruct((s, D_SGU), F32),
                   jax.ShapeDtypeStruct((s, D_FF), BF16), jax.ShapeDtypeStruct((s, D_FF), BF16),
                   jax.ShapeDtypeStruct((s, D_FF), BF16), jax.ShapeDtypeStruct((SUBLANES, D_MODEL), F32)],
        name=f"blk_bwd_{tag}", compiler_params=_cp(dimension_semantics=("arbitrary",)),
    )(*after, dx2, x1, gt, up, p["w_down"], p["w_gate"], p["w_up"], p["w_out"], p["g_ffn"])


S5_OWN = (N_USLAB, LANES, LANES)
assert 2 * SSM_STATE == LANES


def _own_blocks(acc_ref, j):
    pairs = SLAB_STATES // LANES
    group = lax.broadcasted_iota(jnp.int32, (LANES, LANES), 0) // (LANES // (2 * pairs))
    lane = lax.broadcasted_iota(jnp.int32, (LANES, LANES), 1)
    parts = []
    for half in range(2):
        own = jnp.zeros((LANES, LANES), F32)
        for q in range(pairs):
            own = jnp.where(group // 2 == q, acc_ref[j, :, pl.ds(half * SLAB_STATES + q * LANES, LANES)], own)
        parts.append(jnp.where(group % 2 == 0, own, pltpu.roll(own, SSM_STATE, 1)))
    return jnp.where(lane < SSM_STATE, parts[0], pltpu.roll(parts[1], SSM_STATE, 1))


def _own_blocks_split(d):
    d = d.reshape(N_GROUPS, D_SSM // N_GROUPS, 2, SSM_STATE)
    return d[:, :, 0], d[:, :, 1]


def _s5_bwd(dout, u, y, h_re, h_im, p, tag, after=()):
    s = u.shape[0]
    nt = s // TS
    seg = SEG

    def rev(n):
        return pl.BlockSpec((TS, n), lambda i: (nt - 1 - i, 0))

    def body(do_ref, u_ref, y_ref, hr_ref, hi_ref, perm_ref, ar_ref, ai_ref, cb_ref, bb_ref, dsk_ref,
             wglu_ref, bglu_ref,
             du_ref, dct_own_ref, dbb_own_ref, dar_ref, dai_ref, dd_ref, dwglu_ref, dbglu_ref,
             gr, gi, hsr, hsi, er, ei, jr, ji, cr, ci, dct_ref, dbb_ref):
        @pl.when(pl.program_id(0) == 0)
        def _():
            for ref in (cr, ci, dct_ref, dbb_ref, dar_ref, dai_ref, dd_ref, dwglu_ref, dbglu_ref):
                ref[...] = jnp.zeros_like(ref)

        perm = perm_ref[...]
        uv = _to_scan_order(perm, u_ref[...])
        yv = y_ref[...]
        dov = _to_scan_order(perm, do_ref[...])
        g, gelu_dy = _gelu_and_grad(yv)
        gb = g.astype(BF16)
        sg = _sigmoid(_dot(gb, wglu_ref[...]) + bglu_ref[...])
        dpre = dov * g * sg * (1.0 - sg)
        dpb = dpre.astype(BF16)
        dwglu_ref[...] += _dot_tn(gb, dpb)
        dbglu_ref[...] += _colsum8(dpre)
        dy = (dov * sg + _dot_nt(dpb, wglu_ref[...])) * gelu_dy
        dd_ref[...] += _colsum8(dy * uv)
        dyb = dy.astype(BF16)
        hsr[...] = hr_ref[...].astype(F32)
        hsi[...] = hi_ref[...].astype(F32)
        ub = uv.astype(BF16)
        dus = []

        def state_cotangents(j):
            dct_ref[j] += _dot_tn(_lanes(dyb, j), _state_cat(hr_ref, hi_ref, j))
            _state_split(gr, gi, j, _dot(_lanes(dyb, j), cb_ref[j]))

        def input_cotangents(j):
            gb_j = _state_cat(gr, gi, j)
            dbb_ref[j] += _dot_tn(_lanes(ub, j), gb_j)
            dus.append(_dot(gb_j, bb_ref[j]))

        def scan(j):
            cols = _states(j)
            ar = ar_ref[:, cols]
            ai = -ai_ref[:, cols]
            g_r = g_i = jnp.zeros((SUBLANES, SLAB_STATES), F32)
            for k in range(seg - 1, -1, -1):
                rows = pl.ds(SUBLANES * k, SUBLANES)
                n_r, n_i = _cmul(ar, ai, g_r, g_i)
                g_r = n_r + gr[rows, cols]
                g_i = n_i + gi[rows, cols]
            er[:, cols] = g_r
            ei[:, cols] = g_i
            pr, pi = _cpow(ar[0:1, :], ai[0:1, :], seg)
            c_r = cr[:, cols]
            c_i = ci[:, cols]
            for q in range(SUBLANES - 1, -1, -1):
                jr[q:q + 1, cols] = c_r
                ji[q:q + 1, cols] = c_i
                n_r, n_i = _cmul(pr, pi, c_r, c_i)
                c_r = n_r + er[q:q + 1, cols]
                c_i = n_i + ei[q:q + 1, cols]
            cr[:, cols] = c_r
            ci[:, cols] = c_i
            g_r = jr[:, cols]
            g_i = ji[:, cols]
            a_r = a_i = jnp.zeros((SUBLANES, SLAB_STATES), F32)
            for k in range(seg - 1, -1, -1):
                rows = pl.ds(SUBLANES * k, SUBLANES)
                h_r = hsr[rows, cols]
                h_i = hsi[rows, cols]
                a_r = a_r + g_r * h_r + g_i * h_i
                a_i = a_i + g_i * h_r - g_r * h_i
                n_r, n_i = _cmul(ar, ai, g_r, g_i)
                g_r = n_r + gr[rows, cols]
                g_i = n_i + gi[rows, cols]
                gr[rows, cols] = g_r
                gi[rows, cols] = g_i
            dar_ref[:, cols] += a_r
            dai_ref[:, cols] += a_i

        for stage in (state_cotangents, scan, input_cotangents):
            for j in range(N_USLAB):
                stage(j)
        du = dy * dsk_ref[...] + jnp.concatenate(dus, axis=1)
        du_ref[...] = _dot_tn(perm, du.astype(BF16)).astype(BF16)

        @pl.when(pl.program_id(0) == nt - 1)
        def _():
            for j in range(N_USLAB):
                dct_own_ref[j] = _own_blocks(dct_ref, j)
                dbb_own_ref[j] = _own_blocks(dbb_ref, j)

    big = (TS, N_STATE)
    return pl.pallas_call(
        _following(body, after), grid=(nt,),
        in_specs=_anywhere(after) + [rev(D_SSM), rev(D_SSM), rev(D_SSM), rev(N_STATE), rev(N_STATE), _const((TS, TS)),
                  _const(STATE_TILE), _const(STATE_TILE), _const(S5_IN), _const(S5_OUT), _const((1, D_SSM)),
                  _const((D_SSM, D_SSM)), _const((1, D_SSM))],
        out_specs=[rev(D_SSM), _acc(S5_OWN), _acc(S5_OWN), _acc(STATE_TILE), _acc(STATE_TILE),
                   _acc((SUBLANES, D_SSM)), _acc((D_SSM, D_SSM)), _acc((SUBLANES, D_SSM))],
        out_shape=[jax.ShapeDtypeStruct((s, D_SSM), BF16), jax.ShapeDtypeStruct(S5_OWN, F32),
                   jax.ShapeDtypeStruct(S5_OWN, F32), jax.ShapeDtypeStruct(STATE_TILE, F32),
                   jax.ShapeDtypeStruct(STATE_TILE, F32), jax.ShapeDtypeStruct((SUBLANES, D_SSM), F32),
                   jax.ShapeDtypeStruct((D_SSM, D_SSM), F32), jax.ShapeDtypeStruct((SUBLANES, D_SSM), F32)],
        scratch_shapes=[pltpu.VMEM(big, F32), pltpu.VMEM(big, F32), pltpu.VMEM(big, F32), pltpu.VMEM(big, F32),
                        pltpu.VMEM(STATE_TILE, F32), pltpu.VMEM(STATE_TILE, F32), pltpu.VMEM(STATE_TILE, F32),
                        pltpu.VMEM(STATE_TILE, F32), pltpu.VMEM((1, N_STATE), F32), pltpu.VMEM((1, N_STATE), F32),
                        pltpu.VMEM(S5_IN, F32), pltpu.VMEM(S5_IN, F32)],
        name=f"s5_bwd_{tag}", compiler_params=_cp(dimension_semantics=("arbitrary",)),
    )(*after, dout, u, y, h_re, h_im, _scan_order(), p["a_re8"], p["a_im8"], p["cb3"], p["bb3"], p["d_skip"], p["w_glu"],
      p["b_glu"])


def _mix_bwd(dza, db, dc, pooled, zuv, x0, dx1, p, tag, after=()):
    s = x0.shape[0]
    nt = s // TS

    def rev(n):
        return pl.BlockSpec((TS, n), lambda i: (nt - 1 - i, 0))

    def body(da_ref, db_ref, dc_ref, po_ref, z_ref, x_ref, dx1_ref, wl_ref, wp_ref, wpt_ref, sc_ref, lng_ref, lnb_ref,
             wsp_ref, wspt_ref, bias_ref, win_ref, g_ref,
             dx0_ref, dz_ref, dwp_ref, dsc_ref, dws_ref, dbias_ref, dlng_ref, dlnb_ref, dg_ref, buf, *tmp):
        step = pl.program_id(0)
        i = nt - 1 - step

        @pl.when(step == 0)
        def _():
            for ref in (dwp_ref, dsc_ref, dws_ref, dbias_ref, dlng_ref, dlnb_ref, dg_ref):
                ref[...] = jnp.zeros_like(ref)
            for ref in (buf, *tmp):
                ref[pl.ds(TS, POOL_ROWS - TS), :] = jnp.zeros((POOL_ROWS - TS, D_POOL), F32)

        wl = wl_ref[...]
        sc = sc_ref[...]
        dob = db_ref[...]
        pooled_b = po_ref[...]
        dsc_ref[...] += _colsum8(dob * _dot(pooled_b, wp_ref[...]))
        dmixb = (dob * sc).astype(BF16)
        dwp_ref[...] += _dot_tn(pooled_b, dmixb)
        dpool = _dot(dmixb, wpt_ref[...])
        dq = dpool / _pool_count(i, TS, wl)
        buf[pl.ds(0, TS), :] = dq
        dzb = _window_sum(buf, tmp, 0, wl, 1) - dpool
        buf[pl.ds(TS, MAX_WINDOW), :] = dq[:MAX_WINDOW, :]

        lo, hi = _half_masks()
        lng = lng_ref[...]
        u, vn, rs, vl, gelu_du, gelu_dv = _sgu_front(z_ref[...], lng, lnb_ref[...], grads=True)
        mixed = _sgu_mix(vl, wsp_ref, lo, hi) + jnp.tile(bias_ref[...], (TS // CHUNK, 1))
        doc = dc_ref[...]
        dzu = doc * mixed * gelu_du
        dmix = doc * u
        dbias = dbias_ref[...]
        for c in range(TS // CHUNK):
            dmc = dmix[CHUNK * c:CHUNK * (c + 1), :]
            dbias = dbias + dmc
            vlc = vl[CHUNK * c:CHUNK * (c + 1), :].astype(BF16)
            for q in range(SGU_HEADS // 2):
                dmq = _lanes(dmc, q)
                vq = _lanes(vlc, q)
                dws_ref[2 * q] += _dot_nt((dmq * lo).astype(BF16), vq)
                dws_ref[2 * q + 1] += _dot_nt((dmq * hi).astype(BF16), vq)
        dbias_ref[...] = dbias
        dvl = _sgu_mix(dmix, wspt_ref, lo, hi)
        dlng_ref[...] += _colsum8(dvl * vn)
        dlnb_ref[...] += _colsum8(dvl)
        dvn = dvl * lng
        dv = rs * (dvn - jnp.mean(dvn, axis=-1, keepdims=True) - vn * jnp.mean(dvn * vn, axis=-1, keepdims=True))

        dz = jnp.concatenate([da_ref[...], dzb.astype(BF16), dzu.astype(BF16), (dv * gelu_dv).astype(BF16)], axis=1)
        dz_ref[...] = dz
        n, r = _rms(x_ref[...])
        dxn, dgp = _rms_bwd(_dot(dz, win_ref[...]), n, r, g_ref[...])
        dg_ref[...] += _colsum8(dgp)
        dx0_ref[...] = dx1_ref[...] + dxn

    pair = (SGU_HEADS // 2, CHUNK, 2 * CHUNK)
    return pl.pallas_call(
        _following(body, after), grid=(nt,),
        in_specs=_anywhere(after) + [rev(D_SSM), rev(D_POOL), rev(D_SGU), rev(D_POOL), rev(2 * D_SGU), rev(D_MODEL), rev(D_MODEL),
                  _const((1, D_POOL)), _const((D_POOL, D_POOL)), _const((D_POOL, D_POOL)), _const((1, D_POOL)),
                  _const((1, D_SGU)), _const((1, D_SGU)), _const(pair), _const(pair), _const((CHUNK, D_SGU)),
                  _const((D_IN, D_MODEL)), _const((1, D_MODEL))],
        out_specs=[rev(D_MODEL), rev(D_IN), _acc((D_POOL, D_POOL)), _acc((SUBLANES, D_POOL)),
                   _acc((SGU_HEADS, CHUNK, CHUNK)), _acc((CHUNK, D_SGU)), _acc((SUBLANES, D_SGU)),
                   _acc((SUBLANES, D_SGU)), _acc((SUBLANES, D_MODEL))],
        out_shape=[jax.ShapeDtypeStruct((s, D_MODEL), F32), jax.ShapeDtypeStruct((s, D_IN), BF16),
                   jax.ShapeDtypeStruct((D_POOL, D_POOL), F32), jax.ShapeDtypeStruct((SUBLANES, D_POOL), F32),
                   jax.ShapeDtypeStruct((SGU_HEADS, CHUNK, CHUNK), F32), jax.ShapeDtypeStruct((CHUNK, D_SGU), F32),
                   jax.ShapeDtypeStruct((SUBLANES, D_SGU), F32), jax.ShapeDtypeStruct((SUBLANES, D_SGU), F32),
                   jax.ShapeDtypeStruct((SUBLANES, D_MODEL), F32)],
        scratch_shapes=[pltpu.VMEM((POOL_ROWS, D_POOL), F32)] * 4,
        name=f"mix_bwd_{tag}", compiler_params=_cp(dimension_semantics=("arbitrary",)),
    )(*after, dza, db, dc, pooled, zuv, x0, dx1, _pool_consts(), p["w_pool_bd"], p["w_pool_bd_t"], p["pool_scale"],
      p["sgu_ln_g"], p["sgu_ln_b"], p["ws_pair"], p["ws_pair_t"], p["bias_sp"], p["w_in"], p["g_mix"])


def _atb(a, b, tag, after=()):
    s, ka = a.shape
    kb = b.shape[1]
    ts = ATB_ROWS
    tn = min(kb, ATB_COLS)
    ns = s // ts

    def body(a_ref, b_ref, *rest):
        o_ref = rest[-1]

        @pl.when(pl.program_id(1) == 0)
        def _():
            o_ref[...] = jnp.zeros_like(o_ref)

        o_ref[...] += _dot_tn(a_ref[...].astype(BF16), b_ref[...].astype(BF16))

    return pl.pallas_call(
        body, grid=(kb // tn, ns),
        in_specs=[pl.BlockSpec((ts, ka), lambda j, i: (i, 0)), pl.BlockSpec((ts, tn), lambda j, i: (i, j))]
        + [pl.BlockSpec(memory_space=pl.ANY)] * len(after),
        out_specs=pl.BlockSpec((ka, tn), lambda j, i: (0, j)),
        out_shape=jax.ShapeDtypeStruct((ka, kb), F32),
        name=f"atb_{tag}", compiler_params=_cp(dimension_semantics=("arbitrary", "arbitrary")),
    )(a, b, *after)


def _s5_discretise(a_re, a_im, log_dt, b_re, b_im):
    dt = jnp.exp(log_dt)[:, None]
    mag = jnp.exp(a_re * dt)
    ar = mag * jnp.cos(a_im * dt)
    ai = mag * jnp.sin(a_im * dt)
    den = a_re * a_re + a_im * a_im
    f_re = ((ar - 1.0) * a_re + ai * a_im) / den
    f_im = (ai * a_re - (ar - 1.0) * a_im) / den
    bb_re = f_re[..., None] * b_re - f_im[..., None] * b_im
    bb_im = f_re[..., None] * b_im + f_im[..., None] * b_re
    return ar, ai, bb_re, bb_im


def _block_diag(blocks):
    g, r, c = blocks.shape
    eye = jnp.eye(g, dtype=blocks.dtype)
    return (blocks[:, :, None, :] * eye[:, None, :, None]).reshape(g * r, g * c)


def _block_diag_extract(m, g):
    r = m.shape[0] // g
    c = m.shape[1] // g
    eye = jnp.eye(g, dtype=m.dtype)
    return jnp.sum(m.reshape(g, r, g, c) * eye[:, None, :, None], axis=2)


GROUPS_PER_SLAB = N_GROUPS // N_USLAB


def _slab_diag(blocks):
    k = GROUPS_PER_SLAB
    _, r, c = blocks.shape
    eye = jnp.eye(k, dtype=blocks.dtype)
    spread = blocks.reshape(N_USLAB, k, r, 1, c) * eye[None, :, None, :, None]
    return spread.reshape(N_USLAB, k * r, k * c)


def _state_slabs(v):
    return jnp.broadcast_to(v.reshape(1, N_STATE), STATE_TILE)


def _tril():
    return jnp.tril(jnp.ones((CHUNK, CHUNK), dtype=bool))


def _layer_params(w, l):
    row = lambda v: v.reshape(1, -1)
    t = lambda m: jnp.swapaxes(m, -1, -2)
    ar, ai, bb_re, bb_im = _s5_discretise(w["A_re"][l], w["A_im"][l], w["log_dt"][l], w["B_re"][l], w["B_im"][l])
    bbt3 = jnp.concatenate([_slab_diag(t(bb_re)), _slab_diag(t(bb_im))], axis=2).astype(BF16)
    ct3 = jnp.concatenate([_slab_diag(t(w["C_re"][l])), -_slab_diag(t(w["C_im"][l]))], axis=1).astype(BF16)
    ws = jnp.where(_tril()[None], w["w_spatial"][l], 0.0)
    pair = lambda m: jnp.stack([jnp.concatenate([m[2 * q], m[2 * q + 1]], axis=1)
                                for q in range(SGU_HEADS // 2)]).astype(BF16)
    wp = _block_diag(w["w_pool"][l]).astype(BF16)
    p = dict(
        g_mix=row(w["g_mix"][l]), g_ffn=row(w["g_ffn"][l]), d_skip=row(w["D_skip"][l]), b_glu=row(w["b_glu"][l]),
        pool_scale=row(w["pool_scale"][l]), sgu_ln_g=row(w["sgu_ln_g"][l]), sgu_ln_b=row(w["sgu_ln_b"][l]),
        a_re8=_state_slabs(ar), a_im8=_state_slabs(ai),
        bbt3=bbt3, bb3=t(bbt3), ct3=ct3, cb3=t(ct3),
        w_pool_bd=wp, w_pool_bd_t=t(wp), ws_pair=pair(ws), ws_pair_t=pair(t(ws)),
        bias_sp=jnp.repeat(t(w["b_spatial"][l]), SGU_HEAD_DIM, axis=1),
    )
    return p


MIX_WEIGHTS = ("w_in", "w_glu")
FFN_WEIGHTS = ("w_out", "w_gate", "w_up", "w_down")


def _with_big(p, mats):
    p.update(mats)


def _rows_sum(v):
    return jnp.sum(v, axis=0)


ATB_COLS = 1024
ATB_ROWS = 1024


def _after(v, token):
    return v + token[0, 0]


def _layer_bwd(dx2, sv, p, w, l, tag, hooks, after):
    t = lambda m: jnp.swapaxes(m, -1, -2)
    dx1, da, db, dc, dgt, dup, act, dg_ffn = _blk_bwd(dx2, sv["x1"], sv["gt"], sv["up"], p, tag, after)
    after = hooks["tick"]([dx1])
    after = after + hooks["on_grads"](l, "ffn", {
        "w_down": _atb(act, dx2, tag + "_wd", after), "w_gate": _atb(dgt, sv["h2"], tag + "_wg", after),
        "w_up": _atb(dup, sv["h2"], tag + "_wu", after), "w_out": _atb(sv["ycat"], dx1, tag + "_wo", after)})
    g = {}
    g["g_ffn"] = _rows_sum(dg_ffn)
    dza, dc3, dbbt3, dar8, dai8, dd8, dwglu, dbglu8 = _s5_bwd(
        da, sv["za"], sv["y"], sv["h_re"], sv["h_im"], p, tag, after)
    after = hooks["tick"]([dza])
    dx0, dz, dwp, dsc8, dws, dbias, dlng8, dlnb8, dg_mix = _mix_bwd(
        dza, db, dc, sv["pooled"], sv["zuv"], sv["x0"], dx1, p, tag, after)
    g["g_mix"] = _rows_sum(dg_mix)
    g["b_glu"] = _rows_sum(dbglu8)
    g["D_skip"] = _rows_sum(dd8)
    dc_re, dc_im = _own_blocks_split(dc3)
    g["C_re"] = dc_re
    g["C_im"] = -dc_im
    dar = jnp.sum(dar8, axis=0).reshape(N_GROUPS, SSM_STATE)
    dai = jnp.sum(dai8, axis=0).reshape(N_GROUPS, SSM_STATE)
    dbb_re, dbb_im = [t(d) for d in _own_blocks_split(dbbt3)]
    _, disc_vjp = jax.vjp(_s5_discretise, w["A_re"][l], w["A_im"][l], w["log_dt"][l], w["B_re"][l], w["B_im"][l])
    g["A_re"], g["A_im"], g["log_dt"], g["B_re"], g["B_im"] = disc_vjp((dar, dai, dbb_re, dbb_im))
    g["w_pool"] = _block_diag_extract(dwp, len(POOL_WINDOWS))
    g["pool_scale"] = _rows_sum(dsc8)
    g["sgu_ln_g"] = _rows_sum(dlng8)
    g["sgu_ln_b"] = _rows_sum(dlnb8)
    g["w_spatial"] = jnp.where(_tril()[None], dws, 0.0)
    g["b_spatial"] = t(jnp.sum(dbias.reshape(CHUNK, SGU_HEADS, SGU_HEAD_DIM), axis=-1))
    after = hooks["on_small"](l, g)
    after = hooks["on_grads"](l, "mix", {"w_in": _atb(dz, sv["h1"], tag + "_wi", after), "w_glu": dwglu})
    return dx0, after


def _local_step(x, target, w, hooks):
    params = [_layer_params(w, l) for l in range(DEPTH)]
    saved = []
    h = x
    for l in range(DEPTH):
        p, tag = params[l], f"l{l}"
        _with_big(p, hooks["get_big"](l, "mix", [h]))
        za, zuv, h1, ob, pooled, oc = _mix_fwd(h, p, tag)
        oa, y, h_re, h_im = _s5_fwd(za, p, tag)
        _with_big(p, hooks["get_big"](l, "ffn", [oa, ob, oc]))
        head = (target, w["g_final"].reshape(1, -1)) if l == DEPTH - 1 else None
        x1, x2, h2, gt, up, ycat, *loss_parts = _blk_fwd(h, oa, ob, oc, p, tag, head)
        saved.append(dict(x0=h, za=za, zuv=zuv, h1=h1, ycat=ycat, y=y, h_re=h_re, h_im=h_im, pooled=pooled, x1=x1,
                          h2=h2, gt=gt, up=up))
        h = x2
    dx = h
    loss8, dgf8 = loss_parts
    grads = [None] * DEPTH

    def on_small(l, g_l):
        grads[l] = g_l
        if l > 0:
            return []
        g = {n: jnp.stack([grads[k][n] for k in range(DEPTH)]) for n in SMALL if n != "g_final"}
        g["g_final"] = _rows_sum(dgf8)
        return hooks["on_small"](g, loss8[0, 0])

    after = []
    for l in reversed(range(DEPTH)):
        dx, after = _layer_bwd(dx, saved[l], params[l], w, l, f"l{l}", dict(hooks, on_small=on_small), after)
    return dx


_ANY = pl.BlockSpec(memory_space=pl.ANY)
_MESH = pl.DeviceIdType.MESH


def _place():
    return lax.axis_index("x"), lax.axis_index("y"), lax.axis_index("c")


def _other_chips(x, y):
    return [(1 - x, y), (x, 1 - y), (1 - x, 1 - y)]


def _dma_sems(n):
    return pltpu.SemaphoreType.DMA((n,))


def _remote(src, dst, send_sems, recv_sems, k, to):
    return pltpu.make_async_remote_copy(src_ref=src, dst_ref=dst, send_sem=send_sems.at[k], recv_sem=recv_sems.at[k],
                                        device_id=to, device_id_type=_MESH)


_HBM = pl.BlockSpec(memory_space=pltpu.HBM)
_SEM = pl.BlockSpec(memory_space=pltpu.SEMAPHORE)
_EFFECT = pltpu.SideEffectType.DATAFLOW_SIDE_EFFECTING
N_REL = N_CHIPS - 1


def _gather_plan(x, y, c, srcs, lands):
    plan = []
    for l in lands:
        r = l.shape[0] // N_CHIPS
        rows = l.at[pl.ds((2 * x + y) * r, r)]
        plan += [(rows, rows, (cx, cy, c)) for cx, cy in _other_chips(x, y)]
    return plan


def _half_rows(land, chip, c):
    h = land.shape[0] // (2 * N_CHIPS)
    return land.at[pl.ds((2 * chip + c) * h, h)]


def _gather_half_plan(x, y, c, srcs, lands):
    return [(_half_rows(l, 2 * x + y, c), _half_rows(l, 2 * x + y, c), (cx, cy, c))
            for l in lands for cx, cy in _other_chips(x, y)]


def _forward_halves(lands, tag):
    nw = len(lands)

    def body(*refs):
        ins = refs[:nw]
        send_sems, recv_sems = refs[2 * nw:]
        x, y, c = _place()
        chips = [2 * cx + cy for cx, cy in _other_chips(x, y)]
        sends = [_remote(_half_rows(ins[i], k, c), _half_rows(ins[i], k, c), send_sems, recv_sems, N_REL * i + j,
                         (x, y, 1 - c)) for i in range(nw) for j, k in enumerate(chips)]
        for cp in sends:
            cp.start()
        for i in range(nw):
            for j, k in enumerate(chips):
                sends[N_REL * i + j].wait_send()
                _remote(_half_rows(ins[i], k, c), _half_rows(ins[i], k, 1 - c), send_sems, recv_sems, N_REL * i + j,
                        (x, y, 1 - c)).wait_recv()

    return pl.pallas_call(
        body, out_shape=[jax.ShapeDtypeStruct(a.shape, a.dtype) for a in lands], in_specs=[_ANY] * nw,
        out_specs=[_ANY] * nw, input_output_aliases={i: i for i in range(nw)},
        scratch_shapes=[_dma_sems(N_REL * nw), _dma_sems(N_REL * nw)], name=f"forward_halves_{tag}",
    )(*lands)


def _sibling_plan(x, y, c, srcs, lands):
    return [(s.at[:, 1 - c], l, (x, y, 1 - c)) for s, l in zip(srcs, lands)]


def _slab_plan(x, y, c, srcs, lands):
    return [(s.at[2 * cx + cy], l.at[j], (cx, cy, c))
            for s, l in zip(srcs, lands) for j, (cx, cy) in enumerate(_other_chips(x, y))]


def _plan_copies(plan, srcs, lands, send_sems, recv_sems):
    x, y, c = _place()
    return [_remote(s, d, send_sems, recv_sems, k, to) for k, (s, d, to) in enumerate(plan(x, y, c, srcs, lands))]


def _hbm(a):
    return pltpu.with_memory_space_constraint(a, pltpu.HBM)


def _everyone_plan(x, y, c, srcs, lands):
    me = 4 * x + 2 * y + c
    peers = [(x, y, 1 - c)] + [(cx, cy, cc) for cx, cy in _other_chips(x, y) for cc in (c, 1 - c)]
    return [(s, l.at[me], peer) for s, l in zip(srcs, lands) for peer in peers]


def _copies_start(name, plan, srcs, lands, ncopies):
    ns, n = len(srcs), len(srcs) + len(lands)

    def body(*refs):
        for cp in _plan_copies(plan, refs[:ns], refs[ns:n], refs[n], refs[n + 1]):
            cp.start()
        refs[-1][...] = jnp.zeros_like(refs[-1])

    ref_out = [pltpu.HBM(a.shape, a.dtype) for a in (*srcs, *lands)]
    out = pl.pallas_call(
        body, name=name, in_specs=[_HBM] * n,
        out_shape=(_dma_sems(ncopies), _dma_sems(ncopies), *ref_out, jax.ShapeDtypeStruct((SUBLANES, LANES), F32)),
        out_specs=(_SEM, _SEM, *[_HBM] * n, pl.BlockSpec(memory_space=pltpu.VMEM)),
        input_output_aliases={i: 2 + i for i in range(n)},
        compiler_params=pltpu.CompilerParams(has_side_effects=_EFFECT),
    )(*[_hbm(a) for a in (*srcs, *lands)])
    return dict(name=name, plan=plan, sems=out[:2], srcs=out[2:2 + ns], lands=out[2 + ns:2 + n], token=out[-1])


def _copies_wait(started, after):
    ns = len(started["srcs"])
    n = ns + len(started["lands"])
    plan = started["plan"]

    def body(*refs):
        for cp in _plan_copies(plan, refs[:ns], refs[ns:n], refs[n], refs[n + 1]):
            cp.wait_send()
            cp.wait_recv()

    args = (*started["srcs"], *started["lands"])
    out = pl.pallas_call(
        body, name=started["name"] + "_wait", out_shape=[pltpu.HBM(a.shape, a.dtype) for a in args],
        in_specs=[_HBM] * n + [_SEM, _SEM] + [_ANY] * len(after), out_specs=[_HBM] * n,
        input_output_aliases={i: i for i in range(n)},
        compiler_params=pltpu.CompilerParams(has_side_effects=_EFFECT),
    )(*args, *started["sems"], *after)
    return out[:ns], out[ns:]


def _place_shards(ws, layer, sel, after, tag):
    nw = len(ws)

    def body(sel_ref, *refs):
        for i in range(nw):
            refs[nw + len(after) + i][...] = refs[i][...].astype(BF16)

    return pl.pallas_call(
        body, grid_spec=pltpu.PrefetchScalarGridSpec(
            num_scalar_prefetch=1, grid=(1,),
            in_specs=[pl.BlockSpec((None,) + a.shape[1:], lambda i, s: (layer, 0, 0)) for a in ws] + [_ANY] * len(after),
            out_specs=[pl.BlockSpec(a.shape[1:], lambda i, s: (s[1], 0)) for a in ws]),
        out_shape=[jax.ShapeDtypeStruct((N_CHIPS * a.shape[1], a.shape[2]), BF16) for a in ws],
        name=f"place_shards_{tag}", compiler_params=_cp(dimension_semantics=("arbitrary",)),
    )(sel, *ws, *after)


def _share_plan(layer):
    def plan(x, y, c, srcs, lands):
        def mine(f):
            h = f.shape[1] // 2
            return f.at[layer, pl.ds(c * h, h)]

        return [(mine(f), mine(f), (x, y, 1 - c)) for f in lands]

    return plan


def _add_halves(g4s, recvs, sel, tag):
    nw = len(g4s)

    def body(sel_ref, *refs):
        for i in range(nw):
            refs[2 * nw + i][...] = (refs[i][...] + refs[nw + i][...]).astype(BF16)

    mine = [pl.BlockSpec((None, None) + g.shape[2:], lambda k, s: (k, s[0], 0, 0)) for g in g4s]
    slab = [pl.BlockSpec((None,) + g.shape[2:], lambda k, s: (k, 0, 0)) for g in g4s]
    return pl.pallas_call(
        body, grid_spec=pltpu.PrefetchScalarGridSpec(num_scalar_prefetch=1, grid=(N_CHIPS,), in_specs=mine + slab,
                                                     out_specs=slab),
        out_shape=[jax.ShapeDtypeStruct(r.shape, BF16) for r in recvs], name=f"add_halves_{tag}",
        compiler_params=_cp(dimension_semantics=("arbitrary",)),
    )(sel, *g4s, *recvs)


def _add_chips(ps, slabs, fs, layer, sel, tag):
    nw = len(ps)
    old = [f for f in fs if f is not None]

    def body(sel_ref, *refs):
        outs = refs[2 * nw + len(old):]
        for i in range(nw):
            acc = refs[i][...].astype(F32)
            for j in range(N_REL):
                acc = acc + refs[nw + i][j].astype(F32)
            outs[i][...] = acc

    shapes = [(DEPTH, 2 * p.shape[1], p.shape[2]) for p in ps]
    in_specs = [pl.BlockSpec((None,) + p.shape[1:], lambda i, s: (s[1], 0, 0)) for p in ps]
    in_specs += [pl.BlockSpec(sl.shape, lambda i, s: (0, 0, 0)) for sl in slabs]
    in_specs += [_ANY] * len(old)
    first_old = 1 + 2 * nw
    aliases, k = {}, 0
    for i, f in enumerate(fs):
        if f is not None:
            aliases[first_old + k] = i
            k += 1
    return pl.pallas_call(
        body, grid_spec=pltpu.PrefetchScalarGridSpec(
            num_scalar_prefetch=1, grid=(1,), in_specs=in_specs,
            out_specs=[pl.BlockSpec((None,) + p.shape[1:], lambda i, s: (layer, s[0], 0)) for p in ps]),
        out_shape=[jax.ShapeDtypeStruct(sh, F32) for sh in shapes], input_output_aliases=aliases,
        name=f"add_chips_{tag}", compiler_params=_cp(dimension_semantics=("arbitrary",)),
    )(sel, *ps, *slabs, *old)


def _adamw_math(w, g, m, v):
    m = ADAM_B1 * m + (1.0 - ADAM_B1) * g
    v = ADAM_B2 * v + (1.0 - ADAM_B2) * (g * g)
    m_hat = m / (1.0 - ADAM_B1 ** ADAM_STEP)
    v_hat = v / (1.0 - ADAM_B2 ** ADAM_STEP)
    delta = -ADAM_LR * (m_hat / (jnp.sqrt(v_hat) + ADAM_EPS) + ADAM_WD * w)
    return delta, m, v


ADAM_ROWS = 512


def _row_tile(rows, most):
    return max(t for t in range(SUBLANES, most + 1, SUBLANES) if rows % t == 0)


def _adamw(w, g, m, v, tag, after=()):
    depth, rows, cols = w.shape
    tr = _row_tile(rows, ADAM_ROWS)

    def body(w_ref, g_ref, m_ref, v_ref, *rest):
        d_ref, nm_ref, nv_ref = rest[len(after):]
        d, nm, nv = _adamw_math(w_ref[...], g_ref[...], m_ref[...], v_ref[...])
        d_ref[...] = d
        nm_ref[...] = nm
        nv_ref[...] = nv

    spec = pl.BlockSpec((None, tr, cols), lambda l, i: (l, i, 0))
    return pl.pallas_call(
        body, grid=(depth, rows // tr), in_specs=[spec] * 4 + [_ANY] * len(after), out_specs=[spec] * 3,
        out_shape=[jax.ShapeDtypeStruct(w.shape, F32)] * 3, name=f"adamw_{tag}",
        compiler_params=_cp(dimension_semantics=("arbitrary", "arbitrary")),
    )(w, g, m, v, *after)


SMALL_TILE = 512
PRECISE = ("g_final",)
COARSE = [n for n in SMALL if n not in PRECISE]


def _small_reduce(gathered):
    n = len(gathered)

    def body(*refs):
        for ga_ref, g_ref in zip(refs[:n], refs[n:]):
            g = ga_ref[0].astype(F32)
            for k in range(1, N_DEV):
                g = g + ga_ref[k].astype(F32)
            g_ref[...] = g

    return pl.pallas_call(
        body, out_shape=[jax.ShapeDtypeStruct(b.shape[1:], F32) for b in gathered], name="small_reduce",
        compiler_params=_cp(),
    )(*gathered)


def _small_adamw(g, w, m, v):
    names = list(g)
    n = len(names)
    shapes = {k: g[k].shape if g[k].ndim > 1 else (1,) + g[k].shape for k in names}

    def body(*refs):
        ins, outs = refs[:4 * n], refs[4 * n:]
        for i in range(n):
            d, nm, nv = _adamw_math(ins[n + i][...], ins[i][...], ins[2 * n + i][...], ins[3 * n + i][...])
            outs[i][...] = d
            outs[n + i][...] = nm
            outs[2 * n + i][...] = nv

    out = pl.pallas_call(
        body, out_shape=[jax.ShapeDtypeStruct(shapes[k], F32) for _ in range(3) for k in names], name="adamw_small",
        compiler_params=_cp(),
    )(*[src[k].reshape(shapes[k]) for src in (g, w, m, v) for k in names])
    return [{k: out[j * n + i].reshape(g[k].shape) for i, k in enumerate(names)} for j in range(3)]


def _exchange_form(n, a):
    return jnp.swapaxes(a, 1, 2) if n in TRANSPOSED else a


PACK_ROWS = 16


def _rows_of(size):
    return -(-size // (LANES * PACK_ROWS)) * PACK_ROWS


SMALL_VIEW = {"B_re": (0, 1, 3, 2), "B_im": (0, 1, 3, 2), "b_spatial": (1, 0, 2)}
assert all(tuple(order[i] for i in order) == tuple(range(len(order))) for order in SMALL_VIEW.values())


def _view(n, a):
    return jnp.transpose(a, SMALL_VIEW[n]) if n in SMALL_VIEW else a


def _pack(vals, names, extra=None):
    parts = [_view(n, vals[n]).reshape(-1) for n in names] + ([] if extra is None else [extra.reshape(1)])
    tiles = [jnp.pad(a, (0, _rows_of(a.size) * LANES - a.size)).reshape(-1, LANES) for a in parts]
    rows = sum(t.shape[0] for t in tiles)
    if rows > SMALL_TILE:
        tiles.append(jnp.zeros((-rows % SMALL_TILE, LANES), tiles[0].dtype))
    return jnp.concatenate(tiles, axis=0)


def _unpack(buf, like, names):
    out, row = {}, 0
    for n in names:
        rows = _rows_of(like[n].size)
        shape = tuple(like[n].shape[i] for i in SMALL_VIEW.get(n, range(like[n].ndim)))
        out[n] = buf[row:row + rows].reshape(-1)[:like[n].size].reshape(shape)
        row += rows
    return out, buf[row:]


def kernel(x, g_mix, w_in, A_re, A_im, log_dt, B_re, B_im, C_re, C_im, D_skip, w_glu, b_glu, w_pool, pool_scale, sgu_ln_g, sgu_ln_b, w_spatial, b_spatial, w_out, g_ffn, w_gate, w_up, w_down, g_final, loss_target, m_g_mix, m_w_in, m_A_re, m_A_im, m_log_dt, m_B_re, m_B_im, m_C_re, m_C_im, m_D_skip, m_w_glu, m_b_glu, m_w_pool, m_pool_scale, m_sgu_ln_g, m_sgu_ln_b, m_w_spatial, m_b_spatial, m_w_out, m_g_ffn, m_w_gate, m_w_up, m_w_down, m_g_final, v_g_mix, v_w_in, v_A_re, v_A_im, v_log_dt, v_B_re, v_B_im, v_C_re, v_C_im, v_D_skip, v_w_glu, v_b_glu, v_w_pool, v_pool_scale, v_sgu_ln_g, v_sgu_ln_b, v_w_spatial, v_b_spatial, v_w_out, v_g_ffn, v_w_gate, v_w_up, v_w_down, v_g_final):
    loc = locals()
    w = {n: loc[n] for n in WEIGHTS}
    m = {n: loc["m_" + n] for n in WEIGHTS}
    v = {n: loc["v_" + n] for n in WEIGHTS}
    sel = jnp.stack([lax.axis_index("c"), 2 * lax.axis_index("x") + lax.axis_index("y")]).astype(jnp.int32)

    chip = sel[1]

    halves = [(l, half) for l in range(DEPTH) for half in ("mix", "ffn")]
    two_level = {(0, "ffn")}
    names = {"mix": MIX_WEIGHTS, "ffn": FFN_WEIGHTS}
    started = {}
    wx = {n: _exchange_form(n, w[n]) for n in BIG}
    chain = []
    for l, half in halves:
        lands = _place_shards([wx[n] for n in names[half]], l, sel, chain, f"l{l}_{half}")
        plan = _gather_half_plan if (l, half) in two_level else _gather_plan
        started[l, half] = _copies_start(f"weights_l{l}_{half}", plan, [], lands, N_REL * len(lands))
        chain = [started[l, half]["token"]]
    w = dict(w, g_mix=_after(w["g_mix"], started[halves[-1]]["token"]))

    def get_big(l, half, after):
        lands = _copies_wait(started[l, half], after)[1]
        if (l, half) in two_level:
            lands = _forward_halves(lands, f"l{l}_{half}")
        return dict(zip(names[half], lands))

    result = {n: None for n in BIG}
    stage = {"swap": None, "slabs": None, "share": None}

    def advance(after):
        tokens = []
        if stage["share"] is not None:
            sh, ns = stage["share"]
            for n, f in zip(ns, _copies_wait(sh, after)[1]):
                result[n] = f
            stage["share"] = None
        if stage["slabs"] is not None:
            ex, ns, l, tag = stage["slabs"]
            part, slabs = _copies_wait(ex, after)
            bufs = _add_chips(part, slabs, [result[n] for n in ns], l, sel, tag)
            sh = _copies_start(f"share_{tag}", _share_plan(l), [], bufs, len(bufs))
            stage["share"], stage["slabs"] = (sh, ns), None
            tokens.append(sh["token"])
        if stage["swap"] is not None:
            sw, ns, l, tag = stage["swap"]
            part = _add_halves(*_copies_wait(sw, after), sel, tag)
            slabs = [lax.empty((N_REL,) + p.shape[1:], BF16) for p in part]
            ex = _copies_start(f"grads_{tag}", _slab_plan, part, slabs, N_REL * len(part))
            stage["slabs"], stage["swap"] = (ex, ns, l, tag), None
            tokens.append(ex["token"])
        return tokens

    def on_grads(l, half, grads):
        ns = list(grads)
        tag = f"l{l}_{half}"
        tokens = advance([grads[ns[0]]])
        g4s = [grads[n].reshape(N_CHIPS, 2, grads[n].shape[0] // (2 * N_CHIPS), grads[n].shape[1]) for n in ns]
        recvs = [lax.empty((N_CHIPS,) + g4.shape[2:], F32) for g4 in g4s]
        sw = _copies_start(f"swap_{tag}", _sibling_plan, g4s, recvs, len(g4s))
        stage["swap"] = (sw, ns, l, tag)
        return tokens + [sw["token"]]

    small = {}

    def on_small(g, loss_local):
        me = 2 * chip + sel[0]
        blocks = [_pack(g, COARSE).astype(BF16), _pack(g, PRECISE, loss_local)]
        lands = [lax.dynamic_update_slice(lax.empty((N_DEV,) + b.shape, b.dtype), b[None], (me, 0, 0)) for b in blocks]
        small.update(_copies_start("small_grads", _everyone_plan, blocks, lands, (N_DEV - 1) * len(blocks)))
        return [small["token"]]

    dx = _local_step(x[0], loss_target[0], w, dict(get_big=get_big, on_grads=on_grads, tick=advance, on_small=on_small))
    grads, deltas, new_m, new_v = {}, {}, {}, {}

    def update_big(ns, after):
        for n in ns:
            outs = _adamw(wx[n], result[n], _exchange_form(n, m[n]), _exchange_form(n, v[n]), n, after)
            grads[n], deltas[n], new_m[n], new_v[n] = [_exchange_form(n, a) for a in (result[n], *outs)]
            after = [outs[-1]]
        return after

    _, gathered = _copies_wait(small, [stage["swap"][0]["token"]])
    coarse, precise = _small_reduce(gathered)
    small_g, _ = _unpack(coarse, w, COARSE)
    precise_g, rest = _unpack(precise, w, PRECISE)
    small_g.update(precise_g)
    loss = rest[0, 0]
    views = [{n: _view(n, src[n]) for n in small_g} for src in (w, m, v)]
    for store, vals in zip((grads, deltas, new_m, new_v), (small_g, *_small_adamw(small_g, *views))):
        store.update({n: _view(n, a) for n, a in vals.items()})
    sent = advance([new_v[SMALL[0]]])
    shared = advance(update_big(FFN_WEIGHTS[1:], sent))
    advance(update_big(FFN_WEIGHTS[:1], shared))
    update_big(MIX_WEIGHTS, [])
    return (loss, dx[None], *[grads[n] for n in WEIGHTS], *[deltas[n] for n in WEIGHTS],
            *[new_m[n] for n in WEIGHTS], *[new_v[n] for n in WEIGHTS])
```

```python
import math

import jax
import jax.numpy as jnp
from jax import lax
from jax.experimental import pallas as pl
from jax.experimental.pallas import tpu as pltpu

F32 = jnp.float32
BF16 = jnp.bfloat16

D_MODEL = 1024
DEPTH = 2
D_SSM = 384
SSM_GROUP = 16
N_GROUPS = 24
SSM_STATE = 64
N_STATE = N_GROUPS * SSM_STATE
POOL_WINDOWS = (2, 4, 8, 16)
POOL_GROUP = 64
D_POOL = 256
MAX_WINDOW = 16
SGU_HEADS = 6
SGU_HEAD_DIM = 64
D_SGU = 384
CHUNK = 128
D_IN = D_SSM + D_POOL + 2 * D_SGU
D_FF = 2816
EPS = 1e-6

ADAM_LR = 0.001
ADAM_B1 = 0.9
ADAM_B2 = 0.999
ADAM_EPS = 1e-08
ADAM_WD = 0.01
ADAM_STEP = 10

LANES = 128
SUBLANES = 8
VMEM_LIMIT = 56 * 1024 * 1024

TS = 512
TS_FFN = 256

WEIGHTS = ['g_mix', 'w_in', 'A_re', 'A_im', 'log_dt', 'B_re', 'B_im', 'C_re', 'C_im', 'D_skip', 'w_glu', 'b_glu',
           'w_pool', 'pool_scale', 'sgu_ln_g', 'sgu_ln_b', 'w_spatial', 'b_spatial', 'w_out', 'g_ffn', 'w_gate',
           'w_up', 'w_down', 'g_final']
BIG = ['w_in', 'w_glu', 'w_out', 'w_gate', 'w_up', 'w_down']
SMALL = [n for n in WEIGHTS if n not in BIG]
TRANSPOSED = ("w_in", "w_gate", "w_up")
N_CHIPS = 4
N_DEV = 8


def _cp(**kw):
    return pltpu.CompilerParams(vmem_limit_bytes=VMEM_LIMIT, **kw)


def _row(ts, n):
    return pl.BlockSpec((ts, n), lambda i: (i, 0))


def _const(shape):
    nd = len(shape)
    return pl.BlockSpec(shape, lambda i: (0,) * nd, pipeline_mode=pl.Buffered(1))


def _acc(shape):
    nd = len(shape)
    return pl.BlockSpec(shape, lambda i: (0,) * nd)


def _following(body, after):
    k = len(after)
    return lambda *refs: body(*refs[k:])


def _anywhere(after):
    return [pl.BlockSpec(memory_space=pl.ANY)] * len(after)


def _dot(a, b):
    return jnp.dot(a, b, preferred_element_type=F32)


def _dot_tn(a, b):
    return lax.dot_general(a, b, (((0,), (0,)), ((), ())), preferred_element_type=F32)


def _dot_nt(a, b):
    return lax.dot_general(a, b, (((1,), (1,)), ((), ())), preferred_element_type=F32)


_G0 = math.sqrt(2.0 / math.pi)
_G1 = 0.044715


def _gelu(x):
    return 0.5 * x * (1.0 + jnp.tanh(_G0 * (x + _G1 * x * x * x)))


def _gelu_and_grad(x):
    t = jnp.tanh(_G0 * (x + _G1 * x * x * x))
    half = 0.5 * (1.0 + t)
    return x * half, half + 0.5 * x * (1.0 - t * t) * (_G0 * (1.0 + 3.0 * _G1 * x * x))


def _sigmoid(x):
    return 1.0 / (1.0 + jnp.exp(-x))


def _rms(x):
    r = lax.rsqrt(jnp.mean(x * x, axis=-1, keepdims=True) + EPS)
    return x * r, r


def _rms_bwd(dh, n, r, g):
    dn = dh * g
    return r * (dn - n * jnp.mean(dn * n, axis=-1, keepdims=True)), dh * n


def _colsum8(v):
    rows, n = v.shape
    return jnp.sum(v.reshape(rows // SUBLANES, SUBLANES, n), axis=0)


def _cmul(ar, ai, br, bi):
    return ar * br - ai * bi, ar * bi + ai * br


def _cpow(ar, ai, n):
    assert n & (n - 1) == 0
    while n > 1:
        ar, ai = _cmul(ar, ai, ar, ai)
        n //= 2
    return ar, ai


N_USLAB = D_SSM // LANES
SEG = TS // SUBLANES
SLAB_STATES = N_STATE // N_USLAB
S5_IN = (N_USLAB, LANES, 2 * SLAB_STATES)
S5_OUT = (N_USLAB, 2 * SLAB_STATES, LANES)
STATE_TILE = (SUBLANES, N_STATE)


def _scan_order():
    p = jnp.arange(TS)
    src = (p % SUBLANES) * SEG + p // SUBLANES
    return (src[:, None] == jnp.arange(TS)[None, :]).astype(BF16)


def _to_scan_order(perm, v):
    hi = v.astype(BF16)
    lo = (v - hi.astype(F32)).astype(BF16)
    return _dot(perm, hi) + _dot(perm, lo)


def _scan_rows(k):
    return pl.ds(pl.multiple_of(k * SUBLANES, SUBLANES), SUBLANES)


def _lanes(v, j):
    return v[:, LANES * j:LANES * (j + 1)]


def _states(j):
    return pl.ds(SLAB_STATES * j, SLAB_STATES)


def _state_split(re_ref, im_ref, j, v):
    re_ref[:, _states(j)] = v[:, :SLAB_STATES]
    im_ref[:, _states(j)] = v[:, SLAB_STATES:]


def _state_cat(re_ref, im_ref, j):
    return jnp.concatenate([re_ref[:, _states(j)], im_ref[:, _states(j)]], axis=1).astype(BF16)


def _s5_fwd(u, p, tag):
    s = u.shape[0]
    seg = SEG

    def body(u_ref, perm_ref, bbt_ref, ar_ref, ai_ref, ct_ref, dsk_ref, wglu_ref, bglu_ref,
             oa_ref, y_ref, hr_ref, hi_ref, sr, si, er, ei, ir, ii, cr, ci):
        @pl.when(pl.program_id(0) == 0)
        def _():
            cr[...] = jnp.zeros_like(cr)
            ci[...] = jnp.zeros_like(ci)

        perm = perm_ref[...]
        uv = _to_scan_order(perm, u_ref[...])
        ub = uv.astype(BF16)
        for j in range(N_USLAB):
            _state_split(sr, si, j, _dot(_lanes(ub, j), bbt_ref[j]))
        for j in range(N_USLAB):
            cols = _states(j)
            ar = ar_ref[:, cols]
            ai = ai_ref[:, cols]
            h_r = h_i = jnp.zeros((SUBLANES, SLAB_STATES), F32)
            for k in range(seg):
                rows = pl.ds(SUBLANES * k, SUBLANES)
                n_r, n_i = _cmul(ar, ai, h_r, h_i)
                h_r = n_r + sr[rows, cols]
                h_i = n_i + si[rows, cols]
            er[:, cols] = h_r
            ei[:, cols] = h_i
            pr, pi = _cpow(ar[0:1, :], ai[0:1, :], seg)
            c_r = cr[:, cols]
            c_i = ci[:, cols]
            for q in range(SUBLANES):
                ir[q:q + 1, cols] = c_r
                ii[q:q + 1, cols] = c_i
                n_r, n_i = _cmul(pr, pi, c_r, c_i)
                c_r = n_r + er[q:q + 1, cols]
                c_i = n_i + ei[q:q + 1, cols]
            cr[:, cols] = c_r
            ci[:, cols] = c_i
            h_r = ir[:, cols]
            h_i = ii[:, cols]
            for k in range(seg):
                rows = pl.ds(SUBLANES * k, SUBLANES)
                n_r, n_i = _cmul(ar, ai, h_r, h_i)
                h_r = n_r + sr[rows, cols]
                h_i = n_i + si[rows, cols]
                sr[rows, cols] = h_r
                si[rows, cols] = h_i
        hr_ref[...] = sr[...].astype(BF16)
        hi_ref[...] = si[...].astype(BF16)
        y = jnp.concatenate([_dot(_state_cat(hr_ref, hi_ref, j), ct_ref[j]) for j in range(N_USLAB)], axis=1)
        y = y + dsk_ref[...] * uv
        y_ref[...] = y
        g = _gelu(y)
        pre = _dot(g.astype(BF16), wglu_ref[...]) + bglu_ref[...]
        oa_ref[...] = _dot_tn(perm, (g * _sigmoid(pre)).astype(BF16)).astype(BF16)

    return pl.pallas_call(
        body, grid=(s // TS,),
        in_specs=[_row(TS, D_SSM), _const((TS, TS)), _const(S5_IN), _const(STATE_TILE), _const(STATE_TILE),
                  _const(S5_OUT), _const((1, D_SSM)), _const((D_SSM, D_SSM)), _const((1, D_SSM))],
        out_specs=[_row(TS, D_SSM), _row(TS, D_SSM), _row(TS, N_STATE), _row(TS, N_STATE)],
        out_shape=[jax.ShapeDtypeStruct((s, D_SSM), BF16), jax.ShapeDtypeStruct((s, D_SSM), F32),
                   jax.ShapeDtypeStruct((s, N_STATE), BF16), jax.ShapeDtypeStruct((s, N_STATE), BF16)],
        scratch_shapes=[pltpu.VMEM((TS, N_STATE), F32), pltpu.VMEM((TS, N_STATE), F32),
                        pltpu.VMEM(STATE_TILE, F32), pltpu.VMEM(STATE_TILE, F32), pltpu.VMEM(STATE_TILE, F32),
                        pltpu.VMEM(STATE_TILE, F32), pltpu.VMEM((1, N_STATE), F32), pltpu.VMEM((1, N_STATE), F32)],
        name=f"s5_fwd_{tag}", compiler_params=_cp(dimension_semantics=("arbitrary",)),
    )(u, _scan_order(), p["bbt3"], p["a_re8"], p["a_im8"], p["ct3"], p["d_skip"], p["w_glu"], p["b_glu"])


def _pool_consts():
    w = jnp.repeat(jnp.asarray(POOL_WINDOWS, F32), POOL_GROUP)[None, :]
    return w


POOL_PAD = SUBLANES
POOL_ROWS = TS + MAX_WINDOW + POOL_PAD


def _window_sum(buf, tmp, first, wl, step):
    assert POOL_WINDOWS == (2, 4, 8, 16)
    n = TS + MAX_WINDOW
    lo = first - MAX_WINDOW if step < 0 else first
    src = buf
    for k, dst in zip((1, 2, 4), tmp):
        dst[pl.ds(lo, n), :] = src[pl.ds(lo, n), :] + src[pl.ds(lo + step * k, n), :]
        src = dst
    s2, s4, s8 = (t[pl.ds(first, TS), :] for t in tmp)
    s16 = s8 + tmp[2][pl.ds(first + step * 8, TS), :]
    return jnp.where(wl == 2, s2, jnp.where(wl == 4, s4, jnp.where(wl == 8, s8, s16)))


def _pool_count(i, rows, wl):
    t = (i * TS + 1).astype(F32) + lax.broadcasted_iota(jnp.int32, (rows, 1), 0).astype(F32)
    return jnp.minimum(t, wl)


def _sgu_mix(vl, wpair_ref, lo, hi):
    rows = vl.shape[0]
    chunks = []
    for c in range(rows // CHUNK):
        vc = vl[CHUNK * c:CHUNK * (c + 1), :]
        parts = []
        for q in range(SGU_HEADS // 2):
            vq = vc[:, LANES * q:LANES * (q + 1)]
            rhs = jnp.concatenate([vq * lo, vq * hi], axis=0).astype(BF16)
            parts.append(_dot(wpair_ref[q], rhs))
        chunks.append(jnp.concatenate(parts, axis=1))
    return jnp.concatenate(chunks, axis=0)


def _sgu_front(zuv, lng, lnb, grads=False):
    gelu = _gelu_and_grad if grads else lambda z: (_gelu(z), None)
    u, du = gelu(zuv[:, :D_SGU])
    v, dv = gelu(zuv[:, D_SGU:])
    mu = jnp.mean(v, axis=-1, keepdims=True)
    vc = v - mu
    rs = lax.rsqrt(jnp.mean(vc * vc, axis=-1, keepdims=True) + EPS)
    vn = vc * rs
    return u, vn, rs, vn * lng + lnb, du, dv


def _half_masks():
    lane = lax.broadcasted_iota(jnp.int32, (1, LANES), 1)
    lo = (lane < SGU_HEAD_DIM).astype(F32)
    return lo, 1.0 - lo


def _mix_fwd(x, p, tag):
    s = x.shape[0]

    def body(x_ref, g_ref, w_ref, wl_ref, wp_ref, sc_ref, lng_ref, lnb_ref, wsp_ref, bias_ref,
             za_ref, zuv_ref, h_ref, ob_ref, pooled_ref, oc_ref, buf, *tmp):
        i = pl.program_id(0)
        tile0 = POOL_PAD + MAX_WINDOW

        @pl.when(i == 0)
        def _():
            for ref in (buf, *tmp):
                ref[pl.ds(0, tile0), :] = jnp.zeros((tile0, D_POOL), F32)

        n, _ = _rms(x_ref[...])
        h = (n * g_ref[...]).astype(BF16)
        h_ref[...] = h
        z = _dot_nt(h, w_ref[...])
        za_ref[...] = z[:, :D_SSM]
        zb = z[:, D_SSM:D_SSM + D_POOL]
        zuv = z[:, D_SSM + D_POOL:]
        zuv_ref[...] = zuv
        buf[pl.ds(tile0, TS), :] = zb
        wl = wl_ref[...]
        pooled = (_window_sum(buf, tmp, tile0, wl, -1) / _pool_count(i, TS, wl) - zb).astype(BF16)
        buf[pl.ds(POOL_PAD, MAX_WINDOW), :] = zb[TS - MAX_WINDOW:, :]
        pooled_ref[...] = pooled
        ob_ref[...] = (_dot(pooled, wp_ref[...]) * sc_ref[...]).astype(BF16)
        lo, hi = _half_masks()
        u, _, _, vl, _, _ = _sgu_front(zuv, lng_ref[...], lnb_ref[...])
        mixed = _sgu_mix(vl, wsp_ref, lo, hi) + jnp.tile(bias_ref[...], (TS // CHUNK, 1))
        oc_ref[...] = (u * mixed).astype(BF16)

    return pl.pallas_call(
        body, grid=(s // TS,),
        in_specs=[_row(TS, D_MODEL), _const((1, D_MODEL)), _const((D_IN, D_MODEL)), _const((1, D_POOL)),
                  _const((D_POOL, D_POOL)), _const((1, D_POOL)), _const((1, D_SGU)), _const((1, D_SGU)),
                  _const((SGU_HEADS // 2, CHUNK, 2 * CHUNK)), _const((CHUNK, D_SGU))],
        out_specs=[_row(TS, D_SSM), _row(TS, 2 * D_SGU), _row(TS, D_MODEL), _row(TS, D_POOL), _row(TS, D_POOL),
                   _row(TS, D_SGU)],
        out_shape=[jax.ShapeDtypeStruct((s, D_SSM), F32), jax.ShapeDtypeStruct((s, 2 * D_SGU), F32),
                   jax.ShapeDtypeStruct((s, D_MODEL), BF16), jax.ShapeDtypeStruct((s, D_POOL), BF16),
                   jax.ShapeDtypeStruct((s, D_POOL), BF16), jax.ShapeDtypeStruct((s, D_SGU), BF16)],
        scratch_shapes=[pltpu.VMEM((POOL_ROWS, D_POOL), F32)] * 4,
        name=f"mix_fwd_{tag}", compiler_params=_cp(dimension_semantics=("arbitrary",)),
    )(x, p["g_mix"], p["w_in"], _pool_consts(), p["w_pool_bd"], p["pool_scale"], p["sgu_ln_g"], p["sgu_ln_b"],
      p["ws_pair"], p["bias_sp"])


def _blk_fwd(x0, oa, ob, oc, p, tag, head=None):
    s = x0.shape[0]
    ts = TS_FFN
    n_head = 0 if head is None else len(head)

    def body(x0_ref, oa_ref, ob_ref, oc_ref, wo_ref, g_ref, wg_ref, wu_ref, wd_ref, *refs):
        x1_ref, x2_ref, h2_ref, gt_ref, up_ref, ycat_ref = refs[n_head:n_head + 6]
        ycat = jnp.concatenate([oa_ref[...], ob_ref[...], oc_ref[...]], axis=1)
        ycat_ref[...] = ycat
        x1 = x0_ref[...] + _dot(ycat, wo_ref[...])
        x1_ref[...] = x1
        n, _ = _rms(x1)
        h2 = (n * g_ref[...]).astype(BF16)
        h2_ref[...] = h2
        gt = _dot_nt(h2, wg_ref[...])
        up = _dot_nt(h2, wu_ref[...])
        gt_ref[...] = gt.astype(BF16)
        up_ref[...] = up.astype(BF16)
        act = (gt * _sigmoid(gt) * up).astype(BF16)
        x2 = x1 + _dot(act, wd_ref[...])
        if head is None:
            x2_ref[...] = x2
            return
        t_ref, gf_ref = refs[:n_head]
        loss_ref, dgf_ref = refs[n_head + 6:]

        @pl.when(pl.program_id(0) == 0)
        def _():
            loss_ref[...] = jnp.zeros_like(loss_ref)
            dgf_ref[...] = jnp.zeros_like(dgf_ref)

        gf = gf_ref[...]
        nf, rf = _rms(x2)
        diff = nf * gf - t_ref[...]
        loss_ref[...] += jnp.sum(diff * diff) * (0.5 / D_MODEL)
        dxn, dgp = _rms_bwd(diff * (1.0 / D_MODEL), nf, rf, gf)
        dgf_ref[...] += _colsum8(dgp)
        x2_ref[...] = dxn

    in_specs = [_row(ts, D_MODEL), _row(ts, D_SSM), _row(ts, D_POOL), _row(ts, D_SGU), _const((D_MODEL, D_MODEL)),
                _const((1, D_MODEL)), _const((D_FF, D_MODEL)), _const((D_FF, D_MODEL)), _const((D_FF, D_MODEL))]
    out_specs = [_row(ts, D_MODEL), _row(ts, D_MODEL), _row(ts, D_MODEL), _row(ts, D_FF), _row(ts, D_FF),
                 _row(ts, D_MODEL)]
    out_shape = [jax.ShapeDtypeStruct((s, D_MODEL), F32), jax.ShapeDtypeStruct((s, D_MODEL), F32),
                 jax.ShapeDtypeStruct((s, D_MODEL), BF16), jax.ShapeDtypeStruct((s, D_FF), BF16),
                 jax.ShapeDtypeStruct((s, D_FF), BF16), jax.ShapeDtypeStruct((s, D_MODEL), BF16)]
    args = (x0, oa, ob, oc, p["w_out"], p["g_ffn"], p["w_gate"], p["w_up"], p["w_down"])
    if head is not None:
        in_specs += [_row(ts, D_MODEL), _const((1, D_MODEL))]
        out_specs += [_acc((SUBLANES, LANES)), _acc((SUBLANES, D_MODEL))]
        out_shape += [jax.ShapeDtypeStruct((SUBLANES, LANES), F32), jax.ShapeDtypeStruct((SUBLANES, D_MODEL), F32)]
        args += tuple(head)
    return pl.pallas_call(
        body, grid=(s // ts,), in_specs=in_specs, out_specs=out_specs, out_shape=out_shape,
        name=f"blk_fwd_{tag}", compiler_params=_cp(dimension_semantics=("arbitrary",)),
    )(*args)


def _blk_bwd(dx2, x1, gt, up, p, tag, after=()):
    s = dx2.shape[0]
    ts = TS_FFN

    def body(dx2_ref, x1_ref, gt_ref, up_ref, wd_ref, wgt_ref, wut_ref, wo_ref, g_ref,
             dx1_ref, da_ref, db_ref, dc_ref, dgt_ref, dup_ref, act_ref, dg_ref):
        @pl.when(pl.program_id(0) == 0)
        def _():
            dg_ref[...] = jnp.zeros_like(dg_ref)

        dx2v = dx2_ref[...]
        dact = _dot_nt(dx2v.astype(BF16), wd_ref[...])
        gf = gt_ref[...].astype(F32)
        uf = up_ref[...].astype(F32)
        sg = _sigmoid(gf)
        sl = gf * sg
        act_ref[...] = (sl * uf).astype(BF16)
        dgt = (dact * uf * (sg * (1.0 + gf * (1.0 - sg)))).astype(BF16)
        dup = (dact * sl).astype(BF16)
        dgt_ref[...] = dgt
        dup_ref[...] = dup
        dh2 = _dot(dgt, wgt_ref[...]) + _dot(dup, wut_ref[...])
        n, r = _rms(x1_ref[...])
        dxn, dgp = _rms_bwd(dh2, n, r, g_ref[...])
        dg_ref[...] += _colsum8(dgp)
        dx1 = dx2v + dxn
        dx1_ref[...] = dx1
        dy = _dot_nt(dx1.astype(BF16), wo_ref[...])
        da_ref[...] = dy[:, :D_SSM]
        db_ref[...] = dy[:, D_SSM:D_SSM + D_POOL]
        dc_ref[...] = dy[:, D_SSM + D_POOL:]

    return pl.pallas_call(
        _following(body, after), grid=(s // ts,),
        in_specs=_anywhere(after) + [_row(ts, D_MODEL), _row(ts, D_MODEL), _row(ts, D_FF), _row(ts, D_FF),
                  _const((D_FF, D_MODEL)), _const((D_FF, D_MODEL)), _const((D_FF, D_MODEL)),
                  _const((D_MODEL, D_MODEL)), _const((1, D_MODEL))],
        out_specs=[_row(ts, D_MODEL), _row(ts, D_SSM), _row(ts, D_POOL), _row(ts, D_SGU), _row(ts, D_FF),
                   _row(ts, D_FF), _row(ts, D_FF), _acc((SUBLANES, D_MODEL))],
        out_shape=[jax.ShapeDtypeStruct((s, D_MODEL), F32), jax.ShapeDtypeStruct((s, D_SSM), F32),
                   jax.ShapeDtypeStruct((s, D_POOL), F32), jax.ShapeDtypeStruct((s, D_SGU), F32),
                   jax.ShapeDtypeStruct((s, D_FF), BF16), jax.ShapeDtypeStruct((s, D_FF), BF16),
                   jax.ShapeDtypeStruct((s, D_FF), BF16), jax.ShapeDtypeStruct((SUBLANES, D_MODEL), F32)],
        name=f"blk_bwd_{tag}", compiler_params=_cp(dimension_semantics=("arbitrary",)),
    )(*after, dx2, x1, gt, up, p["w_down"], p["w_gate"], p["w_up"], p["w_out"], p["g_ffn"])


S5_OWN = (N_USLAB, LANES, LANES)
assert 2 * SSM_STATE == LANES


def _own_blocks(acc_ref, j):
    pairs = SLAB_STATES // LANES
    group = lax.broadcasted_iota(jnp.int32, (LANES, LANES), 0) // (LANES // (2 * pairs))
    lane = lax.broadcasted_iota(jnp.int32, (LANES, LANES), 1)
    parts = []
    for half in range(2):
        own = jnp.zeros((LANES, LANES), F32)
        for q in range(pairs):
            own = jnp.where(group // 2 == q, acc_ref[j, :, pl.ds(half * SLAB_STATES + q * LANES, LANES)], own)
        parts.append(jnp.where(group % 2 == 0, own, pltpu.roll(own, SSM_STATE, 1)))
    return jnp.where(lane < SSM_STATE, parts[0], pltpu.roll(parts[1], SSM_STATE, 1))


def _own_blocks_split(d):
    d = d.reshape(N_GROUPS, D_SSM // N_GROUPS, 2, SSM_STATE)
    return d[:, :, 0], d[:, :, 1]


def _s5_bwd(dout, u, y, h_re, h_im, p, tag, after=()):
    s = u.shape[0]
    nt = s // TS
    seg = SEG

    def rev(n):
        return pl.BlockSpec((TS, n), lambda i: (nt - 1 - i, 0))

    def body(do_ref, u_ref, y_ref, hr_ref, hi_ref, perm_ref, ar_ref, ai_ref, cb_ref, bb_ref, dsk_ref,
             wglu_ref, bglu_ref,
             du_ref, dct_own_ref, dbb_own_ref, dar_ref, dai_ref, dd_ref, dwglu_ref, dbglu_ref,
             gr, gi, hsr, hsi, er, ei, jr, ji, cr, ci, dct_ref, dbb_ref):
        @pl.when(pl.program_id(0) == 0)
        def _():
            for ref in (cr, ci, dct_ref, dbb_ref, dar_ref, dai_ref, dd_ref, dwglu_ref, dbglu_ref):
                ref[...] = jnp.zeros_like(ref)

        perm = perm_ref[...]
        uv = _to_scan_order(perm, u_ref[...])
        yv = y_ref[...]
        dov = _to_scan_order(perm, do_ref[...])
        g, gelu_dy = _gelu_and_grad(yv)
        gb = g.astype(BF16)
        sg = _sigmoid(_dot(gb, wglu_ref[...]) + bglu_ref[...])
        dpre = dov * g * sg * (1.0 - sg)
        dpb = dpre.astype(BF16)
        dwglu_ref[...] += _dot_tn(gb, dpb)
        dbglu_ref[...] += _colsum8(dpre)
        dy = (dov * sg + _dot_nt(dpb, wglu_ref[...])) * gelu_dy
        dd_ref[...] += _colsum8(dy * uv)
        dyb = dy.astype(BF16)
        hsr[...] = hr_ref[...].astype(F32)
        hsi[...] = hi_ref[...].astype(F32)
        ub = uv.astype(BF16)
        dus = []

        def state_cotangents(j):
            dct_ref[j] += _dot_tn(_lanes(dyb, j), _state_cat(hr_ref, hi_ref, j))
            _state_split(gr, gi, j, _dot(_lanes(dyb, j), cb_ref[j]))

        def input_cotangents(j):
            gb_j = _state_cat(gr, gi, j)
            dbb_ref[j] += _dot_tn(_lanes(ub, j), gb_j)
            dus.append(_dot(gb_j, bb_ref[j]))

        def scan(j):
            cols = _states(j)
            ar = ar_ref[:, cols]
            ai = -ai_ref[:, cols]
            g_r = g_i = jnp.zeros((SUBLANES, SLAB_STATES), F32)
            for k in range(seg - 1, -1, -1):
                rows = pl.ds(SUBLANES * k, SUBLANES)
                n_r, n_i = _cmul(ar, ai, g_r, g_i)
                g_r = n_r + gr[rows, cols]
                g_i = n_i + gi[rows, cols]
            er[:, cols] = g_r
            ei[:, cols] = g_i
            pr, pi = _cpow(ar[0:1, :], ai[0:1, :], seg)
            c_r = cr[:, cols]
            c_i = ci[:, cols]
            for q in range(SUBLANES - 1, -1, -1):
                jr[q:q + 1, cols] = c_r
                ji[q:q + 1, cols] = c_i
                n_r, n_i = _cmul(pr, pi, c_r, c_i)
                c_r = n_r + er[q:q + 1, cols]
                c_i = n_i + ei[q:q + 1, cols]
            cr[:, cols] = c_r
            ci[:, cols] = c_i
            g_r = jr[:, cols]
            g_i = ji[:, cols]
            a_r = a_i = jnp.zeros((SUBLANES, SLAB_STATES), F32)
            for k in range(seg - 1, -1, -1):
                rows = pl.ds(SUBLANES * k, SUBLANES)
                h_r = hsr[rows, cols]
                h_i = hsi[rows, cols]
                a_r = a_r + g_r * h_r + g_i * h_i
                a_i = a_i + g_i * h_r - g_r * h_i
                n_r, n_i = _cmul(ar, ai, g_r, g_i)
                g_r = n_r + gr[rows, cols]
                g_i = n_i + gi[rows, cols]
                gr[rows, cols] = g_r
                gi[rows, cols] = g_i
            dar_ref[:, cols] += a_r
            dai_ref[:, cols] += a_i

        for stage in (state_cotangents, scan, input_cotangents):
            for j in range(N_USLAB):
                stage(j)
        du = dy * dsk_ref[...] + jnp.concatenate(dus, axis=1)
        du_ref[...] = _dot_tn(perm, du.astype(BF16)).astype(BF16)

        @pl.when(pl.program_id(0) == nt - 1)
        def _():
            for j in range(N_USLAB):
                dct_own_ref[j] = _own_blocks(dct_ref, j)
                dbb_own_ref[j] = _own_blocks(dbb_ref, j)

    big = (TS, N_STATE)
    return pl.pallas_call(
        _following(body, after), grid=(nt,),
        in_specs=_anywhere(after) + [rev(D_SSM), rev(D_SSM), rev(D_SSM), rev(N_STATE), rev(N_STATE), _const((TS, TS)),
                  _const(STATE_TILE), _const(STATE_TILE), _const(S5_IN), _const(S5_OUT), _const((1, D_SSM)),
                  _const((D_SSM, D_SSM)), _const((1, D_SSM))],
        out_specs=[rev(D_SSM), _acc(S5_OWN), _acc(S5_OWN), _acc(STATE_TILE), _acc(STATE_TILE),
                   _acc((SUBLANES, D_SSM)), _acc((D_SSM, D_SSM)), _acc((SUBLANES, D_SSM))],
        out_shape=[jax.ShapeDtypeStruct((s, D_SSM), BF16), jax.ShapeDtypeStruct(S5_OWN, F32),
                   jax.ShapeDtypeStruct(S5_OWN, F32), jax.ShapeDtypeStruct(STATE_TILE, F32),
                   jax.ShapeDtypeStruct(STATE_TILE, F32), jax.ShapeDtypeStruct((SUBLANES, D_SSM), F32),
                   jax.ShapeDtypeStruct((D_SSM, D_SSM), F32), jax.ShapeDtypeStruct((SUBLANES, D_SSM), F32)],
        scratch_shapes=[pltpu.VMEM(big, F32), pltpu.VMEM(big, F32), pltpu.VMEM(big, F32), pltpu.VMEM(big, F32),
                        pltpu.VMEM(STATE_TILE, F32), pltpu.VMEM(STATE_TILE, F32), pltpu.VMEM(STATE_TILE, F32),
                        pltpu.VMEM(STATE_TILE, F32), pltpu.VMEM((1, N_STATE), F32), pltpu.VMEM((1, N_STATE), F32),
                        pltpu.VMEM(S5_IN, F32), pltpu.VMEM(S5_IN, F32)],
        name=f"s5_bwd_{tag}", compiler_params=_cp(dimension_semantics=("arbitrary",)),
    )(*after, dout, u, y, h_re, h_im, _scan_order(), p["a_re8"], p["a_im8"], p["cb3"], p["bb3"], p["d_skip"], p["w_glu"],
      p["b_glu"])


def _mix_bwd(dza, db, dc, pooled, zuv, x0, dx1, p, tag, after=()):
    s = x0.shape[0]
    nt = s // TS

    def rev(n):
        return pl.BlockSpec((TS, n), lambda i: (nt - 1 - i, 0))

    def body(da_ref, db_ref, dc_ref, po_ref, z_ref, x_ref, dx1_ref, wl_ref, wp_ref, wpt_ref, sc_ref, lng_ref, lnb_ref,
             wsp_ref, wspt_ref, bias_ref, win_ref, g_ref,
             dx0_ref, dz_ref, dwp_ref, dsc_ref, dws_ref, dbias_ref, dlng_ref, dlnb_ref, dg_ref, buf, *tmp):
        step = pl.program_id(0)
        i = nt - 1 - step

        @pl.when(step == 0)
        def _():
            for ref in (dwp_ref, dsc_ref, dws_ref, dbias_ref, dlng_ref, dlnb_ref, dg_ref):
                ref[...] = jnp.zeros_like(ref)
            for ref in (buf, *tmp):
                ref[pl.ds(TS, POOL_ROWS - TS), :] = jnp.zeros((POOL_ROWS - TS, D_POOL), F32)

        wl = wl_ref[...]
        sc = sc_ref[...]
        dob = db_ref[...]
        pooled_b = po_ref[...]
        dsc_ref[...] += _colsum8(dob * _dot(pooled_b, wp_ref[...]))
        dmixb = (dob * sc).astype(BF16)
        dwp_ref[...] += _dot_tn(pooled_b, dmixb)
        dpool = _dot(dmixb, wpt_ref[...])
        dq = dpool / _pool_count(i, TS, wl)
        buf[pl.ds(0, TS), :] = dq
        dzb = _window_sum(buf, tmp, 0, wl, 1) - dpool
        buf[pl.ds(TS, MAX_WINDOW), :] = dq[:MAX_WINDOW, :]

        lo, hi = _half_masks()
        lng = lng_ref[...]
        u, vn, rs, vl, gelu_du, gelu_dv = _sgu_front(z_ref[...], lng, lnb_ref[...], grads=True)
        mixed = _sgu_mix(vl, wsp_ref, lo, hi) + jnp.tile(bias_ref[...], (TS // CHUNK, 1))
        doc = dc_ref[...]
        dzu = doc * mixed * gelu_du
        dmix = doc * u
        dbias = dbias_ref[...]
        for c in range(TS // CHUNK):
            dmc = dmix[CHUNK * c:CHUNK * (c + 1), :]
            dbias = dbias + dmc
            vlc = vl[CHUNK * c:CHUNK * (c + 1), :].astype(BF16)
            for q in range(SGU_HEADS // 2):
                dmq = _lanes(dmc, q)
                vq = _lanes(vlc, q)
                dws_ref[2 * q] += _dot_nt((dmq * lo).astype(BF16), vq)
                dws_ref[2 * q + 1] += _dot_nt((dmq * hi).astype(BF16), vq)
        dbias_ref[...] = dbias
        dvl = _sgu_mix(dmix, wspt_ref, lo, hi)
        dlng_ref[...] += _colsum8(dvl * vn)
        dlnb_ref[...] += _colsum8(dvl)
        dvn = dvl * lng
        dv = rs * (dvn - jnp.mean(dvn, axis=-1, keepdims=True) - vn * jnp.mean(dvn * vn, axis=-1, keepdims=True))

        dz = jnp.concatenate([da_ref[...], dzb.astype(BF16), dzu.astype(BF16), (dv * gelu_dv).astype(BF16)], axis=1)
        dz_ref[...] = dz
        n, r = _rms(x_ref[...])
        dxn, dgp = _rms_bwd(_dot(dz, win_ref[...]), n, r, g_ref[...])
        dg_ref[...] += _colsum8(dgp)
        dx0_ref[...] = dx1_ref[...] + dxn

    pair = (SGU_HEADS // 2, CHUNK, 2 * CHUNK)
    return pl.pallas_call(
        _following(body, after), grid=(nt,),
        in_specs=_anywhere(after) + [rev(D_SSM), rev(D_POOL), rev(D_SGU), rev(D_POOL), rev(2 * D_SGU), rev(D_MODEL), rev(D_MODEL),
                  _const((1, D_POOL)), _const((D_POOL, D_POOL)), _const((D_POOL, D_POOL)), _const((1, D_POOL)),
                  _const((1, D_SGU)), _const((1, D_SGU)), _const(pair), _const(pair), _const((CHUNK, D_SGU)),
                  _const((D_IN, D_MODEL)), _const((1, D_MODEL))],
        out_specs=[rev(D_MODEL), rev(D_IN), _acc((D_POOL, D_POOL)), _acc((SUBLANES, D_POOL)),
                   _acc((SGU_HEADS, CHUNK, CHUNK)), _acc((CHUNK, D_SGU)), _acc((SUBLANES, D_SGU)),
                   _acc((SUBLANES, D_SGU)), _acc((SUBLANES, D_MODEL))],
        out_shape=[jax.ShapeDtypeStruct((s, D_MODEL), F32), jax.ShapeDtypeStruct((s, D_IN), BF16),
                   jax.ShapeDtypeStruct((D_POOL, D_POOL), F32), jax.ShapeDtypeStruct((SUBLANES, D_POOL), F32),
                   jax.ShapeDtypeStruct((SGU_HEADS, CHUNK, CHUNK), F32), jax.ShapeDtypeStruct((CHUNK, D_SGU), F32),
                   jax.ShapeDtypeStruct((SUBLANES, D_SGU), F32), jax.ShapeDtypeStruct((SUBLANES, D_SGU), F32),
                   jax.ShapeDtypeStruct((SUBLANES, D_MODEL), F32)],
        scratch_shapes=[pltpu.VMEM((POOL_ROWS, D_POOL), F32)] * 4,
        name=f"mix_bwd_{tag}", compiler_params=_cp(dimension_semantics=("arbitrary",)),
    )(*after, dza, db, dc, pooled, zuv, x0, dx1, _pool_consts(), p["w_pool_bd"], p["w_pool_bd_t"], p["pool_scale"],
      p["sgu_ln_g"], p["sgu_ln_b"], p["ws_pair"], p["ws_pair_t"], p["bias_sp"], p["w_in"], p["g_mix"])


def _atb(a, b, tag, after=()):
    s, ka = a.shape
    kb = b.shape[1]
    ts = ATB_ROWS
    tn = min(kb, ATB_COLS)
    ns = s // ts

    def body(a_ref, b_ref, *rest):
        o_ref = rest[-1]

        @pl.when(pl.program_id(1) == 0)
        def _():
            o_ref[...] = jnp.zeros_like(o_ref)

        o_ref[...] += _dot_tn(a_ref[...].astype(BF16), b_ref[...].astype(BF16))

    return pl.pallas_call(
        body, grid=(kb // tn, ns),
        in_specs=[pl.BlockSpec((ts, ka), lambda j, i: (i, 0)), pl.BlockSpec((ts, tn), lambda j, i: (i, j))]
        + [pl.BlockSpec(memory_space=pl.ANY)] * len(after),
        out_specs=pl.BlockSpec((ka, tn), lambda j, i: (0, j)),
        out_shape=jax.ShapeDtypeStruct((ka, kb), F32),
        name=f"atb_{tag}", compiler_params=_cp(dimension_semantics=("arbitrary", "arbitrary")),
    )(a, b, *after)


def _s5_discretise(a_re, a_im, log_dt, b_re, b_im):
    dt = jnp.exp(log_dt)[:, None]
    mag = jnp.exp(a_re * dt)
    ar = mag * jnp.cos(a_im * dt)
    ai = mag * jnp.sin(a_im * dt)
    den = a_re * a_re + a_im * a_im
    f_re = ((ar - 1.0) * a_re + ai * a_im) / den
    f_im = (ai * a_re - (ar - 1.0) * a_im) / den
    bb_re = f_re[..., None] * b_re - f_im[..., None] * b_im
    bb_im = f_re[..., None] * b_im + f_im[..., None] * b_re
    return ar, ai, bb_re, bb_im


def _block_diag(blocks):
    g, r, c = blocks.shape
    eye = jnp.eye(g, dtype=blocks.dtype)
    return (blocks[:, :, None, :] * eye[:, None, :, None]).reshape(g * r, g * c)


def _block_diag_extract(m, g):
    r = m.shape[0] // g
    c = m.shape[1] // g
    eye = jnp.eye(g, dtype=m.dtype)
    return jnp.sum(m.reshape(g, r, g, c) * eye[:, None, :, None], axis=2)


GROUPS_PER_SLAB = N_GROUPS // N_USLAB


def _slab_diag(blocks):
    k = GROUPS_PER_SLAB
    _, r, c = blocks.shape
    eye = jnp.eye(k, dtype=blocks.dtype)
    spread = blocks.reshape(N_USLAB, k, r, 1, c) * eye[None, :, None, :, None]
    return spread.reshape(N_USLAB, k * r, k * c)


def _state_slabs(v):
    return jnp.broadcast_to(v.reshape(1, N_STATE), STATE_TILE)


def _tril():
    return jnp.tril(jnp.ones((CHUNK, CHUNK), dtype=bool))


def _layer_params(w, l):
    row = lambda v: v.reshape(1, -1)
    t = lambda m: jnp.swapaxes(m, -1, -2)
    ar, ai, bb_re, bb_im = _s5_discretise(w["A_re"][l], w["A_im"][l], w["log_dt"][l], w["B_re"][l], w["B_im"][l])
    bbt3 = jnp.concatenate([_slab_diag(t(bb_re)), _slab_diag(t(bb_im))], axis=2).astype(BF16)
    ct3 = jnp.concatenate([_slab_diag(t(w["C_re"][l])), -_slab_diag(t(w["C_im"][l]))], axis=1).astype(BF16)
    ws = jnp.where(_tril()[None], w["w_spatial"][l], 0.0)
    pair = lambda m: jnp.stack([jnp.concatenate([m[2 * q], m[2 * q + 1]], axis=1)
                                for q in range(SGU_HEADS // 2)]).astype(BF16)
    wp = _block_diag(w["w_pool"][l]).astype(BF16)
    p = dict(
        g_mix=row(w["g_mix"][l]), g_ffn=row(w["g_ffn"][l]), d_skip=row(w["D_skip"][l]), b_glu=row(w["b_glu"][l]),
        pool_scale=row(w["pool_scale"][l]), sgu_ln_g=row(w["sgu_ln_g"][l]), sgu_ln_b=row(w["sgu_ln_b"][l]),
        a_re8=_state_slabs(ar), a_im8=_state_slabs(ai),
        bbt3=bbt3, bb3=t(bbt3), ct3=ct3, cb3=t(ct3),
        w_pool_bd=wp, w_pool_bd_t=t(wp), ws_pair=pair(ws), ws_pair_t=pair(t(ws)),
        bias_sp=jnp.repeat(t(w["b_spatial"][l]), SGU_HEAD_DIM, axis=1),
    )
    return p


MIX_WEIGHTS = ("w_in", "w_glu")
FFN_WEIGHTS = ("w_out", "w_gate", "w_up", "w_down")


def _with_big(p, mats):
    p.update(mats)


def _rows_sum(v):
    return jnp.sum(v, axis=0)


ATB_COLS = 1024
ATB_ROWS = 1024


def _after(v, token):
    return v + token[0, 0]


def _layer_bwd(dx2, sv, p, w, l, tag, hooks, after):
    t = lambda m: jnp.swapaxes(m, -1, -2)
    dx1, da, db, dc, dgt, dup, act, dg_ffn = _blk_bwd(dx2, sv["x1"], sv["gt"], sv["up"], p, tag, after)
    after = hooks["tick"]([dx1])
    after = after + hooks["on_grads"](l, "ffn", {
        "w_down": _atb(act, dx2, tag + "_wd", after), "w_gate": _atb(dgt, sv["h2"], tag + "_wg", after),
        "w_up": _atb(dup, sv["h2"], tag + "_wu", after), "w_out": _atb(sv["ycat"], dx1, tag + "_wo", after)})
    g = {}
    g["g_ffn"] = _rows_sum(dg_ffn)
    dza, dc3, dbbt3, dar8, dai8, dd8, dwglu, dbglu8 = _s5_bwd(
        da, sv["za"], sv["y"], sv["h_re"], sv["h_im"], p, tag, after)
    after = hooks["tick"]([dza])
    dx0, dz, dwp, dsc8, dws, dbias, dlng8, dlnb8, dg_mix = _mix_bwd(
        dza, db, dc, sv["pooled"], sv["zuv"], sv["x0"], dx1, p, tag, after)
    g["g_mix"] = _rows_sum(dg_mix)
    g["b_glu"] = _rows_sum(dbglu8)
    g["D_skip"] = _rows_sum(dd8)
    dc_re, dc_im = _own_blocks_split(dc3)
    g["C_re"] = dc_re
    g["C_im"] = -dc_im
    dar = jnp.sum(dar8, axis=0).reshape(N_GROUPS, SSM_STATE)
    dai = jnp.sum(dai8, axis=0).reshape(N_GROUPS, SSM_STATE)
    dbb_re, dbb_im = [t(d) for d in _own_blocks_split(dbbt3)]
    _, disc_vjp = jax.vjp(_s5_discretise, w["A_re"][l], w["A_im"][l], w["log_dt"][l], w["B_re"][l], w["B_im"][l])
    g["A_re"], g["A_im"], g["log_dt"], g["B_re"], g["B_im"] = disc_vjp((dar, dai, dbb_re, dbb_im))
    g["w_pool"] = _block_diag_extract(dwp, len(POOL_WINDOWS))
    g["pool_scale"] = _rows_sum(dsc8)
    g["sgu_ln_g"] = _rows_sum(dlng8)
    g["sgu_ln_b"] = _rows_sum(dlnb8)
    g["w_spatial"] = jnp.where(_tril()[None], dws, 0.0)
    g["b_spatial"] = t(jnp.sum(dbias.reshape(CHUNK, SGU_HEADS, SGU_HEAD_DIM), axis=-1))
    after = hooks["on_small"](l, g)
    after = hooks["on_grads"](l, "mix", {"w_in": _atb(dz, sv["h1"], tag + "_wi", after), "w_glu": dwglu})
    return dx0, after


def _local_step(x, target, w, hooks):
    params = [_layer_params(w, l) for l in range(DEPTH)]
    saved = []
    h = x
    for l in range(DEPTH):
        p, tag = params[l], f"l{l}"
        _with_big(p, hooks["get_big"](l, "mix", [h]))
        za, zuv, h1, ob, pooled, oc = _mix_fwd(h, p, tag)
        oa, y, h_re, h_im = _s5_fwd(za, p, tag)
        _with_big(p, hooks["get_big"](l, "ffn", [oa, ob, oc]))
        head = (target, w["g_final"].reshape(1, -1)) if l == DEPTH - 1 else None
        x1, x2, h2, gt, up, ycat, *loss_parts = _blk_fwd(h, oa, ob, oc, p, tag, head)
        saved.append(dict(x0=h, za=za, zuv=zuv, h1=h1, ycat=ycat, y=y, h_re=h_re, h_im=h_im, pooled=pooled, x1=x1,
                          h2=h2, gt=gt, up=up))
        h = x2
    dx = h
    loss8, dgf8 = loss_parts
    grads = [None] * DEPTH

    def on_small(l, g_l):
        grads[l] = g_l
        if l > 0:
            return []
        g = {n: jnp.stack([grads[k][n] for k in range(DEPTH)]) for n in SMALL if n != "g_final"}
        g["g_final"] = _rows_sum(dgf8)
        return hooks["on_small"](g, loss8[0, 0])

    after = []
    for l in reversed(range(DEPTH)):
        dx, after = _layer_bwd(dx, saved[l], params[l], w, l, f"l{l}", dict(hooks, on_small=on_small), after)
    return dx


_ANY = pl.BlockSpec(memory_space=pl.ANY)
_MESH = pl.DeviceIdType.MESH


def _place():
    return lax.axis_index("x"), lax.axis_index("y"), lax.axis_index("c")


def _other_chips(x, y):
    return [(1 - x, y), (x, 1 - y), (1 - x, 1 - y)]


def _dma_sems(n):
    return pltpu.SemaphoreType.DMA((n,))


def _remote(src, dst, send_sems, recv_sems, k, to):
    return pltpu.make_async_remote_copy(src_ref=src, dst_ref=dst, send_sem=send_sems.at[k], recv_sem=recv_sems.at[k],
                                        device_id=to, device_id_type=_MESH)


_HBM = pl.BlockSpec(memory_space=pltpu.HBM)
_SEM = pl.BlockSpec(memory_space=pltpu.SEMAPHORE)
_EFFECT = pltpu.SideEffectType.DATAFLOW_SIDE_EFFECTING
N_REL = N_CHIPS - 1


def _gather_plan(x, y, c, srcs, lands):
    plan = []
    for l in lands:
        r = l.shape[0] // N_CHIPS
        rows = l.at[pl.ds((2 * x + y) * r, r)]
        plan += [(rows, rows, (cx, cy, c)) for cx, cy in _other_chips(x, y)]
    return plan


def _half_rows(land, chip, c):
    h = land.shape[0] // (2 * N_CHIPS)
    return land.at[pl.ds((2 * chip + c) * h, h)]


def _gather_half_plan(x, y, c, srcs, lands):
    return [(_half_rows(l, 2 * x + y, c), _half_rows(l, 2 * x + y, c), (cx, cy, c))
            for l in lands for cx, cy in _other_chips(x, y)]


def _forward_halves(lands, tag):
    nw = len(lands)

    def body(*refs):
        ins = refs[:nw]
        send_sems, recv_sems = refs[2 * nw:]
        x, y, c = _place()
        chips = [2 * cx + cy for cx, cy in _other_chips(x, y)]
        sends = [_remote(_half_rows(ins[i], k, c), _half_rows(ins[i], k, c), send_sems, recv_sems, N_REL * i + j,
                         (x, y, 1 - c)) for i in range(nw) for j, k in enumerate(chips)]
        for cp in sends:
            cp.start()
        for i in range(nw):
            for j, k in enumerate(chips):
                sends[N_REL * i + j].wait_send()
                _remote(_half_rows(ins[i], k, c), _half_rows(ins[i], k, 1 - c), send_sems, recv_sems, N_REL * i + j,
                        (x, y, 1 - c)).wait_recv()

    return pl.pallas_call(
        body, out_shape=[jax.ShapeDtypeStruct(a.shape, a.dtype) for a in lands], in_specs=[_ANY] * nw,
        out_specs=[_ANY] * nw, input_output_aliases={i: i for i in range(nw)},
        scratch_shapes=[_dma_sems(N_REL * nw), _dma_sems(N_REL * nw)], name=f"forward_halves_{tag}",
    )(*lands)


def _sibling_plan(x, y, c, srcs, lands):
    return [(s.at[:, 1 - c], l, (x, y, 1 - c)) for s, l in zip(srcs, lands)]


def _slab_plan(x, y, c, srcs, lands):
    return [(s.at[2 * cx + cy], l.at[j], (cx, cy, c))
            for s, l in zip(srcs, lands) for j, (cx, cy) in enumerate(_other_chips(x, y))]


def _plan_copies(plan, srcs, lands, send_sems, recv_sems):
    x, y, c = _place()
    return [_remote(s, d, send_sems, recv_sems, k, to) for k, (s, d, to) in enumerate(plan(x, y, c, srcs, lands))]


def _hbm(a):
    return pltpu.with_memory_space_constraint(a, pltpu.HBM)


def _everyone_plan(x, y, c, srcs, lands):
    me = 4 * x + 2 * y + c
    peers = [(x, y, 1 - c)] + [(cx, cy, cc) for cx, cy in _other_chips(x, y) for cc in (c, 1 - c)]
    return [(s, l.at[me], peer) for s, l in zip(srcs, lands) for peer in peers]


def _copies_start(name, plan, srcs, lands, ncopies):
    ns, n = len(srcs), len(srcs) + len(lands)

    def body(*refs):
        for cp in _plan_copies(plan, refs[:ns], refs[ns:n], refs[n], refs[n + 1]):
            cp.start()
        refs[-1][...] = jnp.zeros_like(refs[-1])

    ref_out = [pltpu.HBM(a.shape, a.dtype) for a in (*srcs, *lands)]
    out = pl.pallas_call(
        body, name=name, in_specs=[_HBM] * n,
        out_shape=(_dma_sems(ncopies), _dma_sems(ncopies), *ref_out, jax.ShapeDtypeStruct((SUBLANES, LANES), F32)),
        out_specs=(_SEM, _SEM, *[_HBM] * n, pl.BlockSpec(memory_space=pltpu.VMEM)),
        input_output_aliases={i: 2 + i for i in range(n)},
        compiler_params=pltpu.CompilerParams(has_side_effects=_EFFECT),
    )(*[_hbm(a) for a in (*srcs, *lands)])
    return dict(name=name, plan=plan, sems=out[:2], srcs=out[2:2 + ns], lands=out[2 + ns:2 + n], token=out[-1])


def _copies_wait(started, after):
    ns = len(started["srcs"])
    n = ns + len(started["lands"])
    plan = started["plan"]

    def body(*refs):
        for cp in _plan_copies(plan, refs[:ns], refs[ns:n], refs[n], refs[n + 1]):
            cp.wait_send()
            cp.wait_recv()

    args = (*started["srcs"], *started["lands"])
    out = pl.pallas_call(
        body, name=started["name"] + "_wait", out_shape=[pltpu.HBM(a.shape, a.dtype) for a in args],
        in_specs=[_HBM] * n + [_SEM, _SEM] + [_ANY] * len(after), out_specs=[_HBM] * n,
        input_output_aliases={i: i for i in range(n)},
        compiler_params=pltpu.CompilerParams(has_side_effects=_EFFECT),
    )(*args, *started["sems"], *after)
    return out[:ns], out[ns:]


def _place_shards(ws, layer, sel, after, tag):
    nw = len(ws)

    def body(sel_ref, *refs):
        for i in range(nw):
            refs[nw + len(after) + i][...] = refs[i][...].astype(BF16)

    return pl.pallas_call(
        body, grid_spec=pltpu.PrefetchScalarGridSpec(
            num_scalar_prefetch=1, grid=(1,),
            in_specs=[pl.BlockSpec((None,) + a.shape[1:], lambda i, s: (layer, 0, 0)) for a in ws] + [_ANY] * len(after),
            out_specs=[pl.BlockSpec(a.shape[1:], lambda i, s: (s[1], 0)) for a in ws]),
        out_shape=[jax.ShapeDtypeStruct((N_CHIPS * a.shape[1], a.shape[2]), BF16) for a in ws],
        name=f"place_shards_{tag}", compiler_params=_cp(dimension_semantics=("arbitrary",)),
    )(sel, *ws, *after)


def _share_plan(layer):
    def plan(x, y, c, srcs, lands):
        def mine(f):
            h = f.shape[1] // 2
            return f.at[layer, pl.ds(c * h, h)]

        return [(mine(f), mine(f), (x, y, 1 - c)) for f in lands]

    return plan


def _add_halves(g4s, recvs, sel, tag):
    nw = len(g4s)

    def body(sel_ref, *refs):
        for i in range(nw):
            refs[2 * nw + i][...] = (refs[i][...] + refs[nw + i][...]).astype(BF16)

    mine = [pl.BlockSpec((None, None) + g.shape[2:], lambda k, s: (k, s[0], 0, 0)) for g in g4s]
    slab = [pl.BlockSpec((None,) + g.shape[2:], lambda k, s: (k, 0, 0)) for g in g4s]
    return pl.pallas_call(
        body, grid_spec=pltpu.PrefetchScalarGridSpec(num_scalar_prefetch=1, grid=(N_CHIPS,), in_specs=mine + slab,
                                                     out_specs=slab),
        out_shape=[jax.ShapeDtypeStruct(r.shape, BF16) for r in recvs], name=f"add_halves_{tag}",
        compiler_params=_cp(dimension_semantics=("arbitrary",)),
    )(sel, *g4s, *recvs)


def _add_chips(ps, slabs, fs, layer, sel, tag):
    nw = len(ps)
    old = [f for f in fs if f is not None]

    def body(sel_ref, *refs):
        outs = refs[2 * nw + len(old):]
        for i in range(nw):
            acc = refs[i][...].astype(F32)
            for j in range(N_REL):
                acc = acc + refs[nw + i][j].astype(F32)
            outs[i][...] = acc

    shapes = [(DEPTH, 2 * p.shape[1], p.shape[2]) for p in ps]
    in_specs = [pl.BlockSpec((None,) + p.shape[1:], lambda i, s: (s[1], 0, 0)) for p in ps]
    in_specs += [pl.BlockSpec(sl.shape, lambda i, s: (0, 0, 0)) for sl in slabs]
    in_specs += [_ANY] * len(old)
    first_old = 1 + 2 * nw
    aliases, k = {}, 0
    for i, f in enumerate(fs):
        if f is not None:
            aliases[first_old + k] = i
            k += 1
    return pl.pallas_call(
        body, grid_spec=pltpu.PrefetchScalarGridSpec(
            num_scalar_prefetch=1, grid=(1,), in_specs=in_specs,
            out_specs=[pl.BlockSpec((None,) + p.shape[1:], lambda i, s: (layer, s[0], 0)) for p in ps]),
        out_shape=[jax.ShapeDtypeStruct(sh, F32) for sh in shapes], input_output_aliases=aliases,
        name=f"add_chips_{tag}", compiler_params=_cp(dimension_semantics=("arbitrary",)),
    )(sel, *ps, *slabs, *old)


def _adamw_math(w, g, m, v):
    m = ADAM_B1 * m + (1.0 - ADAM_B1) * g
    v = ADAM_B2 * v + (1.0 - ADAM_B2) * (g * g)
    m_hat = m / (1.0 - ADAM_B1 ** ADAM_STEP)
    v_hat = v / (1.0 - ADAM_B2 ** ADAM_STEP)
    delta = -ADAM_LR * (m_hat / (jnp.sqrt(v_hat) + ADAM_EPS) + ADAM_WD * w)
    return delta, m, v


ADAM_ROWS = 512


def _row_tile(rows, most):
    return max(t for t in range(SUBLANES, most + 1, SUBLANES) if rows % t == 0)


def _adamw(w, g, m, v, tag, after=()):
    depth, rows, cols = w.shape
    tr = _row_tile(rows, ADAM_ROWS)

    def body(w_ref, g_ref, m_ref, v_ref, *rest):
        d_ref, nm_ref, nv_ref = rest[len(after):]
        d, nm, nv = _adamw_math(w_ref[...], g_ref[...], m_ref[...], v_ref[...])
        d_ref[...] = d
        nm_ref[...] = nm
        nv_ref[...] = nv

    spec = pl.BlockSpec((None, tr, cols), lambda l, i: (l, i, 0))
    return pl.pallas_call(
        body, grid=(depth, rows // tr), in_specs=[spec] * 4 + [_ANY] * len(after), out_specs=[spec] * 3,
        out_shape=[jax.ShapeDtypeStruct(w.shape, F32)] * 3, name=f"adamw_{tag}",
        compiler_params=_cp(dimension_semantics=("arbitrary", "arbitrary")),
    )(w, g, m, v, *after)


SMALL_TILE = 512
PRECISE = ("g_final",)
COARSE = [n for n in SMALL if n not in PRECISE]


def _small_reduce(gathered):
    n = len(gathered)

    def body(*refs):
        for ga_ref, g_ref in zip(refs[:n], refs[n:]):
            g = ga_ref[0].astype(F32)
            for k in range(1, N_DEV):
                g = g + ga_ref[k].astype(F32)
            g_ref[...] = g

    return pl.pallas_call(
        body, out_shape=[jax.ShapeDtypeStruct(b.shape[1:], F32) for b in gathered], name="small_reduce",
        compiler_params=_cp(),
    )(*gathered)


def _small_adamw(g, w, m, v):
    names = list(g)
    n = len(names)
    shapes = {k: g[k].shape if g[k].ndim > 1 else (1,) + g[k].shape for k in names}

    def body(*refs):
        ins, outs = refs[:4 * n], refs[4 * n:]
        for i in range(n):
            d, nm, nv = _adamw_math(ins[n + i][...], ins[i][...], ins[2 * n + i][...], ins[3 * n + i][...])
            outs[i][...] = d
            outs[n + i][...] = nm
            outs[2 * n + i][...] = nv

    out = pl.pallas_call(
        body, out_shape=[jax.ShapeDtypeStruct(shapes[k], F32) for _ in range(3) for k in names], name="adamw_small",
        compiler_params=_cp(),
    )(*[src[k].reshape(shapes[k]) for src in (g, w, m, v) for k in names])
    return [{k: out[j * n + i].reshape(g[k].shape) for i, k in enumerate(names)} for j in range(3)]


def _exchange_form(n, a):
    return jnp.swapaxes(a, 1, 2) if n in TRANSPOSED else a


PACK_ROWS = 16


def _rows_of(size):
    return -(-size // (LANES * PACK_ROWS)) * PACK_ROWS


SMALL_VIEW = {"B_re": (0, 1, 3, 2), "B_im": (0, 1, 3, 2), "b_spatial": (1, 0, 2)}
assert all(tuple(order[i] for i in order) == tuple(range(len(order))) for order in SMALL_VIEW.values())


def _view(n, a):
    return jnp.transpose(a, SMALL_VIEW[n]) if n in SMALL_VIEW else a


def _pack(vals, names, extra=None):
    parts = [_view(n, vals[n]).reshape(-1) for n in names] + ([] if extra is None else [extra.reshape(1)])
    tiles = [jnp.pad(a, (0, _rows_of(a.size) * LANES - a.size)).reshape(-1, LANES) for a in parts]
    rows = sum(t.shape[0] for t in tiles)
    if rows > SMALL_TILE:
        tiles.append(jnp.zeros((-rows % SMALL_TILE, LANES), tiles[0].dtype))
    return jnp.concatenate(tiles, axis=0)


def _unpack(buf, like, names):
    out, row = {}, 0
    for n in names:
        rows = _rows_of(like[n].size)
        shape = tuple(like[n].shape[i] for i in SMALL_VIEW.get(n, range(like[n].ndim)))
        out[n] = buf[row:row + rows].reshape(-1)[:like[n].size].reshape(shape)
        row += rows
    return out, buf[row:]


def kernel(x, g_mix, w_in, A_re, A_im, log_dt, B_re, B_im, C_re, C_im, D_skip, w_glu, b_glu, w_pool, pool_scale, sgu_ln_g, sgu_ln_b, w_spatial, b_spatial, w_out, g_ffn, w_gate, w_up, w_down, g_final, loss_target, m_g_mix, m_w_in, m_A_re, m_A_im, m_log_dt, m_B_re, m_B_im, m_C_re, m_C_im, m_D_skip, m_w_glu, m_b_glu, m_w_pool, m_pool_scale, m_sgu_ln_g, m_sgu_ln_b, m_w_spatial, m_b_spatial, m_w_out, m_g_ffn, m_w_gate, m_w_up, m_w_down, m_g_final, v_g_mix, v_w_in, v_A_re, v_A_im, v_log_dt, v_B_re, v_B_im, v_C_re, v_C_im, v_D_skip, v_w_glu, v_b_glu, v_w_pool, v_pool_scale, v_sgu_ln_g, v_sgu_ln_b, v_w_spatial, v_b_spatial, v_w_out, v_g_ffn, v_w_gate, v_w_up, v_w_down, v_g_final):
    loc = locals()
    w = {n: loc[n] for n in WEIGHTS}
    m = {n: loc["m_" + n] for n in WEIGHTS}
    v = {n: loc["v_" + n] for n in WEIGHTS}
    sel = jnp.stack([lax.axis_index("c"), 2 * lax.axis_index("x") + lax.axis_index("y")]).astype(jnp.int32)

    chip = sel[1]

    halves = [(l, half) for l in range(DEPTH) for half in ("mix", "ffn")]
    two_level = {(0, "ffn")}
    names = {"mix": MIX_WEIGHTS, "ffn": FFN_WEIGHTS}
    started = {}
    wx = {n: _exchange_form(n, w[n]) for n in BIG}
    chain = []
    for l, half in halves:
        lands = _place_shards([wx[n] for n in names[half]], l, sel, chain, f"l{l}_{half}")
        plan = _gather_half_plan if (l, half) in two_level else _gather_plan
        started[l, half] = _copies_start(f"weights_l{l}_{half}", plan, [], lands, N_REL * len(lands))
        chain = [started[l, half]["token"]]
    w = dict(w, g_mix=_after(w["g_mix"], started[halves[-1]]["token"]))

    def get_big(l, half, after):
        lands = _copies_wait(started[l, half], after)[1]
        if (l, half) in two_level:
            lands = _forward_halves(lands, f"l{l}_{half}")
        return dict(zip(names[half], lands))

    result = {n: None for n in BIG}
    stage = {"swap": None, "slabs": None, "share": None}

    def advance(after):
        tokens = []
        if stage["share"] is not None:
            sh, ns = stage["share"]
            for n, f in zip(ns, _copies_wait(sh, after)[1]):
                result[n] = f
            stage["share"] = None
        if stage["slabs"] is not None:
            ex, ns, l, tag = stage["slabs"]
            part, slabs = _copies_wait(ex, after)
            bufs = _add_chips(part, slabs, [result[n] for n in ns], l, sel, tag)
            sh = _copies_start(f"share_{tag}", _share_plan(l), [], bufs, len(bufs))
            stage["share"], stage["slabs"] = (sh, ns), None
            tokens.append(sh["token"])
        if stage["swap"] is not None:
            sw, ns, l, tag = stage["swap"]
            part = _add_halves(*_copies_wait(sw, after), sel, tag)
            slabs = [lax.empty((N_REL,) + p.shape[1:], BF16) for p in part]
            ex = _copies_start(f"grads_{tag}", _slab_plan, part, slabs, N_REL * len(part))
            stage["slabs"], stage["swap"] = (ex, ns, l, tag), None
            tokens.append(ex["token"])
        return tokens

    def on_grads(l, half, grads):
        ns = list(grads)
        tag = f"l{l}_{half}"
        tokens = advance([grads[ns[0]]])
        g4s = [grads[n].reshape(N_CHIPS, 2, grads[n].shape[0] // (2 * N_CHIPS), grads[n].shape[1]) for n in ns]
        recvs = [lax.empty((N_CHIPS,) + g4.shape[2:], F32) for g4 in g4s]
        sw = _copies_start(f"swap_{tag}", _sibling_plan, g4s, recvs, len(g4s))
        stage["swap"] = (sw, ns, l, tag)
        return tokens + [sw["token"]]

    small = {}

    def on_small(g, loss_local):
        me = 2 * chip + sel[0]
        blocks = [_pack(g, COARSE).astype(BF16), _pack(g, PRECISE, loss_local)]
        lands = [lax.dynamic_update_slice(lax.empty((N_DEV,) + b.shape, b.dtype), b[None], (me, 0, 0)) for b in blocks]
        small.update(_copies_start("small_grads", _everyone_plan, blocks, lands, (N_DEV - 1) * len(blocks)))
        return [small["token"]] + advance([small["token"]])

    dx = _local_step(x[0], loss_target[0], w, dict(get_big=get_big, on_grads=on_grads, tick=advance, on_small=on_small))
    grads, deltas, new_m, new_v = {}, {}, {}, {}

    def update_big(ns, after):
        for n in ns:
            outs = _adamw(wx[n], result[n], _exchange_form(n, m[n]), _exchange_form(n, v[n]), n, after)
            grads[n], deltas[n], new_m[n], new_v[n] = [_exchange_form(n, a) for a in (result[n], *outs)]
            after = [outs[-1]]
        return after

    sent = advance(update_big(FFN_WEIGHTS[:1], [stage["swap"][0]["token"]]))
    shared = advance(update_big(FFN_WEIGHTS[1:], sent))
    _, gathered = _copies_wait(small, shared)
    coarse, precise = _small_reduce(gathered)
    small_g, _ = _unpack(coarse, w, COARSE)
    precise_g, rest = _unpack(precise, w, PRECISE)
    small_g.update(precise_g)
    loss = rest[0, 0]
    views = [{n: _view(n, src[n]) for n in small_g} for src in (w, m, v)]
    for store, vals in zip((grads, deltas, new_m, new_v), (small_g, *_small_adamw(small_g, *views))):
        store.update({n: _view(n, a) for n, a in vals.items()})
    advance([new_v[SMALL[0]]])
    update_big(MIX_WEIGHTS, [])
    return (loss, dx[None], *[grads[n] for n in WEIGHTS], *[deltas[n] for n in WEIGHTS],
            *[new_m[n] for n in WEIGHTS], *[new_v[n] for n in WEIGHTS])
```

```python
import math

import jax
import jax.numpy as jnp
from jax import lax
from jax.experimental import pallas as pl
from jax.experimental.pallas import tpu as pltpu

F32 = jnp.float32
BF16 = jnp.bfloat16

D_MODEL = 1024
DEPTH = 2
D_SSM = 384
SSM_GROUP = 16
N_GROUPS = 24
SSM_STATE = 64
N_STATE = N_GROUPS * SSM_STATE
POOL_WINDOWS = (2, 4, 8, 16)
POOL_GROUP = 64
D_POOL = 256
MAX_WINDOW = 16
SGU_HEADS = 6
SGU_HEAD_DIM = 64
D_SGU = 384
CHUNK = 128
D_IN = D_SSM + D_POOL + 2 * D_SGU
D_FF = 2816
EPS = 1e-6

ADAM_LR = 0.001
ADAM_B1 = 0.9
ADAM_B2 = 0.999
ADAM_EPS = 1e-08
ADAM_WD = 0.01
ADAM_STEP = 10

LANES = 128
SUBLANES = 8
VMEM_LIMIT = 56 * 1024 * 1024

TS = 512
TS_FFN = 256

WEIGHTS = ['g_mix', 'w_in', 'A_re', 'A_im', 'log_dt', 'B_re', 'B_im', 'C_re', 'C_im', 'D_skip', 'w_glu', 'b_glu',
           'w_pool', 'pool_scale', 'sgu_ln_g', 'sgu_ln_b', 'w_spatial', 'b_spatial', 'w_out', 'g_ffn', 'w_gate',
           'w_up', 'w_down', 'g_final']
BIG = ['w_in', 'w_glu', 'w_out', 'w_gate', 'w_up', 'w_down']
SMALL = [n for n in WEIGHTS if n not in BIG]
TRANSPOSED = ("w_in", "w_gate", "w_up")
N_CHIPS = 4
N_DEV = 8


def _cp(**kw):
    return pltpu.CompilerParams(vmem_limit_bytes=VMEM_LIMIT, **kw)


def _row(ts, n):
    return pl.BlockSpec((ts, n), lambda i: (i, 0))


def _const(shape):
    nd = len(shape)
    return pl.BlockSpec(shape, lambda i: (0,) * nd, pipeline_mode=pl.Buffered(1))


def _acc(shape):
    nd = len(shape)
    return pl.BlockSpec(shape, lambda i: (0,) * nd)


def _following(body, after):
    k = len(after)
    return lambda *refs: body(*refs[k:])


def _anywhere(after):
    return [pl.BlockSpec(memory_space=pl.ANY)] * len(after)


def _dot(a, b):
    return jnp.dot(a, b, preferred_element_type=F32)


def _dot_tn(a, b):
    return lax.dot_general(a, b, (((0,), (0,)), ((), ())), preferred_element_type=F32)


def _dot_nt(a, b):
    return lax.dot_general(a, b, (((1,), (1,)), ((), ())), preferred_element_type=F32)


_G0 = math.sqrt(2.0 / math.pi)
_G1 = 0.044715


def _gelu(x):
    return 0.5 * x * (1.0 + jnp.tanh(_G0 * (x + _G1 * x * x * x)))


def _gelu_and_grad(x):
    t = jnp.tanh(_G0 * (x + _G1 * x * x * x))
    half = 0.5 * (1.0 + t)
    return x * half, half + 0.5 * x * (1.0 - t * t) * (_G0 * (1.0 + 3.0 * _G1 * x * x))


def _sigmoid(x):
    return 1.0 / (1.0 + jnp.exp(-x))


def _rms(x):
    r = lax.rsqrt(jnp.mean(x * x, axis=-1, keepdims=True) + EPS)
    return x * r, r


def _rms_bwd(dh, n, r, g):
    dn = dh * g
    return r * (dn - n * jnp.mean(dn * n, axis=-1, keepdims=True)), dh * n


def _colsum8(v):
    rows, n = v.shape
    return jnp.sum(v.reshape(rows // SUBLANES, SUBLANES, n), axis=0)


def _cmul(ar, ai, br, bi):
    return ar * br - ai * bi, ar * bi + ai * br


def _cpow(ar, ai, n):
    assert n & (n - 1) == 0
    while n > 1:
        ar, ai = _cmul(ar, ai, ar, ai)
        n //= 2
    return ar, ai


N_USLAB = D_SSM // LANES
SEG = TS // SUBLANES
SLAB_STATES = N_STATE // N_USLAB
S5_IN = (N_USLAB, LANES, 2 * SLAB_STATES)
S5_OUT = (N_USLAB, 2 * SLAB_STATES, LANES)
STATE_TILE = (SUBLANES, N_STATE)


def _scan_order():
    p = jnp.arange(TS)
    src = (p % SUBLANES) * SEG + p // SUBLANES
    return (src[:, None] == jnp.arange(TS)[None, :]).astype(BF16)


def _to_scan_order(perm, v):
    hi = v.astype(BF16)
    lo = (v - hi.astype(F32)).astype(BF16)
    return _dot(perm, hi) + _dot(perm, lo)


def _scan_rows(k):
    return pl.ds(pl.multiple_of(k * SUBLANES, SUBLANES), SUBLANES)


def _lanes(v, j):
    return v[:, LANES * j:LANES * (j + 1)]


def _states(j):
    return pl.ds(SLAB_STATES * j, SLAB_STATES)


def _state_split(re_ref, im_ref, j, v):
    re_ref[:, _states(j)] = v[:, :SLAB_STATES]
    im_ref[:, _states(j)] = v[:, SLAB_STATES:]


def _state_cat(re_ref, im_ref, j):
    return jnp.concatenate([re_ref[:, _states(j)], im_ref[:, _states(j)]], axis=1).astype(BF16)


def _s5_fwd(u, p, tag):
    s = u.shape[0]
    seg = SEG

    def body(u_ref, perm_ref, bbt_ref, ar_ref, ai_ref, ct_ref, dsk_ref, wglu_ref, bglu_ref,
             oa_ref, y_ref, hr_ref, hi_ref, sr, si, er, ei, ir, ii, cr, ci):
        @pl.when(pl.program_id(0) == 0)
        def _():
            cr[...] = jnp.zeros_like(cr)
            ci[...] = jnp.zeros_like(ci)

        perm = perm_ref[...]
        uv = _to_scan_order(perm, u_ref[...])
        ub = uv.astype(BF16)
        for j in range(N_USLAB):
            _state_split(sr, si, j, _dot(_lanes(ub, j), bbt_ref[j]))
        for j in range(N_USLAB):
            cols = _states(j)
            ar = ar_ref[:, cols]
            ai = ai_ref[:, cols]
            h_r = h_i = jnp.zeros((SUBLANES, SLAB_STATES), F32)
            for k in range(seg):
                rows = pl.ds(SUBLANES * k, SUBLANES)
                n_r, n_i = _cmul(ar, ai, h_r, h_i)
                h_r = n_r + sr[rows, cols]
                h_i = n_i + si[rows, cols]
            er[:, cols] = h_r
            ei[:, cols] = h_i
            pr, pi = _cpow(ar[0:1, :], ai[0:1, :], seg)
            c_r = cr[:, cols]
            c_i = ci[:, cols]
            for q in range(SUBLANES):
                ir[q:q + 1, cols] = c_r
                ii[q:q + 1, cols] = c_i
                n_r, n_i = _cmul(pr, pi, c_r, c_i)
                c_r = n_r + er[q:q + 1, cols]
                c_i = n_i + ei[q:q + 1, cols]
            cr[:, cols] = c_r
            ci[:, cols] = c_i
            h_r = ir[:, cols]
            h_i = ii[:, cols]
            for k in range(seg):
                rows = pl.ds(SUBLANES * k, SUBLANES)
                n_r, n_i = _cmul(ar, ai, h_r, h_i)
                h_r = n_r + sr[rows, cols]
                h_i = n_i + si[rows, cols]
                sr[rows, cols] = h_r
                si[rows, cols] = h_i
        hr_ref[...] = sr[...].astype(BF16)
        hi_ref[...] = si[...].astype(BF16)
        y = jnp.concatenate([_dot(_state_cat(hr_ref, hi_ref, j), ct_ref[j]) for j in range(N_USLAB)], axis=1)
        y = y + dsk_ref[...] * uv
        y_ref[...] = y
        g = _gelu(y)
        pre = _dot(g.astype(BF16), wglu_ref[...]) + bglu_ref[...]
        oa_ref[...] = _dot_tn(perm, (g * _sigmoid(pre)).astype(BF16)).astype(BF16)

    return pl.pallas_call(
        body, grid=(s // TS,),
        in_specs=[_row(TS, D_SSM), _const((TS, TS)), _const(S5_IN), _const(STATE_TILE), _const(STATE_TILE),
                  _const(S5_OUT), _const((1, D_SSM)), _const((D_SSM, D_SSM)), _const((1, D_SSM))],
        out_specs=[_row(TS, D_SSM), _row(TS, D_SSM), _row(TS, N_STATE), _row(TS, N_STATE)],
        out_shape=[jax.ShapeDtypeStruct((s, D_SSM), BF16), jax.ShapeDtypeStruct((s, D_SSM), F32),
                   jax.ShapeDtypeStruct((s, N_STATE), BF16), jax.ShapeDtypeStruct((s, N_STATE), BF16)],
        scratch_shapes=[pltpu.VMEM((TS, N_STATE), F32), pltpu.VMEM((TS, N_STATE), F32),
                        pltpu.VMEM(STATE_TILE, F32), pltpu.VMEM(STATE_TILE, F32), pltpu.VMEM(STATE_TILE, F32),
                        pltpu.VMEM(STATE_TILE, F32), pltpu.VMEM((1, N_STATE), F32), pltpu.VMEM((1, N_STATE), F32)],
        name=f"s5_fwd_{tag}", compiler_params=_cp(dimension_semantics=("arbitrary",)),
    )(u, _scan_order(), p["bbt3"], p["a_re8"], p["a_im8"], p["ct3"], p["d_skip"], p["w_glu"], p["b_glu"])


def _pool_consts():
    w = jnp.repeat(jnp.asarray(POOL_WINDOWS, F32), POOL_GROUP)[None, :]
    return w


POOL_PAD = SUBLANES
POOL_ROWS = TS + MAX_WINDOW + POOL_PAD


def _window_sum(buf, tmp, first, wl, step):
    assert POOL_WINDOWS == (2, 4, 8, 16)
    n = TS + MAX_WINDOW
    lo = first - MAX_WINDOW if step < 0 else first
    src = buf
    for k, dst in zip((1, 2, 4), tmp):
        dst[pl.ds(lo, n), :] = src[pl.ds(lo, n), :] + src[pl.ds(lo + step * k, n), :]
        src = dst
    s2, s4, s8 = (t[pl.ds(first, TS), :] for t in tmp)
    s16 = s8 + tmp[2][pl.ds(first + step * 8, TS), :]
    return jnp.where(wl == 2, s2, jnp.where(wl == 4, s4, jnp.where(wl == 8, s8, s16)))


def _pool_count(i, rows, wl):
    t = (i * TS + 1).astype(F32) + lax.broadcasted_iota(jnp.int32, (rows, 1), 0).astype(F32)
    return jnp.minimum(t, wl)


def _sgu_mix(vl, wpair_ref, lo, hi):
    rows = vl.shape[0]
    chunks = []
    for c in range(rows // CHUNK):
        vc = vl[CHUNK * c:CHUNK * (c + 1), :]
        parts = []
        for q in range(SGU_HEADS // 2):
            vq = vc[:, LANES * q:LANES * (q + 1)]
            rhs = jnp.concatenate([vq * lo, vq * hi], axis=0).astype(BF16)
            parts.append(_dot(wpair_ref[q], rhs))
        chunks.append(jnp.concatenate(parts, axis=1))
    return jnp.concatenate(chunks, axis=0)


def _sgu_front(zuv, lng, lnb, grads=False):
    gelu = _gelu_and_grad if grads else lambda z: (_gelu(z), None)
    u, du = gelu(zuv[:, :D_SGU])
    v, dv = gelu(zuv[:, D_SGU:])
    mu = jnp.mean(v, axis=-1, keepdims=True)
    vc = v - mu
    rs = lax.rsqrt(jnp.mean(vc * vc, axis=-1, keepdims=True) + EPS)
    vn = vc * rs
    return u, vn, rs, vn * lng + lnb, du, dv


def _half_masks():
    lane = lax.broadcasted_iota(jnp.int32, (1, LANES), 1)
    lo = (lane < SGU_HEAD_DIM).astype(F32)
    return lo, 1.0 - lo


def _mix_fwd(x, p, tag):
    s = x.shape[0]

    def body(x_ref, g_ref, w_ref, wl_ref, wp_ref, sc_ref, lng_ref, lnb_ref, wsp_ref, bias_ref,
             za_ref, zuv_ref, h_ref, ob_ref, pooled_ref, oc_ref, buf, *tmp):
        i = pl.program_id(0)
        tile0 = POOL_PAD + MAX_WINDOW

        @pl.when(i == 0)
        def _():
            for ref in (buf, *tmp):
                ref[pl.ds(0, tile0), :] = jnp.zeros((tile0, D_POOL), F32)

        n, _ = _rms(x_ref[...])
        h = (n * g_ref[...]).astype(BF16)
        h_ref[...] = h
        z = _dot_nt(h, w_ref[...])
        za_ref[...] = z[:, :D_SSM]
        zb = z[:, D_SSM:D_SSM + D_POOL]
        zuv = z[:, D_SSM + D_POOL:]
        zuv_ref[...] = zuv
        buf[pl.ds(tile0, TS), :] = zb
        wl = wl_ref[...]
        pooled = (_window_sum(buf, tmp, tile0, wl, -1) / _pool_count(i, TS, wl) - zb).astype(BF16)
        buf[pl.ds(POOL_PAD, MAX_WINDOW), :] = zb[TS - MAX_WINDOW:, :]
        pooled_ref[...] = pooled
        ob_ref[...] = (_dot(pooled, wp_ref[...]) * sc_ref[...]).astype(BF16)
        lo, hi = _half_masks()
        u, _, _, vl, _, _ = _sgu_front(zuv, lng_ref[...], lnb_ref[...])
        mixed = _sgu_mix(vl, wsp_ref, lo, hi) + jnp.tile(bias_ref[...], (TS // CHUNK, 1))
        oc_ref[...] = (u * mixed).astype(BF16)

    return pl.pallas_call(
        body, grid=(s // TS,),
        in_specs=[_row(TS, D_MODEL), _const((1, D_MODEL)), _const((D_IN, D_MODEL)), _const((1, D_POOL)),
                  _const((D_POOL, D_POOL)), _const((1, D_POOL)), _const((1, D_SGU)), _const((1, D_SGU)),
                  _const((SGU_HEADS // 2, CHUNK, 2 * CHUNK)), _const((CHUNK, D_SGU))],
        out_specs=[_row(TS, D_SSM), _row(TS, 2 * D_SGU), _row(TS, D_MODEL), _row(TS, D_POOL), _row(TS, D_POOL),
                   _row(TS, D_SGU)],
        out_shape=[jax.ShapeDtypeStruct((s, D_SSM), F32), jax.ShapeDtypeStruct((s, 2 * D_SGU), F32),
                   jax.ShapeDtypeStruct((s, D_MODEL), BF16), jax.ShapeDtypeStruct((s, D_POOL), BF16),
                   jax.ShapeDtypeStruct((s, D_POOL), BF16), jax.ShapeDtypeStruct((s, D_SGU), BF16)],
        scratch_shapes=[pltpu.VMEM((POOL_ROWS, D_POOL), F32)] * 4,
        name=f"mix_fwd_{tag}", compiler_params=_cp(dimension_semantics=("arbitrary",)),
    )(x, p["g_mix"], p["w_in"], _pool_consts(), p["w_pool_bd"], p["pool_scale"], p["sgu_ln_g"], p["sgu_ln_b"],
      p["ws_pair"], p["bias_sp"])


def _blk_fwd(x0, oa, ob, oc, p, tag, head=None):
    s = x0.shape[0]
    ts = TS_FFN
    n_head = 0 if head is None else len(head)

    def body(x0_ref, oa_ref, ob_ref, oc_ref, wo_ref, g_ref, wg_ref, wu_ref, wd_ref, *refs):
        x1_ref, x2_ref, h2_ref, gt_ref, up_ref, ycat_ref = refs[n_head:n_head + 6]
        ycat = jnp.concatenate([oa_ref[...], ob_ref[...], oc_ref[...]], axis=1)
        ycat_ref[...] = ycat
        x1 = x0_ref[...] + _dot(ycat, wo_ref[...])
        x1_ref[...] = x1
        n, _ = _rms(x1)
        h2 = (n * g_ref[...]).astype(BF16)
        h2_ref[...] = h2
        gt = _dot_nt(h2, wg_ref[...])
        up = _dot_nt(h2, wu_ref[...])
        gt_ref[...] = gt.astype(BF16)
        up_ref[...] = up.astype(BF16)
        act = (gt * _sigmoid(gt) * up).astype(BF16)
        x2 = x1 + _dot(act, wd_ref[...])
        if head is None:
            x2_ref[...] = x2
            return
        t_ref, gf_ref = refs[:n_head]
        loss_ref, dgf_ref = refs[n_head + 6:]

        @pl.when(pl.program_id(0) == 0)
        def _():
            loss_ref[...] = jnp.zeros_like(loss_ref)
            dgf_ref[...] = jnp.zeros_like(dgf_ref)

        gf = gf_ref[...]
        nf, rf = _rms(x2)
        diff = nf * gf - t_ref[...]
        loss_ref[...] += jnp.sum(diff * diff) * (0.5 / D_MODEL)
        dxn, dgp = _rms_bwd(diff * (1.0 / D_MODEL), nf, rf, gf)
        dgf_ref[...] += _colsum8(dgp)
        x2_ref[...] = dxn

    in_specs = [_row(ts, D_MODEL), _row(ts, D_SSM), _row(ts, D_POOL), _row(ts, D_SGU), _const((D_MODEL, D_MODEL)),
                _const((1, D_MODEL)), _const((D_FF, D_MODEL)), _const((D_FF, D_MODEL)), _const((D_FF, D_MODEL))]
    out_specs = [_row(ts, D_MODEL), _row(ts, D_MODEL), _row(ts, D_MODEL), _row(ts, D_FF), _row(ts, D_FF),
                 _row(ts, D_MODEL)]
    out_shape = [jax.ShapeDtypeStruct((s, D_MODEL), F32), jax.ShapeDtypeStruct((s, D_MODEL), F32),
                 jax.ShapeDtypeStruct((s, D_MODEL), BF16), jax.ShapeDtypeStruct((s, D_FF), BF16),
                 jax.ShapeDtypeStruct((s, D_FF), BF16), jax.ShapeDtypeStruct((s, D_MODEL), BF16)]
    args = (x0, oa, ob, oc, p["w_out"], p["g_ffn"], p["w_gate"], p["w_up"], p["w_down"])
    if head is not None:
        in_specs += [_row(ts, D_MODEL), _const((1, D_MODEL))]
        out_specs += [_acc((SUBLANES, LANES)), _acc((SUBLANES, D_MODEL))]
        out_shape += [jax.ShapeDtypeStruct((SUBLANES, LANES), F32), jax.ShapeDtypeStruct((SUBLANES, D_MODEL), F32)]
        args += tuple(head)
    return pl.pallas_call(
        body, grid=(s // ts,), in_specs=in_specs, out_specs=out_specs, out_shape=out_shape,
        name=f"blk_fwd_{tag}", compiler_params=_cp(dimension_semantics=("arbitrary",)),
    )(*args)


def _blk_bwd(dx2, x1, gt, up, p, tag, after=()):
    s = dx2.shape[0]
    ts = TS_FFN

    def body(dx2_ref, x1_ref, gt_ref, up_ref, wd_ref, wgt_ref, wut_ref, wo_ref, g_ref,
             dx1_ref, da_ref, db_ref, dc_ref, dgt_ref, dup_ref, act_ref, dg_ref):
        @pl.when(pl.program_id(0) == 0)
        def _():
            dg_ref[...] = jnp.zeros_like(dg_ref)

        dx2v = dx2_ref[...]
        dact = _dot_nt(dx2v.astype(BF16), wd_ref[...])
        gf = gt_ref[...].astype(F32)
        uf = up_ref[...].astype(F32)
        sg = _sigmoid(gf)
        sl = gf * sg
        act_ref[...] = (sl * uf).astype(BF16)
        dgt = (dact * uf * (sg * (1.0 + gf * (1.0 - sg)))).astype(BF16)
        dup = (dact * sl).astype(BF16)
        dgt_ref[...] = dgt
        dup_ref[...] = dup
        dh2 = _dot(dgt, wgt_ref[...]) + _dot(dup, wut_ref[...])
        n, r = _rms(x1_ref[...])
        dxn, dgp = _rms_bwd(dh2, n, r, g_ref[...])
        dg_ref[...] += _colsum8(dgp)
        dx1 = dx2v + dxn
        dx1_ref[...] = dx1
        dy = _dot_nt(dx1.astype(BF16), wo_ref[...])
        da_ref[...] = dy[:, :D_SSM]
        db_ref[...] = dy[:, D_SSM:D_SSM + D_POOL]
        dc_ref[...] = dy[:, D_SSM + D_POOL:]

    return pl.pallas_call(
        _following(body, after), grid=(s // ts,),
        in_specs=_anywhere(after) + [_row(ts, D_MODEL), _row(ts, D_MODEL), _row(ts, D_FF), _row(ts, D_FF),
                  _const((D_FF, D_MODEL)), _const((D_FF, D_MODEL)), _const((D_FF, D_MODEL)),
                  _const((D_MODEL, D_MODEL)), _const((1, D_MODEL))],
        out_specs=[_row(ts, D_MODEL), _row(ts, D_SSM), _row(ts, D_POOL), _row(ts, D_SGU), _row(ts, D_FF),
                   _row(ts, D_FF), _row(ts, D_FF), _acc((SUBLANES, D_MODEL))],
        out_shape=[jax.ShapeDtypeStruct((s, D_MODEL), F32), jax.ShapeDtypeStruct((s, D_SSM), F32),
                   jax.ShapeDtypeStruct((s, D_POOL), F32), jax.ShapeDtypeStruct((s, D_SGU), F32),
                   jax.ShapeDtypeStruct((s, D_FF), BF16), jax.ShapeDtypeStruct((s, D_FF), BF16),
                   jax.ShapeDtypeStruct((s, D_FF), BF16), jax.ShapeDtypeStruct((SUBLANES, D_MODEL), F32)],
        name=f"blk_bwd_{tag}", compiler_params=_cp(dimension_semantics=("arbitrary",)),
    )(*after, dx2, x1, gt, up, p["w_down"], p["w_gate"], p["w_up"], p["w_out"], p["g_ffn"])


S5_OWN = (N_USLAB, LANES, LANES)
assert 2 * SSM_STATE == LANES


def _own_blocks(acc_ref, j):
    pairs = SLAB_STATES // LANES
    group = lax.broadcasted_iota(jnp.int32, (LANES, LANES), 0) // (LANES // (2 * pairs))
    lane = lax.broadcasted_iota(jnp.int32, (LANES, LANES), 1)
    parts = []
    for half in range(2):
        own = jnp.zeros((LANES, LANES), F32)
        for q in range(pairs):
            own = jnp.where(group // 2 == q, acc_ref[j, :, pl.ds(half * SLAB_STATES + q * LANES, LANES)], own)
        parts.append(jnp.where(group % 2 == 0, own, pltpu.roll(own, SSM_STATE, 1)))
    return jnp.where(lane < SSM_STATE, parts[0], pltpu.roll(parts[1], SSM_STATE, 1))


def _own_blocks_split(d):
    d = d.reshape(N_GROUPS, D_SSM // N_GROUPS, 2, SSM_STATE)
    return d[:, :, 0], d[:, :, 1]


def _s5_bwd(dout, u, y, h_re, h_im, p, tag, after=()):
    s = u.shape[0]
    nt = s // TS
    seg = SEG

    def rev(n):
        return pl.BlockSpec((TS, n), lambda i: (nt - 1 - i, 0))

    def body(do_ref, u_ref, y_ref, hr_ref, hi_ref, perm_ref, ar_ref, ai_ref, cb_ref, bb_ref, dsk_ref,
             wglu_ref, bglu_ref,
             du_ref, dct_own_ref, dbb_own_ref, dar_ref, dai_ref, dd_ref, dwglu_ref, dbglu_ref,
             gr, gi, hsr, hsi, er, ei, jr, ji, cr, ci, dct_ref, dbb_ref):
        @pl.when(pl.program_id(0) == 0)
        def _():
            for ref in (cr, ci, dct_ref, dbb_ref, dar_ref, dai_ref, dd_ref, dwglu_ref, dbglu_ref):
                ref[...] = jnp.zeros_like(ref)

        perm = perm_ref[...]
        uv = _to_scan_order(perm, u_ref[...])
        yv = y_ref[...]
        dov = _to_scan_order(perm, do_ref[...])
        g, gelu_dy = _gelu_and_grad(yv)
        gb = g.astype(BF16)
        sg = _sigmoid(_dot(gb, wglu_ref[...]) + bglu_ref[...])
        dpre = dov * g * sg * (1.0 - sg)
        dpb = dpre.astype(BF16)
        dwglu_ref[...] += _dot_tn(gb, dpb)
        dbglu_ref[...] += _colsum8(dpre)
        dy = (dov * sg + _dot_nt(dpb, wglu_ref[...])) * gelu_dy
        dd_ref[...] += _colsum8(dy * uv)
        dyb = dy.astype(BF16)
        hsr[...] = hr_ref[...].astype(F32)
        hsi[...] = hi_ref[...].astype(F32)
        ub = uv.astype(BF16)
        dus = []

        def state_cotangents(j):
            dct_ref[j] += _dot_tn(_lanes(dyb, j), _state_cat(hr_ref, hi_ref, j))
            _state_split(gr, gi, j, _dot(_lanes(dyb, j), cb_ref[j]))

        def input_cotangents(j):
            gb_j = _state_cat(gr, gi, j)
            dbb_ref[j] += _dot_tn(_lanes(ub, j), gb_j)
            dus.append(_dot(gb_j, bb_ref[j]))

        def scan(j):
            cols = _states(j)
            ar = ar_ref[:, cols]
            ai = -ai_ref[:, cols]
            g_r = g_i = jnp.zeros((SUBLANES, SLAB_STATES), F32)
            for k in range(seg - 1, -1, -1):
                rows = pl.ds(SUBLANES * k, SUBLANES)
                n_r, n_i = _cmul(ar, ai, g_r, g_i)
                g_r = n_r + gr[rows, cols]
                g_i = n_i + gi[rows, cols]
            er[:, cols] = g_r
            ei[:, cols] = g_i
            pr, pi = _cpow(ar[0:1, :], ai[0:1, :], seg)
            c_r = cr[:, cols]
            c_i = ci[:, cols]
            for q in range(SUBLANES - 1, -1, -1):
                jr[q:q + 1, cols] = c_r
                ji[q:q + 1, cols] = c_i
                n_r, n_i = _cmul(pr, pi, c_r, c_i)
                c_r = n_r + er[q:q + 1, cols]
                c_i = n_i + ei[q:q + 1, cols]
            cr[:, cols] = c_r
            ci[:, cols] = c_i
            g_r = jr[:, cols]
            g_i = ji[:, cols]
            a_r = a_i = jnp.zeros((SUBLANES, SLAB_STATES), F32)
            for k in range(seg - 1, -1, -1):
                rows = pl.ds(SUBLANES * k, SUBLANES)
                h_r = hsr[rows, cols]
                h_i = hsi[rows, cols]
                a_r = a_r + g_r * h_r + g_i * h_i
                a_i = a_i + g_i * h_r - g_r * h_i
                n_r, n_i = _cmul(ar, ai, g_r, g_i)
                g_r = n_r + gr[rows, cols]
                g_i = n_i + gi[rows, cols]
                gr[rows, cols] = g_r
                gi[rows, cols] = g_i
            dar_ref[:, cols] += a_r
            dai_ref[:, cols] += a_i

        for stage in (state_cotangents, scan, input_cotangents):
            for j in range(N_USLAB):
                stage(j)
        du = dy * dsk_ref[...] + jnp.concatenate(dus, axis=1)
        du_ref[...] = _dot_tn(perm, du.astype(BF16)).astype(BF16)

        @pl.when(pl.program_id(0) == nt - 1)
        def _():
            for j in range(N_USLAB):
                dct_own_ref[j] = _own_blocks(dct_ref, j)
                dbb_own_ref[j] = _own_blocks(dbb_ref, j)

    big = (TS, N_STATE)
    return pl.pallas_call(
        _following(body, after), grid=(nt,),
        in_specs=_anywhere(after) + [rev(D_SSM), rev(D_SSM), rev(D_SSM), rev(N_STATE), rev(N_STATE), _const((TS, TS)),
                  _const(STATE_TILE), _const(STATE_TILE), _const(S5_IN), _const(S5_OUT), _const((1, D_SSM)),
                  _const((D_SSM, D_SSM)), _const((1, D_SSM))],
        out_specs=[rev(D_SSM), _acc(S5_OWN), _acc(S5_OWN), _acc(STATE_TILE), _acc(STATE_TILE),
                   _acc((SUBLANES, D_SSM)), _acc((D_SSM, D_SSM)), _acc((SUBLANES, D_SSM))],
        out_shape=[jax.ShapeDtypeStruct((s, D_SSM), BF16), jax.ShapeDtypeStruct(S5_OWN, F32),
                   jax.ShapeDtypeStruct(S5_OWN, F32), jax.ShapeDtypeStruct(STATE_TILE, F32),
                   jax.ShapeDtypeStruct(STATE_TILE, F32), jax.ShapeDtypeStruct((SUBLANES, D_SSM), F32),
                   jax.ShapeDtypeStruct((D_SSM, D_SSM), F32), jax.ShapeDtypeStruct((SUBLANES, D_SSM), F32)],
        scratch_shapes=[pltpu.VMEM(big, F32), pltpu.VMEM(big, F32), pltpu.VMEM(big, F32), pltpu.VMEM(big, F32),
                        pltpu.VMEM(STATE_TILE, F32), pltpu.VMEM(STATE_TILE, F32), pltpu.VMEM(STATE_TILE, F32),
                        pltpu.VMEM(STATE_TILE, F32), pltpu.VMEM((1, N_STATE), F32), pltpu.VMEM((1, N_STATE), F32),
                        pltpu.VMEM(S5_IN, F32), pltpu.VMEM(S5_IN, F32)],
        name=f"s5_bwd_{tag}", compiler_params=_cp(dimension_semantics=("arbitrary",)),
    )(*after, dout, u, y, h_re, h_im, _scan_order(), p["a_re8"], p["a_im8"], p["cb3"], p["bb3"], p["d_skip"], p["w_glu"],
      p["b_glu"])


def _mix_bwd(dza, db, dc, pooled, zuv, x0, dx1, p, tag, after=()):
    s = x0.shape[0]
    nt = s // TS

    def rev(n):
        return pl.BlockSpec((TS, n), lambda i: (nt - 1 - i, 0))

    def body(da_ref, db_ref, dc_ref, po_ref, z_ref, x_ref, dx1_ref, wl_ref, wp_ref, wpt_ref, sc_ref, lng_ref, lnb_ref,
             wsp_ref, wspt_ref, bias_ref, win_ref, g_ref,
             dx0_ref, dz_ref, dwp_ref, dsc_ref, dws_ref, dbias_ref, dlng_ref, dlnb_ref, dg_ref, buf, *tmp):
        step = pl.program_id(0)
        i = nt - 1 - step

        @pl.when(step == 0)
        def _():
            for ref in (dwp_ref, dsc_ref, dws_ref, dbias_ref, dlng_ref, dlnb_ref, dg_ref):
                ref[...] = jnp.zeros_like(ref)
            for ref in (buf, *tmp):
                ref[pl.ds(TS, POOL_ROWS - TS), :] = jnp.zeros((POOL_ROWS - TS, D_POOL), F32)

        wl = wl_ref[...]
        sc = sc_ref[...]
        dob = db_ref[...]
        pooled_b = po_ref[...]
        dsc_ref[...] += _colsum8(dob * _dot(pooled_b, wp_ref[...]))
        dmixb = (dob * sc).astype(BF16)
        dwp_ref[...] += _dot_tn(pooled_b, dmixb)
        dpool = _dot(dmixb, wpt_ref[...])
        dq = dpool / _pool_count(i, TS, wl)
        buf[pl.ds(0, TS), :] = dq
        dzb = _window_sum(buf, tmp, 0, wl, 1) - dpool
        buf[pl.ds(TS, MAX_WINDOW), :] = dq[:MAX_WINDOW, :]

        lo, hi = _half_masks()
        lng = lng_ref[...]
        u, vn, rs, vl, gelu_du, gelu_dv = _sgu_front(z_ref[...], lng, lnb_ref[...], grads=True)
        mixed = _sgu_mix(vl, wsp_ref, lo, hi) + jnp.tile(bias_ref[...], (TS // CHUNK, 1))
        doc = dc_ref[...]
        dzu = doc * mixed * gelu_du
        dmix = doc * u
        dbias = dbias_ref[...]
        for c in range(TS // CHUNK):
            dmc = dmix[CHUNK * c:CHUNK * (c + 1), :]
            dbias = dbias + dmc
            vlc = vl[CHUNK * c:CHUNK * (c + 1), :].astype(BF16)
            for q in range(SGU_HEADS // 2):
                dmq = _lanes(dmc, q)
                vq = _lanes(vlc, q)
                dws_ref[2 * q] += _dot_nt((dmq * lo).astype(BF16), vq)
                dws_ref[2 * q + 1] += _dot_nt((dmq * hi).astype(BF16), vq)
        dbias_ref[...] = dbias
        dvl = _sgu_mix(dmix, wspt_ref, lo, hi)
        dlng_ref[...] += _colsum8(dvl * vn)
        dlnb_ref[...] += _colsum8(dvl)
        dvn = dvl * lng
        dv = rs * (dvn - jnp.mean(dvn, axis=-1, keepdims=True) - vn * jnp.mean(dvn * vn, axis=-1, keepdims=True))

        dz = jnp.concatenate([da_ref[...], dzb.astype(BF16), dzu.astype(BF16), (dv * gelu_dv).astype(BF16)], axis=1)
        dz_ref[...] = dz
        n, r = _rms(x_ref[...])
        dxn, dgp = _rms_bwd(_dot(dz, win_ref[...]), n, r, g_ref[...])
        dg_ref[...] += _colsum8(dgp)
        dx0_ref[...] = dx1_ref[...] + dxn

    pair = (SGU_HEADS // 2, CHUNK, 2 * CHUNK)
    return pl.pallas_call(
        _following(body, after), grid=(nt,),
        in_specs=_anywhere(after) + [rev(D_SSM), rev(D_POOL), rev(D_SGU), rev(D_POOL), rev(2 * D_SGU), rev(D_MODEL), rev(D_MODEL),
                  _const((1, D_POOL)), _const((D_POOL, D_POOL)), _const((D_POOL, D_POOL)), _const((1, D_POOL)),
                  _const((1, D_SGU)), _const((1, D_SGU)), _const(pair), _const(pair), _const((CHUNK, D_SGU)),
                  _const((D_IN, D_MODEL)), _const((1, D_MODEL))],
        out_specs=[rev(D_MODEL), rev(D_IN), _acc((D_POOL, D_POOL)), _acc((SUBLANES, D_POOL)),
                   _acc((SGU_HEADS, CHUNK, CHUNK)), _acc((CHUNK, D_SGU)), _acc((SUBLANES, D_SGU)),
                   _acc((SUBLANES, D_SGU)), _acc((SUBLANES, D_MODEL))],
        out_shape=[jax.ShapeDtypeStruct((s, D_MODEL), F32), jax.ShapeDtypeStruct((s, D_IN), BF16),
                   jax.ShapeDtypeStruct((D_POOL, D_POOL), F32), jax.ShapeDtypeStruct((SUBLANES, D_POOL), F32),
                   jax.ShapeDtypeStruct((SGU_HEADS, CHUNK, CHUNK), F32), jax.ShapeDtypeStruct((CHUNK, D_SGU), F32),
                   jax.ShapeDtypeStruct((SUBLANES, D_SGU), F32), jax.ShapeDtypeStruct((SUBLANES, D_SGU), F32),
                   jax.ShapeDtypeStruct((SUBLANES, D_MODEL), F32)],
        scratch_shapes=[pltpu.VMEM((POOL_ROWS, D_POOL), F32)] * 4,
        name=f"mix_bwd_{tag}", compiler_params=_cp(dimension_semantics=("arbitrary",)),
    )(*after, dza, db, dc, pooled, zuv, x0, dx1, _pool_consts(), p["w_pool_bd"], p["w_pool_bd_t"], p["pool_scale"],
      p["sgu_ln_g"], p["sgu_ln_b"], p["ws_pair"], p["ws_pair_t"], p["bias_sp"], p["w_in"], p["g_mix"])


def _atb(a, b, tag, after=()):
    s, ka = a.shape
    kb = b.shape[1]
    ts = ATB_ROWS
    tn = min(kb, ATB_COLS)
    ns = s // ts

    def body(a_ref, b_ref, *rest):
        o_ref = rest[-1]

        @pl.when(pl.program_id(1) == 0)
        def _():
            o_ref[...] = jnp.zeros_like(o_ref)

        o_ref[...] += _dot_tn(a_ref[...].astype(BF16), b_ref[...].astype(BF16))

    return pl.pallas_call(
        body, grid=(kb // tn, ns),
        in_specs=[pl.BlockSpec((ts, ka), lambda j, i: (i, 0)), pl.BlockSpec((ts, tn), lambda j, i: (i, j))]
        + [pl.BlockSpec(memory_space=pl.ANY)] * len(after),
        out_specs=pl.BlockSpec((ka, tn), lambda j, i: (0, j)),
        out_shape=jax.ShapeDtypeStruct((ka, kb), F32),
        name=f"atb_{tag}", compiler_params=_cp(dimension_semantics=("arbitrary", "arbitrary")),
    )(a, b, *after)


def _s5_discretise(a_re, a_im, log_dt, b_re, b_im):
    dt = jnp.exp(log_dt)[:, None]
    mag = jnp.exp(a_re * dt)
    ar = mag * jnp.cos(a_im * dt)
    ai = mag * jnp.sin(a_im * dt)
    den = a_re * a_re + a_im * a_im
    f_re = ((ar - 1.0) * a_re + ai * a_im) / den
    f_im = (ai * a_re - (ar - 1.0) * a_im) / den
    bb_re = f_re[..., None] * b_re - f_im[..., None] * b_im
    bb_im = f_re[..., None] * b_im + f_im[..., None] * b_re
    return ar, ai, bb_re, bb_im


def _block_diag(blocks):
    g, r, c = blocks.shape
    eye = jnp.eye(g, dtype=blocks.dtype)
    return (blocks[:, :, None, :] * eye[:, None, :, None]).reshape(g * r, g * c)


def _block_diag_extract(m, g):
    r = m.shape[0] // g
    c = m.shape[1] // g
    eye = jnp.eye(g, dtype=m.dtype)
    return jnp.sum(m.reshape(g, r, g, c) * eye[:, None, :, None], axis=2)


GROUPS_PER_SLAB = N_GROUPS // N_USLAB


def _slab_diag(blocks):
    k = GROUPS_PER_SLAB
    _, r, c = blocks.shape
    eye = jnp.eye(k, dtype=blocks.dtype)
    spread = blocks.reshape(N_USLAB, k, r, 1, c) * eye[None, :, None, :, None]
    return spread.reshape(N_USLAB, k * r, k * c)


def _state_slabs(v):
    return jnp.broadcast_to(v.reshape(1, N_STATE), STATE_TILE)


def _tril():
    return jnp.tril(jnp.ones((CHUNK, CHUNK), dtype=bool))


def _layer_params(w, l):
    row = lambda v: v.reshape(1, -1)
    t = lambda m: jnp.swapaxes(m, -1, -2)
    ar, ai, bb_re, bb_im = _s5_discretise(w["A_re"][l], w["A_im"][l], w["log_dt"][l], w["B_re"][l], w["B_im"][l])
    bbt3 = jnp.concatenate([_slab_diag(t(bb_re)), _slab_diag(t(bb_im))], axis=2).astype(BF16)
    ct3 = jnp.concatenate([_slab_diag(t(w["C_re"][l])), -_slab_diag(t(w["C_im"][l]))], axis=1).astype(BF16)
    ws = jnp.where(_tril()[None], w["w_spatial"][l], 0.0)
    pair = lambda m: jnp.stack([jnp.concatenate([m[2 * q], m[2 * q + 1]], axis=1)
                                for q in range(SGU_HEADS // 2)]).astype(BF16)
    wp = _block_diag(w["w_pool"][l]).astype(BF16)
    p = dict(
        g_mix=row(w["g_mix"][l]), g_ffn=row(w["g_ffn"][l]), d_skip=row(w["D_skip"][l]), b_glu=row(w["b_glu"][l]),
        pool_scale=row(w["pool_scale"][l]), sgu_ln_g=row(w["sgu_ln_g"][l]), sgu_ln_b=row(w["sgu_ln_b"][l]),
        a_re8=_state_slabs(ar), a_im8=_state_slabs(ai),
        bbt3=bbt3, bb3=t(bbt3), ct3=ct3, cb3=t(ct3),
        w_pool_bd=wp, w_pool_bd_t=t(wp), ws_pair=pair(ws), ws_pair_t=pair(t(ws)),
        bias_sp=jnp.repeat(t(w["b_spatial"][l]), SGU_HEAD_DIM, axis=1),
    )
    return p


MIX_WEIGHTS = ("w_in", "w_glu")
FFN_WEIGHTS = ("w_out", "w_gate", "w_up", "w_down")


def _with_big(p, mats):
    p.update(mats)


def _rows_sum(v):
    return jnp.sum(v, axis=0)


ATB_COLS = 1024
ATB_ROWS = 1024


def _after(v, token):
    return v + token[0, 0]


def _layer_bwd(dx2, sv, p, w, l, tag, hooks, after):
    t = lambda m: jnp.swapaxes(m, -1, -2)
    dx1, da, db, dc, dgt, dup, act, dg_ffn = _blk_bwd(dx2, sv["x1"], sv["gt"], sv["up"], p, tag, after)
    after = hooks["tick"]([dx1])
    after = after + hooks["on_grads"](l, "ffn", {
        "w_down": _atb(act, dx2, tag + "_wd", after), "w_gate": _atb(dgt, sv["h2"], tag + "_wg", after),
        "w_up": _atb(dup, sv["h2"], tag + "_wu", after), "w_out": _atb(sv["ycat"], dx1, tag + "_wo", after)})
    g = {}
    g["g_ffn"] = _rows_sum(dg_ffn)
    dza, dc3, dbbt3, dar8, dai8, dd8, dwglu, dbglu8 = _s5_bwd(
        da, sv["za"], sv["y"], sv["h_re"], sv["h_im"], p, tag, after)
    after = hooks["tick"]([dza])
    dx0, dz, dwp, dsc8, dws, dbias, dlng8, dlnb8, dg_mix = _mix_bwd(
        dza, db, dc, sv["pooled"], sv["zuv"], sv["x0"], dx1, p, tag, after)
    g["g_mix"] = _rows_sum(dg_mix)
    g["b_glu"] = _rows_sum(dbglu8)
    g["D_skip"] = _rows_sum(dd8)
    dc_re, dc_im = _own_blocks_split(dc3)
    g["C_re"] = dc_re
    g["C_im"] = -dc_im
    dar = jnp.sum(dar8, axis=0).reshape(N_GROUPS, SSM_STATE)
    dai = jnp.sum(dai8, axis=0).reshape(N_GROUPS, SSM_STATE)
    dbb_re, dbb_im = [t(d) for d in _own_blocks_split(dbbt3)]
    _, disc_vjp = jax.vjp(_s5_discretise, w["A_re"][l], w["A_im"][l], w["log_dt"][l], w["B_re"][l], w["B_im"][l])
    g["A_re"], g["A_im"], g["log_dt"], g["B_re"], g["B_im"] = disc_vjp((dar, dai, dbb_re, dbb_im))
    g["w_pool"] = _block_diag_extract(dwp, len(POOL_WINDOWS))
    g["pool_scale"] = _rows_sum(dsc8)
    g["sgu_ln_g"] = _rows_sum(dlng8)
    g["sgu_ln_b"] = _rows_sum(dlnb8)
    g["w_spatial"] = jnp.where(_tril()[None], dws, 0.0)
    g["b_spatial"] = t(jnp.sum(dbias.reshape(CHUNK, SGU_HEADS, SGU_HEAD_DIM), axis=-1))
    after = hooks["on_small"](l, g)
    after = hooks["on_grads"](l, "mix", {"w_in": _atb(dz, sv["h1"], tag + "_wi", after), "w_glu": dwglu})
    return dx0, after


def _local_step(x, target, w, hooks):
    params = [_layer_params(w, l) for l in range(DEPTH)]
    saved = []
    h = x
    for l in range(DEPTH):
        p, tag = params[l], f"l{l}"
        _with_big(p, hooks["get_big"](l, "mix", [h]))
        za, zuv, h1, ob, pooled, oc = _mix_fwd(h, p, tag)
        oa, y, h_re, h_im = _s5_fwd(za, p, tag)
        _with_big(p, hooks["get_big"](l, "ffn", [oa, ob, oc]))
        head = (target, w["g_final"].reshape(1, -1)) if l == DEPTH - 1 else None
        x1, x2, h2, gt, up, ycat, *loss_parts = _blk_fwd(h, oa, ob, oc, p, tag, head)
        saved.append(dict(x0=h, za=za, zuv=zuv, h1=h1, ycat=ycat, y=y, h_re=h_re, h_im=h_im, pooled=pooled, x1=x1,
                          h2=h2, gt=gt, up=up))
        h = x2
    dx = h
    loss8, dgf8 = loss_parts
    grads = [None] * DEPTH

    def on_small(l, g_l):
        grads[l] = g_l
        if l > 0:
            return []
        g = {n: jnp.stack([grads[k][n] for k in range(DEPTH)]) for n in SMALL if n != "g_final"}
        g["g_final"] = _rows_sum(dgf8)
        return hooks["on_small"](g, loss8[0, 0])

    after = []
    for l in reversed(range(DEPTH)):
        dx, after = _layer_bwd(dx, saved[l], params[l], w, l, f"l{l}", dict(hooks, on_small=on_small), after)
    return dx


_ANY = pl.BlockSpec(memory_space=pl.ANY)
_MESH = pl.DeviceIdType.MESH


def _place():
    return lax.axis_index("x"), lax.axis_index("y"), lax.axis_index("c")


def _other_chips(x, y):
    return [(1 - x, y), (x, 1 - y), (1 - x, 1 - y)]


def _dma_sems(n):
    return pltpu.SemaphoreType.DMA((n,))


def _remote(src, dst, send_sems, recv_sems, k, to):
    return pltpu.make_async_remote_copy(src_ref=src, dst_ref=dst, send_sem=send_sems.at[k], recv_sem=recv_sems.at[k],
                                        device_id=to, device_id_type=_MESH)


_HBM = pl.BlockSpec(memory_space=pltpu.HBM)
_SEM = pl.BlockSpec(memory_space=pltpu.SEMAPHORE)
_EFFECT = pltpu.SideEffectType.DATAFLOW_SIDE_EFFECTING
N_REL = N_CHIPS - 1


def _gather_plan(x, y, c, srcs, lands):
    plan = []
    for l in lands:
        r = l.shape[0] // N_CHIPS
        rows = l.at[pl.ds((2 * x + y) * r, r)]
        plan += [(rows, rows, (cx, cy, c)) for cx, cy in _other_chips(x, y)]
    return plan


def _half_rows(land, chip, c):
    h = land.shape[0] // (2 * N_CHIPS)
    return land.at[pl.ds((2 * chip + c) * h, h)]


def _gather_half_plan(x, y, c, srcs, lands):
    return [(_half_rows(l, 2 * x + y, c), _half_rows(l, 2 * x + y, c), (cx, cy, c))
            for l in lands for cx, cy in _other_chips(x, y)]


def _forward_halves(lands, tag):
    nw = len(lands)

    def body(*refs):
        ins = refs[:nw]
        send_sems, recv_sems = refs[2 * nw:]
        x, y, c = _place()
        chips = [2 * cx + cy for cx, cy in _other_chips(x, y)]
        sends = [_remote(_half_rows(ins[i], k, c), _half_rows(ins[i], k, c), send_sems, recv_sems, N_REL * i + j,
                         (x, y, 1 - c)) for i in range(nw) for j, k in enumerate(chips)]
        for cp in sends:
            cp.start()
        for i in range(nw):
            for j, k in enumerate(chips):
                sends[N_REL * i + j].wait_send()
                _remote(_half_rows(ins[i], k, c), _half_rows(ins[i], k, 1 - c), send_sems, recv_sems, N_REL * i + j,
                        (x, y, 1 - c)).wait_recv()

    return pl.pallas_call(
        body, out_shape=[jax.ShapeDtypeStruct(a.shape, a.dtype) for a in lands], in_specs=[_ANY] * nw,
        out_specs=[_ANY] * nw, input_output_aliases={i: i for i in range(nw)},
        scratch_shapes=[_dma_sems(N_REL * nw), _dma_sems(N_REL * nw)], name=f"forward_halves_{tag}",
    )(*lands)


def _sibling_plan(x, y, c, srcs, lands):
    return [(s.at[:, 1 - c], l, (x, y, 1 - c)) for s, l in zip(srcs, lands)]


def _slab_plan(x, y, c, srcs, lands):
    return [(s.at[2 * cx + cy], l.at[j], (cx, cy, c))
            for s, l in zip(srcs, lands) for j, (cx, cy) in enumerate(_other_chips(x, y))]


def _direct_plan(x, y, c, srcs, lands):
    peers = [(x, y, 1 - c)] + [(cx, cy, cc) for cx, cy in _other_chips(x, y) for cc in (c, 1 - c)]
    return [(s.at[2 * px + py, pc], l.at[j], (px, py, pc))
            for s, l in zip(srcs, lands) for j, (px, py, pc) in enumerate(peers)]


def _plan_copies(plan, srcs, lands, send_sems, recv_sems):
    x, y, c = _place()
    return [_remote(s, d, send_sems, recv_sems, k, to) for k, (s, d, to) in enumerate(plan(x, y, c, srcs, lands))]


def _hbm(a):
    return pltpu.with_memory_space_constraint(a, pltpu.HBM)


def _everyone_plan(x, y, c, srcs, lands):
    me = 4 * x + 2 * y + c
    peers = [(x, y, 1 - c)] + [(cx, cy, cc) for cx, cy in _other_chips(x, y) for cc in (c, 1 - c)]
    return [(s, l.at[me], peer) for s, l in zip(srcs, lands) for peer in peers]


def _copies_start(name, plan, srcs, lands, ncopies):
    ns, n = len(srcs), len(srcs) + len(lands)

    def body(*refs):
        for cp in _plan_copies(plan, refs[:ns], refs[ns:n], refs[n], refs[n + 1]):
            cp.start()
        refs[-1][...] = jnp.zeros_like(refs[-1])

    ref_out = [pltpu.HBM(a.shape, a.dtype) for a in (*srcs, *lands)]
    out = pl.pallas_call(
        body, name=name, in_specs=[_HBM] * n,
        out_shape=(_dma_sems(ncopies), _dma_sems(ncopies), *ref_out, jax.ShapeDtypeStruct((SUBLANES, LANES), F32)),
        out_specs=(_SEM, _SEM, *[_HBM] * n, pl.BlockSpec(memory_space=pltpu.VMEM)),
        input_output_aliases={i: 2 + i for i in range(n)},
        compiler_params=pltpu.CompilerParams(has_side_effects=_EFFECT),
    )(*[_hbm(a) for a in (*srcs, *lands)])
    return dict(name=name, plan=plan, sems=out[:2], srcs=out[2:2 + ns], lands=out[2 + ns:2 + n], token=out[-1])


def _copies_wait(started, after):
    ns = len(started["srcs"])
    n = ns + len(started["lands"])
    plan = started["plan"]

    def body(*refs):
        for cp in _plan_copies(plan, refs[:ns], refs[ns:n], refs[n], refs[n + 1]):
            cp.wait_send()
            cp.wait_recv()

    args = (*started["srcs"], *started["lands"])
    out = pl.pallas_call(
        body, name=started["name"] + "_wait", out_shape=[pltpu.HBM(a.shape, a.dtype) for a in args],
        in_specs=[_HBM] * n + [_SEM, _SEM] + [_ANY] * len(after), out_specs=[_HBM] * n,
        input_output_aliases={i: i for i in range(n)},
        compiler_params=pltpu.CompilerParams(has_side_effects=_EFFECT),
    )(*args, *started["sems"], *after)
    return out[:ns], out[ns:]


def _place_shards(ws, layer, sel, after, tag):
    nw = len(ws)

    def body(sel_ref, *refs):
        for i in range(nw):
            refs[nw + len(after) + i][...] = refs[i][...].astype(BF16)

    return pl.pallas_call(
        body, grid_spec=pltpu.PrefetchScalarGridSpec(
            num_scalar_prefetch=1, grid=(1,),
            in_specs=[pl.BlockSpec((None,) + a.shape[1:], lambda i, s: (layer, 0, 0)) for a in ws] + [_ANY] * len(after),
            out_specs=[pl.BlockSpec(a.shape[1:], lambda i, s: (s[1], 0)) for a in ws]),
        out_shape=[jax.ShapeDtypeStruct((N_CHIPS * a.shape[1], a.shape[2]), BF16) for a in ws],
        name=f"place_shards_{tag}", compiler_params=_cp(dimension_semantics=("arbitrary",)),
    )(sel, *ws, *after)


def _share_plan(layer):
    def plan(x, y, c, srcs, lands):
        def mine(f):
            h = f.shape[1] // 2
            return f.at[layer, pl.ds(c * h, h)]

        return [(mine(f), mine(f), (x, y, 1 - c)) for f in lands]

    return plan


def _add_halves(g4s, recvs, sel, tag):
    nw = len(g4s)

    def body(sel_ref, *refs):
        for i in range(nw):
            refs[2 * nw + i][...] = (refs[i][...] + refs[nw + i][...]).astype(BF16)

    mine = [pl.BlockSpec((None, None) + g.shape[2:], lambda k, s: (k, s[0], 0, 0)) for g in g4s]
    slab = [pl.BlockSpec((None,) + g.shape[2:], lambda k, s: (k, 0, 0)) for g in g4s]
    return pl.pallas_call(
        body, grid_spec=pltpu.PrefetchScalarGridSpec(num_scalar_prefetch=1, grid=(N_CHIPS,), in_specs=mine + slab,
                                                     out_specs=slab),
        out_shape=[jax.ShapeDtypeStruct(r.shape, BF16) for r in recvs], name=f"add_halves_{tag}",
        compiler_params=_cp(dimension_semantics=("arbitrary",)),
    )(sel, *g4s, *recvs)


def _to_bf16(g4s, tag):
    nw = len(g4s)

    def body(*refs):
        for i in range(nw):
            refs[nw + i][...] = refs[i][...].astype(BF16)

    specs = [pl.BlockSpec((None,) + g.shape[1:], lambda k: (k, 0, 0, 0)) for g in g4s]
    return pl.pallas_call(
        body, grid=(N_CHIPS,), in_specs=specs, out_specs=specs,
        out_shape=[jax.ShapeDtypeStruct(g.shape, BF16) for g in g4s], name=f"to_bf16_{tag}",
        compiler_params=_cp(dimension_semantics=("arbitrary",)),
    )(*g4s)


def _add_direct(g4s, slabs, fs, layer, sel, tag):
    nw = len(g4s)
    old = [f for f in fs if f is not None]

    def body(sel_ref, *refs):
        outs = refs[2 * nw + len(old):]
        for i in range(nw):
            acc = refs[i][...]
            for j in range(N_DEV - 1):
                acc = acc + refs[nw + i][j].astype(F32)
            outs[i][...] = acc

    shapes = [(DEPTH, 2 * g.shape[2], g.shape[3]) for g in g4s]
    in_specs = [pl.BlockSpec((None, None) + g.shape[2:], lambda i, s: (s[1], s[0], 0, 0)) for g in g4s]
    in_specs += [pl.BlockSpec(sl.shape, lambda i, s: (0, 0, 0), pipeline_mode=pl.Buffered(1)) for sl in slabs]
    in_specs += [_ANY] * len(old)
    first_old = 1 + 2 * nw
    aliases, k = {}, 0
    for i, f in enumerate(fs):
        if f is not None:
            aliases[first_old + k] = i
            k += 1
    return pl.pallas_call(
        body, grid_spec=pltpu.PrefetchScalarGridSpec(
            num_scalar_prefetch=1, grid=(1,), in_specs=in_specs,
            out_specs=[pl.BlockSpec((None,) + g.shape[2:], lambda i, s: (layer, s[0], 0)) for g in g4s]),
        out_shape=[jax.ShapeDtypeStruct(sh, F32) for sh in shapes], input_output_aliases=aliases,
        name=f"add_direct_{tag}", compiler_params=_cp(dimension_semantics=("arbitrary",)),
    )(sel, *g4s, *slabs, *old)


def _add_chips(ps, slabs, fs, layer, sel, tag):
    nw = len(ps)
    old = [f for f in fs if f is not None]

    def body(sel_ref, *refs):
        outs = refs[2 * nw + len(old):]
        for i in range(nw):
            acc = refs[i][...].astype(F32)
            for j in range(N_REL):
                acc = acc + refs[nw + i][j].astype(F32)
            outs[i][...] = acc

    shapes = [(DEPTH, 2 * p.shape[1], p.shape[2]) for p in ps]
    in_specs = [pl.BlockSpec((None,) + p.shape[1:], lambda i, s: (s[1], 0, 0)) for p in ps]
    in_specs += [pl.BlockSpec(sl.shape, lambda i, s: (0, 0, 0)) for sl in slabs]
    in_specs += [_ANY] * len(old)
    first_old = 1 + 2 * nw
    aliases, k = {}, 0
    for i, f in enumerate(fs):
        if f is not None:
            aliases[first_old + k] = i
            k += 1
    return pl.pallas_call(
        body, grid_spec=pltpu.PrefetchScalarGridSpec(
            num_scalar_prefetch=1, grid=(1,), in_specs=in_specs,
            out_specs=[pl.BlockSpec((None,) + p.shape[1:], lambda i, s: (layer, s[0], 0)) for p in ps]),
        out_shape=[jax.ShapeDtypeStruct(sh, F32) for sh in shapes], input_output_aliases=aliases,
        name=f"add_chips_{tag}", compiler_params=_cp(dimension_semantics=("arbitrary",)),
    )(sel, *ps, *slabs, *old)


def _adamw_math(w, g, m, v):
    m = ADAM_B1 * m + (1.0 - ADAM_B1) * g
    v = ADAM_B2 * v + (1.0 - ADAM_B2) * (g * g)
    m_hat = m / (1.0 - ADAM_B1 ** ADAM_STEP)
    v_hat = v / (1.0 - ADAM_B2 ** ADAM_STEP)
    delta = -ADAM_LR * (m_hat / (jnp.sqrt(v_hat) + ADAM_EPS) + ADAM_WD * w)
    return delta, m, v


ADAM_ROWS = 512


def _row_tile(rows, most):
    return max(t for t in range(SUBLANES, most + 1, SUBLANES) if rows % t == 0)


def _adamw(w, g, m, v, tag, after=()):
    depth, rows, cols = w.shape
    tr = _row_tile(rows, ADAM_ROWS)

    def body(w_ref, g_ref, m_ref, v_ref, *rest):
        d_ref, nm_ref, nv_ref = rest[len(after):]
        d, nm, nv = _adamw_math(w_ref[...], g_ref[...], m_ref[...], v_ref[...])
        d_ref[...] = d
        nm_ref[...] = nm
        nv_ref[...] = nv

    spec = pl.BlockSpec((None, tr, cols), lambda l, i: (l, i, 0))
    return pl.pallas_call(
        body, grid=(depth, rows // tr), in_specs=[spec] * 4 + [_ANY] * len(after), out_specs=[spec] * 3,
        out_shape=[jax.ShapeDtypeStruct(w.shape, F32)] * 3, name=f"adamw_{tag}",
        compiler_params=_cp(dimension_semantics=("arbitrary", "arbitrary")),
    )(w, g, m, v, *after)


SMALL_TILE = 512
PRECISE = ("g_final",)
COARSE = [n for n in SMALL if n not in PRECISE]


def _small_reduce(gathered):
    n = len(gathered)

    def body(*refs):
        for ga_ref, g_ref in zip(refs[:n], refs[n:]):
            g = ga_ref[0].astype(F32)
            for k in range(1, N_DEV):
                g = g + ga_ref[k].astype(F32)
            g_ref[...] = g

    return pl.pallas_call(
        body, out_shape=[jax.ShapeDtypeStruct(b.shape[1:], F32) for b in gathered], name="small_reduce",
        compiler_params=_cp(),
    )(*gathered)


def _small_adamw(g, w, m, v):
    names = list(g)
    n = len(names)
    shapes = {k: g[k].shape if g[k].ndim > 1 else (1,) + g[k].shape for k in names}

    def body(*refs):
        ins, outs = refs[:4 * n], refs[4 * n:]
        for i in range(n):
            d, nm, nv = _adamw_math(ins[n + i][...], ins[i][...], ins[2 * n + i][...], ins[3 * n + i][...])
            outs[i][...] = d
            outs[n + i][...] = nm
            outs[2 * n + i][...] = nv

    out = pl.pallas_call(
        body, out_shape=[jax.ShapeDtypeStruct(shapes[k], F32) for _ in range(3) for k in names], name="adamw_small",
        compiler_params=_cp(),
    )(*[src[k].reshape(shapes[k]) for src in (g, w, m, v) for k in names])
    return [{k: out[j * n + i].reshape(g[k].shape) for i, k in enumerate(names)} for j in range(3)]


def _exchange_form(n, a):
    return jnp.swapaxes(a, 1, 2) if n in TRANSPOSED else a


PACK_ROWS = 16


def _rows_of(size):
    return -(-size // (LANES * PACK_ROWS)) * PACK_ROWS


SMALL_VIEW = {"B_re": (0, 1, 3, 2), "B_im": (0, 1, 3, 2), "b_spatial": (1, 0, 2)}
assert all(tuple(order[i] for i in order) == tuple(range(len(order))) for order in SMALL_VIEW.values())


def _view(n, a):
    return jnp.transpose(a, SMALL_VIEW[n]) if n in SMALL_VIEW else a


def _pack(vals, names, extra=None):
    parts = [_view(n, vals[n]).reshape(-1) for n in names] + ([] if extra is None else [extra.reshape(1)])
    tiles = [jnp.pad(a, (0, _rows_of(a.size) * LANES - a.size)).reshape(-1, LANES) for a in parts]
    rows = sum(t.shape[0] for t in tiles)
    if rows > SMALL_TILE:
        tiles.append(jnp.zeros((-rows % SMALL_TILE, LANES), tiles[0].dtype))
    return jnp.concatenate(tiles, axis=0)


def _unpack(buf, like, names):
    out, row = {}, 0
    for n in names:
        rows = _rows_of(like[n].size)
        shape = tuple(like[n].shape[i] for i in SMALL_VIEW.get(n, range(like[n].ndim)))
        out[n] = buf[row:row + rows].reshape(-1)[:like[n].size].reshape(shape)
        row += rows
    return out, buf[row:]


def kernel(x, g_mix, w_in, A_re, A_im, log_dt, B_re, B_im, C_re, C_im, D_skip, w_glu, b_glu, w_pool, pool_scale, sgu_ln_g, sgu_ln_b, w_spatial, b_spatial, w_out, g_ffn, w_gate, w_up, w_down, g_final, loss_target, m_g_mix, m_w_in, m_A_re, m_A_im, m_log_dt, m_B_re, m_B_im, m_C_re, m_C_im, m_D_skip, m_w_glu, m_b_glu, m_w_pool, m_pool_scale, m_sgu_ln_g, m_sgu_ln_b, m_w_spatial, m_b_spatial, m_w_out, m_g_ffn, m_w_gate, m_w_up, m_w_down, m_g_final, v_g_mix, v_w_in, v_A_re, v_A_im, v_log_dt, v_B_re, v_B_im, v_C_re, v_C_im, v_D_skip, v_w_glu, v_b_glu, v_w_pool, v_pool_scale, v_sgu_ln_g, v_sgu_ln_b, v_w_spatial, v_b_spatial, v_w_out, v_g_ffn, v_w_gate, v_w_up, v_w_down, v_g_final):
    loc = locals()
    w = {n: loc[n] for n in WEIGHTS}
    m = {n: loc["m_" + n] for n in WEIGHTS}
    v = {n: loc["v_" + n] for n in WEIGHTS}
    sel = jnp.stack([lax.axis_index("c"), 2 * lax.axis_index("x") + lax.axis_index("y")]).astype(jnp.int32)

    chip = sel[1]

    halves = [(l, half) for l in range(DEPTH) for half in ("mix", "ffn")]
    two_level = {(0, "ffn")}
    names = {"mix": MIX_WEIGHTS, "ffn": FFN_WEIGHTS}
    started = {}
    wx = {n: _exchange_form(n, w[n]) for n in BIG}
    chain = []
    for l, half in halves:
        lands = _place_shards([wx[n] for n in names[half]], l, sel, chain, f"l{l}_{half}")
        plan = _gather_half_plan if (l, half) in two_level else _gather_plan
        started[l, half] = _copies_start(f"weights_l{l}_{half}", plan, [], lands, N_REL * len(lands))
        chain = [started[l, half]["token"]]
    w = dict(w, g_mix=_after(w["g_mix"], started[halves[-1]]["token"]))

    def get_big(l, half, after):
        lands = _copies_wait(started[l, half], after)[1]
        if (l, half) in two_level:
            lands = _forward_halves(lands, f"l{l}_{half}")
        return dict(zip(names[half], lands))

    result = {n: None for n in BIG}
    stage = {"slabs": None, "share": None}

    def advance(after):
        tokens = []
        if stage["share"] is not None:
            sh, ns = stage["share"]
            for n, f in zip(ns, _copies_wait(sh, after)[1]):
                result[n] = f
            stage["share"] = None
        if stage["slabs"] is not None:
            ex, g4s, ns, l, tag = stage["slabs"]
            _, slabs = _copies_wait(ex, after)
            bufs = _add_direct(g4s, slabs, [result[n] for n in ns], l, sel, tag)
            sh = _copies_start(f"share_{tag}", _share_plan(l), [], bufs, len(bufs))
            stage["share"], stage["slabs"] = (sh, ns), None
            tokens.append(sh["token"])
        return tokens

    def on_grads(l, half, grads):
        ns = list(grads)
        tag = f"l{l}_{half}"
        tokens = advance([grads[ns[0]]])
        g4s = [grads[n].reshape(N_CHIPS, 2, grads[n].shape[0] // (2 * N_CHIPS), grads[n].shape[1]) for n in ns]
        slabs = [lax.empty((N_DEV - 1,) + g4.shape[2:], BF16) for g4 in g4s]
        ex = _copies_start(f"grads_{tag}", _direct_plan, _to_bf16(g4s, tag), slabs, (N_DEV - 1) * len(g4s))
        stage["slabs"] = (ex, g4s, ns, l, tag)
        return tokens + [ex["token"]]

    small = {}

    def on_small(g, loss_local):
        me = 2 * chip + sel[0]
        blocks = [_pack(g, COARSE).astype(BF16), _pack(g, PRECISE, loss_local)]
        lands = [lax.dynamic_update_slice(lax.empty((N_DEV,) + b.shape, b.dtype), b[None], (me, 0, 0)) for b in blocks]
        small.update(_copies_start("small_grads", _everyone_plan, blocks, lands, (N_DEV - 1) * len(blocks)))
        return [small["token"]] + advance([small["token"]])

    dx = _local_step(x[0], loss_target[0], w, dict(get_big=get_big, on_grads=on_grads, tick=advance, on_small=on_small))
    grads, deltas, new_m, new_v = {}, {}, {}, {}

    def update_big(ns, after):
        for n in ns:
            outs = _adamw(wx[n], result[n], _exchange_form(n, m[n]), _exchange_form(n, v[n]), n, after)
            grads[n], deltas[n], new_m[n], new_v[n] = [_exchange_form(n, a) for a in (result[n], *outs)]
            after = [outs[-1]]
        return after

    shared = advance(update_big(FFN_WEIGHTS, [stage["slabs"][0]["token"]]))
    _, gathered = _copies_wait(small, shared)
    coarse, precise = _small_reduce(gathered)
    small_g, _ = _unpack(coarse, w, COARSE)
    precise_g, rest = _unpack(precise, w, PRECISE)
    small_g.update(precise_g)
    loss = rest[0, 0]
    views = [{n: _view(n, src[n]) for n in small_g} for src in (w, m, v)]
    for store, vals in zip((grads, deltas, new_m, new_v), (small_g, *_small_adamw(small_g, *views))):
        store.update({n: _view(n, a) for n, a in vals.items()})
    advance([new_v[SMALL[0]]])
    update_big(MIX_WEIGHTS, [])
    return (loss, dx[None], *[grads[n] for n in WEIGHTS], *[deltas[n] for n in WEIGHTS],
            *[new_m[n] for n in WEIGHTS], *[new_v[n] for n in WEIGHTS])
```
